```python
import math
import jax, jax.numpy as jnp
from jax import lax
import numpy as np

D_MODEL = 1024
BATCH = 8
SEQ = 16384
DEPTH = 1

MIX_WIDTH = D_MODEL
CONV_WIDTH = MIX_WIDTH // 2
ATTN_HEADS = 8
HEAD_DIM = (MIX_WIDTH - CONV_WIDTH) // ATTN_HEADS
ATTN_WIDTH = ATTN_HEADS * HEAD_DIM
CONV_K = 3
DILATED_CONFIGS = ((128, 1), (512, 4), (2048, 16))
Q_BLOCK = 128
NUM_BUCKETS = 32
MAX_DISTANCE = 1024
EPS = 1e-6
NEG = -1e30

kernel_name = "hybrid_shortconv_dilated_attn_block"


def rmsnorm(x, w):
    xf = x.astype(jnp.float32)
    y = xf * lax.rsqrt(jnp.mean(xf * xf, axis=-1, keepdims=True) + EPS)
    return (y * w.astype(jnp.float32)).astype(x.dtype)


def t5_bucket(rel):
    half_b = NUM_BUCKETS // 2
    max_exact = half_b // 2
    ret = jnp.where(rel > 0, half_b, 0)
    n = jnp.abs(rel)
    nf = jnp.maximum(n, 1).astype(jnp.float32)
    large = max_exact + (jnp.log(nf / max_exact) / math.log(MAX_DISTANCE / max_exact)
                         * (half_b - max_exact)).astype(jnp.int32)
    large = jnp.minimum(large, half_b - 1)
    return ret + jnp.where(n < max_exact, n, large)


def dilated_window_attention(q, k, v, rel_bias, window, dilation):
    b, s, nh, dh = q.shape
    half = window // (2 * dilation)
    length = s // dilation
    n_blk = -(-length // Q_BLOCK)
    padded = n_blk * Q_BLOCK
    kb_len = Q_BLOCK + 2 * half
    bd = b * dilation

    def to_residue(t):
        t = t.reshape(b, length, dilation, nh, dh)
        return t.transpose(0, 2, 3, 1, 4).reshape(bd, nh, length, dh)

    qr = jnp.pad(to_residue(q), ((0, 0), (0, 0), (0, padded - length), (0, 0)))
    qr = qr.reshape(bd, nh, n_blk, Q_BLOCK, dh)
    pad_kv = ((0, 0), (0, 0), (half, padded - length + half), (0, 0))
    kr = jnp.pad(to_residue(k), pad_kv)
    vr = jnp.pad(to_residue(v), pad_kv)

    key_idx = jnp.arange(n_blk)[:, None] * Q_BLOCK + jnp.arange(kb_len)[None, :]
    kblk = kr[:, :, key_idx]
    vblk = vr[:, :, key_idx]

    logits = jnp.einsum('bhnqd,bhnkd->bhnqk', qr, kblk,
                        preferred_element_type=jnp.float32) * (dh ** -0.5)

    rel = jnp.arange(kb_len)[None, :] - half - jnp.arange(Q_BLOCK)[:, None]
    band = jnp.abs(rel) <= half
    buckets = t5_bucket(jnp.clip(rel, -half, half) * dilation)
    bias = rel_bias[buckets].astype(jnp.float32).transpose(2, 0, 1)
    key_pos = key_idx - half
    key_ok = (key_pos >= 0) & (key_pos < length)
    mask = band[None, :, :] & key_ok[:, None, :]
    logits = jnp.where(mask[None, None], logits + bias[None, :, None], NEG)

    m = jnp.max(logits, axis=-1, keepdims=True)
    p = jnp.exp(logits - m)
    denom = jnp.sum(p, axis=-1, keepdims=True)
    o = jnp.einsum('bhnqk,bhnkd->bhnqd', p, vblk.astype(jnp.float32)) / denom
    lse = (m + jnp.log(denom))[..., 0]

    o = o.reshape(bd, nh, padded, dh)[:, :, :length]
    o = o.reshape(b, dilation, nh, length, dh).transpose(0, 3, 1, 2, 4).reshape(b, s, nh, dh)
    lse = lse.reshape(bd, nh, padded)[:, :, :length]
    lse = lse.reshape(b, dilation, nh, length).transpose(0, 3, 1, 2).reshape(b, s, nh)
    return o, lse


def short_gated_conv(u, gate_b, gate_c, conv_w, conv_b):
    pad = (CONV_K - 1) // 2
    z = lax.conv_general_dilated(gate_c * u, conv_w[:, None, :].astype(u.dtype),
                                 window_strides=(1,), padding=((pad, pad),),
                                 dimension_numbers=('NWC', 'WIO', 'NWC'),
                                 feature_group_count=CONV_WIDTH)
    return gate_b * (z + conv_b.astype(u.dtype))


def _fwd_setup_inputs(seed: int = 0) -> dict:
    key = jax.random.key(seed)
    ks = jax.random.split(key, 9)
    proj_cols = 4 * CONV_WIDTH + 4 * ATTN_WIDTH
    x = jax.random.normal(ks[0], (BATCH, SEQ, D_MODEL), jnp.float32)
    norm_w = 1.0 + 0.05 * jax.random.normal(ks[1], (D_MODEL,), jnp.float32)
    w_in = jax.random.normal(ks[2], (D_MODEL, proj_cols), jnp.float32) * D_MODEL ** -0.5
    conv_w = jax.random.normal(ks[3], (CONV_K, CONV_WIDTH), jnp.float32) * CONV_K ** -0.5
    conv_b = 0.01 * jax.random.normal(ks[4], (CONV_WIDTH,), jnp.float32)
    q_norm_w = 1.0 + 0.05 * jax.random.normal(ks[5], (HEAD_DIM,), jnp.float32)
    k_norm_w = 1.0 + 0.05 * jax.random.normal(ks[6], (HEAD_DIM,), jnp.float32)
    rel_bias = 0.5 * jax.random.normal(ks[7], (NUM_BUCKETS, ATTN_HEADS), jnp.float32)
    w_out = jax.random.normal(ks[8], (MIX_WIDTH, D_MODEL), jnp.float32) * MIX_WIDTH ** -0.5
    return {"x": x, "norm_w": norm_w, "w_in": w_in, "conv_w": conv_w, "conv_b": conv_b,
            "q_norm_w": q_norm_w, "k_norm_w": k_norm_w, "rel_bias": rel_bias, "w_out": w_out}


def _fwd_reference(x, norm_w, w_in, conv_w, conv_b, q_norm_w, k_norm_w, rel_bias, w_out):
    b, s, _ = x.shape
    for _layer in range(DEPTH):
        h = rmsnorm(x, norm_w)
        proj = jnp.einsum('bsd,de->bse', h, w_in)
        c, a = CONV_WIDTH, ATTN_WIDTH
        splits = [c, 2 * c, 3 * c, 4 * c, 4 * c + a, 4 * c + 2 * a, 4 * c + 3 * a]
        u, g_b, g_c, z_conv, q, k, v, z_attn = jnp.split(proj, splits, axis=-1)

        y_conv = short_gated_conv(u, g_b, g_c, conv_w, conv_b) * jax.nn.silu(z_conv)

        q = rmsnorm(q.reshape(b, s, ATTN_HEADS, HEAD_DIM), q_norm_w)
        k = rmsnorm(k.reshape(b, s, ATTN_HEADS, HEAD_DIM), k_norm_w)
        v = v.reshape(b, s, ATTN_HEADS, HEAD_DIM)
        outs, lses = [], []
        for window, dilation in DILATED_CONFIGS:
            o_i, lse_i = dilated_window_attention(q, k, v, rel_bias, window, dilation)
            outs.append(o_i)
            lses.append(lse_i)
        mix = jax.nn.softmax(jnp.stack(lses, axis=0), axis=0)
        o = jnp.einsum('gbsh,gbshd->bshd', mix, jnp.stack(outs, axis=0))
        y_attn = o.reshape(b, s, ATTN_WIDTH).astype(x.dtype) * jax.nn.silu(z_attn)

        y = jnp.concatenate([y_conv, y_attn], axis=-1)
        x = x + jnp.einsum('bse,ed->bsd', y, w_out)
    return x


import jax as _jax
import jax.numpy as _jnp

TWIN_FORMAT = 'train_step'
FWD_PARAMS = ['x', 'norm_w', 'w_in', 'conv_w', 'conv_b', 'q_norm_w', 'k_norm_w', 'rel_bias', 'w_out']
TWIN_WEIGHTS = ['norm_w', 'w_in', 'conv_w', 'conv_b', 'q_norm_w', 'k_norm_w', 'rel_bias', 'w_out']
TWIN_DIFF_INPUT = 'x'
TWIN_INPUTS = ['x', 'norm_w', 'w_in', 'conv_w', 'conv_b', 'q_norm_w', 'k_norm_w', 'rel_bias', 'w_out', 'loss_target', 'm_norm_w', 'm_w_in', 'm_conv_w', 'm_conv_b', 'm_q_norm_w', 'm_k_norm_w', 'm_rel_bias', 'm_w_out', 'v_norm_w', 'v_w_in', 'v_conv_w', 'v_conv_b', 'v_q_norm_w', 'v_k_norm_w', 'v_rel_bias', 'v_w_out']
TWIN_OUTPUTS = ['loss', 'grad_x', 'grad_norm_w', 'grad_w_in', 'grad_conv_w', 'grad_conv_b', 'grad_q_norm_w', 'grad_k_norm_w', 'grad_rel_bias', 'grad_w_out', 'delta_norm_w', 'delta_w_in', 'delta_conv_w', 'delta_conv_b', 'delta_q_norm_w', 'delta_k_norm_w', 'delta_rel_bias', 'delta_w_out', 'new_m_norm_w', 'new_m_w_in', 'new_m_conv_w', 'new_m_conv_b', 'new_m_q_norm_w', 'new_m_k_norm_w', 'new_m_rel_bias', 'new_m_w_out', 'new_v_norm_w', 'new_v_w_in', 'new_v_conv_w', 'new_v_conv_b', 'new_v_q_norm_w', 'new_v_k_norm_w', 'new_v_rel_bias', 'new_v_w_out']
TWIN_LEAF_KINDS = {'loss': 'loss', 'grad_x': 'grad_x', 'grad_norm_w': 'grad_w', 'grad_w_in': 'grad_w', 'grad_conv_w': 'grad_w', 'grad_conv_b': 'grad_w', 'grad_q_norm_w': 'grad_w', 'grad_k_norm_w': 'grad_w', 'grad_rel_bias': 'grad_w', 'grad_w_out': 'grad_w', 'delta_norm_w': 'delta_w', 'delta_w_in': 'delta_w', 'delta_conv_w': 'delta_w', 'delta_conv_b': 'delta_w', 'delta_q_norm_w': 'delta_w', 'delta_k_norm_w': 'delta_w', 'delta_rel_bias': 'delta_w', 'delta_w_out': 'delta_w', 'new_m_norm_w': 'new_m', 'new_m_w_in': 'new_m', 'new_m_conv_w': 'new_m', 'new_m_conv_b': 'new_m', 'new_m_q_norm_w': 'new_m', 'new_m_k_norm_w': 'new_m', 'new_m_rel_bias': 'new_m', 'new_m_w_out': 'new_m', 'new_v_norm_w': 'new_v', 'new_v_w_in': 'new_v', 'new_v_conv_w': 'new_v', 'new_v_conv_b': 'new_v', 'new_v_q_norm_w': 'new_v', 'new_v_k_norm_w': 'new_v', 'new_v_rel_bias': 'new_v', 'new_v_w_out': 'new_v'}


def _forward(args):
    return _fwd_reference(*[args[k] for k in FWD_PARAMS])


def _output_shape():
    def fwd():
        inp = _fwd_setup_inputs(0)
        return _fwd_reference(*[inp[k] for k in FWD_PARAMS])
    out = _jax.eval_shape(fwd)
    return out.shape, out.dtype

N_MICROBATCH = 1
ADAM_LR = 0.001
ADAM_B1 = 0.9
ADAM_B2 = 0.999
ADAM_EPS = 1e-08
ADAM_WD = 0.01
ADAM_STEP = 10
PER_EXAMPLE_BATCH_AXIS = {'x': 0, 'loss_target': 0}
SHARED_INPUTS = []
_WEIGHT_DTYPES = {'norm_w': _jnp.float32, 'w_in': _jnp.float32, 'conv_w': _jnp.float32, 'conv_b': _jnp.float32, 'q_norm_w': _jnp.float32, 'k_norm_w': _jnp.float32, 'rel_bias': _jnp.float32, 'w_out': _jnp.float32}
MOMENT_SCALE = {'norm_w': 8.809127e+01, 'w_in': 9.574845e-01, 'conv_w': 2.367651e+01, 'conv_b': 1.166005e+00, 'q_norm_w': 2.186652e+00, 'k_norm_w': 2.201322e+00, 'rel_bias': 2.740162e-01, 'w_out': 7.359634e-01}


def _to_microbatches(a, axis):
    t = _jnp.moveaxis(a, axis, 0)
    t = t.reshape((N_MICROBATCH, t.shape[0] // N_MICROBATCH) + t.shape[1:])
    return _jnp.moveaxis(t, 1, axis + 1)


def setup_inputs(seed: int = 0) -> dict:
    inp = _fwd_setup_inputs(seed)
    key = _jax.random.fold_in(_jax.random.key(seed), 7919)
    shape, _ = _output_shape()
    out = dict(inp)
    out["loss_target"] = _jax.random.normal(_jax.random.fold_in(key, 0), shape, _jnp.float32)
    for i, name in enumerate(TWIN_WEIGHTS):
        w = inp[name].astype(_jnp.float32)
        if MOMENT_SCALE is None:
            s = _jnp.sqrt(_jnp.mean(_jnp.square(w)) + 1e-30)
        else:
            s = MOMENT_SCALE[name]
        km, kv = _jax.random.split(_jax.random.fold_in(key, i + 1))
        out[name] = w
        out["m_" + name] = s * _jax.random.normal(km, w.shape, _jnp.float32)
        out["v_" + name] = (s * s) * _jax.random.uniform(kv, w.shape, _jnp.float32, 0.5, 1.5)
    if N_MICROBATCH > 1:
        for name, axis in PER_EXAMPLE_BATCH_AXIS.items():
            out[name] = _to_microbatches(out[name], axis)
    return {'x': out['x'], 'norm_w': out['norm_w'], 'w_in': out['w_in'], 'conv_w': out['conv_w'], 'conv_b': out['conv_b'], 'q_norm_w': out['q_norm_w'], 'k_norm_w': out['k_norm_w'], 'rel_bias': out['rel_bias'], 'w_out': out['w_out'], 'loss_target': out['loss_target'], 'm_norm_w': out['m_norm_w'], 'm_w_in': out['m_w_in'], 'm_conv_w': out['m_conv_w'], 'm_conv_b': out['m_conv_b'], 'm_q_norm_w': out['m_q_norm_w'], 'm_k_norm_w': out['m_k_norm_w'], 'm_rel_bias': out['m_rel_bias'], 'm_w_out': out['m_w_out'], 'v_norm_w': out['v_norm_w'], 'v_w_in': out['v_w_in'], 'v_conv_w': out['v_conv_w'], 'v_conv_b': out['v_conv_b'], 'v_q_norm_w': out['v_q_norm_w'], 'v_k_norm_w': out['v_k_norm_w'], 'v_rel_bias': out['v_rel_bias'], 'v_w_out': out['v_w_out']}


def _loss(weights, diff, rest, loss_target):
    with _jax.named_scope("forward"):
        args = {**rest, TWIN_DIFF_INPUT: diff, **{k: w.astype(_WEIGHT_DTYPES[k]) for k, w in weights.items()}}
        y = _forward(args)
    with _jax.named_scope("loss_head"):
        err = _jnp.square(y.astype(_jnp.float32) - loss_target)
        return 0.5 * _jnp.sum(_jnp.mean(err, axis=-1)) if err.ndim else 0.5 * err


def _adamw(w, g, m, v):
    m = ADAM_B1 * m + (1.0 - ADAM_B1) * g
    v = ADAM_B2 * v + (1.0 - ADAM_B2) * _jnp.square(g)
    m_hat = m / (1.0 - ADAM_B1 ** ADAM_STEP)
    v_hat = v / (1.0 - ADAM_B2 ** ADAM_STEP)
    delta = -ADAM_LR * (m_hat / (_jnp.sqrt(v_hat) + ADAM_EPS) + ADAM_WD * w)
    return delta, m, v


def reference(x, norm_w, w_in, conv_w, conv_b, q_norm_w, k_norm_w, rel_bias, w_out, loss_target, m_norm_w, m_w_in, m_conv_w, m_conv_b, m_q_norm_w, m_k_norm_w, m_rel_bias, m_w_out, v_norm_w, v_w_in, v_conv_w, v_conv_b, v_q_norm_w, v_k_norm_w, v_rel_bias, v_w_out):
    given = dict(x=x, norm_w=norm_w, w_in=w_in, conv_w=conv_w, conv_b=conv_b, q_norm_w=q_norm_w, k_norm_w=k_norm_w, rel_bias=rel_bias, w_out=w_out, loss_target=loss_target, m_norm_w=m_norm_w, m_w_in=m_w_in, m_conv_w=m_conv_w, m_conv_b=m_conv_b, m_q_norm_w=m_q_norm_w, m_k_norm_w=m_k_norm_w, m_rel_bias=m_rel_bias, m_w_out=m_w_out, v_norm_w=v_norm_w, v_w_in=v_w_in, v_conv_w=v_conv_w, v_conv_b=v_conv_b, v_q_norm_w=v_q_norm_w, v_k_norm_w=v_k_norm_w, v_rel_bias=v_rel_bias, v_w_out=v_w_out)
    weights = {n: given[n] for n in TWIN_WEIGHTS}
    shared = {n: given[n] for n in SHARED_INPUTS}
    per_example = {n: given[n] for n in ['x']}
    grad_fn = _jax.value_and_grad(_loss, argnums=(0, 1))

    def one_microbatch(ex, loss_target):
        ex = dict(ex)
        diff = ex.pop(TWIN_DIFF_INPUT)
        return grad_fn(weights, diff, {**shared, **ex}, loss_target)

    if N_MICROBATCH == 1:
        loss, (grad_w, grad_x) = one_microbatch(per_example, given["loss_target"])
    else:
        def body(carry, xs):
            loss_sum, grad_sum = carry
            l_k, (gw_k, gx_k) = one_microbatch(xs[0], xs[1])
            with _jax.named_scope("update"):
                return (loss_sum + l_k, _jax.tree.map(_jnp.add, grad_sum, gw_k)), gx_k

        init = (_jnp.zeros((), _jnp.float32), _jax.tree.map(_jnp.zeros_like, weights))
        (loss, grad_w), grad_x = _jax.lax.scan(body, init, (per_example, given["loss_target"]))
    with _jax.named_scope("update"):
        delta_w, new_m, new_v = {}, {}, {}
        for n in TWIN_WEIGHTS:
            delta_w[n], new_m[n], new_v[n] = _adamw(weights[n], grad_w[n], given["m_" + n], given["v_" + n])
    return (loss, grad_x, *[grad_w[n] for n in TWIN_WEIGHTS], *[delta_w[n] for n in TWIN_WEIGHTS],
            *[new_m[n] for n in TWIN_WEIGHTS], *[new_v[n] for n in TWIN_WEIGHTS])
```

```python
import math

import jax
import jax.numpy as jnp
from jax import lax
from jax.experimental import pallas as pl
from jax.experimental.pallas import tpu as pltpu

F32 = jnp.float32
BF16 = jnp.bfloat16
MESH = pl.DeviceIdType.MESH

D_MODEL = 1024
CONV_W = 512
ATTN_W = 512
HEAD_DIM = 64
N_PAIR = 4
DILATIONS = (1, 4, 16)
HALF = 64
QB = 128
KB = QB + 2 * HALF
NUM_BUCKETS = 32
MAX_DISTANCE = 1024
EPS = 1e-6
NEG = -1e30
ADAM_LR, ADAM_B1, ADAM_B2, ADAM_EPS, ADAM_WD, ADAM_STEP = 0.001, 0.9, 0.999, 1e-08, 0.01, 10
VMEM_LIMIT = 48 << 20


def _params(sem=None, vmem=VMEM_LIMIT, **kw):
    if sem is not None:
        kw["dimension_semantics"] = sem
    return pltpu.CompilerParams(vmem_limit_bytes=vmem, **kw)


def _sigmoid(z):
    return 1.0 / (1.0 + jnp.exp(-z))


def _group_sum(val, b_ref):
    hi = val.astype(BF16)
    lo = (val - hi.astype(F32)).astype(BF16)
    outs = []
    for j in range(val.shape[1] // 256):
        sl = slice(256 * j, 256 * j + 256)
        outs.append(jnp.dot(hi[:, sl], b_ref[...], preferred_element_type=F32)
                    + jnp.dot(lo[:, sl], b_ref[...], preferred_element_type=F32))
    return outs[0] if len(outs) == 1 else jnp.concatenate(outs, axis=1)


def _t5_bucket(rel):
    half_b = NUM_BUCKETS // 2
    max_exact = half_b // 2
    ret = jnp.where(rel > 0, half_b, 0)
    n = jnp.abs(rel)
    nf = jnp.maximum(n, 1).astype(F32)
    large = max_exact + (jnp.log(nf / max_exact) / math.log(MAX_DISTANCE / max_exact)
                         * (half_b - max_exact)).astype(jnp.int32)
    large = jnp.minimum(large, half_b - 1)
    return ret + jnp.where(n < max_exact, n, large)


def _window_rel(variant):
    off = (0, HALF, 2 * HALF)[variant]
    return jnp.arange(KB)[None, :] - off - jnp.arange(QB)[:, None]


def _bias_tables(rel_bias, dilation):
    out = []
    for variant in range(3):
        rel = _window_rel(variant)
        band = jnp.abs(rel) <= HALF
        bkt = _t5_bucket(jnp.clip(rel, -HALF, HALF) * dilation)
        bias = rel_bias[bkt].astype(F32).transpose(2, 0, 1)
        bias = jnp.where(band[None], bias, NEG)
        out.append(bias.reshape(N_PAIR, 2 * QB, KB))
    return jnp.stack(out, axis=0)


def _diag_bucket_onehot(dilation):
    out = []
    c = jnp.arange(KB)
    for variant in range(3):
        off = (0, HALF, 2 * HALF)[variant]
        rel = ((c - off + 128) % 256) - 128
        band = jnp.abs(rel) <= HALF
        bkt = _t5_bucket(jnp.clip(rel, -HALF, HALF) * dilation)
        oh = (bkt[:, None] == jnp.arange(128)[None, :]) & band[:, None]
        out.append(oh.astype(F32))
    return jnp.stack(out, axis=0)


def _to_layout(a, d):
    s, c = a.shape
    if d == 1:
        return a[None]
    return a.reshape(s // d, d, c).transpose(1, 0, 2)


def _from_layout(a):
    d, l, c = a.shape
    if d == 1:
        return a[0]
    return a.transpose(1, 0, 2).reshape(d * l, c)


def _wgather(w_in, w_out, conv_w):
    rin, rout = w_in.shape[0] // 2, w_out.shape[0] // 2

    def body(win_ref, wout_ref, cw_ref, win_o, wout_o, cw_o, send_sems, recv_sems):
        x, y, c = lax.axis_index("x"), lax.axis_index("y"), lax.axis_index("c")
        b = 2 * x + y
        win_o[b] = win_ref[...].astype(BF16)
        wout_o[b] = wout_ref[...].astype(BF16)
        cw_o[b] = cw_ref[...]

        def peer(k):
            return (x ^ (k >> 1), y ^ (k & 1))

        def piece(ref, blk, half_rows, core):
            return ref.at[blk, pl.ds(core * half_rows, half_rows), :]

        def copy(sem, src, dst, to):
            return pltpu.make_async_remote_copy(src_ref=src, dst_ref=dst, send_sem=send_sems.at[sem],
                                                recv_sem=recv_sems.at[sem], device_id=to, device_id_type=MESH)

        sends = []
        for k in (1, 2, 3):
            px, py = peer(k)
            sends.append(copy(k - 1, piece(win_o, b, rin, c), piece(win_o, b, rin, c), (px, py, c)))
            sends.append(copy(3 + k - 1, piece(wout_o, b, rout, c), piece(wout_o, b, rout, c), (px, py, c)))
            sends.append(copy(6 + k - 1, cw_o.at[b], cw_o.at[b], (px, py, c)))
        for cp in sends:
            cp.start()
        fwd = []
        for k in (1, 2, 3):
            px, py = peer(k)
            bk = 2 * px + py
            copy(k - 1, piece(win_o, bk, rin, c), piece(win_o, bk, rin, c), (px, py, c)).wait_recv()
            f = copy(9 + k - 1, piece(win_o, bk, rin, c), piece(win_o, bk, rin, c), (x, y, 1 - c))
            f.start()
            fwd.append(f)
            copy(3 + k - 1, piece(wout_o, bk, rout, c), piece(wout_o, bk, rout, c), (px, py, c)).wait_recv()
            f = copy(12 + k - 1, piece(wout_o, bk, rout, c), piece(wout_o, bk, rout, c), (x, y, 1 - c))
            f.start()
            fwd.append(f)
            copy(6 + k - 1, cw_o.at[bk], cw_o.at[bk], (px, py, c)).wait_recv()
        for k in (1, 2, 3):
            px, py = peer(k)
            bk = 2 * px + py
            copy(9 + k - 1, piece(win_o, bk, rin, 1 - c), piece(win_o, bk, rin, 1 - c), (x, y, 1 - c)).wait_recv()
            copy(12 + k - 1, piece(wout_o, bk, rout, 1 - c), piece(wout_o, bk, rout, 1 - c), (x, y, 1 - c)).wait_recv()
        for cp in sends + fwd:
            cp.wait_send()

    vm = pl.BlockSpec(memory_space=pltpu.VMEM)
    return pl.pallas_call(
        body, name="wgather",
        out_shape=(jax.ShapeDtypeStruct((4,) + w_in.shape, BF16),
                   jax.ShapeDtypeStruct((4,) + w_out.shape, BF16),
                   jax.ShapeDtypeStruct((4,) + conv_w.shape, F32)),
        in_specs=[vm, vm, vm], out_specs=(vm, vm, vm),
        scratch_shapes=[pltpu.SemaphoreType.DMA((15,)), pltpu.SemaphoreType.DMA((15,))],
        compiler_params=_params(),
    )(w_in, w_out, conv_w)


def _proj(x, norm_w, wblk, qkw, b256):
    s = x.shape[0]
    tm = 256

    def body(x_ref, nw_ref, w_ref, qkw_ref, b_ref, h_o, cg_o, qkr_o, qkn_o, vz_o):
        xf = x_ref[...]
        r = lax.rsqrt(jnp.mean(xf * xf, axis=-1, keepdims=True) + EPS)
        h = (xf * r * nw_ref[...]).astype(BF16)
        h_o[...] = h
        cg_o[:, 0:1024] = jnp.dot(h, w_ref[0], preferred_element_type=F32).astype(BF16)
        cg_o[:, 1024:2048] = jnp.dot(h, w_ref[1], preferred_element_type=F32).astype(BF16)
        p2 = jnp.dot(h, w_ref[2], preferred_element_type=F32)
        qkr_o[...] = p2.astype(BF16)
        ss = _group_sum(p2 * p2, b_ref)
        rr = lax.rsqrt(ss * (1.0 / HEAD_DIM) + EPS)
        qkn_o[...] = (p2 * rr * qkw_ref[...]).astype(BF16)
        vz_o[...] = jnp.dot(h, w_ref[3], preferred_element_type=F32).astype(BF16)

    row = lambda w: pl.BlockSpec((tm, w), lambda i: (i, 0))
    full = lambda shp: pl.BlockSpec(shp, lambda i: (0,) * len(shp))
    return pl.pallas_call(
        body, name="proj", grid=(s // tm,),
        out_shape=(jax.ShapeDtypeStruct((s, 1024), BF16), jax.ShapeDtypeStruct((s, 2048), BF16),
                   jax.ShapeDtypeStruct((s, 1024), BF16), jax.ShapeDtypeStruct((s, 1024), BF16),
                   jax.ShapeDtypeStruct((s, 1024), BF16)),
        in_specs=[row(1024), full((1, 1024)), full((4, 1024, 1024)), full((1, 1024)), full((256, 256))],
        out_specs=(row(1024), row(2048), row(1024), row(1024), row(1024)),
        compiler_params=_params(("parallel",)),
    )(x, norm_w, wblk, qkw, b256)


def _block_coords(t, i, nsub, nb, length):
    n = t * nsub + i
    q0 = pl.multiple_of(i * QB, QB)
    start = pl.multiple_of(jnp.clip(n * QB - HALF, 0, length - KB), HALF)
    variant = jnp.where(n == 0, 0, jnp.where(n == nb - 1, 2, 1))
    return q0, start, variant


def _split_heads(a, lo):
    zero = jnp.zeros_like(a)
    return jnp.concatenate([jnp.where(lo, a, zero), jnp.where(lo, zero, a)], axis=0)


def _col_pair(ref, q0):
    return jnp.concatenate([ref[pl.ds(q0, QB), 0:1], ref[pl.ds(q0, QB), 64:65]], axis=0)


def _attn_fwd(qkn_l, v_l, bias, name):
    r_cls, length, _ = qkn_l.shape
    qt = min(length, 1024)
    nb, nsub = length // QB, qt // QB

    def body(q_ref, k_ref, v_ref, b_ref, o_ref, lse_ref):
        t = pl.program_id(2)
        lo = lax.broadcasted_iota(jnp.int32, (QB, 128), 1) < HEAD_DIM

        def sub(i, carry):
            q0, start, variant = _block_coords(t, i, nsub, nb, length)
            qq = _split_heads(q_ref[pl.ds(q0, QB), :], lo)
            k = k_ref[pl.ds(start, KB), :]
            v = v_ref[pl.ds(start, KB), :]
            lg = lax.dot_general(qq, k, (((1,), (1,)), ((), ())), preferred_element_type=F32) + b_ref[variant]
            m = jnp.max(lg, axis=-1, keepdims=True)
            p = jnp.exp(lg - m)
            l = jnp.sum(p, axis=-1, keepdims=True)
            pv = jnp.dot(p.astype(BF16), v, preferred_element_type=F32) / l
            lse = m + jnp.log(l)
            o_ref[pl.ds(q0, QB), :] = jnp.where(lo, pv[0:QB], pv[QB:2 * QB]).astype(BF16)
            lse_ref[pl.ds(q0, QB), :] = jnp.where(lo, jnp.broadcast_to(lse[0:QB], (QB, 128)),
                                                  jnp.broadcast_to(lse[QB:2 * QB], (QB, 128)))
            return carry

        lax.fori_loop(0, nsub, sub, 0)

    return pl.pallas_call(
        body, name=name, grid=(r_cls, N_PAIR, length // qt),
        out_shape=(jax.ShapeDtypeStruct((r_cls, length, 512), BF16),
                   jax.ShapeDtypeStruct((r_cls, length, 512), F32)),
        in_specs=[pl.BlockSpec((None, qt, 128), lambda r, p, t: (r, t, p)),
                  pl.BlockSpec((None, length, 128), lambda r, p, t: (r, 0, 4 + p)),
                  pl.BlockSpec((None, length, 128), lambda r, p, t: (r, 0, p)),
                  pl.BlockSpec((3, None, 2 * QB, KB), lambda r, p, t: (0, p, 0, 0))],
        out_specs=(pl.BlockSpec((None, qt, 128), lambda r, p, t: (r, t, p)),
                   pl.BlockSpec((None, qt, 128), lambda r, p, t: (r, t, p))),
        compiler_params=_params(("parallel", "parallel", "arbitrary")),
    )(qkn_l, qkn_l, v_l, bias)


def _combine(o_g, lse_g, cg, vz, x, tgt, wout, cw, cb, b256):
    s = x.shape[0]
    tm = 256
    hb = 16
    nt = s // tm

    def body(o1, o4, o16, l1, l4, l16, cg_ref, cgp_ref, cgn_ref, za_ref, x_ref, t_ref, w_ref, cw_ref, cb_ref,
             b_ref, y_o, dout_o, lse_o, dd_o, do_o, dza_o, dgbz_o, dzc_o, loss_o, dcb_o, dcw_o):
        i = pl.program_id(0)

        @pl.when(i == 0)
        def _():
            loss_o[...] = jnp.zeros_like(loss_o)
            dcb_o[...] = jnp.zeros_like(dcb_o)
            dcw_o[...] = jnp.zeros_like(dcw_o)

        u = cg_ref[:, 0:512].astype(F32)
        gb = cg_ref[:, 512:1024].astype(F32)
        gc = cg_ref[:, 1024:1536].astype(F32)
        zc = cg_ref[:, 1536:2048].astype(F32)
        tt = gc * u
        t_prev = cgp_ref[hb - 1:hb, 0:512].astype(F32) * cgp_ref[hb - 1:hb, 1024:1536].astype(F32)
        t_next = cgn_ref[0:1, 0:512].astype(F32) * cgn_ref[0:1, 1024:1536].astype(F32)
        t_prev = jnp.where(i == 0, 0.0, t_prev)
        t_next = jnp.where(i == nt - 1, 0.0, t_next)
        rows = lax.broadcasted_iota(jnp.int32, (tm, 512), 0)
        t_up = jnp.where(rows == 0, t_prev, pltpu.roll(tt, 1, 0))
        t_dn = jnp.where(rows == tm - 1, t_next, pltpu.roll(tt, tm - 1, 0))
        w0, w1, w2 = cw_ref[0:1, :], cw_ref[1:2, :], cw_ref[2:3, :]
        zb = w0 * t_up + w1 * tt + w2 * t_dn + cb_ref[...]
        sg = _sigmoid(zc)
        sz = zc * sg
        y_conv = gb * zb * sz

        a1, a4, a16 = l1[...], l4[...], l16[...]
        m = jnp.maximum(jnp.maximum(a1, a4), a16)
        e1, e4, e16 = jnp.exp(a1 - m), jnp.exp(a4 - m), jnp.exp(a16 - m)
        den = e1 + e4 + e16
        lse_o[...] = m + jnp.log(den)
        o = (e1 * o1[...].astype(F32) + e4 * o4[...].astype(F32) + e16 * o16[...].astype(F32)) / den
        za = za_ref[...].astype(F32)
        sga = _sigmoid(za)
        sa = za * sga
        y = jnp.concatenate([y_conv, o * sa], axis=1).astype(BF16)
        y_o[...] = y

        out = x_ref[...] + jnp.dot(y, w_ref[...], preferred_element_type=F32)
        diff = out - t_ref[...]
        loss_o[...] += (0.5 / D_MODEL) * jnp.sum(diff * diff)
        dout = diff * (1.0 / D_MODEL)
        dout_o[...] = dout
        dy = lax.dot_general(dout.astype(BF16), w_ref[...], (((1,), (1,)), ((), ())), preferred_element_type=F32)
        dyc, dya = dy[:, 0:512], dy[:, 512:1024]

        do = dya * sa
        do_o[...] = do.astype(BF16)
        dza_o[...] = (dya * o * (sga * (1.0 + za * (1.0 - sga)))).astype(BF16)
        dd_o[...] = _group_sum(do * o, b_ref)

        dzc = dyc * sz * gb
        dzc_o[...] = dzc
        dgbz_o[:, 0:512] = (dyc * sz * zb).astype(BF16)
        dgbz_o[:, 512:1024] = (dyc * gb * zb * (sg * (1.0 + zc * (1.0 - sg)))).astype(BF16)
        dcb_o[...] += jnp.sum(dzc, axis=0, keepdims=True)
        dcw_o[0:1, :] += jnp.sum(dzc * t_up, axis=0, keepdims=True)
        dcw_o[1:2, :] += jnp.sum(dzc * tt, axis=0, keepdims=True)
        dcw_o[2:3, :] += jnp.sum(dzc * t_dn, axis=0, keepdims=True)

    row = lambda w, j=0: pl.BlockSpec((tm, w), lambda i: (i, j))
    full = lambda shp: pl.BlockSpec(shp, lambda i: (0,) * len(shp))
    prev = pl.BlockSpec((hb, 2048), lambda i: (jnp.maximum(i * (tm // hb) - 1, 0), 0))
    nxt = pl.BlockSpec((hb, 2048), lambda i: (jnp.minimum((i + 1) * (tm // hb), s // hb - 1), 0))
    return pl.pallas_call(
        body, name="combine", grid=(nt,),
        out_shape=(jax.ShapeDtypeStruct((s, 1024), BF16), jax.ShapeDtypeStruct((s, 1024), F32),
                   jax.ShapeDtypeStruct((s, 512), F32), jax.ShapeDtypeStruct((s, 512), F32),
                   jax.ShapeDtypeStruct((s, 512), BF16), jax.ShapeDtypeStruct((s, 512), BF16),
                   jax.ShapeDtypeStruct((s, 1024), BF16), jax.ShapeDtypeStruct((s, 512), F32),
                   jax.ShapeDtypeStruct((1, 128), F32), jax.ShapeDtypeStruct((1, 512), F32),
                   jax.ShapeDtypeStruct((8, 512), F32)),
        in_specs=[row(512)] * 6 + [row(2048), prev, nxt, row(512, 1), row(1024), row(1024),
                                   full((1024, 1024)), full((8, 512)), full((1, 512)), full((256, 256))],
        out_specs=(row(1024), row(1024), row(512), row(512), row(512), row(512), row(1024), row(512),
                   full((1, 128)), full((1, 512)), full((8, 512))),
        compiler_params=_params(("arbitrary",)),
    )(*o_g, *lse_g, cg, cg, cg, vz, x, tgt, wout, cw, cb, b256)


def _attn_bwd(qkn_l, v_l, do_l, lse_l, dd_l, bias, name):
    r_cls, length, _ = qkn_l.shape
    qt = min(length, 1024)
    nb, nsub, nt = length // QB, qt // QB, length // qt

    def body(q_ref, k_ref, v_ref, do_ref, lse_ref, dd_ref, b_ref, dq_ref, dkv_hbm, dsum_ref, dk_acc, dv_acc, sems):
        p_id, r, t = pl.program_id(0), pl.program_id(1), pl.program_id(2)
        lo = lax.broadcasted_iota(jnp.int32, (QB, 128), 1) < HEAD_DIM

        @pl.when(t == 0)
        def _():
            dk_acc[...] = jnp.zeros_like(dk_acc)
            dv_acc[...] = jnp.zeros_like(dv_acc)

        @pl.when((t == 0) & (r == 0))
        def _():
            dsum_ref[...] = jnp.zeros_like(dsum_ref)

        def sub(i, carry):
            q0, start, variant = _block_coords(t, i, nsub, nb, length)
            qq = _split_heads(q_ref[pl.ds(q0, QB), :], lo)
            dd = _split_heads(do_ref[pl.ds(q0, QB), :], lo)
            k = k_ref[pl.ds(start, KB), :]
            v = v_ref[pl.ds(start, KB), :]
            nt_dims = (((1,), (1,)), ((), ()))
            tn_dims = (((0,), (0,)), ((), ()))
            lg = lax.dot_general(qq, k, nt_dims, preferred_element_type=F32) + b_ref[variant]
            p = jnp.exp(lg - _col_pair(lse_ref, q0))
            dp = lax.dot_general(dd, v, nt_dims, preferred_element_type=F32)
            ds = p * (dp - _col_pair(dd_ref, q0))
            dsum_ref[variant] += ds
            dsb = ds.astype(BF16)
            dqq = jnp.dot(dsb, k, preferred_element_type=F32)
            dq_ref[pl.ds(q0, QB), :] = jnp.where(lo, dqq[0:QB], dqq[QB:2 * QB]).astype(BF16)
            dk_acc[pl.ds(start, KB), :] += lax.dot_general(dsb, qq, tn_dims, preferred_element_type=F32)
            dv_acc[pl.ds(start, KB), :] += lax.dot_general(p.astype(BF16), dd, tn_dims, preferred_element_type=F32)
            return carry

        lax.fori_loop(0, nsub, sub, 0)

        @pl.when(t == nt - 1)
        def _():
            ck = pltpu.make_async_copy(dk_acc, dkv_hbm.at[r, p_id, 0], sems.at[0])
            cv = pltpu.make_async_copy(dv_acc, dkv_hbm.at[r, p_id, 1], sems.at[1])
            ck.start()
            cv.start()
            ck.wait()
            cv.wait()

    qspec = pl.BlockSpec((None, qt, 128), lambda p, r, t: (r, t, p))
    return pl.pallas_call(
        body, name=name, grid=(N_PAIR, r_cls, nt),
        out_shape=(jax.ShapeDtypeStruct((r_cls, length, 512), BF16),
                   jax.ShapeDtypeStruct((r_cls, N_PAIR, 2, length, 128), F32),
                   jax.ShapeDtypeStruct((N_PAIR, 3, 2 * QB, KB), F32)),
        in_specs=[qspec,
                  pl.BlockSpec((None, length, 128), lambda p, r, t: (r, 0, 4 + p)),
                  pl.BlockSpec((None, length, 128), lambda p, r, t: (r, 0, p)),
                  qspec, qspec, qspec,
                  pl.BlockSpec((3, None, 2 * QB, KB), lambda p, r, t: (0, p, 0, 0))],
        out_specs=(qspec, pl.BlockSpec(memory_space=pl.ANY),
                   pl.BlockSpec((None, 3, 2 * QB, KB), lambda p, r, t: (p, 0, 0, 0))),
        scratch_shapes=[pltpu.VMEM((length, 128), F32), pltpu.VMEM((length, 128), F32),
                        pltpu.SemaphoreType.DMA((2,))],
        compiler_params=_params(("arbitrary", "arbitrary", "arbitrary")),
    )(qkn_l, qkn_l, v_l, do_l, lse_l, dd_l, bias)


def _bwd_tail(dq_g, dk_g, dv_g, qkr, qkw, dza, dgbz, dzc, cg, cw, wblk, x, norm_w, dout, b256):
    s = x.shape[0]
    tm = 256
    hb = 8
    nt = s // tm

    def body(dq1, dq4, dq16, dk1, dk4, dk16, dv1, dv4, dv16, qkr_ref, qkw_ref, dza_ref, dgbz_ref, dzc_ref,
             dzp_ref, dzn_ref, cg_ref, cw_ref, w_ref, x_ref, nw_ref, dout_ref, b_ref,
             gx_o, dproj_o, dnw_o, dqkw_o):
        i = pl.program_id(0)

        @pl.when(i == 0)
        def _():
            dnw_o[...] = jnp.zeros_like(dnw_o)
            dqkw_o[...] = jnp.zeros_like(dqkw_o)

        dzc = dzc_ref[...]
        d_prev = jnp.where(i == 0, 0.0, dzp_ref[hb - 1:hb, :])
        d_next = jnp.where(i == nt - 1, 0.0, dzn_ref[0:1, :])
        rows = lax.broadcasted_iota(jnp.int32, (tm, 512), 0)
        d_up = jnp.where(rows == 0, d_prev, pltpu.roll(dzc, 1, 0))
        d_dn = jnp.where(rows == tm - 1, d_next, pltpu.roll(dzc, tm - 1, 0))
        dt = cw_ref[0:1, :] * d_dn + cw_ref[1:2, :] * dzc + cw_ref[2:3, :] * d_up
        u = cg_ref[:, 0:512].astype(F32)
        gc = cg_ref[:, 1024:1536].astype(F32)
        dproj_o[:, 0:512] = (dt * gc).astype(BF16)
        dproj_o[:, 512:1024] = dgbz_ref[:, 0:512]
        dproj_o[:, 1024:1536] = (dt * u).astype(BF16)
        dproj_o[:, 1536:2048] = dgbz_ref[:, 512:1024]

        dqn = (dq1[...].astype(F32) + dq4[...].astype(F32) + dq16[...].astype(F32)) * (1.0 / 8.0)
        dkn = dk1[...] + dk4[...] + dk16[...]
        g = jnp.concatenate([dqn, dkn], axis=1) * qkw_ref[...]
        raw = qkr_ref[...].astype(F32)
        rr = lax.rsqrt(_group_sum(raw * raw, b_ref) * (1.0 / HEAD_DIM) + EPS)
        proj_gq = _group_sum(g * raw, b_ref) * (1.0 / HEAD_DIM)
        draw = rr * g - raw * (rr * rr * rr) * proj_gq
        dqkw_o[...] += jnp.sum(jnp.concatenate([dqn, dkn], axis=1) * raw * rr, axis=0, keepdims=True)
        dproj_o[:, 2048:3072] = draw.astype(BF16)
        dproj_o[:, 3072:3584] = (dv1[...] + dv4[...] + dv16[...]).astype(BF16)
        dproj_o[:, 3584:4096] = dza_ref[...]

        nt_dims = (((1,), (1,)), ((), ()))
        dh = lax.dot_general(dproj_o[:, 0:1024], w_ref[0], nt_dims, preferred_element_type=F32)
        for b in range(1, 4):
            dh += lax.dot_general(dproj_o[:, 1024 * b:1024 * b + 1024], w_ref[b], nt_dims,
                                  preferred_element_type=F32)

        xf = x_ref[...]
        r = lax.rsqrt(jnp.mean(xf * xf, axis=-1, keepdims=True) + EPS)
        gh = dh * nw_ref[...]
        dnw_o[...] += jnp.sum(dh * xf * r, axis=0, keepdims=True)
        mean_gx = jnp.mean(gh * xf, axis=-1, keepdims=True)
        gx_o[...] = dout_ref[...] + r * gh - xf * (r * r * r) * mean_gx

    row = lambda w, j=0: pl.BlockSpec((tm, w), lambda i: (i, j))
    full = lambda shp: pl.BlockSpec(shp, lambda i: (0,) * len(shp))
    prev = pl.BlockSpec((hb, 512), lambda i: (jnp.maximum(i * (tm // hb) - 1, 0), 0))
    nxt = pl.BlockSpec((hb, 512), lambda i: (jnp.minimum((i + 1) * (tm // hb), s // hb - 1), 0))
    return pl.pallas_call(
        body, name="bwd_tail", grid=(nt,),
        out_shape=(jax.ShapeDtypeStruct((s, 1024), F32), jax.ShapeDtypeStruct((s, 4096), BF16),
                   jax.ShapeDtypeStruct((1, 1024), F32), jax.ShapeDtypeStruct((1, 1024), F32)),
        in_specs=[row(512)] * 9 + [row(1024), full((1, 1024)), row(512), row(1024), row(512), prev, nxt,
                                   row(2048), full((8, 512)), full((4, 1024, 1024)), row(1024),
                                   full((1, 1024)), row(1024), full((256, 256))],
        out_specs=(row(1024), row(4096), full((1, 1024)), full((1, 1024))),
        compiler_params=_params(("arbitrary",)),
    )(*dq_g, *dk_g, *dv_g, qkr, qkw, dza, dgbz, dzc, dzc, dzc, cg, cw, wblk, x, norm_w, dout, b256)


def _wgrad(a, b, row_blocked, name):
    s, m = a.shape
    n = b.shape[1]
    tk = 512
    nj = n // 1024

    def body(a_ref, b_ref, o_ref, acc):
        kk = pl.program_id(1)

        @pl.when(kk == 0)
        def _():
            acc[...] = jnp.zeros_like(acc)

        acc[...] += lax.dot_general(a_ref[...], b_ref[...].astype(BF16), (((0,), (0,)), ((), ())),
                                    preferred_element_type=F32)

        @pl.when(kk == pl.num_programs(1) - 1)
        def _():
            o_ref[...] = acc[...].astype(BF16).reshape(o_ref.shape)

    if row_blocked:
        out_shape = jax.ShapeDtypeStruct((4, 2, m // 8, 1024), BF16)
        out_spec = pl.BlockSpec((4, 2, m // 8, 1024), lambda j, k: (0, 0, 0, 0))
    else:
        out_shape = jax.ShapeDtypeStruct((nj, 2, m // 2, 1024), BF16)
        out_spec = pl.BlockSpec((None, 2, m // 2, 1024), lambda j, k: (j, 0, 0, 0))
    return pl.pallas_call(
        body, name=name, grid=(nj, s // tk),
        out_shape=out_shape,
        in_specs=[pl.BlockSpec((tk, m), lambda j, k: (k, 0)), pl.BlockSpec((tk, 1024), lambda j, k: (k, j))],
        out_specs=out_spec,
        scratch_shapes=[pltpu.VMEM((m, 1024), F32)],
        compiler_params=_params(("parallel", "arbitrary")),
    )(a, b)


def _dbias(dsum_all, onehot_all):
    def body(ds_ref, oh_ref, o_ref):
        step = pl.program_id(0) * 3 + pl.program_id(1)

        @pl.when(step == 0)
        def _():
            o_ref[...] = jnp.zeros_like(o_ref)

        rowq = lax.broadcasted_iota(jnp.int32, (2 * QB, KB), 0) & (QB - 1)
        hrow = lax.broadcasted_iota(jnp.int32, (8, KB), 0)
        diag = jnp.zeros((8, KB), F32)
        for p in range(N_PAIR):
            y = ds_ref[p]
            for bit in range(7):
                sh = 1 << bit
                y = jnp.where((rowq & sh) != 0, pltpu.roll(y, KB - sh, 1), y)
            da = jnp.sum(y[0:QB], axis=0, keepdims=True)
            db = jnp.sum(y[QB:2 * QB], axis=0, keepdims=True)
            diag = jnp.where(hrow == 2 * p, da, diag)
            diag = jnp.where(hrow == 2 * p + 1, db, diag)
        o_ref[...] += jnp.dot(diag, oh_ref[...], preferred_element_type=F32, precision=lax.Precision.HIGHEST)

    return pl.pallas_call(
        body, name="dbias", grid=(3, 3),
        out_shape=jax.ShapeDtypeStruct((8, 128), F32),
        in_specs=[pl.BlockSpec((None, N_PAIR, None, 2 * QB, KB), lambda g, v: (g, 0, v, 0, 0)),
                  pl.BlockSpec((None, None, KB, 128), lambda g, v: (g, v, 0, 0))],
        out_specs=pl.BlockSpec((8, 128), lambda g, v: (0, 0)),
        compiler_params=_params(("arbitrary", "arbitrary")),
    )(dsum_all, onehot_all)


def _gsync(pw_in, pw_out, small):
    hin, hout = pw_in.shape[2], pw_out.shape[2]
    nsmall = small.shape[0]

    def body(pin_hbm, pout_hbm, small_ref, gin_o, gout_o, small_o,
             mine_in, recv_in, sbuf_in, rbuf_in, mine_out, recv_out, sbuf_out, rbuf_out, gather,
             lsem, asend, arecv, bsend, brecv, csend, crecv, ssend, srecv):
        x, y, c = lax.axis_index("x"), lax.axis_index("y"), lax.axis_index("c")
        b = 2 * x + y
        dev = 4 * x + 2 * y + c
        sib = (x, y, 1 - c)

        def rcopy(src, dst, ssem, rsem, to):
            return pltpu.make_async_remote_copy(src_ref=src, dst_ref=dst, send_sem=ssem, recv_sem=rsem,
                                                device_id=to, device_id_type=MESH)

        gather[dev] = small_ref[...]
        s_sends = []
        for k in range(1, 8):
            to = (x ^ (k >> 2), y ^ ((k >> 1) & 1), c ^ (k & 1))
            cp = rcopy(gather.at[dev], gather.at[dev], ssend.at[k - 1], srecv.at[k - 1], to)
            cp.start()
            s_sends.append(cp)

        a_in = rcopy(pin_hbm.at[:, 1 - c], recv_in, asend.at[0], arecv.at[0], sib)
        a_out = rcopy(pout_hbm.at[:, 1 - c], recv_out, asend.at[1], arecv.at[1], sib)
        a_in.start()
        a_out.start()
        l_in = pltpu.make_async_copy(pin_hbm.at[:, c], mine_in, lsem.at[0])
        l_out = pltpu.make_async_copy(pout_hbm.at[:, c], mine_out, lsem.at[1])
        l_in.start()
        l_out.start()
        l_in.wait()
        l_out.wait()

        def stage_b(a_cp, mine, recv, sbuf, rbuf, base):
            a_cp.wait_recv()
            sends = []
            for k in (1, 2, 3):
                bk = b ^ k
                sbuf[k - 1] = (mine[bk].astype(F32) + recv[bk].astype(F32)).astype(BF16)
                cp = rcopy(sbuf.at[k - 1], rbuf.at[k - 1], bsend.at[base + k - 1], brecv.at[base + k - 1],
                           (x ^ (k >> 1), y ^ (k & 1), c))
                cp.start()
                sends.append(cp)
            return sends

        b_in = stage_b(a_in, mine_in, recv_in, sbuf_in, rbuf_in, 0)
        b_out = stage_b(a_out, mine_out, recv_out, sbuf_out, rbuf_out, 3)

        def stage_c(b_sends, mine, recv, rbuf, g_o, half, idx):
            acc = mine[b].astype(F32) + recv[b].astype(F32)
            for k in (1, 2, 3):
                b_sends[k - 1].wait_recv()
                acc = acc + rbuf[k - 1].astype(F32)
            rows = g_o.at[pl.ds(pl.multiple_of(c * half, half), half), :]
            g_o[pl.ds(pl.multiple_of(c * half, half), half), :] = acc
            cp = rcopy(rows, rows, csend.at[idx], crecv.at[idx], sib)
            cp.start()
            return cp

        c_in = stage_c(b_in, mine_in, recv_in, rbuf_in, gin_o, hin, 0)
        c_out = stage_c(b_out, mine_out, recv_out, rbuf_out, gout_o, hout, 1)

        for cp in s_sends:
            cp.wait_recv()
        tot = gather[0]
        for d in range(1, 8):
            tot = tot + gather[d]
        small_o[...] = tot

        for g_o, half, idx in ((gin_o, hin, 0), (gout_o, hout, 1)):
            other = g_o.at[pl.ds(pl.multiple_of((1 - c) * half, half), half), :]
            rcopy(other, other, csend.at[idx], crecv.at[idx], sib).wait_recv()
        for cp in s_sends + [a_in, a_out] + b_in + b_out + [c_in, c_out]:
            cp.wait_send()

    vm = pl.BlockSpec(memory_space=pltpu.VMEM)
    hbm = pl.BlockSpec(memory_space=pl.ANY)
    return pl.pallas_call(
        body, name="gsync",
        out_shape=(jax.ShapeDtypeStruct((2 * hin, 1024), F32), jax.ShapeDtypeStruct((2 * hout, 1024), F32),
                   jax.ShapeDtypeStruct((nsmall, 128), F32)),
        in_specs=[hbm, hbm, vm], out_specs=(vm, vm, vm),
        scratch_shapes=[pltpu.VMEM((4, hin, 1024), BF16), pltpu.VMEM((4, hin, 1024), BF16),
                        pltpu.VMEM((3, hin, 1024), BF16), pltpu.VMEM((3, hin, 1024), BF16),
                        pltpu.VMEM((4, hout, 1024), BF16), pltpu.VMEM((4, hout, 1024), BF16),
                        pltpu.VMEM((3, hout, 1024), BF16), pltpu.VMEM((3, hout, 1024), BF16),
                        pltpu.VMEM((8, nsmall, 128), F32),
                        pltpu.SemaphoreType.DMA((2,)),
                        pltpu.SemaphoreType.DMA((2,)), pltpu.SemaphoreType.DMA((2,)),
                        pltpu.SemaphoreType.DMA((6,)), pltpu.SemaphoreType.DMA((6,)),
                        pltpu.SemaphoreType.DMA((2,)), pltpu.SemaphoreType.DMA((2,)),
                        pltpu.SemaphoreType.DMA((7,)), pltpu.SemaphoreType.DMA((7,))],
        compiler_params=_params(),
    )(pw_in, pw_out, small)


def _adamw_math(w, g, m, v):
    m = ADAM_B1 * m + (1.0 - ADAM_B1) * g
    v = ADAM_B2 * v + (1.0 - ADAM_B2) * (g * g)
    m_hat = m / (1.0 - ADAM_B1 ** ADAM_STEP)
    v_hat = v / (1.0 - ADAM_B2 ** ADAM_STEP)
    delta = -ADAM_LR * (m_hat / (jnp.sqrt(v_hat) + ADAM_EPS) + ADAM_WD * w)
    return delta, m, v


def _adamw(w, g, m, v, name):
    rows, cols = w.shape
    tr = 256 if rows % 256 == 0 else rows

    def body(w_ref, g_ref, m_ref, v_ref, d_o, m_o, v_o):
        d, m2, v2 = _adamw_math(w_ref[...], g_ref[...], m_ref[...], v_ref[...])
        d_o[...] = d
        m_o[...] = m2
        v_o[...] = v2

    spec = pl.BlockSpec((tr, cols), lambda i: (i, 0))
    shp = jax.ShapeDtypeStruct((rows, cols), F32)
    return pl.pallas_call(
        body, name=name, grid=(rows // tr,), out_shape=(shp, shp, shp),
        in_specs=[spec] * 4, out_specs=(spec, spec, spec),
        compiler_params=_params(("parallel",)),
    )(w, g, m, v)


def _fold_heads(dqkw):
    def body(x_ref, o_ref):
        xs = x_ref[...]
        sq = xs[0:1] + xs[1:2] + xs[2:3] + xs[3:4]
        sk = xs[4:5] + xs[5:6] + xs[6:7] + xs[7:8]
        both = jnp.concatenate([sq, sk], axis=0)
        o_ref[...] = both + pltpu.roll(both, HEAD_DIM, 1)

    vm = pl.BlockSpec(memory_space=pltpu.VMEM)
    return pl.pallas_call(body, name="fold_heads", out_shape=jax.ShapeDtypeStruct((2, 128), F32),
                          in_specs=[vm], out_specs=vm, compiler_params=_params())(dqkw)


def kernel(x, norm_w, w_in, conv_w, conv_b, q_norm_w, k_norm_w, rel_bias, w_out, loss_target, m_norm_w, m_w_in, m_conv_w, m_conv_b, m_q_norm_w, m_k_norm_w, m_rel_bias, m_w_out, v_norm_w, v_w_in, v_conv_w, v_conv_b, v_q_norm_w, v_k_norm_w, v_rel_bias, v_w_out):
    x2 = x[0]
    tgt = loss_target[0]
    blk = 2 * lax.axis_index("x") + lax.axis_index("y")

    conv_w8 = jnp.pad(conv_w, ((0, 5), (0, 0)))
    wblk, woutblk, cwblk = _wgather(w_in, w_out, conv_w8)
    wout_full = woutblk.reshape(1024, 1024)
    cw_full = cwblk.transpose(1, 0, 2).reshape(8, 512)

    qkw = jnp.concatenate([jnp.tile(q_norm_w, 8) * 0.125, jnp.tile(k_norm_w, 8)])[None, :]
    qkw_raw = jnp.concatenate([jnp.tile(q_norm_w, 8), jnp.tile(k_norm_w, 8)])[None, :]
    gidx = jnp.arange(256) // HEAD_DIM
    b256 = (gidx[:, None] == gidx[None, :]).astype(BF16)

    h, cg, qkr, qkn, vz = _proj(x2, norm_w[None, :], wblk, qkw, b256)

    biases = [_bias_tables(rel_bias, d) for d in DILATIONS]
    qkn_l = [_to_layout(qkn, d) for d in DILATIONS]
    v_l = [_to_layout(vz[:, 0:512], d) for d in DILATIONS]
    o_g, lse_g = [], []
    for gi, d in enumerate(DILATIONS):
        o_l, lse_l = _attn_fwd(qkn_l[gi], v_l[gi], biases[gi], f"attn_fwd_d{d}")
        o_g.append(_from_layout(o_l))
        lse_g.append(_from_layout(lse_l))

    (y, dout, lse, dd, do, dza, dgbz, dzc, loss_p, dcb, dcw) = _combine(
        o_g, lse_g, cg, vz, x2, tgt, wout_full, cw_full, conv_b[None, :], b256)

    dq_g, dk_g, dv_g, dsums = [], [], [], []
    for gi, d in enumerate(DILATIONS):
        dq_l, dkv_l, dsum = _attn_bwd(qkn_l[gi], v_l[gi], _to_layout(do, d), _to_layout(lse, d),
                                      _to_layout(dd, d), biases[gi], f"attn_bwd_d{d}")
        dq_g.append(_from_layout(dq_l))
        r_cls, _, _, length, _ = dkv_l.shape
        nat = dkv_l.transpose(2, 3, 0, 1, 4).reshape(2, length * r_cls, 512)
        dk_g.append(nat[0])
        dv_g.append(nat[1])
        dsums.append(dsum)

    grad_x, dproj, dnw, dqkw = _bwd_tail(dq_g, dk_g, dv_g, qkr, qkw_raw, dza, dgbz, dzc, cg, cw_full, wblk,
                                         x2, norm_w[None, :], dout, b256)

    pw_in = _wgrad(h, dproj, False, "wgrad_in")
    pw_out = _wgrad(y, dout, True, "wgrad_out")
    dbias8 = _dbias(jnp.stack(dsums, axis=0), jnp.stack([_diag_bucket_onehot(d) for d in DILATIONS], axis=0))

    small = jnp.concatenate([dnw.reshape(8, 128), dcb.reshape(4, 128), dqkw.reshape(8, 128),
                             dcw[0:3].reshape(12, 128), dbias8], axis=0)
    g_win, g_wout, gsmall = _gsync(pw_in, pw_out, small)

    g_nw = gsmall[0:8].reshape(1024)
    g_cb = gsmall[8:12].reshape(512)
    folded = _fold_heads(gsmall[12:20])
    g_qw, g_kw = folded[0, 0:64], folded[1, 0:64]
    g_cw = lax.dynamic_slice(gsmall[20:32].reshape(3, 512), (0, blk * 128), (3, 128))
    g_rb = gsmall[32:40][:, 0:32].T

    loss = lax.psum(loss_p[0, 0], ("x", "y", "c"))

    d_win, nm_win, nv_win = _adamw(w_in, g_win, m_w_in, v_w_in, "adamw_w_in")
    d_wout, nm_wout, nv_wout = _adamw(w_out, g_wout, m_w_out, v_w_out, "adamw_w_out")

    def pack(parts):
        rows = [parts[0].reshape(8, 128), parts[1].reshape(4, 128),
                jnp.pad(parts[2], (0, 64))[None, :], jnp.pad(parts[3], (0, 64))[None, :],
                parts[4], jnp.pad(parts[5].T, ((0, 0), (0, 96)))]
        return jnp.concatenate(rows, axis=0)

    ws = pack([norm_w, conv_b, q_norm_w, k_norm_w, conv_w, rel_bias])
    gs = pack([g_nw, g_cb, g_qw, g_kw, g_cw, g_rb])
    ms = pack([m_norm_w, m_conv_b, m_q_norm_w, m_k_norm_w, m_conv_w, m_rel_bias])
    vs = pack([v_norm_w, v_conv_b, v_q_norm_w, v_k_norm_w, v_conv_w, v_rel_bias])
    rpad = lambda a: jnp.pad(a, ((0, 7), (0, 0)))
    d_s, nm_s, nv_s = _adamw(rpad(ws), rpad(gs), rpad(ms), rpad(vs), "adamw_small")

    def unpack(a):
        return (a[0:8].reshape(1024), a[12:13, 0:64].reshape(64), a[13:14, 0:64].reshape(64),
                a[14:17], a[8:12].reshape(512), a[17:25, 0:32].T)

    def ordered(nw, win, cw, cb, qw, kw, rb, wout):
        return (nw, win, cw, cb, qw, kw, rb, wout)

    g_un = (g_nw, g_qw, g_kw, g_cw, g_cb, g_rb)
    outs = [loss, grad_x[None]]
    for un, win_v, wout_v in ((g_un, g_win, g_wout), (unpack(d_s), d_win, d_wout),
                              (unpack(nm_s), nm_win, nm_wout), (unpack(nv_s), nv_win, nv_wout)):
        nw, qw, kw, cw, cb, rb = un
        outs.extend(ordered(nw, win_v, cw, cb, qw, kw, rb, wout_v))
    return tuple(outs)
```

```python
import math

import jax
import jax.numpy as jnp
from jax import lax
from jax.experimental import pallas as pl
from jax.experimental.pallas import tpu as pltpu

F32 = jnp.float32
BF16 = jnp.bfloat16
MESH = pl.DeviceIdType.MESH

D_MODEL = 1024
CONV_W = 512
ATTN_W = 512
HEAD_DIM = 64
N_PAIR = 4
DILATIONS = (1, 4, 16)
HALF = 64
QB = 128
KB = QB + 2 * HALF
NUM_BUCKETS = 32
MAX_DISTANCE = 1024
EPS = 1e-6
NEG = -1e30
ADAM_LR, ADAM_B1, ADAM_B2, ADAM_EPS, ADAM_WD, ADAM_STEP = 0.001, 0.9, 0.999, 1e-08, 0.01, 10
VMEM_LIMIT = 48 << 20


def _params(sem=None, vmem=VMEM_LIMIT, **kw):
    if sem is not None:
        kw["dimension_semantics"] = sem
    return pltpu.CompilerParams(vmem_limit_bytes=vmem, **kw)


def _sigmoid(z):
    return 1.0 / (1.0 + jnp.exp(-z))


def _group_sum(val, b_ref):
    hi = val.astype(BF16)
    lo = (val - hi.astype(F32)).astype(BF16)
    outs = []
    for j in range(val.shape[1] // 256):
        sl = slice(256 * j, 256 * j + 256)
        outs.append(jnp.dot(hi[:, sl], b_ref[...], preferred_element_type=F32)
                    + jnp.dot(lo[:, sl], b_ref[...], preferred_element_type=F32))
    return outs[0] if len(outs) == 1 else jnp.concatenate(outs, axis=1)


def _t5_bucket(rel):
    half_b = NUM_BUCKETS // 2
    max_exact = half_b // 2
    ret = jnp.where(rel > 0, half_b, 0)
    n = jnp.abs(rel)
    nf = jnp.maximum(n, 1).astype(F32)
    large = max_exact + (jnp.log(nf / max_exact) / math.log(MAX_DISTANCE / max_exact)
                         * (half_b - max_exact)).astype(jnp.int32)
    large = jnp.minimum(large, half_b - 1)
    return ret + jnp.where(n < max_exact, n, large)


def _window_rel(variant):
    off = (0, HALF, 2 * HALF)[variant]
    return jnp.arange(KB)[None, :] - off - jnp.arange(QB)[:, None]


def _bias_tables(rel_bias):
    bkts = []
    for dilation in DILATIONS:
        for variant in range(3):
            rel = _window_rel(variant)
            bkt = _t5_bucket(jnp.clip(rel, -HALF, HALF) * dilation)
            bkts.append(jnp.where(jnp.abs(rel) <= HALF, bkt, -1))
    bkt_all = jnp.stack(bkts, axis=0).astype(jnp.int32)

    def body(rb_ref, bkt_ref, o_ref):
        bkt = bkt_ref[...]
        for h in range(8):
            acc = jnp.full((QB, KB), NEG, F32)
            for b in range(NUM_BUCKETS):
                acc = jnp.where(bkt == b, rb_ref[b, h], acc)
            o_ref[h] = acc

    out = pl.pallas_call(
        body, name="bias_tables", grid=(9,),
        out_shape=jax.ShapeDtypeStruct((9, 8, QB, KB), F32),
        in_specs=[pl.BlockSpec(memory_space=pltpu.SMEM), pl.BlockSpec((None, QB, KB), lambda i: (i, 0, 0))],
        out_specs=pl.BlockSpec((None, 8, QB, KB), lambda i: (i, 0, 0, 0)),
        compiler_params=_params(("parallel",)),
    )(rel_bias, bkt_all)
    return out.reshape(3, 3, N_PAIR, 2 * QB, KB)


def _diag_bucket_onehot(dilation):
    out = []
    c = jnp.arange(KB)
    for variant in range(3):
        off = (0, HALF, 2 * HALF)[variant]
        rel = ((c - off + 128) % 256) - 128
        band = jnp.abs(rel) <= HALF
        bkt = _t5_bucket(jnp.clip(rel, -HALF, HALF) * dilation)
        oh = (bkt[:, None] == jnp.arange(128)[None, :]) & band[:, None]
        out.append(oh.astype(F32))
    return jnp.stack(out, axis=0)


def _wgather(w_in, w_out, conv_w):
    rin, rout = w_in.shape[0] // 2, w_out.shape[0] // 2

    def body(win_ref, wout_ref, cw_ref, win_o, wout_o, cw_o, send_sems, recv_sems):
        x, y, c = lax.axis_index("x"), lax.axis_index("y"), lax.axis_index("c")
        b = 2 * x + y
        win_o[b] = win_ref[...].astype(BF16)
        wout_o[b] = wout_ref[...].astype(BF16)
        cw_o[b] = cw_ref[...]

        def peer(k):
            return (x ^ (k >> 1), y ^ (k & 1))

        def piece(ref, blk, half_rows, core):
            return ref.at[blk, pl.ds(core * half_rows, half_rows), :]

        def copy(sem, src, dst, to):
            return pltpu.make_async_remote_copy(src_ref=src, dst_ref=dst, send_sem=send_sems.at[sem],
                                                recv_sem=recv_sems.at[sem], device_id=to, device_id_type=MESH)

        sends = []
        for k in (1, 2, 3):
            px, py = peer(k)
            sends.append(copy(k - 1, piece(win_o, b, rin, c), piece(win_o, b, rin, c), (px, py, c)))
            sends.append(copy(3 + k - 1, piece(wout_o, b, rout, c), piece(wout_o, b, rout, c), (px, py, c)))
            sends.append(copy(6 + k - 1, cw_o.at[b], cw_o.at[b], (px, py, c)))
        for cp in sends:
            cp.start()
        fwd = []
        for k in (1, 2, 3):
            px, py = peer(k)
            bk = 2 * px + py
            copy(k - 1, piece(win_o, bk, rin, c), piece(win_o, bk, rin, c), (px, py, c)).wait_recv()
            f = copy(9 + k - 1, piece(win_o, bk, rin, c), piece(win_o, bk, rin, c), (x, y, 1 - c))
            f.start()
            fwd.append(f)
            copy(3 + k - 1, piece(wout_o, bk, rout, c), piece(wout_o, bk, rout, c), (px, py, c)).wait_recv()
            f = copy(12 + k - 1, piece(wout_o, bk, rout, c), piece(wout_o, bk, rout, c), (x, y, 1 - c))
            f.start()
            fwd.append(f)
            copy(6 + k - 1, cw_o.at[bk], cw_o.at[bk], (px, py, c)).wait_recv()
        for k in (1, 2, 3):
            px, py = peer(k)
            bk = 2 * px + py
            copy(9 + k - 1, piece(win_o, bk, rin, 1 - c), piece(win_o, bk, rin, 1 - c), (x, y, 1 - c)).wait_recv()
            copy(12 + k - 1, piece(wout_o, bk, rout, 1 - c), piece(wout_o, bk, rout, 1 - c), (x, y, 1 - c)).wait_recv()
        for cp in sends + fwd:
            cp.wait_send()

    vm = pl.BlockSpec(memory_space=pltpu.VMEM)
    return pl.pallas_call(
        body, name="wgather",
        out_shape=(jax.ShapeDtypeStruct((4,) + w_in.shape, BF16),
                   jax.ShapeDtypeStruct((4,) + w_out.shape, BF16),
                   jax.ShapeDtypeStruct((4,) + conv_w.shape, F32)),
        in_specs=[vm, vm, vm], out_specs=(vm, vm, vm),
        scratch_shapes=[pltpu.SemaphoreType.DMA((15,)), pltpu.SemaphoreType.DMA((15,))],
        compiler_params=_params(),
    )(w_in, w_out, conv_w)


TM = 256


def _to_slabs(slab, val, j0=0):
    for j in range(val.shape[1] // 128):
        slab[j0 + j] = val[:, 128 * j:128 * (j + 1)]


def _scatter_classes(slab, j0, nj, out_ref, d):
    n = TM // d
    for r in range(d):
        for j in range(nj):
            out_ref[r, :, 128 * j:128 * (j + 1)] = slab[j0 + j, pl.ds(r, n, stride=d), :].astype(out_ref.dtype)


def _gather_classes(slab, piece, nj, d):
    n = TM // d
    for r in range(d):
        for j in range(nj):
            slab[j, pl.ds(r, n, stride=d), :] = piece(r, j).astype(F32)
    return jnp.concatenate([slab[j] for j in range(nj)], axis=1)


def _class_spec(d, width):
    return pl.BlockSpec((d, TM // d, width), lambda i: (0, i, 0))


def _proj(x, norm_w, wblk, qkw, b256):
    s = x.shape[0]

    def body(x_ref, nw_ref, w_ref, qkw_ref, b_ref, h_o, cg_o, qkr_o, qkn_o, vz_o, qkn4_o, v4_o, qkn16_o, v16_o, slab):
        xf = x_ref[...]
        r = lax.rsqrt(jnp.mean(xf * xf, axis=-1, keepdims=True) + EPS)
        h = (xf * r * nw_ref[...]).astype(BF16)
        h_o[...] = h
        cg_o[:, 0:1024] = jnp.dot(h, w_ref[0], preferred_element_type=F32).astype(BF16)
        cg_o[:, 1024:2048] = jnp.dot(h, w_ref[1], preferred_element_type=F32).astype(BF16)
        p2 = jnp.dot(h, w_ref[2], preferred_element_type=F32)
        qkr_o[...] = p2.astype(BF16)
        ss = _group_sum(p2 * p2, b_ref)
        rr = lax.rsqrt(ss * (1.0 / HEAD_DIM) + EPS)
        qkn = p2 * rr * qkw_ref[...]
        qkn_o[...] = qkn.astype(BF16)
        p3 = jnp.dot(h, w_ref[3], preferred_element_type=F32)
        vz_o[...] = p3.astype(BF16)
        _to_slabs(slab, qkn)
        _to_slabs(slab, p3[:, 0:512], 8)
        for d, q_o, v_o in ((4, qkn4_o, v4_o), (16, qkn16_o, v16_o)):
            _scatter_classes(slab, 0, 8, q_o, d)
            _scatter_classes(slab, 8, 4, v_o, d)

    row = lambda w: pl.BlockSpec((TM, w), lambda i: (i, 0))
    full = lambda shp: pl.BlockSpec(shp, lambda i: (0,) * len(shp))
    nat = lambda w: jax.ShapeDtypeStruct((s, w), BF16)
    cls = lambda d, w: jax.ShapeDtypeStruct((d, s // d, w), BF16)
    return pl.pallas_call(
        body, name="proj", grid=(s // TM,),
        out_shape=(nat(1024), nat(2048), nat(1024), nat(1024), nat(1024),
                   cls(4, 1024), cls(4, 512), cls(16, 1024), cls(16, 512)),
        in_specs=[row(1024), full((1, 1024)), full((4, 1024, 1024)), full((1, 1024)), full((256, 256))],
        out_specs=(row(1024), row(2048), row(1024), row(1024), row(1024),
                   _class_spec(4, 1024), _class_spec(4, 512), _class_spec(16, 1024), _class_spec(16, 512)),
        scratch_shapes=[pltpu.VMEM((12, TM, 128), F32)],
        compiler_params=_params(("parallel",)),
    )(x, norm_w, wblk, qkw, b256)


def _block_coords(t, i, nsub, nb, length):
    n = t * nsub + i
    q0 = i * QB
    start = pl.multiple_of(jnp.clip(n * QB - HALF, 0, length - KB), HALF)
    variant = jnp.where(n == 0, 0, jnp.where(n == nb - 1, 2, 1))
    return q0, start, variant


def _split_heads(a, lo):
    zero = jnp.zeros_like(a)
    return jnp.concatenate([jnp.where(lo, a, zero), jnp.where(lo, zero, a)], axis=0)


def _col_pair(ref, q0, lane):
    return jnp.concatenate([ref[pl.ds(q0, QB), lane:lane + 1],
                            ref[pl.ds(q0, QB), HEAD_DIM + lane:HEAD_DIM + lane + 1]], axis=0)


def _attn_fwd(qkn_l, v_l, bias, name):
    r_cls, length, _ = qkn_l.shape
    qt = min(length, 1024)
    nb, nsub = length // QB, qt // QB

    def body(q_ref, k_ref, v_ref, b_ref, o_ref, lse_ref):
        t = pl.program_id(2)
        lo = lax.broadcasted_iota(jnp.int32, (QB, 128), 1) < HEAD_DIM

        starts, logits = [], []
        for i in range(nsub):
            _, start, variant = _block_coords(t, i, nsub, nb, length)
            qq = _split_heads(q_ref[i * QB:(i + 1) * QB, :], lo)
            k = k_ref[pl.ds(start, KB), :]
            logits.append(lax.dot_general(qq, k, (((1,), (1,)), ((), ())), preferred_element_type=F32)
                          + b_ref[variant])
            starts.append(start)
        lg = jnp.concatenate(logits, axis=0)
        m = jnp.max(lg, axis=-1, keepdims=True)
        pb = jnp.exp(lg - m).astype(BF16)
        l = jnp.dot(pb, jnp.ones((KB, 128), BF16), preferred_element_type=F32)
        lse = m + jnp.log(l)
        inv = 1.0 / l
        for i in range(nsub):
            rows = slice(2 * QB * i, 2 * QB * (i + 1))
            v = v_ref[pl.ds(starts[i], KB), :]
            pv = jnp.dot(pb[rows], v, preferred_element_type=F32) * inv[rows]
            o_ref[i * QB:(i + 1) * QB, :] = jnp.where(lo, pv[0:QB], pv[QB:2 * QB]).astype(BF16)
            ls = lse[rows]
            lse_ref[i * QB:(i + 1) * QB, :] = jnp.where(lo, ls[0:QB], ls[QB:2 * QB])

    return pl.pallas_call(
        body, name=name, grid=(r_cls, N_PAIR, length // qt),
        out_shape=(jax.ShapeDtypeStruct((r_cls, length, 512), BF16),
                   jax.ShapeDtypeStruct((r_cls, length, 512), F32)),
        in_specs=[pl.BlockSpec((None, qt, 128), lambda r, p, t: (r, t, p)),
                  pl.BlockSpec((None, length, 128), lambda r, p, t: (r, 0, 4 + p)),
                  pl.BlockSpec((None, length, 128), lambda r, p, t: (r, 0, p)),
                  pl.BlockSpec((3, None, 2 * QB, KB), lambda r, p, t: (0, p, 0, 0))],
        out_specs=(pl.BlockSpec((None, qt, 128), lambda r, p, t: (r, t, p)),
                   pl.BlockSpec((None, qt, 128), lambda r, p, t: (r, t, p))),
        compiler_params=_params(("parallel", "parallel", "arbitrary")),
    )(qkn_l, qkn_l, v_l, bias)


def _combine(o_g, lse_g, cg, vz, x, tgt, wout, cw, cb, b256):
    s = x.shape[0]
    tm = TM
    hb = 16
    nt = s // tm

    def body(o1, o4, o16, l1, l4, l16, cg_ref, cgp_ref, cgn_ref, za_ref, x_ref, t_ref, w_ref, cw_ref, cb_ref,
             b_ref, y_o, dout_o, ld1_o, do1_o, dza_o, dgbz_o, dzc_o, loss_o, dcb_o, dcw_o,
             do4_o, ld4_o, do16_o, ld16_o, slab):
        i = pl.program_id(0)

        @pl.when(i == 0)
        def _():
            loss_o[...] = jnp.zeros_like(loss_o)
            dcb_o[...] = jnp.zeros_like(dcb_o)
            dcw_o[...] = jnp.zeros_like(dcw_o)

        u = cg_ref[:, 0:512].astype(F32)
        gb = cg_ref[:, 512:1024].astype(F32)
        gc = cg_ref[:, 1024:1536].astype(F32)
        zc = cg_ref[:, 1536:2048].astype(F32)
        tt = gc * u
        t_prev = cgp_ref[hb - 1:hb, 0:512].astype(F32) * cgp_ref[hb - 1:hb, 1024:1536].astype(F32)
        t_next = cgn_ref[0:1, 0:512].astype(F32) * cgn_ref[0:1, 1024:1536].astype(F32)
        t_prev = jnp.where(i == 0, 0.0, t_prev)
        t_next = jnp.where(i == nt - 1, 0.0, t_next)
        rows = lax.broadcasted_iota(jnp.int32, (tm, 512), 0)
        t_up = jnp.where(rows == 0, t_prev, pltpu.roll(tt, 1, 0))
        t_dn = jnp.where(rows == tm - 1, t_next, pltpu.roll(tt, tm - 1, 0))
        w0, w1, w2 = cw_ref[0:1, :], cw_ref[1:2, :], cw_ref[2:3, :]
        zb = w0 * t_up + w1 * tt + w2 * t_dn + cb_ref[...]
        sg = _sigmoid(zc)
        sz = zc * sg
        y_conv = gb * zb * sz

        a1, p1 = l1[0], o1[0].astype(F32)
        a4 = _gather_classes(slab, lambda r, j: l4[r, :, 128 * j:128 * (j + 1)], 4, 4)
        p4 = _gather_classes(slab, lambda r, j: o4[r, :, 128 * j:128 * (j + 1)], 4, 4)
        a16 = _gather_classes(slab, lambda r, j: l16[r, :, 128 * j:128 * (j + 1)], 4, 16)
        p16 = _gather_classes(slab, lambda r, j: o16[r, :, 128 * j:128 * (j + 1)], 4, 16)
        m = jnp.maximum(jnp.maximum(a1, a4), a16)
        e1, e4, e16 = jnp.exp(a1 - m), jnp.exp(a4 - m), jnp.exp(a16 - m)
        den = e1 + e4 + e16
        lse = m + jnp.log(den)
        o = (e1 * p1 + e4 * p4 + e16 * p16) / den
        za = za_ref[...].astype(F32)
        sga = _sigmoid(za)
        sa = za * sga
        y = jnp.concatenate([y_conv, o * sa], axis=1).astype(BF16)
        y_o[...] = y

        out = x_ref[...] + jnp.dot(y, w_ref[...], preferred_element_type=F32)
        diff = out - t_ref[...]
        loss_o[...] += (0.5 / D_MODEL) * jnp.sum(diff * diff)
        dout = diff * (1.0 / D_MODEL)
        dout_o[...] = dout
        dy = lax.dot_general(dout.astype(BF16), w_ref[...], (((1,), (1,)), ((), ())), preferred_element_type=F32)
        dyc, dya = dy[:, 0:512], dy[:, 512:1024]

        do = dya * sa
        dza_o[...] = (dya * o * (sga * (1.0 + za * (1.0 - sga)))).astype(BF16)
        lane = lax.broadcasted_iota(jnp.int32, (tm, 512), 1)
        ld = jnp.where((lane & (HEAD_DIM - 1)) < HEAD_DIM // 2, lse, _group_sum(do * o, b_ref))
        do1_o[0] = do.astype(BF16)
        ld1_o[0] = ld
        _to_slabs(slab, do)
        _scatter_classes(slab, 0, 4, do4_o, 4)
        _scatter_classes(slab, 0, 4, do16_o, 16)
        _to_slabs(slab, ld)
        _scatter_classes(slab, 0, 4, ld4_o, 4)
        _scatter_classes(slab, 0, 4, ld16_o, 16)

        dzc = dyc * sz * gb
        dzc_o[...] = dzc
        dgbz_o[:, 0:512] = (dyc * sz * zb).astype(BF16)
        dgbz_o[:, 512:1024] = (dyc * gb * zb * (sg * (1.0 + zc * (1.0 - sg)))).astype(BF16)
        dcb_o[...] += jnp.sum(dzc, axis=0, keepdims=True)
        dcw_o[0:1, :] += jnp.sum(dzc * t_up, axis=0, keepdims=True)
        dcw_o[1:2, :] += jnp.sum(dzc * tt, axis=0, keepdims=True)
        dcw_o[2:3, :] += jnp.sum(dzc * t_dn, axis=0, keepdims=True)

    row = lambda w, j=0: pl.BlockSpec((tm, w), lambda i: (i, j))
    full = lambda shp: pl.BlockSpec(shp, lambda i: (0,) * len(shp))
    prev = pl.BlockSpec((hb, 2048), lambda i: (jnp.maximum(i * (tm // hb) - 1, 0), 0))
    nxt = pl.BlockSpec((hb, 2048), lambda i: (jnp.minimum((i + 1) * (tm // hb), s // hb - 1), 0))
    cls = lambda d, dt: jax.ShapeDtypeStruct((d, s // d, 512), dt)
    cspecs = [_class_spec(d, 512) for d in DILATIONS]
    return pl.pallas_call(
        body, name="combine", grid=(nt,),
        out_shape=(jax.ShapeDtypeStruct((s, 1024), BF16), jax.ShapeDtypeStruct((s, 1024), F32),
                   cls(1, F32), cls(1, BF16), jax.ShapeDtypeStruct((s, 512), BF16),
                   jax.ShapeDtypeStruct((s, 1024), BF16), jax.ShapeDtypeStruct((s, 512), F32),
                   jax.ShapeDtypeStruct((1, 128), F32), jax.ShapeDtypeStruct((1, 512), F32),
                   jax.ShapeDtypeStruct((8, 512), F32),
                   cls(4, BF16), cls(4, F32), cls(16, BF16), cls(16, F32)),
        in_specs=cspecs + cspecs + [row(2048), prev, nxt, row(512, 1), row(1024), row(1024),
                                    full((1024, 1024)), full((8, 512)), full((1, 512)), full((256, 256))],
        out_specs=(row(1024), row(1024), cspecs[0], cspecs[0], row(512), row(1024), row(512),
                   full((1, 128)), full((1, 512)), full((8, 512)),
                   cspecs[1], cspecs[1], cspecs[2], cspecs[2]),
        scratch_shapes=[pltpu.VMEM((4, tm, 128), F32)],
        compiler_params=_params(("arbitrary",)),
    )(*o_g, *lse_g, cg, cg, cg, vz, x, tgt, wout, cw, cb, b256)


def _attn_bwd(qkn_l, v_l, do_l, ld_l, bias, name):
    r_cls, length, _ = qkn_l.shape
    qt = min(length, 1024)
    nb, nsub, nt = length // QB, qt // QB, length // qt

    def body(q_ref, k_ref, v_ref, do_ref, ld_ref, b_ref, dq_ref, dkv_hbm, dsum_ref, dk_acc, dv_acc, sems):
        p_id, r, t = pl.program_id(0), pl.program_id(1), pl.program_id(2)
        lo = lax.broadcasted_iota(jnp.int32, (QB, 128), 1) < HEAD_DIM

        @pl.when(t == 0)
        def _():
            dk_acc[...] = jnp.zeros_like(dk_acc)
            dv_acc[...] = jnp.zeros_like(dv_acc)

        @pl.when((t == 0) & (r == 0))
        def _():
            dsum_ref[...] = jnp.zeros_like(dsum_ref)

        nt_dims = (((1,), (1,)), ((), ()))
        tn_dims = (((0,), (0,)), ((), ()))
        coords, qqs, dds, logits, dps, lcols, dcols = [], [], [], [], [], [], []
        for i in range(nsub):
            q0, start, variant = _block_coords(t, i, nsub, nb, length)
            qq = _split_heads(q_ref[q0:q0 + QB, :], lo)
            dd = _split_heads(do_ref[q0:q0 + QB, :], lo)
            k = k_ref[pl.ds(start, KB), :]
            v = v_ref[pl.ds(start, KB), :]
            logits.append(lax.dot_general(qq, k, nt_dims, preferred_element_type=F32) + b_ref[variant])
            dps.append(lax.dot_general(dd, v, nt_dims, preferred_element_type=F32))
            lcols.append(_col_pair(ld_ref, q0, 0))
            dcols.append(_col_pair(ld_ref, q0, HEAD_DIM // 2))
            coords.append((q0, start, variant))
            qqs.append(qq)
            dds.append(dd)
        p = jnp.exp(jnp.concatenate(logits, axis=0) - jnp.concatenate(lcols, axis=0))
        ds = p * (jnp.concatenate(dps, axis=0) - jnp.concatenate(dcols, axis=0))
        pb = p.astype(BF16)
        dsb = ds.astype(BF16)
        middle = None
        for i in range(nsub):
            q0, start, variant = coords[i]
            rows = slice(2 * QB * i, 2 * QB * (i + 1))
            if 0 < i < nsub - 1:
                middle = ds[rows] if middle is None else middle + ds[rows]
            else:
                dsum_ref[variant] += ds[rows]
            dqq = jnp.dot(dsb[rows], k_ref[pl.ds(start, KB), :], preferred_element_type=F32)
            dq_ref[q0:q0 + QB, :] = jnp.where(lo, dqq[0:QB], dqq[QB:2 * QB]).astype(BF16)
            dk_acc[pl.ds(start, KB), :] += lax.dot_general(dsb[rows], qqs[i], tn_dims, preferred_element_type=F32)
            dv_acc[pl.ds(start, KB), :] += lax.dot_general(pb[rows], dds[i], tn_dims, preferred_element_type=F32)
        if middle is not None:
            dsum_ref[1] += middle

        @pl.when(t == nt - 1)
        def _():
            ck = pltpu.make_async_copy(dk_acc, dkv_hbm.at[r, p_id, 0], sems.at[0])
            cv = pltpu.make_async_copy(dv_acc, dkv_hbm.at[r, p_id, 1], sems.at[1])
            ck.start()
            cv.start()
            ck.wait()
            cv.wait()

    qspec = pl.BlockSpec((None, qt, 128), lambda p, r, t: (r, t, p))
    return pl.pallas_call(
        body, name=name, grid=(N_PAIR, r_cls, nt),
        out_shape=(jax.ShapeDtypeStruct((r_cls, length, 512), BF16),
                   jax.ShapeDtypeStruct((r_cls, N_PAIR, 2, length, 128), F32),
                   jax.ShapeDtypeStruct((N_PAIR, 3, 2 * QB, KB), F32)),
        in_specs=[qspec,
                  pl.BlockSpec((None, length, 128), lambda p, r, t: (r, 0, 4 + p)),
                  pl.BlockSpec((None, length, 128), lambda p, r, t: (r, 0, p)),
                  qspec, qspec,
                  pl.BlockSpec((3, None, 2 * QB, KB), lambda p, r, t: (0, p, 0, 0))],
        out_specs=(qspec, pl.BlockSpec(memory_space=pl.ANY),
                   pl.BlockSpec((None, 3, 2 * QB, KB), lambda p, r, t: (p, 0, 0, 0))),
        scratch_shapes=[pltpu.VMEM((length, 128), F32), pltpu.VMEM((length, 128), F32),
                        pltpu.SemaphoreType.DMA((2,))],
        compiler_params=_params(("arbitrary", "arbitrary", "arbitrary")),
    )(qkn_l, qkn_l, v_l, do_l, ld_l, bias)


def _bwd_tail(dq_g, dkv_g, qkr, qkw, dza, dgbz, dzc, cg, cw, wblk, x, norm_w, dout, b256):
    s = x.shape[0]
    tm = TM
    hb = 8
    nt = s // tm

    def body(dq1, dq4, dq16, dkv1, dkv4, dkv16, qkr_ref, qkw_ref, dza_ref, dgbz_ref, dzc_ref,
             dzp_ref, dzn_ref, cg_ref, cw_ref, w_ref, x_ref, nw_ref, dout_ref, b_ref,
             gx_o, dproj_o, dnw_o, dqkw_o, slab):
        i = pl.program_id(0)

        def nat_q(ref, d):
            return _gather_classes(slab, lambda r, j: ref[r, :, 128 * j:128 * (j + 1)], 4, d)

        def nat_kv(ref, d, which):
            return _gather_classes(slab, lambda r, j: ref[r, j, which], 4, d)

        @pl.when(i == 0)
        def _():
            dnw_o[...] = jnp.zeros_like(dnw_o)
            dqkw_o[...] = jnp.zeros_like(dqkw_o)

        dzc = dzc_ref[...]
        d_prev = jnp.where(i == 0, 0.0, dzp_ref[hb - 1:hb, :])
        d_next = jnp.where(i == nt - 1, 0.0, dzn_ref[0:1, :])
        rows = lax.broadcasted_iota(jnp.int32, (tm, 512), 0)
        d_up = jnp.where(rows == 0, d_prev, pltpu.roll(dzc, 1, 0))
        d_dn = jnp.where(rows == tm - 1, d_next, pltpu.roll(dzc, tm - 1, 0))
        dt = cw_ref[0:1, :] * d_dn + cw_ref[1:2, :] * dzc + cw_ref[2:3, :] * d_up
        u = cg_ref[:, 0:512].astype(F32)
        gc = cg_ref[:, 1024:1536].astype(F32)
        dproj_o[:, 0:512] = (dt * gc).astype(BF16)
        dproj_o[:, 512:1024] = dgbz_ref[:, 0:512]
        dproj_o[:, 1024:1536] = (dt * u).astype(BF16)
        dproj_o[:, 1536:2048] = dgbz_ref[:, 512:1024]

        dqn = (dq1[0].astype(F32) + nat_q(dq4, 4) + nat_q(dq16, 16)) * (1.0 / 8.0)
        dk1 = jnp.concatenate([dkv1[0, j, 0] for j in range(N_PAIR)], axis=1)
        dv1 = jnp.concatenate([dkv1[0, j, 1] for j in range(N_PAIR)], axis=1)
        dkn = dk1 + nat_kv(dkv4, 4, 0) + nat_kv(dkv16, 16, 0)
        dvn = dv1 + nat_kv(dkv4, 4, 1) + nat_kv(dkv16, 16, 1)
        g = jnp.concatenate([dqn, dkn], axis=1) * qkw_ref[...]
        raw = qkr_ref[...].astype(F32)
        rr = lax.rsqrt(_group_sum(raw * raw, b_ref) * (1.0 / HEAD_DIM) + EPS)
        proj_gq = _group_sum(g * raw, b_ref) * (1.0 / HEAD_DIM)
        draw = rr * g - raw * (rr * rr * rr) * proj_gq
        dqkw_o[...] += jnp.sum(jnp.concatenate([dqn, dkn], axis=1) * raw * rr, axis=0, keepdims=True)
        dproj_o[:, 2048:3072] = draw.astype(BF16)
        dproj_o[:, 3072:3584] = dvn.astype(BF16)
        dproj_o[:, 3584:4096] = dza_ref[...]

        nt_dims = (((1,), (1,)), ((), ()))
        dh = lax.dot_general(dproj_o[:, 0:1024], w_ref[0], nt_dims, preferred_element_type=F32)
        for b in range(1, 4):
            dh += lax.dot_general(dproj_o[:, 1024 * b:1024 * b + 1024], w_ref[b], nt_dims,
                                  preferred_element_type=F32)

        xf = x_ref[...]
        r = lax.rsqrt(jnp.mean(xf * xf, axis=-1, keepdims=True) + EPS)
        gh = dh * nw_ref[...]
        dnw_o[...] += jnp.sum(dh * xf * r, axis=0, keepdims=True)
        mean_gx = jnp.mean(gh * xf, axis=-1, keepdims=True)
        gx_o[...] = dout_ref[...] + r * gh - xf * (r * r * r) * mean_gx

    row = lambda w, j=0: pl.BlockSpec((tm, w), lambda i: (i, j))
    full = lambda shp: pl.BlockSpec(shp, lambda i: (0,) * len(shp))
    prev = pl.BlockSpec((hb, 512), lambda i: (jnp.maximum(i * (tm // hb) - 1, 0), 0))
    nxt = pl.BlockSpec((hb, 512), lambda i: (jnp.minimum((i + 1) * (tm // hb), s // hb - 1), 0))
    return pl.pallas_call(
        body, name="bwd_tail", grid=(nt,),
        out_shape=(jax.ShapeDtypeStruct((s, 1024), F32), jax.ShapeDtypeStruct((s, 4096), BF16),
                   jax.ShapeDtypeStruct((1, 1024), F32), jax.ShapeDtypeStruct((1, 1024), F32)),
        in_specs=[_class_spec(d, 512) for d in DILATIONS]
        + [pl.BlockSpec((d, N_PAIR, 2, tm // d, 128), lambda i: (0, 0, 0, i, 0)) for d in DILATIONS]
        + [row(1024), full((1, 1024)), row(512), row(1024), row(512), prev, nxt,
           row(2048), full((8, 512)), full((4, 1024, 1024)), row(1024),
           full((1, 1024)), row(1024), full((256, 256))],
        out_specs=(row(1024), row(4096), full((1, 1024)), full((1, 1024))),
        scratch_shapes=[pltpu.VMEM((4, tm, 128), F32)],
        compiler_params=_params(("arbitrary",)),
    )(*dq_g, *dkv_g, qkr, qkw, dza, dgbz, dzc, dzc, dzc, cg, cw, wblk, x, norm_w, dout, b256)


def _wgrad(a, b, row_blocked, name):
    s, m = a.shape
    n = b.shape[1]
    tk = 512
    nj = n // 1024

    def body(a_ref, b_ref, o_ref, acc):
        kk = pl.program_id(1)

        @pl.when(kk == 0)
        def _():
            acc[...] = jnp.zeros_like(acc)

        acc[...] += lax.dot_general(a_ref[...], b_ref[...].astype(BF16), (((0,), (0,)), ((), ())),
                                    preferred_element_type=F32)

        @pl.when(kk == pl.num_programs(1) - 1)
        def _():
            o_ref[...] = acc[...].astype(BF16).reshape(o_ref.shape)

    if row_blocked:
        out_shape = jax.ShapeDtypeStruct((4, 2, m // 8, 1024), BF16)
        out_spec = pl.BlockSpec((4, 2, m // 8, 1024), lambda j, k: (0, 0, 0, 0))
    else:
        out_shape = jax.ShapeDtypeStruct((nj, 2, m // 2, 1024), BF16)
        out_spec = pl.BlockSpec((None, 2, m // 2, 1024), lambda j, k: (j, 0, 0, 0))
    return pl.pallas_call(
        body, name=name, grid=(nj, s // tk),
        out_shape=out_shape,
        in_specs=[pl.BlockSpec((tk, m), lambda j, k: (k, 0)), pl.BlockSpec((tk, 1024), lambda j, k: (k, j))],
        out_specs=out_spec,
        scratch_shapes=[pltpu.VMEM((m, 1024), F32)],
        compiler_params=_params(("parallel", "arbitrary")),
    )(a, b)


def _dbias(dsum_all, onehot_all):
    def body(ds_ref, oh_ref, o_ref):
        step = pl.program_id(0) * 3 + pl.program_id(1)

        @pl.when(step == 0)
        def _():
            o_ref[...] = jnp.zeros_like(o_ref)

        rowq = lax.broadcasted_iota(jnp.int32, (2 * QB, KB), 0) & (QB - 1)
        hrow = lax.broadcasted_iota(jnp.int32, (8, KB), 0)
        diag = jnp.zeros((8, KB), F32)
        for p in range(N_PAIR):
            y = ds_ref[p]
            for bit in range(7):
                sh = 1 << bit
                y = jnp.where((rowq & sh) != 0, pltpu.roll(y, KB - sh, 1), y)
            da = jnp.sum(y[0:QB], axis=0, keepdims=True)
            db = jnp.sum(y[QB:2 * QB], axis=0, keepdims=True)
            diag = jnp.where(hrow == 2 * p, da, diag)
            diag = jnp.where(hrow == 2 * p + 1, db, diag)
        o_ref[...] += jnp.dot(diag, oh_ref[...], preferred_element_type=F32, precision=lax.Precision.HIGHEST)

    return pl.pallas_call(
        body, name="dbias", grid=(3, 3),
        out_shape=jax.ShapeDtypeStruct((8, 128), F32),
        in_specs=[pl.BlockSpec((None, N_PAIR, None, 2 * QB, KB), lambda g, v: (g, 0, v, 0, 0)),
                  pl.BlockSpec((None, None, KB, 128), lambda g, v: (g, v, 0, 0))],
        out_specs=pl.BlockSpec((8, 128), lambda g, v: (0, 0)),
        compiler_params=_params(("arbitrary", "arbitrary")),
    )(dsum_all, onehot_all)


def _gsync(pw_in, pw_out, small):
    hin, hout = pw_in.shape[2], pw_out.shape[2]
    nsmall = small.shape[0]

    def body(pin_hbm, pout_hbm, small_ref, gin_o, gout_o, small_o,
             mine_in, recv_in, sbuf_in, rbuf_in, mine_out, recv_out, sbuf_out, rbuf_out, gather,
             lsem, asend, arecv, bsend, brecv, csend, crecv, ssend, srecv):
        x, y, c = lax.axis_index("x"), lax.axis_index("y"), lax.axis_index("c")
        b = 2 * x + y
        dev = 4 * x + 2 * y + c
        sib = (x, y, 1 - c)

        def rcopy(src, dst, ssem, rsem, to):
            return pltpu.make_async_remote_copy(src_ref=src, dst_ref=dst, send_sem=ssem, recv_sem=rsem,
                                                device_id=to, device_id_type=MESH)

        gather[dev] = small_ref[...]
        s_sends = []
        for k in range(1, 8):
            to = (x ^ (k >> 2), y ^ ((k >> 1) & 1), c ^ (k & 1))
            cp = rcopy(gather.at[dev], gather.at[dev], ssend.at[k - 1], srecv.at[k - 1], to)
            cp.start()
            s_sends.append(cp)

        a_in = rcopy(pin_hbm.at[:, 1 - c], recv_in, asend.at[0], arecv.at[0], sib)
        a_out = rcopy(pout_hbm.at[:, 1 - c], recv_out, asend.at[1], arecv.at[1], sib)
        a_in.start()
        a_out.start()
        l_in = pltpu.make_async_copy(pin_hbm.at[:, c], mine_in, lsem.at[0])
        l_out = pltpu.make_async_copy(pout_hbm.at[:, c], mine_out, lsem.at[1])
        l_in.start()
        l_out.start()
        l_in.wait()
        l_out.wait()

        def stage_b(a_cp, mine, recv, sbuf, rbuf, base):
            a_cp.wait_recv()
            sends = []
            for k in (1, 2, 3):
                bk = b ^ k
                sbuf[k - 1] = (mine[bk].astype(F32) + recv[bk].astype(F32)).astype(BF16)
                cp = rcopy(sbuf.at[k - 1], rbuf.at[k - 1], bsend.at[base + k - 1], brecv.at[base + k - 1],
                           (x ^ (k >> 1), y ^ (k & 1), c))
                cp.start()
                sends.append(cp)
            return sends

        b_in = stage_b(a_in, mine_in, recv_in, sbuf_in, rbuf_in, 0)
        b_out = stage_b(a_out, mine_out, recv_out, sbuf_out, rbuf_out, 3)

        def stage_c(b_sends, mine, recv, rbuf, g_o, half, idx):
            acc = mine[b].astype(F32) + recv[b].astype(F32)
            for k in (1, 2, 3):
                b_sends[k - 1].wait_recv()
                acc = acc + rbuf[k - 1].astype(F32)
            rows = g_o.at[pl.ds(pl.multiple_of(c * half, half), half), :]
            g_o[pl.ds(pl.multiple_of(c * half, half), half), :] = acc
            cp = rcopy(rows, rows, csend.at[idx], crecv.at[idx], sib)
            cp.start()
            return cp

        c_in = stage_c(b_in, mine_in, recv_in, rbuf_in, gin_o, hin, 0)
        c_out = stage_c(b_out, mine_out, recv_out, rbuf_out, gout_o, hout, 1)

        for cp in s_sends:
            cp.wait_recv()
        tot = gather[0]
        for d in range(1, 8):
            tot = tot + gather[d]
        small_o[...] = tot

        for g_o, half, idx in ((gin_o, hin, 0), (gout_o, hout, 1)):
            other = g_o.at[pl.ds(pl.multiple_of((1 - c) * half, half), half), :]
            rcopy(other, other, csend.at[idx], crecv.at[idx], sib).wait_recv()
        for cp in s_sends + [a_in, a_out] + b_in + b_out + [c_in, c_out]:
            cp.wait_send()

    vm = pl.BlockSpec(memory_space=pltpu.VMEM)
    hbm = pl.BlockSpec(memory_space=pl.ANY)
    return pl.pallas_call(
        body, name="gsync",
        out_shape=(jax.ShapeDtypeStruct((2 * hin, 1024), F32), jax.ShapeDtypeStruct((2 * hout, 1024), F32),
                   jax.ShapeDtypeStruct((nsmall, 128), F32)),
        in_specs=[hbm, hbm, vm], out_specs=(vm, vm, vm),
        scratch_shapes=[pltpu.VMEM((4, hin, 1024), BF16), pltpu.VMEM((4, hin, 1024), BF16),
                        pltpu.VMEM((3, hin, 1024), BF16), pltpu.VMEM((3, hin, 1024), BF16),
                        pltpu.VMEM((4, hout, 1024), BF16), pltpu.VMEM((4, hout, 1024), BF16),
                        pltpu.VMEM((3, hout, 1024), BF16), pltpu.VMEM((3, hout, 1024), BF16),
                        pltpu.VMEM((8, nsmall, 128), F32),
                        pltpu.SemaphoreType.DMA((2,)),
                        pltpu.SemaphoreType.DMA((2,)), pltpu.SemaphoreType.DMA((2,)),
                        pltpu.SemaphoreType.DMA((6,)), pltpu.SemaphoreType.DMA((6,)),
                        pltpu.SemaphoreType.DMA((2,)), pltpu.SemaphoreType.DMA((2,)),
                        pltpu.SemaphoreType.DMA((7,)), pltpu.SemaphoreType.DMA((7,))],
        compiler_params=_params(),
    )(pw_in, pw_out, small)


def _adamw_math(w, g, m, v):
    m = ADAM_B1 * m + (1.0 - ADAM_B1) * g
    v = ADAM_B2 * v + (1.0 - ADAM_B2) * (g * g)
    m_hat = m / (1.0 - ADAM_B1 ** ADAM_STEP)
    v_hat = v / (1.0 - ADAM_B2 ** ADAM_STEP)
    delta = -ADAM_LR * (m_hat / (jnp.sqrt(v_hat) + ADAM_EPS) + ADAM_WD * w)
    return delta, m, v


def _adamw(w, g, m, v, name):
    rows, cols = w.shape
    tr = 256 if rows % 256 == 0 else rows

    def body(w_ref, g_ref, m_ref, v_ref, d_o, m_o, v_o):
        d, m2, v2 = _adamw_math(w_ref[...], g_ref[...], m_ref[...], v_ref[...])
        d_o[...] = d
        m_o[...] = m2
        v_o[...] = v2

    spec = pl.BlockSpec((tr, cols), lambda i: (i, 0))
    shp = jax.ShapeDtypeStruct((rows, cols), F32)
    return pl.pallas_call(
        body, name=name, grid=(rows // tr,), out_shape=(shp, shp, shp),
        in_specs=[spec] * 4, out_specs=(spec, spec, spec),
        compiler_params=_params(("parallel",)),
    )(w, g, m, v)


def _fold_heads(dqkw):
    def body(x_ref, o_ref):
        xs = x_ref[...]
        sq = xs[0:1] + xs[1:2] + xs[2:3] + xs[3:4]
        sk = xs[4:5] + xs[5:6] + xs[6:7] + xs[7:8]
        both = jnp.concatenate([sq, sk], axis=0)
        o_ref[...] = both + pltpu.roll(both, HEAD_DIM, 1)

    vm = pl.BlockSpec(memory_space=pltpu.VMEM)
    return pl.pallas_call(body, name="fold_heads", out_shape=jax.ShapeDtypeStruct((2, 128), F32),
                          in_specs=[vm], out_specs=vm, compiler_params=_params())(dqkw)


def kernel(x, norm_w, w_in, conv_w, conv_b, q_norm_w, k_norm_w, rel_bias, w_out, loss_target, m_norm_w, m_w_in, m_conv_w, m_conv_b, m_q_norm_w, m_k_norm_w, m_rel_bias, m_w_out, v_norm_w, v_w_in, v_conv_w, v_conv_b, v_q_norm_w, v_k_norm_w, v_rel_bias, v_w_out):
    x2 = x[0]
    tgt = loss_target[0]
    blk = 2 * lax.axis_index("x") + lax.axis_index("y")

    conv_w8 = jnp.pad(conv_w, ((0, 5), (0, 0)))
    wblk, woutblk, cwblk = _wgather(w_in, w_out, conv_w8)
    wout_full = woutblk.reshape(1024, 1024)
    cw_full = cwblk.transpose(1, 0, 2).reshape(8, 512)

    qkw = jnp.concatenate([jnp.tile(q_norm_w, 8) * 0.125, jnp.tile(k_norm_w, 8)])[None, :]
    qkw_raw = jnp.concatenate([jnp.tile(q_norm_w, 8), jnp.tile(k_norm_w, 8)])[None, :]
    gidx = jnp.arange(256) // HEAD_DIM
    b256 = (gidx[:, None] == gidx[None, :]).astype(BF16)

    h, cg, qkr, qkn, vz, qkn4, v4, qkn16, v16 = _proj(x2, norm_w[None, :], wblk, qkw, b256)

    biases = _bias_tables(rel_bias)
    qkn_l = [qkn[None], qkn4, qkn16]
    v_l = [vz[None], v4, v16]
    o_g, lse_g = [], []
    for gi, d in enumerate(DILATIONS):
        o_l, lse_l = _attn_fwd(qkn_l[gi], v_l[gi], biases[gi], f"attn_fwd_d{d}")
        o_g.append(o_l)
        lse_g.append(lse_l)

    (y, dout, ld1, do1, dza, dgbz, dzc, loss_p, dcb, dcw, do4, ld4, do16, ld16) = _combine(
        o_g, lse_g, cg, vz, x2, tgt, wout_full, cw_full, conv_b[None, :], b256)

    dq_g, dkv_g, dsums = [], [], []
    for gi, (d, do_l, ld_l) in enumerate(zip(DILATIONS, (do1, do4, do16), (ld1, ld4, ld16))):
        dq_l, dkv_l, dsum = _attn_bwd(qkn_l[gi], v_l[gi], do_l, ld_l, biases[gi], f"attn_bwd_d{d}")
        dq_g.append(dq_l)
        dkv_g.append(dkv_l)
        dsums.append(dsum)

    grad_x, dproj, dnw, dqkw = _bwd_tail(dq_g, dkv_g, qkr, qkw_raw, dza, dgbz, dzc, cg, cw_full, wblk,
                                         x2, norm_w[None, :], dout, b256)

    pw_in = _wgrad(h, dproj, False, "wgrad_in")
    pw_out = _wgrad(y, dout, True, "wgrad_out")
    dbias8 = _dbias(jnp.stack(dsums, axis=0), jnp.stack([_diag_bucket_onehot(d) for d in DILATIONS], axis=0))

    small = jnp.concatenate([dnw.reshape(8, 128), dcb.reshape(4, 128), dqkw.reshape(8, 128),
                             dcw[0:3].reshape(12, 128), dbias8], axis=0)
    g_win, g_wout, gsmall = _gsync(pw_in, pw_out, small)

    g_nw = gsmall[0:8].reshape(1024)
    g_cb = gsmall[8:12].reshape(512)
    folded = _fold_heads(gsmall[12:20])
    g_qw, g_kw = folded[0, 0:64], folded[1, 0:64]
    g_cw = lax.dynamic_slice(gsmall[20:32].reshape(3, 512), (0, blk * 128), (3, 128))
    g_rb = gsmall[32:40][:, 0:32].T

    loss = lax.psum(loss_p[0, 0], ("x", "y", "c"))

    d_win, nm_win, nv_win = _adamw(w_in, g_win, m_w_in, v_w_in, "adamw_w_in")
    d_wout, nm_wout, nv_wout = _adamw(w_out, g_wout, m_w_out, v_w_out, "adamw_w_out")

    def pack(parts):
        rows = [parts[0].reshape(8, 128), parts[1].reshape(4, 128),
                jnp.pad(parts[2], (0, 64))[None, :], jnp.pad(parts[3], (0, 64))[None, :],
                parts[4], jnp.pad(parts[5].T, ((0, 0), (0, 96)))]
        return jnp.concatenate(rows, axis=0)

    ws = pack([norm_w, conv_b, q_norm_w, k_norm_w, conv_w, rel_bias])
    gs = pack([g_nw, g_cb, g_qw, g_kw, g_cw, g_rb])
    ms = pack([m_norm_w, m_conv_b, m_q_norm_w, m_k_norm_w, m_conv_w, m_rel_bias])
    vs = pack([v_norm_w, v_conv_b, v_q_norm_w, v_k_norm_w, v_conv_w, v_rel_bias])
    rpad = lambda a: jnp.pad(a, ((0, 7), (0, 0)))
    d_s, nm_s, nv_s = _adamw(rpad(ws), rpad(gs), rpad(ms), rpad(vs), "adamw_small")

    def unpack(a):
        return (a[0:8].reshape(1024), a[12:13, 0:64].reshape(64), a[13:14, 0:64].reshape(64),
                a[14:17], a[8:12].reshape(512), a[17:25, 0:32].T)

    def ordered(nw, win, cw, cb, qw, kw, rb, wout):
        return (nw, win, cw, cb, qw, kw, rb, wout)

    g_un = (g_nw, g_qw, g_kw, g_cw, g_cb, g_rb)
    outs = [loss, grad_x[None]]
    for un, win_v, wout_v in ((g_un, g_win, g_wout), (unpack(d_s), d_win, d_wout),
                              (unpack(nm_s), nm_win, nm_wout), (unpack(nv_s), nv_win, nv_wout)):
        nw, qw, kw, cw, cb, rb = un
        outs.extend(ordered(nw, win_v, cw, cb, qw, kw, rb, wout_v))
    return tuple(outs)
```

```python
import math

import jax
import jax.numpy as jnp
from jax import lax
from jax.experimental import pallas as pl
from jax.experimental.pallas import tpu as pltpu

F32 = jnp.float32
BF16 = jnp.bfloat16
MESH = pl.DeviceIdType.MESH

D_MODEL = 1024
CONV_W = 512
ATTN_W = 512
HEAD_DIM = 64
N_PAIR = 4
DILATIONS = (1, 4, 16)
HALF = 64
QB = 128
KB = QB + 2 * HALF
NUM_BUCKETS = 32
MAX_DISTANCE = 1024
EPS = 1e-6
NEG = -1e30
ADAM_LR, ADAM_B1, ADAM_B2, ADAM_EPS, ADAM_WD, ADAM_STEP = 0.001, 0.9, 0.999, 1e-08, 0.01, 10
VMEM_LIMIT = 48 << 20


def _params(sem=None, vmem=VMEM_LIMIT, **kw):
    if sem is not None:
        kw["dimension_semantics"] = sem
    return pltpu.CompilerParams(vmem_limit_bytes=vmem, **kw)


def _sigmoid(z):
    return 1.0 / (1.0 + jnp.exp(-z))


def _group_sum(val, b_ref, split=True):
    hi = val.astype(BF16)
    lo = (val - hi.astype(F32)).astype(BF16) if split else None
    outs = []
    for j in range(val.shape[1] // 256):
        sl = slice(256 * j, 256 * j + 256)
        part = jnp.dot(hi[:, sl], b_ref[...], preferred_element_type=F32)
        if split:
            part = part + jnp.dot(lo[:, sl], b_ref[...], preferred_element_type=F32)
        outs.append(part)
    return outs[0] if len(outs) == 1 else jnp.concatenate(outs, axis=1)


def _t5_bucket(rel):
    half_b = NUM_BUCKETS // 2
    max_exact = half_b // 2
    ret = jnp.where(rel > 0, half_b, 0)
    n = jnp.abs(rel)
    nf = jnp.maximum(n, 1).astype(F32)
    large = max_exact + (jnp.log(nf / max_exact) / math.log(MAX_DISTANCE / max_exact)
                         * (half_b - max_exact)).astype(jnp.int32)
    large = jnp.minimum(large, half_b - 1)
    return ret + jnp.where(n < max_exact, n, large)


def _window_rel(variant):
    off = (0, HALF, 2 * HALF)[variant]
    return jnp.arange(KB)[None, :] - off - jnp.arange(QB)[:, None]


def _bias_tables(rel_bias):
    bkts = []
    for dilation in DILATIONS:
        for variant in range(3):
            rel = _window_rel(variant)
            bkt = _t5_bucket(jnp.clip(rel, -HALF, HALF) * dilation)
            bkts.append(jnp.where(jnp.abs(rel) <= HALF, bkt, -1))
    bkt_all = jnp.stack(bkts, axis=0).astype(jnp.int32)

    def body(rb_ref, bkt_ref, o_ref):
        bkt = bkt_ref[...]
        for h in range(8):
            acc = jnp.full((QB, KB), NEG, F32)
            for b in range(NUM_BUCKETS):
                acc = jnp.where(bkt == b, rb_ref[b, h], acc)
            o_ref[h] = acc

    out = pl.pallas_call(
        body, name="bias_tables", grid=(9,),
        out_shape=jax.ShapeDtypeStruct((9, 8, QB, KB), F32),
        in_specs=[pl.BlockSpec(memory_space=pltpu.SMEM), pl.BlockSpec((None, QB, KB), lambda i: (i, 0, 0))],
        out_specs=pl.BlockSpec((None, 8, QB, KB), lambda i: (i, 0, 0, 0)),
        compiler_params=_params(("parallel",)),
    )(rel_bias, bkt_all)
    return out.reshape(3, 3, N_PAIR, 2 * QB, KB)


def _diag_bucket_onehot(dilation):
    out = []
    c = jnp.arange(KB)
    for variant in range(3):
        off = (0, HALF, 2 * HALF)[variant]
        rel = ((c - off + 128) % 256) - 128
        band = jnp.abs(rel) <= HALF
        bkt = _t5_bucket(jnp.clip(rel, -HALF, HALF) * dilation)
        oh = (bkt[:, None] == jnp.arange(128)[None, :]) & band[:, None]
        out.append(oh.astype(F32))
    return jnp.stack(out, axis=0)


def _wgather(w_in, w_out, conv_w):
    rin, rout = w_in.shape[0] // 2, w_out.shape[0] // 2

    def body(win_ref, wout_ref, cw_ref, win_o, wout_o, cw_o, send_sems, recv_sems):
        x, y, c = lax.axis_index("x"), lax.axis_index("y"), lax.axis_index("c")
        b = 2 * x + y
        win_o[b] = win_ref[...].astype(BF16)
        wout_o[b] = wout_ref[...].astype(BF16)
        cw_o[b] = cw_ref[...]

        def peer(k):
            return (x ^ (k >> 1), y ^ (k & 1))

        def piece(ref, blk, half_rows, core):
            return ref.at[blk, pl.ds(core * half_rows, half_rows), :]

        def copy(sem, src, dst, to):
            return pltpu.make_async_remote_copy(src_ref=src, dst_ref=dst, send_sem=send_sems.at[sem],
                                                recv_sem=recv_sems.at[sem], device_id=to, device_id_type=MESH)

        sends = []
        for k in (1, 2, 3):
            px, py = peer(k)
            sends.append(copy(k - 1, piece(win_o, b, rin, c), piece(win_o, b, rin, c), (px, py, c)))
            sends.append(copy(3 + k - 1, piece(wout_o, b, rout, c), piece(wout_o, b, rout, c), (px, py, c)))
            sends.append(copy(6 + k - 1, cw_o.at[b], cw_o.at[b], (px, py, c)))
        for cp in sends:
            cp.start()
        fwd = []
        for k in (1, 2, 3):
            px, py = peer(k)
            bk = 2 * px + py
            copy(k - 1, piece(win_o, bk, rin, c), piece(win_o, bk, rin, c), (px, py, c)).wait_recv()
            f = copy(9 + k - 1, piece(win_o, bk, rin, c), piece(win_o, bk, rin, c), (x, y, 1 - c))
            f.start()
            fwd.append(f)
            copy(3 + k - 1, piece(wout_o, bk, rout, c), piece(wout_o, bk, rout, c), (px, py, c)).wait_recv()
            f = copy(12 + k - 1, piece(wout_o, bk, rout, c), piece(wout_o, bk, rout, c), (x, y, 1 - c))
            f.start()
            fwd.append(f)
            copy(6 + k - 1, cw_o.at[bk], cw_o.at[bk], (px, py, c)).wait_recv()
        for k in (1, 2, 3):
            px, py = peer(k)
            bk = 2 * px + py
            copy(9 + k - 1, piece(win_o, bk, rin, 1 - c), piece(win_o, bk, rin, 1 - c), (x, y, 1 - c)).wait_recv()
            copy(12 + k - 1, piece(wout_o, bk, rout, 1 - c), piece(wout_o, bk, rout, 1 - c), (x, y, 1 - c)).wait_recv()
        for cp in sends + fwd:
            cp.wait_send()

    vm = pl.BlockSpec(memory_space=pltpu.VMEM)
    return pl.pallas_call(
        body, name="wgather",
        out_shape=(jax.ShapeDtypeStruct((4,) + w_in.shape, BF16),
                   jax.ShapeDtypeStruct((4,) + w_out.shape, BF16),
                   jax.ShapeDtypeStruct((4,) + conv_w.shape, F32)),
        in_specs=[vm, vm, vm], out_specs=(vm, vm, vm),
        scratch_shapes=[pltpu.SemaphoreType.DMA((15,)), pltpu.SemaphoreType.DMA((15,))],
        compiler_params=_params(),
    )(w_in, w_out, conv_w)


TM = 256


def _to_slabs(slab, val, j0=0):
    for j in range(val.shape[1] // 128):
        slab[j0 + j] = val[:, 128 * j:128 * (j + 1)]


def _scatter_classes(slab, j0, nj, out_ref, d):
    n = TM // d
    for r in range(d):
        for j in range(nj):
            out_ref[r, :, 128 * j:128 * (j + 1)] = slab[j0 + j, pl.ds(r, n, stride=d), :].astype(out_ref.dtype)


def _gather_classes(slab, piece, nj, d):
    n = TM // d
    for r in range(d):
        for j in range(nj):
            slab[j, pl.ds(r, n, stride=d), :] = piece(r, j).astype(F32)
    return jnp.concatenate([slab[j] for j in range(nj)], axis=1)


def _class_spec(d, width):
    return pl.BlockSpec((d, TM // d, width), lambda i: (0, i, 0))


def _proj(x, norm_w, wblk, qkw, b256):
    s = x.shape[0]

    def body(x_ref, nw_ref, w_ref, qkw_ref, b_ref, h_o, cg_o, qkr_o, qkn_o, vz_o, qkn4_o, v4_o, qkn16_o, v16_o, slab):
        xf = x_ref[...]
        r = lax.rsqrt(jnp.mean(xf * xf, axis=-1, keepdims=True) + EPS)
        h = (xf * r * nw_ref[...]).astype(BF16)
        h_o[...] = h
        cg_o[:, 0:1024] = jnp.dot(h, w_ref[0], preferred_element_type=F32).astype(BF16)
        cg_o[:, 1024:2048] = jnp.dot(h, w_ref[1], preferred_element_type=F32).astype(BF16)
        p2 = jnp.dot(h, w_ref[2], preferred_element_type=F32)
        qkr_o[...] = p2.astype(BF16)
        ss = _group_sum(p2 * p2, b_ref, split=False)
        rr = lax.rsqrt(ss * (1.0 / HEAD_DIM) + EPS)
        qkn = p2 * rr * qkw_ref[...]
        qkn_o[...] = qkn.astype(BF16)
        p3 = jnp.dot(h, w_ref[3], preferred_element_type=F32)
        vz_o[...] = p3.astype(BF16)
        _to_slabs(slab, qkn)
        _to_slabs(slab, p3[:, 0:512], 8)
        for d, q_o, v_o in ((4, qkn4_o, v4_o), (16, qkn16_o, v16_o)):
            _scatter_classes(slab, 0, 8, q_o, d)
            _scatter_classes(slab, 8, 4, v_o, d)

    row = lambda w: pl.BlockSpec((TM, w), lambda i: (i, 0))
    full = lambda shp: pl.BlockSpec(shp, lambda i: (0,) * len(shp))
    nat = lambda w: jax.ShapeDtypeStruct((s, w), BF16)
    cls = lambda d, w: jax.ShapeDtypeStruct((d, s // d, w), BF16)
    return pl.pallas_call(
        body, name="proj", grid=(s // TM,),
        out_shape=(nat(1024), nat(2048), nat(1024), nat(1024), nat(1024),
                   cls(4, 1024), cls(4, 512), cls(16, 1024), cls(16, 512)),
        in_specs=[row(1024), full((1, 1024)), full((4, 1024, 1024)), full((1, 1024)), full((256, 256))],
        out_specs=(row(1024), row(2048), row(1024), row(1024), row(1024),
                   _class_spec(4, 1024), _class_spec(4, 512), _class_spec(16, 1024), _class_spec(16, 512)),
        scratch_shapes=[pltpu.VMEM((12, TM, 128), F32)],
        compiler_params=_params(("parallel",)),
    )(x, norm_w, wblk, qkw, b256)


def _block_coords(t, i, nsub, nb, length):
    n = t * nsub + i
    q0 = i * QB
    start = pl.multiple_of(jnp.clip(n * QB - HALF, 0, length - KB), HALF)
    variant = jnp.where(n == 0, 0, jnp.where(n == nb - 1, 2, 1))
    return q0, start, variant


def _split_heads(a, lo):
    zero = jnp.zeros_like(a)
    return jnp.concatenate([jnp.where(lo, a, zero), jnp.where(lo, zero, a)], axis=0)


def _col_pair(ref, q0, lane):
    return jnp.concatenate([ref[pl.ds(q0, QB), lane:lane + 1],
                            ref[pl.ds(q0, QB), HEAD_DIM + lane:HEAD_DIM + lane + 1]], axis=0)


def _attn_fwd(qkn_l, v_l, bias, name):
    r_cls, length, _ = qkn_l.shape
    qt = min(length, 1024)
    nb, nsub = length // QB, qt // QB

    def body(q_ref, k_ref, v_ref, b_ref, o_ref, lse_ref):
        t = pl.program_id(2)
        lo = lax.broadcasted_iota(jnp.int32, (QB, 128), 1) < HEAD_DIM

        starts, logits = [], []
        for i in range(nsub):
            _, start, variant = _block_coords(t, i, nsub, nb, length)
            qq = _split_heads(q_ref[i * QB:(i + 1) * QB, :], lo)
            k = k_ref[pl.ds(start, KB), :]
            logits.append(lax.dot_general(qq, k, (((1,), (1,)), ((), ())), preferred_element_type=F32)
                          + b_ref[variant])
            starts.append(start)
        lg = jnp.concatenate(logits, axis=0)
        m = jnp.max(lg, axis=-1, keepdims=True)
        pb = jnp.exp(lg - m).astype(BF16)
        l = jnp.dot(pb, jnp.ones((KB, 128), BF16), preferred_element_type=F32)
        lse = m + jnp.log(l)
        inv = 1.0 / l
        for i in range(nsub):
            rows = slice(2 * QB * i, 2 * QB * (i + 1))
            v = v_ref[pl.ds(starts[i], KB), :]
            pv = jnp.dot(pb[rows], v, preferred_element_type=F32) * inv[rows]
            o_ref[i * QB:(i + 1) * QB, :] = jnp.where(lo, pv[0:QB], pv[QB:2 * QB]).astype(BF16)
            ls = lse[rows]
            lse_ref[i * QB:(i + 1) * QB, :] = jnp.where(lo, ls[0:QB], ls[QB:2 * QB])

    return pl.pallas_call(
        body, name=name, grid=(r_cls, N_PAIR, length // qt),
        out_shape=(jax.ShapeDtypeStruct((r_cls, length, 512), BF16),
                   jax.ShapeDtypeStruct((r_cls, length, 512), F32)),
        in_specs=[pl.BlockSpec((None, qt, 128), lambda r, p, t: (r, t, p)),
                  pl.BlockSpec((None, length, 128), lambda r, p, t: (r, 0, 4 + p)),
                  pl.BlockSpec((None, length, 128), lambda r, p, t: (r, 0, p)),
                  pl.BlockSpec((3, None, 2 * QB, KB), lambda r, p, t: (0, p, 0, 0))],
        out_specs=(pl.BlockSpec((None, qt, 128), lambda r, p, t: (r, t, p)),
                   pl.BlockSpec((None, qt, 128), lambda r, p, t: (r, t, p))),
        compiler_params=_params(("parallel", "parallel", "arbitrary")),
    )(qkn_l, qkn_l, v_l, bias)


def _combine(o_g, lse_g, cg, vz, x, tgt, wout, cw, cb, b256):
    s = x.shape[0]
    tm = TM
    hb = 16
    nt = s // tm

    def body(o1, o4, o16, l1, l4, l16, cg_ref, cgp_ref, cgn_ref, za_ref, x_ref, t_ref, w_ref, cw_ref, cb_ref,
             b_ref, y_o, dout_o, ld1_o, do1_o, dza_o, dgbz_o, dzc_o, loss_o, dcb_o, dcw_o,
             do4_o, ld4_o, do16_o, ld16_o, slab):
        i = pl.program_id(0)

        @pl.when(i == 0)
        def _():
            loss_o[...] = jnp.zeros_like(loss_o)
            dcb_o[...] = jnp.zeros_like(dcb_o)
            dcw_o[...] = jnp.zeros_like(dcw_o)

        u = cg_ref[:, 0:512].astype(F32)
        gb = cg_ref[:, 512:1024].astype(F32)
        gc = cg_ref[:, 1024:1536].astype(F32)
        zc = cg_ref[:, 1536:2048].astype(F32)
        tt = gc * u
        t_prev = cgp_ref[hb - 1:hb, 0:512].astype(F32) * cgp_ref[hb - 1:hb, 1024:1536].astype(F32)
        t_next = cgn_ref[0:1, 0:512].astype(F32) * cgn_ref[0:1, 1024:1536].astype(F32)
        t_prev = jnp.where(i == 0, 0.0, t_prev)
        t_next = jnp.where(i == nt - 1, 0.0, t_next)
        rows = lax.broadcasted_iota(jnp.int32, (tm, 512), 0)
        t_up = jnp.where(rows == 0, t_prev, pltpu.roll(tt, 1, 0))
        t_dn = jnp.where(rows == tm - 1, t_next, pltpu.roll(tt, tm - 1, 0))
        w0, w1, w2 = cw_ref[0:1, :], cw_ref[1:2, :], cw_ref[2:3, :]
        zb = w0 * t_up + w1 * tt + w2 * t_dn + cb_ref[...]
        sg = _sigmoid(zc)
        sz = zc * sg
        y_conv = gb * zb * sz

        a1, p1 = l1[0], o1[0].astype(F32)
        a4 = _gather_classes(slab, lambda r, j: l4[r, :, 128 * j:128 * (j + 1)], 4, 4)
        p4 = _gather_classes(slab, lambda r, j: o4[r, :, 128 * j:128 * (j + 1)], 4, 4)
        a16 = _gather_classes(slab, lambda r, j: l16[r, :, 128 * j:128 * (j + 1)], 4, 16)
        p16 = _gather_classes(slab, lambda r, j: o16[r, :, 128 * j:128 * (j + 1)], 4, 16)
        m = jnp.maximum(jnp.maximum(a1, a4), a16)
        e1, e4, e16 = jnp.exp(a1 - m), jnp.exp(a4 - m), jnp.exp(a16 - m)
        den = e1 + e4 + e16
        lse = m + jnp.log(den)
        o = (e1 * p1 + e4 * p4 + e16 * p16) / den
        za = za_ref[...].astype(F32)
        sga = _sigmoid(za)
        sa = za * sga
        y = jnp.concatenate([y_conv, o * sa], axis=1).astype(BF16)
        y_o[...] = y

        out = x_ref[...] + jnp.dot(y, w_ref[...], preferred_element_type=F32)
        diff = out - t_ref[...]
        loss_o[...] += (0.5 / D_MODEL) * jnp.sum(diff * diff)
        dout = diff * (1.0 / D_MODEL)
        dout_o[...] = dout
        dy = lax.dot_general(dout.astype(BF16), w_ref[...], (((1,), (1,)), ((), ())), preferred_element_type=F32)
        dyc, dya = dy[:, 0:512], dy[:, 512:1024]

        do = dya * sa
        dza_o[...] = (dya * o * (sga * (1.0 + za * (1.0 - sga)))).astype(BF16)
        lane = lax.broadcasted_iota(jnp.int32, (tm, 512), 1)
        ld = jnp.where((lane & (HEAD_DIM - 1)) < HEAD_DIM // 2, lse, _group_sum(do * o, b_ref))
        do1_o[0] = do.astype(BF16)
        ld1_o[0] = ld
        _to_slabs(slab, do)
        _scatter_classes(slab, 0, 4, do4_o, 4)
        _scatter_classes(slab, 0, 4, do16_o, 16)
        _to_slabs(slab, ld)
        _scatter_classes(slab, 0, 4, ld4_o, 4)
        _scatter_classes(slab, 0, 4, ld16_o, 16)

        dzc = dyc * sz * gb
        dzc_o[...] = dzc.astype(BF16)
        dgbz_o[:, 0:512] = (dyc * sz * zb).astype(BF16)
        dgbz_o[:, 512:1024] = (dyc * gb * zb * (sg * (1.0 + zc * (1.0 - sg)))).astype(BF16)
        dcb_o[...] += jnp.sum(dzc, axis=0, keepdims=True)
        dcw_o[0:1, :] += jnp.sum(dzc * t_up, axis=0, keepdims=True)
        dcw_o[1:2, :] += jnp.sum(dzc * tt, axis=0, keepdims=True)
        dcw_o[2:3, :] += jnp.sum(dzc * t_dn, axis=0, keepdims=True)

    row = lambda w, j=0: pl.BlockSpec((tm, w), lambda i: (i, j))
    full = lambda shp: pl.BlockSpec(shp, lambda i: (0,) * len(shp))
    prev = pl.BlockSpec((hb, 2048), lambda i: (jnp.maximum(i * (tm // hb) - 1, 0), 0))
    nxt = pl.BlockSpec((hb, 2048), lambda i: (jnp.minimum((i + 1) * (tm // hb), s // hb - 1), 0))
    cls = lambda d, dt: jax.ShapeDtypeStruct((d, s // d, 512), dt)
    cspecs = [_class_spec(d, 512) for d in DILATIONS]
    return pl.pallas_call(
        body, name="combine", grid=(nt,),
        out_shape=(jax.ShapeDtypeStruct((s, 1024), BF16), jax.ShapeDtypeStruct((s, 1024), F32),
                   cls(1, F32), cls(1, BF16), jax.ShapeDtypeStruct((s, 512), BF16),
                   jax.ShapeDtypeStruct((s, 1024), BF16), jax.ShapeDtypeStruct((s, 512), BF16),
                   jax.ShapeDtypeStruct((1, 128), F32), jax.ShapeDtypeStruct((1, 512), F32),
                   jax.ShapeDtypeStruct((8, 512), F32),
                   cls(4, BF16), cls(4, F32), cls(16, BF16), cls(16, F32)),
        in_specs=cspecs + cspecs + [row(2048), prev, nxt, row(512, 1), row(1024), row(1024),
                                    full((1024, 1024)), full((8, 512)), full((1, 512)), full((256, 256))],
        out_specs=(row(1024), row(1024), cspecs[0], cspecs[0], row(512), row(1024), row(512),
                   full((1, 128)), full((1, 512)), full((8, 512)),
                   cspecs[1], cspecs[1], cspecs[2], cspecs[2]),
        scratch_shapes=[pltpu.VMEM((4, tm, 128), F32)],
        compiler_params=_params(("arbitrary",)),
    )(*o_g, *lse_g, cg, cg, cg, vz, x, tgt, wout, cw, cb, b256)


def _attn_bwd(qkn_l, v_l, do_l, ld_l, bias, name):
    r_cls, length, _ = qkn_l.shape
    qt = min(length, 1024)
    nb, nsub, nt = length // QB, qt // QB, length // qt

    def body(q_ref, k_ref, v_ref, do_ref, ld_ref, b_ref, dq_ref, dkv_hbm, dsum_ref, dk_acc, dv_acc, stage, sems):
        p_id, r, t = pl.program_id(0), pl.program_id(1), pl.program_id(2)
        lo = lax.broadcasted_iota(jnp.int32, (QB, 128), 1) < HEAD_DIM

        @pl.when(t == 0)
        def _():
            dk_acc[...] = jnp.zeros_like(dk_acc)
            dv_acc[...] = jnp.zeros_like(dv_acc)

        @pl.when((t == 0) & (r == 0))
        def _():
            dsum_ref[...] = jnp.zeros_like(dsum_ref)

        nt_dims = (((1,), (1,)), ((), ()))
        tn_dims = (((0,), (0,)), ((), ()))
        coords, qqs, dds, logits, dps, lcols, dcols = [], [], [], [], [], [], []
        for i in range(nsub):
            q0, start, variant = _block_coords(t, i, nsub, nb, length)
            qq = _split_heads(q_ref[q0:q0 + QB, :], lo)
            dd = _split_heads(do_ref[q0:q0 + QB, :], lo)
            k = k_ref[pl.ds(start, KB), :]
            v = v_ref[pl.ds(start, KB), :]
            logits.append(lax.dot_general(qq, k, nt_dims, preferred_element_type=F32) + b_ref[variant])
            dps.append(lax.dot_general(dd, v, nt_dims, preferred_element_type=F32))
            lcols.append(_col_pair(ld_ref, q0, 0))
            dcols.append(_col_pair(ld_ref, q0, HEAD_DIM // 2))
            coords.append((q0, start, variant))
            qqs.append(qq)
            dds.append(dd)
        p = jnp.exp(jnp.concatenate(logits, axis=0) - jnp.concatenate(lcols, axis=0))
        ds = p * (jnp.concatenate(dps, axis=0) - jnp.concatenate(dcols, axis=0))
        pb = p.astype(BF16)
        dsb = ds.astype(BF16)
        middle = None
        for i in range(nsub):
            q0, start, variant = coords[i]
            rows = slice(2 * QB * i, 2 * QB * (i + 1))
            if 0 < i < nsub - 1:
                middle = ds[rows] if middle is None else middle + ds[rows]
            else:
                dsum_ref[variant] += ds[rows]
            dqq = jnp.dot(dsb[rows], k_ref[pl.ds(start, KB), :], preferred_element_type=F32)
            dq_ref[q0:q0 + QB, :] = jnp.where(lo, dqq[0:QB], dqq[QB:2 * QB]).astype(BF16)
            dk_acc[pl.ds(start, KB), :] += lax.dot_general(dsb[rows], qqs[i], tn_dims, preferred_element_type=F32)
            dv_acc[pl.ds(start, KB), :] += lax.dot_general(pb[rows], dds[i], tn_dims, preferred_element_type=F32)
        if middle is not None:
            dsum_ref[1] += middle

        @pl.when(t == nt - 1)
        def _():
            for which, acc in enumerate((dk_acc, dv_acc)):
                stage[...] = acc[...].astype(BF16)
                cp = pltpu.make_async_copy(stage, dkv_hbm.at[r, p_id, which], sems.at[which])
                cp.start()
                cp.wait()

    qspec = pl.BlockSpec((None, qt, 128), lambda p, r, t: (r, t, p))
    return pl.pallas_call(
        body, name=name, grid=(N_PAIR, r_cls, nt),
        out_shape=(jax.ShapeDtypeStruct((r_cls, length, 512), BF16),
                   jax.ShapeDtypeStruct((r_cls, N_PAIR, 2, length, 128), BF16),
                   jax.ShapeDtypeStruct((N_PAIR, 3, 2 * QB, KB), F32)),
        in_specs=[qspec,
                  pl.BlockSpec((None, length, 128), lambda p, r, t: (r, 0, 4 + p)),
                  pl.BlockSpec((None, length, 128), lambda p, r, t: (r, 0, p)),
                  qspec, qspec,
                  pl.BlockSpec((3, None, 2 * QB, KB), lambda p, r, t: (0, p, 0, 0))],
        out_specs=(qspec, pl.BlockSpec(memory_space=pl.ANY),
                   pl.BlockSpec((None, 3, 2 * QB, KB), lambda p, r, t: (p, 0, 0, 0))),
        scratch_shapes=[pltpu.VMEM((length, 128), F32), pltpu.VMEM((length, 128), F32),
                        pltpu.VMEM((length, 128), BF16), pltpu.SemaphoreType.DMA((2,))],
        compiler_params=_params(("arbitrary", "arbitrary", "arbitrary")),
    )(qkn_l, qkn_l, v_l, do_l, ld_l, bias)


def _bwd_tail(dq_g, dkv_g, qkr, qkw, dza, dgbz, dzc, cg, cw, wblk, x, norm_w, dout, b256):
    s = x.shape[0]
    tm = TM
    hb = 16
    nt = s // tm

    def body(dq1, dq4, dq16, dkv1, dkv4, dkv16, qkr_ref, qkw_ref, dza_ref, dgbz_ref, dzc_ref,
             dzp_ref, dzn_ref, u_ref, gc_ref, cw_ref, w_ref, x_ref, nw_ref, dout_ref, b_ref,
             gx_o, dproj_o, dnw_o, dqkw_o, slab):
        i = pl.program_id(0)

        def nat_q(ref, d):
            return _gather_classes(slab, lambda r, j: ref[r, :, 128 * j:128 * (j + 1)], 4, d)

        def nat_kv(ref, d, which):
            return _gather_classes(slab, lambda r, j: ref[r, j, which], 4, d)

        @pl.when(i == 0)
        def _():
            dnw_o[...] = jnp.zeros_like(dnw_o)
            dqkw_o[...] = jnp.zeros_like(dqkw_o)

        dzc = dzc_ref[...].astype(F32)
        d_prev = jnp.where(i == 0, 0.0, dzp_ref[hb - 1:hb, :].astype(F32))
        d_next = jnp.where(i == nt - 1, 0.0, dzn_ref[0:1, :].astype(F32))
        rows = lax.broadcasted_iota(jnp.int32, (tm, 512), 0)
        d_up = jnp.where(rows == 0, d_prev, pltpu.roll(dzc, 1, 0))
        d_dn = jnp.where(rows == tm - 1, d_next, pltpu.roll(dzc, tm - 1, 0))
        dt = cw_ref[0:1, :] * d_dn + cw_ref[1:2, :] * dzc + cw_ref[2:3, :] * d_up
        u = u_ref[...].astype(F32)
        gc = gc_ref[...].astype(F32)
        dproj_o[:, 0:512] = (dt * gc).astype(BF16)
        dproj_o[:, 512:1024] = dgbz_ref[:, 0:512]
        dproj_o[:, 1024:1536] = (dt * u).astype(BF16)
        dproj_o[:, 1536:2048] = dgbz_ref[:, 512:1024]

        dqn = (dq1[0].astype(F32) + nat_q(dq4, 4) + nat_q(dq16, 16)) * (1.0 / 8.0)
        dk1 = jnp.concatenate([dkv1[0, j, 0] for j in range(N_PAIR)], axis=1)
        dv1 = jnp.concatenate([dkv1[0, j, 1] for j in range(N_PAIR)], axis=1)
        dkn = dk1 + nat_kv(dkv4, 4, 0) + nat_kv(dkv16, 16, 0)
        dvn = dv1 + nat_kv(dkv4, 4, 1) + nat_kv(dkv16, 16, 1)
        g = jnp.concatenate([dqn, dkn], axis=1) * qkw_ref[...]
        raw = qkr_ref[...].astype(F32)
        rr = lax.rsqrt(_group_sum(raw * raw, b_ref, split=False) * (1.0 / HEAD_DIM) + EPS)
        proj_gq = _group_sum(g * raw, b_ref) * (1.0 / HEAD_DIM)
        draw = rr * g - raw * (rr * rr * rr) * proj_gq
        dqkw_o[...] += jnp.sum(jnp.concatenate([dqn, dkn], axis=1) * raw * rr, axis=0, keepdims=True)
        dproj_o[:, 2048:3072] = draw.astype(BF16)
        dproj_o[:, 3072:3584] = dvn.astype(BF16)
        dproj_o[:, 3584:4096] = dza_ref[...]

        nt_dims = (((1,), (1,)), ((), ()))
        dh = lax.dot_general(dproj_o[:, 0:1024], w_ref[0], nt_dims, preferred_element_type=F32)
        for b in range(1, 4):
            dh += lax.dot_general(dproj_o[:, 1024 * b:1024 * b + 1024], w_ref[b], nt_dims,
                                  preferred_element_type=F32)

        xf = x_ref[...]
        r = lax.rsqrt(jnp.mean(xf * xf, axis=-1, keepdims=True) + EPS)
        gh = dh * nw_ref[...]
        dnw_o[...] += jnp.sum(dh * xf * r, axis=0, keepdims=True)
        mean_gx = jnp.mean(gh * xf, axis=-1, keepdims=True)
        gx_o[...] = dout_ref[...] + r * gh - xf * (r * r * r) * mean_gx

    row = lambda w, j=0: pl.BlockSpec((tm, w), lambda i: (i, j))
    full = lambda shp: pl.BlockSpec(shp, lambda i: (0,) * len(shp))
    prev = pl.BlockSpec((hb, 512), lambda i: (jnp.maximum(i * (tm // hb) - 1, 0), 0))
    nxt = pl.BlockSpec((hb, 512), lambda i: (jnp.minimum((i + 1) * (tm // hb), s // hb - 1), 0))
    return pl.pallas_call(
        body, name="bwd_tail", grid=(nt,),
        out_shape=(jax.ShapeDtypeStruct((s, 1024), F32), jax.ShapeDtypeStruct((s, 4096), BF16),
                   jax.ShapeDtypeStruct((1, 1024), F32), jax.ShapeDtypeStruct((1, 1024), F32)),
        in_specs=[_class_spec(d, 512) for d in DILATIONS]
        + [pl.BlockSpec((d, N_PAIR, 2, tm // d, 128), lambda i: (0, 0, 0, i, 0)) for d in DILATIONS]
        + [row(1024), full((1, 1024)), row(512), row(1024), row(512), prev, nxt,
           row(512, 0), row(512, 2), full((8, 512)), full((4, 1024, 1024)), row(1024),
           full((1, 1024)), row(1024), full((256, 256))],
        out_specs=(row(1024), row(4096), full((1, 1024)), full((1, 1024))),
        scratch_shapes=[pltpu.VMEM((4, tm, 128), F32)],
        compiler_params=_params(("arbitrary",)),
    )(*dq_g, *dkv_g, qkr, qkw, dza, dgbz, dzc, dzc, dzc, cg, cg, cw, wblk, x, norm_w, dout, b256)


def _wgrad(a, b, row_blocked, name):
    s, m = a.shape
    n = b.shape[1]
    tk = 1024
    ncol = min(n, 2048)
    nj, nk = n // ncol, s // tk

    def body(a_ref, b_ref, o_ref, acc):
        kk = pl.program_id(1)

        @pl.when(kk == 0)
        def _():
            acc[...] = jnp.zeros_like(acc)

        acc[...] += lax.dot_general(a_ref[...], b_ref[...].astype(BF16), (((0,), (0,)), ((), ())),
                                    preferred_element_type=F32)

        @pl.when(kk == nk - 1)
        def _():
            blocks, _, rows, _ = o_ref.shape
            for blk in range(blocks):
                for half in range(2):
                    if row_blocked:
                        r0 = (2 * blk + half) * rows
                        o_ref[blk, half] = acc[r0:r0 + rows, :].astype(BF16)
                    else:
                        o_ref[blk, half] = acc[half * rows:(half + 1) * rows,
                                               1024 * blk:1024 * (blk + 1)].astype(BF16)

    if row_blocked:
        out_shape = jax.ShapeDtypeStruct((4, 2, m // 8, 1024), BF16)
        out_spec = pl.BlockSpec((4, 2, m // 8, 1024), lambda j, k: (0, 0, 0, 0))
    else:
        out_shape = jax.ShapeDtypeStruct((n // 1024, 2, m // 2, 1024), BF16)
        out_spec = pl.BlockSpec((ncol // 1024, 2, m // 2, 1024), lambda j, k: (j, 0, 0, 0))
    return pl.pallas_call(
        body, name=name, grid=(nj, nk),
        out_shape=out_shape,
        in_specs=[pl.BlockSpec((tk, m), lambda j, k: (k, 0)), pl.BlockSpec((tk, ncol), lambda j, k: (k, j))],
        out_specs=out_spec,
        scratch_shapes=[pltpu.VMEM((m, ncol), F32)],
        compiler_params=_params(("parallel", "arbitrary")),
    )(a, b)


def _dbias(dsum_all, onehot_all):
    def body(ds_ref, oh_ref, o_ref):
        step = pl.program_id(0) * 3 + pl.program_id(1)

        @pl.when(step == 0)
        def _():
            o_ref[...] = jnp.zeros_like(o_ref)

        rowq = lax.broadcasted_iota(jnp.int32, (2 * QB, KB), 0) & (QB - 1)
        hrow = lax.broadcasted_iota(jnp.int32, (8, KB), 0)
        diag = jnp.zeros((8, KB), F32)
        for p in range(N_PAIR):
            y = ds_ref[p]
            for bit in range(7):
                sh = 1 << bit
                y = jnp.where((rowq & sh) != 0, pltpu.roll(y, KB - sh, 1), y)
            da = jnp.sum(y[0:QB], axis=0, keepdims=True)
            db = jnp.sum(y[QB:2 * QB], axis=0, keepdims=True)
            diag = jnp.where(hrow == 2 * p, da, diag)
            diag = jnp.where(hrow == 2 * p + 1, db, diag)
        o_ref[...] += jnp.dot(diag, oh_ref[...], preferred_element_type=F32, precision=lax.Precision.HIGHEST)

    return pl.pallas_call(
        body, name="dbias", grid=(3, 3),
        out_shape=jax.ShapeDtypeStruct((8, 128), F32),
        in_specs=[pl.BlockSpec((None, N_PAIR, None, 2 * QB, KB), lambda g, v: (g, 0, v, 0, 0)),
                  pl.BlockSpec((None, None, KB, 128), lambda g, v: (g, v, 0, 0))],
        out_specs=pl.BlockSpec((8, 128), lambda g, v: (0, 0)),
        compiler_params=_params(("arbitrary", "arbitrary")),
    )(dsum_all, onehot_all)


def _gsync(pw_in, pw_out, small):
    hin, hout = pw_in.shape[2], pw_out.shape[2]
    nsmall = small.shape[0]

    def body(pin_hbm, pout_hbm, small_ref, gin_o, gout_o, small_o,
             mine_in, recv_in, sbuf_in, rbuf_in, mine_out, recv_out, sbuf_out, rbuf_out, gather,
             lsem, asend, arecv, bsend, brecv, csend, crecv, ssend, srecv):
        x, y, c = lax.axis_index("x"), lax.axis_index("y"), lax.axis_index("c")
        b = 2 * x + y
        dev = 4 * x + 2 * y + c
        sib = (x, y, 1 - c)

        def rcopy(src, dst, ssem, rsem, to):
            return pltpu.make_async_remote_copy(src_ref=src, dst_ref=dst, send_sem=ssem, recv_sem=rsem,
                                                device_id=to, device_id_type=MESH)

        gather[dev] = small_ref[...]
        s_sends = []
        for k in range(1, 8):
            to = (x ^ (k >> 2), y ^ ((k >> 1) & 1), c ^ (k & 1))
            cp = rcopy(gather.at[dev], gather.at[dev], ssend.at[k - 1], srecv.at[k - 1], to)
            cp.start()
            s_sends.append(cp)

        a_in = rcopy(pin_hbm.at[:, 1 - c], recv_in, asend.at[0], arecv.at[0], sib)
        a_out = rcopy(pout_hbm.at[:, 1 - c], recv_out, asend.at[1], arecv.at[1], sib)
        a_in.start()
        a_out.start()
        l_in = pltpu.make_async_copy(pin_hbm.at[:, c], mine_in, lsem.at[0])
        l_out = pltpu.make_async_copy(pout_hbm.at[:, c], mine_out, lsem.at[1])
        l_in.start()
        l_out.start()
        l_in.wait()
        l_out.wait()

        def stage_b(a_cp, mine, recv, sbuf, rbuf, base):
            a_cp.wait_recv()
            sends = []
            for k in (1, 2, 3):
                bk = b ^ k
                sbuf[k - 1] = (mine[bk].astype(F32) + recv[bk].astype(F32)).astype(BF16)
                cp = rcopy(sbuf.at[k - 1], rbuf.at[k - 1], bsend.at[base + k - 1], brecv.at[base + k - 1],
                           (x ^ (k >> 1), y ^ (k & 1), c))
                cp.start()
                sends.append(cp)
            return sends

        b_in = stage_b(a_in, mine_in, recv_in, sbuf_in, rbuf_in, 0)
        b_out = stage_b(a_out, mine_out, recv_out, sbuf_out, rbuf_out, 3)

        def stage_c(b_sends, mine, recv, rbuf, g_o, half, idx):
            acc = mine[b].astype(F32) + recv[b].astype(F32)
            for k in (1, 2, 3):
                b_sends[k - 1].wait_recv()
                acc = acc + rbuf[k - 1].astype(F32)
            rows = g_o.at[pl.ds(pl.multiple_of(c * half, half), half), :]
            g_o[pl.ds(pl.multiple_of(c * half, half), half), :] = acc
            cp = rcopy(rows, rows, csend.at[idx], crecv.at[idx], sib)
            cp.start()
            return cp

        c_in = stage_c(b_in, mine_in, recv_in, rbuf_in, gin_o, hin, 0)
        c_out = stage_c(b_out, mine_out, recv_out, rbuf_out, gout_o, hout, 1)

        for cp in s_sends:
            cp.wait_recv()
        tot = gather[0]
        for d in range(1, 8):
            tot = tot + gather[d]
        small_o[...] = tot

        for g_o, half, idx in ((gin_o, hin, 0), (gout_o, hout, 1)):
            other = g_o.at[pl.ds(pl.multiple_of((1 - c) * half, half), half), :]
            rcopy(other, other, csend.at[idx], crecv.at[idx], sib).wait_recv()
        for cp in s_sends + [a_in, a_out] + b_in + b_out + [c_in, c_out]:
            cp.wait_send()

    vm = pl.BlockSpec(memory_space=pltpu.VMEM)
    hbm = pl.BlockSpec(memory_space=pl.ANY)
    return pl.pallas_call(
        body, name="gsync",
        out_shape=(jax.ShapeDtypeStruct((2 * hin, 1024), F32), jax.ShapeDtypeStruct((2 * hout, 1024), F32),
                   jax.ShapeDtypeStruct((nsmall, 128), F32)),
        in_specs=[hbm, hbm, vm], out_specs=(vm, vm, vm),
        scratch_shapes=[pltpu.VMEM((4, hin, 1024), BF16), pltpu.VMEM((4, hin, 1024), BF16),
                        pltpu.VMEM((3, hin, 1024), BF16), pltpu.VMEM((3, hin, 1024), BF16),
                        pltpu.VMEM((4, hout, 1024), BF16), pltpu.VMEM((4, hout, 1024), BF16),
                        pltpu.VMEM((3, hout, 1024), BF16), pltpu.VMEM((3, hout, 1024), BF16),
                        pltpu.VMEM((8, nsmall, 128), F32),
                        pltpu.SemaphoreType.DMA((2,)),
                        pltpu.SemaphoreType.DMA((2,)), pltpu.SemaphoreType.DMA((2,)),
                        pltpu.SemaphoreType.DMA((6,)), pltpu.SemaphoreType.DMA((6,)),
                        pltpu.SemaphoreType.DMA((2,)), pltpu.SemaphoreType.DMA((2,)),
                        pltpu.SemaphoreType.DMA((7,)), pltpu.SemaphoreType.DMA((7,))],
        compiler_params=_params(),
    )(pw_in, pw_out, small)


def _adamw_math(w, g, m, v):
    m = ADAM_B1 * m + (1.0 - ADAM_B1) * g
    v = ADAM_B2 * v + (1.0 - ADAM_B2) * (g * g)
    m_hat = m / (1.0 - ADAM_B1 ** ADAM_STEP)
    v_hat = v / (1.0 - ADAM_B2 ** ADAM_STEP)
    delta = -ADAM_LR * (m_hat / (jnp.sqrt(v_hat) + ADAM_EPS) + ADAM_WD * w)
    return delta, m, v


def _adamw(w, g, m, v, name):
    rows, cols = w.shape
    tr = 256 if rows % 256 == 0 else rows

    def body(w_ref, g_ref, m_ref, v_ref, d_o, m_o, v_o):
        d, m2, v2 = _adamw_math(w_ref[...], g_ref[...], m_ref[...], v_ref[...])
        d_o[...] = d
        m_o[...] = m2
        v_o[...] = v2

    spec = pl.BlockSpec((tr, cols), lambda i: (i, 0))
    shp = jax.ShapeDtypeStruct((rows, cols), F32)
    return pl.pallas_call(
        body, name=name, grid=(rows // tr,), out_shape=(shp, shp, shp),
        in_specs=[spec] * 4, out_specs=(spec, spec, spec),
        compiler_params=_params(("parallel",)),
    )(w, g, m, v)


def _fold_heads(dqkw):
    def body(x_ref, o_ref):
        xs = x_ref[...]
        sq = xs[0:1] + xs[1:2] + xs[2:3] + xs[3:4]
        sk = xs[4:5] + xs[5:6] + xs[6:7] + xs[7:8]
        both = jnp.concatenate([sq, sk], axis=0)
        o_ref[...] = both + pltpu.roll(both, HEAD_DIM, 1)

    vm = pl.BlockSpec(memory_space=pltpu.VMEM)
    return pl.pallas_call(body, name="fold_heads", out_shape=jax.ShapeDtypeStruct((2, 128), F32),
                          in_specs=[vm], out_specs=vm, compiler_params=_params())(dqkw)


def kernel(x, norm_w, w_in, conv_w, conv_b, q_norm_w, k_norm_w, rel_bias, w_out, loss_target, m_norm_w, m_w_in, m_conv_w, m_conv_b, m_q_norm_w, m_k_norm_w, m_rel_bias, m_w_out, v_norm_w, v_w_in, v_conv_w, v_conv_b, v_q_norm_w, v_k_norm_w, v_rel_bias, v_w_out):
    x2 = x[0]
    tgt = loss_target[0]
    blk = 2 * lax.axis_index("x") + lax.axis_index("y")

    conv_w8 = jnp.pad(conv_w, ((0, 5), (0, 0)))
    wblk, woutblk, cwblk = _wgather(w_in, w_out, conv_w8)
    wout_full = woutblk.reshape(1024, 1024)
    cw_full = cwblk.transpose(1, 0, 2).reshape(8, 512)

    qkw = jnp.concatenate([jnp.tile(q_norm_w, 8) * 0.125, jnp.tile(k_norm_w, 8)])[None, :]
    qkw_raw = jnp.concatenate([jnp.tile(q_norm_w, 8), jnp.tile(k_norm_w, 8)])[None, :]
    gidx = jnp.arange(256) // HEAD_DIM
    b256 = (gidx[:, None] == gidx[None, :]).astype(BF16)

    h, cg, qkr, qkn, vz, qkn4, v4, qkn16, v16 = _proj(x2, norm_w[None, :], wblk, qkw, b256)

    biases = _bias_tables(rel_bias)
    qkn_l = [qkn[None], qkn4, qkn16]
    v_l = [vz[None], v4, v16]
    o_g, lse_g = [], []
    for gi, d in enumerate(DILATIONS):
        o_l, lse_l = _attn_fwd(qkn_l[gi], v_l[gi], biases[gi], f"attn_fwd_d{d}")
        o_g.append(o_l)
        lse_g.append(lse_l)

    (y, dout, ld1, do1, dza, dgbz, dzc, loss_p, dcb, dcw, do4, ld4, do16, ld16) = _combine(
        o_g, lse_g, cg, vz, x2, tgt, wout_full, cw_full, conv_b[None, :], b256)

    dq_g, dkv_g, dsums = [], [], []
    for gi, (d, do_l, ld_l) in enumerate(zip(DILATIONS, (do1, do4, do16), (ld1, ld4, ld16))):
        dq_l, dkv_l, dsum = _attn_bwd(qkn_l[gi], v_l[gi], do_l, ld_l, biases[gi], f"attn_bwd_d{d}")
        dq_g.append(dq_l)
        dkv_g.append(dkv_l)
        dsums.append(dsum)

    grad_x, dproj, dnw, dqkw = _bwd_tail(dq_g, dkv_g, qkr, qkw_raw, dza, dgbz, dzc, cg, cw_full, wblk,
                                         x2, norm_w[None, :], dout, b256)

    pw_in = _wgrad(h, dproj, False, "wgrad_in")
    pw_out = _wgrad(y, dout, True, "wgrad_out")
    dbias8 = _dbias(jnp.stack(dsums, axis=0), jnp.stack([_diag_bucket_onehot(d) for d in DILATIONS], axis=0))

    small = jnp.concatenate([dnw.reshape(8, 128), dcb.reshape(4, 128), dqkw.reshape(8, 128),
                             dcw[0:3].reshape(12, 128), dbias8], axis=0)
    g_win, g_wout, gsmall = _gsync(pw_in, pw_out, small)

    g_nw = gsmall[0:8].reshape(1024)
    g_cb = gsmall[8:12].reshape(512)
    folded = _fold_heads(gsmall[12:20])
    g_qw, g_kw = folded[0, 0:64], folded[1, 0:64]
    g_cw = lax.dynamic_slice(gsmall[20:32].reshape(3, 512), (0, blk * 128), (3, 128))
    g_rb = gsmall[32:40][:, 0:32].T

    loss = lax.psum(loss_p[0, 0], ("x", "y", "c"))

    d_win, nm_win, nv_win = _adamw(w_in, g_win, m_w_in, v_w_in, "adamw_w_in")
    d_wout, nm_wout, nv_wout = _adamw(w_out, g_wout, m_w_out, v_w_out, "adamw_w_out")

    def pack(parts):
        rows = [parts[0].reshape(8, 128), parts[1].reshape(4, 128),
                jnp.pad(parts[2], (0, 64))[None, :], jnp.pad(parts[3], (0, 64))[None, :],
                parts[4], jnp.pad(parts[5].T, ((0, 0), (0, 96)))]
        return jnp.concatenate(rows, axis=0)

    ws = pack([norm_w, conv_b, q_norm_w, k_norm_w, conv_w, rel_bias])
    gs = pack([g_nw, g_cb, g_qw, g_kw, g_cw, g_rb])
    ms = pack([m_norm_w, m_conv_b, m_q_norm_w, m_k_norm_w, m_conv_w, m_rel_bias])
    vs = pack([v_norm_w, v_conv_b, v_q_norm_w, v_k_norm_w, v_conv_w, v_rel_bias])
    rpad = lambda a: jnp.pad(a, ((0, 7), (0, 0)))
    d_s, nm_s, nv_s = _adamw(rpad(ws), rpad(gs), rpad(ms), rpad(vs), "adamw_small")

    def unpack(a):
        return (a[0:8].reshape(1024), a[12:13, 0:64].reshape(64), a[13:14, 0:64].reshape(64),
                a[14:17], a[8:12].reshape(512), a[17:25, 0:32].T)

    def ordered(nw, win, cw, cb, qw, kw, rb, wout):
        return (nw, win, cw, cb, qw, kw, rb, wout)

    g_un = (g_nw, g_qw, g_kw, g_cw, g_cb, g_rb)
    outs = [loss, grad_x[None]]
    for un, win_v, wout_v in ((g_un, g_win, g_wout), (unpack(d_s), d_win, d_wout),
                              (unpack(nm_s), nm_win, nm_wout), (unpack(nv_s), nv_win, nv_wout)):
        nw, qw, kw, cw, cb, rb = un
        outs.extend(ordered(nw, win_v, cw, cb, qw, kw, rb, wout_v))
    return tuple(outs)
```

```python
import math

import jax
import jax.numpy as jnp
from jax import lax
from jax.experimental import pallas as pl
from jax.experimental.pallas import tpu as pltpu

F32 = jnp.float32
BF16 = jnp.bfloat16
MESH = pl.DeviceIdType.MESH

D_MODEL = 1024
CONV_W = 512
ATTN_W = 512
HEAD_DIM = 64
N_PAIR = 4
DILATIONS = (1, 4, 16)
HALF = 64
QB = 128
KB = QB + 2 * HALF
NUM_BUCKETS = 32
MAX_DISTANCE = 1024
EPS = 1e-6
NEG = -1e30
ADAM_LR, ADAM_B1, ADAM_B2, ADAM_EPS, ADAM_WD, ADAM_STEP = 0.001, 0.9, 0.999, 1e-08, 0.01, 10
VMEM_LIMIT = 48 << 20


def _params(sem=None, vmem=VMEM_LIMIT, **kw):
    if sem is not None:
        kw["dimension_semantics"] = sem
    return pltpu.CompilerParams(vmem_limit_bytes=vmem, **kw)


def _sigmoid(z):
    return 1.0 / (1.0 + jnp.exp(-z))


def _group_sum(val, b_ref, split=True):
    hi = val.astype(BF16)
    lo = (val - hi.astype(F32)).astype(BF16) if split else None
    outs = []
    for j in range(val.shape[1] // 256):
        sl = slice(256 * j, 256 * j + 256)
        part = jnp.dot(hi[:, sl], b_ref[...], preferred_element_type=F32)
        if split:
            part = part + jnp.dot(lo[:, sl], b_ref[...], preferred_element_type=F32)
        outs.append(part)
    return outs[0] if len(outs) == 1 else jnp.concatenate(outs, axis=1)


def _t5_bucket(rel):
    half_b = NUM_BUCKETS // 2
    max_exact = half_b // 2
    ret = jnp.where(rel > 0, half_b, 0)
    n = jnp.abs(rel)
    nf = jnp.maximum(n, 1).astype(F32)
    large = max_exact + (jnp.log(nf / max_exact) / math.log(MAX_DISTANCE / max_exact)
                         * (half_b - max_exact)).astype(jnp.int32)
    large = jnp.minimum(large, half_b - 1)
    return ret + jnp.where(n < max_exact, n, large)


def _window_rel(variant):
    off = (0, HALF, 2 * HALF)[variant]
    return jnp.arange(KB)[None, :] - off - jnp.arange(QB)[:, None]


def _bias_tables(rel_bias):
    bkts = []
    for dilation in DILATIONS:
        for variant in range(3):
            rel = _window_rel(variant)
            bkt = _t5_bucket(jnp.clip(rel, -HALF, HALF) * dilation)
            bkts.append(jnp.where(jnp.abs(rel) <= HALF, bkt, -1))
    bkt_all = jnp.stack(bkts, axis=0).astype(jnp.int32)

    def body(rb_ref, bkt_ref, o_ref):
        bkt = bkt_ref[...]
        for h in range(8):
            acc = jnp.full((QB, KB), NEG, F32)
            for b in range(NUM_BUCKETS):
                acc = jnp.where(bkt == b, rb_ref[b, h], acc)
            o_ref[h] = acc

    out = pl.pallas_call(
        body, name="bias_tables", grid=(9,),
        out_shape=jax.ShapeDtypeStruct((9, 8, QB, KB), F32),
        in_specs=[pl.BlockSpec(memory_space=pltpu.SMEM), pl.BlockSpec((None, QB, KB), lambda i: (i, 0, 0))],
        out_specs=pl.BlockSpec((None, 8, QB, KB), lambda i: (i, 0, 0, 0)),
        compiler_params=_params(("parallel",)),
    )(rel_bias, bkt_all)
    return out.reshape(3, 3, N_PAIR, 2 * QB, KB)


def _diag_bucket_onehot(dilation):
    out = []
    c = jnp.arange(KB)
    for variant in range(3):
        off = (0, HALF, 2 * HALF)[variant]
        rel = ((c - off + 128) % 256) - 128
        band = jnp.abs(rel) <= HALF
        bkt = _t5_bucket(jnp.clip(rel, -HALF, HALF) * dilation)
        oh = (bkt[:, None] == jnp.arange(128)[None, :]) & band[:, None]
        out.append(oh.astype(F32))
    return jnp.stack(out, axis=0)


def _wgather(w_in, w_out, conv_w):
    rin, rout = w_in.shape[0] // 2, w_out.shape[0] // 2

    def body(win_ref, wout_ref, cw_ref, win_o, wout_o, cw_o, send_sems, recv_sems):
        x, y, c = lax.axis_index("x"), lax.axis_index("y"), lax.axis_index("c")
        b = 2 * x + y
        win_o[b] = win_ref[...].astype(BF16)
        wout_o[b] = wout_ref[...].astype(BF16)
        cw_o[b] = cw_ref[...]

        def peer(k):
            return (x ^ (k >> 1), y ^ (k & 1))

        def piece(ref, blk, half_rows, core):
            return ref.at[blk, pl.ds(core * half_rows, half_rows), :]

        def copy(sem, src, dst, to):
            return pltpu.make_async_remote_copy(src_ref=src, dst_ref=dst, send_sem=send_sems.at[sem],
                                                recv_sem=recv_sems.at[sem], device_id=to, device_id_type=MESH)

        sends = []
        for k in (1, 2, 3):
            px, py = peer(k)
            sends.append(copy(k - 1, piece(win_o, b, rin, c), piece(win_o, b, rin, c), (px, py, c)))
            sends.append(copy(3 + k - 1, piece(wout_o, b, rout, c), piece(wout_o, b, rout, c), (px, py, c)))
            sends.append(copy(6 + k - 1, cw_o.at[b], cw_o.at[b], (px, py, c)))
        for cp in sends:
            cp.start()
        fwd = []
        for k in (1, 2, 3):
            px, py = peer(k)
            bk = 2 * px + py
            copy(k - 1, piece(win_o, bk, rin, c), piece(win_o, bk, rin, c), (px, py, c)).wait_recv()
            f = copy(9 + k - 1, piece(win_o, bk, rin, c), piece(win_o, bk, rin, c), (x, y, 1 - c))
            f.start()
            fwd.append(f)
            copy(3 + k - 1, piece(wout_o, bk, rout, c), piece(wout_o, bk, rout, c), (px, py, c)).wait_recv()
            f = copy(12 + k - 1, piece(wout_o, bk, rout, c), piece(wout_o, bk, rout, c), (x, y, 1 - c))
            f.start()
            fwd.append(f)
            copy(6 + k - 1, cw_o.at[bk], cw_o.at[bk], (px, py, c)).wait_recv()
        for k in (1, 2, 3):
            px, py = peer(k)
            bk = 2 * px + py
            copy(9 + k - 1, piece(win_o, bk, rin, 1 - c), piece(win_o, bk, rin, 1 - c), (x, y, 1 - c)).wait_recv()
            copy(12 + k - 1, piece(wout_o, bk, rout, 1 - c), piece(wout_o, bk, rout, 1 - c), (x, y, 1 - c)).wait_recv()
        for cp in sends + fwd:
            cp.wait_send()

    vm = pl.BlockSpec(memory_space=pltpu.VMEM)
    return pl.pallas_call(
        body, name="wgather",
        out_shape=(jax.ShapeDtypeStruct((4,) + w_in.shape, BF16),
                   jax.ShapeDtypeStruct((4,) + w_out.shape, BF16),
                   jax.ShapeDtypeStruct((4,) + conv_w.shape, F32)),
        in_specs=[vm, vm, vm], out_specs=(vm, vm, vm),
        scratch_shapes=[pltpu.SemaphoreType.DMA((15,)), pltpu.SemaphoreType.DMA((15,))],
        compiler_params=_params(),
    )(w_in, w_out, conv_w)


TM_MATMUL = 512
TM_COMBINE = 256


def _resident(shape):
    return pl.BlockSpec(shape, lambda i: (0,) * len(shape), pipeline_mode=pl.Buffered(1))


def _to_slabs(slab, val, j0=0):
    for j in range(val.shape[1] // 128):
        slab[j0 + j] = val[:, 128 * j:128 * (j + 1)]


def _scatter_classes(slab, j0, nj, out_ref, d, part=0):
    n = slab.shape[1] // d
    for r in range(d):
        for j in range(nj):
            out_ref[r, part * n:(part + 1) * n, 128 * j:128 * (j + 1)] = (
                slab[j0 + j, pl.ds(r, n, stride=d), :].astype(out_ref.dtype))


def _gather_classes(slab, piece, nj, d):
    n = slab.shape[1] // d
    for r in range(d):
        for j in range(nj):
            slab[j, pl.ds(r, n, stride=d), :] = piece(r, j).astype(F32)
    return jnp.concatenate([slab[j] for j in range(nj)], axis=1)


def _class_spec(d, width, tm):
    return pl.BlockSpec((d, tm // d, width), lambda i: (0, i, 0))


def _proj(x, norm_w, wblk, qkw, b256):
    s = x.shape[0]
    tm = TM_MATMUL
    nparts = 2
    tp = tm // nparts

    def body(x_ref, nw_ref, w_ref, qkw_ref, b_ref, h_o, cg_o, qkr_o, qkn_o, vz_o, qkn4_o, v4_o, qkn16_o, v16_o, slabs):
        for part in range(nparts):
            rows = slice(part * tp, (part + 1) * tp)
            slab = slabs.at[part]
            xf = x_ref[rows, :]
            r = lax.rsqrt(jnp.mean(xf * xf, axis=-1, keepdims=True) + EPS)
            h = (xf * r * nw_ref[...]).astype(BF16)
            h_o[rows, :] = h
            p2 = jnp.dot(h, w_ref[2], preferred_element_type=F32)
            qkr_o[rows, :] = p2.astype(BF16)
            ss = _group_sum(p2 * p2, b_ref, split=False)
            rr = lax.rsqrt(ss * (1.0 / HEAD_DIM) + EPS)
            qkn = p2 * rr * qkw_ref[...]
            qkn_o[rows, :] = qkn.astype(BF16)
            _to_slabs(slab, qkn)
            p3 = jnp.dot(h, w_ref[3], preferred_element_type=F32)
            vz_o[rows, :] = p3.astype(BF16)
            _to_slabs(slab, p3[:, 0:512], 8)
            cg_o[rows, 0:1024] = jnp.dot(h, w_ref[0], preferred_element_type=F32).astype(BF16)
            cg_o[rows, 1024:2048] = jnp.dot(h, w_ref[1], preferred_element_type=F32).astype(BF16)
            for d, q_o, v_o in ((4, qkn4_o, v4_o), (16, qkn16_o, v16_o)):
                _scatter_classes(slab, 0, 8, q_o, d, part)
                _scatter_classes(slab, 8, 4, v_o, d, part)

    row = lambda w: pl.BlockSpec((tm, w), lambda i: (i, 0))
    full = lambda shp: pl.BlockSpec(shp, lambda i: (0,) * len(shp))
    nat = lambda w: jax.ShapeDtypeStruct((s, w), BF16)
    cls = lambda d, w: jax.ShapeDtypeStruct((d, s // d, w), BF16)
    return pl.pallas_call(
        body, name="proj", grid=(s // tm,),
        out_shape=(nat(1024), nat(2048), nat(1024), nat(1024), nat(1024),
                   cls(4, 1024), cls(4, 512), cls(16, 1024), cls(16, 512)),
        in_specs=[row(1024), full((1, 1024)), _resident((4, 1024, 1024)), full((1, 1024)), full((256, 256))],
        out_specs=(row(1024), row(2048), row(1024), row(1024), row(1024),
                   _class_spec(4, 1024, tm), _class_spec(4, 512, tm),
                   _class_spec(16, 1024, tm), _class_spec(16, 512, tm)),
        scratch_shapes=[pltpu.VMEM((nparts, 12, tp, 128), F32)],
        compiler_params=_params(("parallel",)),
    )(x, norm_w, wblk, qkw, b256)


def _block_coords(t, i, nsub, nb, length):
    n = t * nsub + i
    q0 = i * QB
    start = pl.multiple_of(jnp.clip(n * QB - HALF, 0, length - KB), HALF)
    variant = jnp.where(n == 0, 0, jnp.where(n == nb - 1, 2, 1))
    return q0, start, variant


def _split_heads(a, lo):
    zero = jnp.zeros_like(a)
    return jnp.concatenate([jnp.where(lo, a, zero), jnp.where(lo, zero, a)], axis=0)


def _col_pair(ref, q0, lane):
    return jnp.concatenate([ref[pl.ds(q0, QB), lane:lane + 1],
                            ref[pl.ds(q0, QB), HEAD_DIM + lane:HEAD_DIM + lane + 1]], axis=0)


def _attn_fwd(qkn_l, v_l, bias, name):
    r_cls, length, _ = qkn_l.shape
    qt = min(length, 1024)
    nb, nsub = length // QB, qt // QB

    def body(q_ref, k_ref, v_ref, b_ref, o_ref, lse_ref):
        t = pl.program_id(2)
        lo = lax.broadcasted_iota(jnp.int32, (QB, 128), 1) < HEAD_DIM

        starts, logits = [], []
        for i in range(nsub):
            _, start, variant = _block_coords(t, i, nsub, nb, length)
            qq = _split_heads(q_ref[i * QB:(i + 1) * QB, :], lo)
            k = k_ref[pl.ds(start, KB), :]
            logits.append(lax.dot_general(qq, k, (((1,), (1,)), ((), ())), preferred_element_type=F32)
                          + b_ref[variant])
            starts.append(start)
        lg = jnp.concatenate(logits, axis=0)
        m = jnp.max(lg, axis=-1, keepdims=True)
        pb = jnp.exp(lg - m).astype(BF16)
        l = jnp.dot(pb, jnp.ones((KB, 128), BF16), preferred_element_type=F32)
        lse = m + jnp.log(l)
        inv = 1.0 / l
        for i in range(nsub):
            rows = slice(2 * QB * i, 2 * QB * (i + 1))
            v = v_ref[pl.ds(starts[i], KB), :]
            pv = jnp.dot(pb[rows], v, preferred_element_type=F32) * inv[rows]
            o_ref[i * QB:(i + 1) * QB, :] = jnp.where(lo, pv[0:QB], pv[QB:2 * QB]).astype(BF16)
            ls = lse[rows]
            lse_ref[i * QB:(i + 1) * QB, :] = jnp.where(lo, ls[0:QB], ls[QB:2 * QB])

    return pl.pallas_call(
        body, name=name, grid=(r_cls, N_PAIR, length // qt),
        out_shape=(jax.ShapeDtypeStruct((r_cls, length, 512), BF16),
                   jax.ShapeDtypeStruct((r_cls, length, 512), F32)),
        in_specs=[pl.BlockSpec((None, qt, 128), lambda r, p, t: (r, t, p)),
                  pl.BlockSpec((None, length, 128), lambda r, p, t: (r, 0, 4 + p)),
                  pl.BlockSpec((None, length, 128), lambda r, p, t: (r, 0, p)),
                  pl.BlockSpec((3, None, 2 * QB, KB), lambda r, p, t: (0, p, 0, 0))],
        out_specs=(pl.BlockSpec((None, qt, 128), lambda r, p, t: (r, t, p)),
                   pl.BlockSpec((None, qt, 128), lambda r, p, t: (r, t, p))),
        compiler_params=_params(("parallel", "parallel", "arbitrary")),
    )(qkn_l, qkn_l, v_l, bias)


def _combine(o_g, lse_g, cg, vz, x, tgt, wout, cw, cb, b256):
    s = x.shape[0]
    tm = TM_COMBINE
    hb = 16
    nt = s // tm

    def body(o1, o4, o16, l1, l4, l16, cg_ref, cgp_ref, cgn_ref, za_ref, x_ref, t_ref, w_ref, cw_ref, cb_ref,
             b_ref, y_o, dout_o, ld1_o, do1_o, dza_o, dgbz_o, dzc_o, loss_o, dcb_o, dcw_o,
             do4_o, ld4_o, do16_o, ld16_o, slab):
        i = pl.program_id(0)

        @pl.when(i == 0)
        def _():
            loss_o[...] = jnp.zeros_like(loss_o)
            dcb_o[...] = jnp.zeros_like(dcb_o)
            dcw_o[...] = jnp.zeros_like(dcw_o)

        u = cg_ref[:, 0:512].astype(F32)
        gb = cg_ref[:, 512:1024].astype(F32)
        gc = cg_ref[:, 1024:1536].astype(F32)
        zc = cg_ref[:, 1536:2048].astype(F32)
        tt = gc * u
        t_prev = cgp_ref[hb - 1:hb, 0:512].astype(F32) * cgp_ref[hb - 1:hb, 1024:1536].astype(F32)
        t_next = cgn_ref[0:1, 0:512].astype(F32) * cgn_ref[0:1, 1024:1536].astype(F32)
        t_prev = jnp.where(i == 0, 0.0, t_prev)
        t_next = jnp.where(i == nt - 1, 0.0, t_next)
        rows = lax.broadcasted_iota(jnp.int32, (tm, 512), 0)
        t_up = jnp.where(rows == 0, t_prev, pltpu.roll(tt, 1, 0))
        t_dn = jnp.where(rows == tm - 1, t_next, pltpu.roll(tt, tm - 1, 0))
        w0, w1, w2 = cw_ref[0:1, :], cw_ref[1:2, :], cw_ref[2:3, :]
        zb = w0 * t_up + w1 * tt + w2 * t_dn + cb_ref[...]
        sg = _sigmoid(zc)
        sz = zc * sg
        y_conv = gb * zb * sz

        a1, p1 = l1[0], o1[0].astype(F32)
        a4 = _gather_classes(slab, lambda r, j: l4[r, :, 128 * j:128 * (j + 1)], 4, 4)
        p4 = _gather_classes(slab, lambda r, j: o4[r, :, 128 * j:128 * (j + 1)], 4, 4)
        a16 = _gather_classes(slab, lambda r, j: l16[r, :, 128 * j:128 * (j + 1)], 4, 16)
        p16 = _gather_classes(slab, lambda r, j: o16[r, :, 128 * j:128 * (j + 1)], 4, 16)
        m = jnp.maximum(jnp.maximum(a1, a4), a16)
        e1, e4, e16 = jnp.exp(a1 - m), jnp.exp(a4 - m), jnp.exp(a16 - m)
        den = e1 + e4 + e16
        lse = m + jnp.log(den)
        o = (e1 * p1 + e4 * p4 + e16 * p16) / den
        za = za_ref[...].astype(F32)
        sga = _sigmoid(za)
        sa = za * sga
        y = jnp.concatenate([y_conv, o * sa], axis=1).astype(BF16)
        y_o[...] = y

        out = x_ref[...] + jnp.dot(y, w_ref[...], preferred_element_type=F32)
        diff = out - t_ref[...]
        loss_o[...] += (0.5 / D_MODEL) * jnp.sum(diff * diff)
        dout = diff * (1.0 / D_MODEL)
        dout_o[...] = dout
        dy = lax.dot_general(dout.astype(BF16), w_ref[...], (((1,), (1,)), ((), ())), preferred_element_type=F32)
        dyc, dya = dy[:, 0:512], dy[:, 512:1024]

        do = dya * sa
        dza_o[...] = (dya * o * (sga * (1.0 + za * (1.0 - sga)))).astype(BF16)
        lane = lax.broadcasted_iota(jnp.int32, (tm, 512), 1)
        ld = jnp.where((lane & (HEAD_DIM - 1)) < HEAD_DIM // 2, lse, _group_sum(do * o, b_ref))
        do1_o[0] = do.astype(BF16)
        ld1_o[0] = ld
        _to_slabs(slab, do)
        _scatter_classes(slab, 0, 4, do4_o, 4)
        _scatter_classes(slab, 0, 4, do16_o, 16)
        _to_slabs(slab, ld)
        _scatter_classes(slab, 0, 4, ld4_o, 4)
        _scatter_classes(slab, 0, 4, ld16_o, 16)

        dzc = dyc * sz * gb
        dzc_o[...] = dzc.astype(BF16)
        dgbz_o[:, 0:512] = (dyc * sz * zb).astype(BF16)
        dgbz_o[:, 512:1024] = (dyc * gb * zb * (sg * (1.0 + zc * (1.0 - sg)))).astype(BF16)
        dcb_o[...] += jnp.sum(dzc, axis=0, keepdims=True)
        dcw_o[0:1, :] += jnp.sum(dzc * t_up, axis=0, keepdims=True)
        dcw_o[1:2, :] += jnp.sum(dzc * tt, axis=0, keepdims=True)
        dcw_o[2:3, :] += jnp.sum(dzc * t_dn, axis=0, keepdims=True)

    row = lambda w, j=0: pl.BlockSpec((tm, w), lambda i: (i, j))
    full = lambda shp: pl.BlockSpec(shp, lambda i: (0,) * len(shp))
    prev = pl.BlockSpec((hb, 2048), lambda i: (jnp.maximum(i * (tm // hb) - 1, 0), 0))
    nxt = pl.BlockSpec((hb, 2048), lambda i: (jnp.minimum((i + 1) * (tm // hb), s // hb - 1), 0))
    cls = lambda d, dt: jax.ShapeDtypeStruct((d, s // d, 512), dt)
    cspecs = [_class_spec(d, 512, tm) for d in DILATIONS]
    return pl.pallas_call(
        body, name="combine", grid=(nt,),
        out_shape=(jax.ShapeDtypeStruct((s, 1024), BF16), jax.ShapeDtypeStruct((s, 1024), F32),
                   cls(1, F32), cls(1, BF16), jax.ShapeDtypeStruct((s, 512), BF16),
                   jax.ShapeDtypeStruct((s, 1024), BF16), jax.ShapeDtypeStruct((s, 512), BF16),
                   jax.ShapeDtypeStruct((1, 128), F32), jax.ShapeDtypeStruct((1, 512), F32),
                   jax.ShapeDtypeStruct((8, 512), F32),
                   cls(4, BF16), cls(4, F32), cls(16, BF16), cls(16, F32)),
        in_specs=cspecs + cspecs + [row(2048), prev, nxt, row(512, 1), row(1024), row(1024),
                                    _resident((1024, 1024)), full((8, 512)), full((1, 512)), full((256, 256))],
        out_specs=(row(1024), row(1024), cspecs[0], cspecs[0], row(512), row(1024), row(512),
                   full((1, 128)), full((1, 512)), full((8, 512)),
                   cspecs[1], cspecs[1], cspecs[2], cspecs[2]),
        scratch_shapes=[pltpu.VMEM((4, tm, 128), F32)],
        compiler_params=_params(("arbitrary",)),
    )(*o_g, *lse_g, cg, cg, cg, vz, x, tgt, wout, cw, cb, b256)


def _attn_bwd(qkn_l, v_l, do_l, ld_l, bias, name):
    r_cls, length, _ = qkn_l.shape
    qt = min(length, 1024)
    nb, nsub, nt = length // QB, qt // QB, length // qt
    nstage = 2 if length <= 4096 else 1

    def body(q_ref, k_ref, v_ref, do_ref, ld_ref, b_ref, dq_ref, dkv_hbm, dsum_ref, dk_acc, dv_acc, stage, sems):
        p_id, r, t = pl.program_id(0), pl.program_id(1), pl.program_id(2)
        lo = lax.broadcasted_iota(jnp.int32, (QB, 128), 1) < HEAD_DIM

        @pl.when(t == 0)
        def _():
            dk_acc[...] = jnp.zeros_like(dk_acc)
            dv_acc[...] = jnp.zeros_like(dv_acc)

        @pl.when((t == 0) & (r == 0))
        def _():
            dsum_ref[...] = jnp.zeros_like(dsum_ref)

        nt_dims = (((1,), (1,)), ((), ()))
        tn_dims = (((0,), (0,)), ((), ()))
        coords, qqs, dds, logits, dps, lcols, dcols = [], [], [], [], [], [], []
        for i in range(nsub):
            q0, start, variant = _block_coords(t, i, nsub, nb, length)
            qq = _split_heads(q_ref[q0:q0 + QB, :], lo)
            dd = _split_heads(do_ref[q0:q0 + QB, :], lo)
            k = k_ref[pl.ds(start, KB), :]
            v = v_ref[pl.ds(start, KB), :]
            logits.append(lax.dot_general(qq, k, nt_dims, preferred_element_type=F32) + b_ref[variant])
            dps.append(lax.dot_general(dd, v, nt_dims, preferred_element_type=F32))
            lcols.append(_col_pair(ld_ref, q0, 0))
            dcols.append(_col_pair(ld_ref, q0, HEAD_DIM // 2))
            coords.append((q0, start, variant))
            qqs.append(qq)
            dds.append(dd)
        p = jnp.exp(jnp.concatenate(logits, axis=0) - jnp.concatenate(lcols, axis=0))
        ds = p * (jnp.concatenate(dps, axis=0) - jnp.concatenate(dcols, axis=0))
        pb = p.astype(BF16)
        dsb = ds.astype(BF16)
        middle = None
        for i in range(nsub):
            q0, start, variant = coords[i]
            rows = slice(2 * QB * i, 2 * QB * (i + 1))
            if 0 < i < nsub - 1:
                middle = ds[rows] if middle is None else middle + ds[rows]
            else:
                dsum_ref[variant] += ds[rows]
            dqq = jnp.dot(dsb[rows], k_ref[pl.ds(start, KB), :], preferred_element_type=F32)
            dq_ref[q0:q0 + QB, :] = jnp.where(lo, dqq[0:QB], dqq[QB:2 * QB]).astype(BF16)
            dk_acc[pl.ds(start, KB), :] += lax.dot_general(dsb[rows], qqs[i], tn_dims, preferred_element_type=F32)
            dv_acc[pl.ds(start, KB), :] += lax.dot_general(pb[rows], dds[i], tn_dims, preferred_element_type=F32)
        if middle is not None:
            dsum_ref[1] += middle

        @pl.when(t == nt - 1)
        def _():
            copies = []
            for which, acc in enumerate((dk_acc, dv_acc)):
                buf = stage.at[which % nstage]
                buf[...] = acc[...].astype(BF16)
                cp = pltpu.make_async_copy(buf, dkv_hbm.at[r, p_id, which], sems.at[which])
                cp.start()
                if nstage == 1:
                    cp.wait()
                else:
                    copies.append(cp)
            for cp in copies:
                cp.wait()

    qspec = pl.BlockSpec((None, qt, 128), lambda p, r, t: (r, t, p))
    return pl.pallas_call(
        body, name=name, grid=(N_PAIR, r_cls, nt),
        out_shape=(jax.ShapeDtypeStruct((r_cls, length, 512), BF16),
                   jax.ShapeDtypeStruct((r_cls, N_PAIR, 2, length, 128), BF16),
                   jax.ShapeDtypeStruct((N_PAIR, 3, 2 * QB, KB), F32)),
        in_specs=[qspec,
                  pl.BlockSpec((None, length, 128), lambda p, r, t: (r, 0, 4 + p)),
                  pl.BlockSpec((None, length, 128), lambda p, r, t: (r, 0, p)),
                  qspec, qspec,
                  pl.BlockSpec((3, None, 2 * QB, KB), lambda p, r, t: (0, p, 0, 0))],
        out_specs=(qspec, pl.BlockSpec(memory_space=pl.ANY),
                   pl.BlockSpec((None, 3, 2 * QB, KB), lambda p, r, t: (p, 0, 0, 0))),
        scratch_shapes=[pltpu.VMEM((length, 128), F32), pltpu.VMEM((length, 128), F32),
                        pltpu.VMEM((nstage, length, 128), BF16), pltpu.SemaphoreType.DMA((2,))],
        compiler_params=_params(("arbitrary", "arbitrary", "arbitrary")),
    )(qkn_l, qkn_l, v_l, do_l, ld_l, bias)


def _bwd_tail(dq_g, dkv_g, qkr, qkw, dza, dgbz, dzc, cg, cw, wblk, x, norm_w, dout, b256):
    s = x.shape[0]
    tm = TM_COMBINE
    hb = 16
    nt = s // tm

    def body(dq1, dq4, dq16, dkv1, dkv4, dkv16, qkr_ref, qkw_ref, dza_ref, dgbz_ref, dzc_ref,
             dzp_ref, dzn_ref, u_ref, gc_ref, cw_ref, w_ref, x_ref, nw_ref, dout_ref, b_ref,
             gx_o, dproj_o, dnw_o, dqkw_o, slab):
        i = pl.program_id(0)

        def nat_q(ref, d):
            return _gather_classes(slab, lambda r, j: ref[r, :, 128 * j:128 * (j + 1)], 4, d)

        def nat_kv(ref, d, which):
            return _gather_classes(slab, lambda r, j: ref[r, j, which], 4, d)

        @pl.when(i == 0)
        def _():
            dnw_o[...] = jnp.zeros_like(dnw_o)
            dqkw_o[...] = jnp.zeros_like(dqkw_o)

        dzc = dzc_ref[...].astype(F32)
        d_prev = jnp.where(i == 0, 0.0, dzp_ref[hb - 1:hb, :].astype(F32))
        d_next = jnp.where(i == nt - 1, 0.0, dzn_ref[0:1, :].astype(F32))
        rows = lax.broadcasted_iota(jnp.int32, (tm, 512), 0)
        d_up = jnp.where(rows == 0, d_prev, pltpu.roll(dzc, 1, 0))
        d_dn = jnp.where(rows == tm - 1, d_next, pltpu.roll(dzc, tm - 1, 0))
        dt = cw_ref[0:1, :] * d_dn + cw_ref[1:2, :] * dzc + cw_ref[2:3, :] * d_up
        u = u_ref[...].astype(F32)
        gc = gc_ref[...].astype(F32)
        dproj_o[:, 0:512] = (dt * gc).astype(BF16)
        dproj_o[:, 512:1024] = dgbz_ref[:, 0:512]
        dproj_o[:, 1024:1536] = (dt * u).astype(BF16)
        dproj_o[:, 1536:2048] = dgbz_ref[:, 512:1024]

        dqn = (dq1[0].astype(F32) + nat_q(dq4, 4) + nat_q(dq16, 16)) * (1.0 / 8.0)
        dk1 = jnp.concatenate([dkv1[0, j, 0] for j in range(N_PAIR)], axis=1)
        dv1 = jnp.concatenate([dkv1[0, j, 1] for j in range(N_PAIR)], axis=1)
        dkn = dk1 + nat_kv(dkv4, 4, 0) + nat_kv(dkv16, 16, 0)
        dvn = dv1 + nat_kv(dkv4, 4, 1) + nat_kv(dkv16, 16, 1)
        g = jnp.concatenate([dqn, dkn], axis=1) * qkw_ref[...]
        raw = qkr_ref[...].astype(F32)
        rr = lax.rsqrt(_group_sum(raw * raw, b_ref, split=False) * (1.0 / HEAD_DIM) + EPS)
        proj_gq = _group_sum(g * raw, b_ref) * (1.0 / HEAD_DIM)
        draw = rr * g - raw * (rr * rr * rr) * proj_gq
        dqkw_o[...] += jnp.sum(jnp.concatenate([dqn, dkn], axis=1) * raw * rr, axis=0, keepdims=True)
        dproj_o[:, 2048:3072] = draw.astype(BF16)
        dproj_o[:, 3072:3584] = dvn.astype(BF16)
        dproj_o[:, 3584:4096] = dza_ref[...]

        nt_dims = (((1,), (1,)), ((), ()))
        dh = lax.dot_general(dproj_o[:, 0:1024], w_ref[0], nt_dims, preferred_element_type=F32)
        for b in range(1, 4):
            dh += lax.dot_general(dproj_o[:, 1024 * b:1024 * b + 1024], w_ref[b], nt_dims,
                                  preferred_element_type=F32)

        xf = x_ref[...]
        r = lax.rsqrt(jnp.mean(xf * xf, axis=-1, keepdims=True) + EPS)
        gh = dh * nw_ref[...]
        dnw_o[...] += jnp.sum(dh * xf * r, axis=0, keepdims=True)
        mean_gx = jnp.mean(gh * xf, axis=-1, keepdims=True)
        gx_o[...] = dout_ref[...] + r * gh - xf * (r * r * r) * mean_gx

    row = lambda w, j=0: pl.BlockSpec((tm, w), lambda i: (i, j))
    full = lambda shp: pl.BlockSpec(shp, lambda i: (0,) * len(shp))
    prev = pl.BlockSpec((hb, 512), lambda i: (jnp.maximum(i * (tm // hb) - 1, 0), 0))
    nxt = pl.BlockSpec((hb, 512), lambda i: (jnp.minimum((i + 1) * (tm // hb), s // hb - 1), 0))
    return pl.pallas_call(
        body, name="bwd_tail", grid=(nt,),
        out_shape=(jax.ShapeDtypeStruct((s, 1024), F32), jax.ShapeDtypeStruct((s, 4096), BF16),
                   jax.ShapeDtypeStruct((1, 1024), F32), jax.ShapeDtypeStruct((1, 1024), F32)),
        in_specs=[_class_spec(d, 512, tm) for d in DILATIONS]
        + [pl.BlockSpec((d, N_PAIR, 2, tm // d, 128), lambda i: (0, 0, 0, i, 0)) for d in DILATIONS]
        + [row(1024), full((1, 1024)), row(512), row(1024), row(512), prev, nxt,
           row(512, 0), row(512, 2), full((8, 512)), _resident((4, 1024, 1024)), row(1024),
           full((1, 1024)), row(1024), full((256, 256))],
        out_specs=(row(1024), row(4096), full((1, 1024)), full((1, 1024))),
        scratch_shapes=[pltpu.VMEM((4, tm, 128), F32)],
        compiler_params=_params(("arbitrary",)),
    )(*dq_g, *dkv_g, qkr, qkw, dza, dgbz, dzc, dzc, dzc, cg, cg, cw, wblk, x, norm_w, dout, b256)


def _wgrad(a, b, row_blocked, name):
    s, m = a.shape
    n = b.shape[1]
    tk = 1024
    ncol = min(n, 2048)
    nj, nk = n // ncol, s // tk

    def body(a_ref, b_ref, o_ref, acc):
        kk = pl.program_id(1)

        @pl.when(kk == 0)
        def _():
            acc[...] = jnp.zeros_like(acc)

        acc[...] += lax.dot_general(a_ref[...], b_ref[...].astype(BF16), (((0,), (0,)), ((), ())),
                                    preferred_element_type=F32)

        @pl.when(kk == nk - 1)
        def _():
            blocks, _, rows, _ = o_ref.shape
            for blk in range(blocks):
                for half in range(2):
                    if row_blocked:
                        r0 = (2 * blk + half) * rows
                        o_ref[blk, half] = acc[r0:r0 + rows, :].astype(BF16)
                    else:
                        o_ref[blk, half] = acc[half * rows:(half + 1) * rows,
                                               1024 * blk:1024 * (blk + 1)].astype(BF16)

    if row_blocked:
        out_shape = jax.ShapeDtypeStruct((4, 2, m // 8, 1024), BF16)
        out_spec = pl.BlockSpec((4, 2, m // 8, 1024), lambda j, k: (0, 0, 0, 0))
    else:
        out_shape = jax.ShapeDtypeStruct((n // 1024, 2, m // 2, 1024), BF16)
        out_spec = pl.BlockSpec((ncol // 1024, 2, m // 2, 1024), lambda j, k: (j, 0, 0, 0))
    return pl.pallas_call(
        body, name=name, grid=(nj, nk),
        out_shape=out_shape,
        in_specs=[pl.BlockSpec((tk, m), lambda j, k: (k, 0)), pl.BlockSpec((tk, ncol), lambda j, k: (k, j))],
        out_specs=out_spec,
        scratch_shapes=[pltpu.VMEM((m, ncol), F32)],
        compiler_params=_params(("parallel", "arbitrary")),
    )(a, b)


def _dbias(dsum_all, onehot_all):
    def body(ds_ref, oh_ref, o_ref):
        step = pl.program_id(0) * 3 + pl.program_id(1)

        @pl.when(step == 0)
        def _():
            o_ref[...] = jnp.zeros_like(o_ref)

        rowq = lax.broadcasted_iota(jnp.int32, (2 * QB, KB), 0) & (QB - 1)
        hrow = lax.broadcasted_iota(jnp.int32, (8, KB), 0)
        diag = jnp.zeros((8, KB), F32)
        for p in range(N_PAIR):
            y = ds_ref[p]
            for bit in range(7):
                sh = 1 << bit
                y = jnp.where((rowq & sh) != 0, pltpu.roll(y, KB - sh, 1), y)
            da = jnp.sum(y[0:QB], axis=0, keepdims=True)
            db = jnp.sum(y[QB:2 * QB], axis=0, keepdims=True)
            diag = jnp.where(hrow == 2 * p, da, diag)
            diag = jnp.where(hrow == 2 * p + 1, db, diag)
        o_ref[...] += jnp.dot(diag, oh_ref[...], preferred_element_type=F32, precision=lax.Precision.HIGHEST)

    return pl.pallas_call(
        body, name="dbias", grid=(3, 3),
        out_shape=jax.ShapeDtypeStruct((8, 128), F32),
        in_specs=[pl.BlockSpec((None, N_PAIR, None, 2 * QB, KB), lambda g, v: (g, 0, v, 0, 0)),
                  pl.BlockSpec((None, None, KB, 128), lambda g, v: (g, v, 0, 0))],
        out_specs=pl.BlockSpec((8, 128), lambda g, v: (0, 0)),
        compiler_params=_params(("arbitrary", "arbitrary")),
    )(dsum_all, onehot_all)


def _gsync(pw_in, pw_out, small):
    hin, hout = pw_in.shape[2], pw_out.shape[2]
    nsmall = small.shape[0]

    def body(pin_hbm, pout_hbm, small_ref, gin_o, gout_o, small_o,
             mine_in, recv_in, sbuf_in, rbuf_in, mine_out, recv_out, sbuf_out, rbuf_out, gather,
             lsem, asend, arecv, bsend, brecv, csend, crecv, ssend, srecv):
        x, y, c = lax.axis_index("x"), lax.axis_index("y"), lax.axis_index("c")
        b = 2 * x + y
        dev = 4 * x + 2 * y + c
        sib = (x, y, 1 - c)

        def rcopy(src, dst, ssem, rsem, to):
            return pltpu.make_async_remote_copy(src_ref=src, dst_ref=dst, send_sem=ssem, recv_sem=rsem,
                                                device_id=to, device_id_type=MESH)

        gather[dev] = small_ref[...]
        s_sends = []
        for k in range(1, 8):
            to = (x ^ (k >> 2), y ^ ((k >> 1) & 1), c ^ (k & 1))
            cp = rcopy(gather.at[dev], gather.at[dev], ssend.at[k - 1], srecv.at[k - 1], to)
            cp.start()
            s_sends.append(cp)

        a_in = rcopy(pin_hbm.at[:, 1 - c], recv_in, asend.at[0], arecv.at[0], sib)
        a_out = rcopy(pout_hbm.at[:, 1 - c], recv_out, asend.at[1], arecv.at[1], sib)
        a_in.start()
        a_out.start()
        l_in = pltpu.make_async_copy(pin_hbm.at[:, c], mine_in, lsem.at[0])
        l_out = pltpu.make_async_copy(pout_hbm.at[:, c], mine_out, lsem.at[1])
        l_in.start()
        l_out.start()
        l_in.wait()
        l_out.wait()

        def stage_b(a_cp, mine, recv, sbuf, rbuf, base):
            a_cp.wait_recv()
            sends = []
            for k in (1, 2, 3):
                bk = b ^ k
                sbuf[k - 1] = (mine[bk].astype(F32) + recv[bk].astype(F32)).astype(BF16)
                cp = rcopy(sbuf.at[k - 1], rbuf.at[k - 1], bsend.at[base + k - 1], brecv.at[base + k - 1],
                           (x ^ (k >> 1), y ^ (k & 1), c))
                cp.start()
                sends.append(cp)
            return sends

        b_in = stage_b(a_in, mine_in, recv_in, sbuf_in, rbuf_in, 0)
        b_out = stage_b(a_out, mine_out, recv_out, sbuf_out, rbuf_out, 3)

        def stage_c(b_sends, mine, recv, rbuf, g_o, half, idx):
            acc = mine[b].astype(F32) + recv[b].astype(F32)
            for k in (1, 2, 3):
                b_sends[k - 1].wait_recv()
                acc = acc + rbuf[k - 1].astype(F32)
            rows = g_o.at[pl.ds(pl.multiple_of(c * half, half), half), :]
            g_o[pl.ds(pl.multiple_of(c * half, half), half), :] = acc
            cp = rcopy(rows, rows, csend.at[idx], crecv.at[idx], sib)
            cp.start()
            return cp

        c_in = stage_c(b_in, mine_in, recv_in, rbuf_in, gin_o, hin, 0)
        c_out = stage_c(b_out, mine_out, recv_out, rbuf_out, gout_o, hout, 1)

        for cp in s_sends:
            cp.wait_recv()
        tot = gather[0]
        for d in range(1, 8):
            tot = tot + gather[d]
        small_o[...] = tot

        for g_o, half, idx in ((gin_o, hin, 0), (gout_o, hout, 1)):
            other = g_o.at[pl.ds(pl.multiple_of((1 - c) * half, half), half), :]
            rcopy(other, other, csend.at[idx], crecv.at[idx], sib).wait_recv()
        for cp in s_sends + [a_in, a_out] + b_in + b_out + [c_in, c_out]:
            cp.wait_send()

    vm = pl.BlockSpec(memory_space=pltpu.VMEM)
    hbm = pl.BlockSpec(memory_space=pl.ANY)
    return pl.pallas_call(
        body, name="gsync",
        out_shape=(jax.ShapeDtypeStruct((2 * hin, 1024), F32), jax.ShapeDtypeStruct((2 * hout, 1024), F32),
                   jax.ShapeDtypeStruct((nsmall, 128), F32)),
        in_specs=[hbm, hbm, vm], out_specs=(vm, vm, vm),
        scratch_shapes=[pltpu.VMEM((4, hin, 1024), BF16), pltpu.VMEM((4, hin, 1024), BF16),
                        pltpu.VMEM((3, hin, 1024), BF16), pltpu.VMEM((3, hin, 1024), BF16),
                        pltpu.VMEM((4, hout, 1024), BF16), pltpu.VMEM((4, hout, 1024), BF16),
                        pltpu.VMEM((3, hout, 1024), BF16), pltpu.VMEM((3, hout, 1024), BF16),
                        pltpu.VMEM((8, nsmall, 128), F32),
                        pltpu.SemaphoreType.DMA((2,)),
                        pltpu.SemaphoreType.DMA((2,)), pltpu.SemaphoreType.DMA((2,)),
                        pltpu.SemaphoreType.DMA((6,)), pltpu.SemaphoreType.DMA((6,)),
                        pltpu.SemaphoreType.DMA((2,)), pltpu.SemaphoreType.DMA((2,)),
                        pltpu.SemaphoreType.DMA((7,)), pltpu.SemaphoreType.DMA((7,))],
        compiler_params=_params(),
    )(pw_in, pw_out, small)


def _adamw_math(w, g, m, v):
    m = ADAM_B1 * m + (1.0 - ADAM_B1) * g
    v = ADAM_B2 * v + (1.0 - ADAM_B2) * (g * g)
    m_hat = m / (1.0 - ADAM_B1 ** ADAM_STEP)
    v_hat = v / (1.0 - ADAM_B2 ** ADAM_STEP)
    delta = -ADAM_LR * (m_hat / (jnp.sqrt(v_hat) + ADAM_EPS) + ADAM_WD * w)
    return delta, m, v


def _adamw(w, g, m, v, name):
    rows, cols = w.shape
    tr = 256 if rows % 256 == 0 else rows

    def body(w_ref, g_ref, m_ref, v_ref, d_o, m_o, v_o):
        d, m2, v2 = _adamw_math(w_ref[...], g_ref[...], m_ref[...], v_ref[...])
        d_o[...] = d
        m_o[...] = m2
        v_o[...] = v2

    spec = pl.BlockSpec((tr, cols), lambda i: (i, 0))
    shp = jax.ShapeDtypeStruct((rows, cols), F32)
    return pl.pallas_call(
        body, name=name, grid=(rows // tr,), out_shape=(shp, shp, shp),
        in_specs=[spec] * 4, out_specs=(spec, spec, spec),
        compiler_params=_params(("parallel",)),
    )(w, g, m, v)


def _fold_heads(dqkw):
    def body(x_ref, o_ref):
        xs = x_ref[...]
        sq = xs[0:1] + xs[1:2] + xs[2:3] + xs[3:4]
        sk = xs[4:5] + xs[5:6] + xs[6:7] + xs[7:8]
        both = jnp.concatenate([sq, sk], axis=0)
        o_ref[...] = both + pltpu.roll(both, HEAD_DIM, 1)

    vm = pl.BlockSpec(memory_space=pltpu.VMEM)
    return pl.pallas_call(body, name="fold_heads", out_shape=jax.ShapeDtypeStruct((2, 128), F32),
                          in_specs=[vm], out_specs=vm, compiler_params=_params())(dqkw)


def kernel(x, norm_w, w_in, conv_w, conv_b, q_norm_w, k_norm_w, rel_bias, w_out, loss_target, m_norm_w, m_w_in, m_conv_w, m_conv_b, m_q_norm_w, m_k_norm_w, m_rel_bias, m_w_out, v_norm_w, v_w_in, v_conv_w, v_conv_b, v_q_norm_w, v_k_norm_w, v_rel_bias, v_w_out):
    x2 = x[0]
    tgt = loss_target[0]
    blk = 2 * lax.axis_index("x") + lax.axis_index("y")

    conv_w8 = jnp.pad(conv_w, ((0, 5), (0, 0)))
    wblk, woutblk, cwblk = _wgather(w_in, w_out, conv_w8)
    wout_full = woutblk.reshape(1024, 1024)
    cw_full = cwblk.transpose(1, 0, 2).reshape(8, 512)

    qkw = jnp.concatenate([jnp.tile(q_norm_w, 8) * 0.125, jnp.tile(k_norm_w, 8)])[None, :]
    qkw_raw = jnp.concatenate([jnp.tile(q_norm_w, 8), jnp.tile(k_norm_w, 8)])[None, :]
    gidx = jnp.arange(256) // HEAD_DIM
    b256 = (gidx[:, None] == gidx[None, :]).astype(BF16)

    h, cg, qkr, qkn, vz, qkn4, v4, qkn16, v16 = _proj(x2, norm_w[None, :], wblk, qkw, b256)

    biases = _bias_tables(rel_bias)
    qkn_l = [qkn[None], qkn4, qkn16]
    v_l = [vz[None], v4, v16]
    o_g, lse_g = [], []
    for gi, d in enumerate(DILATIONS):
        o_l, lse_l = _attn_fwd(qkn_l[gi], v_l[gi], biases[gi], f"attn_fwd_d{d}")
        o_g.append(o_l)
        lse_g.append(lse_l)

    (y, dout, ld1, do1, dza, dgbz, dzc, loss_p, dcb, dcw, do4, ld4, do16, ld16) = _combine(
        o_g, lse_g, cg, vz, x2, tgt, wout_full, cw_full, conv_b[None, :], b256)

    dq_g, dkv_g, dsums = [], [], []
    for gi, (d, do_l, ld_l) in enumerate(zip(DILATIONS, (do1, do4, do16), (ld1, ld4, ld16))):
        dq_l, dkv_l, dsum = _attn_bwd(qkn_l[gi], v_l[gi], do_l, ld_l, biases[gi], f"attn_bwd_d{d}")
        dq_g.append(dq_l)
        dkv_g.append(dkv_l)
        dsums.append(dsum)

    grad_x, dproj, dnw, dqkw = _bwd_tail(dq_g, dkv_g, qkr, qkw_raw, dza, dgbz, dzc, cg, cw_full, wblk,
                                         x2, norm_w[None, :], dout, b256)

    pw_in = _wgrad(h, dproj, False, "wgrad_in")
    pw_out = _wgrad(y, dout, True, "wgrad_out")
    dbias8 = _dbias(jnp.stack(dsums, axis=0), jnp.stack([_diag_bucket_onehot(d) for d in DILATIONS], axis=0))

    small = jnp.concatenate([dnw.reshape(8, 128), dcb.reshape(4, 128), dqkw.reshape(8, 128),
                             dcw[0:3].reshape(12, 128), dbias8], axis=0)
    g_win, g_wout, gsmall = _gsync(pw_in, pw_out, small)

    g_nw = gsmall[0:8].reshape(1024)
    g_cb = gsmall[8:12].reshape(512)
    folded = _fold_heads(gsmall[12:20])
    g_qw, g_kw = folded[0, 0:64], folded[1, 0:64]
    g_cw = lax.dynamic_slice(gsmall[20:32].reshape(3, 512), (0, blk * 128), (3, 128))
    g_rb = gsmall[32:40][:, 0:32].T

    loss = lax.psum(loss_p[0, 0], ("x", "y", "c"))

    d_win, nm_win, nv_win = _adamw(w_in, g_win, m_w_in, v_w_in, "adamw_w_in")
    d_wout, nm_wout, nv_wout = _adamw(w_out, g_wout, m_w_out, v_w_out, "adamw_w_out")

    def pack(parts):
        rows = [parts[0].reshape(8, 128), parts[1].reshape(4, 128),
                jnp.pad(parts[2], (0, 64))[None, :], jnp.pad(parts[3], (0, 64))[None, :],
                parts[4], jnp.pad(parts[5].T, ((0, 0), (0, 96)))]
        return jnp.concatenate(rows, axis=0)

    ws = pack([norm_w, conv_b, q_norm_w, k_norm_w, conv_w, rel_bias])
    gs = pack([g_nw, g_cb, g_qw, g_kw, g_cw, g_rb])
    ms = pack([m_norm_w, m_conv_b, m_q_norm_w, m_k_norm_w, m_conv_w, m_rel_bias])
    vs = pack([v_norm_w, v_conv_b, v_q_norm_w, v_k_norm_w, v_conv_w, v_rel_bias])
    rpad = lambda a: jnp.pad(a, ((0, 7), (0, 0)))
    d_s, nm_s, nv_s = _adamw(rpad(ws), rpad(gs), rpad(ms), rpad(vs), "adamw_small")

    def unpack(a):
        return (a[0:8].reshape(1024), a[12:13, 0:64].reshape(64), a[13:14, 0:64].reshape(64),
                a[14:17], a[8:12].reshape(512), a[17:25, 0:32].T)

    def ordered(nw, win, cw, cb, qw, kw, rb, wout):
        return (nw, win, cw, cb, qw, kw, rb, wout)

    g_un = (g_nw, g_qw, g_kw, g_cw, g_cb, g_rb)
    outs = [loss, grad_x[None]]
    for un, win_v, wout_v in ((g_un, g_win, g_wout), (unpack(d_s), d_win, d_wout),
                              (unpack(nm_s), nm_win, nm_wout), (unpack(nv_s), nv_win, nv_wout)):
        nw, qw, kw, cw, cb, rb = un
        outs.extend(ordered(nw, win_v, cw, cb, qw, kw, rb, wout_v))
    return tuple(outs)
```

```python
import math

import jax
import jax.numpy as jnp
from jax import lax
from jax.experimental import pallas as pl
from jax.experimental.pallas import tpu as pltpu

F32 = jnp.float32
BF16 = jnp.bfloat16
MESH = pl.DeviceIdType.MESH

D_MODEL = 1024
CONV_W = 512
ATTN_W = 512
HEAD_DIM = 64
N_PAIR = 4
DILATIONS = (1, 4, 16)
HALF = 64
QB = 128
KB = QB + 2 * HALF
NUM_BUCKETS = 32
MAX_DISTANCE = 1024
EPS = 1e-6
NEG = -1e30
ADAM_LR, ADAM_B1, ADAM_B2, ADAM_EPS, ADAM_WD, ADAM_STEP = 0.001, 0.9, 0.999, 1e-08, 0.01, 10
VMEM_LIMIT = 48 << 20


def _params(sem=None, vmem=VMEM_LIMIT, **kw):
    if sem is not None:
        kw["dimension_semantics"] = sem
    return pltpu.CompilerParams(vmem_limit_bytes=vmem, **kw)


def _sigmoid(z):
    return 1.0 / (1.0 + jnp.exp(-z))


def _group_sum(val, b_ref, split=True):
    hi = val.astype(BF16)
    lo = (val - hi.astype(F32)).astype(BF16) if split else None
    outs = []
    for j in range(val.shape[1] // 256):
        sl = slice(256 * j, 256 * j + 256)
        part = jnp.dot(hi[:, sl], b_ref[...], preferred_element_type=F32)
        if split:
            part = part + jnp.dot(lo[:, sl], b_ref[...], preferred_element_type=F32)
        outs.append(part)
    return outs[0] if len(outs) == 1 else jnp.concatenate(outs, axis=1)


def _t5_bucket(rel):
    half_b = NUM_BUCKETS // 2
    max_exact = half_b // 2
    ret = jnp.where(rel > 0, half_b, 0)
    n = jnp.abs(rel)
    nf = jnp.maximum(n, 1).astype(F32)
    large = max_exact + (jnp.log(nf / max_exact) / math.log(MAX_DISTANCE / max_exact)
                         * (half_b - max_exact)).astype(jnp.int32)
    large = jnp.minimum(large, half_b - 1)
    return ret + jnp.where(n < max_exact, n, large)


def _window_rel(variant):
    off = (0, HALF, 2 * HALF)[variant]
    return jnp.arange(KB)[None, :] - off - jnp.arange(QB)[:, None]


def _bias_tables(rel_bias):
    bkts = []
    for dilation in DILATIONS:
        for variant in range(3):
            rel = _window_rel(variant)
            bkt = _t5_bucket(jnp.clip(rel, -HALF, HALF) * dilation)
            bkts.append(jnp.where(jnp.abs(rel) <= HALF, bkt, -1))
    bkt_all = jnp.stack(bkts, axis=0).astype(jnp.int32)

    def body(rb_ref, bkt_ref, o_ref):
        bkt = bkt_ref[...]
        for h in range(8):
            acc = jnp.full((QB, KB), NEG, F32)
            for b in range(NUM_BUCKETS):
                acc = jnp.where(bkt == b, rb_ref[b, h], acc)
            o_ref[h] = acc

    out = pl.pallas_call(
        body, name="bias_tables", grid=(9,),
        out_shape=jax.ShapeDtypeStruct((9, 8, QB, KB), F32),
        in_specs=[pl.BlockSpec(memory_space=pltpu.SMEM), pl.BlockSpec((None, QB, KB), lambda i: (i, 0, 0))],
        out_specs=pl.BlockSpec((None, 8, QB, KB), lambda i: (i, 0, 0, 0)),
        compiler_params=_params(("parallel",)),
    )(rel_bias, bkt_all)
    return out.reshape(3, 3, N_PAIR, 2 * QB, KB)


def _diag_bucket_onehot(dilation):
    out = []
    c = jnp.arange(KB)
    for variant in range(3):
        off = (0, HALF, 2 * HALF)[variant]
        rel = ((c - off + 128) % 256) - 128
        band = jnp.abs(rel) <= HALF
        bkt = _t5_bucket(jnp.clip(rel, -HALF, HALF) * dilation)
        oh = (bkt[:, None] == jnp.arange(128)[None, :]) & band[:, None]
        out.append(oh.astype(F32))
    return jnp.stack(out, axis=0)


def _wgather(w_in, w_out, conv_w):
    rin, rout = w_in.shape[0] // 2, w_out.shape[0] // 2

    def body(win_ref, wout_ref, cw_ref, win_o, wout_o, cw_o, send_sems, recv_sems):
        x, y, c = lax.axis_index("x"), lax.axis_index("y"), lax.axis_index("c")
        b = 2 * x + y
        win_o[b] = win_ref[...].astype(BF16)
        wout_o[b] = wout_ref[...].astype(BF16)
        cw_o[b] = cw_ref[...]

        def peer(k):
            return (x ^ (k >> 1), y ^ (k & 1))

        def piece(ref, blk, half_rows, core):
            return ref.at[blk, pl.ds(core * half_rows, half_rows), :]

        def copy(sem, src, dst, to):
            return pltpu.make_async_remote_copy(src_ref=src, dst_ref=dst, send_sem=send_sems.at[sem],
                                                recv_sem=recv_sems.at[sem], device_id=to, device_id_type=MESH)

        sends = []
        for k in (1, 2, 3):
            px, py = peer(k)
            sends.append(copy(k - 1, piece(win_o, b, rin, c), piece(win_o, b, rin, c), (px, py, c)))
            sends.append(copy(3 + k - 1, piece(wout_o, b, rout, c), piece(wout_o, b, rout, c), (px, py, c)))
            sends.append(copy(6 + k - 1, cw_o.at[b], cw_o.at[b], (px, py, c)))
        for cp in sends:
            cp.start()
        fwd = []
        for k in (1, 2, 3):
            px, py = peer(k)
            bk = 2 * px + py
            copy(k - 1, piece(win_o, bk, rin, c), piece(win_o, bk, rin, c), (px, py, c)).wait_recv()
            f = copy(9 + k - 1, piece(win_o, bk, rin, c), piece(win_o, bk, rin, c), (x, y, 1 - c))
            f.start()
            fwd.append(f)
            copy(3 + k - 1, piece(wout_o, bk, rout, c), piece(wout_o, bk, rout, c), (px, py, c)).wait_recv()
            f = copy(12 + k - 1, piece(wout_o, bk, rout, c), piece(wout_o, bk, rout, c), (x, y, 1 - c))
            f.start()
            fwd.append(f)
            copy(6 + k - 1, cw_o.at[bk], cw_o.at[bk], (px, py, c)).wait_recv()
        for k in (1, 2, 3):
            px, py = peer(k)
            bk = 2 * px + py
            copy(9 + k - 1, piece(win_o, bk, rin, 1 - c), piece(win_o, bk, rin, 1 - c), (x, y, 1 - c)).wait_recv()
            copy(12 + k - 1, piece(wout_o, bk, rout, 1 - c), piece(wout_o, bk, rout, 1 - c), (x, y, 1 - c)).wait_recv()
        for cp in sends + fwd:
            cp.wait_send()

    vm = pl.BlockSpec(memory_space=pltpu.VMEM)
    return pl.pallas_call(
        body, name="wgather",
        out_shape=(jax.ShapeDtypeStruct((4,) + w_in.shape, BF16),
                   jax.ShapeDtypeStruct((4,) + w_out.shape, BF16),
                   jax.ShapeDtypeStruct((4,) + conv_w.shape, F32)),
        in_specs=[vm, vm, vm], out_specs=(vm, vm, vm),
        scratch_shapes=[pltpu.SemaphoreType.DMA((15,)), pltpu.SemaphoreType.DMA((15,))],
        compiler_params=_params(),
    )(w_in, w_out, conv_w)


TM_MATMUL = 512
TM_COMBINE = 256


def _resident(shape):
    return pl.BlockSpec(shape, lambda i: (0,) * len(shape), pipeline_mode=pl.Buffered(1))


def _to_slabs(slab, val, j0=0):
    for j in range(val.shape[1] // 128):
        slab[j0 + j] = val[:, 128 * j:128 * (j + 1)]


def _scatter_classes(slab, j0, nj, out_ref, d, part=0):
    n = slab.shape[1] // d
    for r in range(d):
        for j in range(nj):
            out_ref[r, part * n:(part + 1) * n, 128 * j:128 * (j + 1)] = (
                slab[j0 + j, pl.ds(r, n, stride=d), :].astype(out_ref.dtype))


def _gather_classes(slab, piece, nj, d):
    n = slab.shape[1] // d
    for r in range(d):
        for j in range(nj):
            slab[j, pl.ds(r, n, stride=d), :] = piece(r, j).astype(F32)
    return jnp.concatenate([slab[j] for j in range(nj)], axis=1)


def _class_spec(d, width, tm):
    return pl.BlockSpec((d, tm // d, width), lambda i: (0, i, 0))


def _proj(x, norm_w, wblk, qkw, b256):
    s = x.shape[0]
    tm = TM_MATMUL
    nparts = 2
    tp = tm // nparts

    def body(x_ref, nw_ref, w_ref, qkw_ref, b_ref, h_o, cg_o, qkr_o, qkn_o, vz_o, qkn4_o, v4_o, qkn16_o, v16_o, slabs):
        for part in range(nparts):
            rows = slice(part * tp, (part + 1) * tp)
            slab = slabs.at[part]
            xf = x_ref[rows, :]
            r = lax.rsqrt(jnp.mean(xf * xf, axis=-1, keepdims=True) + EPS)
            h = (xf * r * nw_ref[...]).astype(BF16)
            h_o[rows, :] = h
            p2 = jnp.dot(h, w_ref[2], preferred_element_type=F32)
            qkr_o[rows, :] = p2.astype(BF16)
            ss = _group_sum(p2 * p2, b_ref, split=False)
            rr = lax.rsqrt(ss * (1.0 / HEAD_DIM) + EPS)
            qkn = p2 * rr * qkw_ref[...]
            qkn_o[rows, :] = qkn.astype(BF16)
            _to_slabs(slab, qkn)
            p3 = jnp.dot(h, w_ref[3], preferred_element_type=F32)
            vz_o[rows, :] = p3.astype(BF16)
            _to_slabs(slab, p3[:, 0:512], 8)
            cg_o[rows, 0:1024] = jnp.dot(h, w_ref[0], preferred_element_type=F32).astype(BF16)
            cg_o[rows, 1024:2048] = jnp.dot(h, w_ref[1], preferred_element_type=F32).astype(BF16)
            for d, q_o, v_o in ((4, qkn4_o, v4_o), (16, qkn16_o, v16_o)):
                _scatter_classes(slab, 0, 8, q_o, d, part)
                _scatter_classes(slab, 8, 4, v_o, d, part)

    row = lambda w: pl.BlockSpec((tm, w), lambda i: (i, 0))
    full = lambda shp: pl.BlockSpec(shp, lambda i: (0,) * len(shp))
    nat = lambda w: jax.ShapeDtypeStruct((s, w), BF16)
    cls = lambda d, w: jax.ShapeDtypeStruct((d, s // d, w), BF16)
    return pl.pallas_call(
        body, name="proj", grid=(s // tm,),
        out_shape=(nat(1024), nat(2048), nat(1024), nat(1024), nat(1024),
                   cls(4, 1024), cls(4, 512), cls(16, 1024), cls(16, 512)),
        in_specs=[row(1024), full((1, 1024)), _resident((4, 1024, 1024)), full((1, 1024)), full((256, 256))],
        out_specs=(row(1024), row(2048), row(1024), row(1024), row(1024),
                   _class_spec(4, 1024, tm), _class_spec(4, 512, tm),
                   _class_spec(16, 1024, tm), _class_spec(16, 512, tm)),
        scratch_shapes=[pltpu.VMEM((nparts, 12, tp, 128), F32)],
        compiler_params=_params(("parallel",)),
    )(x, norm_w, wblk, qkw, b256)


def _block_coords(t, i, nsub, nb, length):
    n = t * nsub + i
    q0 = i * QB
    start = pl.multiple_of(jnp.clip(n * QB - HALF, 0, length - KB), HALF)
    variant = jnp.where(n == 0, 0, jnp.where(n == nb - 1, 2, 1))
    return q0, start, variant


def _split_heads(a, lo):
    zero = jnp.zeros_like(a)
    return jnp.concatenate([jnp.where(lo, a, zero), jnp.where(lo, zero, a)], axis=0)


def _col_pair(ref, q0, lane):
    return jnp.concatenate([ref[pl.ds(q0, QB), lane:lane + 1],
                            ref[pl.ds(q0, QB), HEAD_DIM + lane:HEAD_DIM + lane + 1]], axis=0)


def _attn_fwd(qkn_l, v_l, bias, name):
    r_cls, length, _ = qkn_l.shape
    qt = min(length, 1024)
    nb, nsub = length // QB, qt // QB

    def body(q_ref, k_ref, v_ref, b_ref, o_ref, lse_ref):
        t = pl.program_id(2)
        lo = lax.broadcasted_iota(jnp.int32, (QB, 128), 1) < HEAD_DIM

        starts, logits = [], []
        for i in range(nsub):
            _, start, variant = _block_coords(t, i, nsub, nb, length)
            qq = _split_heads(q_ref[i * QB:(i + 1) * QB, :], lo)
            k = k_ref[pl.ds(start, KB), :]
            logits.append(lax.dot_general(qq, k, (((1,), (1,)), ((), ())), preferred_element_type=F32)
                          + b_ref[variant])
            starts.append(start)
        lg = jnp.concatenate(logits, axis=0)
        m = jnp.max(lg, axis=-1, keepdims=True)
        p = jnp.exp(lg - m)
        pb = p.astype(BF16)
        l = jnp.sum(p, axis=-1, keepdims=True)
        lse = jnp.broadcast_to(m + jnp.log(l), (nsub * 2 * QB, 128))
        inv = 1.0 / l
        for i in range(nsub):
            rows = slice(2 * QB * i, 2 * QB * (i + 1))
            v = v_ref[pl.ds(starts[i], KB), :]
            pv = jnp.dot(pb[rows], v, preferred_element_type=F32) * inv[rows]
            o_ref[i * QB:(i + 1) * QB, :] = jnp.where(lo, pv[0:QB], pv[QB:2 * QB]).astype(BF16)
            ls = lse[rows]
            lse_ref[i * QB:(i + 1) * QB, :] = jnp.where(lo, ls[0:QB], ls[QB:2 * QB])

    return pl.pallas_call(
        body, name=name, grid=(r_cls, N_PAIR, length // qt),
        out_shape=(jax.ShapeDtypeStruct((r_cls, length, 512), BF16),
                   jax.ShapeDtypeStruct((r_cls, length, 512), F32)),
        in_specs=[pl.BlockSpec((None, qt, 128), lambda r, p, t: (r, t, p)),
                  pl.BlockSpec((None, length, 128), lambda r, p, t: (r, 0, 4 + p)),
                  pl.BlockSpec((None, length, 128), lambda r, p, t: (r, 0, p)),
                  pl.BlockSpec((3, None, 2 * QB, KB), lambda r, p, t: (0, p, 0, 0))],
        out_specs=(pl.BlockSpec((None, qt, 128), lambda r, p, t: (r, t, p)),
                   pl.BlockSpec((None, qt, 128), lambda r, p, t: (r, t, p))),
        compiler_params=_params(("parallel", "parallel", "arbitrary")),
    )(qkn_l, qkn_l, v_l, bias)


def _combine(o_g, lse_g, cg, vz, x, tgt, wout, cw, cb, b256):
    s = x.shape[0]
    tm = TM_COMBINE
    hb = 16
    nt = s // tm

    def body(o1, o4, o16, l1, l4, l16, cg_ref, cgp_ref, cgn_ref, za_ref, x_ref, t_ref, w_ref, cw_ref, cb_ref,
             b_ref, y_o, dout_o, ld1_o, do1_o, dza_o, dgbz_o, dzc_o, loss_o, dcb_o, dcw_o,
             do4_o, ld4_o, do16_o, ld16_o, slab):
        i = pl.program_id(0)

        @pl.when(i == 0)
        def _():
            loss_o[...] = jnp.zeros_like(loss_o)
            dcb_o[...] = jnp.zeros_like(dcb_o)
            dcw_o[...] = jnp.zeros_like(dcw_o)

        u = cg_ref[:, 0:512].astype(F32)
        gb = cg_ref[:, 512:1024].astype(F32)
        gc = cg_ref[:, 1024:1536].astype(F32)
        zc = cg_ref[:, 1536:2048].astype(F32)
        tt = gc * u
        t_prev = cgp_ref[hb - 1:hb, 0:512].astype(F32) * cgp_ref[hb - 1:hb, 1024:1536].astype(F32)
        t_next = cgn_ref[0:1, 0:512].astype(F32) * cgn_ref[0:1, 1024:1536].astype(F32)
        t_prev = jnp.where(i == 0, 0.0, t_prev)
        t_next = jnp.where(i == nt - 1, 0.0, t_next)
        rows = lax.broadcasted_iota(jnp.int32, (tm, 512), 0)
        t_up = jnp.where(rows == 0, t_prev, pltpu.roll(tt, 1, 0))
        t_dn = jnp.where(rows == tm - 1, t_next, pltpu.roll(tt, tm - 1, 0))
        w0, w1, w2 = cw_ref[0:1, :], cw_ref[1:2, :], cw_ref[2:3, :]
        zb = w0 * t_up + w1 * tt + w2 * t_dn + cb_ref[...]
        sg = _sigmoid(zc)
        sz = zc * sg
        y_conv = gb * zb * sz

        a1, p1 = l1[0], o1[0].astype(F32)
        a4 = _gather_classes(slab, lambda r, j: l4[r, :, 128 * j:128 * (j + 1)], 4, 4)
        p4 = _gather_classes(slab, lambda r, j: o4[r, :, 128 * j:128 * (j + 1)], 4, 4)
        a16 = _gather_classes(slab, lambda r, j: l16[r, :, 128 * j:128 * (j + 1)], 4, 16)
        p16 = _gather_classes(slab, lambda r, j: o16[r, :, 128 * j:128 * (j + 1)], 4, 16)
        m = jnp.maximum(jnp.maximum(a1, a4), a16)
        e1, e4, e16 = jnp.exp(a1 - m), jnp.exp(a4 - m), jnp.exp(a16 - m)
        den = e1 + e4 + e16
        lse = m + jnp.log(den)
        o = (e1 * p1 + e4 * p4 + e16 * p16) / den
        za = za_ref[...].astype(F32)
        sga = _sigmoid(za)
        sa = za * sga
        y = jnp.concatenate([y_conv, o * sa], axis=1).astype(BF16)
        y_o[...] = y

        out = x_ref[...] + jnp.dot(y, w_ref[...], preferred_element_type=F32)
        diff = out - t_ref[...]
        loss_o[...] += (0.5 / D_MODEL) * jnp.sum(diff * diff)
        dout = diff * (1.0 / D_MODEL)
        dout_o[...] = dout
        dy = lax.dot_general(dout.astype(BF16), w_ref[...], (((1,), (1,)), ((), ())), preferred_element_type=F32)
        dyc, dya = dy[:, 0:512], dy[:, 512:1024]

        do = dya * sa
        dza_o[...] = (dya * o * (sga * (1.0 + za * (1.0 - sga)))).astype(BF16)
        lane = lax.broadcasted_iota(jnp.int32, (tm, 512), 1)
        ld = jnp.where((lane & (HEAD_DIM - 1)) < HEAD_DIM // 2, lse, _group_sum(do * o, b_ref))
        do1_o[0] = do.astype(BF16)
        ld1_o[0] = ld
        _to_slabs(slab, do)
        _scatter_classes(slab, 0, 4, do4_o, 4)
        _scatter_classes(slab, 0, 4, do16_o, 16)
        _to_slabs(slab, ld)
        _scatter_classes(slab, 0, 4, ld4_o, 4)
        _scatter_classes(slab, 0, 4, ld16_o, 16)

        dzc = dyc * sz * gb
        dzc_o[...] = dzc.astype(BF16)
        dgbz_o[:, 0:512] = (dyc * sz * zb).astype(BF16)
        dgbz_o[:, 512:1024] = (dyc * gb * zb * (sg * (1.0 + zc * (1.0 - sg)))).astype(BF16)
        dcb_o[...] += jnp.sum(dzc, axis=0, keepdims=True)
        dcw_o[0:1, :] += jnp.sum(dzc * t_up, axis=0, keepdims=True)
        dcw_o[1:2, :] += jnp.sum(dzc * tt, axis=0, keepdims=True)
        dcw_o[2:3, :] += jnp.sum(dzc * t_dn, axis=0, keepdims=True)

    row = lambda w, j=0: pl.BlockSpec((tm, w), lambda i: (i, j))
    full = lambda shp: pl.BlockSpec(shp, lambda i: (0,) * len(shp))
    prev = pl.BlockSpec((hb, 2048), lambda i: (jnp.maximum(i * (tm // hb) - 1, 0), 0))
    nxt = pl.BlockSpec((hb, 2048), lambda i: (jnp.minimum((i + 1) * (tm // hb), s // hb - 1), 0))
    cls = lambda d, dt: jax.ShapeDtypeStruct((d, s // d, 512), dt)
    cspecs = [_class_spec(d, 512, tm) for d in DILATIONS]
    return pl.pallas_call(
        body, name="combine", grid=(nt,),
        out_shape=(jax.ShapeDtypeStruct((s, 1024), BF16), jax.ShapeDtypeStruct((s, 1024), F32),
                   cls(1, F32), cls(1, BF16), jax.ShapeDtypeStruct((s, 512), BF16),
                   jax.ShapeDtypeStruct((s, 1024), BF16), jax.ShapeDtypeStruct((s, 512), BF16),
                   jax.ShapeDtypeStruct((1, 128), F32), jax.ShapeDtypeStruct((1, 512), F32),
                   jax.ShapeDtypeStruct((8, 512), F32),
                   cls(4, BF16), cls(4, F32), cls(16, BF16), cls(16, F32)),
        in_specs=cspecs + cspecs + [row(2048), prev, nxt, row(512, 1), row(1024), row(1024),
                                    _resident((1024, 1024)), full((8, 512)), full((1, 512)), full((256, 256))],
        out_specs=(row(1024), row(1024), cspecs[0], cspecs[0], row(512), row(1024), row(512),
                   full((1, 128)), full((1, 512)), full((8, 512)),
                   cspecs[1], cspecs[1], cspecs[2], cspecs[2]),
        scratch_shapes=[pltpu.VMEM((4, tm, 128), F32)],
        compiler_params=_params(("arbitrary",)),
    )(*o_g, *lse_g, cg, cg, cg, vz, x, tgt, wout, cw, cb, b256)


def _attn_bwd(qkn_l, v_l, do_l, ld_l, bias, name):
    r_cls, length, _ = qkn_l.shape
    qt = min(length, 1024)
    nb, nsub, nt = length // QB, qt // QB, length // qt
    nstage = 2 if length <= 4096 else 1

    def body(q_ref, k_ref, v_ref, do_ref, ld_ref, b_ref, dq_ref, dkv_hbm, dsum_ref, dk_acc, dv_acc, stage, sems):
        p_id, r, t = pl.program_id(0), pl.program_id(1), pl.program_id(2)
        lo = lax.broadcasted_iota(jnp.int32, (QB, 128), 1) < HEAD_DIM

        @pl.when(t == 0)
        def _():
            dk_acc[...] = jnp.zeros_like(dk_acc)
            dv_acc[...] = jnp.zeros_like(dv_acc)

        @pl.when((t == 0) & (r == 0))
        def _():
            dsum_ref[...] = jnp.zeros_like(dsum_ref)

        nt_dims = (((1,), (1,)), ((), ()))
        tn_dims = (((0,), (0,)), ((), ()))
        coords, qqs, dds, logits, dps, lcols, dcols = [], [], [], [], [], [], []
        for i in range(nsub):
            q0, start, variant = _block_coords(t, i, nsub, nb, length)
            qq = _split_heads(q_ref[q0:q0 + QB, :], lo)
            dd = _split_heads(do_ref[q0:q0 + QB, :], lo)
            k = k_ref[pl.ds(start, KB), :]
            v = v_ref[pl.ds(start, KB), :]
            logits.append(lax.dot_general(qq, k, nt_dims, preferred_element_type=F32) + b_ref[variant])
            dps.append(lax.dot_general(dd, v, nt_dims, preferred_element_type=F32))
            lcols.append(_col_pair(ld_ref, q0, 0))
            dcols.append(_col_pair(ld_ref, q0, HEAD_DIM // 2))
            coords.append((q0, start, variant))
            qqs.append(qq)
            dds.append(dd)
        p = jnp.exp(jnp.concatenate(logits, axis=0) - jnp.concatenate(lcols, axis=0))
        ds = p * (jnp.concatenate(dps, axis=0) - jnp.concatenate(dcols, axis=0))
        pb = p.astype(BF16)
        dsb = ds.astype(BF16)
        middle = None
        for i in range(nsub):
            q0, start, variant = coords[i]
            rows = slice(2 * QB * i, 2 * QB * (i + 1))
            if 0 < i < nsub - 1:
                middle = ds[rows] if middle is None else middle + ds[rows]
            else:
                dsum_ref[variant] += ds[rows]
            dqq = jnp.dot(dsb[rows], k_ref[pl.ds(start, KB), :], preferred_element_type=F32)
            dq_ref[q0:q0 + QB, :] = jnp.where(lo, dqq[0:QB], dqq[QB:2 * QB]).astype(BF16)
            dk_acc[pl.ds(start, KB), :] += lax.dot_general(dsb[rows], qqs[i], tn_dims, preferred_element_type=F32)
            dv_acc[pl.ds(start, KB), :] += lax.dot_general(pb[rows], dds[i], tn_dims, preferred_element_type=F32)
        if middle is not None:
            dsum_ref[1] += middle

        @pl.when(t == nt - 1)
        def _():
            copies = []
            for which, acc in enumerate((dk_acc, dv_acc)):
                buf = stage.at[which % nstage]
                buf[...] = acc[...].astype(BF16)
                cp = pltpu.make_async_copy(buf, dkv_hbm.at[r, p_id, which], sems.at[which])
                cp.start()
                if nstage == 1:
                    cp.wait()
                else:
                    copies.append(cp)
            for cp in copies:
                cp.wait()

    qspec = pl.BlockSpec((None, qt, 128), lambda p, r, t: (r, t, p))
    return pl.pallas_call(
        body, name=name, grid=(N_PAIR, r_cls, nt),
        out_shape=(jax.ShapeDtypeStruct((r_cls, length, 512), BF16),
                   jax.ShapeDtypeStruct((r_cls, N_PAIR, 2, length, 128), BF16),
                   jax.ShapeDtypeStruct((N_PAIR, 3, 2 * QB, KB), F32)),
        in_specs=[qspec,
                  pl.BlockSpec((None, length, 128), lambda p, r, t: (r, 0, 4 + p)),
                  pl.BlockSpec((None, length, 128), lambda p, r, t: (r, 0, p)),
                  qspec, qspec,
                  pl.BlockSpec((3, None, 2 * QB, KB), lambda p, r, t: (0, p, 0, 0))],
        out_specs=(qspec, pl.BlockSpec(memory_space=pl.ANY),
                   pl.BlockSpec((None, 3, 2 * QB, KB), lambda p, r, t: (p, 0, 0, 0))),
        scratch_shapes=[pltpu.VMEM((length, 128), F32), pltpu.VMEM((length, 128), F32),
                        pltpu.VMEM((nstage, length, 128), BF16), pltpu.SemaphoreType.DMA((2,))],
        compiler_params=_params(("arbitrary", "arbitrary", "arbitrary")),
    )(qkn_l, qkn_l, v_l, do_l, ld_l, bias)


def _bwd_tail(dq_g, dkv_g, qkr, qkw, dza, dgbz, dzc, cg, cw, wblk, x, norm_w, dout, b256):
    s = x.shape[0]
    tm = TM_COMBINE
    hb = 16
    nt = s // tm

    def body(dq1, dq4, dq16, dkv1, dkv4, dkv16, qkr_ref, qkw_ref, dza_ref, dgbz_ref, dzc_ref,
             dzp_ref, dzn_ref, u_ref, gc_ref, cw_ref, w_ref, x_ref, nw_ref, dout_ref, b_ref,
             gx_o, dproj_o, dnw_o, dqkw_o, slab):
        i = pl.program_id(0)

        def nat_q(ref, d):
            return _gather_classes(slab, lambda r, j: ref[r, :, 128 * j:128 * (j + 1)], 4, d)

        def nat_kv(ref, d, which):
            return _gather_classes(slab, lambda r, j: ref[r, j, which], 4, d)

        @pl.when(i == 0)
        def _():
            dnw_o[...] = jnp.zeros_like(dnw_o)
            dqkw_o[...] = jnp.zeros_like(dqkw_o)

        dzc = dzc_ref[...].astype(F32)
        d_prev = jnp.where(i == 0, 0.0, dzp_ref[hb - 1:hb, :].astype(F32))
        d_next = jnp.where(i == nt - 1, 0.0, dzn_ref[0:1, :].astype(F32))
        rows = lax.broadcasted_iota(jnp.int32, (tm, 512), 0)
        d_up = jnp.where(rows == 0, d_prev, pltpu.roll(dzc, 1, 0))
        d_dn = jnp.where(rows == tm - 1, d_next, pltpu.roll(dzc, tm - 1, 0))
        dt = cw_ref[0:1, :] * d_dn + cw_ref[1:2, :] * dzc + cw_ref[2:3, :] * d_up
        u = u_ref[...].astype(F32)
        gc = gc_ref[...].astype(F32)
        dproj_o[:, 0:512] = (dt * gc).astype(BF16)
        dproj_o[:, 512:1024] = dgbz_ref[:, 0:512]
        dproj_o[:, 1024:1536] = (dt * u).astype(BF16)
        dproj_o[:, 1536:2048] = dgbz_ref[:, 512:1024]

        dqn = (dq1[0].astype(F32) + nat_q(dq4, 4) + nat_q(dq16, 16)) * (1.0 / 8.0)
        dk1 = jnp.concatenate([dkv1[0, j, 0] for j in range(N_PAIR)], axis=1)
        dv1 = jnp.concatenate([dkv1[0, j, 1] for j in range(N_PAIR)], axis=1)
        dkn = dk1 + nat_kv(dkv4, 4, 0) + nat_kv(dkv16, 16, 0)
        dvn = dv1 + nat_kv(dkv4, 4, 1) + nat_kv(dkv16, 16, 1)
        g = jnp.concatenate([dqn, dkn], axis=1) * qkw_ref[...]
        raw = qkr_ref[...].astype(F32)
        rr = lax.rsqrt(_group_sum(raw * raw, b_ref, split=False) * (1.0 / HEAD_DIM) + EPS)
        proj_gq = _group_sum(g * raw, b_ref) * (1.0 / HEAD_DIM)
        draw = rr * g - raw * (rr * rr * rr) * proj_gq
        dqkw_o[...] += jnp.sum(jnp.concatenate([dqn, dkn], axis=1) * raw * rr, axis=0, keepdims=True)
        dproj_o[:, 2048:3072] = draw.astype(BF16)
        dproj_o[:, 3072:3584] = dvn.astype(BF16)
        dproj_o[:, 3584:4096] = dza_ref[...]

        nt_dims = (((1,), (1,)), ((), ()))
        dh = lax.dot_general(dproj_o[:, 0:1024], w_ref[0], nt_dims, preferred_element_type=F32)
        for b in range(1, 4):
            dh += lax.dot_general(dproj_o[:, 1024 * b:1024 * b + 1024], w_ref[b], nt_dims,
                                  preferred_element_type=F32)

        xf = x_ref[...]
        r = lax.rsqrt(jnp.mean(xf * xf, axis=-1, keepdims=True) + EPS)
        gh = dh * nw_ref[...]
        dnw_o[...] += jnp.sum(dh * xf * r, axis=0, keepdims=True)
        mean_gx = jnp.mean(gh * xf, axis=-1, keepdims=True)
        gx_o[...] = dout_ref[...] + r * gh - xf * (r * r * r) * mean_gx

    row = lambda w, j=0: pl.BlockSpec((tm, w), lambda i: (i, j))
    full = lambda shp: pl.BlockSpec(shp, lambda i: (0,) * len(shp))
    prev = pl.BlockSpec((hb, 512), lambda i: (jnp.maximum(i * (tm // hb) - 1, 0), 0))
    nxt = pl.BlockSpec((hb, 512), lambda i: (jnp.minimum((i + 1) * (tm // hb), s // hb - 1), 0))
    return pl.pallas_call(
        body, name="bwd_tail", grid=(nt,),
        out_shape=(jax.ShapeDtypeStruct((s, 1024), F32), jax.ShapeDtypeStruct((s, 4096), BF16),
                   jax.ShapeDtypeStruct((1, 1024), F32), jax.ShapeDtypeStruct((1, 1024), F32)),
        in_specs=[_class_spec(d, 512, tm) for d in DILATIONS]
        + [pl.BlockSpec((d, N_PAIR, 2, tm // d, 128), lambda i: (0, 0, 0, i, 0)) for d in DILATIONS]
        + [row(1024), full((1, 1024)), row(512), row(1024), row(512), prev, nxt,
           row(512, 0), row(512, 2), full((8, 512)), _resident((4, 1024, 1024)), row(1024),
           full((1, 1024)), row(1024), full((256, 256))],
        out_specs=(row(1024), row(4096), full((1, 1024)), full((1, 1024))),
        scratch_shapes=[pltpu.VMEM((4, tm, 128), F32)],
        compiler_params=_params(("arbitrary",)),
    )(*dq_g, *dkv_g, qkr, qkw, dza, dgbz, dzc, dzc, dzc, cg, cg, cw, wblk, x, norm_w, dout, b256)


def _wgrad(a, b, row_blocked, name):
    s, m = a.shape
    n = b.shape[1]
    tk = 1024
    ncol = min(n, 2048)
    nj, nk = n // ncol, s // tk

    def body(a_ref, b_ref, o_ref, acc):
        kk = pl.program_id(1)

        @pl.when(kk == 0)
        def _():
            acc[...] = jnp.zeros_like(acc)

        acc[...] += lax.dot_general(a_ref[...], b_ref[...].astype(BF16), (((0,), (0,)), ((), ())),
                                    preferred_element_type=F32)

        @pl.when(kk == nk - 1)
        def _():
            blocks, _, rows, _ = o_ref.shape
            for blk in range(blocks):
                for half in range(2):
                    if row_blocked:
                        r0 = (2 * blk + half) * rows
                        o_ref[blk, half] = acc[r0:r0 + rows, :].astype(BF16)
                    else:
                        o_ref[blk, half] = acc[half * rows:(half + 1) * rows,
                                               1024 * blk:1024 * (blk + 1)].astype(BF16)

    if row_blocked:
        out_shape = jax.ShapeDtypeStruct((4, 2, m // 8, 1024), BF16)
        out_spec = pl.BlockSpec((4, 2, m // 8, 1024), lambda j, k: (0, 0, 0, 0))
    else:
        out_shape = jax.ShapeDtypeStruct((n // 1024, 2, m // 2, 1024), BF16)
        out_spec = pl.BlockSpec((ncol // 1024, 2, m // 2, 1024), lambda j, k: (j, 0, 0, 0))
    return pl.pallas_call(
        body, name=name, grid=(nj, nk),
        out_shape=out_shape,
        in_specs=[pl.BlockSpec((tk, m), lambda j, k: (k, 0)), pl.BlockSpec((tk, ncol), lambda j, k: (k, j))],
        out_specs=out_spec,
        scratch_shapes=[pltpu.VMEM((m, ncol), F32)],
        compiler_params=_params(("parallel", "arbitrary")),
    )(a, b)


def _dbias(dsum_all, onehot_all):
    def body(ds_ref, oh_ref, o_ref):
        step = pl.program_id(0) * 3 + pl.program_id(1)

        @pl.when(step == 0)
        def _():
            o_ref[...] = jnp.zeros_like(o_ref)

        rowq = lax.broadcasted_iota(jnp.int32, (2 * QB, KB), 0) & (QB - 1)
        hrow = lax.broadcasted_iota(jnp.int32, (8, KB), 0)
        diag = jnp.zeros((8, KB), F32)
        for p in range(N_PAIR):
            y = ds_ref[p]
            for bit in range(7):
                sh = 1 << bit
                y = jnp.where((rowq & sh) != 0, pltpu.roll(y, KB - sh, 1), y)
            da = jnp.sum(y[0:QB], axis=0, keepdims=True)
            db = jnp.sum(y[QB:2 * QB], axis=0, keepdims=True)
            diag = jnp.where(hrow == 2 * p, da, diag)
            diag = jnp.where(hrow == 2 * p + 1, db, diag)
        o_ref[...] += jnp.dot(diag, oh_ref[...], preferred_element_type=F32, precision=lax.Precision.HIGHEST)

    return pl.pallas_call(
        body, name="dbias", grid=(3, 3),
        out_shape=jax.ShapeDtypeStruct((8, 128), F32),
        in_specs=[pl.BlockSpec((None, N_PAIR, None, 2 * QB, KB), lambda g, v: (g, 0, v, 0, 0)),
                  pl.BlockSpec((None, None, KB, 128), lambda g, v: (g, v, 0, 0))],
        out_specs=pl.BlockSpec((8, 128), lambda g, v: (0, 0)),
        compiler_params=_params(("arbitrary", "arbitrary")),
    )(dsum_all, onehot_all)


def _gsync(pw_in, pw_out, small):
    hin, hout = pw_in.shape[2], pw_out.shape[2]
    nsmall = small.shape[0]

    def body(pin_hbm, pout_hbm, small_ref, gin_o, gout_o, small_o,
             mine_in, recv_in, sbuf_in, rbuf_in, mine_out, recv_out, sbuf_out, rbuf_out, gather,
             lsem, asend, arecv, bsend, brecv, csend, crecv, ssend, srecv):
        x, y, c = lax.axis_index("x"), lax.axis_index("y"), lax.axis_index("c")
        b = 2 * x + y
        dev = 4 * x + 2 * y + c
        sib = (x, y, 1 - c)

        def rcopy(src, dst, ssem, rsem, to):
            return pltpu.make_async_remote_copy(src_ref=src, dst_ref=dst, send_sem=ssem, recv_sem=rsem,
                                                device_id=to, device_id_type=MESH)

        gather[dev] = small_ref[...]
        s_sends = []
        for k in range(1, 8):
            to = (x ^ (k >> 2), y ^ ((k >> 1) & 1), c ^ (k & 1))
            cp = rcopy(gather.at[dev], gather.at[dev], ssend.at[k - 1], srecv.at[k - 1], to)
            cp.start()
            s_sends.append(cp)

        a_in = rcopy(pin_hbm.at[:, 1 - c], recv_in, asend.at[0], arecv.at[0], sib)
        a_out = rcopy(pout_hbm.at[:, 1 - c], recv_out, asend.at[1], arecv.at[1], sib)
        a_in.start()
        a_out.start()
        l_in = pltpu.make_async_copy(pin_hbm.at[:, c], mine_in, lsem.at[0])
        l_out = pltpu.make_async_copy(pout_hbm.at[:, c], mine_out, lsem.at[1])
        l_in.start()
        l_out.start()
        l_in.wait()
        l_out.wait()

        def stage_b(a_cp, mine, recv, sbuf, rbuf, base):
            a_cp.wait_recv()
            sends = []
            for k in (1, 2, 3):
                bk = b ^ k
                sbuf[k - 1] = (mine[bk].astype(F32) + recv[bk].astype(F32)).astype(BF16)
                cp = rcopy(sbuf.at[k - 1], rbuf.at[k - 1], bsend.at[base + k - 1], brecv.at[base + k - 1],
                           (x ^ (k >> 1), y ^ (k & 1), c))
                cp.start()
                sends.append(cp)
            return sends

        b_in = stage_b(a_in, mine_in, recv_in, sbuf_in, rbuf_in, 0)
        b_out = stage_b(a_out, mine_out, recv_out, sbuf_out, rbuf_out, 3)

        def stage_c(b_sends, mine, recv, rbuf, g_o, half, idx):
            acc = mine[b].astype(F32) + recv[b].astype(F32)
            for k in (1, 2, 3):
                b_sends[k - 1].wait_recv()
                acc = acc + rbuf[k - 1].astype(F32)
            rows = g_o.at[pl.ds(pl.multiple_of(c * half, half), half), :]
            g_o[pl.ds(pl.multiple_of(c * half, half), half), :] = acc
            cp = rcopy(rows, rows, csend.at[idx], crecv.at[idx], sib)
            cp.start()
            return cp

        c_in = stage_c(b_in, mine_in, recv_in, rbuf_in, gin_o, hin, 0)
        c_out = stage_c(b_out, mine_out, recv_out, rbuf_out, gout_o, hout, 1)

        for cp in s_sends:
            cp.wait_recv()
        tot = gather[0]
        for d in range(1, 8):
            tot = tot + gather[d]
        small_o[...] = tot

        for g_o, half, idx in ((gin_o, hin, 0), (gout_o, hout, 1)):
            other = g_o.at[pl.ds(pl.multiple_of((1 - c) * half, half), half), :]
            rcopy(other, other, csend.at[idx], crecv.at[idx], sib).wait_recv()
        for cp in s_sends + [a_in, a_out] + b_in + b_out + [c_in, c_out]:
            cp.wait_send()

    vm = pl.BlockSpec(memory_space=pltpu.VMEM)
    hbm = pl.BlockSpec(memory_space=pl.ANY)
    return pl.pallas_call(
        body, name="gsync",
        out_shape=(jax.ShapeDtypeStruct((2 * hin, 1024), F32), jax.ShapeDtypeStruct((2 * hout, 1024), F32),
                   jax.ShapeDtypeStruct((nsmall, 128), F32)),
        in_specs=[hbm, hbm, vm], out_specs=(vm, vm, vm),
        scratch_shapes=[pltpu.VMEM((4, hin, 1024), BF16), pltpu.VMEM((4, hin, 1024), BF16),
                        pltpu.VMEM((3, hin, 1024), BF16), pltpu.VMEM((3, hin, 1024), BF16),
                        pltpu.VMEM((4, hout, 1024), BF16), pltpu.VMEM((4, hout, 1024), BF16),
                        pltpu.VMEM((3, hout, 1024), BF16), pltpu.VMEM((3, hout, 1024), BF16),
                        pltpu.VMEM((8, nsmall, 128), F32),
                        pltpu.SemaphoreType.DMA((2,)),
                        pltpu.SemaphoreType.DMA((2,)), pltpu.SemaphoreType.DMA((2,)),
                        pltpu.SemaphoreType.DMA((6,)), pltpu.SemaphoreType.DMA((6,)),
                        pltpu.SemaphoreType.DMA((2,)), pltpu.SemaphoreType.DMA((2,)),
                        pltpu.SemaphoreType.DMA((7,)), pltpu.SemaphoreType.DMA((7,))],
        compiler_params=_params(),
    )(pw_in, pw_out, small)


def _adamw_math(w, g, m, v):
    m = ADAM_B1 * m + (1.0 - ADAM_B1) * g
    v = ADAM_B2 * v + (1.0 - ADAM_B2) * (g * g)
    m_hat = m / (1.0 - ADAM_B1 ** ADAM_STEP)
    v_hat = v / (1.0 - ADAM_B2 ** ADAM_STEP)
    delta = -ADAM_LR * (m_hat / (jnp.sqrt(v_hat) + ADAM_EPS) + ADAM_WD * w)
    return delta, m, v


def _adamw(w, g, m, v, name):
    rows, cols = w.shape
    tr = 256 if rows % 256 == 0 else rows

    def body(w_ref, g_ref, m_ref, v_ref, d_o, m_o, v_o):
        d, m2, v2 = _adamw_math(w_ref[...], g_ref[...], m_ref[...], v_ref[...])
        d_o[...] = d
        m_o[...] = m2
        v_o[...] = v2

    spec = pl.BlockSpec((tr, cols), lambda i: (i, 0))
    shp = jax.ShapeDtypeStruct((rows, cols), F32)
    return pl.pallas_call(
        body, name=name, grid=(rows // tr,), out_shape=(shp, shp, shp),
        in_specs=[spec] * 4, out_specs=(spec, spec, spec),
        compiler_params=_params(("parallel",)),
    )(w, g, m, v)


def _fold_heads(dqkw):
    def body(x_ref, o_ref):
        xs = x_ref[...]
        sq = xs[0:1] + xs[1:2] + xs[2:3] + xs[3:4]
        sk = xs[4:5] + xs[5:6] + xs[6:7] + xs[7:8]
        both = jnp.concatenate([sq, sk], axis=0)
        o_ref[...] = both + pltpu.roll(both, HEAD_DIM, 1)

    vm = pl.BlockSpec(memory_space=pltpu.VMEM)
    return pl.pallas_call(body, name="fold_heads", out_shape=jax.ShapeDtypeStruct((2, 128), F32),
                          in_specs=[vm], out_specs=vm, compiler_params=_params())(dqkw)


def kernel(x, norm_w, w_in, conv_w, conv_b, q_norm_w, k_norm_w, rel_bias, w_out, loss_target, m_norm_w, m_w_in, m_conv_w, m_conv_b, m_q_norm_w, m_k_norm_w, m_rel_bias, m_w_out, v_norm_w, v_w_in, v_conv_w, v_conv_b, v_q_norm_w, v_k_norm_w, v_rel_bias, v_w_out):
    x2 = x[0]
    tgt = loss_target[0]
    blk = 2 * lax.axis_index("x") + lax.axis_index("y")

    conv_w8 = jnp.pad(conv_w, ((0, 5), (0, 0)))
    wblk, woutblk, cwblk = _wgather(w_in, w_out, conv_w8)
    wout_full = woutblk.reshape(1024, 1024)
    cw_full = cwblk.transpose(1, 0, 2).reshape(8, 512)

    qkw = jnp.concatenate([jnp.tile(q_norm_w, 8) * 0.125, jnp.tile(k_norm_w, 8)])[None, :]
    qkw_raw = jnp.concatenate([jnp.tile(q_norm_w, 8), jnp.tile(k_norm_w, 8)])[None, :]
    gidx = jnp.arange(256) // HEAD_DIM
    b256 = (gidx[:, None] == gidx[None, :]).astype(BF16)

    h, cg, qkr, qkn, vz, qkn4, v4, qkn16, v16 = _proj(x2, norm_w[None, :], wblk, qkw, b256)

    biases = _bias_tables(rel_bias)
    qkn_l = [qkn[None], qkn4, qkn16]
    v_l = [vz[None], v4, v16]
    o_g, lse_g = [], []
    for gi, d in enumerate(DILATIONS):
        o_l, lse_l = _attn_fwd(qkn_l[gi], v_l[gi], biases[gi], f"attn_fwd_d{d}")
        o_g.append(o_l)
        lse_g.append(lse_l)

    (y, dout, ld1, do1, dza, dgbz, dzc, loss_p, dcb, dcw, do4, ld4, do16, ld16) = _combine(
        o_g, lse_g, cg, vz, x2, tgt, wout_full, cw_full, conv_b[None, :], b256)

    dq_g, dkv_g, dsums = [], [], []
    for gi, (d, do_l, ld_l) in enumerate(zip(DILATIONS, (do1, do4, do16), (ld1, ld4, ld16))):
        dq_l, dkv_l, dsum = _attn_bwd(qkn_l[gi], v_l[gi], do_l, ld_l, biases[gi], f"attn_bwd_d{d}")
        dq_g.append(dq_l)
        dkv_g.append(dkv_l)
        dsums.append(dsum)

    grad_x, dproj, dnw, dqkw = _bwd_tail(dq_g, dkv_g, qkr, qkw_raw, dza, dgbz, dzc, cg, cw_full, wblk,
                                         x2, norm_w[None, :], dout, b256)

    pw_in = _wgrad(h, dproj, False, "wgrad_in")
    pw_out = _wgrad(y, dout, True, "wgrad_out")
    dbias8 = _dbias(jnp.stack(dsums, axis=0), jnp.stack([_diag_bucket_onehot(d) for d in DILATIONS], axis=0))

    small = jnp.concatenate([dnw.reshape(8, 128), dcb.reshape(4, 128), dqkw.reshape(8, 128),
                             dcw[0:3].reshape(12, 128), dbias8, jnp.pad(loss_p, ((0, 7), (0, 0)))], axis=0)
    g_win, g_wout, gsmall = _gsync(pw_in, pw_out, small)

    g_nw = gsmall[0:8].reshape(1024)
    g_cb = gsmall[8:12].reshape(512)
    folded = _fold_heads(gsmall[12:20])
    g_qw, g_kw = folded[0, 0:64], folded[1, 0:64]
    g_cw = lax.dynamic_slice(gsmall[20:32].reshape(3, 512), (0, blk * 128), (3, 128))
    g_rb = gsmall[32:40][:, 0:32].T
    loss = gsmall[40, 0]

    d_win, nm_win, nv_win = _adamw(w_in, g_win, m_w_in, v_w_in, "adamw_w_in")
    d_wout, nm_wout, nv_wout = _adamw(w_out, g_wout, m_w_out, v_w_out, "adamw_w_out")

    def pack(parts):
        rows = [parts[0].reshape(8, 128), parts[1].reshape(4, 128),
                jnp.pad(parts[2], (0, 64))[None, :], jnp.pad(parts[3], (0, 64))[None, :],
                parts[4], jnp.pad(parts[5].T, ((0, 0), (0, 96)))]
        return jnp.concatenate(rows, axis=0)

    ws = pack([norm_w, conv_b, q_norm_w, k_norm_w, conv_w, rel_bias])
    gs = pack([g_nw, g_cb, g_qw, g_kw, g_cw, g_rb])
    ms = pack([m_norm_w, m_conv_b, m_q_norm_w, m_k_norm_w, m_conv_w, m_rel_bias])
    vs = pack([v_norm_w, v_conv_b, v_q_norm_w, v_k_norm_w, v_conv_w, v_rel_bias])
    rpad = lambda a: jnp.pad(a, ((0, 7), (0, 0)))
    d_s, nm_s, nv_s = _adamw(rpad(ws), rpad(gs), rpad(ms), rpad(vs), "adamw_small")

    def unpack(a):
        return (a[0:8].reshape(1024), a[12:13, 0:64].reshape(64), a[13:14, 0:64].reshape(64),
                a[14:17], a[8:12].reshape(512), a[17:25, 0:32].T)

    def ordered(nw, win, cw, cb, qw, kw, rb, wout):
        return (nw, win, cw, cb, qw, kw, rb, wout)

    g_un = (g_nw, g_qw, g_kw, g_cw, g_cb, g_rb)
    outs = [loss, grad_x[None]]
    for un, win_v, wout_v in ((g_un, g_win, g_wout), (unpack(d_s), d_win, d_wout),
                              (unpack(nm_s), nm_win, nm_wout), (unpack(nv_s), nv_win, nv_wout)):
        nw, qw, kw, cw, cb, rb = un
        outs.extend(ordered(nw, win_v, cw, cb, qw, kw, rb, wout_v))
    return tuple(outs)
```

```python
import math

import jax
import jax.numpy as jnp
from jax import lax
from jax.experimental import pallas as pl
from jax.experimental.pallas import tpu as pltpu

F32 = jnp.float32
BF16 = jnp.bfloat16
MESH = pl.DeviceIdType.MESH

D_MODEL = 1024
CONV_W = 512
ATTN_W = 512
HEAD_DIM = 64
N_PAIR = 4
DILATIONS = (1, 4, 16)
HALF = 64
QB = 128
KB = QB + 2 * HALF
NUM_BUCKETS = 32
MAX_DISTANCE = 1024
EPS = 1e-6
NEG = -1e30
ADAM_LR, ADAM_B1, ADAM_B2, ADAM_EPS, ADAM_WD, ADAM_STEP = 0.001, 0.9, 0.999, 1e-08, 0.01, 10
VMEM_LIMIT = 48 << 20


def _params(sem=None, vmem=VMEM_LIMIT, **kw):
    if sem is not None:
        kw["dimension_semantics"] = sem
    return pltpu.CompilerParams(vmem_limit_bytes=vmem, **kw)


def _sigmoid(z):
    return 1.0 / (1.0 + jnp.exp(-z))


def _group_sum(val, b_ref, split=True):
    hi = val.astype(BF16)
    lo = (val - hi.astype(F32)).astype(BF16) if split else None
    outs = []
    for j in range(val.shape[1] // 256):
        sl = slice(256 * j, 256 * j + 256)
        part = jnp.dot(hi[:, sl], b_ref[...], preferred_element_type=F32)
        if split:
            part = part + jnp.dot(lo[:, sl], b_ref[...], preferred_element_type=F32)
        outs.append(part)
    return outs[0] if len(outs) == 1 else jnp.concatenate(outs, axis=1)


def _t5_bucket(rel):
    half_b = NUM_BUCKETS // 2
    max_exact = half_b // 2
    ret = jnp.where(rel > 0, half_b, 0)
    n = jnp.abs(rel)
    nf = jnp.maximum(n, 1).astype(F32)
    large = max_exact + (jnp.log(nf / max_exact) / math.log(MAX_DISTANCE / max_exact)
                         * (half_b - max_exact)).astype(jnp.int32)
    large = jnp.minimum(large, half_b - 1)
    return ret + jnp.where(n < max_exact, n, large)


def _window_rel(variant):
    off = (0, HALF, 2 * HALF)[variant]
    return jnp.arange(KB)[None, :] - off - jnp.arange(QB)[:, None]


def _bias_tables(rel_bias):
    bkts = []
    for dilation in DILATIONS:
        for variant in range(3):
            rel = _window_rel(variant)
            bkt = _t5_bucket(jnp.clip(rel, -HALF, HALF) * dilation)
            bkts.append(jnp.where(jnp.abs(rel) <= HALF, bkt, -1))
    bkt_all = jnp.stack(bkts, axis=0).astype(jnp.int32)

    def body(rb_ref, bkt_ref, o_ref):
        bkt = bkt_ref[...]
        for h in range(8):
            acc = jnp.full((QB, KB), NEG, F32)
            for b in range(NUM_BUCKETS):
                acc = jnp.where(bkt == b, rb_ref[b, h], acc)
            o_ref[h] = acc

    out = pl.pallas_call(
        body, name="bias_tables", grid=(9,),
        out_shape=jax.ShapeDtypeStruct((9, 8, QB, KB), F32),
        in_specs=[pl.BlockSpec(memory_space=pltpu.SMEM), pl.BlockSpec((None, QB, KB), lambda i: (i, 0, 0))],
        out_specs=pl.BlockSpec((None, 8, QB, KB), lambda i: (i, 0, 0, 0)),
        compiler_params=_params(("parallel",)),
    )(rel_bias, bkt_all)
    return out.reshape(3, 3, N_PAIR, 2 * QB, KB)


def _diag_bucket_onehot(dilation):
    out = []
    c = jnp.arange(KB)
    for variant in range(3):
        off = (0, HALF, 2 * HALF)[variant]
        rel = ((c - off + 128) % 256) - 128
        band = jnp.abs(rel) <= HALF
        bkt = _t5_bucket(jnp.clip(rel, -HALF, HALF) * dilation)
        oh = (bkt[:, None] == jnp.arange(128)[None, :]) & band[:, None]
        out.append(oh.astype(F32))
    return jnp.stack(out, axis=0)


def _wgather(w_in, w_out, conv_w):
    rin, rout = w_in.shape[0] // 2, w_out.shape[0] // 2

    def body(win_ref, wout_ref, cw_ref, win_o, wout_o, cw_o, send_sems, recv_sems):
        x, y, c = lax.axis_index("x"), lax.axis_index("y"), lax.axis_index("c")
        b = 2 * x + y
        win_o[b] = win_ref[...].astype(BF16)
        wout_o[b] = wout_ref[...].astype(BF16)
        cw_o[b] = cw_ref[...]

        def peer(k):
            return (x ^ (k >> 1), y ^ (k & 1))

        def piece(ref, blk, half_rows, core):
            return ref.at[blk, pl.ds(core * half_rows, half_rows), :]

        def copy(sem, src, dst, to):
            return pltpu.make_async_remote_copy(src_ref=src, dst_ref=dst, send_sem=send_sems.at[sem],
                                                recv_sem=recv_sems.at[sem], device_id=to, device_id_type=MESH)

        sends = []
        for k in (1, 2, 3):
            px, py = peer(k)
            sends.append(copy(k - 1, piece(win_o, b, rin, c), piece(win_o, b, rin, c), (px, py, c)))
            sends.append(copy(3 + k - 1, piece(wout_o, b, rout, c), piece(wout_o, b, rout, c), (px, py, c)))
            sends.append(copy(6 + k - 1, cw_o.at[b], cw_o.at[b], (px, py, c)))
        for cp in sends:
            cp.start()
        fwd = []
        for k in (1, 2, 3):
            px, py = peer(k)
            bk = 2 * px + py
            copy(k - 1, piece(win_o, bk, rin, c), piece(win_o, bk, rin, c), (px, py, c)).wait_recv()
            f = copy(9 + k - 1, piece(win_o, bk, rin, c), piece(win_o, bk, rin, c), (x, y, 1 - c))
            f.start()
            fwd.append(f)
            copy(3 + k - 1, piece(wout_o, bk, rout, c), piece(wout_o, bk, rout, c), (px, py, c)).wait_recv()
            f = copy(12 + k - 1, piece(wout_o, bk, rout, c), piece(wout_o, bk, rout, c), (x, y, 1 - c))
            f.start()
            fwd.append(f)
            copy(6 + k - 1, cw_o.at[bk], cw_o.at[bk], (px, py, c)).wait_recv()
        for k in (1, 2, 3):
            px, py = peer(k)
            bk = 2 * px + py
            copy(9 + k - 1, piece(win_o, bk, rin, 1 - c), piece(win_o, bk, rin, 1 - c), (x, y, 1 - c)).wait_recv()
            copy(12 + k - 1, piece(wout_o, bk, rout, 1 - c), piece(wout_o, bk, rout, 1 - c), (x, y, 1 - c)).wait_recv()
        for cp in sends + fwd:
            cp.wait_send()

    vm = pl.BlockSpec(memory_space=pltpu.VMEM)
    return pl.pallas_call(
        body, name="wgather",
        out_shape=(jax.ShapeDtypeStruct((4,) + w_in.shape, BF16),
                   jax.ShapeDtypeStruct((4,) + w_out.shape, BF16),
                   jax.ShapeDtypeStruct((4,) + conv_w.shape, F32)),
        in_specs=[vm, vm, vm], out_specs=(vm, vm, vm),
        scratch_shapes=[pltpu.SemaphoreType.DMA((15,)), pltpu.SemaphoreType.DMA((15,))],
        compiler_params=_params(),
    )(w_in, w_out, conv_w)


TM_MATMUL = 512
TM_COMBINE = 256


def _resident(shape):
    return pl.BlockSpec(shape, lambda i: (0,) * len(shape), pipeline_mode=pl.Buffered(1))


def _to_slabs(slab, val, j0=0):
    for j in range(val.shape[1] // 128):
        slab[j0 + j] = val[:, 128 * j:128 * (j + 1)]


def _scatter_classes(slab, j0, nj, out_ref, d, part=0, mid=None):
    tm = slab.shape[1]
    n = tm // d
    if d == 4:
        for r in range(d):
            for j in range(nj):
                out_ref[r, part * n:(part + 1) * n, 128 * j:128 * (j + 1)] = (
                    slab[j0 + j, pl.ds(r, n, stride=d), :].astype(out_ref.dtype))
        return
    q = tm // 4
    for lo in range(4):
        for j in range(nj):
            mid[j0 + j, lo * q:(lo + 1) * q, :] = slab[j0 + j, pl.ds(lo, q, stride=4), :]
    for hi in range(4):
        for lo in range(4):
            for j in range(nj):
                out_ref[4 * hi + lo, part * n:(part + 1) * n, 128 * j:128 * (j + 1)] = (
                    mid[j0 + j, pl.ds(lo * q + hi, n, stride=4), :].astype(out_ref.dtype))


def _gather_classes(slab, piece, nj, d, mid=None):
    tm = slab.shape[1]
    n = tm // d
    if d == 4:
        for r in range(d):
            for j in range(nj):
                slab[j, pl.ds(r, n, stride=d), :] = piece(r, j).astype(F32)
    else:
        q = tm // 4
        for hi in range(4):
            for lo in range(4):
                for j in range(nj):
                    mid[j, pl.ds(lo * q + hi, n, stride=4), :] = piece(4 * hi + lo, j).astype(F32)
        for lo in range(4):
            for j in range(nj):
                slab[j, pl.ds(lo, q, stride=4), :] = mid[j, lo * q:(lo + 1) * q, :]
    return jnp.concatenate([slab[j] for j in range(nj)], axis=1)


def _class_spec(d, width, tm):
    return pl.BlockSpec((d, tm // d, width), lambda i: (0, i, 0))


def _proj(x, norm_w, wblk, qkw, b256):
    s = x.shape[0]
    tm = TM_MATMUL
    nparts = 2
    tp = tm // nparts

    def body(x_ref, nw_ref, w_ref, qkw_ref, b_ref, h_o, cg_o, qkr_o, qkn_o, vz_o, qkn4_o, v4_o, qkn16_o, v16_o,
             slabs, mids):
        for part in range(nparts):
            rows = slice(part * tp, (part + 1) * tp)
            slab = slabs.at[part]
            xf = x_ref[rows, :]
            r = lax.rsqrt(jnp.mean(xf * xf, axis=-1, keepdims=True) + EPS)
            h = (xf * r * nw_ref[...]).astype(BF16)
            h_o[rows, :] = h
            p2 = jnp.dot(h, w_ref[2], preferred_element_type=F32)
            qkr_o[rows, :] = p2.astype(BF16)
            ss = _group_sum(p2 * p2, b_ref, split=False)
            rr = lax.rsqrt(ss * (1.0 / HEAD_DIM) + EPS)
            qkn = p2 * rr * qkw_ref[...]
            qkn_o[rows, :] = qkn.astype(BF16)
            _to_slabs(slab, qkn)
            p3 = jnp.dot(h, w_ref[3], preferred_element_type=F32)
            vz_o[rows, :] = p3.astype(BF16)
            _to_slabs(slab, p3[:, 0:512], 8)
            cg_o[rows, 0:1024] = jnp.dot(h, w_ref[0], preferred_element_type=F32).astype(BF16)
            cg_o[rows, 1024:2048] = jnp.dot(h, w_ref[1], preferred_element_type=F32).astype(BF16)
            for d, q_o, v_o in ((4, qkn4_o, v4_o), (16, qkn16_o, v16_o)):
                _scatter_classes(slab, 0, 8, q_o, d, part, mids.at[part])
                _scatter_classes(slab, 8, 4, v_o, d, part, mids.at[part])

    row = lambda w: pl.BlockSpec((tm, w), lambda i: (i, 0))
    full = lambda shp: pl.BlockSpec(shp, lambda i: (0,) * len(shp))
    nat = lambda w: jax.ShapeDtypeStruct((s, w), BF16)
    cls = lambda d, w: jax.ShapeDtypeStruct((d, s // d, w), BF16)
    return pl.pallas_call(
        body, name="proj", grid=(s // tm,),
        out_shape=(nat(1024), nat(2048), nat(1024), nat(1024), nat(1024),
                   cls(4, 1024), cls(4, 512), cls(16, 1024), cls(16, 512)),
        in_specs=[row(1024), full((1, 1024)), _resident((4, 1024, 1024)), full((1, 1024)), full((256, 256))],
        out_specs=(row(1024), row(2048), row(1024), row(1024), row(1024),
                   _class_spec(4, 1024, tm), _class_spec(4, 512, tm),
                   _class_spec(16, 1024, tm), _class_spec(16, 512, tm)),
        scratch_shapes=[pltpu.VMEM((nparts, 12, tp, 128), F32), pltpu.VMEM((nparts, 12, tp, 128), F32)],
        compiler_params=_params(("parallel",)),
    )(x, norm_w, wblk, qkw, b256)


def _block_coords(t, i, nsub, nb, length):
    n = t * nsub + i
    q0 = i * QB
    start = pl.multiple_of(jnp.clip(n * QB - HALF, 0, length - KB), HALF)
    variant = jnp.where(n == 0, 0, jnp.where(n == nb - 1, 2, 1))
    return q0, start, variant


def _split_heads(a, lo):
    zero = jnp.zeros_like(a)
    return jnp.concatenate([jnp.where(lo, a, zero), jnp.where(lo, zero, a)], axis=0)


def _col_pair(ref, q0, lane):
    return jnp.concatenate([ref[pl.ds(q0, QB), lane:lane + 1],
                            ref[pl.ds(q0, QB), HEAD_DIM + lane:HEAD_DIM + lane + 1]], axis=0)


def _attn_fwd(qkn_l, v_l, bias, name):
    r_cls, length, _ = qkn_l.shape
    qt = min(length, 1024)
    nb, nsub = length // QB, qt // QB

    def body(q_ref, k_ref, v_ref, b_ref, o_ref, lse_ref):
        t = pl.program_id(2)
        lo = lax.broadcasted_iota(jnp.int32, (QB, 128), 1) < HEAD_DIM

        starts, logits = [], []
        for i in range(nsub):
            _, start, variant = _block_coords(t, i, nsub, nb, length)
            qq = _split_heads(q_ref[i * QB:(i + 1) * QB, :], lo)
            k = k_ref[pl.ds(start, KB), :]
            logits.append(lax.dot_general(qq, k, (((1,), (1,)), ((), ())), preferred_element_type=F32)
                          + b_ref[variant])
            starts.append(start)
        lg = jnp.concatenate(logits, axis=0)
        m = jnp.max(lg, axis=-1, keepdims=True)
        p = jnp.exp(lg - m)
        pb = p.astype(BF16)
        l = jnp.sum(p, axis=-1, keepdims=True)
        lse = jnp.broadcast_to(m + jnp.log(l), (nsub * 2 * QB, 128))
        inv = 1.0 / l
        for i in range(nsub):
            rows = slice(2 * QB * i, 2 * QB * (i + 1))
            v = v_ref[pl.ds(starts[i], KB), :]
            pv = jnp.dot(pb[rows], v, preferred_element_type=F32) * inv[rows]
            o_ref[i * QB:(i + 1) * QB, :] = jnp.where(lo, pv[0:QB], pv[QB:2 * QB]).astype(BF16)
            ls = lse[rows]
            lse_ref[i * QB:(i + 1) * QB, :] = jnp.where(lo, ls[0:QB], ls[QB:2 * QB])

    return pl.pallas_call(
        body, name=name, grid=(r_cls, N_PAIR, length // qt),
        out_shape=(jax.ShapeDtypeStruct((r_cls, length, 512), BF16),
                   jax.ShapeDtypeStruct((r_cls, length, 512), F32)),
        in_specs=[pl.BlockSpec((None, qt, 128), lambda r, p, t: (r, t, p)),
                  pl.BlockSpec((None, length, 128), lambda r, p, t: (r, 0, 4 + p)),
                  pl.BlockSpec((None, length, 128), lambda r, p, t: (r, 0, p)),
                  pl.BlockSpec((3, None, 2 * QB, KB), lambda r, p, t: (0, p, 0, 0))],
        out_specs=(pl.BlockSpec((None, qt, 128), lambda r, p, t: (r, t, p)),
                   pl.BlockSpec((None, qt, 128), lambda r, p, t: (r, t, p))),
        compiler_params=_params(("parallel", "parallel", "arbitrary")),
    )(qkn_l, qkn_l, v_l, bias)


def _combine(o_g, lse_g, cg, vz, x, tgt, wout, cw, cb, b256):
    s = x.shape[0]
    tm = TM_COMBINE
    hb = 16
    nt = s // tm

    def body(o1, o4, o16, l1, l4, l16, cg_ref, cgp_ref, cgn_ref, za_ref, x_ref, t_ref, w_ref, cw_ref, cb_ref,
             b_ref, y_o, dout_o, ld1_o, do1_o, dza_o, dgbz_o, dzc_o, loss_o, dcb_o, dcw_o,
             do4_o, ld4_o, do16_o, ld16_o, slab, mid):
        i = pl.program_id(0)

        @pl.when(i == 0)
        def _():
            loss_o[...] = jnp.zeros_like(loss_o)
            dcb_o[...] = jnp.zeros_like(dcb_o)
            dcw_o[...] = jnp.zeros_like(dcw_o)

        u = cg_ref[:, 0:512].astype(F32)
        gb = cg_ref[:, 512:1024].astype(F32)
        gc = cg_ref[:, 1024:1536].astype(F32)
        zc = cg_ref[:, 1536:2048].astype(F32)
        tt = gc * u
        t_prev = cgp_ref[hb - 1:hb, 0:512].astype(F32) * cgp_ref[hb - 1:hb, 1024:1536].astype(F32)
        t_next = cgn_ref[0:1, 0:512].astype(F32) * cgn_ref[0:1, 1024:1536].astype(F32)
        t_prev = jnp.where(i == 0, 0.0, t_prev)
        t_next = jnp.where(i == nt - 1, 0.0, t_next)
        rows = lax.broadcasted_iota(jnp.int32, (tm, 512), 0)
        t_up = jnp.where(rows == 0, t_prev, pltpu.roll(tt, 1, 0))
        t_dn = jnp.where(rows == tm - 1, t_next, pltpu.roll(tt, tm - 1, 0))
        w0, w1, w2 = cw_ref[0:1, :], cw_ref[1:2, :], cw_ref[2:3, :]
        zb = w0 * t_up + w1 * tt + w2 * t_dn + cb_ref[...]
        sg = _sigmoid(zc)
        sz = zc * sg
        y_conv = gb * zb * sz

        a1, p1 = l1[0], o1[0].astype(F32)
        a4 = _gather_classes(slab, lambda r, j: l4[r, :, 128 * j:128 * (j + 1)], 4, 4)
        p4 = _gather_classes(slab, lambda r, j: o4[r, :, 128 * j:128 * (j + 1)], 4, 4)
        a16 = _gather_classes(slab, lambda r, j: l16[r, :, 128 * j:128 * (j + 1)], 4, 16, mid)
        p16 = _gather_classes(slab, lambda r, j: o16[r, :, 128 * j:128 * (j + 1)], 4, 16, mid)
        m = jnp.maximum(jnp.maximum(a1, a4), a16)
        e1, e4, e16 = jnp.exp(a1 - m), jnp.exp(a4 - m), jnp.exp(a16 - m)
        den = e1 + e4 + e16
        lse = m + jnp.log(den)
        o = (e1 * p1 + e4 * p4 + e16 * p16) / den
        za = za_ref[...].astype(F32)
        sga = _sigmoid(za)
        sa = za * sga
        y = jnp.concatenate([y_conv, o * sa], axis=1).astype(BF16)
        y_o[...] = y

        out = x_ref[...] + jnp.dot(y, w_ref[...], preferred_element_type=F32)
        diff = out - t_ref[...]
        loss_o[...] += (0.5 / D_MODEL) * jnp.sum(diff * diff)
        dout = diff * (1.0 / D_MODEL)
        dout_o[...] = dout
        dy = lax.dot_general(dout.astype(BF16), w_ref[...], (((1,), (1,)), ((), ())), preferred_element_type=F32)
        dyc, dya = dy[:, 0:512], dy[:, 512:1024]

        do = dya * sa
        dza_o[...] = (dya * o * (sga * (1.0 + za * (1.0 - sga)))).astype(BF16)
        lane = lax.broadcasted_iota(jnp.int32, (tm, 512), 1)
        ld = jnp.where((lane & (HEAD_DIM - 1)) < HEAD_DIM // 2, lse, _group_sum(do * o, b_ref))
        do1_o[0] = do.astype(BF16)
        ld1_o[0] = ld
        _to_slabs(slab, do)
        _scatter_classes(slab, 0, 4, do4_o, 4)
        _scatter_classes(slab, 0, 4, do16_o, 16, 0, mid)
        _to_slabs(slab, ld)
        _scatter_classes(slab, 0, 4, ld4_o, 4)
        _scatter_classes(slab, 0, 4, ld16_o, 16, 0, mid)

        dzc = dyc * sz * gb
        dzc_o[...] = dzc.astype(BF16)
        dgbz_o[:, 0:512] = (dyc * sz * zb).astype(BF16)
        dgbz_o[:, 512:1024] = (dyc * gb * zb * (sg * (1.0 + zc * (1.0 - sg)))).astype(BF16)
        dcb_o[...] += jnp.sum(dzc, axis=0, keepdims=True)
        dcw_o[0:1, :] += jnp.sum(dzc * t_up, axis=0, keepdims=True)
        dcw_o[1:2, :] += jnp.sum(dzc * tt, axis=0, keepdims=True)
        dcw_o[2:3, :] += jnp.sum(dzc * t_dn, axis=0, keepdims=True)

    row = lambda w, j=0: pl.BlockSpec((tm, w), lambda i: (i, j))
    full = lambda shp: pl.BlockSpec(shp, lambda i: (0,) * len(shp))
    prev = pl.BlockSpec((hb, 2048), lambda i: (jnp.maximum(i * (tm // hb) - 1, 0), 0))
    nxt = pl.BlockSpec((hb, 2048), lambda i: (jnp.minimum((i + 1) * (tm // hb), s // hb - 1), 0))
    cls = lambda d, dt: jax.ShapeDtypeStruct((d, s // d, 512), dt)
    cspecs = [_class_spec(d, 512, tm) for d in DILATIONS]
    return pl.pallas_call(
        body, name="combine", grid=(nt,),
        out_shape=(jax.ShapeDtypeStruct((s, 1024), BF16), jax.ShapeDtypeStruct((s, 1024), F32),
                   cls(1, F32), cls(1, BF16), jax.ShapeDtypeStruct((s, 512), BF16),
                   jax.ShapeDtypeStruct((s, 1024), BF16), jax.ShapeDtypeStruct((s, 512), BF16),
                   jax.ShapeDtypeStruct((1, 128), F32), jax.ShapeDtypeStruct((1, 512), F32),
                   jax.ShapeDtypeStruct((8, 512), F32),
                   cls(4, BF16), cls(4, F32), cls(16, BF16), cls(16, F32)),
        in_specs=cspecs + cspecs + [row(2048), prev, nxt, row(512, 1), row(1024), row(1024),
                                    _resident((1024, 1024)), full((8, 512)), full((1, 512)), full((256, 256))],
        out_specs=(row(1024), row(1024), cspecs[0], cspecs[0], row(512), row(1024), row(512),
                   full((1, 128)), full((1, 512)), full((8, 512)),
                   cspecs[1], cspecs[1], cspecs[2], cspecs[2]),
        scratch_shapes=[pltpu.VMEM((4, tm, 128), F32), pltpu.VMEM((4, tm, 128), F32)],
        compiler_params=_params(("arbitrary",)),
    )(*o_g, *lse_g, cg, cg, cg, vz, x, tgt, wout, cw, cb, b256)


def _attn_bwd(qkn_l, v_l, do_l, ld_l, bias, name):
    r_cls, length, _ = qkn_l.shape
    qt = min(length, 1024)
    nb, nsub, nt = length // QB, qt // QB, length // qt
    nstage = 2 if length <= 4096 else 1

    def body(q_ref, k_ref, v_ref, do_ref, ld_ref, b_ref, dq_ref, dkv_hbm, dsum_ref, dk_acc, dv_acc, stage, sems):
        p_id, r, t = pl.program_id(0), pl.program_id(1), pl.program_id(2)
        lo = lax.broadcasted_iota(jnp.int32, (QB, 128), 1) < HEAD_DIM

        @pl.when(t == 0)
        def _():
            dk_acc[...] = jnp.zeros_like(dk_acc)
            dv_acc[...] = jnp.zeros_like(dv_acc)

        @pl.when((t == 0) & (r == 0))
        def _():
            dsum_ref[...] = jnp.zeros_like(dsum_ref)

        nt_dims = (((1,), (1,)), ((), ()))
        tn_dims = (((0,), (0,)), ((), ()))
        coords, qqs, dds, logits, dps, lcols, dcols = [], [], [], [], [], [], []
        for i in range(nsub):
            q0, start, variant = _block_coords(t, i, nsub, nb, length)
            qq = _split_heads(q_ref[q0:q0 + QB, :], lo)
            dd = _split_heads(do_ref[q0:q0 + QB, :], lo)
            k = k_ref[pl.ds(start, KB), :]
            v = v_ref[pl.ds(start, KB), :]
            logits.append(lax.dot_general(qq, k, nt_dims, preferred_element_type=F32) + b_ref[variant])
            dps.append(lax.dot_general(dd, v, nt_dims, preferred_element_type=F32))
            lcols.append(_col_pair(ld_ref, q0, 0))
            dcols.append(_col_pair(ld_ref, q0, HEAD_DIM // 2))
            coords.append((q0, start, variant))
            qqs.append(qq)
            dds.append(dd)
        p = jnp.exp(jnp.concatenate(logits, axis=0) - jnp.concatenate(lcols, axis=0))
        ds = p * (jnp.concatenate(dps, axis=0) - jnp.concatenate(dcols, axis=0))
        pb = p.astype(BF16)
        dsb = ds.astype(BF16)
        middle = None
        for i in range(nsub):
            q0, start, variant = coords[i]
            rows = slice(2 * QB * i, 2 * QB * (i + 1))
            if 0 < i < nsub - 1:
                middle = ds[rows] if middle is None else middle + ds[rows]
            else:
                dsum_ref[variant] += ds[rows]
            dqq = jnp.dot(dsb[rows], k_ref[pl.ds(start, KB), :], preferred_element_type=F32)
            dq_ref[q0:q0 + QB, :] = jnp.where(lo, dqq[0:QB], dqq[QB:2 * QB]).astype(BF16)
            dk_acc[pl.ds(start, KB), :] += lax.dot_general(dsb[rows], qqs[i], tn_dims, preferred_element_type=F32)
            dv_acc[pl.ds(start, KB), :] += lax.dot_general(pb[rows], dds[i], tn_dims, preferred_element_type=F32)
        if middle is not None:
            dsum_ref[1] += middle

        @pl.when(t == nt - 1)
        def _():
            copies = []
            for which, acc in enumerate((dk_acc, dv_acc)):
                buf = stage.at[which % nstage]
                buf[...] = acc[...].astype(BF16)
                cp = pltpu.make_async_copy(buf, dkv_hbm.at[r, p_id, which], sems.at[which])
                cp.start()
                if nstage == 1:
                    cp.wait()
                else:
                    copies.append(cp)
            for cp in copies:
                cp.wait()

    qspec = pl.BlockSpec((None, qt, 128), lambda p, r, t: (r, t, p))
    return pl.pallas_call(
        body, name=name, grid=(N_PAIR, r_cls, nt),
        out_shape=(jax.ShapeDtypeStruct((r_cls, length, 512), BF16),
                   jax.ShapeDtypeStruct((r_cls, N_PAIR, 2, length, 128), BF16),
                   jax.ShapeDtypeStruct((N_PAIR, 3, 2 * QB, KB), F32)),
        in_specs=[qspec,
                  pl.BlockSpec((None, length, 128), lambda p, r, t: (r, 0, 4 + p)),
                  pl.BlockSpec((None, length, 128), lambda p, r, t: (r, 0, p)),
                  qspec, qspec,
                  pl.BlockSpec((3, None, 2 * QB, KB), lambda p, r, t: (0, p, 0, 0))],
        out_specs=(qspec, pl.BlockSpec(memory_space=pl.ANY),
                   pl.BlockSpec((None, 3, 2 * QB, KB), lambda p, r, t: (p, 0, 0, 0))),
        scratch_shapes=[pltpu.VMEM((length, 128), F32), pltpu.VMEM((length, 128), F32),
                        pltpu.VMEM((nstage, length, 128), BF16), pltpu.SemaphoreType.DMA((2,))],
        compiler_params=_params(("arbitrary", "arbitrary", "arbitrary")),
    )(qkn_l, qkn_l, v_l, do_l, ld_l, bias)


def _bwd_tail(dq_g, dkv_g, qkr, qkw, dza, dgbz, dzc, cg, cw, wblk, x, norm_w, dout, b256):
    s = x.shape[0]
    tm = TM_COMBINE
    hb = 16
    nt = s // tm

    def body(dq1, dq4, dq16, dkv1, dkv4, dkv16, qkr_ref, qkw_ref, dza_ref, dgbz_ref, dzc_ref,
             dzp_ref, dzn_ref, u_ref, gc_ref, cw_ref, w_ref, x_ref, nw_ref, dout_ref, b_ref,
             gx_o, dproj_o, dnw_o, dqkw_o, slab, mid):
        i = pl.program_id(0)

        def nat_q(ref, d):
            return _gather_classes(slab, lambda r, j: ref[r, :, 128 * j:128 * (j + 1)], 4, d, mid)

        def nat_kv(ref, d, which):
            return _gather_classes(slab, lambda r, j: ref[r, j, which], 4, d, mid)

        @pl.when(i == 0)
        def _():
            dnw_o[...] = jnp.zeros_like(dnw_o)
            dqkw_o[...] = jnp.zeros_like(dqkw_o)

        dzc = dzc_ref[...].astype(F32)
        d_prev = jnp.where(i == 0, 0.0, dzp_ref[hb - 1:hb, :].astype(F32))
        d_next = jnp.where(i == nt - 1, 0.0, dzn_ref[0:1, :].astype(F32))
        rows = lax.broadcasted_iota(jnp.int32, (tm, 512), 0)
        d_up = jnp.where(rows == 0, d_prev, pltpu.roll(dzc, 1, 0))
        d_dn = jnp.where(rows == tm - 1, d_next, pltpu.roll(dzc, tm - 1, 0))
        dt = cw_ref[0:1, :] * d_dn + cw_ref[1:2, :] * dzc + cw_ref[2:3, :] * d_up
        u = u_ref[...].astype(F32)
        gc = gc_ref[...].astype(F32)
        dproj_o[:, 0:512] = (dt * gc).astype(BF16)
        dproj_o[:, 512:1024] = dgbz_ref[:, 0:512]
        dproj_o[:, 1024:1536] = (dt * u).astype(BF16)
        dproj_o[:, 1536:2048] = dgbz_ref[:, 512:1024]

        dqn = (dq1[0].astype(F32) + nat_q(dq4, 4) + nat_q(dq16, 16)) * (1.0 / 8.0)
        dk1 = jnp.concatenate([dkv1[0, j, 0] for j in range(N_PAIR)], axis=1)
        dv1 = jnp.concatenate([dkv1[0, j, 1] for j in range(N_PAIR)], axis=1)
        dkn = dk1 + nat_kv(dkv4, 4, 0) + nat_kv(dkv16, 16, 0)
        dvn = dv1 + nat_kv(dkv4, 4, 1) + nat_kv(dkv16, 16, 1)
        g = jnp.concatenate([dqn, dkn], axis=1) * qkw_ref[...]
        raw = qkr_ref[...].astype(F32)
        rr = lax.rsqrt(_group_sum(raw * raw, b_ref, split=False) * (1.0 / HEAD_DIM) + EPS)
        proj_gq = _group_sum(g * raw, b_ref) * (1.0 / HEAD_DIM)
        draw = rr * g - raw * (rr * rr * rr) * proj_gq
        dqkw_o[...] += jnp.sum(jnp.concatenate([dqn, dkn], axis=1) * raw * rr, axis=0, keepdims=True)
        dproj_o[:, 2048:3072] = draw.astype(BF16)
        dproj_o[:, 3072:3584] = dvn.astype(BF16)
        dproj_o[:, 3584:4096] = dza_ref[...]

        nt_dims = (((1,), (1,)), ((), ()))
        dh = lax.dot_general(dproj_o[:, 0:1024], w_ref[0], nt_dims, preferred_element_type=F32)
        for b in range(1, 4):
            dh += lax.dot_general(dproj_o[:, 1024 * b:1024 * b + 1024], w_ref[b], nt_dims,
                                  preferred_element_type=F32)

        xf = x_ref[...]
        r = lax.rsqrt(jnp.mean(xf * xf, axis=-1, keepdims=True) + EPS)
        gh = dh * nw_ref[...]
        dnw_o[...] += jnp.sum(dh * xf * r, axis=0, keepdims=True)
        mean_gx = jnp.mean(gh * xf, axis=-1, keepdims=True)
        gx_o[...] = dout_ref[...] + r * gh - xf * (r * r * r) * mean_gx

    row = lambda w, j=0: pl.BlockSpec((tm, w), lambda i: (i, j))
    full = lambda shp: pl.BlockSpec(shp, lambda i: (0,) * len(shp))
    prev = pl.BlockSpec((hb, 512), lambda i: (jnp.maximum(i * (tm // hb) - 1, 0), 0))
    nxt = pl.BlockSpec((hb, 512), lambda i: (jnp.minimum((i + 1) * (tm // hb), s // hb - 1), 0))
    return pl.pallas_call(
        body, name="bwd_tail", grid=(nt,),
        out_shape=(jax.ShapeDtypeStruct((s, 1024), F32), jax.ShapeDtypeStruct((s, 4096), BF16),
                   jax.ShapeDtypeStruct((1, 1024), F32), jax.ShapeDtypeStruct((1, 1024), F32)),
        in_specs=[_class_spec(d, 512, tm) for d in DILATIONS]
        + [pl.BlockSpec((d, N_PAIR, 2, tm // d, 128), lambda i: (0, 0, 0, i, 0)) for d in DILATIONS]
        + [row(1024), full((1, 1024)), row(512), row(1024), row(512), prev, nxt,
           row(512, 0), row(512, 2), full((8, 512)), _resident((4, 1024, 1024)), row(1024),
           full((1, 1024)), row(1024), full((256, 256))],
        out_specs=(row(1024), row(4096), full((1, 1024)), full((1, 1024))),
        scratch_shapes=[pltpu.VMEM((4, tm, 128), F32), pltpu.VMEM((4, tm, 128), F32)],
        compiler_params=_params(("arbitrary",)),
    )(*dq_g, *dkv_g, qkr, qkw, dza, dgbz, dzc, dzc, dzc, cg, cg, cw, wblk, x, norm_w, dout, b256)


def _wgrad(a, b, row_blocked, name):
    s, m = a.shape
    n = b.shape[1]
    tk = 1024
    ncol = min(n, 2048)
    nj, nk = n // ncol, s // tk

    def body(a_ref, b_ref, o_ref, acc):
        kk = pl.program_id(1)

        @pl.when(kk == 0)
        def _():
            acc[...] = jnp.zeros_like(acc)

        acc[...] += lax.dot_general(a_ref[...], b_ref[...].astype(BF16), (((0,), (0,)), ((), ())),
                                    preferred_element_type=F32)

        @pl.when(kk == nk - 1)
        def _():
            blocks, _, rows, _ = o_ref.shape
            for blk in range(blocks):
                for half in range(2):
                    if row_blocked:
                        r0 = (2 * blk + half) * rows
                        o_ref[blk, half] = acc[r0:r0 + rows, :].astype(BF16)
                    else:
                        o_ref[blk, half] = acc[half * rows:(half + 1) * rows,
                                               1024 * blk:1024 * (blk + 1)].astype(BF16)

    if row_blocked:
        out_shape = jax.ShapeDtypeStruct((4, 2, m // 8, 1024), BF16)
        out_spec = pl.BlockSpec((4, 2, m // 8, 1024), lambda j, k: (0, 0, 0, 0))
    else:
        out_shape = jax.ShapeDtypeStruct((n // 1024, 2, m // 2, 1024), BF16)
        out_spec = pl.BlockSpec((ncol // 1024, 2, m // 2, 1024), lambda j, k: (j, 0, 0, 0))
    return pl.pallas_call(
        body, name=name, grid=(nj, nk),
        out_shape=out_shape,
        in_specs=[pl.BlockSpec((tk, m), lambda j, k: (k, 0)), pl.BlockSpec((tk, ncol), lambda j, k: (k, j))],
        out_specs=out_spec,
        scratch_shapes=[pltpu.VMEM((m, ncol), F32)],
        compiler_params=_params(("parallel", "arbitrary")),
    )(a, b)


def _dbias(dsum_all, onehot_all):
    def body(ds_ref, oh_ref, o_ref):
        step = pl.program_id(0) * 3 + pl.program_id(1)

        @pl.when(step == 0)
        def _():
            o_ref[...] = jnp.zeros_like(o_ref)

        rowq = lax.broadcasted_iota(jnp.int32, (2 * QB, KB), 0) & (QB - 1)
        hrow = lax.broadcasted_iota(jnp.int32, (8, KB), 0)
        diag = jnp.zeros((8, KB), F32)
        for p in range(N_PAIR):
            y = ds_ref[p]
            for bit in range(7):
                sh = 1 << bit
                y = jnp.where((rowq & sh) != 0, pltpu.roll(y, KB - sh, 1), y)
            da = jnp.sum(y[0:QB], axis=0, keepdims=True)
            db = jnp.sum(y[QB:2 * QB], axis=0, keepdims=True)
            diag = jnp.where(hrow == 2 * p, da, diag)
            diag = jnp.where(hrow == 2 * p + 1, db, diag)
        o_ref[...] += jnp.dot(diag, oh_ref[...], preferred_element_type=F32, precision=lax.Precision.HIGHEST)

    return pl.pallas_call(
        body, name="dbias", grid=(3, 3),
        out_shape=jax.ShapeDtypeStruct((8, 128), F32),
        in_specs=[pl.BlockSpec((None, N_PAIR, None, 2 * QB, KB), lambda g, v: (g, 0, v, 0, 0)),
                  pl.BlockSpec((None, None, KB, 128), lambda g, v: (g, v, 0, 0))],
        out_specs=pl.BlockSpec((8, 128), lambda g, v: (0, 0)),
        compiler_params=_params(("arbitrary", "arbitrary")),
    )(dsum_all, onehot_all)


def _gsync(pw_in, pw_out, small):
    hin, hout = pw_in.shape[2], pw_out.shape[2]
    nsmall = small.shape[0]

    def body(pin_hbm, pout_hbm, small_ref, gin_o, gout_o, small_o,
             mine_in, recv_in, sbuf_in, rbuf_in, mine_out, recv_out, sbuf_out, rbuf_out, gather,
             lsem, asend, arecv, bsend, brecv, csend, crecv, ssend, srecv):
        x, y, c = lax.axis_index("x"), lax.axis_index("y"), lax.axis_index("c")
        b = 2 * x + y
        dev = 4 * x + 2 * y + c
        sib = (x, y, 1 - c)

        def rcopy(src, dst, ssem, rsem, to):
            return pltpu.make_async_remote_copy(src_ref=src, dst_ref=dst, send_sem=ssem, recv_sem=rsem,
                                                device_id=to, device_id_type=MESH)

        gather[dev] = small_ref[...]
        s_sends = []
        for k in range(1, 8):
            to = (x ^ (k >> 2), y ^ ((k >> 1) & 1), c ^ (k & 1))
            cp = rcopy(gather.at[dev], gather.at[dev], ssend.at[k - 1], srecv.at[k - 1], to)
            cp.start()
            s_sends.append(cp)

        a_in = rcopy(pin_hbm.at[:, 1 - c], recv_in, asend.at[0], arecv.at[0], sib)
        a_out = rcopy(pout_hbm.at[:, 1 - c], recv_out, asend.at[1], arecv.at[1], sib)
        a_in.start()
        a_out.start()
        l_in = pltpu.make_async_copy(pin_hbm.at[:, c], mine_in, lsem.at[0])
        l_out = pltpu.make_async_copy(pout_hbm.at[:, c], mine_out, lsem.at[1])
        l_in.start()
        l_out.start()
        l_in.wait()
        l_out.wait()

        def stage_b(a_cp, mine, recv, sbuf, rbuf, base):
            a_cp.wait_recv()
            sends = []
            for k in (1, 2, 3):
                bk = b ^ k
                sbuf[k - 1] = (mine[bk].astype(F32) + recv[bk].astype(F32)).astype(BF16)
                cp = rcopy(sbuf.at[k - 1], rbuf.at[k - 1], bsend.at[base + k - 1], brecv.at[base + k - 1],
                           (x ^ (k >> 1), y ^ (k & 1), c))
                cp.start()
                sends.append(cp)
            return sends

        b_in = stage_b(a_in, mine_in, recv_in, sbuf_in, rbuf_in, 0)
        b_out = stage_b(a_out, mine_out, recv_out, sbuf_out, rbuf_out, 3)

        def stage_c(b_sends, mine, recv, rbuf, g_o, half, idx):
            acc = mine[b].astype(F32) + recv[b].astype(F32)
            for k in (1, 2, 3):
                b_sends[k - 1].wait_recv()
                acc = acc + rbuf[k - 1].astype(F32)
            rows = g_o.at[pl.ds(pl.multiple_of(c * half, half), half), :]
            g_o[pl.ds(pl.multiple_of(c * half, half), half), :] = acc
            cp = rcopy(rows, rows, csend.at[idx], crecv.at[idx], sib)
            cp.start()
            return cp

        c_in = stage_c(b_in, mine_in, recv_in, rbuf_in, gin_o, hin, 0)
        c_out = stage_c(b_out, mine_out, recv_out, rbuf_out, gout_o, hout, 1)

        for cp in s_sends:
            cp.wait_recv()
        tot = gather[0]
        for d in range(1, 8):
            tot = tot + gather[d]
        small_o[...] = tot

        for g_o, half, idx in ((gin_o, hin, 0), (gout_o, hout, 1)):
            other = g_o.at[pl.ds(pl.multiple_of((1 - c) * half, half), half), :]
            rcopy(other, other, csend.at[idx], crecv.at[idx], sib).wait_recv()
        for cp in s_sends + [a_in, a_out] + b_in + b_out + [c_in, c_out]:
            cp.wait_send()

    vm = pl.BlockSpec(memory_space=pltpu.VMEM)
    hbm = pl.BlockSpec(memory_space=pl.ANY)
    return pl.pallas_call(
        body, name="gsync",
        out_shape=(jax.ShapeDtypeStruct((2 * hin, 1024), F32), jax.ShapeDtypeStruct((2 * hout, 1024), F32),
                   jax.ShapeDtypeStruct((nsmall, 128), F32)),
        in_specs=[hbm, hbm, vm], out_specs=(vm, vm, vm),
        scratch_shapes=[pltpu.VMEM((4, hin, 1024), BF16), pltpu.VMEM((4, hin, 1024), BF16),
                        pltpu.VMEM((3, hin, 1024), BF16), pltpu.VMEM((3, hin, 1024), BF16),
                        pltpu.VMEM((4, hout, 1024), BF16), pltpu.VMEM((4, hout, 1024), BF16),
                        pltpu.VMEM((3, hout, 1024), BF16), pltpu.VMEM((3, hout, 1024), BF16),
                        pltpu.VMEM((8, nsmall, 128), F32),
                        pltpu.SemaphoreType.DMA((2,)),
                        pltpu.SemaphoreType.DMA((2,)), pltpu.SemaphoreType.DMA((2,)),
                        pltpu.SemaphoreType.DMA((6,)), pltpu.SemaphoreType.DMA((6,)),
                        pltpu.SemaphoreType.DMA((2,)), pltpu.SemaphoreType.DMA((2,)),
                        pltpu.SemaphoreType.DMA((7,)), pltpu.SemaphoreType.DMA((7,))],
        compiler_params=_params(),
    )(pw_in, pw_out, small)


def _adamw_math(w, g, m, v):
    m = ADAM_B1 * m + (1.0 - ADAM_B1) * g
    v = ADAM_B2 * v + (1.0 - ADAM_B2) * (g * g)
    m_hat = m / (1.0 - ADAM_B1 ** ADAM_STEP)
    v_hat = v / (1.0 - ADAM_B2 ** ADAM_STEP)
    delta = -ADAM_LR * (m_hat / (jnp.sqrt(v_hat) + ADAM_EPS) + ADAM_WD * w)
    return delta, m, v


def _adamw(w, g, m, v, name):
    rows, cols = w.shape
    tr = 256 if rows % 256 == 0 else rows

    def body(w_ref, g_ref, m_ref, v_ref, d_o, m_o, v_o):
        d, m2, v2 = _adamw_math(w_ref[...], g_ref[...], m_ref[...], v_ref[...])
        d_o[...] = d
        m_o[...] = m2
        v_o[...] = v2

    spec = pl.BlockSpec((tr, cols), lambda i: (i, 0))
    shp = jax.ShapeDtypeStruct((rows, cols), F32)
    return pl.pallas_call(
        body, name=name, grid=(rows // tr,), out_shape=(shp, shp, shp),
        in_specs=[spec] * 4, out_specs=(spec, spec, spec),
        compiler_params=_params(("parallel",)),
    )(w, g, m, v)


def _fold_heads(dqkw):
    def body(x_ref, o_ref):
        xs = x_ref[...]
        sq = xs[0:1] + xs[1:2] + xs[2:3] + xs[3:4]
        sk = xs[4:5] + xs[5:6] + xs[6:7] + xs[7:8]
        both = jnp.concatenate([sq, sk], axis=0)
        o_ref[...] = both + pltpu.roll(both, HEAD_DIM, 1)

    vm = pl.BlockSpec(memory_space=pltpu.VMEM)
    return pl.pallas_call(body, name="fold_heads", out_shape=jax.ShapeDtypeStruct((2, 128), F32),
                          in_specs=[vm], out_specs=vm, compiler_params=_params())(dqkw)


def kernel(x, norm_w, w_in, conv_w, conv_b, q_norm_w, k_norm_w, rel_bias, w_out, loss_target, m_norm_w, m_w_in, m_conv_w, m_conv_b, m_q_norm_w, m_k_norm_w, m_rel_bias, m_w_out, v_norm_w, v_w_in, v_conv_w, v_conv_b, v_q_norm_w, v_k_norm_w, v_rel_bias, v_w_out):
    x2 = x[0]
    tgt = loss_target[0]
    blk = 2 * lax.axis_index("x") + lax.axis_index("y")

    conv_w8 = jnp.pad(conv_w, ((0, 5), (0, 0)))
    wblk, woutblk, cwblk = _wgather(w_in, w_out, conv_w8)
    wout_full = woutblk.reshape(1024, 1024)
    cw_full = cwblk.transpose(1, 0, 2).reshape(8, 512)

    qkw = jnp.concatenate([jnp.tile(q_norm_w, 8) * 0.125, jnp.tile(k_norm_w, 8)])[None, :]
    qkw_raw = jnp.concatenate([jnp.tile(q_norm_w, 8), jnp.tile(k_norm_w, 8)])[None, :]
    gidx = jnp.arange(256) // HEAD_DIM
    b256 = (gidx[:, None] == gidx[None, :]).astype(BF16)

    h, cg, qkr, qkn, vz, qkn4, v4, qkn16, v16 = _proj(x2, norm_w[None, :], wblk, qkw, b256)

    biases = _bias_tables(rel_bias)
    qkn_l = [qkn[None], qkn4, qkn16]
    v_l = [vz[None], v4, v16]
    o_g, lse_g = [], []
    for gi, d in enumerate(DILATIONS):
        o_l, lse_l = _attn_fwd(qkn_l[gi], v_l[gi], biases[gi], f"attn_fwd_d{d}")
        o_g.append(o_l)
        lse_g.append(lse_l)

    (y, dout, ld1, do1, dza, dgbz, dzc, loss_p, dcb, dcw, do4, ld4, do16, ld16) = _combine(
        o_g, lse_g, cg, vz, x2, tgt, wout_full, cw_full, conv_b[None, :], b256)

    dq_g, dkv_g, dsums = [], [], []
    for gi, (d, do_l, ld_l) in enumerate(zip(DILATIONS, (do1, do4, do16), (ld1, ld4, ld16))):
        dq_l, dkv_l, dsum = _attn_bwd(qkn_l[gi], v_l[gi], do_l, ld_l, biases[gi], f"attn_bwd_d{d}")
        dq_g.append(dq_l)
        dkv_g.append(dkv_l)
        dsums.append(dsum)

    grad_x, dproj, dnw, dqkw = _bwd_tail(dq_g, dkv_g, qkr, qkw_raw, dza, dgbz, dzc, cg, cw_full, wblk,
                                         x2, norm_w[None, :], dout, b256)

    pw_in = _wgrad(h, dproj, False, "wgrad_in")
    pw_out = _wgrad(y, dout, True, "wgrad_out")
    dbias8 = _dbias(jnp.stack(dsums, axis=0), jnp.stack([_diag_bucket_onehot(d) for d in DILATIONS], axis=0))

    small = jnp.concatenate([dnw.reshape(8, 128), dcb.reshape(4, 128), dqkw.reshape(8, 128),
                             dcw[0:3].reshape(12, 128), dbias8, jnp.pad(loss_p, ((0, 7), (0, 0)))], axis=0)
    g_win, g_wout, gsmall = _gsync(pw_in, pw_out, small)

    g_nw = gsmall[0:8].reshape(1024)
    g_cb = gsmall[8:12].reshape(512)
    folded = _fold_heads(gsmall[12:20])
    g_qw, g_kw = folded[0, 0:64], folded[1, 0:64]
    g_cw = lax.dynamic_slice(gsmall[20:32].reshape(3, 512), (0, blk * 128), (3, 128))
    g_rb = gsmall[32:40][:, 0:32].T
    loss = gsmall[40, 0]

    d_win, nm_win, nv_win = _adamw(w_in, g_win, m_w_in, v_w_in, "adamw_w_in")
    d_wout, nm_wout, nv_wout = _adamw(w_out, g_wout, m_w_out, v_w_out, "adamw_w_out")

    def pack(parts):
        rows = [parts[0].reshape(8, 128), parts[1].reshape(4, 128),
                jnp.pad(parts[2], (0, 64))[None, :], jnp.pad(parts[3], (0, 64))[None, :],
                parts[4], jnp.pad(parts[5].T, ((0, 0), (0, 96)))]
        return jnp.concatenate(rows, axis=0)

    ws = pack([norm_w, conv_b, q_norm_w, k_norm_w, conv_w, rel_bias])
    gs = pack([g_nw, g_cb, g_qw, g_kw, g_cw, g_rb])
    ms = pack([m_norm_w, m_conv_b, m_q_norm_w, m_k_norm_w, m_conv_w, m_rel_bias])
    vs = pack([v_norm_w, v_conv_b, v_q_norm_w, v_k_norm_w, v_conv_w, v_rel_bias])
    rpad = lambda a: jnp.pad(a, ((0, 7), (0, 0)))
    d_s, nm_s, nv_s = _adamw(rpad(ws), rpad(gs), rpad(ms), rpad(vs), "adamw_small")

    def unpack(a):
        return (a[0:8].reshape(1024), a[12:13, 0:64].reshape(64), a[13:14, 0:64].reshape(64),
                a[14:17], a[8:12].reshape(512), a[17:25, 0:32].T)

    def ordered(nw, win, cw, cb, qw, kw, rb, wout):
        return (nw, win, cw, cb, qw, kw, rb, wout)

    g_un = (g_nw, g_qw, g_kw, g_cw, g_cb, g_rb)
    outs = [loss, grad_x[None]]
    for un, win_v, wout_v in ((g_un, g_win, g_wout), (unpack(d_s), d_win, d_wout),
                              (unpack(nm_s), nm_win, nm_wout), (unpack(nv_s), nv_win, nv_wout)):
        nw, qw, kw, cw, cb, rb = un
        outs.extend(ordered(nw, win_v, cw, cb, qw, kw, rb, wout_v))
    return tuple(outs)
```

```python
import math

import jax
import jax.numpy as jnp
from jax import lax
from jax.experimental import pallas as pl
from jax.experimental.pallas import tpu as pltpu

F32 = jnp.float32
BF16 = jnp.bfloat16
MESH = pl.DeviceIdType.MESH

D_MODEL = 1024
CONV_W = 512
ATTN_W = 512
HEAD_DIM = 64
N_PAIR = 4
DILATIONS = (1, 4, 16)
HALF = 64
QB = 128
KB = QB + 2 * HALF
NUM_BUCKETS = 32
MAX_DISTANCE = 1024
EPS = 1e-6
NEG = -1e30
ADAM_LR, ADAM_B1, ADAM_B2, ADAM_EPS, ADAM_WD, ADAM_STEP = 0.001, 0.9, 0.999, 1e-08, 0.01, 10
VMEM_LIMIT = 48 << 20


def _params(sem=None, vmem=VMEM_LIMIT, **kw):
    if sem is not None:
        kw["dimension_semantics"] = sem
    return pltpu.CompilerParams(vmem_limit_bytes=vmem, **kw)


def _sigmoid(z):
    return 1.0 / (1.0 + jnp.exp(-z))


def _group_sum(val, b_ref, split=True):
    hi = val.astype(BF16)
    lo = (val - hi.astype(F32)).astype(BF16) if split else None
    outs = []
    for j in range(val.shape[1] // 256):
        sl = slice(256 * j, 256 * j + 256)
        part = jnp.dot(hi[:, sl], b_ref[...], preferred_element_type=F32)
        if split:
            part = part + jnp.dot(lo[:, sl], b_ref[...], preferred_element_type=F32)
        outs.append(part)
    return outs[0] if len(outs) == 1 else jnp.concatenate(outs, axis=1)


def _t5_bucket(rel):
    half_b = NUM_BUCKETS // 2
    max_exact = half_b // 2
    ret = jnp.where(rel > 0, half_b, 0)
    n = jnp.abs(rel)
    nf = jnp.maximum(n, 1).astype(F32)
    large = max_exact + (jnp.log(nf / max_exact) / math.log(MAX_DISTANCE / max_exact)
                         * (half_b - max_exact)).astype(jnp.int32)
    large = jnp.minimum(large, half_b - 1)
    return ret + jnp.where(n < max_exact, n, large)


def _window_rel(variant):
    off = (0, HALF, 2 * HALF)[variant]
    return jnp.arange(KB)[None, :] - off - jnp.arange(QB)[:, None]


def _bias_tables(rel_bias):
    bkts = []
    for dilation in DILATIONS:
        for variant in range(3):
            rel = _window_rel(variant)
            bkt = _t5_bucket(jnp.clip(rel, -HALF, HALF) * dilation)
            bkts.append(jnp.where(jnp.abs(rel) <= HALF, bkt, -1))
    bkt_all = jnp.stack(bkts, axis=0).astype(jnp.int32)

    def body(rb_ref, bkt_ref, o_ref):
        bkt = bkt_ref[...]
        for h in range(8):
            acc = jnp.full((QB, KB), NEG, F32)
            for b in range(NUM_BUCKETS):
                acc = jnp.where(bkt == b, rb_ref[b, h], acc)
            o_ref[h] = acc

    out = pl.pallas_call(
        body, name="bias_tables", grid=(9,),
        out_shape=jax.ShapeDtypeStruct((9, 8, QB, KB), F32),
        in_specs=[pl.BlockSpec(memory_space=pltpu.SMEM), pl.BlockSpec((None, QB, KB), lambda i: (i, 0, 0))],
        out_specs=pl.BlockSpec((None, 8, QB, KB), lambda i: (i, 0, 0, 0)),
        compiler_params=_params(("parallel",)),
    )(rel_bias, bkt_all)
    return out.reshape(3, 3, N_PAIR, 2 * QB, KB)


def _diag_bucket_onehot(dilation):
    out = []
    c = jnp.arange(KB)
    for variant in range(3):
        off = (0, HALF, 2 * HALF)[variant]
        rel = ((c - off + 128) % 256) - 128
        band = jnp.abs(rel) <= HALF
        bkt = _t5_bucket(jnp.clip(rel, -HALF, HALF) * dilation)
        oh = (bkt[:, None] == jnp.arange(128)[None, :]) & band[:, None]
        out.append(oh.astype(F32))
    return jnp.stack(out, axis=0)


def _wgather(w_in, w_out, conv_w):
    rin, rout = w_in.shape[0] // 2, w_out.shape[0] // 2

    def body(win_ref, wout_ref, cw_ref, win_o, wout_o, cw_o, send_sems, recv_sems):
        x, y, c = lax.axis_index("x"), lax.axis_index("y"), lax.axis_index("c")
        b = 2 * x + y
        win_o[b] = win_ref[...].astype(BF16)
        wout_o[b] = wout_ref[...].astype(BF16)
        cw_o[b] = cw_ref[...]

        def peer(k):
            return (x ^ (k >> 1), y ^ (k & 1))

        def piece(ref, blk, half_rows, core):
            return ref.at[blk, pl.ds(core * half_rows, half_rows), :]

        def copy(sem, src, dst, to):
            return pltpu.make_async_remote_copy(src_ref=src, dst_ref=dst, send_sem=send_sems.at[sem],
                                                recv_sem=recv_sems.at[sem], device_id=to, device_id_type=MESH)

        sends = []
        for k in (1, 2, 3):
            px, py = peer(k)
            sends.append(copy(k - 1, piece(win_o, b, rin, c), piece(win_o, b, rin, c), (px, py, c)))
            sends.append(copy(3 + k - 1, piece(wout_o, b, rout, c), piece(wout_o, b, rout, c), (px, py, c)))
            sends.append(copy(6 + k - 1, cw_o.at[b], cw_o.at[b], (px, py, c)))
        for cp in sends:
            cp.start()
        fwd = []
        for k in (1, 2, 3):
            px, py = peer(k)
            bk = 2 * px + py
            copy(k - 1, piece(win_o, bk, rin, c), piece(win_o, bk, rin, c), (px, py, c)).wait_recv()
            f = copy(9 + k - 1, piece(win_o, bk, rin, c), piece(win_o, bk, rin, c), (x, y, 1 - c))
            f.start()
            fwd.append(f)
            copy(3 + k - 1, piece(wout_o, bk, rout, c), piece(wout_o, bk, rout, c), (px, py, c)).wait_recv()
            f = copy(12 + k - 1, piece(wout_o, bk, rout, c), piece(wout_o, bk, rout, c), (x, y, 1 - c))
            f.start()
            fwd.append(f)
            copy(6 + k - 1, cw_o.at[bk], cw_o.at[bk], (px, py, c)).wait_recv()
        for k in (1, 2, 3):
            px, py = peer(k)
            bk = 2 * px + py
            copy(9 + k - 1, piece(win_o, bk, rin, 1 - c), piece(win_o, bk, rin, 1 - c), (x, y, 1 - c)).wait_recv()
            copy(12 + k - 1, piece(wout_o, bk, rout, 1 - c), piece(wout_o, bk, rout, 1 - c), (x, y, 1 - c)).wait_recv()
        for cp in sends + fwd:
            cp.wait_send()

    vm = pl.BlockSpec(memory_space=pltpu.VMEM)
    return pl.pallas_call(
        body, name="wgather",
        out_shape=(jax.ShapeDtypeStruct((4,) + w_in.shape, BF16),
                   jax.ShapeDtypeStruct((4,) + w_out.shape, BF16),
                   jax.ShapeDtypeStruct((4,) + conv_w.shape, F32)),
        in_specs=[vm, vm, vm], out_specs=(vm, vm, vm),
        scratch_shapes=[pltpu.SemaphoreType.DMA((15,)), pltpu.SemaphoreType.DMA((15,))],
        compiler_params=_params(),
    )(w_in, w_out, conv_w)


TM_MATMUL = 512
TM_COMBINE = 256


def _resident(shape):
    return pl.BlockSpec(shape, lambda i: (0,) * len(shape), pipeline_mode=pl.Buffered(1))


def _to_slabs(slab, val, j0=0):
    for j in range(val.shape[1] // 128):
        slab[j0 + j] = val[:, 128 * j:128 * (j + 1)]


def _scatter_classes(slab, j0, nj, out_ref, d, part=0, mid=None):
    tm = slab.shape[1]
    n = tm // d
    if d == 4:
        for r in range(d):
            for j in range(nj):
                out_ref[r, part * n:(part + 1) * n, 128 * j:128 * (j + 1)] = (
                    slab[j0 + j, pl.ds(r, n, stride=d), :].astype(out_ref.dtype))
        return
    q = tm // 4
    for lo in range(4):
        for j in range(nj):
            mid[j0 + j, lo * q:(lo + 1) * q, :] = slab[j0 + j, pl.ds(lo, q, stride=4), :]
    for hi in range(4):
        for lo in range(4):
            for j in range(nj):
                out_ref[4 * hi + lo, part * n:(part + 1) * n, 128 * j:128 * (j + 1)] = (
                    mid[j0 + j, pl.ds(lo * q + hi, n, stride=4), :].astype(out_ref.dtype))


def _gather_classes(slab, piece, nj, d, mid=None):
    tm = slab.shape[1]
    n = tm // d
    if d == 4:
        for r in range(d):
            for j in range(nj):
                slab[j, pl.ds(r, n, stride=d), :] = piece(r, j).astype(F32)
    else:
        q = tm // 4
        for hi in range(4):
            for lo in range(4):
                for j in range(nj):
                    mid[j, pl.ds(lo * q + hi, n, stride=4), :] = piece(4 * hi + lo, j).astype(F32)
        for lo in range(4):
            for j in range(nj):
                slab[j, pl.ds(lo, q, stride=4), :] = mid[j, lo * q:(lo + 1) * q, :]
    return jnp.concatenate([slab[j] for j in range(nj)], axis=1)


def _class_spec(d, width, tm):
    return pl.BlockSpec((d, tm // d, width), lambda i: (0, i, 0))


def _proj(x, norm_w, wblk, qkw, b256):
    s = x.shape[0]
    tm = TM_MATMUL
    nparts = 2
    tp = tm // nparts

    def body(x_ref, nw_ref, w_ref, qkw_ref, b_ref, h_o, cg_o, qkr_o, qkn_o, vz_o, qkn4_o, v4_o, qkn16_o, v16_o,
             slabs, mids):
        for part in range(nparts):
            rows = slice(part * tp, (part + 1) * tp)
            slab = slabs.at[part]
            xf = x_ref[rows, :]
            r = lax.rsqrt(jnp.mean(xf * xf, axis=-1, keepdims=True) + EPS)
            h = (xf * r * nw_ref[...]).astype(BF16)
            h_o[rows, :] = h
            p2 = jnp.dot(h, w_ref[2], preferred_element_type=F32)
            qkr_o[rows, :] = p2.astype(BF16)
            ss = _group_sum(p2 * p2, b_ref, split=False)
            rr = lax.rsqrt(ss * (1.0 / HEAD_DIM) + EPS)
            qkn = p2 * rr * qkw_ref[...]
            qkn_o[rows, :] = qkn.astype(BF16)
            _to_slabs(slab, qkn)
            p3 = jnp.dot(h, w_ref[3], preferred_element_type=F32)
            vz_o[rows, :] = p3.astype(BF16)
            _to_slabs(slab, p3[:, 0:512], 8)
            cg_o[rows, 0:1024] = jnp.dot(h, w_ref[0], preferred_element_type=F32).astype(BF16)
            cg_o[rows, 1024:2048] = jnp.dot(h, w_ref[1], preferred_element_type=F32).astype(BF16)
            for d, q_o, v_o in ((4, qkn4_o, v4_o), (16, qkn16_o, v16_o)):
                _scatter_classes(slab, 0, 8, q_o, d, part, mids.at[part])
                _scatter_classes(slab, 8, 4, v_o, d, part, mids.at[part])

    row = lambda w: pl.BlockSpec((tm, w), lambda i: (i, 0))
    full = lambda shp: pl.BlockSpec(shp, lambda i: (0,) * len(shp))
    nat = lambda w: jax.ShapeDtypeStruct((s, w), BF16)
    cls = lambda d, w: jax.ShapeDtypeStruct((d, s // d, w), BF16)
    return pl.pallas_call(
        body, name="proj", grid=(s // tm,),
        out_shape=(nat(1024), nat(2048), nat(1024), nat(1024), nat(1024),
                   cls(4, 1024), cls(4, 512), cls(16, 1024), cls(16, 512)),
        in_specs=[row(1024), full((1, 1024)), _resident((4, 1024, 1024)), full((1, 1024)), full((256, 256))],
        out_specs=(row(1024), row(2048), row(1024), row(1024), row(1024),
                   _class_spec(4, 1024, tm), _class_spec(4, 512, tm),
                   _class_spec(16, 1024, tm), _class_spec(16, 512, tm)),
        scratch_shapes=[pltpu.VMEM((nparts, 12, tp, 128), F32), pltpu.VMEM((nparts, 12, tp, 128), F32)],
        compiler_params=_params(("parallel",)),
    )(x, norm_w, wblk, qkw, b256)


def _block_coords(t, i, nsub, nb, length):
    n = t * nsub + i
    q0 = i * QB
    start = pl.multiple_of(jnp.clip(n * QB - HALF, 0, length - KB), HALF)
    variant = jnp.where(n == 0, 0, jnp.where(n == nb - 1, 2, 1))
    return q0, start, variant


def _split_heads(a, lo):
    zero = jnp.zeros_like(a)
    return jnp.concatenate([jnp.where(lo, a, zero), jnp.where(lo, zero, a)], axis=0)


def _col_pair(ref, q0, lane):
    return jnp.concatenate([ref[pl.ds(q0, QB), lane:lane + 1],
                            ref[pl.ds(q0, QB), HEAD_DIM + lane:HEAD_DIM + lane + 1]], axis=0)


def _attn_fwd(qkn_l, v_l, bias, name):
    r_cls, length, _ = qkn_l.shape
    qt = min(length, 2048)
    nb, nsub = length // QB, qt // QB

    def body(q_ref, k_ref, v_ref, b_ref, o_ref, lse_ref):
        t = pl.program_id(2)
        lo = lax.broadcasted_iota(jnp.int32, (QB, 128), 1) < HEAD_DIM

        starts, logits = [], []
        for i in range(nsub):
            _, start, variant = _block_coords(t, i, nsub, nb, length)
            qq = _split_heads(q_ref[i * QB:(i + 1) * QB, :], lo)
            k = k_ref[pl.ds(start, KB), :]
            logits.append(lax.dot_general(qq, k, (((1,), (1,)), ((), ())), preferred_element_type=F32)
                          + b_ref[variant])
            starts.append(start)
        lg = jnp.concatenate(logits, axis=0)
        m = jnp.max(lg, axis=-1, keepdims=True)
        p = jnp.exp(lg - m)
        pb = p.astype(BF16)
        l = jnp.sum(p, axis=-1, keepdims=True)
        lse = jnp.broadcast_to(m + jnp.log(l), (nsub * 2 * QB, 128))
        inv = 1.0 / l
        for i in range(nsub):
            rows = slice(2 * QB * i, 2 * QB * (i + 1))
            v = v_ref[pl.ds(starts[i], KB), :]
            pv = jnp.dot(pb[rows], v, preferred_element_type=F32) * inv[rows]
            o_ref[i * QB:(i + 1) * QB, :] = jnp.where(lo, pv[0:QB], pv[QB:2 * QB]).astype(BF16)
            ls = lse[rows]
            lse_ref[i * QB:(i + 1) * QB, :] = jnp.where(lo, ls[0:QB], ls[QB:2 * QB])

    return pl.pallas_call(
        body, name=name, grid=(r_cls, N_PAIR, length // qt),
        out_shape=(jax.ShapeDtypeStruct((r_cls, length, 512), BF16),
                   jax.ShapeDtypeStruct((r_cls, length, 512), F32)),
        in_specs=[pl.BlockSpec((None, qt, 128), lambda r, p, t: (r, t, p)),
                  pl.BlockSpec((None, length, 128), lambda r, p, t: (r, 0, 4 + p)),
                  pl.BlockSpec((None, length, 128), lambda r, p, t: (r, 0, p)),
                  pl.BlockSpec((3, None, 2 * QB, KB), lambda r, p, t: (0, p, 0, 0))],
        out_specs=(pl.BlockSpec((None, qt, 128), lambda r, p, t: (r, t, p)),
                   pl.BlockSpec((None, qt, 128), lambda r, p, t: (r, t, p))),
        compiler_params=_params(("parallel", "parallel", "arbitrary")),
    )(qkn_l, qkn_l, v_l, bias)


def _combine(o_g, lse_g, cg, vz, x, tgt, wout, cw, cb, b256):
    s = x.shape[0]
    tm = TM_COMBINE
    hb = 16
    nt = s // tm

    def body(o1, o4, o16, l1, l4, l16, cg_ref, cgp_ref, cgn_ref, za_ref, x_ref, t_ref, w_ref, cw_ref, cb_ref,
             b_ref, y_o, dout_o, ld1_o, do1_o, dza_o, dgbz_o, dzc_o, loss_o, dcb_o, dcw_o,
             do4_o, ld4_o, do16_o, ld16_o, slab, mid):
        i = pl.program_id(0)

        @pl.when(i == 0)
        def _():
            loss_o[...] = jnp.zeros_like(loss_o)
            dcb_o[...] = jnp.zeros_like(dcb_o)
            dcw_o[...] = jnp.zeros_like(dcw_o)

        u = cg_ref[:, 0:512].astype(F32)
        gb = cg_ref[:, 512:1024].astype(F32)
        gc = cg_ref[:, 1024:1536].astype(F32)
        zc = cg_ref[:, 1536:2048].astype(F32)
        tt = gc * u
        t_prev = cgp_ref[hb - 1:hb, 0:512].astype(F32) * cgp_ref[hb - 1:hb, 1024:1536].astype(F32)
        t_next = cgn_ref[0:1, 0:512].astype(F32) * cgn_ref[0:1, 1024:1536].astype(F32)
        t_prev = jnp.where(i == 0, 0.0, t_prev)
        t_next = jnp.where(i == nt - 1, 0.0, t_next)
        rows = lax.broadcasted_iota(jnp.int32, (tm, 512), 0)
        t_up = jnp.where(rows == 0, t_prev, pltpu.roll(tt, 1, 0))
        t_dn = jnp.where(rows == tm - 1, t_next, pltpu.roll(tt, tm - 1, 0))
        w0, w1, w2 = cw_ref[0:1, :], cw_ref[1:2, :], cw_ref[2:3, :]
        zb = w0 * t_up + w1 * tt + w2 * t_dn + cb_ref[...]
        sg = _sigmoid(zc)
        sz = zc * sg
        y_conv = gb * zb * sz

        a1, p1 = l1[0], o1[0].astype(F32)
        a4 = _gather_classes(slab, lambda r, j: l4[r, :, 128 * j:128 * (j + 1)], 4, 4)
        p4 = _gather_classes(slab, lambda r, j: o4[r, :, 128 * j:128 * (j + 1)], 4, 4)
        a16 = _gather_classes(slab, lambda r, j: l16[r, :, 128 * j:128 * (j + 1)], 4, 16, mid)
        p16 = _gather_classes(slab, lambda r, j: o16[r, :, 128 * j:128 * (j + 1)], 4, 16, mid)
        m = jnp.maximum(jnp.maximum(a1, a4), a16)
        e1, e4, e16 = jnp.exp(a1 - m), jnp.exp(a4 - m), jnp.exp(a16 - m)
        den = e1 + e4 + e16
        lse = m + jnp.log(den)
        o = (e1 * p1 + e4 * p4 + e16 * p16) / den
        za = za_ref[...].astype(F32)
        sga = _sigmoid(za)
        sa = za * sga
        y = jnp.concatenate([y_conv, o * sa], axis=1).astype(BF16)
        y_o[...] = y

        out = x_ref[...] + jnp.dot(y, w_ref[...], preferred_element_type=F32)
        diff = out - t_ref[...]
        loss_o[...] += (0.5 / D_MODEL) * jnp.sum(diff * diff)
        dout = diff * (1.0 / D_MODEL)
        dout_o[...] = dout
        dy = lax.dot_general(dout.astype(BF16), w_ref[...], (((1,), (1,)), ((), ())), preferred_element_type=F32)
        dyc, dya = dy[:, 0:512], dy[:, 512:1024]

        do = dya * sa
        dza_o[...] = (dya * o * (sga * (1.0 + za * (1.0 - sga)))).astype(BF16)
        lane = lax.broadcasted_iota(jnp.int32, (tm, 512), 1)
        ld = jnp.where((lane & (HEAD_DIM - 1)) < HEAD_DIM // 2, lse, _group_sum(do * o, b_ref))
        do1_o[0] = do.astype(BF16)
        ld1_o[0] = ld
        _to_slabs(slab, do)
        _scatter_classes(slab, 0, 4, do4_o, 4)
        _scatter_classes(slab, 0, 4, do16_o, 16, 0, mid)
        _to_slabs(slab, ld)
        _scatter_classes(slab, 0, 4, ld4_o, 4)
        _scatter_classes(slab, 0, 4, ld16_o, 16, 0, mid)

        dzc = dyc * sz * gb
        dzc_o[...] = dzc.astype(BF16)
        dgbz_o[:, 0:512] = (dyc * sz * zb).astype(BF16)
        dgbz_o[:, 512:1024] = (dyc * gb * zb * (sg * (1.0 + zc * (1.0 - sg)))).astype(BF16)
        dcb_o[...] += jnp.sum(dzc, axis=0, keepdims=True)
        dcw_o[0:1, :] += jnp.sum(dzc * t_up, axis=0, keepdims=True)
        dcw_o[1:2, :] += jnp.sum(dzc * tt, axis=0, keepdims=True)
        dcw_o[2:3, :] += jnp.sum(dzc * t_dn, axis=0, keepdims=True)

    row = lambda w, j=0: pl.BlockSpec((tm, w), lambda i: (i, j))
    full = lambda shp: pl.BlockSpec(shp, lambda i: (0,) * len(shp))
    prev = pl.BlockSpec((hb, 2048), lambda i: (jnp.maximum(i * (tm // hb) - 1, 0), 0))
    nxt = pl.BlockSpec((hb, 2048), lambda i: (jnp.minimum((i + 1) * (tm // hb), s // hb - 1), 0))
    cls = lambda d, dt: jax.ShapeDtypeStruct((d, s // d, 512), dt)
    cspecs = [_class_spec(d, 512, tm) for d in DILATIONS]
    return pl.pallas_call(
        body, name="combine", grid=(nt,),
        out_shape=(jax.ShapeDtypeStruct((s, 1024), BF16), jax.ShapeDtypeStruct((s, 1024), F32),
                   cls(1, F32), cls(1, BF16), jax.ShapeDtypeStruct((s, 512), BF16),
                   jax.ShapeDtypeStruct((s, 1024), BF16), jax.ShapeDtypeStruct((s, 512), BF16),
                   jax.ShapeDtypeStruct((1, 128), F32), jax.ShapeDtypeStruct((1, 512), F32),
                   jax.ShapeDtypeStruct((8, 512), F32),
                   cls(4, BF16), cls(4, F32), cls(16, BF16), cls(16, F32)),
        in_specs=cspecs + cspecs + [row(2048), prev, nxt, row(512, 1), row(1024), row(1024),
                                    _resident((1024, 1024)), full((8, 512)), full((1, 512)), full((256, 256))],
        out_specs=(row(1024), row(1024), cspecs[0], cspecs[0], row(512), row(1024), row(512),
                   full((1, 128)), full((1, 512)), full((8, 512)),
                   cspecs[1], cspecs[1], cspecs[2], cspecs[2]),
        scratch_shapes=[pltpu.VMEM((4, tm, 128), F32), pltpu.VMEM((4, tm, 128), F32)],
        compiler_params=_params(("arbitrary",)),
    )(*o_g, *lse_g, cg, cg, cg, vz, x, tgt, wout, cw, cb, b256)


def _attn_bwd(qkn_l, v_l, do_l, ld_l, bias, name):
    r_cls, length, _ = qkn_l.shape
    qt = min(length, 1024)
    nb, nsub, nt = length // QB, qt // QB, length // qt
    nstage = 2 if length <= 4096 else 1

    def body(q_ref, k_ref, v_ref, do_ref, ld_ref, b_ref, dq_ref, dkv_hbm, dsum_ref, dk_acc, dv_acc, stage, sems):
        p_id, r, t = pl.program_id(0), pl.program_id(1), pl.program_id(2)
        lo = lax.broadcasted_iota(jnp.int32, (QB, 128), 1) < HEAD_DIM

        @pl.when(t == 0)
        def _():
            dk_acc[...] = jnp.zeros_like(dk_acc)
            dv_acc[...] = jnp.zeros_like(dv_acc)

        @pl.when((t == 0) & (r == 0))
        def _():
            dsum_ref[...] = jnp.zeros_like(dsum_ref)

        nt_dims = (((1,), (1,)), ((), ()))
        tn_dims = (((0,), (0,)), ((), ()))
        coords, qqs, dds, logits, dps, lcols, dcols = [], [], [], [], [], [], []
        for i in range(nsub):
            q0, start, variant = _block_coords(t, i, nsub, nb, length)
            qq = _split_heads(q_ref[q0:q0 + QB, :], lo)
            dd = _split_heads(do_ref[q0:q0 + QB, :], lo)
            k = k_ref[pl.ds(start, KB), :]
            v = v_ref[pl.ds(start, KB), :]
            logits.append(lax.dot_general(qq, k, nt_dims, preferred_element_type=F32) + b_ref[variant])
            dps.append(lax.dot_general(dd, v, nt_dims, preferred_element_type=F32))
            lcols.append(_col_pair(ld_ref, q0, 0))
            dcols.append(_col_pair(ld_ref, q0, HEAD_DIM // 2))
            coords.append((q0, start, variant))
            qqs.append(qq)
            dds.append(dd)
        p = jnp.exp(jnp.concatenate(logits, axis=0) - jnp.concatenate(lcols, axis=0))
        ds = p * (jnp.concatenate(dps, axis=0) - jnp.concatenate(dcols, axis=0))
        pb = p.astype(BF16)
        dsb = ds.astype(BF16)
        middle = None
        for i in range(nsub):
            q0, start, variant = coords[i]
            rows = slice(2 * QB * i, 2 * QB * (i + 1))
            if 0 < i < nsub - 1:
                middle = ds[rows] if middle is None else middle + ds[rows]
            else:
                dsum_ref[variant] += ds[rows]
            dqq = jnp.dot(dsb[rows], k_ref[pl.ds(start, KB), :], preferred_element_type=F32)
            dq_ref[q0:q0 + QB, :] = jnp.where(lo, dqq[0:QB], dqq[QB:2 * QB]).astype(BF16)
            dk_acc[pl.ds(start, KB), :] += lax.dot_general(dsb[rows], qqs[i], tn_dims, preferred_element_type=F32)
            dv_acc[pl.ds(start, KB), :] += lax.dot_general(pb[rows], dds[i], tn_dims, preferred_element_type=F32)
        if middle is not None:
            dsum_ref[1] += middle

        @pl.when(t == nt - 1)
        def _():
            def copy(which):
                return pltpu.make_async_copy(stage.at[which % nstage], dkv_hbm.at[r, p_id, which], sems.at[which])

            if nstage == 2:
                @pl.when((p_id > 0) | (r > 0))
                def _():
                    copy(0).wait()
                    copy(1).wait()

            for which, acc in enumerate((dk_acc, dv_acc)):
                stage[which % nstage] = acc[...].astype(BF16)
                copy(which).start()
                if nstage == 1:
                    copy(which).wait()

            if nstage == 2:
                @pl.when((p_id == N_PAIR - 1) & (r == r_cls - 1))
                def _():
                    copy(0).wait()
                    copy(1).wait()

    qspec = pl.BlockSpec((None, qt, 128), lambda p, r, t: (r, t, p))
    return pl.pallas_call(
        body, name=name, grid=(N_PAIR, r_cls, nt),
        out_shape=(jax.ShapeDtypeStruct((r_cls, length, 512), BF16),
                   jax.ShapeDtypeStruct((r_cls, N_PAIR, 2, length, 128), BF16),
                   jax.ShapeDtypeStruct((N_PAIR, 3, 2 * QB, KB), F32)),
        in_specs=[qspec,
                  pl.BlockSpec((None, length, 128), lambda p, r, t: (r, 0, 4 + p)),
                  pl.BlockSpec((None, length, 128), lambda p, r, t: (r, 0, p)),
                  qspec, qspec,
                  pl.BlockSpec((3, None, 2 * QB, KB), lambda p, r, t: (0, p, 0, 0))],
        out_specs=(qspec, pl.BlockSpec(memory_space=pl.ANY),
                   pl.BlockSpec((None, 3, 2 * QB, KB), lambda p, r, t: (p, 0, 0, 0))),
        scratch_shapes=[pltpu.VMEM((length, 128), F32), pltpu.VMEM((length, 128), F32),
                        pltpu.VMEM((nstage, length, 128), BF16), pltpu.SemaphoreType.DMA((2,))],
        compiler_params=_params(("arbitrary", "arbitrary", "arbitrary")),
    )(qkn_l, qkn_l, v_l, do_l, ld_l, bias)


def _bwd_tail(dq_g, dkv_g, qkr, qkw, dza, dgbz, dzc, cg, cw, wblk, x, norm_w, dout, b256):
    s = x.shape[0]
    tm = TM_COMBINE
    hb = 16
    nt = s // tm

    def body(dq1, dq4, dq16, dkv1, dkv4, dkv16, qkr_ref, qkw_ref, dza_ref, dgbz_ref, dzc_ref,
             dzp_ref, dzn_ref, u_ref, gc_ref, cw_ref, w_ref, x_ref, nw_ref, dout_ref, b_ref,
             gx_o, dproj_o, dnw_o, dqkw_o, slab, mid):
        i = pl.program_id(0)

        def nat_q(ref, d):
            return _gather_classes(slab, lambda r, j: ref[r, :, 128 * j:128 * (j + 1)], 4, d, mid)

        def nat_kv(ref, d, which):
            return _gather_classes(slab, lambda r, j: ref[r, j, which], 4, d, mid)

        @pl.when(i == 0)
        def _():
            dnw_o[...] = jnp.zeros_like(dnw_o)
            dqkw_o[...] = jnp.zeros_like(dqkw_o)

        dzc = dzc_ref[...].astype(F32)
        d_prev = jnp.where(i == 0, 0.0, dzp_ref[hb - 1:hb, :].astype(F32))
        d_next = jnp.where(i == nt - 1, 0.0, dzn_ref[0:1, :].astype(F32))
        rows = lax.broadcasted_iota(jnp.int32, (tm, 512), 0)
        d_up = jnp.where(rows == 0, d_prev, pltpu.roll(dzc, 1, 0))
        d_dn = jnp.where(rows == tm - 1, d_next, pltpu.roll(dzc, tm - 1, 0))
        dt = cw_ref[0:1, :] * d_dn + cw_ref[1:2, :] * dzc + cw_ref[2:3, :] * d_up
        u = u_ref[...].astype(F32)
        gc = gc_ref[...].astype(F32)
        dproj_o[:, 0:512] = (dt * gc).astype(BF16)
        dproj_o[:, 512:1024] = dgbz_ref[:, 0:512]
        dproj_o[:, 1024:1536] = (dt * u).astype(BF16)
        dproj_o[:, 1536:2048] = dgbz_ref[:, 512:1024]

        dqn = (dq1[0].astype(F32) + nat_q(dq4, 4) + nat_q(dq16, 16)) * (1.0 / 8.0)
        dk1 = jnp.concatenate([dkv1[0, j, 0] for j in range(N_PAIR)], axis=1)
        dv1 = jnp.concatenate([dkv1[0, j, 1] for j in range(N_PAIR)], axis=1)
        dkn = dk1 + nat_kv(dkv4, 4, 0) + nat_kv(dkv16, 16, 0)
        dvn = dv1 + nat_kv(dkv4, 4, 1) + nat_kv(dkv16, 16, 1)
        g = jnp.concatenate([dqn, dkn], axis=1) * qkw_ref[...]
        raw = qkr_ref[...].astype(F32)
        rr = lax.rsqrt(_group_sum(raw * raw, b_ref, split=False) * (1.0 / HEAD_DIM) + EPS)
        proj_gq = _group_sum(g * raw, b_ref) * (1.0 / HEAD_DIM)
        draw = rr * g - raw * (rr * rr * rr) * proj_gq
        dqkw_o[...] += jnp.sum(jnp.concatenate([dqn, dkn], axis=1) * raw * rr, axis=0, keepdims=True)
        dproj_o[:, 2048:3072] = draw.astype(BF16)
        dproj_o[:, 3072:3584] = dvn.astype(BF16)
        dproj_o[:, 3584:4096] = dza_ref[...]

        nt_dims = (((1,), (1,)), ((), ()))
        dh = lax.dot_general(dproj_o[:, 0:1024], w_ref[0], nt_dims, preferred_element_type=F32)
        for b in range(1, 4):
            dh += lax.dot_general(dproj_o[:, 1024 * b:1024 * b + 1024], w_ref[b], nt_dims,
                                  preferred_element_type=F32)

        xf = x_ref[...]
        r = lax.rsqrt(jnp.mean(xf * xf, axis=-1, keepdims=True) + EPS)
        gh = dh * nw_ref[...]
        dnw_o[...] += jnp.sum(dh * xf * r, axis=0, keepdims=True)
        mean_gx = jnp.mean(gh * xf, axis=-1, keepdims=True)
        gx_o[...] = dout_ref[...] + r * gh - xf * (r * r * r) * mean_gx

    row = lambda w, j=0: pl.BlockSpec((tm, w), lambda i: (i, j))
    full = lambda shp: pl.BlockSpec(shp, lambda i: (0,) * len(shp))
    prev = pl.BlockSpec((hb, 512), lambda i: (jnp.maximum(i * (tm // hb) - 1, 0), 0))
    nxt = pl.BlockSpec((hb, 512), lambda i: (jnp.minimum((i + 1) * (tm // hb), s // hb - 1), 0))
    return pl.pallas_call(
        body, name="bwd_tail", grid=(nt,),
        out_shape=(jax.ShapeDtypeStruct((s, 1024), F32), jax.ShapeDtypeStruct((s, 4096), BF16),
                   jax.ShapeDtypeStruct((1, 1024), F32), jax.ShapeDtypeStruct((1, 1024), F32)),
        in_specs=[_class_spec(d, 512, tm) for d in DILATIONS]
        + [pl.BlockSpec((d, N_PAIR, 2, tm // d, 128), lambda i: (0, 0, 0, i, 0)) for d in DILATIONS]
        + [row(1024), full((1, 1024)), row(512), row(1024), row(512), prev, nxt,
           row(512, 0), row(512, 2), full((8, 512)), _resident((4, 1024, 1024)), row(1024),
           full((1, 1024)), row(1024), full((256, 256))],
        out_specs=(row(1024), row(4096), full((1, 1024)), full((1, 1024))),
        scratch_shapes=[pltpu.VMEM((4, tm, 128), F32), pltpu.VMEM((4, tm, 128), F32)],
        compiler_params=_params(("arbitrary",)),
    )(*dq_g, *dkv_g, qkr, qkw, dza, dgbz, dzc, dzc, dzc, cg, cg, cw, wblk, x, norm_w, dout, b256)


def _wgrad(a, b, row_blocked, name):
    s, m = a.shape
    n = b.shape[1]
    tk = 1024
    ncol = min(n, 2048)
    nj, nk = n // ncol, s // tk

    def body(a_ref, b_ref, o_ref, acc):
        kk = pl.program_id(1)

        @pl.when(kk == 0)
        def _():
            acc[...] = jnp.zeros_like(acc)

        acc[...] += lax.dot_general(a_ref[...], b_ref[...].astype(BF16), (((0,), (0,)), ((), ())),
                                    preferred_element_type=F32)

        @pl.when(kk == nk - 1)
        def _():
            blocks, _, rows, _ = o_ref.shape
            for blk in range(blocks):
                for half in range(2):
                    if row_blocked:
                        r0 = (2 * blk + half) * rows
                        o_ref[blk, half] = acc[r0:r0 + rows, :].astype(BF16)
                    else:
                        o_ref[blk, half] = acc[half * rows:(half + 1) * rows,
                                               1024 * blk:1024 * (blk + 1)].astype(BF16)

    if row_blocked:
        out_shape = jax.ShapeDtypeStruct((4, 2, m // 8, 1024), BF16)
        out_spec = pl.BlockSpec((4, 2, m // 8, 1024), lambda j, k: (0, 0, 0, 0))
    else:
        out_shape = jax.ShapeDtypeStruct((n // 1024, 2, m // 2, 1024), BF16)
        out_spec = pl.BlockSpec((ncol // 1024, 2, m // 2, 1024), lambda j, k: (j, 0, 0, 0))
    return pl.pallas_call(
        body, name=name, grid=(nj, nk),
        out_shape=out_shape,
        in_specs=[pl.BlockSpec((tk, m), lambda j, k: (k, 0)), pl.BlockSpec((tk, ncol), lambda j, k: (k, j))],
        out_specs=out_spec,
        scratch_shapes=[pltpu.VMEM((m, ncol), F32)],
        compiler_params=_params(("parallel", "arbitrary")),
    )(a, b)


def _dbias(dsum_all, onehot_all):
    def body(ds_ref, oh_ref, o_ref):
        step = pl.program_id(0) * 3 + pl.program_id(1)

        @pl.when(step == 0)
        def _():
            o_ref[...] = jnp.zeros_like(o_ref)

        rowq = lax.broadcasted_iota(jnp.int32, (2 * QB, KB), 0) & (QB - 1)
        hrow = lax.broadcasted_iota(jnp.int32, (8, KB), 0)
        diag = jnp.zeros((8, KB), F32)
        for p in range(N_PAIR):
            y = ds_ref[p]
            for bit in range(7):
                sh = 1 << bit
                y = jnp.where((rowq & sh) != 0, pltpu.roll(y, KB - sh, 1), y)
            da = jnp.sum(y[0:QB], axis=0, keepdims=True)
            db = jnp.sum(y[QB:2 * QB], axis=0, keepdims=True)
            diag = jnp.where(hrow == 2 * p, da, diag)
            diag = jnp.where(hrow == 2 * p + 1, db, diag)
        o_ref[...] += jnp.dot(diag, oh_ref[...], preferred_element_type=F32, precision=lax.Precision.HIGHEST)

    return pl.pallas_call(
        body, name="dbias", grid=(3, 3),
        out_shape=jax.ShapeDtypeStruct((8, 128), F32),
        in_specs=[pl.BlockSpec((None, N_PAIR, None, 2 * QB, KB), lambda g, v: (g, 0, v, 0, 0)),
                  pl.BlockSpec((None, None, KB, 128), lambda g, v: (g, v, 0, 0))],
        out_specs=pl.BlockSpec((8, 128), lambda g, v: (0, 0)),
        compiler_params=_params(("arbitrary", "arbitrary")),
    )(dsum_all, onehot_all)


def _gsync(pw_in, pw_out, small):
    hin, hout = pw_in.shape[2], pw_out.shape[2]
    nsmall = small.shape[0]

    def body(pin_hbm, pout_hbm, small_ref, gin_o, gout_o, small_o,
             mine_in, recv_in, sbuf_in, rbuf_in, mine_out, recv_out, sbuf_out, rbuf_out, gather,
             lsem, asend, arecv, bsend, brecv, csend, crecv, ssend, srecv):
        x, y, c = lax.axis_index("x"), lax.axis_index("y"), lax.axis_index("c")
        b = 2 * x + y
        dev = 4 * x + 2 * y + c
        sib = (x, y, 1 - c)

        def rcopy(src, dst, ssem, rsem, to):
            return pltpu.make_async_remote_copy(src_ref=src, dst_ref=dst, send_sem=ssem, recv_sem=rsem,
                                                device_id=to, device_id_type=MESH)

        gather[dev] = small_ref[...]
        s_sends = []
        for k in range(1, 8):
            to = (x ^ (k >> 2), y ^ ((k >> 1) & 1), c ^ (k & 1))
            cp = rcopy(gather.at[dev], gather.at[dev], ssend.at[k - 1], srecv.at[k - 1], to)
            cp.start()
            s_sends.append(cp)

        a_in = rcopy(pin_hbm.at[:, 1 - c], recv_in, asend.at[0], arecv.at[0], sib)
        a_out = rcopy(pout_hbm.at[:, 1 - c], recv_out, asend.at[1], arecv.at[1], sib)
        a_in.start()
        a_out.start()
        l_in = pltpu.make_async_copy(pin_hbm.at[:, c], mine_in, lsem.at[0])
        l_out = pltpu.make_async_copy(pout_hbm.at[:, c], mine_out, lsem.at[1])
        l_in.start()
        l_out.start()
        l_in.wait()
        l_out.wait()

        def stage_b(a_cp, mine, recv, sbuf, rbuf, base):
            a_cp.wait_recv()
            sends = []
            for k in (1, 2, 3):
                bk = b ^ k
                sbuf[k - 1] = (mine[bk].astype(F32) + recv[bk].astype(F32)).astype(BF16)
                cp = rcopy(sbuf.at[k - 1], rbuf.at[k - 1], bsend.at[base + k - 1], brecv.at[base + k - 1],
                           (x ^ (k >> 1), y ^ (k & 1), c))
                cp.start()
                sends.append(cp)
            return sends

        b_in = stage_b(a_in, mine_in, recv_in, sbuf_in, rbuf_in, 0)
        b_out = stage_b(a_out, mine_out, recv_out, sbuf_out, rbuf_out, 3)

        def stage_c(b_sends, mine, recv, rbuf, g_o, half, idx):
            acc = mine[b].astype(F32) + recv[b].astype(F32)
            for k in (1, 2, 3):
                b_sends[k - 1].wait_recv()
                acc = acc + rbuf[k - 1].astype(F32)
            rows = g_o.at[pl.ds(pl.multiple_of(c * half, half), half), :]
            g_o[pl.ds(pl.multiple_of(c * half, half), half), :] = acc
            cp = rcopy(rows, rows, csend.at[idx], crecv.at[idx], sib)
            cp.start()
            return cp

        c_in = stage_c(b_in, mine_in, recv_in, rbuf_in, gin_o, hin, 0)
        c_out = stage_c(b_out, mine_out, recv_out, rbuf_out, gout_o, hout, 1)

        for cp in s_sends:
            cp.wait_recv()
        tot = gather[0]
        for d in range(1, 8):
            tot = tot + gather[d]
        small_o[...] = tot

        for g_o, half, idx in ((gin_o, hin, 0), (gout_o, hout, 1)):
            other = g_o.at[pl.ds(pl.multiple_of((1 - c) * half, half), half), :]
            rcopy(other, other, csend.at[idx], crecv.at[idx], sib).wait_recv()
        for cp in s_sends + [a_in, a_out] + b_in + b_out + [c_in, c_out]:
            cp.wait_send()

    vm = pl.BlockSpec(memory_space=pltpu.VMEM)
    hbm = pl.BlockSpec(memory_space=pl.ANY)
    return pl.pallas_call(
        body, name="gsync",
        out_shape=(jax.ShapeDtypeStruct((2 * hin, 1024), F32), jax.ShapeDtypeStruct((2 * hout, 1024), F32),
                   jax.ShapeDtypeStruct((nsmall, 128), F32)),
        in_specs=[hbm, hbm, vm], out_specs=(vm, vm, vm),
        scratch_shapes=[pltpu.VMEM((4, hin, 1024), BF16), pltpu.VMEM((4, hin, 1024), BF16),
                        pltpu.VMEM((3, hin, 1024), BF16), pltpu.VMEM((3, hin, 1024), BF16),
                        pltpu.VMEM((4, hout, 1024), BF16), pltpu.VMEM((4, hout, 1024), BF16),
                        pltpu.VMEM((3, hout, 1024), BF16), pltpu.VMEM((3, hout, 1024), BF16),
                        pltpu.VMEM((8, nsmall, 128), F32),
                        pltpu.SemaphoreType.DMA((2,)),
                        pltpu.SemaphoreType.DMA((2,)), pltpu.SemaphoreType.DMA((2,)),
                        pltpu.SemaphoreType.DMA((6,)), pltpu.SemaphoreType.DMA((6,)),
                        pltpu.SemaphoreType.DMA((2,)), pltpu.SemaphoreType.DMA((2,)),
                        pltpu.SemaphoreType.DMA((7,)), pltpu.SemaphoreType.DMA((7,))],
        compiler_params=_params(),
    )(pw_in, pw_out, small)


def _adamw_math(w, g, m, v):
    m = ADAM_B1 * m + (1.0 - ADAM_B1) * g
    v = ADAM_B2 * v + (1.0 - ADAM_B2) * (g * g)
    m_hat = m / (1.0 - ADAM_B1 ** ADAM_STEP)
    v_hat = v / (1.0 - ADAM_B2 ** ADAM_STEP)
    delta = -ADAM_LR * (m_hat / (jnp.sqrt(v_hat) + ADAM_EPS) + ADAM_WD * w)
    return delta, m, v


def _adamw(w, g, m, v, name):
    rows, cols = w.shape
    tr = 256 if rows % 256 == 0 else rows

    def body(w_ref, g_ref, m_ref, v_ref, d_o, m_o, v_o):
        d, m2, v2 = _adamw_math(w_ref[...], g_ref[...], m_ref[...], v_ref[...])
        d_o[...] = d
        m_o[...] = m2
        v_o[...] = v2

    spec = pl.BlockSpec((tr, cols), lambda i: (i, 0))
    shp = jax.ShapeDtypeStruct((rows, cols), F32)
    return pl.pallas_call(
        body, name=name, grid=(rows // tr,), out_shape=(shp, shp, shp),
        in_specs=[spec] * 4, out_specs=(spec, spec, spec),
        compiler_params=_params(("parallel",)),
    )(w, g, m, v)


def _fold_heads(dqkw):
    def body(x_ref, o_ref):
        xs = x_ref[...]
        sq = xs[0:1] + xs[1:2] + xs[2:3] + xs[3:4]
        sk = xs[4:5] + xs[5:6] + xs[6:7] + xs[7:8]
        both = jnp.concatenate([sq, sk], axis=0)
        o_ref[...] = both + pltpu.roll(both, HEAD_DIM, 1)

    vm = pl.BlockSpec(memory_space=pltpu.VMEM)
    return pl.pallas_call(body, name="fold_heads", out_shape=jax.ShapeDtypeStruct((2, 128), F32),
                          in_specs=[vm], out_specs=vm, compiler_params=_params())(dqkw)


def kernel(x, norm_w, w_in, conv_w, conv_b, q_norm_w, k_norm_w, rel_bias, w_out, loss_target, m_norm_w, m_w_in, m_conv_w, m_conv_b, m_q_norm_w, m_k_norm_w, m_rel_bias, m_w_out, v_norm_w, v_w_in, v_conv_w, v_conv_b, v_q_norm_w, v_k_norm_w, v_rel_bias, v_w_out):
    x2 = x[0]
    tgt = loss_target[0]
    blk = 2 * lax.axis_index("x") + lax.axis_index("y")

    conv_w8 = jnp.pad(conv_w, ((0, 5), (0, 0)))
    wblk, woutblk, cwblk = _wgather(w_in, w_out, conv_w8)
    wout_full = woutblk.reshape(1024, 1024)
    cw_full = cwblk.transpose(1, 0, 2).reshape(8, 512)

    qkw = jnp.concatenate([jnp.tile(q_norm_w, 8) * 0.125, jnp.tile(k_norm_w, 8)])[None, :]
    qkw_raw = jnp.concatenate([jnp.tile(q_norm_w, 8), jnp.tile(k_norm_w, 8)])[None, :]
    gidx = jnp.arange(256) // HEAD_DIM
    b256 = (gidx[:, None] == gidx[None, :]).astype(BF16)

    h, cg, qkr, qkn, vz, qkn4, v4, qkn16, v16 = _proj(x2, norm_w[None, :], wblk, qkw, b256)

    biases = _bias_tables(rel_bias)
    qkn_l = [qkn[None], qkn4, qkn16]
    v_l = [vz[None], v4, v16]
    o_g, lse_g = [], []
    for gi, d in enumerate(DILATIONS):
        o_l, lse_l = _attn_fwd(qkn_l[gi], v_l[gi], biases[gi], f"attn_fwd_d{d}")
        o_g.append(o_l)
        lse_g.append(lse_l)

    (y, dout, ld1, do1, dza, dgbz, dzc, loss_p, dcb, dcw, do4, ld4, do16, ld16) = _combine(
        o_g, lse_g, cg, vz, x2, tgt, wout_full, cw_full, conv_b[None, :], b256)

    dq_g, dkv_g, dsums = [], [], []
    for gi, (d, do_l, ld_l) in enumerate(zip(DILATIONS, (do1, do4, do16), (ld1, ld4, ld16))):
        dq_l, dkv_l, dsum = _attn_bwd(qkn_l[gi], v_l[gi], do_l, ld_l, biases[gi], f"attn_bwd_d{d}")
        dq_g.append(dq_l)
        dkv_g.append(dkv_l)
        dsums.append(dsum)

    grad_x, dproj, dnw, dqkw = _bwd_tail(dq_g, dkv_g, qkr, qkw_raw, dza, dgbz, dzc, cg, cw_full, wblk,
                                         x2, norm_w[None, :], dout, b256)

    pw_in = _wgrad(h, dproj, False, "wgrad_in")
    pw_out = _wgrad(y, dout, True, "wgrad_out")
    dbias8 = _dbias(jnp.stack(dsums, axis=0), jnp.stack([_diag_bucket_onehot(d) for d in DILATIONS], axis=0))

    small = jnp.concatenate([dnw.reshape(8, 128), dcb.reshape(4, 128), dqkw.reshape(8, 128),
                             dcw[0:3].reshape(12, 128), dbias8, jnp.pad(loss_p, ((0, 7), (0, 0)))], axis=0)
    g_win, g_wout, gsmall = _gsync(pw_in, pw_out, small)

    g_nw = gsmall[0:8].reshape(1024)
    g_cb = gsmall[8:12].reshape(512)
    folded = _fold_heads(gsmall[12:20])
    g_qw, g_kw = folded[0, 0:64], folded[1, 0:64]
    g_cw = lax.dynamic_slice(gsmall[20:32].reshape(3, 512), (0, blk * 128), (3, 128))
    g_rb = gsmall[32:40][:, 0:32].T
    loss = gsmall[40, 0]

    d_win, nm_win, nv_win = _adamw(w_in, g_win, m_w_in, v_w_in, "adamw_w_in")
    d_wout, nm_wout, nv_wout = _adamw(w_out, g_wout, m_w_out, v_w_out, "adamw_w_out")

    def pack(parts):
        rows = [parts[0].reshape(8, 128), parts[1].reshape(4, 128),
                jnp.pad(parts[2], (0, 64))[None, :], jnp.pad(parts[3], (0, 64))[None, :],
                parts[4], jnp.pad(parts[5].T, ((0, 0), (0, 96)))]
        return jnp.concatenate(rows, axis=0)

    ws = pack([norm_w, conv_b, q_norm_w, k_norm_w, conv_w, rel_bias])
    gs = pack([g_nw, g_cb, g_qw, g_kw, g_cw, g_rb])
    ms = pack([m_norm_w, m_conv_b, m_q_norm_w, m_k_norm_w, m_conv_w, m_rel_bias])
    vs = pack([v_norm_w, v_conv_b, v_q_norm_w, v_k_norm_w, v_conv_w, v_rel_bias])
    rpad = lambda a: jnp.pad(a, ((0, 7), (0, 0)))
    d_s, nm_s, nv_s = _adamw(rpad(ws), rpad(gs), rpad(ms), rpad(vs), "adamw_small")

    def unpack(a):
        return (a[0:8].reshape(1024), a[12:13, 0:64].reshape(64), a[13:14, 0:64].reshape(64),
                a[14:17], a[8:12].reshape(512), a[17:25, 0:32].T)

    def ordered(nw, win, cw, cb, qw, kw, rb, wout):
        return (nw, win, cw, cb, qw, kw, rb, wout)

    g_un = (g_nw, g_qw, g_kw, g_cw, g_cb, g_rb)
    outs = [loss, grad_x[None]]
    for un, win_v, wout_v in ((g_un, g_win, g_wout), (unpack(d_s), d_win, d_wout),
                              (unpack(nm_s), nm_win, nm_wout), (unpack(nv_s), nv_win, nv_wout)):
        nw, qw, kw, cw, cb, rb = un
        outs.extend(ordered(nw, win_v, cw, cb, qw, kw, rb, wout_v))
    return tuple(outs)
```

```python
import math

import jax
import jax.numpy as jnp
from jax import lax
from jax.experimental import pallas as pl
from jax.experimental.pallas import tpu as pltpu

F32 = jnp.float32
BF16 = jnp.bfloat16
MESH = pl.DeviceIdType.MESH

D_MODEL = 1024
CONV_W = 512
ATTN_W = 512
HEAD_DIM = 64
N_PAIR = 4
DILATIONS = (1, 4, 16)
HALF = 64
QB = 128
KB = QB + 2 * HALF
NUM_BUCKETS = 32
MAX_DISTANCE = 1024
EPS = 1e-6
NEG = -1e30
ADAM_LR, ADAM_B1, ADAM_B2, ADAM_EPS, ADAM_WD, ADAM_STEP = 0.001, 0.9, 0.999, 1e-08, 0.01, 10
VMEM_LIMIT = 48 << 20


def _params(sem=None, vmem=VMEM_LIMIT, **kw):
    if sem is not None:
        kw["dimension_semantics"] = sem
    return pltpu.CompilerParams(vmem_limit_bytes=vmem, **kw)


def _sigmoid(z):
    return 1.0 / (1.0 + jnp.exp(-z))


def _group_sum(val, b_ref, split=True):
    hi = val.astype(BF16)
    lo = (val - hi.astype(F32)).astype(BF16) if split else None
    outs = []
    for j in range(val.shape[1] // 256):
        sl = slice(256 * j, 256 * j + 256)
        part = jnp.dot(hi[:, sl], b_ref[...], preferred_element_type=F32)
        if split:
            part = part + jnp.dot(lo[:, sl], b_ref[...], preferred_element_type=F32)
        outs.append(part)
    return outs[0] if len(outs) == 1 else jnp.concatenate(outs, axis=1)


def _t5_bucket(rel):
    half_b = NUM_BUCKETS // 2
    max_exact = half_b // 2
    ret = jnp.where(rel > 0, half_b, 0)
    n = jnp.abs(rel)
    nf = jnp.maximum(n, 1).astype(F32)
    large = max_exact + (jnp.log(nf / max_exact) / math.log(MAX_DISTANCE / max_exact)
                         * (half_b - max_exact)).astype(jnp.int32)
    large = jnp.minimum(large, half_b - 1)
    return ret + jnp.where(n < max_exact, n, large)


def _window_rel(variant):
    off = (0, HALF, 2 * HALF)[variant]
    return jnp.arange(KB)[None, :] - off - jnp.arange(QB)[:, None]


def _bias_tables(rel_bias):
    bkts = []
    for dilation in DILATIONS:
        for variant in range(3):
            rel = _window_rel(variant)
            bkt = _t5_bucket(jnp.clip(rel, -HALF, HALF) * dilation)
            bkts.append(jnp.where(jnp.abs(rel) <= HALF, bkt, -1))
    bkt_all = jnp.stack(bkts, axis=0).astype(jnp.int32)

    def body(rb_ref, bkt_ref, o_ref):
        bkt = bkt_ref[...]
        for h in range(8):
            acc = jnp.full((QB, KB), NEG, F32)
            for b in range(NUM_BUCKETS):
                acc = jnp.where(bkt == b, rb_ref[b, h], acc)
            o_ref[h] = acc

    out = pl.pallas_call(
        body, name="bias_tables", grid=(9,),
        out_shape=jax.ShapeDtypeStruct((9, 8, QB, KB), F32),
        in_specs=[pl.BlockSpec(memory_space=pltpu.SMEM), pl.BlockSpec((None, QB, KB), lambda i: (i, 0, 0))],
        out_specs=pl.BlockSpec((None, 8, QB, KB), lambda i: (i, 0, 0, 0)),
        compiler_params=_params(("parallel",)),
    )(rel_bias, bkt_all)
    return out.reshape(3, 3, N_PAIR, 2 * QB, KB)


def _diag_bucket_onehot(dilation):
    out = []
    c = jnp.arange(KB)
    for variant in range(3):
        off = (0, HALF, 2 * HALF)[variant]
        rel = ((c - off + 128) % 256) - 128
        band = jnp.abs(rel) <= HALF
        bkt = _t5_bucket(jnp.clip(rel, -HALF, HALF) * dilation)
        oh = (bkt[:, None] == jnp.arange(128)[None, :]) & band[:, None]
        out.append(oh.astype(F32))
    return jnp.stack(out, axis=0)


def _wgather(w_in, w_out, conv_w):
    rin, rout = w_in.shape[0] // 2, w_out.shape[0] // 2

    def body(win_ref, wout_ref, cw_ref, win_o, wout_o, cw_o, send_sems, recv_sems):
        x, y, c = lax.axis_index("x"), lax.axis_index("y"), lax.axis_index("c")
        b = 2 * x + y
        win_o[b] = win_ref[...].astype(BF16)
        wout_o[b] = wout_ref[...].astype(BF16)
        cw_o[b] = cw_ref[...]

        def peer(k):
            return (x ^ (k >> 1), y ^ (k & 1))

        def piece(ref, blk, half_rows, core):
            return ref.at[blk, pl.ds(core * half_rows, half_rows), :]

        def copy(sem, src, dst, to):
            return pltpu.make_async_remote_copy(src_ref=src, dst_ref=dst, send_sem=send_sems.at[sem],
                                                recv_sem=recv_sems.at[sem], device_id=to, device_id_type=MESH)

        sends = []
        for k in (1, 2, 3):
            px, py = peer(k)
            sends.append(copy(k - 1, piece(win_o, b, rin, c), piece(win_o, b, rin, c), (px, py, c)))
            sends.append(copy(3 + k - 1, piece(wout_o, b, rout, c), piece(wout_o, b, rout, c), (px, py, c)))
            sends.append(copy(6 + k - 1, cw_o.at[b], cw_o.at[b], (px, py, c)))
        for cp in sends:
            cp.start()
        fwd = []
        for k in (1, 2, 3):
            px, py = peer(k)
            bk = 2 * px + py
            copy(k - 1, piece(win_o, bk, rin, c), piece(win_o, bk, rin, c), (px, py, c)).wait_recv()
            f = copy(9 + k - 1, piece(win_o, bk, rin, c), piece(win_o, bk, rin, c), (x, y, 1 - c))
            f.start()
            fwd.append(f)
            copy(3 + k - 1, piece(wout_o, bk, rout, c), piece(wout_o, bk, rout, c), (px, py, c)).wait_recv()
            f = copy(12 + k - 1, piece(wout_o, bk, rout, c), piece(wout_o, bk, rout, c), (x, y, 1 - c))
            f.start()
            fwd.append(f)
            copy(6 + k - 1, cw_o.at[bk], cw_o.at[bk], (px, py, c)).wait_recv()
        for k in (1, 2, 3):
            px, py = peer(k)
            bk = 2 * px + py
            copy(9 + k - 1, piece(win_o, bk, rin, 1 - c), piece(win_o, bk, rin, 1 - c), (x, y, 1 - c)).wait_recv()
            copy(12 + k - 1, piece(wout_o, bk, rout, 1 - c), piece(wout_o, bk, rout, 1 - c), (x, y, 1 - c)).wait_recv()
        for cp in sends + fwd:
            cp.wait_send()

    vm = pl.BlockSpec(memory_space=pltpu.VMEM)
    return pl.pallas_call(
        body, name="wgather",
        out_shape=(jax.ShapeDtypeStruct((4,) + w_in.shape, BF16),
                   jax.ShapeDtypeStruct((4,) + w_out.shape, BF16),
                   jax.ShapeDtypeStruct((4,) + conv_w.shape, F32)),
        in_specs=[vm, vm, vm], out_specs=(vm, vm, vm),
        scratch_shapes=[pltpu.SemaphoreType.DMA((15,)), pltpu.SemaphoreType.DMA((15,))],
        compiler_params=_params(),
    )(w_in, w_out, conv_w)


TM_MATMUL = 512
TM_COMBINE = 256


def _resident(shape):
    return pl.BlockSpec(shape, lambda i: (0,) * len(shape), pipeline_mode=pl.Buffered(1))


def _to_slabs(slab, val, j0=0):
    for j in range(val.shape[1] // 128):
        slab[j0 + j] = val[:, 128 * j:128 * (j + 1)]


def _scatter_classes(slab, j0, nj, out_ref, d, part=0, mid=None):
    tm = slab.shape[1]
    n = tm // d
    if d == 4:
        for r in range(d):
            for j in range(nj):
                out_ref[r, part * n:(part + 1) * n, 128 * j:128 * (j + 1)] = (
                    slab[j0 + j, pl.ds(r, n, stride=d), :].astype(out_ref.dtype))
        return
    q = tm // 4
    for lo in range(4):
        for j in range(nj):
            mid[j0 + j, lo * q:(lo + 1) * q, :] = slab[j0 + j, pl.ds(lo, q, stride=4), :]
    for hi in range(4):
        for lo in range(4):
            for j in range(nj):
                out_ref[4 * hi + lo, part * n:(part + 1) * n, 128 * j:128 * (j + 1)] = (
                    mid[j0 + j, pl.ds(lo * q + hi, n, stride=4), :].astype(out_ref.dtype))


def _gather_classes(slab, piece, nj, d, mid=None):
    tm = slab.shape[1]
    n = tm // d
    if d == 4:
        for r in range(d):
            for j in range(nj):
                slab[j, pl.ds(r, n, stride=d), :] = piece(r, j).astype(F32)
    else:
        q = tm // 4
        for hi in range(4):
            for lo in range(4):
                for j in range(nj):
                    mid[j, pl.ds(lo * q + hi, n, stride=4), :] = piece(4 * hi + lo, j).astype(F32)
        for lo in range(4):
            for j in range(nj):
                slab[j, pl.ds(lo, q, stride=4), :] = mid[j, lo * q:(lo + 1) * q, :]
    return jnp.concatenate([slab[j] for j in range(nj)], axis=1)


def _class_spec(d, width, tm):
    return pl.BlockSpec((d, tm // d, width), lambda i: (0, i, 0))


def _proj(x, norm_w, wblk, qkw, b256):
    s = x.shape[0]
    tm = TM_MATMUL
    nparts = 2
    tp = tm // nparts

    def body(x_ref, nw_ref, w_ref, qkw_ref, b_ref, h_o, cg_o, qkr_o, qkn_o, vz_o, qkn4_o, v4_o, qkn16_o, v16_o,
             slabs, mids):
        for part in range(nparts):
            rows = slice(part * tp, (part + 1) * tp)
            slab = slabs.at[part]
            xf = x_ref[rows, :]
            r = lax.rsqrt(jnp.mean(xf * xf, axis=-1, keepdims=True) + EPS)
            h = (xf * r * nw_ref[...]).astype(BF16)
            h_o[rows, :] = h
            p2 = jnp.dot(h, w_ref[2], preferred_element_type=F32)
            qkr_o[rows, :] = p2.astype(BF16)
            ss = _group_sum(p2 * p2, b_ref, split=False)
            rr = lax.rsqrt(ss * (1.0 / HEAD_DIM) + EPS)
            qkn = p2 * rr * qkw_ref[...]
            qkn_o[rows, :] = qkn.astype(BF16)
            _to_slabs(slab, qkn)
            p3 = jnp.dot(h, w_ref[3], preferred_element_type=F32)
            vz_o[rows, :] = p3.astype(BF16)
            _to_slabs(slab, p3[:, 0:512], 8)
            cg_o[rows, 0:1024] = jnp.dot(h, w_ref[0], preferred_element_type=F32).astype(BF16)
            cg_o[rows, 1024:2048] = jnp.dot(h, w_ref[1], preferred_element_type=F32).astype(BF16)
            for d, q_o, v_o in ((4, qkn4_o, v4_o), (16, qkn16_o, v16_o)):
                _scatter_classes(slab, 0, 8, q_o, d, part, mids.at[part])
                _scatter_classes(slab, 8, 4, v_o, d, part, mids.at[part])

    row = lambda w: pl.BlockSpec((tm, w), lambda i: (i, 0))
    full = lambda shp: pl.BlockSpec(shp, lambda i: (0,) * len(shp))
    nat = lambda w: jax.ShapeDtypeStruct((s, w), BF16)
    cls = lambda d, w: jax.ShapeDtypeStruct((d, s // d, w), BF16)
    return pl.pallas_call(
        body, name="proj", grid=(s // tm,),
        out_shape=(nat(1024), nat(2048), nat(1024), nat(1024), nat(1024),
                   cls(4, 1024), cls(4, 512), cls(16, 1024), cls(16, 512)),
        in_specs=[row(1024), full((1, 1024)), _resident((4, 1024, 1024)), full((1, 1024)), full((256, 256))],
        out_specs=(row(1024), row(2048), row(1024), row(1024), row(1024),
                   _class_spec(4, 1024, tm), _class_spec(4, 512, tm),
                   _class_spec(16, 1024, tm), _class_spec(16, 512, tm)),
        scratch_shapes=[pltpu.VMEM((nparts, 12, tp, 128), F32), pltpu.VMEM((nparts, 12, tp, 128), F32)],
        compiler_params=_params(("parallel",)),
    )(x, norm_w, wblk, qkw, b256)


def _block_coords(t, i, nsub, nb, length):
    n = t * nsub + i
    q0 = i * QB
    start = pl.multiple_of(jnp.clip(n * QB - HALF, 0, length - KB), HALF)
    variant = jnp.where(n == 0, 0, jnp.where(n == nb - 1, 2, 1))
    return q0, start, variant


def _split_heads(a, lo):
    zero = jnp.zeros_like(a)
    return jnp.concatenate([jnp.where(lo, a, zero), jnp.where(lo, zero, a)], axis=0)


def _col_pair(ref, q0, lane):
    return jnp.concatenate([ref[pl.ds(q0, QB), lane:lane + 1],
                            ref[pl.ds(q0, QB), HEAD_DIM + lane:HEAD_DIM + lane + 1]], axis=0)


def _attn_fwd(qkn_l, v_l, bias, name):
    r_cls, length, _ = qkn_l.shape
    qt = min(length, 2048)
    nb, nsub = length // QB, qt // QB

    def body(q_ref, k_ref, v_ref, b_ref, o_ref, lse_ref):
        t = pl.program_id(2)
        lo = lax.broadcasted_iota(jnp.int32, (QB, 128), 1) < HEAD_DIM

        starts, logits = [], []
        for i in range(nsub):
            _, start, variant = _block_coords(t, i, nsub, nb, length)
            qq = _split_heads(q_ref[i * QB:(i + 1) * QB, :], lo)
            k = k_ref[pl.ds(start, KB), :]
            logits.append(lax.dot_general(qq, k, (((1,), (1,)), ((), ())), preferred_element_type=F32)
                          + b_ref[variant])
            starts.append(start)
        lg = jnp.concatenate(logits, axis=0)
        m = jnp.max(lg, axis=-1, keepdims=True)
        p = jnp.exp(lg - m)
        pb = p.astype(BF16)
        l = jnp.sum(p, axis=-1, keepdims=True)
        lse = jnp.broadcast_to(m + jnp.log(l), (nsub * 2 * QB, 128))
        inv = 1.0 / l
        for i in range(nsub):
            rows = slice(2 * QB * i, 2 * QB * (i + 1))
            v = v_ref[pl.ds(starts[i], KB), :]
            pv = jnp.dot(pb[rows], v, preferred_element_type=F32) * inv[rows]
            o_ref[i * QB:(i + 1) * QB, :] = jnp.where(lo, pv[0:QB], pv[QB:2 * QB]).astype(BF16)
            ls = lse[rows]
            lse_ref[i * QB:(i + 1) * QB, :] = jnp.where(lo, ls[0:QB], ls[QB:2 * QB])

    return pl.pallas_call(
        body, name=name, grid=(r_cls, N_PAIR, length // qt),
        out_shape=(jax.ShapeDtypeStruct((r_cls, length, 512), BF16),
                   jax.ShapeDtypeStruct((r_cls, length, 512), F32)),
        in_specs=[pl.BlockSpec((None, qt, 128), lambda r, p, t: (r, t, p)),
                  pl.BlockSpec((None, length, 128), lambda r, p, t: (r, 0, 4 + p)),
                  pl.BlockSpec((None, length, 128), lambda r, p, t: (r, 0, p)),
                  pl.BlockSpec((3, None, 2 * QB, KB), lambda r, p, t: (0, p, 0, 0))],
        out_specs=(pl.BlockSpec((None, qt, 128), lambda r, p, t: (r, t, p)),
                   pl.BlockSpec((None, qt, 128), lambda r, p, t: (r, t, p))),
        compiler_params=_params(("parallel", "parallel", "arbitrary")),
    )(qkn_l, qkn_l, v_l, bias)


def _combine(o_g, lse_g, cg, vz, x, tgt, wout, cw, cb, b256):
    s = x.shape[0]
    tm = TM_COMBINE
    hb = 16
    nt = s // tm

    def body(o1, o4, o16, l1, l4, l16, cg_ref, cgp_ref, cgn_ref, za_ref, x_ref, t_ref, w_ref, cw_ref, cb_ref,
             b_ref, y_o, dout_o, ld1_o, do1_o, dza_o, dgbz_o, dzc_o, loss_o, dcb_o, dcw_o,
             do4_o, ld4_o, do16_o, ld16_o, slab, mid):
        i = pl.program_id(0)

        @pl.when(i == 0)
        def _():
            loss_o[...] = jnp.zeros_like(loss_o)
            dcb_o[...] = jnp.zeros_like(dcb_o)
            dcw_o[...] = jnp.zeros_like(dcw_o)

        u = cg_ref[:, 0:512].astype(F32)
        gb = cg_ref[:, 512:1024].astype(F32)
        gc = cg_ref[:, 1024:1536].astype(F32)
        zc = cg_ref[:, 1536:2048].astype(F32)
        tt = gc * u
        t_prev = cgp_ref[hb - 1:hb, 0:512].astype(F32) * cgp_ref[hb - 1:hb, 1024:1536].astype(F32)
        t_next = cgn_ref[0:1, 0:512].astype(F32) * cgn_ref[0:1, 1024:1536].astype(F32)
        t_prev = jnp.where(i == 0, 0.0, t_prev)
        t_next = jnp.where(i == nt - 1, 0.0, t_next)
        rows = lax.broadcasted_iota(jnp.int32, (tm, 512), 0)
        t_up = jnp.where(rows == 0, t_prev, pltpu.roll(tt, 1, 0))
        t_dn = jnp.where(rows == tm - 1, t_next, pltpu.roll(tt, tm - 1, 0))
        w0, w1, w2 = cw_ref[0:1, :], cw_ref[1:2, :], cw_ref[2:3, :]
        zb = w0 * t_up + w1 * tt + w2 * t_dn + cb_ref[...]
        sg = _sigmoid(zc)
        sz = zc * sg
        y_conv = gb * zb * sz

        a1, p1 = l1[0], o1[0].astype(F32)
        a4 = _gather_classes(slab, lambda r, j: l4[r, :, 128 * j:128 * (j + 1)], 4, 4)
        p4 = _gather_classes(slab, lambda r, j: o4[r, :, 128 * j:128 * (j + 1)], 4, 4)
        a16 = _gather_classes(slab, lambda r, j: l16[r, :, 128 * j:128 * (j + 1)], 4, 16, mid)
        p16 = _gather_classes(slab, lambda r, j: o16[r, :, 128 * j:128 * (j + 1)], 4, 16, mid)
        m = jnp.maximum(jnp.maximum(a1, a4), a16)
        e1, e4, e16 = jnp.exp(a1 - m), jnp.exp(a4 - m), jnp.exp(a16 - m)
        den = e1 + e4 + e16
        lse = m + jnp.log(den)
        o = (e1 * p1 + e4 * p4 + e16 * p16) / den
        za = za_ref[...].astype(F32)
        sga = _sigmoid(za)
        sa = za * sga
        y = jnp.concatenate([y_conv, o * sa], axis=1).astype(BF16)
        y_o[...] = y

        out = x_ref[...] + jnp.dot(y, w_ref[...], preferred_element_type=F32)
        diff = out - t_ref[...]
        loss_o[...] += (0.5 / D_MODEL) * jnp.sum(diff * diff)
        dout = diff * (1.0 / D_MODEL)
        dout_o[...] = dout
        dy = lax.dot_general(dout.astype(BF16), w_ref[...], (((1,), (1,)), ((), ())), preferred_element_type=F32)
        dyc, dya = dy[:, 0:512], dy[:, 512:1024]

        do = dya * sa
        dza_o[...] = (dya * o * (sga * (1.0 + za * (1.0 - sga)))).astype(BF16)
        lane = lax.broadcasted_iota(jnp.int32, (tm, 512), 1)
        ld = jnp.where((lane & (HEAD_DIM - 1)) < HEAD_DIM // 2, lse, _group_sum(do * o, b_ref))
        do1_o[0] = do.astype(BF16)
        ld1_o[0] = ld
        _to_slabs(slab, do)
        _scatter_classes(slab, 0, 4, do4_o, 4)
        _scatter_classes(slab, 0, 4, do16_o, 16, 0, mid)
        _to_slabs(slab, ld)
        _scatter_classes(slab, 0, 4, ld4_o, 4)
        _scatter_classes(slab, 0, 4, ld16_o, 16, 0, mid)

        dzc = dyc * sz * gb
        dzc_o[...] = dzc.astype(BF16)
        dgbz_o[:, 0:512] = (dyc * sz * zb).astype(BF16)
        dgbz_o[:, 512:1024] = (dyc * gb * zb * (sg * (1.0 + zc * (1.0 - sg)))).astype(BF16)
        dcb_o[...] += jnp.sum(dzc, axis=0, keepdims=True)
        dcw_o[0:1, :] += jnp.sum(dzc * t_up, axis=0, keepdims=True)
        dcw_o[1:2, :] += jnp.sum(dzc * tt, axis=0, keepdims=True)
        dcw_o[2:3, :] += jnp.sum(dzc * t_dn, axis=0, keepdims=True)

    row = lambda w, j=0: pl.BlockSpec((tm, w), lambda i: (i, j))
    full = lambda shp: pl.BlockSpec(shp, lambda i: (0,) * len(shp))
    prev = pl.BlockSpec((hb, 2048), lambda i: (jnp.maximum(i * (tm // hb) - 1, 0), 0))
    nxt = pl.BlockSpec((hb, 2048), lambda i: (jnp.minimum((i + 1) * (tm // hb), s // hb - 1), 0))
    cls = lambda d, dt: jax.ShapeDtypeStruct((d, s // d, 512), dt)
    cspecs = [_class_spec(d, 512, tm) for d in DILATIONS]
    return pl.pallas_call(
        body, name="combine", grid=(nt,),
        out_shape=(jax.ShapeDtypeStruct((s, 1024), BF16), jax.ShapeDtypeStruct((s, 1024), F32),
                   cls(1, F32), cls(1, BF16), jax.ShapeDtypeStruct((s, 512), BF16),
                   jax.ShapeDtypeStruct((s, 1024), BF16), jax.ShapeDtypeStruct((s, 512), BF16),
                   jax.ShapeDtypeStruct((1, 128), F32), jax.ShapeDtypeStruct((1, 512), F32),
                   jax.ShapeDtypeStruct((8, 512), F32),
                   cls(4, BF16), cls(4, F32), cls(16, BF16), cls(16, F32)),
        in_specs=cspecs + cspecs + [row(2048), prev, nxt, row(512, 1), row(1024), row(1024),
                                    _resident((1024, 1024)), full((8, 512)), full((1, 512)), full((256, 256))],
        out_specs=(row(1024), row(1024), cspecs[0], cspecs[0], row(512), row(1024), row(512),
                   full((1, 128)), full((1, 512)), full((8, 512)),
                   cspecs[1], cspecs[1], cspecs[2], cspecs[2]),
        scratch_shapes=[pltpu.VMEM((4, tm, 128), F32), pltpu.VMEM((4, tm, 128), F32)],
        compiler_params=_params(("arbitrary",)),
    )(*o_g, *lse_g, cg, cg, cg, vz, x, tgt, wout, cw, cb, b256)


def _attn_bwd(qkn_l, v_l, do_l, ld_l, bias, name):
    r_cls, length, _ = qkn_l.shape
    qt = min(length, 2048 if length <= 4096 else 1024)
    nb, nsub, nt = length // QB, qt // QB, length // qt
    chunk = min(length, 4096)
    nchunk = length // chunk

    def body(q_ref, k_ref, v_ref, do_ref, ld_ref, b_ref, dq_ref, dkv_hbm, dsum_ref, dk_acc, dv_acc, stage, sems):
        p_id, r, t = pl.program_id(0), pl.program_id(1), pl.program_id(2)
        lo = lax.broadcasted_iota(jnp.int32, (QB, 128), 1) < HEAD_DIM

        @pl.when(t == 0)
        def _():
            dk_acc[...] = jnp.zeros_like(dk_acc)
            dv_acc[...] = jnp.zeros_like(dv_acc)

        @pl.when((t == 0) & (r == 0))
        def _():
            dsum_ref[...] = jnp.zeros_like(dsum_ref)

        nt_dims = (((1,), (1,)), ((), ()))
        tn_dims = (((0,), (0,)), ((), ()))
        coords, qqs, dds, logits, dps, lcols, dcols = [], [], [], [], [], [], []
        for i in range(nsub):
            q0, start, variant = _block_coords(t, i, nsub, nb, length)
            qq = _split_heads(q_ref[q0:q0 + QB, :], lo)
            dd = _split_heads(do_ref[q0:q0 + QB, :], lo)
            k = k_ref[pl.ds(start, KB), :]
            v = v_ref[pl.ds(start, KB), :]
            logits.append(lax.dot_general(qq, k, nt_dims, preferred_element_type=F32) + b_ref[variant])
            dps.append(lax.dot_general(dd, v, nt_dims, preferred_element_type=F32))
            lcols.append(_col_pair(ld_ref, q0, 0))
            dcols.append(_col_pair(ld_ref, q0, HEAD_DIM // 2))
            coords.append((q0, start, variant))
            qqs.append(qq)
            dds.append(dd)
        p = jnp.exp(jnp.concatenate(logits, axis=0) - jnp.concatenate(lcols, axis=0))
        ds = p * (jnp.concatenate(dps, axis=0) - jnp.concatenate(dcols, axis=0))
        pb = p.astype(BF16)
        dsb = ds.astype(BF16)
        middle = None
        for i in range(nsub):
            q0, start, variant = coords[i]
            rows = slice(2 * QB * i, 2 * QB * (i + 1))
            if 0 < i < nsub - 1:
                middle = ds[rows] if middle is None else middle + ds[rows]
            else:
                dsum_ref[variant] += ds[rows]
            dqq = jnp.dot(dsb[rows], k_ref[pl.ds(start, KB), :], preferred_element_type=F32)
            dq_ref[q0:q0 + QB, :] = jnp.where(lo, dqq[0:QB], dqq[QB:2 * QB]).astype(BF16)
            dk_acc[pl.ds(start, KB), :] += lax.dot_general(dsb[rows], qqs[i], tn_dims, preferred_element_type=F32)
            dv_acc[pl.ds(start, KB), :] += lax.dot_general(pb[rows], dds[i], tn_dims, preferred_element_type=F32)
        if middle is not None:
            dsum_ref[1] += middle

        @pl.when(t == nt - 1)
        def _():
            def copy(k):
                which, c = k // nchunk, k % nchunk
                rows = pl.ds(c * chunk, chunk)
                return pltpu.make_async_copy(stage.at[k % 2], dkv_hbm.at[r, p_id, which, rows, :], sems.at[k % 2])

            for k in range(2 * nchunk):
                if k < 2:
                    @pl.when((p_id > 0) | (r > 0))
                    def _():
                        copy(k).wait()
                else:
                    copy(k).wait()
                acc = (dk_acc, dv_acc)[k // nchunk]
                stage[k % 2] = acc[pl.ds((k % nchunk) * chunk, chunk), :].astype(BF16)
                copy(k).start()

            @pl.when((p_id == N_PAIR - 1) & (r == r_cls - 1))
            def _():
                copy(0).wait()
                copy(1).wait()

    qspec = pl.BlockSpec((None, qt, 128), lambda p, r, t: (r, t, p))
    return pl.pallas_call(
        body, name=name, grid=(N_PAIR, r_cls, nt),
        out_shape=(jax.ShapeDtypeStruct((r_cls, length, 512), BF16),
                   jax.ShapeDtypeStruct((r_cls, N_PAIR, 2, length, 128), BF16),
                   jax.ShapeDtypeStruct((N_PAIR, 3, 2 * QB, KB), F32)),
        in_specs=[qspec,
                  pl.BlockSpec((None, length, 128), lambda p, r, t: (r, 0, 4 + p)),
                  pl.BlockSpec((None, length, 128), lambda p, r, t: (r, 0, p)),
                  qspec, qspec,
                  pl.BlockSpec((3, None, 2 * QB, KB), lambda p, r, t: (0, p, 0, 0))],
        out_specs=(qspec, pl.BlockSpec(memory_space=pl.ANY),
                   pl.BlockSpec((None, 3, 2 * QB, KB), lambda p, r, t: (p, 0, 0, 0))),
        scratch_shapes=[pltpu.VMEM((length, 128), F32), pltpu.VMEM((length, 128), F32),
                        pltpu.VMEM((2, chunk, 128), BF16), pltpu.SemaphoreType.DMA((2,))],
        compiler_params=_params(("arbitrary", "arbitrary", "arbitrary")),
    )(qkn_l, qkn_l, v_l, do_l, ld_l, bias)


def _bwd_tail(dq_g, dkv_g, qkr, qkw, dza, dgbz, dzc, cg, cw, wblk, x, norm_w, dout, b256):
    s = x.shape[0]
    tm = TM_COMBINE
    hb = 16
    nt = s // tm

    def body(dq1, dq4, dq16, dkv1, dkv4, dkv16, qkr_ref, qkw_ref, dza_ref, dgbz_ref, dzc_ref,
             dzp_ref, dzn_ref, u_ref, gc_ref, cw_ref, w_ref, x_ref, nw_ref, dout_ref, b_ref,
             gx_o, dproj_o, dnw_o, dqkw_o, slab, mid):
        i = pl.program_id(0)

        def nat_q(ref, d):
            return _gather_classes(slab, lambda r, j: ref[r, :, 128 * j:128 * (j + 1)], 4, d, mid)

        def nat_kv(ref, d, which):
            return _gather_classes(slab, lambda r, j: ref[r, j, which], 4, d, mid)

        @pl.when(i == 0)
        def _():
            dnw_o[...] = jnp.zeros_like(dnw_o)
            dqkw_o[...] = jnp.zeros_like(dqkw_o)

        dzc = dzc_ref[...].astype(F32)
        d_prev = jnp.where(i == 0, 0.0, dzp_ref[hb - 1:hb, :].astype(F32))
        d_next = jnp.where(i == nt - 1, 0.0, dzn_ref[0:1, :].astype(F32))
        rows = lax.broadcasted_iota(jnp.int32, (tm, 512), 0)
        d_up = jnp.where(rows == 0, d_prev, pltpu.roll(dzc, 1, 0))
        d_dn = jnp.where(rows == tm - 1, d_next, pltpu.roll(dzc, tm - 1, 0))
        dt = cw_ref[0:1, :] * d_dn + cw_ref[1:2, :] * dzc + cw_ref[2:3, :] * d_up
        u = u_ref[...].astype(F32)
        gc = gc_ref[...].astype(F32)
        dproj_o[:, 0:512] = (dt * gc).astype(BF16)
        dproj_o[:, 512:1024] = dgbz_ref[:, 0:512]
        dproj_o[:, 1024:1536] = (dt * u).astype(BF16)
        dproj_o[:, 1536:2048] = dgbz_ref[:, 512:1024]

        dqn = (dq1[0].astype(F32) + nat_q(dq4, 4) + nat_q(dq16, 16)) * (1.0 / 8.0)
        dk1 = jnp.concatenate([dkv1[0, j, 0] for j in range(N_PAIR)], axis=1)
        dv1 = jnp.concatenate([dkv1[0, j, 1] for j in range(N_PAIR)], axis=1)
        dkn = dk1 + nat_kv(dkv4, 4, 0) + nat_kv(dkv16, 16, 0)
        dvn = dv1 + nat_kv(dkv4, 4, 1) + nat_kv(dkv16, 16, 1)
        g = jnp.concatenate([dqn, dkn], axis=1) * qkw_ref[...]
        raw = qkr_ref[...].astype(F32)
        rr = lax.rsqrt(_group_sum(raw * raw, b_ref, split=False) * (1.0 / HEAD_DIM) + EPS)
        proj_gq = _group_sum(g * raw, b_ref) * (1.0 / HEAD_DIM)
        draw = rr * g - raw * (rr * rr * rr) * proj_gq
        dqkw_o[...] += jnp.sum(jnp.concatenate([dqn, dkn], axis=1) * raw * rr, axis=0, keepdims=True)
        dproj_o[:, 2048:3072] = draw.astype(BF16)
        dproj_o[:, 3072:3584] = dvn.astype(BF16)
        dproj_o[:, 3584:4096] = dza_ref[...]

        nt_dims = (((1,), (1,)), ((), ()))
        dh = lax.dot_general(dproj_o[:, 0:1024], w_ref[0], nt_dims, preferred_element_type=F32)
        for b in range(1, 4):
            dh += lax.dot_general(dproj_o[:, 1024 * b:1024 * b + 1024], w_ref[b], nt_dims,
                                  preferred_element_type=F32)

        xf = x_ref[...]
        r = lax.rsqrt(jnp.mean(xf * xf, axis=-1, keepdims=True) + EPS)
        gh = dh * nw_ref[...]
        dnw_o[...] += jnp.sum(dh * xf * r, axis=0, keepdims=True)
        mean_gx = jnp.mean(gh * xf, axis=-1, keepdims=True)
        gx_o[...] = dout_ref[...] + r * gh - xf * (r * r * r) * mean_gx

    row = lambda w, j=0: pl.BlockSpec((tm, w), lambda i: (i, j))
    full = lambda shp: pl.BlockSpec(shp, lambda i: (0,) * len(shp))
    prev = pl.BlockSpec((hb, 512), lambda i: (jnp.maximum(i * (tm // hb) - 1, 0), 0))
    nxt = pl.BlockSpec((hb, 512), lambda i: (jnp.minimum((i + 1) * (tm // hb), s // hb - 1), 0))
    return pl.pallas_call(
        body, name="bwd_tail", grid=(nt,),
        out_shape=(jax.ShapeDtypeStruct((s, 1024), F32), jax.ShapeDtypeStruct((s, 4096), BF16),
                   jax.ShapeDtypeStruct((1, 1024), F32), jax.ShapeDtypeStruct((1, 1024), F32)),
        in_specs=[_class_spec(d, 512, tm) for d in DILATIONS]
        + [pl.BlockSpec((d, N_PAIR, 2, tm // d, 128), lambda i: (0, 0, 0, i, 0)) for d in DILATIONS]
        + [row(1024), full((1, 1024)), row(512), row(1024), row(512), prev, nxt,
           row(512, 0), row(512, 2), full((8, 512)), _resident((4, 1024, 1024)), row(1024),
           full((1, 1024)), row(1024), full((256, 256))],
        out_specs=(row(1024), row(4096), full((1, 1024)), full((1, 1024))),
        scratch_shapes=[pltpu.VMEM((4, tm, 128), F32), pltpu.VMEM((4, tm, 128), F32)],
        compiler_params=_params(("arbitrary",)),
    )(*dq_g, *dkv_g, qkr, qkw, dza, dgbz, dzc, dzc, dzc, cg, cg, cw, wblk, x, norm_w, dout, b256)


def _wgrad(a, b, row_blocked, name):
    s, m = a.shape
    n = b.shape[1]
    tk = 1024
    ncol = min(n, 2048)
    nj, nk = n // ncol, s // tk

    def body(a_ref, b_ref, o_ref, acc):
        kk = pl.program_id(1)

        @pl.when(kk == 0)
        def _():
            acc[...] = jnp.zeros_like(acc)

        acc[...] += lax.dot_general(a_ref[...], b_ref[...].astype(BF16), (((0,), (0,)), ((), ())),
                                    preferred_element_type=F32)

        @pl.when(kk == nk - 1)
        def _():
            blocks, _, rows, _ = o_ref.shape
            for blk in range(blocks):
                for half in range(2):
                    if row_blocked:
                        r0 = (2 * blk + half) * rows
                        o_ref[blk, half] = acc[r0:r0 + rows, :].astype(BF16)
                    else:
                        o_ref[blk, half] = acc[half * rows:(half + 1) * rows,
                                               1024 * blk:1024 * (blk + 1)].astype(BF16)

    if row_blocked:
        out_shape = jax.ShapeDtypeStruct((4, 2, m // 8, 1024), BF16)
        out_spec = pl.BlockSpec((4, 2, m // 8, 1024), lambda j, k: (0, 0, 0, 0))
    else:
        out_shape = jax.ShapeDtypeStruct((n // 1024, 2, m // 2, 1024), BF16)
        out_spec = pl.BlockSpec((ncol // 1024, 2, m // 2, 1024), lambda j, k: (j, 0, 0, 0))
    return pl.pallas_call(
        body, name=name, grid=(nj, nk),
        out_shape=out_shape,
        in_specs=[pl.BlockSpec((tk, m), lambda j, k: (k, 0)), pl.BlockSpec((tk, ncol), lambda j, k: (k, j))],
        out_specs=out_spec,
        scratch_shapes=[pltpu.VMEM((m, ncol), F32)],
        compiler_params=_params(("parallel", "arbitrary")),
    )(a, b)


def _dbias(dsum_all, onehot_all):
    def body(ds_ref, oh_ref, o_ref):
        step = pl.program_id(0) * 3 + pl.program_id(1)

        @pl.when(step == 0)
        def _():
            o_ref[...] = jnp.zeros_like(o_ref)

        rowq = lax.broadcasted_iota(jnp.int32, (2 * QB, KB), 0) & (QB - 1)
        hrow = lax.broadcasted_iota(jnp.int32, (8, KB), 0)
        diag = jnp.zeros((8, KB), F32)
        for p in range(N_PAIR):
            y = ds_ref[p]
            for bit in range(7):
                sh = 1 << bit
                y = jnp.where((rowq & sh) != 0, pltpu.roll(y, KB - sh, 1), y)
            da = jnp.sum(y[0:QB], axis=0, keepdims=True)
            db = jnp.sum(y[QB:2 * QB], axis=0, keepdims=True)
            diag = jnp.where(hrow == 2 * p, da, diag)
            diag = jnp.where(hrow == 2 * p + 1, db, diag)
        o_ref[...] += jnp.dot(diag, oh_ref[...], preferred_element_type=F32, precision=lax.Precision.HIGHEST)

    return pl.pallas_call(
        body, name="dbias", grid=(3, 3),
        out_shape=jax.ShapeDtypeStruct((8, 128), F32),
        in_specs=[pl.BlockSpec((None, N_PAIR, None, 2 * QB, KB), lambda g, v: (g, 0, v, 0, 0)),
                  pl.BlockSpec((None, None, KB, 128), lambda g, v: (g, v, 0, 0))],
        out_specs=pl.BlockSpec((8, 128), lambda g, v: (0, 0)),
        compiler_params=_params(("arbitrary", "arbitrary")),
    )(dsum_all, onehot_all)


def _gsync(pw_in, pw_out, small):
    hin, hout = pw_in.shape[2], pw_out.shape[2]
    nsmall = small.shape[0]

    def body(pin_hbm, pout_hbm, small_ref, gin_o, gout_o, small_o,
             mine_in, recv_in, sbuf_in, rbuf_in, mine_out, recv_out, sbuf_out, rbuf_out, gather,
             lsem, asend, arecv, bsend, brecv, csend, crecv, ssend, srecv):
        x, y, c = lax.axis_index("x"), lax.axis_index("y"), lax.axis_index("c")
        b = 2 * x + y
        dev = 4 * x + 2 * y + c
        sib = (x, y, 1 - c)

        def rcopy(src, dst, ssem, rsem, to):
            return pltpu.make_async_remote_copy(src_ref=src, dst_ref=dst, send_sem=ssem, recv_sem=rsem,
                                                device_id=to, device_id_type=MESH)

        gather[dev] = small_ref[...]
        s_sends = []
        for k in range(1, 8):
            to = (x ^ (k >> 2), y ^ ((k >> 1) & 1), c ^ (k & 1))
            cp = rcopy(gather.at[dev], gather.at[dev], ssend.at[k - 1], srecv.at[k - 1], to)
            cp.start()
            s_sends.append(cp)

        a_in = rcopy(pin_hbm.at[:, 1 - c], recv_in, asend.at[0], arecv.at[0], sib)
        a_out = rcopy(pout_hbm.at[:, 1 - c], recv_out, asend.at[1], arecv.at[1], sib)
        a_in.start()
        a_out.start()
        l_in = pltpu.make_async_copy(pin_hbm.at[:, c], mine_in, lsem.at[0])
        l_out = pltpu.make_async_copy(pout_hbm.at[:, c], mine_out, lsem.at[1])
        l_in.start()
        l_out.start()
        l_in.wait()
        l_out.wait()

        def stage_b(a_cp, mine, recv, sbuf, rbuf, base):
            a_cp.wait_recv()
            sends = []
            for k in (1, 2, 3):
                bk = b ^ k
                sbuf[k - 1] = (mine[bk].astype(F32) + recv[bk].astype(F32)).astype(BF16)
                cp = rcopy(sbuf.at[k - 1], rbuf.at[k - 1], bsend.at[base + k - 1], brecv.at[base + k - 1],
                           (x ^ (k >> 1), y ^ (k & 1), c))
                cp.start()
                sends.append(cp)
            return sends

        b_in = stage_b(a_in, mine_in, recv_in, sbuf_in, rbuf_in, 0)
        b_out = stage_b(a_out, mine_out, recv_out, sbuf_out, rbuf_out, 3)

        def stage_c(b_sends, mine, recv, rbuf, g_o, half, idx):
            acc = mine[b].astype(F32) + recv[b].astype(F32)
            for k in (1, 2, 3):
                b_sends[k - 1].wait_recv()
                acc = acc + rbuf[k - 1].astype(F32)
            rows = g_o.at[pl.ds(pl.multiple_of(c * half, half), half), :]
            g_o[pl.ds(pl.multiple_of(c * half, half), half), :] = acc
            cp = rcopy(rows, rows, csend.at[idx], crecv.at[idx], sib)
            cp.start()
            return cp

        c_in = stage_c(b_in, mine_in, recv_in, rbuf_in, gin_o, hin, 0)
        c_out = stage_c(b_out, mine_out, recv_out, rbuf_out, gout_o, hout, 1)

        for cp in s_sends:
            cp.wait_recv()
        tot = gather[0]
        for d in range(1, 8):
            tot = tot + gather[d]
        small_o[...] = tot

        for g_o, half, idx in ((gin_o, hin, 0), (gout_o, hout, 1)):
            other = g_o.at[pl.ds(pl.multiple_of((1 - c) * half, half), half), :]
            rcopy(other, other, csend.at[idx], crecv.at[idx], sib).wait_recv()
        for cp in s_sends + [a_in, a_out] + b_in + b_out + [c_in, c_out]:
            cp.wait_send()

    vm = pl.BlockSpec(memory_space=pltpu.VMEM)
    hbm = pl.BlockSpec(memory_space=pl.ANY)
    return pl.pallas_call(
        body, name="gsync",
        out_shape=(jax.ShapeDtypeStruct((2 * hin, 1024), F32), jax.ShapeDtypeStruct((2 * hout, 1024), F32),
                   jax.ShapeDtypeStruct((nsmall, 128), F32)),
        in_specs=[hbm, hbm, vm], out_specs=(vm, vm, vm),
        scratch_shapes=[pltpu.VMEM((4, hin, 1024), BF16), pltpu.VMEM((4, hin, 1024), BF16),
                        pltpu.VMEM((3, hin, 1024), BF16), pltpu.VMEM((3, hin, 1024), BF16),
                        pltpu.VMEM((4, hout, 1024), BF16), pltpu.VMEM((4, hout, 1024), BF16),
                        pltpu.VMEM((3, hout, 1024), BF16), pltpu.VMEM((3, hout, 1024), BF16),
                        pltpu.VMEM((8, nsmall, 128), F32),
                        pltpu.SemaphoreType.DMA((2,)),
                        pltpu.SemaphoreType.DMA((2,)), pltpu.SemaphoreType.DMA((2,)),
                        pltpu.SemaphoreType.DMA((6,)), pltpu.SemaphoreType.DMA((6,)),
                        pltpu.SemaphoreType.DMA((2,)), pltpu.SemaphoreType.DMA((2,)),
                        pltpu.SemaphoreType.DMA((7,)), pltpu.SemaphoreType.DMA((7,))],
        compiler_params=_params(),
    )(pw_in, pw_out, small)


def _adamw_math(w, g, m, v):
    m = ADAM_B1 * m + (1.0 - ADAM_B1) * g
    v = ADAM_B2 * v + (1.0 - ADAM_B2) * (g * g)
    m_hat = m / (1.0 - ADAM_B1 ** ADAM_STEP)
    v_hat = v / (1.0 - ADAM_B2 ** ADAM_STEP)
    delta = -ADAM_LR * (m_hat / (jnp.sqrt(v_hat) + ADAM_EPS) + ADAM_WD * w)
    return delta, m, v


def _adamw(w, g, m, v, name):
    rows, cols = w.shape
    tr = 256 if rows % 256 == 0 else rows

    def body(w_ref, g_ref, m_ref, v_ref, d_o, m_o, v_o):
        d, m2, v2 = _adamw_math(w_ref[...], g_ref[...], m_ref[...], v_ref[...])
        d_o[...] = d
        m_o[...] = m2
        v_o[...] = v2

    spec = pl.BlockSpec((tr, cols), lambda i: (i, 0))
    shp = jax.ShapeDtypeStruct((rows, cols), F32)
    return pl.pallas_call(
        body, name=name, grid=(rows // tr,), out_shape=(shp, shp, shp),
        in_specs=[spec] * 4, out_specs=(spec, spec, spec),
        compiler_params=_params(("parallel",)),
    )(w, g, m, v)


def _fold_heads(dqkw):
    def body(x_ref, o_ref):
        xs = x_ref[...]
        sq = xs[0:1] + xs[1:2] + xs[2:3] + xs[3:4]
        sk = xs[4:5] + xs[5:6] + xs[6:7] + xs[7:8]
        both = jnp.concatenate([sq, sk], axis=0)
        o_ref[...] = both + pltpu.roll(both, HEAD_DIM, 1)

    vm = pl.BlockSpec(memory_space=pltpu.VMEM)
    return pl.pallas_call(body, name="fold_heads", out_shape=jax.ShapeDtypeStruct((2, 128), F32),
                          in_specs=[vm], out_specs=vm, compiler_params=_params())(dqkw)


def kernel(x, norm_w, w_in, conv_w, conv_b, q_norm_w, k_norm_w, rel_bias, w_out, loss_target, m_norm_w, m_w_in, m_conv_w, m_conv_b, m_q_norm_w, m_k_norm_w, m_rel_bias, m_w_out, v_norm_w, v_w_in, v_conv_w, v_conv_b, v_q_norm_w, v_k_norm_w, v_rel_bias, v_w_out):
    x2 = x[0]
    tgt = loss_target[0]
    blk = 2 * lax.axis_index("x") + lax.axis_index("y")

    conv_w8 = jnp.pad(conv_w, ((0, 5), (0, 0)))
    wblk, woutblk, cwblk = _wgather(w_in, w_out, conv_w8)
    wout_full = woutblk.reshape(1024, 1024)
    cw_full = cwblk.transpose(1, 0, 2).reshape(8, 512)

    qkw = jnp.concatenate([jnp.tile(q_norm_w, 8) * 0.125, jnp.tile(k_norm_w, 8)])[None, :]
    qkw_raw = jnp.concatenate([jnp.tile(q_norm_w, 8), jnp.tile(k_norm_w, 8)])[None, :]
    gidx = jnp.arange(256) // HEAD_DIM
    b256 = (gidx[:, None] == gidx[None, :]).astype(BF16)

    h, cg, qkr, qkn, vz, qkn4, v4, qkn16, v16 = _proj(x2, norm_w[None, :], wblk, qkw, b256)

    biases = _bias_tables(rel_bias)
    qkn_l = [qkn[None], qkn4, qkn16]
    v_l = [vz[None], v4, v16]
    o_g, lse_g = [], []
    for gi, d in enumerate(DILATIONS):
        o_l, lse_l = _attn_fwd(qkn_l[gi], v_l[gi], biases[gi], f"attn_fwd_d{d}")
        o_g.append(o_l)
        lse_g.append(lse_l)

    (y, dout, ld1, do1, dza, dgbz, dzc, loss_p, dcb, dcw, do4, ld4, do16, ld16) = _combine(
        o_g, lse_g, cg, vz, x2, tgt, wout_full, cw_full, conv_b[None, :], b256)

    dq_g, dkv_g, dsums = [], [], []
    for gi, (d, do_l, ld_l) in enumerate(zip(DILATIONS, (do1, do4, do16), (ld1, ld4, ld16))):
        dq_l, dkv_l, dsum = _attn_bwd(qkn_l[gi], v_l[gi], do_l, ld_l, biases[gi], f"attn_bwd_d{d}")
        dq_g.append(dq_l)
        dkv_g.append(dkv_l)
        dsums.append(dsum)

    grad_x, dproj, dnw, dqkw = _bwd_tail(dq_g, dkv_g, qkr, qkw_raw, dza, dgbz, dzc, cg, cw_full, wblk,
                                         x2, norm_w[None, :], dout, b256)

    pw_in = _wgrad(h, dproj, False, "wgrad_in")
    pw_out = _wgrad(y, dout, True, "wgrad_out")
    dbias8 = _dbias(jnp.stack(dsums, axis=0), jnp.stack([_diag_bucket_onehot(d) for d in DILATIONS], axis=0))

    small = jnp.concatenate([dnw.reshape(8, 128), dcb.reshape(4, 128), dqkw.reshape(8, 128),
                             dcw[0:3].reshape(12, 128), dbias8, jnp.pad(loss_p, ((0, 7), (0, 0)))], axis=0)
    g_win, g_wout, gsmall = _gsync(pw_in, pw_out, small)

    g_nw = gsmall[0:8].reshape(1024)
    g_cb = gsmall[8:12].reshape(512)
    folded = _fold_heads(gsmall[12:20])
    g_qw, g_kw = folded[0, 0:64], folded[1, 0:64]
    g_cw = lax.dynamic_slice(gsmall[20:32].reshape(3, 512), (0, blk * 128), (3, 128))
    g_rb = gsmall[32:40][:, 0:32].T
    loss = gsmall[40, 0]

    d_win, nm_win, nv_win = _adamw(w_in, g_win, m_w_in, v_w_in, "adamw_w_in")
    d_wout, nm_wout, nv_wout = _adamw(w_out, g_wout, m_w_out, v_w_out, "adamw_w_out")

    def pack(parts):
        rows = [parts[0].reshape(8, 128), parts[1].reshape(4, 128),
                jnp.pad(parts[2], (0, 64))[None, :], jnp.pad(parts[3], (0, 64))[None, :],
                parts[4], jnp.pad(parts[5].T, ((0, 0), (0, 96)))]
        return jnp.concatenate(rows, axis=0)

    ws = pack([norm_w, conv_b, q_norm_w, k_norm_w, conv_w, rel_bias])
    gs = pack([g_nw, g_cb, g_qw, g_kw, g_cw, g_rb])
    ms = pack([m_norm_w, m_conv_b, m_q_norm_w, m_k_norm_w, m_conv_w, m_rel_bias])
    vs = pack([v_norm_w, v_conv_b, v_q_norm_w, v_k_norm_w, v_conv_w, v_rel_bias])
    rpad = lambda a: jnp.pad(a, ((0, 7), (0, 0)))
    d_s, nm_s, nv_s = _adamw(rpad(ws), rpad(gs), rpad(ms), rpad(vs), "adamw_small")

    def unpack(a):
        return (a[0:8].reshape(1024), a[12:13, 0:64].reshape(64), a[13:14, 0:64].reshape(64),
                a[14:17], a[8:12].reshape(512), a[17:25, 0:32].T)

    def ordered(nw, win, cw, cb, qw, kw, rb, wout):
        return (nw, win, cw, cb, qw, kw, rb, wout)

    g_un = (g_nw, g_qw, g_kw, g_cw, g_cb, g_rb)
    outs = [loss, grad_x[None]]
    for un, win_v, wout_v in ((g_un, g_win, g_wout), (unpack(d_s), d_win, d_wout),
                              (unpack(nm_s), nm_win, nm_wout), (unpack(nv_s), nv_win, nv_wout)):
        nw, qw, kw, cw, cb, rb = un
        outs.extend(ordered(nw, win_v, cw, cb, qw, kw, rb, wout_v))
    return tuple(outs)
```

```python
import math

import jax
import jax.numpy as jnp
from jax import lax
from jax.experimental import pallas as pl
from jax.experimental.pallas import tpu as pltpu

F32 = jnp.float32
BF16 = jnp.bfloat16
MESH = pl.DeviceIdType.MESH

D_MODEL = 1024
CONV_W = 512
ATTN_W = 512
HEAD_DIM = 64
N_PAIR = 4
DILATIONS = (1, 4, 16)
HALF = 64
QB = 128
KB = QB + 2 * HALF
NUM_BUCKETS = 32
MAX_DISTANCE = 1024
EPS = 1e-6
NEG = -1e30
ADAM_LR, ADAM_B1, ADAM_B2, ADAM_EPS, ADAM_WD, ADAM_STEP = 0.001, 0.9, 0.999, 1e-08, 0.01, 10
VMEM_LIMIT = 48 << 20


def _params(sem=None, vmem=VMEM_LIMIT, **kw):
    if sem is not None:
        kw["dimension_semantics"] = sem
    return pltpu.CompilerParams(vmem_limit_bytes=vmem, **kw)


def _sigmoid(z):
    return 1.0 / (1.0 + jnp.exp(-z))


def _group_sum(val, b_ref, split=True):
    hi = val.astype(BF16)
    lo = (val - hi.astype(F32)).astype(BF16) if split else None
    outs = []
    for j in range(val.shape[1] // 256):
        sl = slice(256 * j, 256 * j + 256)
        part = jnp.dot(hi[:, sl], b_ref[...], preferred_element_type=F32)
        if split:
            part = part + jnp.dot(lo[:, sl], b_ref[...], preferred_element_type=F32)
        outs.append(part)
    return outs[0] if len(outs) == 1 else jnp.concatenate(outs, axis=1)


def _t5_bucket(rel):
    half_b = NUM_BUCKETS // 2
    max_exact = half_b // 2
    ret = jnp.where(rel > 0, half_b, 0)
    n = jnp.abs(rel)
    nf = jnp.maximum(n, 1).astype(F32)
    large = max_exact + (jnp.log(nf / max_exact) / math.log(MAX_DISTANCE / max_exact)
                         * (half_b - max_exact)).astype(jnp.int32)
    large = jnp.minimum(large, half_b - 1)
    return ret + jnp.where(n < max_exact, n, large)


def _bias_tables(rel_bias):
    rows = []
    key = jnp.arange(KB)
    for dilation in DILATIONS:
        for variant in range(3):
            off = (0, HALF, 2 * HALF)[variant]
            rel = ((key - off + KB // 2) % KB) - KB // 2
            bkt = _t5_bucket(jnp.clip(rel, -HALF, HALF) * dilation)
            rows.append(jnp.where(jnp.abs(rel) <= HALF, bkt, -1))
    bkt_all = jnp.broadcast_to(jnp.stack(rows, axis=0).astype(jnp.int32)[:, None, :], (9, 8, KB))

    def body(rb_ref, bkt_ref, o_ref):
        bkt = bkt_ref[...]
        off = (pl.program_id(0) % 3) * HALF
        rel = (lax.broadcasted_iota(jnp.int32, (QB, KB), 1) - lax.broadcasted_iota(jnp.int32, (QB, KB), 0)) - off
        band = jnp.abs(rel) <= HALF
        for h in range(8):
            acc = jnp.full((8, KB), NEG, F32)
            for b in range(NUM_BUCKETS):
                acc = jnp.where(bkt == b, rb_ref[b, h], acc)
            rolled = pltpu.roll(jnp.broadcast_to(acc[0:1], (QB, KB)), 0, 1, stride=1, stride_axis=0)
            o_ref[h] = jnp.where(band, rolled, NEG)

    out = pl.pallas_call(
        body, name="bias_tables", grid=(9,),
        out_shape=jax.ShapeDtypeStruct((9, 8, QB, KB), F32),
        in_specs=[pl.BlockSpec(memory_space=pltpu.SMEM), pl.BlockSpec((None, 8, KB), lambda i: (i, 0, 0))],
        out_specs=pl.BlockSpec((None, 8, QB, KB), lambda i: (i, 0, 0, 0)),
        compiler_params=_params(("parallel",)),
    )(rel_bias, bkt_all)
    return out.reshape(3, 3, N_PAIR, 2 * QB, KB)


def _diag_bucket_onehot(dilation):
    out = []
    c = jnp.arange(KB)
    for variant in range(3):
        off = (0, HALF, 2 * HALF)[variant]
        rel = ((c - off + 128) % 256) - 128
        band = jnp.abs(rel) <= HALF
        bkt = _t5_bucket(jnp.clip(rel, -HALF, HALF) * dilation)
        oh = (bkt[:, None] == jnp.arange(128)[None, :]) & band[:, None]
        out.append(oh.astype(F32))
    return jnp.stack(out, axis=0)


def _wgather(w_in, w_out, conv_w):
    rin, rout = w_in.shape[0] // 2, w_out.shape[0] // 2

    def body(win_ref, wout_ref, cw_ref, win_o, wout_o, cw_o, send_sems, recv_sems):
        x, y, c = lax.axis_index("x"), lax.axis_index("y"), lax.axis_index("c")
        b = 2 * x + y
        win_o[b] = win_ref[...].astype(BF16)
        wout_o[b] = wout_ref[...].astype(BF16)
        cw_o[b] = cw_ref[...]

        def peer(k):
            return (x ^ (k >> 1), y ^ (k & 1))

        def piece(ref, blk, half_rows, core):
            return ref.at[blk, pl.ds(core * half_rows, half_rows), :]

        def copy(sem, src, dst, to):
            return pltpu.make_async_remote_copy(src_ref=src, dst_ref=dst, send_sem=send_sems.at[sem],
                                                recv_sem=recv_sems.at[sem], device_id=to, device_id_type=MESH)

        sends = []
        for k in (1, 2, 3):
            px, py = peer(k)
            sends.append(copy(k - 1, piece(win_o, b, rin, c), piece(win_o, b, rin, c), (px, py, c)))
            sends.append(copy(3 + k - 1, piece(wout_o, b, rout, c), piece(wout_o, b, rout, c), (px, py, c)))
            sends.append(copy(6 + k - 1, cw_o.at[b], cw_o.at[b], (px, py, c)))
        for cp in sends:
            cp.start()
        fwd = []
        for k in (1, 2, 3):
            px, py = peer(k)
            bk = 2 * px + py
            copy(k - 1, piece(win_o, bk, rin, c), piece(win_o, bk, rin, c), (px, py, c)).wait_recv()
            f = copy(9 + k - 1, piece(win_o, bk, rin, c), piece(win_o, bk, rin, c), (x, y, 1 - c))
            f.start()
            fwd.append(f)
            copy(3 + k - 1, piece(wout_o, bk, rout, c), piece(wout_o, bk, rout, c), (px, py, c)).wait_recv()
            f = copy(12 + k - 1, piece(wout_o, bk, rout, c), piece(wout_o, bk, rout, c), (x, y, 1 - c))
            f.start()
            fwd.append(f)
            copy(6 + k - 1, cw_o.at[bk], cw_o.at[bk], (px, py, c)).wait_recv()
        for k in (1, 2, 3):
            px, py = peer(k)
            bk = 2 * px + py
            copy(9 + k - 1, piece(win_o, bk, rin, 1 - c), piece(win_o, bk, rin, 1 - c), (x, y, 1 - c)).wait_recv()
            copy(12 + k - 1, piece(wout_o, bk, rout, 1 - c), piece(wout_o, bk, rout, 1 - c), (x, y, 1 - c)).wait_recv()
        for cp in sends + fwd:
            cp.wait_send()

    vm = pl.BlockSpec(memory_space=pltpu.VMEM)
    return pl.pallas_call(
        body, name="wgather",
        out_shape=(jax.ShapeDtypeStruct((4,) + w_in.shape, BF16),
                   jax.ShapeDtypeStruct((4,) + w_out.shape, BF16),
                   jax.ShapeDtypeStruct((4,) + conv_w.shape, F32)),
        in_specs=[vm, vm, vm], out_specs=(vm, vm, vm),
        scratch_shapes=[pltpu.SemaphoreType.DMA((15,)), pltpu.SemaphoreType.DMA((15,))],
        compiler_params=_params(),
    )(w_in, w_out, conv_w)


TM_MATMUL = 512
TM_COMBINE = 256


def _resident(shape):
    return pl.BlockSpec(shape, lambda i: (0,) * len(shape), pipeline_mode=pl.Buffered(1))


def _to_slabs(slab, val, j0=0):
    for j in range(val.shape[1] // 128):
        slab[j0 + j] = val[:, 128 * j:128 * (j + 1)]


def _scatter_classes(slab, j0, nj, out_ref, d, part=0, mid=None):
    tm = slab.shape[1]
    n = tm // d
    if d == 4:
        for r in range(d):
            for j in range(nj):
                out_ref[r, part * n:(part + 1) * n, 128 * j:128 * (j + 1)] = (
                    slab[j0 + j, pl.ds(r, n, stride=d), :].astype(out_ref.dtype))
        return
    q = tm // 4
    for lo in range(4):
        for j in range(nj):
            mid[j0 + j, lo * q:(lo + 1) * q, :] = slab[j0 + j, pl.ds(lo, q, stride=4), :]
    for hi in range(4):
        for lo in range(4):
            for j in range(nj):
                out_ref[4 * hi + lo, part * n:(part + 1) * n, 128 * j:128 * (j + 1)] = (
                    mid[j0 + j, pl.ds(lo * q + hi, n, stride=4), :].astype(out_ref.dtype))


def _gather_classes(slab, piece, nj, d, mid=None):
    tm = slab.shape[1]
    n = tm // d
    if d == 4:
        for r in range(d):
            for j in range(nj):
                slab[j, pl.ds(r, n, stride=d), :] = piece(r, j).astype(F32)
    else:
        q = tm // 4
        for hi in range(4):
            for lo in range(4):
                for j in range(nj):
                    mid[j, pl.ds(lo * q + hi, n, stride=4), :] = piece(4 * hi + lo, j).astype(F32)
        for lo in range(4):
            for j in range(nj):
                slab[j, pl.ds(lo, q, stride=4), :] = mid[j, lo * q:(lo + 1) * q, :]
    return jnp.concatenate([slab[j] for j in range(nj)], axis=1)


def _class_spec(d, width, tm):
    return pl.BlockSpec((d, tm // d, width), lambda i: (0, i, 0))


def _proj(x, norm_w, wblk, qkw, b256):
    s = x.shape[0]
    tm = TM_MATMUL
    nparts = 2
    tp = tm // nparts

    def body(x_ref, nw_ref, w_ref, qkw_ref, b_ref, h_o, cg_o, qkr_o, qkn_o, vz_o, qkn4_o, v4_o, qkn16_o, v16_o,
             slabs, mids):
        for part in range(nparts):
            rows = slice(part * tp, (part + 1) * tp)
            slab = slabs.at[part]
            xf = x_ref[rows, :]
            r = lax.rsqrt(jnp.mean(xf * xf, axis=-1, keepdims=True) + EPS)
            h = (xf * r * nw_ref[...]).astype(BF16)
            h_o[rows, :] = h
            p2 = jnp.dot(h, w_ref[2], preferred_element_type=F32)
            qkr_o[rows, :] = p2.astype(BF16)
            ss = _group_sum(p2 * p2, b_ref, split=False)
            rr = lax.rsqrt(ss * (1.0 / HEAD_DIM) + EPS)
            qkn = p2 * rr * qkw_ref[...]
            qkn_o[rows, :] = qkn.astype(BF16)
            _to_slabs(slab, qkn)
            p3 = jnp.dot(h, w_ref[3], preferred_element_type=F32)
            vz_o[rows, :] = p3.astype(BF16)
            _to_slabs(slab, p3[:, 0:512], 8)
            cg_o[rows, 0:1024] = jnp.dot(h, w_ref[0], preferred_element_type=F32).astype(BF16)
            cg_o[rows, 1024:2048] = jnp.dot(h, w_ref[1], preferred_element_type=F32).astype(BF16)
            for d, q_o, v_o in ((4, qkn4_o, v4_o), (16, qkn16_o, v16_o)):
                _scatter_classes(slab, 0, 8, q_o, d, part, mids.at[part])
                _scatter_classes(slab, 8, 4, v_o, d, part, mids.at[part])

    row = lambda w: pl.BlockSpec((tm, w), lambda i: (i, 0))
    full = lambda shp: pl.BlockSpec(shp, lambda i: (0,) * len(shp))
    nat = lambda w: jax.ShapeDtypeStruct((s, w), BF16)
    cls = lambda d, w: jax.ShapeDtypeStruct((d, s // d, w), BF16)
    return pl.pallas_call(
        body, name="proj", grid=(s // tm,),
        out_shape=(nat(1024), nat(2048), nat(1024), nat(1024), nat(1024),
                   cls(4, 1024), cls(4, 512), cls(16, 1024), cls(16, 512)),
        in_specs=[row(1024), full((1, 1024)), _resident((4, 1024, 1024)), full((1, 1024)), full((256, 256))],
        out_specs=(row(1024), row(2048), row(1024), row(1024), row(1024),
                   _class_spec(4, 1024, tm), _class_spec(4, 512, tm),
                   _class_spec(16, 1024, tm), _class_spec(16, 512, tm)),
        scratch_shapes=[pltpu.VMEM((nparts, 12, tp, 128), F32), pltpu.VMEM((nparts, 12, tp, 128), F32)],
        compiler_params=_params(("parallel",)),
    )(x, norm_w, wblk, qkw, b256)


def _block_coords(t, i, nsub, nb, length):
    n = t * nsub + i
    q0 = i * QB
    start = pl.multiple_of(jnp.clip(n * QB - HALF, 0, length - KB), HALF)
    variant = jnp.where(n == 0, 0, jnp.where(n == nb - 1, 2, 1))
    return q0, start, variant


def _split_heads(a, lo):
    zero = jnp.zeros_like(a)
    return jnp.concatenate([jnp.where(lo, a, zero), jnp.where(lo, zero, a)], axis=0)


def _col_pair(ref, q0, lane):
    return jnp.concatenate([ref[pl.ds(q0, QB), lane:lane + 1],
                            ref[pl.ds(q0, QB), HEAD_DIM + lane:HEAD_DIM + lane + 1]], axis=0)


def _attn_fwd(qkn_l, v_l, bias, gi, name):
    r_cls, length, _ = qkn_l.shape
    qt = min(length, 2048)
    nb, nsub = length // QB, qt // QB

    def body(q_ref, k_ref, v_ref, b_ref, o_ref, lse_ref):
        t = pl.program_id(2)
        lo = lax.broadcasted_iota(jnp.int32, (QB, 128), 1) < HEAD_DIM

        starts, logits = [], []
        for i in range(nsub):
            _, start, variant = _block_coords(t, i, nsub, nb, length)
            qq = _split_heads(q_ref[i * QB:(i + 1) * QB, :], lo)
            k = k_ref[pl.ds(start, KB), :]
            logits.append(lax.dot_general(qq, k, (((1,), (1,)), ((), ())), preferred_element_type=F32)
                          + b_ref[variant])
            starts.append(start)
        lg = jnp.concatenate(logits, axis=0)
        m = jnp.max(lg, axis=-1, keepdims=True)
        p = jnp.exp(lg - m)
        pb = p.astype(BF16)
        l = jnp.sum(p, axis=-1, keepdims=True)
        lse = jnp.broadcast_to(m + jnp.log(l), (nsub * 2 * QB, 128))
        inv = 1.0 / l
        for i in range(nsub):
            rows = slice(2 * QB * i, 2 * QB * (i + 1))
            v = v_ref[pl.ds(starts[i], KB), :]
            pv = jnp.dot(pb[rows], v, preferred_element_type=F32) * inv[rows]
            o_ref[i * QB:(i + 1) * QB, :] = jnp.where(lo, pv[0:QB], pv[QB:2 * QB]).astype(BF16)
            ls = lse[rows]
            lse_ref[i * QB:(i + 1) * QB, :] = jnp.where(lo, ls[0:QB], ls[QB:2 * QB])

    return pl.pallas_call(
        body, name=name, grid=(r_cls, N_PAIR, length // qt),
        out_shape=(jax.ShapeDtypeStruct((r_cls, length, 512), BF16),
                   jax.ShapeDtypeStruct((r_cls, length, 512), F32)),
        in_specs=[pl.BlockSpec((None, qt, 128), lambda r, p, t: (r, t, p)),
                  pl.BlockSpec((None, length, 128), lambda r, p, t: (r, 0, 4 + p)),
                  pl.BlockSpec((None, length, 128), lambda r, p, t: (r, 0, p)),
                  pl.BlockSpec((None, 3, None, 2 * QB, KB), lambda r, p, t: (gi, 0, p, 0, 0))],
        out_specs=(pl.BlockSpec((None, qt, 128), lambda r, p, t: (r, t, p)),
                   pl.BlockSpec((None, qt, 128), lambda r, p, t: (r, t, p))),
        compiler_params=_params(("parallel", "parallel", "arbitrary")),
    )(qkn_l, qkn_l, v_l, bias)


def _combine(o_g, lse_g, cg, vz, x, tgt, wout, cw, cb, b256):
    s = x.shape[0]
    tm = TM_COMBINE
    hb = 16
    nt = s // tm

    def body(o1, o4, o16, l1, l4, l16, cg_ref, cgp_ref, cgn_ref, za_ref, x_ref, t_ref, w_ref, cw_ref, cb_ref,
             b_ref, y_o, dout_o, ld1_o, do1_o, dza_o, dgbz_o, dzc_o, loss_o, dcb_o, dcw_o,
             do4_o, ld4_o, do16_o, ld16_o, slab, mid):
        i = pl.program_id(0)

        @pl.when(i == 0)
        def _():
            loss_o[...] = jnp.zeros_like(loss_o)
            dcb_o[...] = jnp.zeros_like(dcb_o)
            dcw_o[...] = jnp.zeros_like(dcw_o)

        u = cg_ref[:, 0:512].astype(F32)
        gb = cg_ref[:, 512:1024].astype(F32)
        gc = cg_ref[:, 1024:1536].astype(F32)
        zc = cg_ref[:, 1536:2048].astype(F32)
        tt = gc * u
        t_prev = cgp_ref[hb - 1:hb, 0:512].astype(F32) * cgp_ref[hb - 1:hb, 1024:1536].astype(F32)
        t_next = cgn_ref[0:1, 0:512].astype(F32) * cgn_ref[0:1, 1024:1536].astype(F32)
        t_prev = jnp.where(i == 0, 0.0, t_prev)
        t_next = jnp.where(i == nt - 1, 0.0, t_next)
        rows = lax.broadcasted_iota(jnp.int32, (tm, 512), 0)
        t_up = jnp.where(rows == 0, t_prev, pltpu.roll(tt, 1, 0))
        t_dn = jnp.where(rows == tm - 1, t_next, pltpu.roll(tt, tm - 1, 0))
        w0, w1, w2 = cw_ref[0:1, :], cw_ref[1:2, :], cw_ref[2:3, :]
        zb = w0 * t_up + w1 * tt + w2 * t_dn + cb_ref[...]
        sg = _sigmoid(zc)
        sz = zc * sg
        y_conv = gb * zb * sz

        a1, p1 = l1[0], o1[0].astype(F32)
        a4 = _gather_classes(slab, lambda r, j: l4[r, :, 128 * j:128 * (j + 1)], 4, 4)
        p4 = _gather_classes(slab, lambda r, j: o4[r, :, 128 * j:128 * (j + 1)], 4, 4)
        a16 = _gather_classes(slab, lambda r, j: l16[r, :, 128 * j:128 * (j + 1)], 4, 16, mid)
        p16 = _gather_classes(slab, lambda r, j: o16[r, :, 128 * j:128 * (j + 1)], 4, 16, mid)
        m = jnp.maximum(jnp.maximum(a1, a4), a16)
        e1, e4, e16 = jnp.exp(a1 - m), jnp.exp(a4 - m), jnp.exp(a16 - m)
        den = e1 + e4 + e16
        lse = m + jnp.log(den)
        o = (e1 * p1 + e4 * p4 + e16 * p16) / den
        za = za_ref[...].astype(F32)
        sga = _sigmoid(za)
        sa = za * sga
        y = jnp.concatenate([y_conv, o * sa], axis=1).astype(BF16)
        y_o[...] = y

        out = x_ref[...] + jnp.dot(y, w_ref[...], preferred_element_type=F32)
        diff = out - t_ref[...]
        loss_o[...] += (0.5 / D_MODEL) * jnp.sum(diff * diff)
        dout = diff * (1.0 / D_MODEL)
        dout_o[...] = dout
        dy = lax.dot_general(dout.astype(BF16), w_ref[...], (((1,), (1,)), ((), ())), preferred_element_type=F32)
        dyc, dya = dy[:, 0:512], dy[:, 512:1024]

        do = dya * sa
        dza_o[...] = (dya * o * (sga * (1.0 + za * (1.0 - sga)))).astype(BF16)
        lane = lax.broadcasted_iota(jnp.int32, (tm, 512), 1)
        ld = jnp.where((lane & (HEAD_DIM - 1)) < HEAD_DIM // 2, lse, _group_sum(do * o, b_ref))
        do1_o[0] = do.astype(BF16)
        ld1_o[0] = ld
        _to_slabs(slab, do)
        _scatter_classes(slab, 0, 4, do4_o, 4)
        _scatter_classes(slab, 0, 4, do16_o, 16, 0, mid)
        _to_slabs(slab, ld)
        _scatter_classes(slab, 0, 4, ld4_o, 4)
        _scatter_classes(slab, 0, 4, ld16_o, 16, 0, mid)

        dzc = dyc * sz * gb
        dzc_o[...] = dzc.astype(BF16)
        dgbz_o[:, 0:512] = (dyc * sz * zb).astype(BF16)
        dgbz_o[:, 512:1024] = (dyc * gb * zb * (sg * (1.0 + zc * (1.0 - sg)))).astype(BF16)
        dcb_o[...] += jnp.sum(dzc, axis=0, keepdims=True)
        dcw_o[0:1, :] += jnp.sum(dzc * t_up, axis=0, keepdims=True)
        dcw_o[1:2, :] += jnp.sum(dzc * tt, axis=0, keepdims=True)
        dcw_o[2:3, :] += jnp.sum(dzc * t_dn, axis=0, keepdims=True)

    row = lambda w, j=0: pl.BlockSpec((tm, w), lambda i: (i, j))
    full = lambda shp: pl.BlockSpec(shp, lambda i: (0,) * len(shp))
    prev = pl.BlockSpec((hb, 2048), lambda i: (jnp.maximum(i * (tm // hb) - 1, 0), 0))
    nxt = pl.BlockSpec((hb, 2048), lambda i: (jnp.minimum((i + 1) * (tm // hb), s // hb - 1), 0))
    cls = lambda d, dt: jax.ShapeDtypeStruct((d, s // d, 512), dt)
    cspecs = [_class_spec(d, 512, tm) for d in DILATIONS]
    return pl.pallas_call(
        body, name="combine", grid=(nt,),
        out_shape=(jax.ShapeDtypeStruct((s, 1024), BF16), jax.ShapeDtypeStruct((s, 1024), F32),
                   cls(1, F32), cls(1, BF16), jax.ShapeDtypeStruct((s, 512), BF16),
                   jax.ShapeDtypeStruct((s, 1024), BF16), jax.ShapeDtypeStruct((s, 512), BF16),
                   jax.ShapeDtypeStruct((1, 128), F32), jax.ShapeDtypeStruct((1, 512), F32),
                   jax.ShapeDtypeStruct((8, 512), F32),
                   cls(4, BF16), cls(4, F32), cls(16, BF16), cls(16, F32)),
        in_specs=cspecs + cspecs + [row(2048), prev, nxt, row(512, 1), row(1024), row(1024),
                                    _resident((1024, 1024)), full((8, 512)), full((1, 512)), full((256, 256))],
        out_specs=(row(1024), row(1024), cspecs[0], cspecs[0], row(512), row(1024), row(512),
                   full((1, 128)), full((1, 512)), full((8, 512)),
                   cspecs[1], cspecs[1], cspecs[2], cspecs[2]),
        scratch_shapes=[pltpu.VMEM((4, tm, 128), F32), pltpu.VMEM((4, tm, 128), F32)],
        compiler_params=_params(("arbitrary",)),
    )(*o_g, *lse_g, cg, cg, cg, vz, x, tgt, wout, cw, cb, b256)


def _attn_bwd(qkn_l, v_l, do_l, ld_l, bias, gi, name):
    r_cls, length, _ = qkn_l.shape
    qt = min(length, 2048 if length <= 4096 else 1024)
    nb, nsub, nt = length // QB, qt // QB, length // qt
    chunk = min(length, 4096)
    nchunk = length // chunk

    def body(q_ref, k_ref, v_ref, do_ref, ld_ref, b_ref, dq_ref, dkv_hbm, dsum_ref, dk_acc, dv_acc, stage, sems):
        p_id, r, t = pl.program_id(0), pl.program_id(1), pl.program_id(2)
        lo = lax.broadcasted_iota(jnp.int32, (QB, 128), 1) < HEAD_DIM

        @pl.when(t == 0)
        def _():
            dk_acc[...] = jnp.zeros_like(dk_acc)
            dv_acc[...] = jnp.zeros_like(dv_acc)

        @pl.when((t == 0) & (r == 0))
        def _():
            dsum_ref[...] = jnp.zeros_like(dsum_ref)

        nt_dims = (((1,), (1,)), ((), ()))
        tn_dims = (((0,), (0,)), ((), ()))
        coords, qqs, dds, logits, dps, lcols, dcols = [], [], [], [], [], [], []
        for i in range(nsub):
            q0, start, variant = _block_coords(t, i, nsub, nb, length)
            qq = _split_heads(q_ref[q0:q0 + QB, :], lo)
            dd = _split_heads(do_ref[q0:q0 + QB, :], lo)
            k = k_ref[pl.ds(start, KB), :]
            v = v_ref[pl.ds(start, KB), :]
            logits.append(lax.dot_general(qq, k, nt_dims, preferred_element_type=F32) + b_ref[variant])
            dps.append(lax.dot_general(dd, v, nt_dims, preferred_element_type=F32))
            lcols.append(_col_pair(ld_ref, q0, 0))
            dcols.append(_col_pair(ld_ref, q0, HEAD_DIM // 2))
            coords.append((q0, start, variant))
            qqs.append(qq)
            dds.append(dd)
        p = jnp.exp(jnp.concatenate(logits, axis=0) - jnp.concatenate(lcols, axis=0))
        ds = p * (jnp.concatenate(dps, axis=0) - jnp.concatenate(dcols, axis=0))
        pb = p.astype(BF16)
        dsb = ds.astype(BF16)
        middle = None
        for i in range(nsub):
            q0, start, variant = coords[i]
            rows = slice(2 * QB * i, 2 * QB * (i + 1))
            if 0 < i < nsub - 1:
                middle = ds[rows] if middle is None else middle + ds[rows]
            else:
                dsum_ref[variant] += ds[rows]
            dqq = jnp.dot(dsb[rows], k_ref[pl.ds(start, KB), :], preferred_element_type=F32)
            dq_ref[q0:q0 + QB, :] = jnp.where(lo, dqq[0:QB], dqq[QB:2 * QB]).astype(BF16)
            dk_acc[pl.ds(start, KB), :] += lax.dot_general(dsb[rows], qqs[i], tn_dims, preferred_element_type=F32)
            dv_acc[pl.ds(start, KB), :] += lax.dot_general(pb[rows], dds[i], tn_dims, preferred_element_type=F32)
        if middle is not None:
            dsum_ref[1] += middle

        @pl.when(t == nt - 1)
        def _():
            def copy(k):
                which, c = k // nchunk, k % nchunk
                rows = pl.ds(c * chunk, chunk)
                return pltpu.make_async_copy(stage.at[k % 2], dkv_hbm.at[r, p_id, which, rows, :], sems.at[k % 2])

            for k in range(2 * nchunk):
                if k < 2:
                    @pl.when((p_id > 0) | (r > 0))
                    def _():
                        copy(k).wait()
                else:
                    copy(k).wait()
                acc = (dk_acc, dv_acc)[k // nchunk]
                stage[k % 2] = acc[pl.ds((k % nchunk) * chunk, chunk), :].astype(BF16)
                copy(k).start()

            @pl.when((p_id == N_PAIR - 1) & (r == r_cls - 1))
            def _():
                copy(0).wait()
                copy(1).wait()

    qspec = pl.BlockSpec((None, qt, 128), lambda p, r, t: (r, t, p))
    return pl.pallas_call(
        body, name=name, grid=(N_PAIR, r_cls, nt),
        out_shape=(jax.ShapeDtypeStruct((r_cls, length, 512), BF16),
                   jax.ShapeDtypeStruct((r_cls, N_PAIR, 2, length, 128), BF16),
                   jax.ShapeDtypeStruct((N_PAIR, 3, 2 * QB, KB), F32)),
        in_specs=[qspec,
                  pl.BlockSpec((None, length, 128), lambda p, r, t: (r, 0, 4 + p)),
                  pl.BlockSpec((None, length, 128), lambda p, r, t: (r, 0, p)),
                  qspec, qspec,
                  pl.BlockSpec((None, 3, None, 2 * QB, KB), lambda p, r, t: (gi, 0, p, 0, 0))],
        out_specs=(qspec, pl.BlockSpec(memory_space=pl.ANY),
                   pl.BlockSpec((None, 3, 2 * QB, KB), lambda p, r, t: (p, 0, 0, 0))),
        scratch_shapes=[pltpu.VMEM((length, 128), F32), pltpu.VMEM((length, 128), F32),
                        pltpu.VMEM((2, chunk, 128), BF16), pltpu.SemaphoreType.DMA((2,))],
        compiler_params=_params(("arbitrary", "arbitrary", "arbitrary")),
    )(qkn_l, qkn_l, v_l, do_l, ld_l, bias)


def _bwd_tail(dq_g, dkv_g, qkr, qkw, dza, dgbz, dzc, cg, cw, wblk, x, norm_w, dout, b256):
    s = x.shape[0]
    tm = TM_COMBINE
    hb = 16
    nt = s // tm

    def body(dq1, dq4, dq16, dkv1, dkv4, dkv16, qkr_ref, qkw_ref, dza_ref, dgbz_ref, dzc_ref,
             dzp_ref, dzn_ref, u_ref, gc_ref, cw_ref, w_ref, x_ref, nw_ref, dout_ref, b_ref,
             gx_o, dproj_o, dnw_o, dqkw_o, slab, mid):
        i = pl.program_id(0)

        def nat_q(ref, d):
            return _gather_classes(slab, lambda r, j: ref[r, :, 128 * j:128 * (j + 1)], 4, d, mid)

        def nat_kv(ref, d, which):
            return _gather_classes(slab, lambda r, j: ref[r, j, which], 4, d, mid)

        @pl.when(i == 0)
        def _():
            dnw_o[...] = jnp.zeros_like(dnw_o)
            dqkw_o[...] = jnp.zeros_like(dqkw_o)

        dzc = dzc_ref[...].astype(F32)
        d_prev = jnp.where(i == 0, 0.0, dzp_ref[hb - 1:hb, :].astype(F32))
        d_next = jnp.where(i == nt - 1, 0.0, dzn_ref[0:1, :].astype(F32))
        rows = lax.broadcasted_iota(jnp.int32, (tm, 512), 0)
        d_up = jnp.where(rows == 0, d_prev, pltpu.roll(dzc, 1, 0))
        d_dn = jnp.where(rows == tm - 1, d_next, pltpu.roll(dzc, tm - 1, 0))
        dt = cw_ref[0:1, :] * d_dn + cw_ref[1:2, :] * dzc + cw_ref[2:3, :] * d_up
        u = u_ref[...].astype(F32)
        gc = gc_ref[...].astype(F32)
        dproj_o[:, 0:512] = (dt * gc).astype(BF16)
        dproj_o[:, 512:1024] = dgbz_ref[:, 0:512]
        dproj_o[:, 1024:1536] = (dt * u).astype(BF16)
        dproj_o[:, 1536:2048] = dgbz_ref[:, 512:1024]

        dqn = (dq1[0].astype(F32) + nat_q(dq4, 4) + nat_q(dq16, 16)) * (1.0 / 8.0)
        dk1 = jnp.concatenate([dkv1[0, j, 0] for j in range(N_PAIR)], axis=1)
        dv1 = jnp.concatenate([dkv1[0, j, 1] for j in range(N_PAIR)], axis=1)
        dkn = dk1 + nat_kv(dkv4, 4, 0) + nat_kv(dkv16, 16, 0)
        dvn = dv1 + nat_kv(dkv4, 4, 1) + nat_kv(dkv16, 16, 1)
        g = jnp.concatenate([dqn, dkn], axis=1) * qkw_ref[...]
        raw = qkr_ref[...].astype(F32)
        rr = lax.rsqrt(_group_sum(raw * raw, b_ref, split=False) * (1.0 / HEAD_DIM) + EPS)
        proj_gq = _group_sum(g * raw, b_ref) * (1.0 / HEAD_DIM)
        draw = rr * g - raw * (rr * rr * rr) * proj_gq
        dqkw_o[...] += jnp.sum(jnp.concatenate([dqn, dkn], axis=1) * raw * rr, axis=0, keepdims=True)
        dproj_o[:, 2048:3072] = draw.astype(BF16)
        dproj_o[:, 3072:3584] = dvn.astype(BF16)
        dproj_o[:, 3584:4096] = dza_ref[...]

        nt_dims = (((1,), (1,)), ((), ()))
        dh = lax.dot_general(dproj_o[:, 0:1024], w_ref[0], nt_dims, preferred_element_type=F32)
        for b in range(1, 4):
            dh += lax.dot_general(dproj_o[:, 1024 * b:1024 * b + 1024], w_ref[b], nt_dims,
                                  preferred_element_type=F32)

        xf = x_ref[...]
        r = lax.rsqrt(jnp.mean(xf * xf, axis=-1, keepdims=True) + EPS)
        gh = dh * nw_ref[...]
        dnw_o[...] += jnp.sum(dh * xf * r, axis=0, keepdims=True)
        mean_gx = jnp.mean(gh * xf, axis=-1, keepdims=True)
        gx_o[...] = dout_ref[...] + r * gh - xf * (r * r * r) * mean_gx

    row = lambda w, j=0: pl.BlockSpec((tm, w), lambda i: (i, j))
    full = lambda shp: pl.BlockSpec(shp, lambda i: (0,) * len(shp))
    prev = pl.BlockSpec((hb, 512), lambda i: (jnp.maximum(i * (tm // hb) - 1, 0), 0))
    nxt = pl.BlockSpec((hb, 512), lambda i: (jnp.minimum((i + 1) * (tm // hb), s // hb - 1), 0))
    return pl.pallas_call(
        body, name="bwd_tail", grid=(nt,),
        out_shape=(jax.ShapeDtypeStruct((s, 1024), F32), jax.ShapeDtypeStruct((s, 4096), BF16),
                   jax.ShapeDtypeStruct((1, 1024), F32), jax.ShapeDtypeStruct((1, 1024), F32)),
        in_specs=[_class_spec(d, 512, tm) for d in DILATIONS]
        + [pl.BlockSpec((d, N_PAIR, 2, tm // d, 128), lambda i: (0, 0, 0, i, 0)) for d in DILATIONS]
        + [row(1024), full((1, 1024)), row(512), row(1024), row(512), prev, nxt,
           row(512, 0), row(512, 2), full((8, 512)), _resident((4, 1024, 1024)), row(1024),
           full((1, 1024)), row(1024), full((256, 256))],
        out_specs=(row(1024), row(4096), full((1, 1024)), full((1, 1024))),
        scratch_shapes=[pltpu.VMEM((4, tm, 128), F32), pltpu.VMEM((4, tm, 128), F32)],
        compiler_params=_params(("arbitrary",)),
    )(*dq_g, *dkv_g, qkr, qkw, dza, dgbz, dzc, dzc, dzc, cg, cg, cw, wblk, x, norm_w, dout, b256)


def _wgrad(a, b, row_blocked, name):
    s, m = a.shape
    n = b.shape[1]
    tk = 1024
    ncol = min(n, 2048)
    nj, nk = n // ncol, s // tk

    def body(a_ref, b_ref, o_ref, acc):
        kk = pl.program_id(1)

        @pl.when(kk == 0)
        def _():
            acc[...] = jnp.zeros_like(acc)

        acc[...] += lax.dot_general(a_ref[...], b_ref[...].astype(BF16), (((0,), (0,)), ((), ())),
                                    preferred_element_type=F32)

        @pl.when(kk == nk - 1)
        def _():
            blocks, _, rows, _ = o_ref.shape
            for blk in range(blocks):
                for half in range(2):
                    if row_blocked:
                        r0 = (2 * blk + half) * rows
                        o_ref[blk, half] = acc[r0:r0 + rows, :].astype(BF16)
                    else:
                        o_ref[blk, half] = acc[half * rows:(half + 1) * rows,
                                               1024 * blk:1024 * (blk + 1)].astype(BF16)

    if row_blocked:
        out_shape = jax.ShapeDtypeStruct((4, 2, m // 8, 1024), BF16)
        out_spec = pl.BlockSpec((4, 2, m // 8, 1024), lambda j, k: (0, 0, 0, 0))
    else:
        out_shape = jax.ShapeDtypeStruct((n // 1024, 2, m // 2, 1024), BF16)
        out_spec = pl.BlockSpec((ncol // 1024, 2, m // 2, 1024), lambda j, k: (j, 0, 0, 0))
    return pl.pallas_call(
        body, name=name, grid=(nj, nk),
        out_shape=out_shape,
        in_specs=[pl.BlockSpec((tk, m), lambda j, k: (k, 0)), pl.BlockSpec((tk, ncol), lambda j, k: (k, j))],
        out_specs=out_spec,
        scratch_shapes=[pltpu.VMEM((m, ncol), F32)],
        compiler_params=_params(("parallel", "arbitrary")),
    )(a, b)


def _dbias(dsums, onehot_all):
    def body(ds1_ref, ds4_ref, ds16_ref, oh_ref, o_ref):
        @pl.when(pl.program_id(0) == 0)
        def _():
            o_ref[...] = jnp.zeros_like(o_ref)

        rowq = lax.broadcasted_iota(jnp.int32, (2 * QB, KB), 0) & (QB - 1)
        hrow = lax.broadcasted_iota(jnp.int32, (8, KB), 0)
        for g, ds_ref in enumerate((ds1_ref, ds4_ref, ds16_ref)):
            diag = jnp.zeros((8, KB), F32)
            for p in range(N_PAIR):
                y = ds_ref[p]
                for bit in range(7):
                    sh = 1 << bit
                    y = jnp.where((rowq & sh) != 0, pltpu.roll(y, KB - sh, 1), y)
                da = jnp.sum(y[0:QB], axis=0, keepdims=True)
                db = jnp.sum(y[QB:2 * QB], axis=0, keepdims=True)
                diag = jnp.where(hrow == 2 * p, da, diag)
                diag = jnp.where(hrow == 2 * p + 1, db, diag)
            o_ref[...] += jnp.dot(diag, oh_ref[g], preferred_element_type=F32, precision=lax.Precision.HIGHEST)

    ds_spec = pl.BlockSpec((N_PAIR, None, 2 * QB, KB), lambda v: (0, v, 0, 0))
    return pl.pallas_call(
        body, name="dbias", grid=(3,),
        out_shape=jax.ShapeDtypeStruct((8, 128), F32),
        in_specs=[ds_spec, ds_spec, ds_spec, pl.BlockSpec((3, None, KB, 128), lambda v: (0, v, 0, 0))],
        out_specs=pl.BlockSpec((8, 128), lambda v: (0, 0)),
        compiler_params=_params(("arbitrary",)),
    )(*dsums, onehot_all)


def _gsync(pw_in, pw_out, small):
    hin, hout = pw_in.shape[2], pw_out.shape[2]
    nsmall = small.shape[0]

    def body(pin_hbm, pout_hbm, small_ref, gin_o, gout_o, small_o,
             mine_in, recv_in, sbuf_in, rbuf_in, mine_out, recv_out, sbuf_out, rbuf_out, gather,
             lsem, asend, arecv, bsend, brecv, csend, crecv, ssend, srecv):
        x, y, c = lax.axis_index("x"), lax.axis_index("y"), lax.axis_index("c")
        b = 2 * x + y
        dev = 4 * x + 2 * y + c
        sib = (x, y, 1 - c)

        def rcopy(src, dst, ssem, rsem, to):
            return pltpu.make_async_remote_copy(src_ref=src, dst_ref=dst, send_sem=ssem, recv_sem=rsem,
                                                device_id=to, device_id_type=MESH)

        gather[dev] = small_ref[...]
        s_sends = []
        for k in range(1, 8):
            to = (x ^ (k >> 2), y ^ ((k >> 1) & 1), c ^ (k & 1))
            cp = rcopy(gather.at[dev], gather.at[dev], ssend.at[k - 1], srecv.at[k - 1], to)
            cp.start()
            s_sends.append(cp)

        a_in = rcopy(pin_hbm.at[:, 1 - c], recv_in, asend.at[0], arecv.at[0], sib)
        a_out = rcopy(pout_hbm.at[:, 1 - c], recv_out, asend.at[1], arecv.at[1], sib)
        a_in.start()
        a_out.start()
        l_in = pltpu.make_async_copy(pin_hbm.at[:, c], mine_in, lsem.at[0])
        l_out = pltpu.make_async_copy(pout_hbm.at[:, c], mine_out, lsem.at[1])
        l_in.start()
        l_out.start()
        l_in.wait()
        l_out.wait()

        def stage_b(a_cp, mine, recv, sbuf, rbuf, base):
            a_cp.wait_recv()
            sends = []
            for k in (1, 2, 3):
                bk = b ^ k
                sbuf[k - 1] = (mine[bk].astype(F32) + recv[bk].astype(F32)).astype(BF16)
                cp = rcopy(sbuf.at[k - 1], rbuf.at[k - 1], bsend.at[base + k - 1], brecv.at[base + k - 1],
                           (x ^ (k >> 1), y ^ (k & 1), c))
                cp.start()
                sends.append(cp)
            return sends

        b_in = stage_b(a_in, mine_in, recv_in, sbuf_in, rbuf_in, 0)
        b_out = stage_b(a_out, mine_out, recv_out, sbuf_out, rbuf_out, 3)

        def stage_c(b_sends, mine, recv, rbuf, g_o, half, idx):
            acc = mine[b].astype(F32) + recv[b].astype(F32)
            for k in (1, 2, 3):
                b_sends[k - 1].wait_recv()
                acc = acc + rbuf[k - 1].astype(F32)
            rows = g_o.at[pl.ds(pl.multiple_of(c * half, half), half), :]
            g_o[pl.ds(pl.multiple_of(c * half, half), half), :] = acc
            cp = rcopy(rows, rows, csend.at[idx], crecv.at[idx], sib)
            cp.start()
            return cp

        c_in = stage_c(b_in, mine_in, recv_in, rbuf_in, gin_o, hin, 0)
        c_out = stage_c(b_out, mine_out, recv_out, rbuf_out, gout_o, hout, 1)

        for cp in s_sends:
            cp.wait_recv()
        tot = gather[0]
        for d in range(1, 8):
            tot = tot + gather[d]
        small_o[...] = tot

        for g_o, half, idx in ((gin_o, hin, 0), (gout_o, hout, 1)):
            other = g_o.at[pl.ds(pl.multiple_of((1 - c) * half, half), half), :]
            rcopy(other, other, csend.at[idx], crecv.at[idx], sib).wait_recv()
        for cp in s_sends + [a_in, a_out] + b_in + b_out + [c_in, c_out]:
            cp.wait_send()

    vm = pl.BlockSpec(memory_space=pltpu.VMEM)
    hbm = pl.BlockSpec(memory_space=pl.ANY)
    return pl.pallas_call(
        body, name="gsync",
        out_shape=(jax.ShapeDtypeStruct((2 * hin, 1024), F32), jax.ShapeDtypeStruct((2 * hout, 1024), F32),
                   jax.ShapeDtypeStruct((nsmall, 128), F32)),
        in_specs=[hbm, hbm, vm], out_specs=(vm, vm, vm),
        scratch_shapes=[pltpu.VMEM((4, hin, 1024), BF16), pltpu.VMEM((4, hin, 1024), BF16),
                        pltpu.VMEM((3, hin, 1024), BF16), pltpu.VMEM((3, hin, 1024), BF16),
                        pltpu.VMEM((4, hout, 1024), BF16), pltpu.VMEM((4, hout, 1024), BF16),
                        pltpu.VMEM((3, hout, 1024), BF16), pltpu.VMEM((3, hout, 1024), BF16),
                        pltpu.VMEM((8, nsmall, 128), F32),
                        pltpu.SemaphoreType.DMA((2,)),
                        pltpu.SemaphoreType.DMA((2,)), pltpu.SemaphoreType.DMA((2,)),
                        pltpu.SemaphoreType.DMA((6,)), pltpu.SemaphoreType.DMA((6,)),
                        pltpu.SemaphoreType.DMA((2,)), pltpu.SemaphoreType.DMA((2,)),
                        pltpu.SemaphoreType.DMA((7,)), pltpu.SemaphoreType.DMA((7,))],
        compiler_params=_params(),
    )(pw_in, pw_out, small)


def _adamw_math(w, g, m, v):
    m = ADAM_B1 * m + (1.0 - ADAM_B1) * g
    v = ADAM_B2 * v + (1.0 - ADAM_B2) * (g * g)
    m_hat = m / (1.0 - ADAM_B1 ** ADAM_STEP)
    v_hat = v / (1.0 - ADAM_B2 ** ADAM_STEP)
    delta = -ADAM_LR * (m_hat / (jnp.sqrt(v_hat) + ADAM_EPS) + ADAM_WD * w)
    return delta, m, v


def _adamw(w, g, m, v, name):
    rows, cols = w.shape
    tr = 256 if rows % 256 == 0 else rows

    def body(w_ref, g_ref, m_ref, v_ref, d_o, m_o, v_o):
        d, m2, v2 = _adamw_math(w_ref[...], g_ref[...], m_ref[...], v_ref[...])
        d_o[...] = d
        m_o[...] = m2
        v_o[...] = v2

    spec = pl.BlockSpec((tr, cols), lambda i: (i, 0))
    shp = jax.ShapeDtypeStruct((rows, cols), F32)
    return pl.pallas_call(
        body, name=name, grid=(rows // tr,), out_shape=(shp, shp, shp),
        in_specs=[spec] * 4, out_specs=(spec, spec, spec),
        compiler_params=_params(("parallel",)),
    )(w, g, m, v)


def _fold_heads(dqkw):
    def body(x_ref, o_ref):
        xs = x_ref[...]
        sq = xs[0:1] + xs[1:2] + xs[2:3] + xs[3:4]
        sk = xs[4:5] + xs[5:6] + xs[6:7] + xs[7:8]
        both = jnp.concatenate([sq, sk], axis=0)
        o_ref[...] = both + pltpu.roll(both, HEAD_DIM, 1)

    vm = pl.BlockSpec(memory_space=pltpu.VMEM)
    return pl.pallas_call(body, name="fold_heads", out_shape=jax.ShapeDtypeStruct((2, 128), F32),
                          in_specs=[vm], out_specs=vm, compiler_params=_params())(dqkw)


def kernel(x, norm_w, w_in, conv_w, conv_b, q_norm_w, k_norm_w, rel_bias, w_out, loss_target, m_norm_w, m_w_in, m_conv_w, m_conv_b, m_q_norm_w, m_k_norm_w, m_rel_bias, m_w_out, v_norm_w, v_w_in, v_conv_w, v_conv_b, v_q_norm_w, v_k_norm_w, v_rel_bias, v_w_out):
    x2 = x[0]
    tgt = loss_target[0]
    blk = 2 * lax.axis_index("x") + lax.axis_index("y")

    conv_w8 = jnp.pad(conv_w, ((0, 5), (0, 0)))
    wblk, woutblk, cwblk = _wgather(w_in, w_out, conv_w8)
    wout_full = woutblk.reshape(1024, 1024)
    cw_full = cwblk.transpose(1, 0, 2).reshape(8, 512)

    qkw = jnp.concatenate([jnp.tile(q_norm_w, 8) * 0.125, jnp.tile(k_norm_w, 8)])[None, :]
    qkw_raw = jnp.concatenate([jnp.tile(q_norm_w, 8), jnp.tile(k_norm_w, 8)])[None, :]
    gidx = jnp.arange(256) // HEAD_DIM
    b256 = (gidx[:, None] == gidx[None, :]).astype(BF16)

    h, cg, qkr, qkn, vz, qkn4, v4, qkn16, v16 = _proj(x2, norm_w[None, :], wblk, qkw, b256)

    biases = _bias_tables(rel_bias)
    qkn_l = [qkn[None], qkn4, qkn16]
    v_l = [vz[None], v4, v16]
    o_g, lse_g = [], []
    for gi, d in enumerate(DILATIONS):
        o_l, lse_l = _attn_fwd(qkn_l[gi], v_l[gi], biases, gi, f"attn_fwd_d{d}")
        o_g.append(o_l)
        lse_g.append(lse_l)

    (y, dout, ld1, do1, dza, dgbz, dzc, loss_p, dcb, dcw, do4, ld4, do16, ld16) = _combine(
        o_g, lse_g, cg, vz, x2, tgt, wout_full, cw_full, conv_b[None, :], b256)

    dq_g, dkv_g, dsums = [], [], []
    for gi, (d, do_l, ld_l) in enumerate(zip(DILATIONS, (do1, do4, do16), (ld1, ld4, ld16))):
        dq_l, dkv_l, dsum = _attn_bwd(qkn_l[gi], v_l[gi], do_l, ld_l, biases, gi, f"attn_bwd_d{d}")
        dq_g.append(dq_l)
        dkv_g.append(dkv_l)
        dsums.append(dsum)

    grad_x, dproj, dnw, dqkw = _bwd_tail(dq_g, dkv_g, qkr, qkw_raw, dza, dgbz, dzc, cg, cw_full, wblk,
                                         x2, norm_w[None, :], dout, b256)

    pw_in = _wgrad(h, dproj, False, "wgrad_in")
    pw_out = _wgrad(y, dout, True, "wgrad_out")
    dbias8 = _dbias(dsums, jnp.stack([_diag_bucket_onehot(d) for d in DILATIONS], axis=0))

    small = jnp.concatenate([dnw.reshape(8, 128), dcb.reshape(4, 128), dqkw.reshape(8, 128),
                             dcw[0:3].reshape(12, 128), dbias8, jnp.pad(loss_p, ((0, 7), (0, 0)))], axis=0)
    g_win, g_wout, gsmall = _gsync(pw_in, pw_out, small)

    g_nw = gsmall[0:8].reshape(1024)
    g_cb = gsmall[8:12].reshape(512)
    folded = _fold_heads(gsmall[12:20])
    g_qw, g_kw = folded[0, 0:64], folded[1, 0:64]
    g_cw = lax.dynamic_slice(gsmall[20:32].reshape(3, 512), (0, blk * 128), (3, 128))
    g_rb = gsmall[32:40][:, 0:32].T
    loss = gsmall[40, 0]

    d_win, nm_win, nv_win = _adamw(w_in, g_win, m_w_in, v_w_in, "adamw_w_in")
    d_wout, nm_wout, nv_wout = _adamw(w_out, g_wout, m_w_out, v_w_out, "adamw_w_out")

    def pack(parts):
        rows = [parts[0].reshape(8, 128), parts[1].reshape(4, 128),
                jnp.pad(parts[2], (0, 64))[None, :], jnp.pad(parts[3], (0, 64))[None, :],
                parts[4], jnp.pad(parts[5].T, ((0, 0), (0, 96)))]
        return jnp.concatenate(rows, axis=0)

    ws = pack([norm_w, conv_b, q_norm_w, k_norm_w, conv_w, rel_bias])
    gs = pack([g_nw, g_cb, g_qw, g_kw, g_cw, g_rb])
    ms = pack([m_norm_w, m_conv_b, m_q_norm_w, m_k_norm_w, m_conv_w, m_rel_bias])
    vs = pack([v_norm_w, v_conv_b, v_q_norm_w, v_k_norm_w, v_conv_w, v_rel_bias])
    rpad = lambda a: jnp.pad(a, ((0, 7), (0, 0)))
    d_s, nm_s, nv_s = _adamw(rpad(ws), rpad(gs), rpad(ms), rpad(vs), "adamw_small")

    def unpack(a):
        return (a[0:8].reshape(1024), a[12:13, 0:64].reshape(64), a[13:14, 0:64].reshape(64),
                a[14:17], a[8:12].reshape(512), a[17:25, 0:32].T)

    def ordered(nw, win, cw, cb, qw, kw, rb, wout):
        return (nw, win, cw, cb, qw, kw, rb, wout)

    g_un = (g_nw, g_qw, g_kw, g_cw, g_cb, g_rb)
    outs = [loss, grad_x[None]]
    for un, win_v, wout_v in ((g_un, g_win, g_wout), (unpack(d_s), d_win, d_wout),
                              (unpack(nm_s), nm_win, nm_wout), (unpack(nv_s), nv_win, nv_wout)):
        nw, qw, kw, cw, cb, rb = un
        outs.extend(ordered(nw, win_v, cw, cb, qw, kw, rb, wout_v))
    return tuple(outs)
```

```python
import math

import jax
import jax.numpy as jnp
from jax import lax
from jax.experimental import pallas as pl
from jax.experimental.pallas import tpu as pltpu

F32 = jnp.float32
BF16 = jnp.bfloat16
MESH = pl.DeviceIdType.MESH

D_MODEL = 1024
CONV_W = 512
ATTN_W = 512
HEAD_DIM = 64
N_PAIR = 4
DILATIONS = (1, 4, 16)
HALF = 64
QB = 128
KB = QB + 2 * HALF
NUM_BUCKETS = 32
MAX_DISTANCE = 1024
EPS = 1e-6
NEG = -1e30
ADAM_LR, ADAM_B1, ADAM_B2, ADAM_EPS, ADAM_WD, ADAM_STEP = 0.001, 0.9, 0.999, 1e-08, 0.01, 10
VMEM_LIMIT = 48 << 20


def _params(sem=None, vmem=VMEM_LIMIT, **kw):
    if sem is not None:
        kw["dimension_semantics"] = sem
    return pltpu.CompilerParams(vmem_limit_bytes=vmem, **kw)


def _sigmoid(z):
    return 1.0 / (1.0 + jnp.exp(-z))


def _group_sum(val, b_ref, split=True):
    hi = val.astype(BF16)
    lo = (val - hi.astype(F32)).astype(BF16) if split else None
    outs = []
    for j in range(val.shape[1] // 256):
        sl = slice(256 * j, 256 * j + 256)
        part = jnp.dot(hi[:, sl], b_ref[...], preferred_element_type=F32)
        if split:
            part = part + jnp.dot(lo[:, sl], b_ref[...], preferred_element_type=F32)
        outs.append(part)
    return outs[0] if len(outs) == 1 else jnp.concatenate(outs, axis=1)


def _t5_bucket(rel):
    half_b = NUM_BUCKETS // 2
    max_exact = half_b // 2
    ret = jnp.where(rel > 0, half_b, 0)
    n = jnp.abs(rel)
    nf = jnp.maximum(n, 1).astype(F32)
    large = max_exact + (jnp.log(nf / max_exact) / math.log(MAX_DISTANCE / max_exact)
                         * (half_b - max_exact)).astype(jnp.int32)
    large = jnp.minimum(large, half_b - 1)
    return ret + jnp.where(n < max_exact, n, large)


def _bias_tables(rel_bias):
    rows = []
    key = jnp.arange(KB)
    for dilation in DILATIONS:
        for variant in range(3):
            off = (0, HALF, 2 * HALF)[variant]
            rel = ((key - off + KB // 2) % KB) - KB // 2
            bkt = _t5_bucket(jnp.clip(rel, -HALF, HALF) * dilation)
            rows.append(jnp.where(jnp.abs(rel) <= HALF, bkt, -1))
    bkt_all = jnp.broadcast_to(jnp.stack(rows, axis=0).astype(jnp.int32)[:, None, :], (9, 8, KB))

    def body(rb_ref, bkt_ref, o_ref):
        bkt = bkt_ref[...]
        off = (pl.program_id(0) % 3) * HALF
        rel = (lax.broadcasted_iota(jnp.int32, (QB, KB), 1) - lax.broadcasted_iota(jnp.int32, (QB, KB), 0)) - off
        band = jnp.abs(rel) <= HALF
        for h in range(8):
            acc = jnp.full((8, KB), NEG, F32)
            for b in range(NUM_BUCKETS):
                acc = jnp.where(bkt == b, rb_ref[b, h], acc)
            rolled = pltpu.roll(jnp.broadcast_to(acc[0:1], (QB, KB)), 0, 1, stride=1, stride_axis=0)
            o_ref[h] = jnp.where(band, rolled, NEG)

    out = pl.pallas_call(
        body, name="bias_tables", grid=(9,),
        out_shape=jax.ShapeDtypeStruct((9, 8, QB, KB), F32),
        in_specs=[pl.BlockSpec(memory_space=pltpu.SMEM), pl.BlockSpec((None, 8, KB), lambda i: (i, 0, 0))],
        out_specs=pl.BlockSpec((None, 8, QB, KB), lambda i: (i, 0, 0, 0)),
        compiler_params=_params(("parallel",)),
    )(rel_bias, bkt_all)
    return out.reshape(3, 3, N_PAIR, 2 * QB, KB)


def _diag_bucket_onehot(dilation):
    out = []
    c = jnp.arange(KB)
    for variant in range(3):
        off = (0, HALF, 2 * HALF)[variant]
        rel = ((c - off + 128) % 256) - 128
        band = jnp.abs(rel) <= HALF
        bkt = _t5_bucket(jnp.clip(rel, -HALF, HALF) * dilation)
        oh = (bkt[:, None] == jnp.arange(128)[None, :]) & band[:, None]
        out.append(oh.astype(F32))
    return jnp.stack(out, axis=0)


def _wgather(w_in, w_out, conv_w):
    rin, rout = w_in.shape[0] // 2, w_out.shape[0] // 2

    def body(win_ref, wout_ref, cw_ref, win_o, wout_o, cw_o, send_sems, recv_sems):
        x, y, c = lax.axis_index("x"), lax.axis_index("y"), lax.axis_index("c")
        b = 2 * x + y
        win_o[b] = win_ref[...].astype(BF16)
        wout_o[b] = wout_ref[...].astype(BF16)
        cw_o[b] = cw_ref[...]

        def peer(k):
            return (x ^ (k >> 1), y ^ (k & 1))

        def piece(ref, blk, half_rows, core):
            return ref.at[blk, pl.ds(core * half_rows, half_rows), :]

        def copy(sem, src, dst, to):
            return pltpu.make_async_remote_copy(src_ref=src, dst_ref=dst, send_sem=send_sems.at[sem],
                                                recv_sem=recv_sems.at[sem], device_id=to, device_id_type=MESH)

        sends = []
        for k in (1, 2, 3):
            px, py = peer(k)
            sends.append(copy(k - 1, piece(win_o, b, rin, c), piece(win_o, b, rin, c), (px, py, c)))
            sends.append(copy(3 + k - 1, piece(wout_o, b, rout, c), piece(wout_o, b, rout, c), (px, py, c)))
            sends.append(copy(6 + k - 1, cw_o.at[b], cw_o.at[b], (px, py, c)))
        for cp in sends:
            cp.start()
        fwd = []
        for k in (1, 2, 3):
            px, py = peer(k)
            bk = 2 * px + py
            copy(k - 1, piece(win_o, bk, rin, c), piece(win_o, bk, rin, c), (px, py, c)).wait_recv()
            f = copy(9 + k - 1, piece(win_o, bk, rin, c), piece(win_o, bk, rin, c), (x, y, 1 - c))
            f.start()
            fwd.append(f)
            copy(3 + k - 1, piece(wout_o, bk, rout, c), piece(wout_o, bk, rout, c), (px, py, c)).wait_recv()
            f = copy(12 + k - 1, piece(wout_o, bk, rout, c), piece(wout_o, bk, rout, c), (x, y, 1 - c))
            f.start()
            fwd.append(f)
            copy(6 + k - 1, cw_o.at[bk], cw_o.at[bk], (px, py, c)).wait_recv()
        for k in (1, 2, 3):
            px, py = peer(k)
            bk = 2 * px + py
            copy(9 + k - 1, piece(win_o, bk, rin, 1 - c), piece(win_o, bk, rin, 1 - c), (x, y, 1 - c)).wait_recv()
            copy(12 + k - 1, piece(wout_o, bk, rout, 1 - c), piece(wout_o, bk, rout, 1 - c), (x, y, 1 - c)).wait_recv()
        for cp in sends + fwd:
            cp.wait_send()

    vm = pl.BlockSpec(memory_space=pltpu.VMEM)
    return pl.pallas_call(
        body, name="wgather",
        out_shape=(jax.ShapeDtypeStruct((4,) + w_in.shape, BF16),
                   jax.ShapeDtypeStruct((4,) + w_out.shape, BF16),
                   jax.ShapeDtypeStruct((4,) + conv_w.shape, F32)),
        in_specs=[vm, vm, vm], out_specs=(vm, vm, vm),
        scratch_shapes=[pltpu.SemaphoreType.DMA((15,)), pltpu.SemaphoreType.DMA((15,))],
        compiler_params=_params(),
    )(w_in, w_out, conv_w)


TM_MATMUL = 512
TM_COMBINE = 256


def _resident(shape):
    return pl.BlockSpec(shape, lambda i: (0,) * len(shape), pipeline_mode=pl.Buffered(1))


def _to_slabs(slab, val, j0=0):
    for j in range(val.shape[1] // 128):
        slab[j0 + j] = val[:, 128 * j:128 * (j + 1)]


def _scatter_classes(slab, j0, nj, out_ref, d, part=0, mid=None):
    tm = slab.shape[1]
    n = tm // d
    if d == 4:
        for r in range(d):
            for j in range(nj):
                out_ref[r, part * n:(part + 1) * n, 128 * j:128 * (j + 1)] = (
                    slab[j0 + j, pl.ds(r, n, stride=d), :].astype(out_ref.dtype))
        return
    q = tm // 4
    for lo in range(4):
        for j in range(nj):
            mid[j0 + j, lo * q:(lo + 1) * q, :] = slab[j0 + j, pl.ds(lo, q, stride=4), :]
    for hi in range(4):
        for lo in range(4):
            for j in range(nj):
                out_ref[4 * hi + lo, part * n:(part + 1) * n, 128 * j:128 * (j + 1)] = (
                    mid[j0 + j, pl.ds(lo * q + hi, n, stride=4), :].astype(out_ref.dtype))


def _gather_classes(slab, piece, nj, d, mid=None):
    tm = slab.shape[1]
    n = tm // d
    if d == 4:
        for r in range(d):
            for j in range(nj):
                slab[j, pl.ds(r, n, stride=d), :] = piece(r, j).astype(F32)
    else:
        q = tm // 4
        for hi in range(4):
            for lo in range(4):
                for j in range(nj):
                    mid[j, pl.ds(lo * q + hi, n, stride=4), :] = piece(4 * hi + lo, j).astype(F32)
        for lo in range(4):
            for j in range(nj):
                slab[j, pl.ds(lo, q, stride=4), :] = mid[j, lo * q:(lo + 1) * q, :]
    return jnp.concatenate([slab[j] for j in range(nj)], axis=1)


def _class_spec(d, width, tm):
    return pl.BlockSpec((d, tm // d, width), lambda i: (0, i, 0))


def _proj(x, norm_w, wblk, qkw, b256):
    s = x.shape[0]
    tm = TM_MATMUL
    nparts = 2
    tp = tm // nparts

    def body(x_ref, nw_ref, w_ref, qkw_ref, b_ref, h_o, cg_o, qkr_o, qkn_o, vz_o, qkn4_o, v4_o, qkn16_o, v16_o,
             slabs, mids):
        for part in range(nparts):
            rows = slice(part * tp, (part + 1) * tp)
            slab = slabs.at[part]
            xf = x_ref[rows, :]
            r = lax.rsqrt(jnp.mean(xf * xf, axis=-1, keepdims=True) + EPS)
            h = (xf * r * nw_ref[...]).astype(BF16)
            h_o[rows, :] = h
            p2 = jnp.dot(h, w_ref[2], preferred_element_type=F32)
            qkr_o[rows, :] = p2.astype(BF16)
            ss = _group_sum(p2 * p2, b_ref, split=False)
            rr = lax.rsqrt(ss * (1.0 / HEAD_DIM) + EPS)
            qkn = p2 * rr * qkw_ref[...]
            qkn_o[rows, :] = qkn.astype(BF16)
            _to_slabs(slab, qkn)
            p3 = jnp.dot(h, w_ref[3], preferred_element_type=F32)
            vz_o[rows, :] = p3.astype(BF16)
            _to_slabs(slab, p3[:, 0:512], 8)
            cg_o[rows, 0:1024] = jnp.dot(h, w_ref[0], preferred_element_type=F32).astype(BF16)
            cg_o[rows, 1024:2048] = jnp.dot(h, w_ref[1], preferred_element_type=F32).astype(BF16)
            for d, q_o, v_o in ((4, qkn4_o, v4_o), (16, qkn16_o, v16_o)):
                _scatter_classes(slab, 0, 8, q_o, d, part, mids.at[part])
                _scatter_classes(slab, 8, 4, v_o, d, part, mids.at[part])

    row = lambda w: pl.BlockSpec((tm, w), lambda i: (i, 0))
    full = lambda shp: pl.BlockSpec(shp, lambda i: (0,) * len(shp))
    nat = lambda w: jax.ShapeDtypeStruct((s, w), BF16)
    cls = lambda d, w: jax.ShapeDtypeStruct((d, s // d, w), BF16)
    return pl.pallas_call(
        body, name="proj", grid=(s // tm,),
        out_shape=(nat(1024), nat(2048), nat(1024), nat(1024), nat(1024),
                   cls(4, 1024), cls(4, 512), cls(16, 1024), cls(16, 512)),
        in_specs=[row(1024), full((1, 1024)), _resident((4, 1024, 1024)), full((1, 1024)), full((256, 256))],
        out_specs=(row(1024), row(2048), row(1024), row(1024), row(1024),
                   _class_spec(4, 1024, tm), _class_spec(4, 512, tm),
                   _class_spec(16, 1024, tm), _class_spec(16, 512, tm)),
        scratch_shapes=[pltpu.VMEM((nparts, 12, tp, 128), F32), pltpu.VMEM((nparts, 12, tp, 128), F32)],
        compiler_params=_params(("parallel",)),
    )(x, norm_w, wblk, qkw, b256)


def _block_coords(t, i, nsub, nb, length):
    n = t * nsub + i
    q0 = i * QB
    start = pl.multiple_of(jnp.clip(n * QB - HALF, 0, length - KB), HALF)
    variant = jnp.where(n == 0, 0, jnp.where(n == nb - 1, 2, 1))
    return q0, start, variant


def _split_heads(a, lo):
    zero = jnp.zeros_like(a)
    return jnp.concatenate([jnp.where(lo, a, zero), jnp.where(lo, zero, a)], axis=0)


def _col_pair(ref, q0, lane):
    return jnp.concatenate([ref[pl.ds(q0, QB), lane:lane + 1],
                            ref[pl.ds(q0, QB), HEAD_DIM + lane:HEAD_DIM + lane + 1]], axis=0)


def _attn_fwd(qkn_l, v_l, bias, gi, name):
    r_cls, length, _ = qkn_l.shape
    qt = min(length, 2048)
    nb, nsub = length // QB, qt // QB

    def body(q_ref, k_ref, v_ref, b_ref, o_ref, lse_ref):
        t = pl.program_id(2)
        lo = lax.broadcasted_iota(jnp.int32, (QB, 128), 1) < HEAD_DIM

        starts, logits = [], []
        for i in range(nsub):
            _, start, variant = _block_coords(t, i, nsub, nb, length)
            qq = _split_heads(q_ref[i * QB:(i + 1) * QB, :], lo)
            k = k_ref[pl.ds(start, KB), :]
            logits.append(lax.dot_general(qq, k, (((1,), (1,)), ((), ())), preferred_element_type=F32)
                          + b_ref[variant])
            starts.append(start)
        lg = jnp.concatenate(logits, axis=0)
        m = jnp.max(lg, axis=-1, keepdims=True)
        p = jnp.exp(lg - m)
        pb = p.astype(BF16)
        l = jnp.sum(p, axis=-1, keepdims=True)
        lse = jnp.broadcast_to(m + jnp.log(l), (nsub * 2 * QB, 128))
        inv = 1.0 / l
        for i in range(nsub):
            rows = slice(2 * QB * i, 2 * QB * (i + 1))
            v = v_ref[pl.ds(starts[i], KB), :]
            pv = jnp.dot(pb[rows], v, preferred_element_type=F32) * inv[rows]
            o_ref[i * QB:(i + 1) * QB, :] = jnp.where(lo, pv[0:QB], pv[QB:2 * QB]).astype(BF16)
            ls = lse[rows]
            lse_ref[i * QB:(i + 1) * QB, :] = jnp.where(lo, ls[0:QB], ls[QB:2 * QB])

    return pl.pallas_call(
        body, name=name, grid=(N_PAIR, r_cls, length // qt),
        out_shape=(jax.ShapeDtypeStruct((r_cls, length, 512), BF16),
                   jax.ShapeDtypeStruct((r_cls, length, 512), F32)),
        in_specs=[pl.BlockSpec((None, qt, 128), lambda p, r, t: (r, t, p)),
                  pl.BlockSpec((None, length, 128), lambda p, r, t: (r, 0, 4 + p)),
                  pl.BlockSpec((None, length, 128), lambda p, r, t: (r, 0, p)),
                  pl.BlockSpec((None, 3, None, 2 * QB, KB), lambda p, r, t: (gi, 0, p, 0, 0))],
        out_specs=(pl.BlockSpec((None, qt, 128), lambda p, r, t: (r, t, p)),
                   pl.BlockSpec((None, qt, 128), lambda p, r, t: (r, t, p))),
        compiler_params=_params(("parallel", "parallel", "arbitrary")),
    )(qkn_l, qkn_l, v_l, bias)


def _combine(o_g, lse_g, cg, vz, x, tgt, wout, cw, cb, b256):
    s = x.shape[0]
    tm = TM_COMBINE
    hb = 16
    nt = s // tm

    def body(o1, o4, o16, l1, l4, l16, cg_ref, cgp_ref, cgn_ref, za_ref, x_ref, t_ref, w_ref, cw_ref, cb_ref,
             b_ref, y_o, dout_o, ld1_o, do1_o, dza_o, dgbz_o, dzc_o, loss_o, dcb_o, dcw_o,
             do4_o, ld4_o, do16_o, ld16_o, slab, mid):
        i = pl.program_id(0)

        @pl.when(i == 0)
        def _():
            loss_o[...] = jnp.zeros_like(loss_o)
            dcb_o[...] = jnp.zeros_like(dcb_o)
            dcw_o[...] = jnp.zeros_like(dcw_o)

        u = cg_ref[:, 0:512].astype(F32)
        gb = cg_ref[:, 512:1024].astype(F32)
        gc = cg_ref[:, 1024:1536].astype(F32)
        zc = cg_ref[:, 1536:2048].astype(F32)
        tt = gc * u
        t_prev = cgp_ref[hb - 1:hb, 0:512].astype(F32) * cgp_ref[hb - 1:hb, 1024:1536].astype(F32)
        t_next = cgn_ref[0:1, 0:512].astype(F32) * cgn_ref[0:1, 1024:1536].astype(F32)
        t_prev = jnp.where(i == 0, 0.0, t_prev)
        t_next = jnp.where(i == nt - 1, 0.0, t_next)
        rows = lax.broadcasted_iota(jnp.int32, (tm, 512), 0)
        t_up = jnp.where(rows == 0, t_prev, pltpu.roll(tt, 1, 0))
        t_dn = jnp.where(rows == tm - 1, t_next, pltpu.roll(tt, tm - 1, 0))
        w0, w1, w2 = cw_ref[0:1, :], cw_ref[1:2, :], cw_ref[2:3, :]
        zb = w0 * t_up + w1 * tt + w2 * t_dn + cb_ref[...]
        sg = _sigmoid(zc)
        sz = zc * sg
        y_conv = gb * zb * sz

        a1, p1 = l1[0], o1[0].astype(F32)
        a4 = _gather_classes(slab, lambda r, j: l4[r, :, 128 * j:128 * (j + 1)], 4, 4)
        p4 = _gather_classes(slab, lambda r, j: o4[r, :, 128 * j:128 * (j + 1)], 4, 4)
        a16 = _gather_classes(slab, lambda r, j: l16[r, :, 128 * j:128 * (j + 1)], 4, 16, mid)
        p16 = _gather_classes(slab, lambda r, j: o16[r, :, 128 * j:128 * (j + 1)], 4, 16, mid)
        m = jnp.maximum(jnp.maximum(a1, a4), a16)
        e1, e4, e16 = jnp.exp(a1 - m), jnp.exp(a4 - m), jnp.exp(a16 - m)
        den = e1 + e4 + e16
        lse = m + jnp.log(den)
        o = (e1 * p1 + e4 * p4 + e16 * p16) / den
        za = za_ref[...].astype(F32)
        sga = _sigmoid(za)
        sa = za * sga
        y = jnp.concatenate([y_conv, o * sa], axis=1).astype(BF16)
        y_o[...] = y

        out = x_ref[...] + jnp.dot(y, w_ref[...], preferred_element_type=F32)
        diff = out - t_ref[...]
        loss_o[...] += (0.5 / D_MODEL) * jnp.sum(diff * diff)
        dout = diff * (1.0 / D_MODEL)
        dout_o[...] = dout
        dy = lax.dot_general(dout.astype(BF16), w_ref[...], (((1,), (1,)), ((), ())), preferred_element_type=F32)
        dyc, dya = dy[:, 0:512], dy[:, 512:1024]

        do = dya * sa
        dza_o[...] = (dya * o * (sga * (1.0 + za * (1.0 - sga)))).astype(BF16)
        lane = lax.broadcasted_iota(jnp.int32, (tm, 512), 1)
        ld = jnp.where((lane & (HEAD_DIM - 1)) < HEAD_DIM // 2, lse, _group_sum(do * o, b_ref))
        do1_o[0] = do.astype(BF16)
        ld1_o[0] = ld
        _to_slabs(slab, do)
        _scatter_classes(slab, 0, 4, do4_o, 4)
        _scatter_classes(slab, 0, 4, do16_o, 16, 0, mid)
        _to_slabs(slab, ld)
        _scatter_classes(slab, 0, 4, ld4_o, 4)
        _scatter_classes(slab, 0, 4, ld16_o, 16, 0, mid)

        dzc = dyc * sz * gb
        dzc_o[...] = dzc.astype(BF16)
        dgbz_o[:, 0:512] = (dyc * sz * zb).astype(BF16)
        dgbz_o[:, 512:1024] = (dyc * gb * zb * (sg * (1.0 + zc * (1.0 - sg)))).astype(BF16)
        dcb_o[...] += jnp.sum(dzc, axis=0, keepdims=True)
        dcw_o[0:1, :] += jnp.sum(dzc * t_up, axis=0, keepdims=True)
        dcw_o[1:2, :] += jnp.sum(dzc * tt, axis=0, keepdims=True)
        dcw_o[2:3, :] += jnp.sum(dzc * t_dn, axis=0, keepdims=True)

    row = lambda w, j=0: pl.BlockSpec((tm, w), lambda i: (i, j))
    full = lambda shp: pl.BlockSpec(shp, lambda i: (0,) * len(shp))
    prev = pl.BlockSpec((hb, 2048), lambda i: (jnp.maximum(i * (tm // hb) - 1, 0), 0))
    nxt = pl.BlockSpec((hb, 2048), lambda i: (jnp.minimum((i + 1) * (tm // hb), s // hb - 1), 0))
    cls = lambda d, dt: jax.ShapeDtypeStruct((d, s // d, 512), dt)
    cspecs = [_class_spec(d, 512, tm) for d in DILATIONS]
    return pl.pallas_call(
        body, name="combine", grid=(nt,),
        out_shape=(jax.ShapeDtypeStruct((s, 1024), BF16), jax.ShapeDtypeStruct((s, 1024), F32),
                   cls(1, F32), cls(1, BF16), jax.ShapeDtypeStruct((s, 512), BF16),
                   jax.ShapeDtypeStruct((s, 1024), BF16), jax.ShapeDtypeStruct((s, 512), BF16),
                   jax.ShapeDtypeStruct((1, 128), F32), jax.ShapeDtypeStruct((1, 512), F32),
                   jax.ShapeDtypeStruct((8, 512), F32),
                   cls(4, BF16), cls(4, F32), cls(16, BF16), cls(16, F32)),
        in_specs=cspecs + cspecs + [row(2048), prev, nxt, row(512, 1), row(1024), row(1024),
                                    _resident((1024, 1024)), full((8, 512)), full((1, 512)), full((256, 256))],
        out_specs=(row(1024), row(1024), cspecs[0], cspecs[0], row(512), row(1024), row(512),
                   full((1, 128)), full((1, 512)), full((8, 512)),
                   cspecs[1], cspecs[1], cspecs[2], cspecs[2]),
        scratch_shapes=[pltpu.VMEM((4, tm, 128), F32), pltpu.VMEM((4, tm, 128), F32)],
        compiler_params=_params(("arbitrary",)),
    )(*o_g, *lse_g, cg, cg, cg, vz, x, tgt, wout, cw, cb, b256)


def _attn_bwd(qkn_l, v_l, do_l, ld_l, bias, gi, name):
    r_cls, length, _ = qkn_l.shape
    qt = min(length, 2048 if length <= 4096 else 1024)
    nb, nsub, nt = length // QB, qt // QB, length // qt
    chunk = min(length, 4096)
    nchunk = length // chunk

    def body(q_ref, k_ref, v_ref, do_ref, ld_ref, b_ref, dq_ref, dkv_hbm, dsum_ref, dk_acc, dv_acc, stage, sems):
        p_id, r, t = pl.program_id(0), pl.program_id(1), pl.program_id(2)
        lo = lax.broadcasted_iota(jnp.int32, (QB, 128), 1) < HEAD_DIM

        @pl.when(t == 0)
        def _():
            dk_acc[...] = jnp.zeros_like(dk_acc)
            dv_acc[...] = jnp.zeros_like(dv_acc)

        @pl.when((t == 0) & (r == 0))
        def _():
            dsum_ref[...] = jnp.zeros_like(dsum_ref)

        nt_dims = (((1,), (1,)), ((), ()))
        tn_dims = (((0,), (0,)), ((), ()))
        coords, qqs, dds, logits, dps, lcols, dcols = [], [], [], [], [], [], []
        for i in range(nsub):
            q0, start, variant = _block_coords(t, i, nsub, nb, length)
            qq = _split_heads(q_ref[q0:q0 + QB, :], lo)
            dd = _split_heads(do_ref[q0:q0 + QB, :], lo)
            k = k_ref[pl.ds(start, KB), :]
            v = v_ref[pl.ds(start, KB), :]
            logits.append(lax.dot_general(qq, k, nt_dims, preferred_element_type=F32) + b_ref[variant])
            dps.append(lax.dot_general(dd, v, nt_dims, preferred_element_type=F32))
            lcols.append(_col_pair(ld_ref, q0, 0))
            dcols.append(_col_pair(ld_ref, q0, HEAD_DIM // 2))
            coords.append((q0, start, variant))
            qqs.append(qq)
            dds.append(dd)
        p = jnp.exp(jnp.concatenate(logits, axis=0) - jnp.concatenate(lcols, axis=0))
        ds = p * (jnp.concatenate(dps, axis=0) - jnp.concatenate(dcols, axis=0))
        pb = p.astype(BF16)
        dsb = ds.astype(BF16)
        middle = None
        for i in range(nsub):
            q0, start, variant = coords[i]
            rows = slice(2 * QB * i, 2 * QB * (i + 1))
            if 0 < i < nsub - 1:
                middle = ds[rows] if middle is None else middle + ds[rows]
            else:
                dsum_ref[variant] += ds[rows]
            dqq = jnp.dot(dsb[rows], k_ref[pl.ds(start, KB), :], preferred_element_type=F32)
            dq_ref[q0:q0 + QB, :] = jnp.where(lo, dqq[0:QB], dqq[QB:2 * QB]).astype(BF16)
            dk_acc[pl.ds(start, KB), :] += lax.dot_general(dsb[rows], qqs[i], tn_dims, preferred_element_type=F32)
            dv_acc[pl.ds(start, KB), :] += lax.dot_general(pb[rows], dds[i], tn_dims, preferred_element_type=F32)
        if middle is not None:
            dsum_ref[1] += middle

        @pl.when(t == nt - 1)
        def _():
            def copy(k):
                which, c = k // nchunk, k % nchunk
                rows = pl.ds(c * chunk, chunk)
                return pltpu.make_async_copy(stage.at[k % 2], dkv_hbm.at[r, p_id, which, rows, :], sems.at[k % 2])

            for k in range(2 * nchunk):
                if k < 2:
                    @pl.when((p_id > 0) | (r > 0))
                    def _():
                        copy(k).wait()
                else:
                    copy(k).wait()
                acc = (dk_acc, dv_acc)[k // nchunk]
                stage[k % 2] = acc[pl.ds((k % nchunk) * chunk, chunk), :].astype(BF16)
                copy(k).start()

            @pl.when((p_id == N_PAIR - 1) & (r == r_cls - 1))
            def _():
                copy(0).wait()
                copy(1).wait()

    qspec = pl.BlockSpec((None, qt, 128), lambda p, r, t: (r, t, p))
    return pl.pallas_call(
        body, name=name, grid=(N_PAIR, r_cls, nt),
        out_shape=(jax.ShapeDtypeStruct((r_cls, length, 512), BF16),
                   jax.ShapeDtypeStruct((r_cls, N_PAIR, 2, length, 128), BF16),
                   jax.ShapeDtypeStruct((N_PAIR, 3, 2 * QB, KB), F32)),
        in_specs=[qspec,
                  pl.BlockSpec((None, length, 128), lambda p, r, t: (r, 0, 4 + p)),
                  pl.BlockSpec((None, length, 128), lambda p, r, t: (r, 0, p)),
                  qspec, qspec,
                  pl.BlockSpec((None, 3, None, 2 * QB, KB), lambda p, r, t: (gi, 0, p, 0, 0))],
        out_specs=(qspec, pl.BlockSpec(memory_space=pl.ANY),
                   pl.BlockSpec((None, 3, 2 * QB, KB), lambda p, r, t: (p, 0, 0, 0))),
        scratch_shapes=[pltpu.VMEM((length, 128), F32), pltpu.VMEM((length, 128), F32),
                        pltpu.VMEM((2, chunk, 128), BF16), pltpu.SemaphoreType.DMA((2,))],
        compiler_params=_params(("arbitrary", "arbitrary", "arbitrary")),
    )(qkn_l, qkn_l, v_l, do_l, ld_l, bias)


def _bwd_tail(dq_g, dkv_g, qkr, qkw, dza, dgbz, dzc, cg, cw, wblk, x, norm_w, dout, b256):
    s = x.shape[0]
    tm = TM_COMBINE
    hb = 16
    nt = s // tm

    def body(dq1, dq4, dq16, dkv1, dkv4, dkv16, qkr_ref, qkw_ref, dza_ref, dgbz_ref, dzc_ref,
             dzp_ref, dzn_ref, u_ref, gc_ref, cw_ref, w_ref, x_ref, nw_ref, dout_ref, b_ref,
             gx_o, dproj_o, dnw_o, dqkw_o, slab, mid):
        i = pl.program_id(0)

        def nat_q(ref, d):
            return _gather_classes(slab, lambda r, j: ref[r, :, 128 * j:128 * (j + 1)], 4, d, mid)

        def nat_kv(ref, d, which):
            return _gather_classes(slab, lambda r, j: ref[r, j, which], 4, d, mid)

        @pl.when(i == 0)
        def _():
            dnw_o[...] = jnp.zeros_like(dnw_o)
            dqkw_o[...] = jnp.zeros_like(dqkw_o)

        dzc = dzc_ref[...].astype(F32)
        d_prev = jnp.where(i == 0, 0.0, dzp_ref[hb - 1:hb, :].astype(F32))
        d_next = jnp.where(i == nt - 1, 0.0, dzn_ref[0:1, :].astype(F32))
        rows = lax.broadcasted_iota(jnp.int32, (tm, 512), 0)
        d_up = jnp.where(rows == 0, d_prev, pltpu.roll(dzc, 1, 0))
        d_dn = jnp.where(rows == tm - 1, d_next, pltpu.roll(dzc, tm - 1, 0))
        dt = cw_ref[0:1, :] * d_dn + cw_ref[1:2, :] * dzc + cw_ref[2:3, :] * d_up
        u = u_ref[...].astype(F32)
        gc = gc_ref[...].astype(F32)
        dproj_o[:, 0:512] = (dt * gc).astype(BF16)
        dproj_o[:, 512:1024] = dgbz_ref[:, 0:512]
        dproj_o[:, 1024:1536] = (dt * u).astype(BF16)
        dproj_o[:, 1536:2048] = dgbz_ref[:, 512:1024]

        dqn = (dq1[0].astype(F32) + nat_q(dq4, 4) + nat_q(dq16, 16)) * (1.0 / 8.0)
        dk1 = jnp.concatenate([dkv1[0, j, 0] for j in range(N_PAIR)], axis=1)
        dv1 = jnp.concatenate([dkv1[0, j, 1] for j in range(N_PAIR)], axis=1)
        dkn = dk1 + nat_kv(dkv4, 4, 0) + nat_kv(dkv16, 16, 0)
        dvn = dv1 + nat_kv(dkv4, 4, 1) + nat_kv(dkv16, 16, 1)
        g = jnp.concatenate([dqn, dkn], axis=1) * qkw_ref[...]
        raw = qkr_ref[...].astype(F32)
        rr = lax.rsqrt(_group_sum(raw * raw, b_ref, split=False) * (1.0 / HEAD_DIM) + EPS)
        proj_gq = _group_sum(g * raw, b_ref) * (1.0 / HEAD_DIM)
        draw = rr * g - raw * (rr * rr * rr) * proj_gq
        dqkw_o[...] += jnp.sum(jnp.concatenate([dqn, dkn], axis=1) * raw * rr, axis=0, keepdims=True)
        dproj_o[:, 2048:3072] = draw.astype(BF16)
        dproj_o[:, 3072:3584] = dvn.astype(BF16)
        dproj_o[:, 3584:4096] = dza_ref[...]

        nt_dims = (((1,), (1,)), ((), ()))
        dh = lax.dot_general(dproj_o[:, 0:1024], w_ref[0], nt_dims, preferred_element_type=F32)
        for b in range(1, 4):
            dh += lax.dot_general(dproj_o[:, 1024 * b:1024 * b + 1024], w_ref[b], nt_dims,
                                  preferred_element_type=F32)

        xf = x_ref[...]
        r = lax.rsqrt(jnp.mean(xf * xf, axis=-1, keepdims=True) + EPS)
        gh = dh * nw_ref[...]
        dnw_o[...] += jnp.sum(dh * xf * r, axis=0, keepdims=True)
        mean_gx = jnp.mean(gh * xf, axis=-1, keepdims=True)
        gx_o[...] = dout_ref[...] + r * gh - xf * (r * r * r) * mean_gx

    row = lambda w, j=0: pl.BlockSpec((tm, w), lambda i: (i, j))
    full = lambda shp: pl.BlockSpec(shp, lambda i: (0,) * len(shp))
    prev = pl.BlockSpec((hb, 512), lambda i: (jnp.maximum(i * (tm // hb) - 1, 0), 0))
    nxt = pl.BlockSpec((hb, 512), lambda i: (jnp.minimum((i + 1) * (tm // hb), s // hb - 1), 0))
    return pl.pallas_call(
        body, name="bwd_tail", grid=(nt,),
        out_shape=(jax.ShapeDtypeStruct((s, 1024), F32), jax.ShapeDtypeStruct((s, 4096), BF16),
                   jax.ShapeDtypeStruct((1, 1024), F32), jax.ShapeDtypeStruct((1, 1024), F32)),
        in_specs=[_class_spec(d, 512, tm) for d in DILATIONS]
        + [pl.BlockSpec((d, N_PAIR, 2, tm // d, 128), lambda i: (0, 0, 0, i, 0)) for d in DILATIONS]
        + [row(1024), full((1, 1024)), row(512), row(1024), row(512), prev, nxt,
           row(512, 0), row(512, 2), full((8, 512)), _resident((4, 1024, 1024)), row(1024),
           full((1, 1024)), row(1024), full((256, 256))],
        out_specs=(row(1024), row(4096), full((1, 1024)), full((1, 1024))),
        scratch_shapes=[pltpu.VMEM((4, tm, 128), F32), pltpu.VMEM((4, tm, 128), F32)],
        compiler_params=_params(("arbitrary",)),
    )(*dq_g, *dkv_g, qkr, qkw, dza, dgbz, dzc, dzc, dzc, cg, cg, cw, wblk, x, norm_w, dout, b256)


def _wgrad(a, b, row_blocked, name):
    s, m = a.shape
    n = b.shape[1]
    tk = 1024
    ncol = min(n, 2048)
    nj, nk = n // ncol, s // tk

    def body(a_ref, b_ref, o_ref, acc):
        kk = pl.program_id(1)

        @pl.when(kk == 0)
        def _():
            acc[...] = jnp.zeros_like(acc)

        acc[...] += lax.dot_general(a_ref[...], b_ref[...].astype(BF16), (((0,), (0,)), ((), ())),
                                    preferred_element_type=F32)

        @pl.when(kk == nk - 1)
        def _():
            blocks, _, rows, _ = o_ref.shape
            for blk in range(blocks):
                for half in range(2):
                    if row_blocked:
                        r0 = (2 * blk + half) * rows
                        o_ref[blk, half] = acc[r0:r0 + rows, :].astype(BF16)
                    else:
                        o_ref[blk, half] = acc[half * rows:(half + 1) * rows,
                                               1024 * blk:1024 * (blk + 1)].astype(BF16)

    if row_blocked:
        out_shape = jax.ShapeDtypeStruct((4, 2, m // 8, 1024), BF16)
        out_spec = pl.BlockSpec((4, 2, m // 8, 1024), lambda j, k: (0, 0, 0, 0))
    else:
        out_shape = jax.ShapeDtypeStruct((n // 1024, 2, m // 2, 1024), BF16)
        out_spec = pl.BlockSpec((ncol // 1024, 2, m // 2, 1024), lambda j, k: (j, 0, 0, 0))
    return pl.pallas_call(
        body, name=name, grid=(nj, nk),
        out_shape=out_shape,
        in_specs=[pl.BlockSpec((tk, m), lambda j, k: (k, 0)), pl.BlockSpec((tk, ncol), lambda j, k: (k, j))],
        out_specs=out_spec,
        scratch_shapes=[pltpu.VMEM((m, ncol), F32)],
        compiler_params=_params(("parallel", "arbitrary")),
    )(a, b)


def _dbias(dsums, onehot_all):
    def body(ds1_ref, ds4_ref, ds16_ref, oh_ref, o_ref):
        @pl.when(pl.program_id(0) == 0)
        def _():
            o_ref[...] = jnp.zeros_like(o_ref)

        hrow = lax.broadcasted_iota(jnp.int32, (8, KB), 0)
        flip = (lax.broadcasted_iota(jnp.int32, (QB, QB), 0)
                + lax.broadcasted_iota(jnp.int32, (QB, QB), 1) == QB - 1).astype(F32)

        def diagonal_sums(tile):
            rev = jnp.dot(flip, tile, preferred_element_type=F32, precision=lax.Precision.HIGHEST)
            sums = jnp.sum(pltpu.roll(rev, 0, 1, stride=1, stride_axis=0), axis=0, keepdims=True)
            return pltpu.roll(sums, KB - (QB - 1), 1)

        for g, ds_ref in enumerate((ds1_ref, ds4_ref, ds16_ref)):
            diag = jnp.zeros((8, KB), F32)
            for p in range(N_PAIR):
                diag = jnp.where(hrow == 2 * p, diagonal_sums(ds_ref[p, 0:QB, :]), diag)
                diag = jnp.where(hrow == 2 * p + 1, diagonal_sums(ds_ref[p, QB:2 * QB, :]), diag)
            o_ref[...] += jnp.dot(diag, oh_ref[g], preferred_element_type=F32, precision=lax.Precision.HIGHEST)

    ds_spec = pl.BlockSpec((N_PAIR, None, 2 * QB, KB), lambda v: (0, v, 0, 0))
    return pl.pallas_call(
        body, name="dbias", grid=(3,),
        out_shape=jax.ShapeDtypeStruct((8, 128), F32),
        in_specs=[ds_spec, ds_spec, ds_spec, pl.BlockSpec((3, None, KB, 128), lambda v: (0, v, 0, 0))],
        out_specs=pl.BlockSpec((8, 128), lambda v: (0, 0)),
        compiler_params=_params(("arbitrary",)),
    )(*dsums, onehot_all)


def _gsync(pw_in, pw_out, small):
    hin, hout = pw_in.shape[2], pw_out.shape[2]
    nsmall = small.shape[0]

    def body(pin_hbm, pout_hbm, small_ref, gin_o, gout_o, small_o,
             mine_in, recv_in, sbuf_in, rbuf_in, mine_out, recv_out, sbuf_out, rbuf_out, gather,
             lsem, asend, arecv, bsend, brecv, csend, crecv, ssend, srecv):
        x, y, c = lax.axis_index("x"), lax.axis_index("y"), lax.axis_index("c")
        b = 2 * x + y
        dev = 4 * x + 2 * y + c
        sib = (x, y, 1 - c)

        def rcopy(src, dst, ssem, rsem, to):
            return pltpu.make_async_remote_copy(src_ref=src, dst_ref=dst, send_sem=ssem, recv_sem=rsem,
                                                device_id=to, device_id_type=MESH)

        gather[dev] = small_ref[...]
        s_sends = []
        for k in range(1, 8):
            to = (x ^ (k >> 2), y ^ ((k >> 1) & 1), c ^ (k & 1))
            cp = rcopy(gather.at[dev], gather.at[dev], ssend.at[k - 1], srecv.at[k - 1], to)
            cp.start()
            s_sends.append(cp)

        a_in = rcopy(pin_hbm.at[:, 1 - c], recv_in, asend.at[0], arecv.at[0], sib)
        a_out = rcopy(pout_hbm.at[:, 1 - c], recv_out, asend.at[1], arecv.at[1], sib)
        a_in.start()
        a_out.start()
        l_in = pltpu.make_async_copy(pin_hbm.at[:, c], mine_in, lsem.at[0])
        l_out = pltpu.make_async_copy(pout_hbm.at[:, c], mine_out, lsem.at[1])
        l_in.start()
        l_out.start()
        l_in.wait()
        l_out.wait()

        def stage_b(a_cp, mine, recv, sbuf, rbuf, base):
            a_cp.wait_recv()
            sends = []
            for k in (1, 2, 3):
                bk = b ^ k
                sbuf[k - 1] = (mine[bk].astype(F32) + recv[bk].astype(F32)).astype(BF16)
                cp = rcopy(sbuf.at[k - 1], rbuf.at[k - 1], bsend.at[base + k - 1], brecv.at[base + k - 1],
                           (x ^ (k >> 1), y ^ (k & 1), c))
                cp.start()
                sends.append(cp)
            return sends

        b_in = stage_b(a_in, mine_in, recv_in, sbuf_in, rbuf_in, 0)
        b_out = stage_b(a_out, mine_out, recv_out, sbuf_out, rbuf_out, 3)

        def stage_c(b_sends, mine, recv, rbuf, g_o, half, idx):
            acc = mine[b].astype(F32) + recv[b].astype(F32)
            for k in (1, 2, 3):
                b_sends[k - 1].wait_recv()
                acc = acc + rbuf[k - 1].astype(F32)
            rows = g_o.at[pl.ds(pl.multiple_of(c * half, half), half), :]
            g_o[pl.ds(pl.multiple_of(c * half, half), half), :] = acc
            cp = rcopy(rows, rows, csend.at[idx], crecv.at[idx], sib)
            cp.start()
            return cp

        c_in = stage_c(b_in, mine_in, recv_in, rbuf_in, gin_o, hin, 0)
        c_out = stage_c(b_out, mine_out, recv_out, rbuf_out, gout_o, hout, 1)

        for cp in s_sends:
            cp.wait_recv()
        tot = gather[0]
        for d in range(1, 8):
            tot = tot + gather[d]
        small_o[...] = tot

        for g_o, half, idx in ((gin_o, hin, 0), (gout_o, hout, 1)):
            other = g_o.at[pl.ds(pl.multiple_of((1 - c) * half, half), half), :]
            rcopy(other, other, csend.at[idx], crecv.at[idx], sib).wait_recv()
        for cp in s_sends + [a_in, a_out] + b_in + b_out + [c_in, c_out]:
            cp.wait_send()

    vm = pl.BlockSpec(memory_space=pltpu.VMEM)
    hbm = pl.BlockSpec(memory_space=pl.ANY)
    return pl.pallas_call(
        body, name="gsync",
        out_shape=(jax.ShapeDtypeStruct((2 * hin, 1024), F32), jax.ShapeDtypeStruct((2 * hout, 1024), F32),
                   jax.ShapeDtypeStruct((nsmall, 128), F32)),
        in_specs=[hbm, hbm, vm], out_specs=(vm, vm, vm),
        scratch_shapes=[pltpu.VMEM((4, hin, 1024), BF16), pltpu.VMEM((4, hin, 1024), BF16),
                        pltpu.VMEM((3, hin, 1024), BF16), pltpu.VMEM((3, hin, 1024), BF16),
                        pltpu.VMEM((4, hout, 1024), BF16), pltpu.VMEM((4, hout, 1024), BF16),
                        pltpu.VMEM((3, hout, 1024), BF16), pltpu.VMEM((3, hout, 1024), BF16),
                        pltpu.VMEM((8, nsmall, 128), F32),
                        pltpu.SemaphoreType.DMA((2,)),
                        pltpu.SemaphoreType.DMA((2,)), pltpu.SemaphoreType.DMA((2,)),
                        pltpu.SemaphoreType.DMA((6,)), pltpu.SemaphoreType.DMA((6,)),
                        pltpu.SemaphoreType.DMA((2,)), pltpu.SemaphoreType.DMA((2,)),
                        pltpu.SemaphoreType.DMA((7,)), pltpu.SemaphoreType.DMA((7,))],
        compiler_params=_params(),
    )(pw_in, pw_out, small)


def _adamw_math(w, g, m, v):
    m = ADAM_B1 * m + (1.0 - ADAM_B1) * g
    v = ADAM_B2 * v + (1.0 - ADAM_B2) * (g * g)
    m_hat = m / (1.0 - ADAM_B1 ** ADAM_STEP)
    v_hat = v / (1.0 - ADAM_B2 ** ADAM_STEP)
    delta = -ADAM_LR * (m_hat / (jnp.sqrt(v_hat) + ADAM_EPS) + ADAM_WD * w)
    return delta, m, v


def _adamw(w, g, m, v, name):
    rows, cols = w.shape
    tr = 256 if rows % 256 == 0 else rows

    def body(w_ref, g_ref, m_ref, v_ref, d_o, m_o, v_o):
        d, m2, v2 = _adamw_math(w_ref[...], g_ref[...], m_ref[...], v_ref[...])
        d_o[...] = d
        m_o[...] = m2
        v_o[...] = v2

    spec = pl.BlockSpec((tr, cols), lambda i: (i, 0))
    shp = jax.ShapeDtypeStruct((rows, cols), F32)
    return pl.pallas_call(
        body, name=name, grid=(rows // tr,), out_shape=(shp, shp, shp),
        in_specs=[spec] * 4, out_specs=(spec, spec, spec),
        compiler_params=_params(("parallel",)),
    )(w, g, m, v)


def _fold_heads(dqkw):
    def body(x_ref, o_ref):
        xs = x_ref[...]
        sq = xs[0:1] + xs[1:2] + xs[2:3] + xs[3:4]
        sk = xs[4:5] + xs[5:6] + xs[6:7] + xs[7:8]
        both = jnp.concatenate([sq, sk], axis=0)
        o_ref[...] = both + pltpu.roll(both, HEAD_DIM, 1)

    vm = pl.BlockSpec(memory_space=pltpu.VMEM)
    return pl.pallas_call(body, name="fold_heads", out_shape=jax.ShapeDtypeStruct((2, 128), F32),
                          in_specs=[vm], out_specs=vm, compiler_params=_params())(dqkw)


def kernel(x, norm_w, w_in, conv_w, conv_b, q_norm_w, k_norm_w, rel_bias, w_out, loss_target, m_norm_w, m_w_in, m_conv_w, m_conv_b, m_q_norm_w, m_k_norm_w, m_rel_bias, m_w_out, v_norm_w, v_w_in, v_conv_w, v_conv_b, v_q_norm_w, v_k_norm_w, v_rel_bias, v_w_out):
    x2 = x[0]
    tgt = loss_target[0]
    blk = 2 * lax.axis_index("x") + lax.axis_index("y")

    conv_w8 = jnp.pad(conv_w, ((0, 5), (0, 0)))
    wblk, woutblk, cwblk = _wgather(w_in, w_out, conv_w8)
    wout_full = woutblk.reshape(1024, 1024)
    cw_full = cwblk.transpose(1, 0, 2).reshape(8, 512)

    qkw = jnp.concatenate([jnp.tile(q_norm_w, 8) * 0.125, jnp.tile(k_norm_w, 8)])[None, :]
    qkw_raw = jnp.concatenate([jnp.tile(q_norm_w, 8), jnp.tile(k_norm_w, 8)])[None, :]
    gidx = jnp.arange(256) // HEAD_DIM
    b256 = (gidx[:, None] == gidx[None, :]).astype(BF16)

    h, cg, qkr, qkn, vz, qkn4, v4, qkn16, v16 = _proj(x2, norm_w[None, :], wblk, qkw, b256)

    biases = _bias_tables(rel_bias)
    qkn_l = [qkn[None], qkn4, qkn16]
    v_l = [vz[None], v4, v16]
    o_g, lse_g = [], []
    for gi, d in enumerate(DILATIONS):
        o_l, lse_l = _attn_fwd(qkn_l[gi], v_l[gi], biases, gi, f"attn_fwd_d{d}")
        o_g.append(o_l)
        lse_g.append(lse_l)

    (y, dout, ld1, do1, dza, dgbz, dzc, loss_p, dcb, dcw, do4, ld4, do16, ld16) = _combine(
        o_g, lse_g, cg, vz, x2, tgt, wout_full, cw_full, conv_b[None, :], b256)

    dq_g, dkv_g, dsums = [], [], []
    for gi, (d, do_l, ld_l) in enumerate(zip(DILATIONS, (do1, do4, do16), (ld1, ld4, ld16))):
        dq_l, dkv_l, dsum = _attn_bwd(qkn_l[gi], v_l[gi], do_l, ld_l, biases, gi, f"attn_bwd_d{d}")
        dq_g.append(dq_l)
        dkv_g.append(dkv_l)
        dsums.append(dsum)

    grad_x, dproj, dnw, dqkw = _bwd_tail(dq_g, dkv_g, qkr, qkw_raw, dza, dgbz, dzc, cg, cw_full, wblk,
                                         x2, norm_w[None, :], dout, b256)

    pw_in = _wgrad(h, dproj, False, "wgrad_in")
    pw_out = _wgrad(y, dout, True, "wgrad_out")
    dbias8 = _dbias(dsums, jnp.stack([_diag_bucket_onehot(d) for d in DILATIONS], axis=0))

    small = jnp.concatenate([dnw.reshape(8, 128), dcb.reshape(4, 128), dqkw.reshape(8, 128),
                             dcw[0:3].reshape(12, 128), dbias8, jnp.pad(loss_p, ((0, 7), (0, 0)))], axis=0)
    g_win, g_wout, gsmall = _gsync(pw_in, pw_out, small)

    g_nw = gsmall[0:8].reshape(1024)
    g_cb = gsmall[8:12].reshape(512)
    folded = _fold_heads(gsmall[12:20])
    g_qw, g_kw = folded[0, 0:64], folded[1, 0:64]
    g_cw = lax.dynamic_slice(gsmall[20:32].reshape(3, 512), (0, blk * 128), (3, 128))
    g_rb = gsmall[32:40][:, 0:32].T
    loss = gsmall[40, 0]

    d_win, nm_win, nv_win = _adamw(w_in, g_win, m_w_in, v_w_in, "adamw_w_in")
    d_wout, nm_wout, nv_wout = _adamw(w_out, g_wout, m_w_out, v_w_out, "adamw_w_out")

    def pack(parts):
        rows = [parts[0].reshape(8, 128), parts[1].reshape(4, 128),
                jnp.pad(parts[2], (0, 64))[None, :], jnp.pad(parts[3], (0, 64))[None, :],
                parts[4], jnp.pad(parts[5].T, ((0, 0), (0, 96)))]
        return jnp.concatenate(rows, axis=0)

    ws = pack([norm_w, conv_b, q_norm_w, k_norm_w, conv_w, rel_bias])
    gs = pack([g_nw, g_cb, g_qw, g_kw, g_cw, g_rb])
    ms = pack([m_norm_w, m_conv_b, m_q_norm_w, m_k_norm_w, m_conv_w, m_rel_bias])
    vs = pack([v_norm_w, v_conv_b, v_q_norm_w, v_k_norm_w, v_conv_w, v_rel_bias])
    rpad = lambda a: jnp.pad(a, ((0, 7), (0, 0)))
    d_s, nm_s, nv_s = _adamw(rpad(ws), rpad(gs), rpad(ms), rpad(vs), "adamw_small")

    def unpack(a):
        return (a[0:8].reshape(1024), a[12:13, 0:64].reshape(64), a[13:14, 0:64].reshape(64),
                a[14:17], a[8:12].reshape(512), a[17:25, 0:32].T)

    def ordered(nw, win, cw, cb, qw, kw, rb, wout):
        return (nw, win, cw, cb, qw, kw, rb, wout)

    g_un = (g_nw, g_qw, g_kw, g_cw, g_cb, g_rb)
    outs = [loss, grad_x[None]]
    for un, win_v, wout_v in ((g_un, g_win, g_wout), (unpack(d_s), d_win, d_wout),
                              (unpack(nm_s), nm_win, nm_wout), (unpack(nv_s), nv_win, nv_wout)):
        nw, qw, kw, cw, cb, rb = un
        outs.extend(ordered(nw, win_v, cw, cb, qw, kw, rb, wout_v))
    return tuple(outs)
```

```python
import math

import jax
import jax.numpy as jnp
from jax import lax
from jax.experimental import pallas as pl
from jax.experimental.pallas import tpu as pltpu

F32 = jnp.float32
BF16 = jnp.bfloat16
MESH = pl.DeviceIdType.MESH

D_MODEL = 1024
CONV_W = 512
ATTN_W = 512
HEAD_DIM = 64
N_PAIR = 4
DILATIONS = (1, 4, 16)
HALF = 64
QB = 128
KB = QB + 2 * HALF
NUM_BUCKETS = 32
MAX_DISTANCE = 1024
EPS = 1e-6
NEG = -1e30
ADAM_LR, ADAM_B1, ADAM_B2, ADAM_EPS, ADAM_WD, ADAM_STEP = 0.001, 0.9, 0.999, 1e-08, 0.01, 10
VMEM_LIMIT = 48 << 20


def _params(sem=None, vmem=VMEM_LIMIT, **kw):
    if sem is not None:
        kw["dimension_semantics"] = sem
    return pltpu.CompilerParams(vmem_limit_bytes=vmem, **kw)


def _sigmoid(z):
    return 1.0 / (1.0 + jnp.exp(-z))


def _group_sum(val, b_ref, split=True):
    hi = val.astype(BF16)
    lo = (val - hi.astype(F32)).astype(BF16) if split else None
    outs = []
    for j in range(val.shape[1] // 256):
        sl = slice(256 * j, 256 * j + 256)
        part = jnp.dot(hi[:, sl], b_ref[...], preferred_element_type=F32)
        if split:
            part = part + jnp.dot(lo[:, sl], b_ref[...], preferred_element_type=F32)
        outs.append(part)
    return outs[0] if len(outs) == 1 else jnp.concatenate(outs, axis=1)


def _t5_bucket(rel):
    half_b = NUM_BUCKETS // 2
    max_exact = half_b // 2
    ret = jnp.where(rel > 0, half_b, 0)
    n = jnp.abs(rel)
    nf = jnp.maximum(n, 1).astype(F32)
    large = max_exact + (jnp.log(nf / max_exact) / math.log(MAX_DISTANCE / max_exact)
                         * (half_b - max_exact)).astype(jnp.int32)
    large = jnp.minimum(large, half_b - 1)
    return ret + jnp.where(n < max_exact, n, large)


def _bias_tables(rel_bias):
    rows = []
    key = jnp.arange(KB)
    for dilation in DILATIONS:
        for variant in range(3):
            off = (0, HALF, 2 * HALF)[variant]
            rel = ((key - off + KB // 2) % KB) - KB // 2
            bkt = _t5_bucket(jnp.clip(rel, -HALF, HALF) * dilation)
            rows.append(jnp.where(jnp.abs(rel) <= HALF, bkt, -1))
    bkt_all = jnp.broadcast_to(jnp.stack(rows, axis=0).astype(jnp.int32)[:, None, :], (9, 8, KB))

    def body(rb_ref, bkt_ref, o_ref):
        bkt = bkt_ref[...]
        off = (pl.program_id(0) % 3) * HALF
        rel = (lax.broadcasted_iota(jnp.int32, (QB, KB), 1) - lax.broadcasted_iota(jnp.int32, (QB, KB), 0)) - off
        band = jnp.abs(rel) <= HALF
        for h in range(8):
            acc = jnp.full((8, KB), NEG, F32)
            for b in range(NUM_BUCKETS):
                acc = jnp.where(bkt == b, rb_ref[b, h], acc)
            rolled = pltpu.roll(jnp.broadcast_to(acc[0:1], (QB, KB)), 0, 1, stride=1, stride_axis=0)
            o_ref[h] = jnp.where(band, rolled, NEG)

    out = pl.pallas_call(
        body, name="bias_tables", grid=(9,),
        out_shape=jax.ShapeDtypeStruct((9, 8, QB, KB), F32),
        in_specs=[pl.BlockSpec(memory_space=pltpu.SMEM), pl.BlockSpec((None, 8, KB), lambda i: (i, 0, 0))],
        out_specs=pl.BlockSpec((None, 8, QB, KB), lambda i: (i, 0, 0, 0)),
        compiler_params=_params(("parallel",)),
    )(rel_bias, bkt_all)
    return out.reshape(3, 3, N_PAIR, 2 * QB, KB)


def _diag_bucket_onehot(dilation):
    out = []
    c = jnp.arange(KB)
    for variant in range(3):
        off = (0, HALF, 2 * HALF)[variant]
        rel = ((c - off + 128) % 256) - 128
        band = jnp.abs(rel) <= HALF
        bkt = _t5_bucket(jnp.clip(rel, -HALF, HALF) * dilation)
        oh = (bkt[:, None] == jnp.arange(128)[None, :]) & band[:, None]
        out.append(oh.astype(F32))
    return jnp.stack(out, axis=0)


def _wgather(w_in, w_out, conv_w):
    rin, rout = w_in.shape[0] // 2, w_out.shape[0] // 2

    def body(win_ref, wout_ref, cw_ref, win_o, wout_o, cw_o, send_sems, recv_sems):
        x, y, c = lax.axis_index("x"), lax.axis_index("y"), lax.axis_index("c")
        b = 2 * x + y
        win_o[b] = win_ref[...].astype(BF16)
        wout_o[b] = wout_ref[...].astype(BF16)
        cw_o[b] = cw_ref[...]

        def peer(k):
            return (x ^ (k >> 1), y ^ (k & 1))

        def piece(ref, blk, half_rows, core):
            return ref.at[blk, pl.ds(core * half_rows, half_rows), :]

        def copy(sem, src, dst, to):
            return pltpu.make_async_remote_copy(src_ref=src, dst_ref=dst, send_sem=send_sems.at[sem],
                                                recv_sem=recv_sems.at[sem], device_id=to, device_id_type=MESH)

        sends = []
        for k in (1, 2, 3):
            px, py = peer(k)
            sends.append(copy(k - 1, piece(win_o, b, rin, c), piece(win_o, b, rin, c), (px, py, c)))
            sends.append(copy(3 + k - 1, piece(wout_o, b, rout, c), piece(wout_o, b, rout, c), (px, py, c)))
            sends.append(copy(6 + k - 1, cw_o.at[b], cw_o.at[b], (px, py, c)))
        for cp in sends:
            cp.start()
        fwd = []
        for k in (1, 2, 3):
            px, py = peer(k)
            bk = 2 * px + py
            copy(k - 1, piece(win_o, bk, rin, c), piece(win_o, bk, rin, c), (px, py, c)).wait_recv()
            f = copy(9 + k - 1, piece(win_o, bk, rin, c), piece(win_o, bk, rin, c), (x, y, 1 - c))
            f.start()
            fwd.append(f)
            copy(3 + k - 1, piece(wout_o, bk, rout, c), piece(wout_o, bk, rout, c), (px, py, c)).wait_recv()
            f = copy(12 + k - 1, piece(wout_o, bk, rout, c), piece(wout_o, bk, rout, c), (x, y, 1 - c))
            f.start()
            fwd.append(f)
            copy(6 + k - 1, cw_o.at[bk], cw_o.at[bk], (px, py, c)).wait_recv()
        for k in (1, 2, 3):
            px, py = peer(k)
            bk = 2 * px + py
            copy(9 + k - 1, piece(win_o, bk, rin, 1 - c), piece(win_o, bk, rin, 1 - c), (x, y, 1 - c)).wait_recv()
            copy(12 + k - 1, piece(wout_o, bk, rout, 1 - c), piece(wout_o, bk, rout, 1 - c), (x, y, 1 - c)).wait_recv()
        for cp in sends + fwd:
            cp.wait_send()

    vm = pl.BlockSpec(memory_space=pltpu.VMEM)
    return pl.pallas_call(
        body, name="wgather",
        out_shape=(jax.ShapeDtypeStruct((4,) + w_in.shape, BF16),
                   jax.ShapeDtypeStruct((4,) + w_out.shape, BF16),
                   jax.ShapeDtypeStruct((4,) + conv_w.shape, F32)),
        in_specs=[vm, vm, vm], out_specs=(vm, vm, vm),
        scratch_shapes=[pltpu.SemaphoreType.DMA((15,)), pltpu.SemaphoreType.DMA((15,))],
        compiler_params=_params(),
    )(w_in, w_out, conv_w)


TM_MATMUL = 512
TM_COMBINE = 256


def _resident(shape):
    return pl.BlockSpec(shape, lambda i: (0,) * len(shape), pipeline_mode=pl.Buffered(1))


def _to_slabs(slab, val, j0=0):
    for j in range(val.shape[1] // 128):
        slab[j0 + j] = val[:, 128 * j:128 * (j + 1)]


def _scatter_classes(slab, j0, nj, out_ref, d, part=0, mid=None):
    tm = slab.shape[1]
    n = tm // d
    if d == 4:
        for r in range(d):
            for j in range(nj):
                out_ref[r, part * n:(part + 1) * n, 128 * j:128 * (j + 1)] = (
                    slab[j0 + j, pl.ds(r, n, stride=d), :].astype(out_ref.dtype))
        return
    q = tm // 4
    for lo in range(4):
        for j in range(nj):
            mid[j0 + j, lo * q:(lo + 1) * q, :] = slab[j0 + j, pl.ds(lo, q, stride=4), :]
    for hi in range(4):
        for lo in range(4):
            for j in range(nj):
                out_ref[4 * hi + lo, part * n:(part + 1) * n, 128 * j:128 * (j + 1)] = (
                    mid[j0 + j, pl.ds(lo * q + hi, n, stride=4), :].astype(out_ref.dtype))


def _gather_classes(slab, piece, nj, d, mid=None):
    tm = slab.shape[1]
    n = tm // d
    if d == 4:
        for r in range(d):
            for j in range(nj):
                slab[j, pl.ds(r, n, stride=d), :] = piece(r, j).astype(F32)
    else:
        q = tm // 4
        for hi in range(4):
            for lo in range(4):
                for j in range(nj):
                    mid[j, pl.ds(lo * q + hi, n, stride=4), :] = piece(4 * hi + lo, j).astype(F32)
        for lo in range(4):
            for j in range(nj):
                slab[j, pl.ds(lo, q, stride=4), :] = mid[j, lo * q:(lo + 1) * q, :]
    return jnp.concatenate([slab[j] for j in range(nj)], axis=1)


def _class_spec(d, width, tm):
    return pl.BlockSpec((d, tm // d, width), lambda i: (0, i, 0))


def _proj(x, tgt, norm_w, wblk, qkw, b256):
    s = x.shape[0]
    tm = TM_MATMUL
    nparts = 2
    tp = tm // nparts

    def body(x_ref, t_ref, nw_ref, w_ref, qkw_ref, b_ref, h_o, cg_o, qkr_o, qkn_o, vz_o, qkn4_o, v4_o, qkn16_o,
             v16_o, xmt_o, slabs, mids):
        for part in range(nparts):
            rows = slice(part * tp, (part + 1) * tp)
            slab = slabs.at[part]
            xf = x_ref[rows, :]
            xmt_o[rows, :] = xf - t_ref[rows, :]
            r = lax.rsqrt(jnp.mean(xf * xf, axis=-1, keepdims=True) + EPS)
            h = (xf * r * nw_ref[...]).astype(BF16)
            h_o[rows, :] = h
            p2 = jnp.dot(h, w_ref[2], preferred_element_type=F32)
            qkr_o[rows, :] = p2.astype(BF16)
            ss = _group_sum(p2 * p2, b_ref, split=False)
            rr = lax.rsqrt(ss * (1.0 / HEAD_DIM) + EPS)
            qkn = p2 * rr * qkw_ref[...]
            qkn_o[rows, :] = qkn.astype(BF16)
            _to_slabs(slab, qkn)
            p3 = jnp.dot(h, w_ref[3], preferred_element_type=F32)
            vz_o[rows, :] = p3.astype(BF16)
            _to_slabs(slab, p3[:, 0:512], 8)
            cg_o[rows, 0:1024] = jnp.dot(h, w_ref[0], preferred_element_type=F32).astype(BF16)
            cg_o[rows, 1024:2048] = jnp.dot(h, w_ref[1], preferred_element_type=F32).astype(BF16)
            for d, q_o, v_o in ((4, qkn4_o, v4_o), (16, qkn16_o, v16_o)):
                _scatter_classes(slab, 0, 8, q_o, d, part, mids.at[part])
                _scatter_classes(slab, 8, 4, v_o, d, part, mids.at[part])

    row = lambda w: pl.BlockSpec((tm, w), lambda i: (i, 0))
    full = lambda shp: pl.BlockSpec(shp, lambda i: (0,) * len(shp))
    nat = lambda w: jax.ShapeDtypeStruct((s, w), BF16)
    cls = lambda d, w: jax.ShapeDtypeStruct((d, s // d, w), BF16)
    return pl.pallas_call(
        body, name="proj", grid=(s // tm,),
        out_shape=(nat(1024), nat(2048), nat(1024), nat(1024), nat(1024),
                   cls(4, 1024), cls(4, 512), cls(16, 1024), cls(16, 512), jax.ShapeDtypeStruct((s, 1024), F32)),
        in_specs=[row(1024), row(1024), full((1, 1024)), _resident((4, 1024, 1024)), full((1, 1024)),
                  full((256, 256))],
        out_specs=(row(1024), row(2048), row(1024), row(1024), row(1024),
                   _class_spec(4, 1024, tm), _class_spec(4, 512, tm),
                   _class_spec(16, 1024, tm), _class_spec(16, 512, tm), row(1024)),
        scratch_shapes=[pltpu.VMEM((nparts, 12, tp, 128), F32), pltpu.VMEM((nparts, 12, tp, 128), F32)],
        compiler_params=_params(("parallel",)),
    )(x, tgt, norm_w, wblk, qkw, b256)


def _block_coords(t, i, nsub, nb, length):
    n = t * nsub + i
    q0 = i * QB
    start = pl.multiple_of(jnp.clip(n * QB - HALF, 0, length - KB), HALF)
    variant = jnp.where(n == 0, 0, jnp.where(n == nb - 1, 2, 1))
    return q0, start, variant


def _split_heads(a, lo):
    zero = jnp.zeros_like(a)
    return jnp.concatenate([jnp.where(lo, a, zero), jnp.where(lo, zero, a)], axis=0)


def _col_pair(ref, q0, lane):
    return jnp.concatenate([ref[pl.ds(q0, QB), lane:lane + 1],
                            ref[pl.ds(q0, QB), HEAD_DIM + lane:HEAD_DIM + lane + 1]], axis=0)


def _attn_fwd(qkn_l, v_l, bias, gi, name):
    r_cls, length, _ = qkn_l.shape
    qt = min(length, 2048)
    nb, nsub = length // QB, qt // QB

    def body(q_ref, k_ref, v_ref, b_ref, o_ref, lse_ref):
        t = pl.program_id(2)
        lo = lax.broadcasted_iota(jnp.int32, (QB, 128), 1) < HEAD_DIM

        starts, logits = [], []
        for i in range(nsub):
            _, start, variant = _block_coords(t, i, nsub, nb, length)
            qq = _split_heads(q_ref[i * QB:(i + 1) * QB, :], lo)
            k = k_ref[pl.ds(start, KB), :]
            logits.append(lax.dot_general(qq, k, (((1,), (1,)), ((), ())), preferred_element_type=F32)
                          + b_ref[variant])
            starts.append(start)
        lg = jnp.concatenate(logits, axis=0)
        m = jnp.max(lg, axis=-1, keepdims=True)
        p = jnp.exp(lg - m)
        pb = p.astype(BF16)
        l = jnp.sum(p, axis=-1, keepdims=True)
        lse = jnp.broadcast_to(m + jnp.log(l), (nsub * 2 * QB, 128))
        inv = 1.0 / l
        for i in range(nsub):
            rows = slice(2 * QB * i, 2 * QB * (i + 1))
            v = v_ref[pl.ds(starts[i], KB), :]
            pv = jnp.dot(pb[rows], v, preferred_element_type=F32) * inv[rows]
            o_ref[i * QB:(i + 1) * QB, :] = jnp.where(lo, pv[0:QB], pv[QB:2 * QB]).astype(BF16)
            ls = lse[rows]
            lse_ref[i * QB:(i + 1) * QB, :] = jnp.where(lo, ls[0:QB], ls[QB:2 * QB])

    return pl.pallas_call(
        body, name=name, grid=(N_PAIR, r_cls, length // qt),
        out_shape=(jax.ShapeDtypeStruct((r_cls, length, 512), BF16),
                   jax.ShapeDtypeStruct((r_cls, length, 512), F32)),
        in_specs=[pl.BlockSpec((None, qt, 128), lambda p, r, t: (r, t, p)),
                  pl.BlockSpec((None, length, 128), lambda p, r, t: (r, 0, 4 + p)),
                  pl.BlockSpec((None, length, 128), lambda p, r, t: (r, 0, p)),
                  pl.BlockSpec((None, 3, None, 2 * QB, KB), lambda p, r, t: (gi, 0, p, 0, 0))],
        out_specs=(pl.BlockSpec((None, qt, 128), lambda p, r, t: (r, t, p)),
                   pl.BlockSpec((None, qt, 128), lambda p, r, t: (r, t, p))),
        compiler_params=_params(("parallel", "parallel", "arbitrary")),
    )(qkn_l, qkn_l, v_l, bias)


def _combine(o_g, lse_g, cg, vz, xmt, wout, cw, cb, b256):
    s = xmt.shape[0]
    tm = TM_COMBINE
    hb = 16
    nt = s // tm

    def body(o1, o4, o16, l1, l4, l16, cg_ref, cgp_ref, cgn_ref, za_ref, xmt_ref, w_ref, cw_ref, cb_ref,
             b_ref, y_o, dout_o, ld1_o, do1_o, dza_o, dgbz_o, dzc_o, loss_o, dcb_o, dcw_o,
             do4_o, ld4_o, do16_o, ld16_o, slab, mid):
        i = pl.program_id(0)

        @pl.when(i == 0)
        def _():
            loss_o[...] = jnp.zeros_like(loss_o)
            dcb_o[...] = jnp.zeros_like(dcb_o)
            dcw_o[...] = jnp.zeros_like(dcw_o)

        u = cg_ref[:, 0:512].astype(F32)
        gb = cg_ref[:, 512:1024].astype(F32)
        gc = cg_ref[:, 1024:1536].astype(F32)
        zc = cg_ref[:, 1536:2048].astype(F32)
        tt = gc * u
        t_prev = cgp_ref[hb - 1:hb, 0:512].astype(F32) * cgp_ref[hb - 1:hb, 1024:1536].astype(F32)
        t_next = cgn_ref[0:1, 0:512].astype(F32) * cgn_ref[0:1, 1024:1536].astype(F32)
        t_prev = jnp.where(i == 0, 0.0, t_prev)
        t_next = jnp.where(i == nt - 1, 0.0, t_next)
        rows = lax.broadcasted_iota(jnp.int32, (tm, 512), 0)
        t_up = jnp.where(rows == 0, t_prev, pltpu.roll(tt, 1, 0))
        t_dn = jnp.where(rows == tm - 1, t_next, pltpu.roll(tt, tm - 1, 0))
        w0, w1, w2 = cw_ref[0:1, :], cw_ref[1:2, :], cw_ref[2:3, :]
        zb = w0 * t_up + w1 * tt + w2 * t_dn + cb_ref[...]
        sg = _sigmoid(zc)
        sz = zc * sg
        y_conv = gb * zb * sz

        a1, p1 = l1[0], o1[0].astype(F32)
        a4 = _gather_classes(slab, lambda r, j: l4[r, :, 128 * j:128 * (j + 1)], 4, 4)
        p4 = _gather_classes(slab, lambda r, j: o4[r, :, 128 * j:128 * (j + 1)], 4, 4)
        a16 = _gather_classes(slab, lambda r, j: l16[r, :, 128 * j:128 * (j + 1)], 4, 16, mid)
        p16 = _gather_classes(slab, lambda r, j: o16[r, :, 128 * j:128 * (j + 1)], 4, 16, mid)
        m = jnp.maximum(jnp.maximum(a1, a4), a16)
        e1, e4, e16 = jnp.exp(a1 - m), jnp.exp(a4 - m), jnp.exp(a16 - m)
        den = e1 + e4 + e16
        lse = m + jnp.log(den)
        o = (e1 * p1 + e4 * p4 + e16 * p16) / den
        za = za_ref[...].astype(F32)
        sga = _sigmoid(za)
        sa = za * sga
        y = jnp.concatenate([y_conv, o * sa], axis=1).astype(BF16)
        y_o[...] = y

        diff = xmt_ref[...] + jnp.dot(y, w_ref[...], preferred_element_type=F32)
        loss_o[...] += (0.5 / D_MODEL) * jnp.sum(diff * diff)
        dout = diff * (1.0 / D_MODEL)
        dout_o[...] = dout
        dy = lax.dot_general(dout.astype(BF16), w_ref[...], (((1,), (1,)), ((), ())), preferred_element_type=F32)
        dyc, dya = dy[:, 0:512], dy[:, 512:1024]

        do = dya * sa
        dza_o[...] = (dya * o * (sga * (1.0 + za * (1.0 - sga)))).astype(BF16)
        lane = lax.broadcasted_iota(jnp.int32, (tm, 512), 1)
        ld = jnp.where((lane & (HEAD_DIM - 1)) < HEAD_DIM // 2, lse, _group_sum(do * o, b_ref))
        do1_o[0] = do.astype(BF16)
        ld1_o[0] = ld
        _to_slabs(slab, do)
        _scatter_classes(slab, 0, 4, do4_o, 4)
        _scatter_classes(slab, 0, 4, do16_o, 16, 0, mid)
        _to_slabs(slab, ld)
        _scatter_classes(slab, 0, 4, ld4_o, 4)
        _scatter_classes(slab, 0, 4, ld16_o, 16, 0, mid)

        dzc = dyc * sz * gb
        dzc_o[...] = dzc.astype(BF16)
        dgbz_o[:, 0:512] = (dyc * sz * zb).astype(BF16)
        dgbz_o[:, 512:1024] = (dyc * gb * zb * (sg * (1.0 + zc * (1.0 - sg)))).astype(BF16)
        dcb_o[...] += jnp.sum(dzc, axis=0, keepdims=True)
        dcw_o[0:1, :] += jnp.sum(dzc * t_up, axis=0, keepdims=True)
        dcw_o[1:2, :] += jnp.sum(dzc * tt, axis=0, keepdims=True)
        dcw_o[2:3, :] += jnp.sum(dzc * t_dn, axis=0, keepdims=True)

    row = lambda w, j=0: pl.BlockSpec((tm, w), lambda i: (i, j))
    full = lambda shp: pl.BlockSpec(shp, lambda i: (0,) * len(shp))
    prev = pl.BlockSpec((hb, 2048), lambda i: (jnp.maximum(i * (tm // hb) - 1, 0), 0))
    nxt = pl.BlockSpec((hb, 2048), lambda i: (jnp.minimum((i + 1) * (tm // hb), s // hb - 1), 0))
    cls = lambda d, dt: jax.ShapeDtypeStruct((d, s // d, 512), dt)
    cspecs = [_class_spec(d, 512, tm) for d in DILATIONS]
    return pl.pallas_call(
        body, name="combine", grid=(nt,),
        out_shape=(jax.ShapeDtypeStruct((s, 1024), BF16), jax.ShapeDtypeStruct((s, 1024), F32),
                   cls(1, F32), cls(1, BF16), jax.ShapeDtypeStruct((s, 512), BF16),
                   jax.ShapeDtypeStruct((s, 1024), BF16), jax.ShapeDtypeStruct((s, 512), BF16),
                   jax.ShapeDtypeStruct((1, 128), F32), jax.ShapeDtypeStruct((1, 512), F32),
                   jax.ShapeDtypeStruct((8, 512), F32),
                   cls(4, BF16), cls(4, F32), cls(16, BF16), cls(16, F32)),
        in_specs=cspecs + cspecs + [row(2048), prev, nxt, row(512, 1), row(1024),
                                    _resident((1024, 1024)), full((8, 512)), full((1, 512)), full((256, 256))],
        out_specs=(row(1024), row(1024), cspecs[0], cspecs[0], row(512), row(1024), row(512),
                   full((1, 128)), full((1, 512)), full((8, 512)),
                   cspecs[1], cspecs[1], cspecs[2], cspecs[2]),
        scratch_shapes=[pltpu.VMEM((4, tm, 128), F32), pltpu.VMEM((4, tm, 128), F32)],
        compiler_params=_params(("arbitrary",)),
    )(*o_g, *lse_g, cg, cg, cg, vz, xmt, wout, cw, cb, b256)


def _attn_bwd(qkn_l, v_l, do_l, ld_l, bias, gi, name):
    r_cls, length, _ = qkn_l.shape
    qt = min(length, 2048 if length <= 4096 else 1024)
    nb, nsub, nt = length // QB, qt // QB, length // qt
    chunk = min(length, 4096)
    nchunk = length // chunk

    def body(q_ref, k_ref, v_ref, do_ref, ld_ref, b_ref, dq_ref, dkv_hbm, dsum_ref, dk_acc, dv_acc, stage, sems):
        p_id, r, t = pl.program_id(0), pl.program_id(1), pl.program_id(2)
        lo = lax.broadcasted_iota(jnp.int32, (QB, 128), 1) < HEAD_DIM

        @pl.when(t == 0)
        def _():
            dk_acc[...] = jnp.zeros_like(dk_acc)
            dv_acc[...] = jnp.zeros_like(dv_acc)

        @pl.when((t == 0) & (r == 0))
        def _():
            dsum_ref[...] = jnp.zeros_like(dsum_ref)

        nt_dims = (((1,), (1,)), ((), ()))
        tn_dims = (((0,), (0,)), ((), ()))
        coords, qqs, dds, logits, dps, lcols, dcols = [], [], [], [], [], [], []
        for i in range(nsub):
            q0, start, variant = _block_coords(t, i, nsub, nb, length)
            qq = _split_heads(q_ref[q0:q0 + QB, :], lo)
            dd = _split_heads(do_ref[q0:q0 + QB, :], lo)
            k = k_ref[pl.ds(start, KB), :]
            v = v_ref[pl.ds(start, KB), :]
            logits.append(lax.dot_general(qq, k, nt_dims, preferred_element_type=F32) + b_ref[variant])
            dps.append(lax.dot_general(dd, v, nt_dims, preferred_element_type=F32))
            lcols.append(_col_pair(ld_ref, q0, 0))
            dcols.append(_col_pair(ld_ref, q0, HEAD_DIM // 2))
            coords.append((q0, start, variant))
            qqs.append(qq)
            dds.append(dd)
        p = jnp.exp(jnp.concatenate(logits, axis=0) - jnp.concatenate(lcols, axis=0))
        ds = p * (jnp.concatenate(dps, axis=0) - jnp.concatenate(dcols, axis=0))
        pb = p.astype(BF16)
        dsb = ds.astype(BF16)
        middle = None
        for i in range(nsub):
            q0, start, variant = coords[i]
            rows = slice(2 * QB * i, 2 * QB * (i + 1))
            if 0 < i < nsub - 1:
                middle = ds[rows] if middle is None else middle + ds[rows]
            else:
                dsum_ref[variant] += ds[rows]
            dqq = jnp.dot(dsb[rows], k_ref[pl.ds(start, KB), :], preferred_element_type=F32)
            dq_ref[q0:q0 + QB, :] = jnp.where(lo, dqq[0:QB], dqq[QB:2 * QB]).astype(BF16)
            dk_acc[pl.ds(start, KB), :] += lax.dot_general(dsb[rows], qqs[i], tn_dims, preferred_element_type=F32)
            dv_acc[pl.ds(start, KB), :] += lax.dot_general(pb[rows], dds[i], tn_dims, preferred_element_type=F32)
        if middle is not None:
            dsum_ref[1] += middle

        @pl.when(t == nt - 1)
        def _():
            def copy(k):
                which, c = k // nchunk, k % nchunk
                rows = pl.ds(c * chunk, chunk)
                return pltpu.make_async_copy(stage.at[k % 2], dkv_hbm.at[r, p_id, which, rows, :], sems.at[k % 2])

            for k in range(2 * nchunk):
                if k < 2:
                    @pl.when((p_id > 0) | (r > 0))
                    def _():
                        copy(k).wait()
                else:
                    copy(k).wait()
                acc = (dk_acc, dv_acc)[k // nchunk]
                stage[k % 2] = acc[pl.ds((k % nchunk) * chunk, chunk), :].astype(BF16)
                copy(k).start()

            @pl.when((p_id == N_PAIR - 1) & (r == r_cls - 1))
            def _():
                copy(0).wait()
                copy(1).wait()

    qspec = pl.BlockSpec((None, qt, 128), lambda p, r, t: (r, t, p))
    return pl.pallas_call(
        body, name=name, grid=(N_PAIR, r_cls, nt),
        out_shape=(jax.ShapeDtypeStruct((r_cls, length, 512), BF16),
                   jax.ShapeDtypeStruct((r_cls, N_PAIR, 2, length, 128), BF16),
                   jax.ShapeDtypeStruct((N_PAIR, 3, 2 * QB, KB), F32)),
        in_specs=[qspec,
                  pl.BlockSpec((None, length, 128), lambda p, r, t: (r, 0, 4 + p)),
                  pl.BlockSpec((None, length, 128), lambda p, r, t: (r, 0, p)),
                  qspec, qspec,
                  pl.BlockSpec((None, 3, None, 2 * QB, KB), lambda p, r, t: (gi, 0, p, 0, 0))],
        out_specs=(qspec, pl.BlockSpec(memory_space=pl.ANY),
                   pl.BlockSpec((None, 3, 2 * QB, KB), lambda p, r, t: (p, 0, 0, 0))),
        scratch_shapes=[pltpu.VMEM((length, 128), F32), pltpu.VMEM((length, 128), F32),
                        pltpu.VMEM((2, chunk, 128), BF16), pltpu.SemaphoreType.DMA((2,))],
        compiler_params=_params(("arbitrary", "arbitrary", "arbitrary")),
    )(qkn_l, qkn_l, v_l, do_l, ld_l, bias)


def _bwd_tail(dq_g, dkv_g, qkr, qkw, dza, dgbz, dzc, cg, cw, wblk, x, norm_w, dout, b256):
    s = x.shape[0]
    tm = TM_COMBINE
    hb = 16
    nt = s // tm

    def body(dq1, dq4, dq16, dkv1, dkv4, dkv16, qkr_ref, qkw_ref, dza_ref, dgbz_ref, dzc_ref,
             dzp_ref, dzn_ref, u_ref, gc_ref, cw_ref, w_ref, x_ref, nw_ref, dout_ref, b_ref,
             gx_o, dproj_o, dnw_o, dqkw_o, slab, mid):
        i = pl.program_id(0)

        def nat_q(ref, d):
            return _gather_classes(slab, lambda r, j: ref[r, :, 128 * j:128 * (j + 1)], 4, d, mid)

        def nat_kv(ref, d, which):
            return _gather_classes(slab, lambda r, j: ref[r, j, which], 4, d, mid)

        @pl.when(i == 0)
        def _():
            dnw_o[...] = jnp.zeros_like(dnw_o)
            dqkw_o[...] = jnp.zeros_like(dqkw_o)

        dzc = dzc_ref[...].astype(F32)
        d_prev = jnp.where(i == 0, 0.0, dzp_ref[hb - 1:hb, :].astype(F32))
        d_next = jnp.where(i == nt - 1, 0.0, dzn_ref[0:1, :].astype(F32))
        rows = lax.broadcasted_iota(jnp.int32, (tm, 512), 0)
        d_up = jnp.where(rows == 0, d_prev, pltpu.roll(dzc, 1, 0))
        d_dn = jnp.where(rows == tm - 1, d_next, pltpu.roll(dzc, tm - 1, 0))
        dt = cw_ref[0:1, :] * d_dn + cw_ref[1:2, :] * dzc + cw_ref[2:3, :] * d_up
        u = u_ref[...].astype(F32)
        gc = gc_ref[...].astype(F32)
        dproj_o[:, 0:512] = (dt * gc).astype(BF16)
        dproj_o[:, 512:1024] = dgbz_ref[:, 0:512]
        dproj_o[:, 1024:1536] = (dt * u).astype(BF16)
        dproj_o[:, 1536:2048] = dgbz_ref[:, 512:1024]

        dqn = (dq1[0].astype(F32) + nat_q(dq4, 4) + nat_q(dq16, 16)) * (1.0 / 8.0)
        dk1 = jnp.concatenate([dkv1[0, j, 0] for j in range(N_PAIR)], axis=1)
        dv1 = jnp.concatenate([dkv1[0, j, 1] for j in range(N_PAIR)], axis=1)
        dkn = dk1 + nat_kv(dkv4, 4, 0) + nat_kv(dkv16, 16, 0)
        dvn = dv1 + nat_kv(dkv4, 4, 1) + nat_kv(dkv16, 16, 1)
        g = jnp.concatenate([dqn, dkn], axis=1) * qkw_ref[...]
        raw = qkr_ref[...].astype(F32)
        rr = lax.rsqrt(_group_sum(raw * raw, b_ref, split=False) * (1.0 / HEAD_DIM) + EPS)
        proj_gq = _group_sum(g * raw, b_ref) * (1.0 / HEAD_DIM)
        draw = rr * g - raw * (rr * rr * rr) * proj_gq
        dqkw_o[...] += jnp.sum(jnp.concatenate([dqn, dkn], axis=1) * raw * rr, axis=0, keepdims=True)
        dproj_o[:, 2048:3072] = draw.astype(BF16)
        dproj_o[:, 3072:3584] = dvn.astype(BF16)
        dproj_o[:, 3584:4096] = dza_ref[...]

        nt_dims = (((1,), (1,)), ((), ()))
        dh = lax.dot_general(dproj_o[:, 0:1024], w_ref[0], nt_dims, preferred_element_type=F32)
        for b in range(1, 4):
            dh += lax.dot_general(dproj_o[:, 1024 * b:1024 * b + 1024], w_ref[b], nt_dims,
                                  preferred_element_type=F32)

        xf = x_ref[...]
        r = lax.rsqrt(jnp.mean(xf * xf, axis=-1, keepdims=True) + EPS)
        gh = dh * nw_ref[...]
        dnw_o[...] += jnp.sum(dh * xf * r, axis=0, keepdims=True)
        mean_gx = jnp.mean(gh * xf, axis=-1, keepdims=True)
        gx_o[...] = dout_ref[...] + r * gh - xf * (r * r * r) * mean_gx

    row = lambda w, j=0: pl.BlockSpec((tm, w), lambda i: (i, j))
    full = lambda shp: pl.BlockSpec(shp, lambda i: (0,) * len(shp))
    prev = pl.BlockSpec((hb, 512), lambda i: (jnp.maximum(i * (tm // hb) - 1, 0), 0))
    nxt = pl.BlockSpec((hb, 512), lambda i: (jnp.minimum((i + 1) * (tm // hb), s // hb - 1), 0))
    return pl.pallas_call(
        body, name="bwd_tail", grid=(nt,),
        out_shape=(jax.ShapeDtypeStruct((s, 1024), F32), jax.ShapeDtypeStruct((s, 4096), BF16),
                   jax.ShapeDtypeStruct((1, 1024), F32), jax.ShapeDtypeStruct((1, 1024), F32)),
        in_specs=[_class_spec(d, 512, tm) for d in DILATIONS]
        + [pl.BlockSpec((d, N_PAIR, 2, tm // d, 128), lambda i: (0, 0, 0, i, 0)) for d in DILATIONS]
        + [row(1024), full((1, 1024)), row(512), row(1024), row(512), prev, nxt,
           row(512, 0), row(512, 2), full((8, 512)), _resident((4, 1024, 1024)), row(1024),
           full((1, 1024)), row(1024), full((256, 256))],
        out_specs=(row(1024), row(4096), full((1, 1024)), full((1, 1024))),
        scratch_shapes=[pltpu.VMEM((4, tm, 128), F32), pltpu.VMEM((4, tm, 128), F32)],
        compiler_params=_params(("arbitrary",)),
    )(*dq_g, *dkv_g, qkr, qkw, dza, dgbz, dzc, dzc, dzc, cg, cg, cw, wblk, x, norm_w, dout, b256)


def _wgrad(a, b, row_blocked, name):
    s, m = a.shape
    n = b.shape[1]
    tk = 1024
    ncol = min(n, 2048)
    nj, nk = n // ncol, s // tk

    def body(a_ref, b_ref, o_ref, acc):
        kk = pl.program_id(1)

        @pl.when(kk == 0)
        def _():
            acc[...] = jnp.zeros_like(acc)

        acc[...] += lax.dot_general(a_ref[...], b_ref[...].astype(BF16), (((0,), (0,)), ((), ())),
                                    preferred_element_type=F32)

        @pl.when(kk == nk - 1)
        def _():
            blocks, _, rows, _ = o_ref.shape
            for blk in range(blocks):
                for half in range(2):
                    if row_blocked:
                        r0 = (2 * blk + half) * rows
                        o_ref[blk, half] = acc[r0:r0 + rows, :].astype(BF16)
                    else:
                        o_ref[blk, half] = acc[half * rows:(half + 1) * rows,
                                               1024 * blk:1024 * (blk + 1)].astype(BF16)

    if row_blocked:
        out_shape = jax.ShapeDtypeStruct((4, 2, m // 8, 1024), BF16)
        out_spec = pl.BlockSpec((4, 2, m // 8, 1024), lambda j, k: (0, 0, 0, 0))
    else:
        out_shape = jax.ShapeDtypeStruct((n // 1024, 2, m // 2, 1024), BF16)
        out_spec = pl.BlockSpec((ncol // 1024, 2, m // 2, 1024), lambda j, k: (j, 0, 0, 0))
    return pl.pallas_call(
        body, name=name, grid=(nj, nk),
        out_shape=out_shape,
        in_specs=[pl.BlockSpec((tk, m), lambda j, k: (k, 0)), pl.BlockSpec((tk, ncol), lambda j, k: (k, j))],
        out_specs=out_spec,
        scratch_shapes=[pltpu.VMEM((m, ncol), F32)],
        compiler_params=_params(("parallel", "arbitrary")),
    )(a, b)


def _dbias(dsums, onehot_all):
    def body(ds1_ref, ds4_ref, ds16_ref, oh_ref, o_ref):
        @pl.when(pl.program_id(0) == 0)
        def _():
            o_ref[...] = jnp.zeros_like(o_ref)

        hrow = lax.broadcasted_iota(jnp.int32, (8, KB), 0)
        flip = (lax.broadcasted_iota(jnp.int32, (QB, QB), 0)
                + lax.broadcasted_iota(jnp.int32, (QB, QB), 1) == QB - 1).astype(F32)

        def diagonal_sums(tile):
            rev = jnp.dot(flip, tile, preferred_element_type=F32, precision=lax.Precision.HIGHEST)
            sums = jnp.sum(pltpu.roll(rev, 0, 1, stride=1, stride_axis=0), axis=0, keepdims=True)
            return pltpu.roll(sums, KB - (QB - 1), 1)

        for g, ds_ref in enumerate((ds1_ref, ds4_ref, ds16_ref)):
            diag = jnp.zeros((8, KB), F32)
            for p in range(N_PAIR):
                diag = jnp.where(hrow == 2 * p, diagonal_sums(ds_ref[p, 0:QB, :]), diag)
                diag = jnp.where(hrow == 2 * p + 1, diagonal_sums(ds_ref[p, QB:2 * QB, :]), diag)
            o_ref[...] += jnp.dot(diag, oh_ref[g], preferred_element_type=F32, precision=lax.Precision.HIGHEST)

    ds_spec = pl.BlockSpec((N_PAIR, None, 2 * QB, KB), lambda v: (0, v, 0, 0))
    return pl.pallas_call(
        body, name="dbias", grid=(3,),
        out_shape=jax.ShapeDtypeStruct((8, 128), F32),
        in_specs=[ds_spec, ds_spec, ds_spec, pl.BlockSpec((3, None, KB, 128), lambda v: (0, v, 0, 0))],
        out_specs=pl.BlockSpec((8, 128), lambda v: (0, 0)),
        compiler_params=_params(("arbitrary",)),
    )(*dsums, onehot_all)


def _gsync(pw_in, pw_out, small):
    hin, hout = pw_in.shape[2], pw_out.shape[2]
    nsmall = small.shape[0]

    def body(pin_hbm, pout_hbm, small_ref, gin_o, gout_o, small_o,
             mine_in, recv_in, sbuf_in, rbuf_in, mine_out, recv_out, sbuf_out, rbuf_out, gather,
             lsem, asend, arecv, bsend, brecv, csend, crecv, ssend, srecv):
        x, y, c = lax.axis_index("x"), lax.axis_index("y"), lax.axis_index("c")
        b = 2 * x + y
        dev = 4 * x + 2 * y + c
        sib = (x, y, 1 - c)

        def rcopy(src, dst, ssem, rsem, to):
            return pltpu.make_async_remote_copy(src_ref=src, dst_ref=dst, send_sem=ssem, recv_sem=rsem,
                                                device_id=to, device_id_type=MESH)

        gather[dev] = small_ref[...]
        s_sends = []
        for k in range(1, 8):
            to = (x ^ (k >> 2), y ^ ((k >> 1) & 1), c ^ (k & 1))
            cp = rcopy(gather.at[dev], gather.at[dev], ssend.at[k - 1], srecv.at[k - 1], to)
            cp.start()
            s_sends.append(cp)

        a_in = rcopy(pin_hbm.at[:, 1 - c], recv_in, asend.at[0], arecv.at[0], sib)
        a_out = rcopy(pout_hbm.at[:, 1 - c], recv_out, asend.at[1], arecv.at[1], sib)
        a_in.start()
        a_out.start()
        l_in = pltpu.make_async_copy(pin_hbm.at[:, c], mine_in, lsem.at[0])
        l_out = pltpu.make_async_copy(pout_hbm.at[:, c], mine_out, lsem.at[1])
        l_in.start()
        l_out.start()
        l_in.wait()
        l_out.wait()

        def stage_b(a_cp, mine, recv, sbuf, rbuf, base):
            a_cp.wait_recv()
            sends = []
            for k in (1, 2, 3):
                bk = b ^ k
                sbuf[k - 1] = (mine[bk].astype(F32) + recv[bk].astype(F32)).astype(BF16)
                cp = rcopy(sbuf.at[k - 1], rbuf.at[k - 1], bsend.at[base + k - 1], brecv.at[base + k - 1],
                           (x ^ (k >> 1), y ^ (k & 1), c))
                cp.start()
                sends.append(cp)
            return sends

        b_in = stage_b(a_in, mine_in, recv_in, sbuf_in, rbuf_in, 0)
        b_out = stage_b(a_out, mine_out, recv_out, sbuf_out, rbuf_out, 3)

        def stage_c(b_sends, mine, recv, rbuf, g_o, half, idx):
            acc = mine[b].astype(F32) + recv[b].astype(F32)
            for k in (1, 2, 3):
                b_sends[k - 1].wait_recv()
                acc = acc + rbuf[k - 1].astype(F32)
            rows = g_o.at[pl.ds(pl.multiple_of(c * half, half), half), :]
            g_o[pl.ds(pl.multiple_of(c * half, half), half), :] = acc
            cp = rcopy(rows, rows, csend.at[idx], crecv.at[idx], sib)
            cp.start()
            return cp

        c_in = stage_c(b_in, mine_in, recv_in, rbuf_in, gin_o, hin, 0)
        c_out = stage_c(b_out, mine_out, recv_out, rbuf_out, gout_o, hout, 1)

        for cp in s_sends:
            cp.wait_recv()
        tot = gather[0]
        for d in range(1, 8):
            tot = tot + gather[d]
        small_o[...] = tot

        for g_o, half, idx in ((gin_o, hin, 0), (gout_o, hout, 1)):
            other = g_o.at[pl.ds(pl.multiple_of((1 - c) * half, half), half), :]
            rcopy(other, other, csend.at[idx], crecv.at[idx], sib).wait_recv()
        for cp in s_sends + [a_in, a_out] + b_in + b_out + [c_in, c_out]:
            cp.wait_send()

    vm = pl.BlockSpec(memory_space=pltpu.VMEM)
    hbm = pl.BlockSpec(memory_space=pl.ANY)
    return pl.pallas_call(
        body, name="gsync",
        out_shape=(jax.ShapeDtypeStruct((2 * hin, 1024), F32), jax.ShapeDtypeStruct((2 * hout, 1024), F32),
                   jax.ShapeDtypeStruct((nsmall, 128), F32)),
        in_specs=[hbm, hbm, vm], out_specs=(vm, vm, vm),
        scratch_shapes=[pltpu.VMEM((4, hin, 1024), BF16), pltpu.VMEM((4, hin, 1024), BF16),
                        pltpu.VMEM((3, hin, 1024), BF16), pltpu.VMEM((3, hin, 1024), BF16),
                        pltpu.VMEM((4, hout, 1024), BF16), pltpu.VMEM((4, hout, 1024), BF16),
                        pltpu.VMEM((3, hout, 1024), BF16), pltpu.VMEM((3, hout, 1024), BF16),
                        pltpu.VMEM((8, nsmall, 128), F32),
                        pltpu.SemaphoreType.DMA((2,)),
                        pltpu.SemaphoreType.DMA((2,)), pltpu.SemaphoreType.DMA((2,)),
                        pltpu.SemaphoreType.DMA((6,)), pltpu.SemaphoreType.DMA((6,)),
                        pltpu.SemaphoreType.DMA((2,)), pltpu.SemaphoreType.DMA((2,)),
                        pltpu.SemaphoreType.DMA((7,)), pltpu.SemaphoreType.DMA((7,))],
        compiler_params=_params(),
    )(pw_in, pw_out, small)


def _adamw_math(w, g, m, v):
    m = ADAM_B1 * m + (1.0 - ADAM_B1) * g
    v = ADAM_B2 * v + (1.0 - ADAM_B2) * (g * g)
    m_hat = m / (1.0 - ADAM_B1 ** ADAM_STEP)
    v_hat = v / (1.0 - ADAM_B2 ** ADAM_STEP)
    delta = -ADAM_LR * (m_hat / (jnp.sqrt(v_hat) + ADAM_EPS) + ADAM_WD * w)
    return delta, m, v


def _adamw(w, g, m, v, name):
    rows, cols = w.shape
    tr = 256 if rows % 256 == 0 else rows

    def body(w_ref, g_ref, m_ref, v_ref, d_o, m_o, v_o):
        d, m2, v2 = _adamw_math(w_ref[...], g_ref[...], m_ref[...], v_ref[...])
        d_o[...] = d
        m_o[...] = m2
        v_o[...] = v2

    spec = pl.BlockSpec((tr, cols), lambda i: (i, 0))
    shp = jax.ShapeDtypeStruct((rows, cols), F32)
    return pl.pallas_call(
        body, name=name, grid=(rows // tr,), out_shape=(shp, shp, shp),
        in_specs=[spec] * 4, out_specs=(spec, spec, spec),
        compiler_params=_params(("parallel",)),
    )(w, g, m, v)


def _fold_heads(dqkw):
    def body(x_ref, o_ref):
        xs = x_ref[...]
        sq = xs[0:1] + xs[1:2] + xs[2:3] + xs[3:4]
        sk = xs[4:5] + xs[5:6] + xs[6:7] + xs[7:8]
        both = jnp.concatenate([sq, sk], axis=0)
        o_ref[...] = both + pltpu.roll(both, HEAD_DIM, 1)

    vm = pl.BlockSpec(memory_space=pltpu.VMEM)
    return pl.pallas_call(body, name="fold_heads", out_shape=jax.ShapeDtypeStruct((2, 128), F32),
                          in_specs=[vm], out_specs=vm, compiler_params=_params())(dqkw)


def kernel(x, norm_w, w_in, conv_w, conv_b, q_norm_w, k_norm_w, rel_bias, w_out, loss_target, m_norm_w, m_w_in, m_conv_w, m_conv_b, m_q_norm_w, m_k_norm_w, m_rel_bias, m_w_out, v_norm_w, v_w_in, v_conv_w, v_conv_b, v_q_norm_w, v_k_norm_w, v_rel_bias, v_w_out):
    x2 = x[0]
    tgt = loss_target[0]
    blk = 2 * lax.axis_index("x") + lax.axis_index("y")

    conv_w8 = jnp.pad(conv_w, ((0, 5), (0, 0)))
    wblk, woutblk, cwblk = _wgather(w_in, w_out, conv_w8)
    wout_full = woutblk.reshape(1024, 1024)
    cw_full = cwblk.transpose(1, 0, 2).reshape(8, 512)

    qkw = jnp.concatenate([jnp.tile(q_norm_w, 8) * 0.125, jnp.tile(k_norm_w, 8)])[None, :]
    qkw_raw = jnp.concatenate([jnp.tile(q_norm_w, 8), jnp.tile(k_norm_w, 8)])[None, :]
    gidx = jnp.arange(256) // HEAD_DIM
    b256 = (gidx[:, None] == gidx[None, :]).astype(BF16)

    h, cg, qkr, qkn, vz, qkn4, v4, qkn16, v16, xmt = _proj(x2, tgt, norm_w[None, :], wblk, qkw, b256)

    biases = _bias_tables(rel_bias)
    qkn_l = [qkn[None], qkn4, qkn16]
    v_l = [vz[None], v4, v16]
    o_g, lse_g = [], []
    for gi, d in enumerate(DILATIONS):
        o_l, lse_l = _attn_fwd(qkn_l[gi], v_l[gi], biases, gi, f"attn_fwd_d{d}")
        o_g.append(o_l)
        lse_g.append(lse_l)

    (y, dout, ld1, do1, dza, dgbz, dzc, loss_p, dcb, dcw, do4, ld4, do16, ld16) = _combine(
        o_g, lse_g, cg, vz, xmt, wout_full, cw_full, conv_b[None, :], b256)

    dq_g, dkv_g, dsums = [], [], []
    for gi, (d, do_l, ld_l) in enumerate(zip(DILATIONS, (do1, do4, do16), (ld1, ld4, ld16))):
        dq_l, dkv_l, dsum = _attn_bwd(qkn_l[gi], v_l[gi], do_l, ld_l, biases, gi, f"attn_bwd_d{d}")
        dq_g.append(dq_l)
        dkv_g.append(dkv_l)
        dsums.append(dsum)

    grad_x, dproj, dnw, dqkw = _bwd_tail(dq_g, dkv_g, qkr, qkw_raw, dza, dgbz, dzc, cg, cw_full, wblk,
                                         x2, norm_w[None, :], dout, b256)

    pw_in = _wgrad(h, dproj, False, "wgrad_in")
    pw_out = _wgrad(y, dout, True, "wgrad_out")
    dbias8 = _dbias(dsums, jnp.stack([_diag_bucket_onehot(d) for d in DILATIONS], axis=0))

    small = jnp.concatenate([dnw.reshape(8, 128), dcb.reshape(4, 128), dqkw.reshape(8, 128),
                             dcw[0:3].reshape(12, 128), dbias8, jnp.pad(loss_p, ((0, 7), (0, 0)))], axis=0)
    g_win, g_wout, gsmall = _gsync(pw_in, pw_out, small)

    g_nw = gsmall[0:8].reshape(1024)
    g_cb = gsmall[8:12].reshape(512)
    folded = _fold_heads(gsmall[12:20])
    g_qw, g_kw = folded[0, 0:64], folded[1, 0:64]
    g_cw = lax.dynamic_slice(gsmall[20:32].reshape(3, 512), (0, blk * 128), (3, 128))
    g_rb = gsmall[32:40][:, 0:32].T
    loss = gsmall[40, 0]

    d_win, nm_win, nv_win = _adamw(w_in, g_win, m_w_in, v_w_in, "adamw_w_in")
    d_wout, nm_wout, nv_wout = _adamw(w_out, g_wout, m_w_out, v_w_out, "adamw_w_out")

    def pack(parts):
        rows = [parts[0].reshape(8, 128), parts[1].reshape(4, 128),
                jnp.pad(parts[2], (0, 64))[None, :], jnp.pad(parts[3], (0, 64))[None, :],
                parts[4], jnp.pad(parts[5].T, ((0, 0), (0, 96)))]
        return jnp.concatenate(rows, axis=0)

    ws = pack([norm_w, conv_b, q_norm_w, k_norm_w, conv_w, rel_bias])
    gs = pack([g_nw, g_cb, g_qw, g_kw, g_cw, g_rb])
    ms = pack([m_norm_w, m_conv_b, m_q_norm_w, m_k_norm_w, m_conv_w, m_rel_bias])
    vs = pack([v_norm_w, v_conv_b, v_q_norm_w, v_k_norm_w, v_conv_w, v_rel_bias])
    rpad = lambda a: jnp.pad(a, ((0, 7), (0, 0)))
    d_s, nm_s, nv_s = _adamw(rpad(ws), rpad(gs), rpad(ms), rpad(vs), "adamw_small")

    def unpack(a):
        return (a[0:8].reshape(1024), a[12:13, 0:64].reshape(64), a[13:14, 0:64].reshape(64),
                a[14:17], a[8:12].reshape(512), a[17:25, 0:32].T)

    def ordered(nw, win, cw, cb, qw, kw, rb, wout):
        return (nw, win, cw, cb, qw, kw, rb, wout)

    g_un = (g_nw, g_qw, g_kw, g_cw, g_cb, g_rb)
    outs = [loss, grad_x[None]]
    for un, win_v, wout_v in ((g_un, g_win, g_wout), (unpack(d_s), d_win, d_wout),
                              (unpack(nm_s), nm_win, nm_wout), (unpack(nv_s), nv_win, nv_wout)):
        nw, qw, kw, cw, cb, rb = un
        outs.extend(ordered(nw, win_v, cw, cb, qw, kw, rb, wout_v))
    return tuple(outs)
```

```python
import math

import jax
import jax.numpy as jnp
from jax import lax
from jax.experimental import pallas as pl
from jax.experimental.pallas import tpu as pltpu

F32 = jnp.float32
BF16 = jnp.bfloat16
MESH = pl.DeviceIdType.MESH

D_MODEL = 1024
CONV_W = 512
ATTN_W = 512
HEAD_DIM = 64
N_PAIR = 4
DILATIONS = (1, 4, 16)
HALF = 64
QB = 128
KB = QB + 2 * HALF
NUM_BUCKETS = 32
MAX_DISTANCE = 1024
EPS = 1e-6
NEG = -1e30
ADAM_LR, ADAM_B1, ADAM_B2, ADAM_EPS, ADAM_WD, ADAM_STEP = 0.001, 0.9, 0.999, 1e-08, 0.01, 10
VMEM_LIMIT = 48 << 20


def _params(sem=None, vmem=VMEM_LIMIT, **kw):
    if sem is not None:
        kw["dimension_semantics"] = sem
    return pltpu.CompilerParams(vmem_limit_bytes=vmem, **kw)


def _sigmoid(z):
    return 1.0 / (1.0 + jnp.exp(-z))


def _group_sum(val, b_ref, split=True):
    hi = val.astype(BF16)
    lo = (val - hi.astype(F32)).astype(BF16) if split else None
    outs = []
    for j in range(val.shape[1] // 256):
        sl = slice(256 * j, 256 * j + 256)
        part = jnp.dot(hi[:, sl], b_ref[...], preferred_element_type=F32)
        if split:
            part = part + jnp.dot(lo[:, sl], b_ref[...], preferred_element_type=F32)
        outs.append(part)
    return outs[0] if len(outs) == 1 else jnp.concatenate(outs, axis=1)


def _t5_bucket(rel):
    half_b = NUM_BUCKETS // 2
    max_exact = half_b // 2
    ret = jnp.where(rel > 0, half_b, 0)
    n = jnp.abs(rel)
    nf = jnp.maximum(n, 1).astype(F32)
    large = max_exact + (jnp.log(nf / max_exact) / math.log(MAX_DISTANCE / max_exact)
                         * (half_b - max_exact)).astype(jnp.int32)
    large = jnp.minimum(large, half_b - 1)
    return ret + jnp.where(n < max_exact, n, large)


def _bias_tables(rel_bias):
    rows = []
    key = jnp.arange(KB)
    for dilation in DILATIONS:
        for variant in range(3):
            off = (0, HALF, 2 * HALF)[variant]
            rel = ((key - off + KB // 2) % KB) - KB // 2
            bkt = _t5_bucket(jnp.clip(rel, -HALF, HALF) * dilation)
            rows.append(jnp.where(jnp.abs(rel) <= HALF, bkt, -1))
    bkt_all = jnp.broadcast_to(jnp.stack(rows, axis=0).astype(jnp.int32)[:, None, :], (9, 8, KB))

    def body(rb_ref, bkt_ref, o_ref):
        bkt = bkt_ref[...]
        off = (pl.program_id(0) % 3) * HALF
        rel = (lax.broadcasted_iota(jnp.int32, (QB, KB), 1) - lax.broadcasted_iota(jnp.int32, (QB, KB), 0)) - off
        band = jnp.abs(rel) <= HALF
        for h in range(8):
            acc = jnp.full((8, KB), NEG, F32)
            for b in range(NUM_BUCKETS):
                acc = jnp.where(bkt == b, rb_ref[b, h], acc)
            rolled = pltpu.roll(jnp.broadcast_to(acc[0:1], (QB, KB)), 0, 1, stride=1, stride_axis=0)
            o_ref[h] = jnp.where(band, rolled, NEG)

    out = pl.pallas_call(
        body, name="bias_tables", grid=(9,),
        out_shape=jax.ShapeDtypeStruct((9, 8, QB, KB), F32),
        in_specs=[pl.BlockSpec(memory_space=pltpu.SMEM), pl.BlockSpec((None, 8, KB), lambda i: (i, 0, 0))],
        out_specs=pl.BlockSpec((None, 8, QB, KB), lambda i: (i, 0, 0, 0)),
        compiler_params=_params(("parallel",)),
    )(rel_bias, bkt_all)
    return out.reshape(3, 3, N_PAIR, 2 * QB, KB)


def _diag_bucket_onehot(dilation):
    out = []
    c = jnp.arange(KB)
    for variant in range(3):
        off = (0, HALF, 2 * HALF)[variant]
        rel = ((c - off + 128) % 256) - 128
        band = jnp.abs(rel) <= HALF
        bkt = _t5_bucket(jnp.clip(rel, -HALF, HALF) * dilation)
        oh = (bkt[:, None] == jnp.arange(128)[None, :]) & band[:, None]
        out.append(oh.astype(F32))
    return jnp.stack(out, axis=0)


def _wgather(w_in, w_out, conv_w):
    rin, rout = w_in.shape[0] // 2, w_out.shape[0] // 2

    def body(win_ref, wout_ref, cw_ref, win_o, wout_o, cw_o, send_sems, recv_sems):
        x, y, c = lax.axis_index("x"), lax.axis_index("y"), lax.axis_index("c")
        b = 2 * x + y
        win_o[b] = win_ref[...].astype(BF16)
        wout_o[b] = wout_ref[...].astype(BF16)
        cw_o[b] = cw_ref[...]

        def peer(k):
            return (x ^ (k >> 1), y ^ (k & 1))

        def piece(ref, blk, half_rows, core):
            return ref.at[blk, pl.ds(core * half_rows, half_rows), :]

        def copy(sem, src, dst, to):
            return pltpu.make_async_remote_copy(src_ref=src, dst_ref=dst, send_sem=send_sems.at[sem],
                                                recv_sem=recv_sems.at[sem], device_id=to, device_id_type=MESH)

        sends = []
        for k in (1, 2, 3):
            px, py = peer(k)
            sends.append(copy(k - 1, piece(win_o, b, rin, c), piece(win_o, b, rin, c), (px, py, c)))
            sends.append(copy(3 + k - 1, piece(wout_o, b, rout, c), piece(wout_o, b, rout, c), (px, py, c)))
            sends.append(copy(6 + k - 1, cw_o.at[b], cw_o.at[b], (px, py, c)))
        for cp in sends:
            cp.start()
        fwd = []
        for k in (1, 2, 3):
            px, py = peer(k)
            bk = 2 * px + py
            copy(k - 1, piece(win_o, bk, rin, c), piece(win_o, bk, rin, c), (px, py, c)).wait_recv()
            f = copy(9 + k - 1, piece(win_o, bk, rin, c), piece(win_o, bk, rin, c), (x, y, 1 - c))
            f.start()
            fwd.append(f)
            copy(3 + k - 1, piece(wout_o, bk, rout, c), piece(wout_o, bk, rout, c), (px, py, c)).wait_recv()
            f = copy(12 + k - 1, piece(wout_o, bk, rout, c), piece(wout_o, bk, rout, c), (x, y, 1 - c))
            f.start()
            fwd.append(f)
            copy(6 + k - 1, cw_o.at[bk], cw_o.at[bk], (px, py, c)).wait_recv()
        for k in (1, 2, 3):
            px, py = peer(k)
            bk = 2 * px + py
            copy(9 + k - 1, piece(win_o, bk, rin, 1 - c), piece(win_o, bk, rin, 1 - c), (x, y, 1 - c)).wait_recv()
            copy(12 + k - 1, piece(wout_o, bk, rout, 1 - c), piece(wout_o, bk, rout, 1 - c), (x, y, 1 - c)).wait_recv()
        for cp in sends + fwd:
            cp.wait_send()

    vm = pl.BlockSpec(memory_space=pltpu.VMEM)
    return pl.pallas_call(
        body, name="wgather",
        out_shape=(jax.ShapeDtypeStruct((4,) + w_in.shape, BF16),
                   jax.ShapeDtypeStruct((4,) + w_out.shape, BF16),
                   jax.ShapeDtypeStruct((4,) + conv_w.shape, F32)),
        in_specs=[vm, vm, vm], out_specs=(vm, vm, vm),
        scratch_shapes=[pltpu.SemaphoreType.DMA((15,)), pltpu.SemaphoreType.DMA((15,))],
        compiler_params=_params(),
    )(w_in, w_out, conv_w)


TM_MATMUL = 512
TM_COMBINE = 256


def _resident(shape):
    return pl.BlockSpec(shape, lambda i: (0,) * len(shape), pipeline_mode=pl.Buffered(1))


def _to_slabs(slab, val, j0=0):
    for j in range(val.shape[1] // 128):
        slab[j0 + j] = val[:, 128 * j:128 * (j + 1)]


def _scatter_classes(slab, j0, nj, out_ref, d, part=0, mid=None):
    tm = slab.shape[1]
    n = tm // d
    if d == 4:
        for r in range(d):
            for j in range(nj):
                out_ref[r, part * n:(part + 1) * n, 128 * j:128 * (j + 1)] = (
                    slab[j0 + j, pl.ds(r, n, stride=d), :].astype(out_ref.dtype))
        return
    q = tm // 4
    for lo in range(4):
        for j in range(nj):
            mid[j0 + j, lo * q:(lo + 1) * q, :] = slab[j0 + j, pl.ds(lo, q, stride=4), :]
    for hi in range(4):
        for lo in range(4):
            for j in range(nj):
                out_ref[4 * hi + lo, part * n:(part + 1) * n, 128 * j:128 * (j + 1)] = (
                    mid[j0 + j, pl.ds(lo * q + hi, n, stride=4), :].astype(out_ref.dtype))


def _gather_classes(slab, piece, nj, d, mid=None):
    tm = slab.shape[1]
    n = tm // d
    if d == 4:
        for r in range(d):
            for j in range(nj):
                slab[j, pl.ds(r, n, stride=d), :] = piece(r, j).astype(F32)
    else:
        q = tm // 4
        for hi in range(4):
            for lo in range(4):
                for j in range(nj):
                    mid[j, pl.ds(lo * q + hi, n, stride=4), :] = piece(4 * hi + lo, j).astype(F32)
        for lo in range(4):
            for j in range(nj):
                slab[j, pl.ds(lo, q, stride=4), :] = mid[j, lo * q:(lo + 1) * q, :]
    return jnp.concatenate([slab[j] for j in range(nj)], axis=1)


def _class_spec(d, width, tm):
    return pl.BlockSpec((d, tm // d, width), lambda i: (0, i, 0))


def _proj(x, tgt, norm_w, wblk, qkw, b256):
    s = x.shape[0]
    tm = TM_MATMUL
    nparts = 2
    tp = tm // nparts

    def body(x_ref, t_ref, nw_ref, w_ref, qkw_ref, b_ref, h_o, cg_o, qkr_o, qkn_o, vz_o, qkn4_o, v4_o, qkn16_o,
             v16_o, xmt_o, slabs, mids):
        for part in range(nparts):
            rows = slice(part * tp, (part + 1) * tp)
            slab = slabs.at[part]
            xf = x_ref[rows, :]
            xmt_o[rows, :] = xf - t_ref[rows, :]
            r = lax.rsqrt(jnp.mean(xf * xf, axis=-1, keepdims=True) + EPS)
            h = (xf * r * nw_ref[...]).astype(BF16)
            h_o[rows, :] = h
            p2 = jnp.dot(h, w_ref[2], preferred_element_type=F32)
            qkr_o[rows, :] = p2.astype(BF16)
            ss = _group_sum(p2 * p2, b_ref, split=False)
            rr = lax.rsqrt(ss * (1.0 / HEAD_DIM) + EPS)
            qkn = p2 * rr * qkw_ref[...]
            qkn_o[rows, :] = qkn.astype(BF16)
            _to_slabs(slab, qkn)
            p3 = jnp.dot(h, w_ref[3], preferred_element_type=F32)
            vz_o[rows, :] = p3.astype(BF16)
            _to_slabs(slab, p3[:, 0:512], 8)
            cg_o[rows, 0:1024] = jnp.dot(h, w_ref[0], preferred_element_type=F32).astype(BF16)
            cg_o[rows, 1024:2048] = jnp.dot(h, w_ref[1], preferred_element_type=F32).astype(BF16)
            for d, q_o, v_o in ((4, qkn4_o, v4_o), (16, qkn16_o, v16_o)):
                _scatter_classes(slab, 0, 8, q_o, d, part, mids.at[part])
                _scatter_classes(slab, 8, 4, v_o, d, part, mids.at[part])

    row = lambda w: pl.BlockSpec((tm, w), lambda i: (i, 0))
    full = lambda shp: pl.BlockSpec(shp, lambda i: (0,) * len(shp))
    nat = lambda w: jax.ShapeDtypeStruct((s, w), BF16)
    cls = lambda d, w: jax.ShapeDtypeStruct((d, s // d, w), BF16)
    return pl.pallas_call(
        body, name="proj", grid=(s // tm,),
        out_shape=(nat(1024), nat(2048), nat(1024), nat(1024), nat(1024),
                   cls(4, 1024), cls(4, 512), cls(16, 1024), cls(16, 512), jax.ShapeDtypeStruct((s, 1024), F32)),
        in_specs=[row(1024), row(1024), full((1, 1024)), _resident((4, 1024, 1024)), full((1, 1024)),
                  full((256, 256))],
        out_specs=(row(1024), row(2048), row(1024), row(1024), row(1024),
                   _class_spec(4, 1024, tm), _class_spec(4, 512, tm),
                   _class_spec(16, 1024, tm), _class_spec(16, 512, tm), row(1024)),
        scratch_shapes=[pltpu.VMEM((nparts, 12, tp, 128), F32), pltpu.VMEM((nparts, 12, tp, 128), F32)],
        compiler_params=_params(("parallel",)),
    )(x, tgt, norm_w, wblk, qkw, b256)


def _block_coords(t, i, nsub, nb, length):
    n = t * nsub + i
    q0 = i * QB
    start = pl.multiple_of(jnp.clip(n * QB - HALF, 0, length - KB), HALF)
    variant = jnp.where(n == 0, 0, jnp.where(n == nb - 1, 2, 1))
    return q0, start, variant


def _split_heads(a, lo):
    zero = jnp.zeros_like(a)
    return jnp.concatenate([jnp.where(lo, a, zero), jnp.where(lo, zero, a)], axis=0)


def _col_pair(ref, q0, lane):
    return jnp.concatenate([ref[pl.ds(q0, QB), lane:lane + 1],
                            ref[pl.ds(q0, QB), HEAD_DIM + lane:HEAD_DIM + lane + 1]], axis=0)


def _attn_fwd(qkn_l, v_l, bias, gi, name):
    r_cls, length, _ = qkn_l.shape
    qt = min(length, 2048)
    nb, nsub = length // QB, qt // QB

    def body(q_ref, k_ref, v_ref, b_ref, o_ref, lse_ref):
        t = pl.program_id(2)
        lo = lax.broadcasted_iota(jnp.int32, (QB, 128), 1) < HEAD_DIM

        starts, logits = [], []
        for i in range(nsub):
            _, start, variant = _block_coords(t, i, nsub, nb, length)
            qq = _split_heads(q_ref[i * QB:(i + 1) * QB, :], lo)
            k = k_ref[pl.ds(start, KB), :]
            logits.append(lax.dot_general(qq, k, (((1,), (1,)), ((), ())), preferred_element_type=F32)
                          + b_ref[variant])
            starts.append(start)
        lg = jnp.concatenate(logits, axis=0)
        m = jnp.max(lg, axis=-1, keepdims=True)
        p = jnp.exp(lg - m)
        pb = p.astype(BF16)
        l = jnp.sum(p, axis=-1, keepdims=True)
        lse = jnp.broadcast_to(m + jnp.log(l), (nsub * 2 * QB, 128))
        inv = 1.0 / l
        for i in range(nsub):
            rows = slice(2 * QB * i, 2 * QB * (i + 1))
            v = v_ref[pl.ds(starts[i], KB), :]
            pv = jnp.dot(pb[rows], v, preferred_element_type=F32) * inv[rows]
            o_ref[i * QB:(i + 1) * QB, :] = jnp.where(lo, pv[0:QB], pv[QB:2 * QB]).astype(BF16)
            ls = lse[rows]
            lse_ref[i * QB:(i + 1) * QB, :] = jnp.where(lo, ls[0:QB], ls[QB:2 * QB])

    return pl.pallas_call(
        body, name=name, grid=(N_PAIR, r_cls, length // qt),
        out_shape=(jax.ShapeDtypeStruct((r_cls, length, 512), BF16),
                   jax.ShapeDtypeStruct((r_cls, length, 512), F32)),
        in_specs=[pl.BlockSpec((None, qt, 128), lambda p, r, t: (r, t, p)),
                  pl.BlockSpec((None, length, 128), lambda p, r, t: (r, 0, 4 + p)),
                  pl.BlockSpec((None, length, 128), lambda p, r, t: (r, 0, p)),
                  pl.BlockSpec((None, 3, None, 2 * QB, KB), lambda p, r, t: (gi, 0, p, 0, 0))],
        out_specs=(pl.BlockSpec((None, qt, 128), lambda p, r, t: (r, t, p)),
                   pl.BlockSpec((None, qt, 128), lambda p, r, t: (r, t, p))),
        compiler_params=_params(("parallel", "parallel", "arbitrary")),
    )(qkn_l, qkn_l, v_l, bias)


def _combine(o_g, lse_g, cg, vz, xmt, wout, cw, cb, b256):
    s = xmt.shape[0]
    tm = TM_COMBINE
    hb = 16
    nt = s // tm

    def body(o1, o4, o16, l1, l4, l16, cg_ref, cgp_ref, cgn_ref, za_ref, xmt_ref, w_ref, cw_ref, cb_ref,
             b_ref, y_o, dout_o, ld1_o, do1_o, dza_o, dgbz_o, dzc_o, loss_o, dcb_o, dcw_o,
             do4_o, ld4_o, do16_o, ld16_o, slab, mid):
        i = pl.program_id(0)

        @pl.when(i == 0)
        def _():
            loss_o[...] = jnp.zeros_like(loss_o)
            dcb_o[...] = jnp.zeros_like(dcb_o)
            dcw_o[...] = jnp.zeros_like(dcw_o)

        u = cg_ref[:, 0:512].astype(F32)
        gb = cg_ref[:, 512:1024].astype(F32)
        gc = cg_ref[:, 1024:1536].astype(F32)
        zc = cg_ref[:, 1536:2048].astype(F32)
        tt = gc * u
        t_prev = cgp_ref[hb - 1:hb, 0:512].astype(F32) * cgp_ref[hb - 1:hb, 1024:1536].astype(F32)
        t_next = cgn_ref[0:1, 0:512].astype(F32) * cgn_ref[0:1, 1024:1536].astype(F32)
        t_prev = jnp.where(i == 0, 0.0, t_prev)
        t_next = jnp.where(i == nt - 1, 0.0, t_next)
        rows = lax.broadcasted_iota(jnp.int32, (tm, 512), 0)
        t_up = jnp.where(rows == 0, t_prev, pltpu.roll(tt, 1, 0))
        t_dn = jnp.where(rows == tm - 1, t_next, pltpu.roll(tt, tm - 1, 0))
        w0, w1, w2 = cw_ref[0:1, :], cw_ref[1:2, :], cw_ref[2:3, :]
        zb = w0 * t_up + w1 * tt + w2 * t_dn + cb_ref[...]
        sg = _sigmoid(zc)
        sz = zc * sg
        y_conv = gb * zb * sz

        a1, p1 = l1[0], o1[0].astype(F32)
        a4 = _gather_classes(slab, lambda r, j: l4[r, :, 128 * j:128 * (j + 1)], 4, 4)
        p4 = _gather_classes(slab, lambda r, j: o4[r, :, 128 * j:128 * (j + 1)], 4, 4)
        a16 = _gather_classes(slab, lambda r, j: l16[r, :, 128 * j:128 * (j + 1)], 4, 16, mid)
        p16 = _gather_classes(slab, lambda r, j: o16[r, :, 128 * j:128 * (j + 1)], 4, 16, mid)
        m = jnp.maximum(jnp.maximum(a1, a4), a16)
        e1, e4, e16 = jnp.exp(a1 - m), jnp.exp(a4 - m), jnp.exp(a16 - m)
        den = e1 + e4 + e16
        lse = m + jnp.log(den)
        o = (e1 * p1 + e4 * p4 + e16 * p16) / den
        za = za_ref[...].astype(F32)
        sga = _sigmoid(za)
        sa = za * sga
        y = jnp.concatenate([y_conv, o * sa], axis=1).astype(BF16)
        y_o[...] = y

        diff = xmt_ref[...] + jnp.dot(y, w_ref[...], preferred_element_type=F32)
        loss_o[...] += (0.5 / D_MODEL) * jnp.sum(diff * diff)
        dout = diff * (1.0 / D_MODEL)
        dout_o[...] = dout
        dy = lax.dot_general(dout.astype(BF16), w_ref[...], (((1,), (1,)), ((), ())), preferred_element_type=F32)
        dyc, dya = dy[:, 0:512], dy[:, 512:1024]

        do = dya * sa
        dza_o[...] = (dya * o * (sga * (1.0 + za * (1.0 - sga)))).astype(BF16)
        lane = lax.broadcasted_iota(jnp.int32, (tm, 512), 1)
        ld = jnp.where((lane & (HEAD_DIM - 1)) < HEAD_DIM // 2, lse, _group_sum(do * o, b_ref))
        do1_o[0] = do.astype(BF16)
        ld1_o[0] = ld
        _to_slabs(slab, do)
        _scatter_classes(slab, 0, 4, do4_o, 4)
        _scatter_classes(slab, 0, 4, do16_o, 16, 0, mid)
        _to_slabs(slab, ld)
        _scatter_classes(slab, 0, 4, ld4_o, 4)
        _scatter_classes(slab, 0, 4, ld16_o, 16, 0, mid)

        dzc = dyc * sz * gb
        dzc_o[...] = dzc.astype(BF16)
        dgbz_o[:, 0:512] = (dyc * sz * zb).astype(BF16)
        dgbz_o[:, 512:1024] = (dyc * gb * zb * (sg * (1.0 + zc * (1.0 - sg)))).astype(BF16)
        dcb_o[...] += jnp.sum(dzc, axis=0, keepdims=True)
        dcw_o[0:1, :] += jnp.sum(dzc * t_up, axis=0, keepdims=True)
        dcw_o[1:2, :] += jnp.sum(dzc * tt, axis=0, keepdims=True)
        dcw_o[2:3, :] += jnp.sum(dzc * t_dn, axis=0, keepdims=True)

    row = lambda w, j=0: pl.BlockSpec((tm, w), lambda i: (i, j))
    full = lambda shp: pl.BlockSpec(shp, lambda i: (0,) * len(shp))
    prev = pl.BlockSpec((hb, 2048), lambda i: (jnp.maximum(i * (tm // hb) - 1, 0), 0))
    nxt = pl.BlockSpec((hb, 2048), lambda i: (jnp.minimum((i + 1) * (tm // hb), s // hb - 1), 0))
    cls = lambda d, dt: jax.ShapeDtypeStruct((d, s // d, 512), dt)
    cspecs = [_class_spec(d, 512, tm) for d in DILATIONS]
    return pl.pallas_call(
        body, name="combine", grid=(nt,),
        out_shape=(jax.ShapeDtypeStruct((s, 1024), BF16), jax.ShapeDtypeStruct((s, 1024), F32),
                   cls(1, F32), cls(1, BF16), jax.ShapeDtypeStruct((s, 512), BF16),
                   jax.ShapeDtypeStruct((s, 1024), BF16), jax.ShapeDtypeStruct((s, 512), BF16),
                   jax.ShapeDtypeStruct((1, 128), F32), jax.ShapeDtypeStruct((1, 512), F32),
                   jax.ShapeDtypeStruct((8, 512), F32),
                   cls(4, BF16), cls(4, F32), cls(16, BF16), cls(16, F32)),
        in_specs=cspecs + cspecs + [row(2048), prev, nxt, row(512, 1), row(1024),
                                    _resident((1024, 1024)), full((8, 512)), full((1, 512)), full((256, 256))],
        out_specs=(row(1024), row(1024), cspecs[0], cspecs[0], row(512), row(1024), row(512),
                   full((1, 128)), full((1, 512)), full((8, 512)),
                   cspecs[1], cspecs[1], cspecs[2], cspecs[2]),
        scratch_shapes=[pltpu.VMEM((4, tm, 128), F32), pltpu.VMEM((4, tm, 128), F32)],
        compiler_params=_params(("arbitrary",)),
    )(*o_g, *lse_g, cg, cg, cg, vz, xmt, wout, cw, cb, b256)


def _attn_bwd(qkn_l, v_l, do_l, ld_l, bias, gi, name):
    r_cls, length, _ = qkn_l.shape
    qt = min(length, 2048 if length <= 4096 else 1024)
    nb, nsub, nt = length // QB, qt // QB, length // qt
    chunk = min(length, 4096)
    nchunk = length // chunk

    def body(q_ref, k_ref, v_ref, do_ref, ld_ref, b_ref, dq_ref, dkv_hbm, dsum_ref, dk_acc, dv_acc, stage, sems):
        p_id, r, t = pl.program_id(0), pl.program_id(1), pl.program_id(2)
        lo = lax.broadcasted_iota(jnp.int32, (QB, 128), 1) < HEAD_DIM

        @pl.when(t == 0)
        def _():
            dk_acc[...] = jnp.zeros_like(dk_acc)
            dv_acc[...] = jnp.zeros_like(dv_acc)

        @pl.when((t == 0) & (r == 0))
        def _():
            dsum_ref[...] = jnp.zeros_like(dsum_ref)

        nt_dims = (((1,), (1,)), ((), ()))
        tn_dims = (((0,), (0,)), ((), ()))
        coords, qqs, dds, logits, dps, lcols, dcols = [], [], [], [], [], [], []
        for i in range(nsub):
            q0, start, variant = _block_coords(t, i, nsub, nb, length)
            qq = _split_heads(q_ref[q0:q0 + QB, :], lo)
            dd = _split_heads(do_ref[q0:q0 + QB, :], lo)
            k = k_ref[pl.ds(start, KB), :]
            v = v_ref[pl.ds(start, KB), :]
            logits.append(lax.dot_general(qq, k, nt_dims, preferred_element_type=F32) + b_ref[variant])
            dps.append(lax.dot_general(dd, v, nt_dims, preferred_element_type=F32))
            lcols.append(_col_pair(ld_ref, q0, 0))
            dcols.append(_col_pair(ld_ref, q0, HEAD_DIM // 2))
            coords.append((q0, start, variant))
            qqs.append(qq)
            dds.append(dd)
        p = jnp.exp(jnp.concatenate(logits, axis=0) - jnp.concatenate(lcols, axis=0))
        ds = p * (jnp.concatenate(dps, axis=0) - jnp.concatenate(dcols, axis=0))
        pb = p.astype(BF16)
        dsb = ds.astype(BF16)
        middle = None
        for i in range(nsub):
            q0, start, variant = coords[i]
            rows = slice(2 * QB * i, 2 * QB * (i + 1))
            if 0 < i < nsub - 1:
                middle = ds[rows] if middle is None else middle + ds[rows]
            else:
                dsum_ref[variant] += ds[rows]
            dqq = jnp.dot(dsb[rows], k_ref[pl.ds(start, KB), :], preferred_element_type=F32)
            dq_ref[q0:q0 + QB, :] = jnp.where(lo, dqq[0:QB], dqq[QB:2 * QB]).astype(BF16)
            dk_acc[pl.ds(start, KB), :] += lax.dot_general(dsb[rows], qqs[i], tn_dims, preferred_element_type=F32)
            dv_acc[pl.ds(start, KB), :] += lax.dot_general(pb[rows], dds[i], tn_dims, preferred_element_type=F32)
        if middle is not None:
            dsum_ref[1] += middle

        @pl.when(t == nt - 1)
        def _():
            def copy(k):
                which, c = k // nchunk, k % nchunk
                rows = pl.ds(c * chunk, chunk)
                return pltpu.make_async_copy(stage.at[k % 2], dkv_hbm.at[r, p_id, which, rows, :], sems.at[k % 2])

            for k in range(2 * nchunk):
                if k < 2:
                    @pl.when((p_id > 0) | (r > 0))
                    def _():
                        copy(k).wait()
                else:
                    copy(k).wait()
                acc = (dk_acc, dv_acc)[k // nchunk]
                stage[k % 2] = acc[pl.ds((k % nchunk) * chunk, chunk), :].astype(BF16)
                copy(k).start()

            @pl.when((p_id == N_PAIR - 1) & (r == r_cls - 1))
            def _():
                copy(0).wait()
                copy(1).wait()

    qspec = pl.BlockSpec((None, qt, 128), lambda p, r, t: (r, t, p))
    return pl.pallas_call(
        body, name=name, grid=(N_PAIR, r_cls, nt),
        out_shape=(jax.ShapeDtypeStruct((r_cls, length, 512), BF16),
                   jax.ShapeDtypeStruct((r_cls, N_PAIR, 2, length, 128), BF16),
                   jax.ShapeDtypeStruct((N_PAIR, 3, 2 * QB, KB), F32)),
        in_specs=[qspec,
                  pl.BlockSpec((None, length, 128), lambda p, r, t: (r, 0, 4 + p)),
                  pl.BlockSpec((None, length, 128), lambda p, r, t: (r, 0, p)),
                  qspec, qspec,
                  pl.BlockSpec((None, 3, None, 2 * QB, KB), lambda p, r, t: (gi, 0, p, 0, 0))],
        out_specs=(qspec, pl.BlockSpec(memory_space=pl.ANY),
                   pl.BlockSpec((None, 3, 2 * QB, KB), lambda p, r, t: (p, 0, 0, 0))),
        scratch_shapes=[pltpu.VMEM((length, 128), F32), pltpu.VMEM((length, 128), F32),
                        pltpu.VMEM((2, chunk, 128), BF16), pltpu.SemaphoreType.DMA((2,))],
        compiler_params=_params(("arbitrary", "arbitrary", "arbitrary")),
    )(qkn_l, qkn_l, v_l, do_l, ld_l, bias)


def _bwd_tail(dq_g, dkv_g, qkr, qkw, dza, dgbz, dzc, cg, cw, wblk, x, norm_w, dout, b256):
    s = x.shape[0]
    tm = TM_COMBINE
    hb = 16
    nt = s // tm

    def body(dq1, dq4, dq16, dkv1, dkv4, dkv16, qkr_ref, qkw_ref, dza_ref, dgbz_ref, dzc_ref,
             dzp_ref, dzn_ref, u_ref, gc_ref, cw_ref, w_ref, x_ref, nw_ref, dout_ref, b_ref,
             gx_o, dproj_o, dnw_o, dqkw_o, slab, mid):
        i = pl.program_id(0)

        def nat_q(ref, d):
            return _gather_classes(slab, lambda r, j: ref[r, :, 128 * j:128 * (j + 1)], 4, d, mid)

        def nat_kv(ref, d, which):
            return _gather_classes(slab, lambda r, j: ref[r, j, which], 4, d, mid)

        @pl.when(i == 0)
        def _():
            dnw_o[...] = jnp.zeros_like(dnw_o)
            dqkw_o[...] = jnp.zeros_like(dqkw_o)

        dzc = dzc_ref[...].astype(F32)
        d_prev = jnp.where(i == 0, 0.0, dzp_ref[hb - 1:hb, :].astype(F32))
        d_next = jnp.where(i == nt - 1, 0.0, dzn_ref[0:1, :].astype(F32))
        rows = lax.broadcasted_iota(jnp.int32, (tm, 512), 0)
        d_up = jnp.where(rows == 0, d_prev, pltpu.roll(dzc, 1, 0))
        d_dn = jnp.where(rows == tm - 1, d_next, pltpu.roll(dzc, tm - 1, 0))
        dt = cw_ref[0:1, :] * d_dn + cw_ref[1:2, :] * dzc + cw_ref[2:3, :] * d_up
        u = u_ref[...].astype(F32)
        gc = gc_ref[...].astype(F32)
        dproj_o[:, 0:512] = (dt * gc).astype(BF16)
        dproj_o[:, 512:1024] = dgbz_ref[:, 0:512]
        dproj_o[:, 1024:1536] = (dt * u).astype(BF16)
        dproj_o[:, 1536:2048] = dgbz_ref[:, 512:1024]

        dqn = (dq1[0].astype(F32) + nat_q(dq4, 4) + nat_q(dq16, 16)) * (1.0 / 8.0)
        dk1 = jnp.concatenate([dkv1[0, j, 0] for j in range(N_PAIR)], axis=1)
        dv1 = jnp.concatenate([dkv1[0, j, 1] for j in range(N_PAIR)], axis=1)
        dkn = dk1 + nat_kv(dkv4, 4, 0) + nat_kv(dkv16, 16, 0)
        dvn = dv1 + nat_kv(dkv4, 4, 1) + nat_kv(dkv16, 16, 1)
        g = jnp.concatenate([dqn, dkn], axis=1) * qkw_ref[...]
        raw = qkr_ref[...].astype(F32)
        rr = lax.rsqrt(_group_sum(raw * raw, b_ref, split=False) * (1.0 / HEAD_DIM) + EPS)
        proj_gq = _group_sum(g * raw, b_ref) * (1.0 / HEAD_DIM)
        draw = rr * g - raw * (rr * rr * rr) * proj_gq
        dqkw_o[...] += jnp.sum(jnp.concatenate([dqn, dkn], axis=1) * raw * rr, axis=0, keepdims=True)
        dproj_o[:, 2048:3072] = draw.astype(BF16)
        dproj_o[:, 3072:3584] = dvn.astype(BF16)
        dproj_o[:, 3584:4096] = dza_ref[...]

        nt_dims = (((1,), (1,)), ((), ()))
        dh = lax.dot_general(dproj_o[:, 0:1024], w_ref[0], nt_dims, preferred_element_type=F32)
        for b in range(1, 4):
            dh += lax.dot_general(dproj_o[:, 1024 * b:1024 * b + 1024], w_ref[b], nt_dims,
                                  preferred_element_type=F32)

        xf = x_ref[...]
        r = lax.rsqrt(jnp.mean(xf * xf, axis=-1, keepdims=True) + EPS)
        gh = dh * nw_ref[...]
        dnw_o[...] += jnp.sum(dh * xf * r, axis=0, keepdims=True)
        mean_gx = jnp.mean(gh * xf, axis=-1, keepdims=True)
        gx_o[...] = dout_ref[...] + r * gh - xf * (r * r * r) * mean_gx

    row = lambda w, j=0: pl.BlockSpec((tm, w), lambda i: (i, j))
    full = lambda shp: pl.BlockSpec(shp, lambda i: (0,) * len(shp))
    prev = pl.BlockSpec((hb, 512), lambda i: (jnp.maximum(i * (tm // hb) - 1, 0), 0))
    nxt = pl.BlockSpec((hb, 512), lambda i: (jnp.minimum((i + 1) * (tm // hb), s // hb - 1), 0))
    return pl.pallas_call(
        body, name="bwd_tail", grid=(nt,),
        out_shape=(jax.ShapeDtypeStruct((s, 1024), F32), jax.ShapeDtypeStruct((s, 4096), BF16),
                   jax.ShapeDtypeStruct((1, 1024), F32), jax.ShapeDtypeStruct((1, 1024), F32)),
        in_specs=[_class_spec(d, 512, tm) for d in DILATIONS]
        + [pl.BlockSpec((d, N_PAIR, 2, tm // d, 128), lambda i: (0, 0, 0, i, 0)) for d in DILATIONS]
        + [row(1024), full((1, 1024)), row(512), row(1024), row(512), prev, nxt,
           row(512, 0), row(512, 2), full((8, 512)), _resident((4, 1024, 1024)), row(1024),
           full((1, 1024)), row(1024), full((256, 256))],
        out_specs=(row(1024), row(4096), full((1, 1024)), full((1, 1024))),
        scratch_shapes=[pltpu.VMEM((4, tm, 128), F32), pltpu.VMEM((4, tm, 128), F32)],
        compiler_params=_params(("arbitrary",)),
    )(*dq_g, *dkv_g, qkr, qkw, dza, dgbz, dzc, dzc, dzc, cg, cg, cw, wblk, x, norm_w, dout, b256)


def _wgrad(a, b, row_blocked, name):
    s, m = a.shape
    n = b.shape[1]
    tk = 1024
    ncol = min(n, 2048)
    nj, nk = n // ncol, s // tk

    def body(a_ref, b_ref, o_ref, acc):
        kk = pl.program_id(1)

        @pl.when(kk == 0)
        def _():
            acc[...] = jnp.zeros_like(acc)

        acc[...] += lax.dot_general(a_ref[...], b_ref[...].astype(BF16), (((0,), (0,)), ((), ())),
                                    preferred_element_type=F32)

        @pl.when(kk == nk - 1)
        def _():
            blocks, _, rows, _ = o_ref.shape
            for blk in range(blocks):
                for half in range(2):
                    if row_blocked:
                        r0 = (2 * blk + half) * rows
                        o_ref[blk, half] = acc[r0:r0 + rows, :].astype(BF16)
                    else:
                        o_ref[blk, half] = acc[half * rows:(half + 1) * rows,
                                               1024 * blk:1024 * (blk + 1)].astype(BF16)

    if row_blocked:
        out_shape = jax.ShapeDtypeStruct((4, 2, m // 8, 1024), BF16)
        out_spec = pl.BlockSpec((4, 2, m // 8, 1024), lambda j, k: (0, 0, 0, 0))
    else:
        out_shape = jax.ShapeDtypeStruct((n // 1024, 2, m // 2, 1024), BF16)
        out_spec = pl.BlockSpec((ncol // 1024, 2, m // 2, 1024), lambda j, k: (j, 0, 0, 0))
    return pl.pallas_call(
        body, name=name, grid=(nj, nk),
        out_shape=out_shape,
        in_specs=[pl.BlockSpec((tk, m), lambda j, k: (k, 0)), pl.BlockSpec((tk, ncol), lambda j, k: (k, j))],
        out_specs=out_spec,
        scratch_shapes=[pltpu.VMEM((m, ncol), F32)],
        compiler_params=_params(("parallel", "arbitrary")),
    )(a, b)


def _dbias(dsums, onehot_all):
    def body(ds1_ref, ds4_ref, ds16_ref, oh_ref, o_ref):
        @pl.when(pl.program_id(0) == 0)
        def _():
            o_ref[...] = jnp.zeros_like(o_ref)

        hrow = lax.broadcasted_iota(jnp.int32, (8, KB), 0)
        flip = (lax.broadcasted_iota(jnp.int32, (QB, QB), 0)
                + lax.broadcasted_iota(jnp.int32, (QB, QB), 1) == QB - 1).astype(F32)

        def diagonal_sums(tile):
            rev = jnp.dot(flip, tile, preferred_element_type=F32, precision=lax.Precision.HIGHEST)
            sums = jnp.sum(pltpu.roll(rev, 0, 1, stride=1, stride_axis=0), axis=0, keepdims=True)
            return pltpu.roll(sums, KB - (QB - 1), 1)

        for g, ds_ref in enumerate((ds1_ref, ds4_ref, ds16_ref)):
            diag = jnp.zeros((8, KB), F32)
            for p in range(N_PAIR):
                diag = jnp.where(hrow == 2 * p, diagonal_sums(ds_ref[p, 0:QB, :]), diag)
                diag = jnp.where(hrow == 2 * p + 1, diagonal_sums(ds_ref[p, QB:2 * QB, :]), diag)
            o_ref[...] += jnp.dot(diag, oh_ref[g], preferred_element_type=F32, precision=lax.Precision.HIGHEST)

    ds_spec = pl.BlockSpec((N_PAIR, None, 2 * QB, KB), lambda v: (0, v, 0, 0))
    return pl.pallas_call(
        body, name="dbias", grid=(3,),
        out_shape=jax.ShapeDtypeStruct((8, 128), F32),
        in_specs=[ds_spec, ds_spec, ds_spec, pl.BlockSpec((3, None, KB, 128), lambda v: (0, v, 0, 0))],
        out_specs=pl.BlockSpec((8, 128), lambda v: (0, 0)),
        compiler_params=_params(("arbitrary",)),
    )(*dsums, onehot_all)


def _gsync(pw_in, pw_out, small):
    hin, hout = pw_in.shape[2], pw_out.shape[2]
    nsmall = small.shape[0]

    def body(pin_hbm, pout_hbm, small_ref, gin_o, gout_o, small_o,
             mine_in, recv_in, sbuf_in, rbuf_in, mine_out, recv_out, sbuf_out, rbuf_out, gather,
             lsem, asend, arecv, bsend, brecv, csend, crecv, ssend, srecv):
        x, y, c = lax.axis_index("x"), lax.axis_index("y"), lax.axis_index("c")
        b = 2 * x + y
        dev = 4 * x + 2 * y + c
        sib = (x, y, 1 - c)

        def rcopy(src, dst, ssem, rsem, to):
            return pltpu.make_async_remote_copy(src_ref=src, dst_ref=dst, send_sem=ssem, recv_sem=rsem,
                                                device_id=to, device_id_type=MESH)

        gather[dev] = small_ref[...]
        s_sends = []
        for k in range(1, 8):
            to = (x ^ (k >> 2), y ^ ((k >> 1) & 1), c ^ (k & 1))
            cp = rcopy(gather.at[dev], gather.at[dev], ssend.at[k - 1], srecv.at[k - 1], to)
            cp.start()
            s_sends.append(cp)

        a_in = rcopy(pin_hbm.at[:, 1 - c], recv_in, asend.at[0], arecv.at[0], sib)
        a_out = rcopy(pout_hbm.at[:, 1 - c], recv_out, asend.at[1], arecv.at[1], sib)
        a_in.start()
        a_out.start()
        l_in = pltpu.make_async_copy(pin_hbm.at[:, c], mine_in, lsem.at[0])
        l_out = pltpu.make_async_copy(pout_hbm.at[:, c], mine_out, lsem.at[1])
        l_in.start()
        l_out.start()
        l_in.wait()
        l_out.wait()

        def stage_b(a_cp, mine, recv, sbuf, rbuf, base):
            a_cp.wait_recv()
            sends = []
            for k in (1, 2, 3):
                bk = b ^ k
                sbuf[k - 1] = (mine[bk].astype(F32) + recv[bk].astype(F32)).astype(BF16)
                cp = rcopy(sbuf.at[k - 1], rbuf.at[k - 1], bsend.at[base + k - 1], brecv.at[base + k - 1],
                           (x ^ (k >> 1), y ^ (k & 1), c))
                cp.start()
                sends.append(cp)
            return sends

        b_in = stage_b(a_in, mine_in, recv_in, sbuf_in, rbuf_in, 0)
        b_out = stage_b(a_out, mine_out, recv_out, sbuf_out, rbuf_out, 3)

        def stage_c(b_sends, mine, recv, rbuf, g_o, half, idx):
            acc = mine[b].astype(F32) + recv[b].astype(F32)
            for k in (1, 2, 3):
                b_sends[k - 1].wait_recv()
                acc = acc + rbuf[k - 1].astype(F32)
            rows = g_o.at[pl.ds(pl.multiple_of(c * half, half), half), :]
            g_o[pl.ds(pl.multiple_of(c * half, half), half), :] = acc
            cp = rcopy(rows, rows, csend.at[idx], crecv.at[idx], sib)
            cp.start()
            return cp

        c_in = stage_c(b_in, mine_in, recv_in, rbuf_in, gin_o, hin, 0)
        c_out = stage_c(b_out, mine_out, recv_out, rbuf_out, gout_o, hout, 1)

        for cp in s_sends:
            cp.wait_recv()
        tot = gather[0]
        for d in range(1, 8):
            tot = tot + gather[d]
        small_o[...] = tot

        for g_o, half, idx in ((gin_o, hin, 0), (gout_o, hout, 1)):
            other = g_o.at[pl.ds(pl.multiple_of((1 - c) * half, half), half), :]
            rcopy(other, other, csend.at[idx], crecv.at[idx], sib).wait_recv()
        for cp in s_sends + [a_in, a_out] + b_in + b_out + [c_in, c_out]:
            cp.wait_send()

    vm = pl.BlockSpec(memory_space=pltpu.VMEM)
    hbm = pl.BlockSpec(memory_space=pl.ANY)
    return pl.pallas_call(
        body, name="gsync",
        out_shape=(jax.ShapeDtypeStruct((2 * hin, 1024), F32), jax.ShapeDtypeStruct((2 * hout, 1024), F32),
                   jax.ShapeDtypeStruct((nsmall, 128), F32)),
        in_specs=[hbm, hbm, vm], out_specs=(vm, vm, vm),
        scratch_shapes=[pltpu.VMEM((4, hin, 1024), BF16), pltpu.VMEM((4, hin, 1024), BF16),
                        pltpu.VMEM((3, hin, 1024), BF16), pltpu.VMEM((3, hin, 1024), BF16),
                        pltpu.VMEM((4, hout, 1024), BF16), pltpu.VMEM((4, hout, 1024), BF16),
                        pltpu.VMEM((3, hout, 1024), BF16), pltpu.VMEM((3, hout, 1024), BF16),
                        pltpu.VMEM((8, nsmall, 128), F32),
                        pltpu.SemaphoreType.DMA((2,)),
                        pltpu.SemaphoreType.DMA((2,)), pltpu.SemaphoreType.DMA((2,)),
                        pltpu.SemaphoreType.DMA((6,)), pltpu.SemaphoreType.DMA((6,)),
                        pltpu.SemaphoreType.DMA((2,)), pltpu.SemaphoreType.DMA((2,)),
                        pltpu.SemaphoreType.DMA((7,)), pltpu.SemaphoreType.DMA((7,))],
        compiler_params=_params(),
    )(pw_in, pw_out, small)


def _adamw_math(w, g, m, v):
    m = ADAM_B1 * m + (1.0 - ADAM_B1) * g
    v = ADAM_B2 * v + (1.0 - ADAM_B2) * (g * g)
    m_hat = m / (1.0 - ADAM_B1 ** ADAM_STEP)
    v_hat = v / (1.0 - ADAM_B2 ** ADAM_STEP)
    delta = -ADAM_LR * (m_hat / (jnp.sqrt(v_hat) + ADAM_EPS) + ADAM_WD * w)
    return delta, m, v


def _adamw(w, g, m, v, name):
    rows, cols = w.shape
    tr = 256 if rows % 256 == 0 else rows

    def body(w_ref, g_ref, m_ref, v_ref, g_o, d_o, m_o, v_o):
        g = g_ref[...]
        d, m2, v2 = _adamw_math(w_ref[...], g, m_ref[...], v_ref[...])
        g_o[...] = g
        d_o[...] = d
        m_o[...] = m2
        v_o[...] = v2

    spec = pl.BlockSpec((tr, cols), lambda i: (i, 0))
    shp = jax.ShapeDtypeStruct((rows, cols), F32)
    return pl.pallas_call(
        body, name=name, grid=(rows // tr,), out_shape=(shp, shp, shp, shp),
        in_specs=[spec] * 4, out_specs=(spec, spec, spec, spec),
        compiler_params=_params(("parallel",)),
    )(w, g, m, v)


def _fold_heads(dqkw):
    def body(x_ref, o_ref):
        xs = x_ref[...]
        sq = xs[0:1] + xs[1:2] + xs[2:3] + xs[3:4]
        sk = xs[4:5] + xs[5:6] + xs[6:7] + xs[7:8]
        both = jnp.concatenate([sq, sk], axis=0)
        o_ref[...] = both + pltpu.roll(both, HEAD_DIM, 1)

    vm = pl.BlockSpec(memory_space=pltpu.VMEM)
    return pl.pallas_call(body, name="fold_heads", out_shape=jax.ShapeDtypeStruct((2, 128), F32),
                          in_specs=[vm], out_specs=vm, compiler_params=_params())(dqkw)


def kernel(x, norm_w, w_in, conv_w, conv_b, q_norm_w, k_norm_w, rel_bias, w_out, loss_target, m_norm_w, m_w_in, m_conv_w, m_conv_b, m_q_norm_w, m_k_norm_w, m_rel_bias, m_w_out, v_norm_w, v_w_in, v_conv_w, v_conv_b, v_q_norm_w, v_k_norm_w, v_rel_bias, v_w_out):
    x2 = x[0]
    tgt = loss_target[0]
    blk = 2 * lax.axis_index("x") + lax.axis_index("y")

    conv_w8 = jnp.pad(conv_w, ((0, 5), (0, 0)))
    wblk, woutblk, cwblk = _wgather(w_in, w_out, conv_w8)
    wout_full = woutblk.reshape(1024, 1024)
    cw_full = cwblk.transpose(1, 0, 2).reshape(8, 512)

    qkw = jnp.concatenate([jnp.tile(q_norm_w, 8) * 0.125, jnp.tile(k_norm_w, 8)])[None, :]
    qkw_raw = jnp.concatenate([jnp.tile(q_norm_w, 8), jnp.tile(k_norm_w, 8)])[None, :]
    gidx = jnp.arange(256) // HEAD_DIM
    b256 = (gidx[:, None] == gidx[None, :]).astype(BF16)

    h, cg, qkr, qkn, vz, qkn4, v4, qkn16, v16, xmt = _proj(x2, tgt, norm_w[None, :], wblk, qkw, b256)

    biases = _bias_tables(rel_bias)
    qkn_l = [qkn[None], qkn4, qkn16]
    v_l = [vz[None], v4, v16]
    o_g, lse_g = [], []
    for gi, d in enumerate(DILATIONS):
        o_l, lse_l = _attn_fwd(qkn_l[gi], v_l[gi], biases, gi, f"attn_fwd_d{d}")
        o_g.append(o_l)
        lse_g.append(lse_l)

    (y, dout, ld1, do1, dza, dgbz, dzc, loss_p, dcb, dcw, do4, ld4, do16, ld16) = _combine(
        o_g, lse_g, cg, vz, xmt, wout_full, cw_full, conv_b[None, :], b256)

    dq_g, dkv_g, dsums = [], [], []
    for gi, (d, do_l, ld_l) in enumerate(zip(DILATIONS, (do1, do4, do16), (ld1, ld4, ld16))):
        dq_l, dkv_l, dsum = _attn_bwd(qkn_l[gi], v_l[gi], do_l, ld_l, biases, gi, f"attn_bwd_d{d}")
        dq_g.append(dq_l)
        dkv_g.append(dkv_l)
        dsums.append(dsum)

    grad_x, dproj, dnw, dqkw = _bwd_tail(dq_g, dkv_g, qkr, qkw_raw, dza, dgbz, dzc, cg, cw_full, wblk,
                                         x2, norm_w[None, :], dout, b256)

    pw_in = _wgrad(h, dproj, False, "wgrad_in")
    pw_out = _wgrad(y, dout, True, "wgrad_out")
    dbias8 = _dbias(dsums, jnp.stack([_diag_bucket_onehot(d) for d in DILATIONS], axis=0))

    small = jnp.concatenate([dnw.reshape(8, 128), dcb.reshape(4, 128), dqkw.reshape(8, 128),
                             dcw[0:3].reshape(12, 128), dbias8, jnp.pad(loss_p, ((0, 7), (0, 0)))], axis=0)
    g_win, g_wout, gsmall = _gsync(pw_in, pw_out, small)

    g_nw = gsmall[0:8].reshape(1024)
    g_cb = gsmall[8:12].reshape(512)
    folded = _fold_heads(gsmall[12:20])
    g_qw, g_kw = folded[0, 0:64], folded[1, 0:64]
    g_cw = lax.dynamic_slice(gsmall[20:32].reshape(3, 512), (0, blk * 128), (3, 128))
    g_rb = gsmall[32:40][:, 0:32].T
    loss = gsmall[40, 0]

    g_win, d_win, nm_win, nv_win = _adamw(w_in, g_win, m_w_in, v_w_in, "adamw_w_in")
    g_wout, d_wout, nm_wout, nv_wout = _adamw(w_out, g_wout, m_w_out, v_w_out, "adamw_w_out")

    def pack(parts):
        rows = [parts[0].reshape(8, 128), parts[1].reshape(4, 128),
                jnp.pad(parts[2], (0, 64))[None, :], jnp.pad(parts[3], (0, 64))[None, :],
                parts[4], jnp.pad(parts[5].T, ((0, 0), (0, 96)))]
        return jnp.concatenate(rows, axis=0)

    ws = pack([norm_w, conv_b, q_norm_w, k_norm_w, conv_w, rel_bias])
    gs = pack([g_nw, g_cb, g_qw, g_kw, g_cw, g_rb])
    ms = pack([m_norm_w, m_conv_b, m_q_norm_w, m_k_norm_w, m_conv_w, m_rel_bias])
    vs = pack([v_norm_w, v_conv_b, v_q_norm_w, v_k_norm_w, v_conv_w, v_rel_bias])
    rpad = lambda a: jnp.pad(a, ((0, 7), (0, 0)))
    _, d_s, nm_s, nv_s = _adamw(rpad(ws), rpad(gs), rpad(ms), rpad(vs), "adamw_small")

    def unpack(a):
        return (a[0:8].reshape(1024), a[12:13, 0:64].reshape(64), a[13:14, 0:64].reshape(64),
                a[14:17], a[8:12].reshape(512), a[17:25, 0:32].T)

    def ordered(nw, win, cw, cb, qw, kw, rb, wout):
        return (nw, win, cw, cb, qw, kw, rb, wout)

    g_un = (g_nw, g_qw, g_kw, g_cw, g_cb, g_rb)
    outs = [loss, grad_x[None]]
    for un, win_v, wout_v in ((g_un, g_win, g_wout), (unpack(d_s), d_win, d_wout),
                              (unpack(nm_s), nm_win, nm_wout), (unpack(nv_s), nv_win, nv_wout)):
        nw, qw, kw, cw, cb, rb = un
        outs.extend(ordered(nw, win_v, cw, cb, qw, kw, rb, wout_v))
    return tuple(outs)
```

```python
import math

import jax
import jax.numpy as jnp
from jax import lax
from jax.experimental import pallas as pl
from jax.experimental.pallas import tpu as pltpu

F32 = jnp.float32
BF16 = jnp.bfloat16
MESH = pl.DeviceIdType.MESH

D_MODEL = 1024
CONV_W = 512
ATTN_W = 512
HEAD_DIM = 64
N_PAIR = 4
DILATIONS = (1, 4, 16)
HALF = 64
QB = 128
KB = QB + 2 * HALF
NUM_BUCKETS = 32
MAX_DISTANCE = 1024
EPS = 1e-6
NEG = -1e30
ADAM_LR, ADAM_B1, ADAM_B2, ADAM_EPS, ADAM_WD, ADAM_STEP = 0.001, 0.9, 0.999, 1e-08, 0.01, 10
VMEM_LIMIT = 48 << 20


def _params(sem=None, vmem=VMEM_LIMIT, **kw):
    if sem is not None:
        kw["dimension_semantics"] = sem
    return pltpu.CompilerParams(vmem_limit_bytes=vmem, **kw)


def _sigmoid(z):
    return 1.0 / (1.0 + jnp.exp(-z))


def _group_sum(val, b_ref, split=True):
    hi = val.astype(BF16)
    lo = (val - hi.astype(F32)).astype(BF16) if split else None
    outs = []
    for j in range(val.shape[1] // 256):
        sl = slice(256 * j, 256 * j + 256)
        part = jnp.dot(hi[:, sl], b_ref[...], preferred_element_type=F32)
        if split:
            part = part + jnp.dot(lo[:, sl], b_ref[...], preferred_element_type=F32)
        outs.append(part)
    return outs[0] if len(outs) == 1 else jnp.concatenate(outs, axis=1)


def _t5_bucket(rel):
    half_b = NUM_BUCKETS // 2
    max_exact = half_b // 2
    ret = jnp.where(rel > 0, half_b, 0)
    n = jnp.abs(rel)
    nf = jnp.maximum(n, 1).astype(F32)
    large = max_exact + (jnp.log(nf / max_exact) / math.log(MAX_DISTANCE / max_exact)
                         * (half_b - max_exact)).astype(jnp.int32)
    large = jnp.minimum(large, half_b - 1)
    return ret + jnp.where(n < max_exact, n, large)


def _bias_tables(rel_bias):
    rows = []
    key = jnp.arange(KB)
    for dilation in DILATIONS:
        for variant in range(3):
            off = (0, HALF, 2 * HALF)[variant]
            rel = ((key - off + KB // 2) % KB) - KB // 2
            bkt = _t5_bucket(jnp.clip(rel, -HALF, HALF) * dilation)
            rows.append(jnp.where(jnp.abs(rel) <= HALF, bkt, -1))
    bkt_all = jnp.broadcast_to(jnp.stack(rows, axis=0).astype(jnp.int32)[:, None, :], (9, 8, KB))

    def body(rb_ref, bkt_ref, o_ref):
        bkt = bkt_ref[...]
        off = (pl.program_id(0) % 3) * HALF
        rel = (lax.broadcasted_iota(jnp.int32, (QB, KB), 1) - lax.broadcasted_iota(jnp.int32, (QB, KB), 0)) - off
        band = jnp.abs(rel) <= HALF
        for h in range(8):
            acc = jnp.full((8, KB), NEG, F32)
            for b in range(NUM_BUCKETS):
                acc = jnp.where(bkt == b, rb_ref[b, h], acc)
            rolled = pltpu.roll(jnp.broadcast_to(acc[0:1], (QB, KB)), 0, 1, stride=1, stride_axis=0)
            o_ref[h] = jnp.where(band, rolled, NEG)

    out = pl.pallas_call(
        body, name="bias_tables", grid=(9,),
        out_shape=jax.ShapeDtypeStruct((9, 8, QB, KB), F32),
        in_specs=[pl.BlockSpec(memory_space=pltpu.SMEM), pl.BlockSpec((None, 8, KB), lambda i: (i, 0, 0))],
        out_specs=pl.BlockSpec((None, 8, QB, KB), lambda i: (i, 0, 0, 0)),
        compiler_params=_params(("parallel",)),
    )(rel_bias, bkt_all)
    return out.reshape(3, 3, N_PAIR, 2 * QB, KB)


def _diag_bucket_onehot(dilation):
    out = []
    c = jnp.arange(KB)
    for variant in range(3):
        off = (0, HALF, 2 * HALF)[variant]
        rel = ((c - off + 128) % 256) - 128
        band = jnp.abs(rel) <= HALF
        bkt = _t5_bucket(jnp.clip(rel, -HALF, HALF) * dilation)
        oh = (bkt[:, None] == jnp.arange(128)[None, :]) & band[:, None]
        out.append(oh.astype(F32))
    return jnp.stack(out, axis=0)


def _wgather(w_in, w_out, conv_w):
    rin, rout = w_in.shape[0] // 2, w_out.shape[0] // 2

    def body(win_ref, wout_ref, cw_ref, win_o, wout_o, cw_o, send_sems, recv_sems):
        x, y, c = lax.axis_index("x"), lax.axis_index("y"), lax.axis_index("c")
        b = 2 * x + y
        win_o[b] = win_ref[...].astype(BF16)
        wout_o[b] = wout_ref[...].astype(BF16)
        cw_o[b] = cw_ref[...]

        def peer(k):
            return (x ^ (k >> 1), y ^ (k & 1))

        def piece(ref, blk, half_rows, core):
            return ref.at[blk, pl.ds(core * half_rows, half_rows), :]

        def copy(sem, src, dst, to):
            return pltpu.make_async_remote_copy(src_ref=src, dst_ref=dst, send_sem=send_sems.at[sem],
                                                recv_sem=recv_sems.at[sem], device_id=to, device_id_type=MESH)

        sends = []
        for k in (1, 2, 3):
            px, py = peer(k)
            sends.append(copy(k - 1, piece(win_o, b, rin, c), piece(win_o, b, rin, c), (px, py, c)))
            sends.append(copy(3 + k - 1, piece(wout_o, b, rout, c), piece(wout_o, b, rout, c), (px, py, c)))
            sends.append(copy(6 + k - 1, cw_o.at[b], cw_o.at[b], (px, py, c)))
        for cp in sends:
            cp.start()
        fwd = []
        for k in (1, 2, 3):
            px, py = peer(k)
            bk = 2 * px + py
            copy(k - 1, piece(win_o, bk, rin, c), piece(win_o, bk, rin, c), (px, py, c)).wait_recv()
            f = copy(9 + k - 1, piece(win_o, bk, rin, c), piece(win_o, bk, rin, c), (x, y, 1 - c))
            f.start()
            fwd.append(f)
            copy(3 + k - 1, piece(wout_o, bk, rout, c), piece(wout_o, bk, rout, c), (px, py, c)).wait_recv()
            f = copy(12 + k - 1, piece(wout_o, bk, rout, c), piece(wout_o, bk, rout, c), (x, y, 1 - c))
            f.start()
            fwd.append(f)
            copy(6 + k - 1, cw_o.at[bk], cw_o.at[bk], (px, py, c)).wait_recv()
        for k in (1, 2, 3):
            px, py = peer(k)
            bk = 2 * px + py
            copy(9 + k - 1, piece(win_o, bk, rin, 1 - c), piece(win_o, bk, rin, 1 - c), (x, y, 1 - c)).wait_recv()
            copy(12 + k - 1, piece(wout_o, bk, rout, 1 - c), piece(wout_o, bk, rout, 1 - c), (x, y, 1 - c)).wait_recv()
        for cp in sends + fwd:
            cp.wait_send()

    vm = pl.BlockSpec(memory_space=pltpu.VMEM)
    return pl.pallas_call(
        body, name="wgather",
        out_shape=(jax.ShapeDtypeStruct((4,) + w_in.shape, BF16),
                   jax.ShapeDtypeStruct((4,) + w_out.shape, BF16),
                   jax.ShapeDtypeStruct((4,) + conv_w.shape, F32)),
        in_specs=[vm, vm, vm], out_specs=(vm, vm, vm),
        scratch_shapes=[pltpu.SemaphoreType.DMA((15,)), pltpu.SemaphoreType.DMA((15,))],
        compiler_params=_params(),
    )(w_in, w_out, conv_w)


TM_MATMUL = 512
TM_COMBINE = 256


def _resident(shape):
    return pl.BlockSpec(shape, lambda i: (0,) * len(shape), pipeline_mode=pl.Buffered(1))


def _to_slabs(slab, val, j0=0):
    for j in range(val.shape[1] // 128):
        slab[j0 + j] = val[:, 128 * j:128 * (j + 1)]


def _scatter_classes(slab, j0, nj, out_ref, d, part=0, mid=None):
    tm = slab.shape[1]
    n = tm // d
    if d == 4:
        for r in range(d):
            for j in range(nj):
                out_ref[r, part * n:(part + 1) * n, 128 * j:128 * (j + 1)] = (
                    slab[j0 + j, pl.ds(r, n, stride=d), :].astype(out_ref.dtype))
        return
    q = tm // 4
    for lo in range(4):
        for j in range(nj):
            mid[j0 + j, lo * q:(lo + 1) * q, :] = slab[j0 + j, pl.ds(lo, q, stride=4), :]
    for hi in range(4):
        for lo in range(4):
            for j in range(nj):
                out_ref[4 * hi + lo, part * n:(part + 1) * n, 128 * j:128 * (j + 1)] = (
                    mid[j0 + j, pl.ds(lo * q + hi, n, stride=4), :].astype(out_ref.dtype))


def _gather_classes(slab, piece, nj, d, mid=None):
    tm = slab.shape[1]
    n = tm // d
    if d == 4:
        for r in range(d):
            for j in range(nj):
                slab[j, pl.ds(r, n, stride=d), :] = piece(r, j).astype(F32)
    else:
        q = tm // 4
        for hi in range(4):
            for lo in range(4):
                for j in range(nj):
                    mid[j, pl.ds(lo * q + hi, n, stride=4), :] = piece(4 * hi + lo, j).astype(F32)
        for lo in range(4):
            for j in range(nj):
                slab[j, pl.ds(lo, q, stride=4), :] = mid[j, lo * q:(lo + 1) * q, :]
    return jnp.concatenate([slab[j] for j in range(nj)], axis=1)


def _class_spec(d, width, tm):
    return pl.BlockSpec((d, tm // d, width), lambda i: (0, i, 0))


def _proj(x, tgt, norm_w, wblk, qkw, b256):
    s = x.shape[0]
    tm = TM_MATMUL
    nparts = 2
    tp = tm // nparts

    def body(x_ref, t_ref, nw_ref, w_ref, qkw_ref, b_ref, h_o, cg_o, qkr_o, qkn_o, vz_o, qkn4_o, v4_o, qkn16_o,
             v16_o, xmt_o, slabs, mids):
        for part in range(nparts):
            rows = slice(part * tp, (part + 1) * tp)
            slab = slabs.at[part]
            xf = x_ref[rows, :]
            xmt_o[rows, :] = xf - t_ref[rows, :]
            r = lax.rsqrt(jnp.mean(xf * xf, axis=-1, keepdims=True) + EPS)
            h = (xf * r * nw_ref[...]).astype(BF16)
            h_o[rows, :] = h
            p2 = jnp.dot(h, w_ref[2], preferred_element_type=F32)
            qkr_o[rows, :] = p2.astype(BF16)
            ss = _group_sum(p2 * p2, b_ref, split=False)
            rr = lax.rsqrt(ss * (1.0 / HEAD_DIM) + EPS)
            qkn = p2 * rr * qkw_ref[...]
            qkn_o[rows, :] = qkn.astype(BF16)
            _to_slabs(slab, qkn)
            p3 = jnp.dot(h, w_ref[3], preferred_element_type=F32)
            vz_o[rows, :] = p3.astype(BF16)
            _to_slabs(slab, p3[:, 0:512], 8)
            cg_o[rows, 0:1024] = jnp.dot(h, w_ref[0], preferred_element_type=F32).astype(BF16)
            cg_o[rows, 1024:2048] = jnp.dot(h, w_ref[1], preferred_element_type=F32).astype(BF16)
            for d, q_o, v_o in ((4, qkn4_o, v4_o), (16, qkn16_o, v16_o)):
                _scatter_classes(slab, 0, 8, q_o, d, part, mids.at[part])
                _scatter_classes(slab, 8, 4, v_o, d, part, mids.at[part])

    row = lambda w: pl.BlockSpec((tm, w), lambda i: (i, 0))
    full = lambda shp: pl.BlockSpec(shp, lambda i: (0,) * len(shp))
    nat = lambda w: jax.ShapeDtypeStruct((s, w), BF16)
    cls = lambda d, w: jax.ShapeDtypeStruct((d, s // d, w), BF16)
    return pl.pallas_call(
        body, name="proj", grid=(s // tm,),
        out_shape=(nat(1024), nat(2048), nat(1024), nat(1024), nat(1024),
                   cls(4, 1024), cls(4, 512), cls(16, 1024), cls(16, 512), jax.ShapeDtypeStruct((s, 1024), F32)),
        in_specs=[row(1024), row(1024), full((1, 1024)), _resident((4, 1024, 1024)), full((1, 1024)),
                  full((256, 256))],
        out_specs=(row(1024), row(2048), row(1024), row(1024), row(1024),
                   _class_spec(4, 1024, tm), _class_spec(4, 512, tm),
                   _class_spec(16, 1024, tm), _class_spec(16, 512, tm), row(1024)),
        scratch_shapes=[pltpu.VMEM((nparts, 12, tp, 128), F32), pltpu.VMEM((nparts, 12, tp, 128), F32)],
        compiler_params=_params(("parallel",)),
    )(x, tgt, norm_w, wblk, qkw, b256)


def _block_coords(t, i, nsub, nb, length):
    n = t * nsub + i
    q0 = i * QB
    start = pl.multiple_of(jnp.clip(n * QB - HALF, 0, length - KB), HALF)
    variant = jnp.where(n == 0, 0, jnp.where(n == nb - 1, 2, 1))
    return q0, start, variant


def _split_heads(a, lo):
    zero = jnp.zeros_like(a)
    return jnp.concatenate([jnp.where(lo, a, zero), jnp.where(lo, zero, a)], axis=0)


def _col_pair(ref, q0, lane):
    return jnp.concatenate([ref[pl.ds(q0, QB), lane:lane + 1],
                            ref[pl.ds(q0, QB), HEAD_DIM + lane:HEAD_DIM + lane + 1]], axis=0)


def _attn_fwd(qkn_l, v_l, bias, gi, name):
    r_cls, length, _ = qkn_l.shape
    qt = min(length, 2048)
    nb, nsub = length // QB, qt // QB

    def body(q_ref, k_ref, v_ref, b_ref, o_ref, lse_ref):
        t = pl.program_id(2)
        lo = lax.broadcasted_iota(jnp.int32, (QB, 128), 1) < HEAD_DIM

        starts, logits = [], []
        for i in range(nsub):
            _, start, variant = _block_coords(t, i, nsub, nb, length)
            qq = _split_heads(q_ref[i * QB:(i + 1) * QB, :], lo)
            k = k_ref[pl.ds(start, KB), :]
            logits.append(lax.dot_general(qq, k, (((1,), (1,)), ((), ())), preferred_element_type=F32)
                          + b_ref[variant])
            starts.append(start)
        lg = jnp.concatenate(logits, axis=0)
        m = jnp.max(lg, axis=-1, keepdims=True)
        p = jnp.exp(lg - m)
        pb = p.astype(BF16)
        l = jnp.sum(p, axis=-1, keepdims=True)
        lse = jnp.broadcast_to(m + jnp.log(l), (nsub * 2 * QB, 128))
        inv = 1.0 / l
        for i in range(nsub):
            rows = slice(2 * QB * i, 2 * QB * (i + 1))
            v = v_ref[pl.ds(starts[i], KB), :]
            pv = jnp.dot(pb[rows], v, preferred_element_type=F32) * inv[rows]
            o_ref[i * QB:(i + 1) * QB, :] = jnp.where(lo, pv[0:QB], pv[QB:2 * QB]).astype(BF16)
            ls = lse[rows]
            lse_ref[i * QB:(i + 1) * QB, :] = jnp.where(lo, ls[0:QB], ls[QB:2 * QB])

    return pl.pallas_call(
        body, name=name, grid=(N_PAIR, r_cls, length // qt),
        out_shape=(jax.ShapeDtypeStruct((r_cls, length, 512), BF16),
                   jax.ShapeDtypeStruct((r_cls, length, 512), F32)),
        in_specs=[pl.BlockSpec((None, qt, 128), lambda p, r, t: (r, t, p)),
                  pl.BlockSpec((None, length, 128), lambda p, r, t: (r, 0, 4 + p)),
                  pl.BlockSpec((None, length, 128), lambda p, r, t: (r, 0, p)),
                  pl.BlockSpec((None, 3, None, 2 * QB, KB), lambda p, r, t: (gi, 0, p, 0, 0))],
        out_specs=(pl.BlockSpec((None, qt, 128), lambda p, r, t: (r, t, p)),
                   pl.BlockSpec((None, qt, 128), lambda p, r, t: (r, t, p))),
        compiler_params=_params(("parallel", "parallel", "arbitrary")),
    )(qkn_l, qkn_l, v_l, bias)


def _combine(o_g, lse_g, cg, vz, xmt, wout, cw, cb, b256):
    s = xmt.shape[0]
    tm = TM_COMBINE
    hb = 16
    nt = s // tm

    def body(o1, o4, o16, l1, l4, l16, cg_ref, cgp_ref, cgn_ref, za_ref, xmt_ref, w_ref, cw_ref, cb_ref,
             b_ref, dout_o, ld1_o, do1_o, dza_o, dgbz_o, dzc_o, loss_o, dcb_o, dcw_o,
             do4_o, ld4_o, do16_o, ld16_o, pwout_o, slab, mid, wacc):
        i = pl.program_id(0)

        @pl.when(i == 0)
        def _():
            wacc[...] = jnp.zeros_like(wacc)
            loss_o[...] = jnp.zeros_like(loss_o)
            dcb_o[...] = jnp.zeros_like(dcb_o)
            dcw_o[...] = jnp.zeros_like(dcw_o)

        u = cg_ref[:, 0:512].astype(F32)
        gb = cg_ref[:, 512:1024].astype(F32)
        gc = cg_ref[:, 1024:1536].astype(F32)
        zc = cg_ref[:, 1536:2048].astype(F32)
        tt = gc * u
        t_prev = cgp_ref[hb - 1:hb, 0:512].astype(F32) * cgp_ref[hb - 1:hb, 1024:1536].astype(F32)
        t_next = cgn_ref[0:1, 0:512].astype(F32) * cgn_ref[0:1, 1024:1536].astype(F32)
        t_prev = jnp.where(i == 0, 0.0, t_prev)
        t_next = jnp.where(i == nt - 1, 0.0, t_next)
        rows = lax.broadcasted_iota(jnp.int32, (tm, 512), 0)
        t_up = jnp.where(rows == 0, t_prev, pltpu.roll(tt, 1, 0))
        t_dn = jnp.where(rows == tm - 1, t_next, pltpu.roll(tt, tm - 1, 0))
        w0, w1, w2 = cw_ref[0:1, :], cw_ref[1:2, :], cw_ref[2:3, :]
        zb = w0 * t_up + w1 * tt + w2 * t_dn + cb_ref[...]
        sg = _sigmoid(zc)
        sz = zc * sg
        y_conv = gb * zb * sz

        a1, p1 = l1[0], o1[0].astype(F32)
        a4 = _gather_classes(slab, lambda r, j: l4[r, :, 128 * j:128 * (j + 1)], 4, 4)
        p4 = _gather_classes(slab, lambda r, j: o4[r, :, 128 * j:128 * (j + 1)], 4, 4)
        a16 = _gather_classes(slab, lambda r, j: l16[r, :, 128 * j:128 * (j + 1)], 4, 16, mid)
        p16 = _gather_classes(slab, lambda r, j: o16[r, :, 128 * j:128 * (j + 1)], 4, 16, mid)
        m = jnp.maximum(jnp.maximum(a1, a4), a16)
        e1, e4, e16 = jnp.exp(a1 - m), jnp.exp(a4 - m), jnp.exp(a16 - m)
        den = e1 + e4 + e16
        lse = m + jnp.log(den)
        o = (e1 * p1 + e4 * p4 + e16 * p16) / den
        za = za_ref[...].astype(F32)
        sga = _sigmoid(za)
        sa = za * sga
        y = jnp.concatenate([y_conv, o * sa], axis=1).astype(BF16)

        diff = xmt_ref[...] + jnp.dot(y, w_ref[...], preferred_element_type=F32)
        loss_o[...] += (0.5 / D_MODEL) * jnp.sum(diff * diff)
        dout = diff * (1.0 / D_MODEL)
        dout_o[...] = dout
        dout_b = dout.astype(BF16)
        dy = lax.dot_general(dout_b, w_ref[...], (((1,), (1,)), ((), ())), preferred_element_type=F32)
        wacc[...] += lax.dot_general(y, dout_b, (((0,), (0,)), ((), ())), preferred_element_type=F32)

        @pl.when(i == nt - 1)
        def _():
            rows_out = pwout_o.shape[2]
            for blk in range(4):
                for half in range(2):
                    r0 = (2 * blk + half) * rows_out
                    pwout_o[blk, half] = wacc[r0:r0 + rows_out, :].astype(BF16)
        dyc, dya = dy[:, 0:512], dy[:, 512:1024]

        do = dya * sa
        dza_o[...] = (dya * o * (sga * (1.0 + za * (1.0 - sga)))).astype(BF16)
        lane = lax.broadcasted_iota(jnp.int32, (tm, 512), 1)
        ld = jnp.where((lane & (HEAD_DIM - 1)) < HEAD_DIM // 2, lse, _group_sum(do * o, b_ref))
        do1_o[0] = do.astype(BF16)
        ld1_o[0] = ld
        _to_slabs(slab, do)
        _scatter_classes(slab, 0, 4, do4_o, 4)
        _scatter_classes(slab, 0, 4, do16_o, 16, 0, mid)
        _to_slabs(slab, ld)
        _scatter_classes(slab, 0, 4, ld4_o, 4)
        _scatter_classes(slab, 0, 4, ld16_o, 16, 0, mid)

        dzc = dyc * sz * gb
        dzc_o[...] = dzc.astype(BF16)
        dgbz_o[:, 0:512] = (dyc * sz * zb).astype(BF16)
        dgbz_o[:, 512:1024] = (dyc * gb * zb * (sg * (1.0 + zc * (1.0 - sg)))).astype(BF16)
        dcb_o[...] += jnp.sum(dzc, axis=0, keepdims=True)
        dcw_o[0:1, :] += jnp.sum(dzc * t_up, axis=0, keepdims=True)
        dcw_o[1:2, :] += jnp.sum(dzc * tt, axis=0, keepdims=True)
        dcw_o[2:3, :] += jnp.sum(dzc * t_dn, axis=0, keepdims=True)

    row = lambda w, j=0: pl.BlockSpec((tm, w), lambda i: (i, j))
    full = lambda shp: pl.BlockSpec(shp, lambda i: (0,) * len(shp))
    prev = pl.BlockSpec((hb, 2048), lambda i: (jnp.maximum(i * (tm // hb) - 1, 0), 0))
    nxt = pl.BlockSpec((hb, 2048), lambda i: (jnp.minimum((i + 1) * (tm // hb), s // hb - 1), 0))
    cls = lambda d, dt: jax.ShapeDtypeStruct((d, s // d, 512), dt)
    cspecs = [_class_spec(d, 512, tm) for d in DILATIONS]
    return pl.pallas_call(
        body, name="combine", grid=(nt,),
        out_shape=(jax.ShapeDtypeStruct((s, 1024), F32),
                   cls(1, F32), cls(1, BF16), jax.ShapeDtypeStruct((s, 512), BF16),
                   jax.ShapeDtypeStruct((s, 1024), BF16), jax.ShapeDtypeStruct((s, 512), BF16),
                   jax.ShapeDtypeStruct((1, 128), F32), jax.ShapeDtypeStruct((1, 512), F32),
                   jax.ShapeDtypeStruct((8, 512), F32),
                   cls(4, BF16), cls(4, F32), cls(16, BF16), cls(16, F32),
                   jax.ShapeDtypeStruct((4, 2, 128, 1024), BF16)),
        in_specs=cspecs + cspecs + [row(2048), prev, nxt, row(512, 1), row(1024),
                                    _resident((1024, 1024)), full((8, 512)), full((1, 512)), full((256, 256))],
        out_specs=(row(1024), cspecs[0], cspecs[0], row(512), row(1024), row(512),
                   full((1, 128)), full((1, 512)), full((8, 512)),
                   cspecs[1], cspecs[1], cspecs[2], cspecs[2], full((4, 2, 128, 1024))),
        scratch_shapes=[pltpu.VMEM((4, tm, 128), F32), pltpu.VMEM((4, tm, 128), F32),
                        pltpu.VMEM((1024, 1024), F32)],
        compiler_params=_params(("arbitrary",)),
    )(*o_g, *lse_g, cg, cg, cg, vz, xmt, wout, cw, cb, b256)


def _attn_bwd(qkn_l, v_l, do_l, ld_l, bias, gi, name):
    r_cls, length, _ = qkn_l.shape
    qt = min(length, 2048 if length <= 4096 else 1024)
    nb, nsub, nt = length // QB, qt // QB, length // qt
    chunk = min(length, 4096)
    nchunk = length // chunk

    def body(q_ref, k_ref, v_ref, do_ref, ld_ref, b_ref, dq_ref, dkv_hbm, dsum_ref, dk_acc, dv_acc, stage, sems):
        p_id, r, t = pl.program_id(0), pl.program_id(1), pl.program_id(2)
        lo = lax.broadcasted_iota(jnp.int32, (QB, 128), 1) < HEAD_DIM

        @pl.when(t == 0)
        def _():
            dk_acc[...] = jnp.zeros_like(dk_acc)
            dv_acc[...] = jnp.zeros_like(dv_acc)

        @pl.when((t == 0) & (r == 0))
        def _():
            dsum_ref[...] = jnp.zeros_like(dsum_ref)

        nt_dims = (((1,), (1,)), ((), ()))
        tn_dims = (((0,), (0,)), ((), ()))
        coords, qqs, dds, logits, dps, lcols, dcols = [], [], [], [], [], [], []
        for i in range(nsub):
            q0, start, variant = _block_coords(t, i, nsub, nb, length)
            qq = _split_heads(q_ref[q0:q0 + QB, :], lo)
            dd = _split_heads(do_ref[q0:q0 + QB, :], lo)
            k = k_ref[pl.ds(start, KB), :]
            v = v_ref[pl.ds(start, KB), :]
            logits.append(lax.dot_general(qq, k, nt_dims, preferred_element_type=F32) + b_ref[variant])
            dps.append(lax.dot_general(dd, v, nt_dims, preferred_element_type=F32))
            lcols.append(_col_pair(ld_ref, q0, 0))
            dcols.append(_col_pair(ld_ref, q0, HEAD_DIM // 2))
            coords.append((q0, start, variant))
            qqs.append(qq)
            dds.append(dd)
        p = jnp.exp(jnp.concatenate(logits, axis=0) - jnp.concatenate(lcols, axis=0))
        ds = p * (jnp.concatenate(dps, axis=0) - jnp.concatenate(dcols, axis=0))
        pb = p.astype(BF16)
        dsb = ds.astype(BF16)
        middle = None
        for i in range(nsub):
            q0, start, variant = coords[i]
            rows = slice(2 * QB * i, 2 * QB * (i + 1))
            if 0 < i < nsub - 1:
                middle = ds[rows] if middle is None else middle + ds[rows]
            else:
                dsum_ref[variant] += ds[rows]
            dqq = jnp.dot(dsb[rows], k_ref[pl.ds(start, KB), :], preferred_element_type=F32)
            dq_ref[q0:q0 + QB, :] = jnp.where(lo, dqq[0:QB], dqq[QB:2 * QB]).astype(BF16)
            dk_acc[pl.ds(start, KB), :] += lax.dot_general(dsb[rows], qqs[i], tn_dims, preferred_element_type=F32)
            dv_acc[pl.ds(start, KB), :] += lax.dot_general(pb[rows], dds[i], tn_dims, preferred_element_type=F32)
        if middle is not None:
            dsum_ref[1] += middle

        @pl.when(t == nt - 1)
        def _():
            def copy(k):
                which, c = k // nchunk, k % nchunk
                rows = pl.ds(c * chunk, chunk)
                return pltpu.make_async_copy(stage.at[k % 2], dkv_hbm.at[r, p_id, which, rows, :], sems.at[k % 2])

            for k in range(2 * nchunk):
                if k < 2:
                    @pl.when((p_id > 0) | (r > 0))
                    def _():
                        copy(k).wait()
                else:
                    copy(k).wait()
                acc = (dk_acc, dv_acc)[k // nchunk]
                stage[k % 2] = acc[pl.ds((k % nchunk) * chunk, chunk), :].astype(BF16)
                copy(k).start()

            @pl.when((p_id == N_PAIR - 1) & (r == r_cls - 1))
            def _():
                copy(0).wait()
                copy(1).wait()

    qspec = pl.BlockSpec((None, qt, 128), lambda p, r, t: (r, t, p))
    return pl.pallas_call(
        body, name=name, grid=(N_PAIR, r_cls, nt),
        out_shape=(jax.ShapeDtypeStruct((r_cls, length, 512), BF16),
                   jax.ShapeDtypeStruct((r_cls, N_PAIR, 2, length, 128), BF16),
                   jax.ShapeDtypeStruct((N_PAIR, 3, 2 * QB, KB), F32)),
        in_specs=[qspec,
                  pl.BlockSpec((None, length, 128), lambda p, r, t: (r, 0, 4 + p)),
                  pl.BlockSpec((None, length, 128), lambda p, r, t: (r, 0, p)),
                  qspec, qspec,
                  pl.BlockSpec((None, 3, None, 2 * QB, KB), lambda p, r, t: (gi, 0, p, 0, 0))],
        out_specs=(qspec, pl.BlockSpec(memory_space=pl.ANY),
                   pl.BlockSpec((None, 3, 2 * QB, KB), lambda p, r, t: (p, 0, 0, 0))),
        scratch_shapes=[pltpu.VMEM((length, 128), F32), pltpu.VMEM((length, 128), F32),
                        pltpu.VMEM((2, chunk, 128), BF16), pltpu.SemaphoreType.DMA((2,))],
        compiler_params=_params(("arbitrary", "arbitrary", "arbitrary")),
    )(qkn_l, qkn_l, v_l, do_l, ld_l, bias)


def _bwd_tail(dq_g, dkv_g, qkr, qkw, dza, dgbz, dzc, cg, cw, wblk, x, norm_w, dout, b256):
    s = x.shape[0]
    tm = TM_COMBINE
    hb = 16
    nt = s // tm

    def body(dq1, dq4, dq16, dkv1, dkv4, dkv16, qkr_ref, qkw_ref, dza_ref, dgbz_ref, dzc_ref,
             dzp_ref, dzn_ref, u_ref, gc_ref, cw_ref, w_ref, x_ref, nw_ref, dout_ref, b_ref,
             gx_o, dproj_o, dnw_o, dqkw_o, slab, mid):
        i = pl.program_id(0)

        def nat_q(ref, d):
            return _gather_classes(slab, lambda r, j: ref[r, :, 128 * j:128 * (j + 1)], 4, d, mid)

        def nat_kv(ref, d, which):
            return _gather_classes(slab, lambda r, j: ref[r, j, which], 4, d, mid)

        @pl.when(i == 0)
        def _():
            dnw_o[...] = jnp.zeros_like(dnw_o)
            dqkw_o[...] = jnp.zeros_like(dqkw_o)

        dzc = dzc_ref[...].astype(F32)
        d_prev = jnp.where(i == 0, 0.0, dzp_ref[hb - 1:hb, :].astype(F32))
        d_next = jnp.where(i == nt - 1, 0.0, dzn_ref[0:1, :].astype(F32))
        rows = lax.broadcasted_iota(jnp.int32, (tm, 512), 0)
        d_up = jnp.where(rows == 0, d_prev, pltpu.roll(dzc, 1, 0))
        d_dn = jnp.where(rows == tm - 1, d_next, pltpu.roll(dzc, tm - 1, 0))
        dt = cw_ref[0:1, :] * d_dn + cw_ref[1:2, :] * dzc + cw_ref[2:3, :] * d_up
        u = u_ref[...].astype(F32)
        gc = gc_ref[...].astype(F32)
        dproj_o[:, 0:512] = (dt * gc).astype(BF16)
        dproj_o[:, 512:1024] = dgbz_ref[:, 0:512]
        dproj_o[:, 1024:1536] = (dt * u).astype(BF16)
        dproj_o[:, 1536:2048] = dgbz_ref[:, 512:1024]

        dqn = (dq1[0].astype(F32) + nat_q(dq4, 4) + nat_q(dq16, 16)) * (1.0 / 8.0)
        dk1 = jnp.concatenate([dkv1[0, j, 0] for j in range(N_PAIR)], axis=1)
        dv1 = jnp.concatenate([dkv1[0, j, 1] for j in range(N_PAIR)], axis=1)
        dkn = dk1 + nat_kv(dkv4, 4, 0) + nat_kv(dkv16, 16, 0)
        dvn = dv1 + nat_kv(dkv4, 4, 1) + nat_kv(dkv16, 16, 1)
        g = jnp.concatenate([dqn, dkn], axis=1) * qkw_ref[...]
        raw = qkr_ref[...].astype(F32)
        rr = lax.rsqrt(_group_sum(raw * raw, b_ref, split=False) * (1.0 / HEAD_DIM) + EPS)
        proj_gq = _group_sum(g * raw, b_ref) * (1.0 / HEAD_DIM)
        draw = rr * g - raw * (rr * rr * rr) * proj_gq
        dqkw_o[...] += jnp.sum(jnp.concatenate([dqn, dkn], axis=1) * raw * rr, axis=0, keepdims=True)
        dproj_o[:, 2048:3072] = draw.astype(BF16)
        dproj_o[:, 3072:3584] = dvn.astype(BF16)
        dproj_o[:, 3584:4096] = dza_ref[...]

        nt_dims = (((1,), (1,)), ((), ()))
        dh = lax.dot_general(dproj_o[:, 0:1024], w_ref[0], nt_dims, preferred_element_type=F32)
        for b in range(1, 4):
            dh += lax.dot_general(dproj_o[:, 1024 * b:1024 * b + 1024], w_ref[b], nt_dims,
                                  preferred_element_type=F32)

        xf = x_ref[...]
        r = lax.rsqrt(jnp.mean(xf * xf, axis=-1, keepdims=True) + EPS)
        gh = dh * nw_ref[...]
        dnw_o[...] += jnp.sum(dh * xf * r, axis=0, keepdims=True)
        mean_gx = jnp.mean(gh * xf, axis=-1, keepdims=True)
        gx_o[...] = dout_ref[...] + r * gh - xf * (r * r * r) * mean_gx

    row = lambda w, j=0: pl.BlockSpec((tm, w), lambda i: (i, j))
    full = lambda shp: pl.BlockSpec(shp, lambda i: (0,) * len(shp))
    prev = pl.BlockSpec((hb, 512), lambda i: (jnp.maximum(i * (tm // hb) - 1, 0), 0))
    nxt = pl.BlockSpec((hb, 512), lambda i: (jnp.minimum((i + 1) * (tm // hb), s // hb - 1), 0))
    return pl.pallas_call(
        body, name="bwd_tail", grid=(nt,),
        out_shape=(jax.ShapeDtypeStruct((s, 1024), F32), jax.ShapeDtypeStruct((s, 4096), BF16),
                   jax.ShapeDtypeStruct((1, 1024), F32), jax.ShapeDtypeStruct((1, 1024), F32)),
        in_specs=[_class_spec(d, 512, tm) for d in DILATIONS]
        + [pl.BlockSpec((d, N_PAIR, 2, tm // d, 128), lambda i: (0, 0, 0, i, 0)) for d in DILATIONS]
        + [row(1024), full((1, 1024)), row(512), row(1024), row(512), prev, nxt,
           row(512, 0), row(512, 2), full((8, 512)), _resident((4, 1024, 1024)), row(1024),
           full((1, 1024)), row(1024), full((256, 256))],
        out_specs=(row(1024), row(4096), full((1, 1024)), full((1, 1024))),
        scratch_shapes=[pltpu.VMEM((4, tm, 128), F32), pltpu.VMEM((4, tm, 128), F32)],
        compiler_params=_params(("arbitrary",)),
    )(*dq_g, *dkv_g, qkr, qkw, dza, dgbz, dzc, dzc, dzc, cg, cg, cw, wblk, x, norm_w, dout, b256)


def _wgrad(a, b, name):
    s, m = a.shape
    n = b.shape[1]
    tk = 1024
    ncol = min(n, 2048)
    nj, nk = n // ncol, s // tk

    def body(a_ref, b_ref, o_ref, acc):
        kk = pl.program_id(1)

        @pl.when(kk == 0)
        def _():
            acc[...] = jnp.zeros_like(acc)

        acc[...] += lax.dot_general(a_ref[...], b_ref[...].astype(BF16), (((0,), (0,)), ((), ())),
                                    preferred_element_type=F32)

        @pl.when(kk == nk - 1)
        def _():
            blocks, _, rows, _ = o_ref.shape
            for blk in range(blocks):
                for half in range(2):
                    o_ref[blk, half] = acc[half * rows:(half + 1) * rows,
                                           1024 * blk:1024 * (blk + 1)].astype(BF16)

    out_shape = jax.ShapeDtypeStruct((n // 1024, 2, m // 2, 1024), BF16)
    out_spec = pl.BlockSpec((ncol // 1024, 2, m // 2, 1024), lambda j, k: (j, 0, 0, 0))
    return pl.pallas_call(
        body, name=name, grid=(nj, nk),
        out_shape=out_shape,
        in_specs=[pl.BlockSpec((tk, m), lambda j, k: (k, 0)), pl.BlockSpec((tk, ncol), lambda j, k: (k, j))],
        out_specs=out_spec,
        scratch_shapes=[pltpu.VMEM((m, ncol), F32)],
        compiler_params=_params(("parallel", "arbitrary")),
    )(a, b)


def _dbias(dsums, onehot_all):
    def body(ds1_ref, ds4_ref, ds16_ref, oh_ref, o_ref):
        @pl.when(pl.program_id(0) == 0)
        def _():
            o_ref[...] = jnp.zeros_like(o_ref)

        hrow = lax.broadcasted_iota(jnp.int32, (8, KB), 0)
        flip = (lax.broadcasted_iota(jnp.int32, (QB, QB), 0)
                + lax.broadcasted_iota(jnp.int32, (QB, QB), 1) == QB - 1).astype(F32)

        def diagonal_sums(tile):
            rev = jnp.dot(flip, tile, preferred_element_type=F32, precision=lax.Precision.HIGHEST)
            sums = jnp.sum(pltpu.roll(rev, 0, 1, stride=1, stride_axis=0), axis=0, keepdims=True)
            return pltpu.roll(sums, KB - (QB - 1), 1)

        for g, ds_ref in enumerate((ds1_ref, ds4_ref, ds16_ref)):
            diag = jnp.zeros((8, KB), F32)
            for p in range(N_PAIR):
                diag = jnp.where(hrow == 2 * p, diagonal_sums(ds_ref[p, 0:QB, :]), diag)
                diag = jnp.where(hrow == 2 * p + 1, diagonal_sums(ds_ref[p, QB:2 * QB, :]), diag)
            o_ref[...] += jnp.dot(diag, oh_ref[g], preferred_element_type=F32, precision=lax.Precision.HIGHEST)

    ds_spec = pl.BlockSpec((N_PAIR, None, 2 * QB, KB), lambda v: (0, v, 0, 0))
    return pl.pallas_call(
        body, name="dbias", grid=(3,),
        out_shape=jax.ShapeDtypeStruct((8, 128), F32),
        in_specs=[ds_spec, ds_spec, ds_spec, pl.BlockSpec((3, None, KB, 128), lambda v: (0, v, 0, 0))],
        out_specs=pl.BlockSpec((8, 128), lambda v: (0, 0)),
        compiler_params=_params(("arbitrary",)),
    )(*dsums, onehot_all)


def _gsync(pw_in, pw_out, small):
    hin, hout = pw_in.shape[2], pw_out.shape[2]
    nsmall = small.shape[0]

    def body(pin_hbm, pout_hbm, small_ref, gin_o, gout_o, small_o,
             mine_in, recv_in, sbuf_in, rbuf_in, mine_out, recv_out, sbuf_out, rbuf_out, gather,
             lsem, asend, arecv, bsend, brecv, csend, crecv, ssend, srecv):
        x, y, c = lax.axis_index("x"), lax.axis_index("y"), lax.axis_index("c")
        b = 2 * x + y
        dev = 4 * x + 2 * y + c
        sib = (x, y, 1 - c)

        def rcopy(src, dst, ssem, rsem, to):
            return pltpu.make_async_remote_copy(src_ref=src, dst_ref=dst, send_sem=ssem, recv_sem=rsem,
                                                device_id=to, device_id_type=MESH)

        gather[dev] = small_ref[...]
        s_sends = []
        for k in range(1, 8):
            to = (x ^ (k >> 2), y ^ ((k >> 1) & 1), c ^ (k & 1))
            cp = rcopy(gather.at[dev], gather.at[dev], ssend.at[k - 1], srecv.at[k - 1], to)
            cp.start()
            s_sends.append(cp)

        a_in = rcopy(pin_hbm.at[:, 1 - c], recv_in, asend.at[0], arecv.at[0], sib)
        a_out = rcopy(pout_hbm.at[:, 1 - c], recv_out, asend.at[1], arecv.at[1], sib)
        a_in.start()
        a_out.start()
        l_in = pltpu.make_async_copy(pin_hbm.at[:, c], mine_in, lsem.at[0])
        l_out = pltpu.make_async_copy(pout_hbm.at[:, c], mine_out, lsem.at[1])
        l_in.start()
        l_out.start()
        l_in.wait()
        l_out.wait()

        def stage_b(a_cp, mine, recv, sbuf, rbuf, base):
            a_cp.wait_recv()
            sends = []
            for k in (1, 2, 3):
                bk = b ^ k
                sbuf[k - 1] = (mine[bk].astype(F32) + recv[bk].astype(F32)).astype(BF16)
                cp = rcopy(sbuf.at[k - 1], rbuf.at[k - 1], bsend.at[base + k - 1], brecv.at[base + k - 1],
                           (x ^ (k >> 1), y ^ (k & 1), c))
                cp.start()
                sends.append(cp)
            return sends

        b_in = stage_b(a_in, mine_in, recv_in, sbuf_in, rbuf_in, 0)
        b_out = stage_b(a_out, mine_out, recv_out, sbuf_out, rbuf_out, 3)

        def stage_c(b_sends, mine, recv, rbuf, g_o, half, idx):
            acc = mine[b].astype(F32) + recv[b].astype(F32)
            for k in (1, 2, 3):
                b_sends[k - 1].wait_recv()
                acc = acc + rbuf[k - 1].astype(F32)
            rows = g_o.at[pl.ds(pl.multiple_of(c * half, half), half), :]
            g_o[pl.ds(pl.multiple_of(c * half, half), half), :] = acc
            cp = rcopy(rows, rows, csend.at[idx], crecv.at[idx], sib)
            cp.start()
            return cp

        c_in = stage_c(b_in, mine_in, recv_in, rbuf_in, gin_o, hin, 0)
        c_out = stage_c(b_out, mine_out, recv_out, rbuf_out, gout_o, hout, 1)

        for cp in s_sends:
            cp.wait_recv()
        tot = gather[0]
        for d in range(1, 8):
            tot = tot + gather[d]
        small_o[...] = tot

        for g_o, half, idx in ((gin_o, hin, 0), (gout_o, hout, 1)):
            other = g_o.at[pl.ds(pl.multiple_of((1 - c) * half, half), half), :]
            rcopy(other, other, csend.at[idx], crecv.at[idx], sib).wait_recv()
        for cp in s_sends + [a_in, a_out] + b_in + b_out + [c_in, c_out]:
            cp.wait_send()

    vm = pl.BlockSpec(memory_space=pltpu.VMEM)
    hbm = pl.BlockSpec(memory_space=pl.ANY)
    return pl.pallas_call(
        body, name="gsync",
        out_shape=(jax.ShapeDtypeStruct((2 * hin, 1024), F32), jax.ShapeDtypeStruct((2 * hout, 1024), F32),
                   jax.ShapeDtypeStruct((nsmall, 128), F32)),
        in_specs=[hbm, hbm, vm], out_specs=(vm, vm, vm),
        scratch_shapes=[pltpu.VMEM((4, hin, 1024), BF16), pltpu.VMEM((4, hin, 1024), BF16),
                        pltpu.VMEM((3, hin, 1024), BF16), pltpu.VMEM((3, hin, 1024), BF16),
                        pltpu.VMEM((4, hout, 1024), BF16), pltpu.VMEM((4, hout, 1024), BF16),
                        pltpu.VMEM((3, hout, 1024), BF16), pltpu.VMEM((3, hout, 1024), BF16),
                        pltpu.VMEM((8, nsmall, 128), F32),
                        pltpu.SemaphoreType.DMA((2,)),
                        pltpu.SemaphoreType.DMA((2,)), pltpu.SemaphoreType.DMA((2,)),
                        pltpu.SemaphoreType.DMA((6,)), pltpu.SemaphoreType.DMA((6,)),
                        pltpu.SemaphoreType.DMA((2,)), pltpu.SemaphoreType.DMA((2,)),
                        pltpu.SemaphoreType.DMA((7,)), pltpu.SemaphoreType.DMA((7,))],
        compiler_params=_params(),
    )(pw_in, pw_out, small)


def _adamw_math(w, g, m, v):
    m = ADAM_B1 * m + (1.0 - ADAM_B1) * g
    v = ADAM_B2 * v + (1.0 - ADAM_B2) * (g * g)
    m_hat = m / (1.0 - ADAM_B1 ** ADAM_STEP)
    v_hat = v / (1.0 - ADAM_B2 ** ADAM_STEP)
    delta = -ADAM_LR * (m_hat / (jnp.sqrt(v_hat) + ADAM_EPS) + ADAM_WD * w)
    return delta, m, v


def _adamw(w, g, m, v, name):
    rows, cols = w.shape
    tr = 256 if rows % 256 == 0 else rows

    def body(w_ref, g_ref, m_ref, v_ref, g_o, d_o, m_o, v_o):
        g = g_ref[...]
        d, m2, v2 = _adamw_math(w_ref[...], g, m_ref[...], v_ref[...])
        g_o[...] = g
        d_o[...] = d
        m_o[...] = m2
        v_o[...] = v2

    spec = pl.BlockSpec((tr, cols), lambda i: (i, 0))
    shp = jax.ShapeDtypeStruct((rows, cols), F32)
    return pl.pallas_call(
        body, name=name, grid=(rows // tr,), out_shape=(shp, shp, shp, shp),
        in_specs=[spec] * 4, out_specs=(spec, spec, spec, spec),
        compiler_params=_params(("parallel",)),
    )(w, g, m, v)


def _fold_heads(dqkw):
    def body(x_ref, o_ref):
        xs = x_ref[...]
        sq = xs[0:1] + xs[1:2] + xs[2:3] + xs[3:4]
        sk = xs[4:5] + xs[5:6] + xs[6:7] + xs[7:8]
        both = jnp.concatenate([sq, sk], axis=0)
        o_ref[...] = both + pltpu.roll(both, HEAD_DIM, 1)

    vm = pl.BlockSpec(memory_space=pltpu.VMEM)
    return pl.pallas_call(body, name="fold_heads", out_shape=jax.ShapeDtypeStruct((2, 128), F32),
                          in_specs=[vm], out_specs=vm, compiler_params=_params())(dqkw)


def kernel(x, norm_w, w_in, conv_w, conv_b, q_norm_w, k_norm_w, rel_bias, w_out, loss_target, m_norm_w, m_w_in, m_conv_w, m_conv_b, m_q_norm_w, m_k_norm_w, m_rel_bias, m_w_out, v_norm_w, v_w_in, v_conv_w, v_conv_b, v_q_norm_w, v_k_norm_w, v_rel_bias, v_w_out):
    x2 = x[0]
    tgt = loss_target[0]
    blk = 2 * lax.axis_index("x") + lax.axis_index("y")

    conv_w8 = jnp.pad(conv_w, ((0, 5), (0, 0)))
    wblk, woutblk, cwblk = _wgather(w_in, w_out, conv_w8)
    wout_full = woutblk.reshape(1024, 1024)
    cw_full = cwblk.transpose(1, 0, 2).reshape(8, 512)

    qkw = jnp.concatenate([jnp.tile(q_norm_w, 8) * 0.125, jnp.tile(k_norm_w, 8)])[None, :]
    qkw_raw = jnp.concatenate([jnp.tile(q_norm_w, 8), jnp.tile(k_norm_w, 8)])[None, :]
    gidx = jnp.arange(256) // HEAD_DIM
    b256 = (gidx[:, None] == gidx[None, :]).astype(BF16)

    h, cg, qkr, qkn, vz, qkn4, v4, qkn16, v16, xmt = _proj(x2, tgt, norm_w[None, :], wblk, qkw, b256)

    biases = _bias_tables(rel_bias)
    qkn_l = [qkn[None], qkn4, qkn16]
    v_l = [vz[None], v4, v16]
    o_g, lse_g = [], []
    for gi, d in enumerate(DILATIONS):
        o_l, lse_l = _attn_fwd(qkn_l[gi], v_l[gi], biases, gi, f"attn_fwd_d{d}")
        o_g.append(o_l)
        lse_g.append(lse_l)

    (dout, ld1, do1, dza, dgbz, dzc, loss_p, dcb, dcw, do4, ld4, do16, ld16, pw_out) = _combine(
        o_g, lse_g, cg, vz, xmt, wout_full, cw_full, conv_b[None, :], b256)

    dq_g, dkv_g, dsums = [], [], []
    for gi, (d, do_l, ld_l) in enumerate(zip(DILATIONS, (do1, do4, do16), (ld1, ld4, ld16))):
        dq_l, dkv_l, dsum = _attn_bwd(qkn_l[gi], v_l[gi], do_l, ld_l, biases, gi, f"attn_bwd_d{d}")
        dq_g.append(dq_l)
        dkv_g.append(dkv_l)
        dsums.append(dsum)

    grad_x, dproj, dnw, dqkw = _bwd_tail(dq_g, dkv_g, qkr, qkw_raw, dza, dgbz, dzc, cg, cw_full, wblk,
                                         x2, norm_w[None, :], dout, b256)

    pw_in = _wgrad(h, dproj, "wgrad_in")
    dbias8 = _dbias(dsums, jnp.stack([_diag_bucket_onehot(d) for d in DILATIONS], axis=0))

    small = jnp.concatenate([dnw.reshape(8, 128), dcb.reshape(4, 128), dqkw.reshape(8, 128),
                             dcw[0:3].reshape(12, 128), dbias8, jnp.pad(loss_p, ((0, 7), (0, 0)))], axis=0)
    g_win, g_wout, gsmall = _gsync(pw_in, pw_out, small)

    g_nw = gsmall[0:8].reshape(1024)
    g_cb = gsmall[8:12].reshape(512)
    folded = _fold_heads(gsmall[12:20])
    g_qw, g_kw = folded[0, 0:64], folded[1, 0:64]
    g_cw = lax.dynamic_slice(gsmall[20:32].reshape(3, 512), (0, blk * 128), (3, 128))
    g_rb = gsmall[32:40][:, 0:32].T
    loss = gsmall[40, 0]

    g_win, d_win, nm_win, nv_win = _adamw(w_in, g_win, m_w_in, v_w_in, "adamw_w_in")
    g_wout, d_wout, nm_wout, nv_wout = _adamw(w_out, g_wout, m_w_out, v_w_out, "adamw_w_out")

    def pack(parts):
        rows = [parts[0].reshape(8, 128), parts[1].reshape(4, 128),
                jnp.pad(parts[2], (0, 64))[None, :], jnp.pad(parts[3], (0, 64))[None, :],
                parts[4], jnp.pad(parts[5].T, ((0, 0), (0, 96)))]
        return jnp.concatenate(rows, axis=0)

    ws = pack([norm_w, conv_b, q_norm_w, k_norm_w, conv_w, rel_bias])
    gs = pack([g_nw, g_cb, g_qw, g_kw, g_cw, g_rb])
    ms = pack([m_norm_w, m_conv_b, m_q_norm_w, m_k_norm_w, m_conv_w, m_rel_bias])
    vs = pack([v_norm_w, v_conv_b, v_q_norm_w, v_k_norm_w, v_conv_w, v_rel_bias])
    rpad = lambda a: jnp.pad(a, ((0, 7), (0, 0)))
    _, d_s, nm_s, nv_s = _adamw(rpad(ws), rpad(gs), rpad(ms), rpad(vs), "adamw_small")

    def unpack(a):
        return (a[0:8].reshape(1024), a[12:13, 0:64].reshape(64), a[13:14, 0:64].reshape(64),
                a[14:17], a[8:12].reshape(512), a[17:25, 0:32].T)

    def ordered(nw, win, cw, cb, qw, kw, rb, wout):
        return (nw, win, cw, cb, qw, kw, rb, wout)

    g_un = (g_nw, g_qw, g_kw, g_cw, g_cb, g_rb)
    outs = [loss, grad_x[None]]
    for un, win_v, wout_v in ((g_un, g_win, g_wout), (unpack(d_s), d_win, d_wout),
                              (unpack(nm_s), nm_win, nm_wout), (unpack(nv_s), nv_win, nv_wout)):
        nw, qw, kw, cw, cb, rb = un
        outs.extend(ordered(nw, win_v, cw, cb, qw, kw, rb, wout_v))
    return tuple(outs)
```

```python
import math

import jax
import jax.numpy as jnp
from jax import lax
from jax.experimental import pallas as pl
from jax.experimental.pallas import tpu as pltpu

F32 = jnp.float32
BF16 = jnp.bfloat16
MESH = pl.DeviceIdType.MESH

D_MODEL = 1024
CONV_W = 512
ATTN_W = 512
HEAD_DIM = 64
N_PAIR = 4
DILATIONS = (1, 4, 16)
HALF = 64
QB = 128
KB = QB + 2 * HALF
NUM_BUCKETS = 32
MAX_DISTANCE = 1024
EPS = 1e-6
NEG = -1e30
ADAM_LR, ADAM_B1, ADAM_B2, ADAM_EPS, ADAM_WD, ADAM_STEP = 0.001, 0.9, 0.999, 1e-08, 0.01, 10
VMEM_LIMIT = 48 << 20


def _params(sem=None, vmem=VMEM_LIMIT, **kw):
    if sem is not None:
        kw["dimension_semantics"] = sem
    return pltpu.CompilerParams(vmem_limit_bytes=vmem, **kw)


def _sigmoid(z):
    return 1.0 / (1.0 + jnp.exp(-z))


def _group_sum(val, b_ref, split=True):
    hi = val.astype(BF16)
    lo = (val - hi.astype(F32)).astype(BF16) if split else None
    outs = []
    for j in range(val.shape[1] // 256):
        sl = slice(256 * j, 256 * j + 256)
        part = jnp.dot(hi[:, sl], b_ref[...], preferred_element_type=F32)
        if split:
            part = part + jnp.dot(lo[:, sl], b_ref[...], preferred_element_type=F32)
        outs.append(part)
    return outs[0] if len(outs) == 1 else jnp.concatenate(outs, axis=1)


def _t5_bucket(rel):
    half_b = NUM_BUCKETS // 2
    max_exact = half_b // 2
    ret = jnp.where(rel > 0, half_b, 0)
    n = jnp.abs(rel)
    nf = jnp.maximum(n, 1).astype(F32)
    large = max_exact + (jnp.log(nf / max_exact) / math.log(MAX_DISTANCE / max_exact)
                         * (half_b - max_exact)).astype(jnp.int32)
    large = jnp.minimum(large, half_b - 1)
    return ret + jnp.where(n < max_exact, n, large)


def _bias_tables(rel_bias):
    rows = []
    key = jnp.arange(KB)
    for dilation in DILATIONS:
        for variant in range(3):
            off = (0, HALF, 2 * HALF)[variant]
            rel = ((key - off + KB // 2) % KB) - KB // 2
            bkt = _t5_bucket(jnp.clip(rel, -HALF, HALF) * dilation)
            rows.append(jnp.where(jnp.abs(rel) <= HALF, bkt, -1))
    bkt_all = jnp.broadcast_to(jnp.stack(rows, axis=0).astype(jnp.int32)[:, None, :], (9, 8, KB))

    def body(rb_ref, bkt_ref, o_ref):
        bkt = bkt_ref[...]
        off = (pl.program_id(0) % 3) * HALF
        rel = (lax.broadcasted_iota(jnp.int32, (QB, KB), 1) - lax.broadcasted_iota(jnp.int32, (QB, KB), 0)) - off
        band = jnp.abs(rel) <= HALF
        for h in range(8):
            acc = jnp.full((8, KB), NEG, F32)
            for b in range(NUM_BUCKETS):
                acc = jnp.where(bkt == b, rb_ref[b, h], acc)
            rolled = pltpu.roll(jnp.broadcast_to(acc[0:1], (QB, KB)), 0, 1, stride=1, stride_axis=0)
            o_ref[h] = jnp.where(band, rolled, NEG)

    out = pl.pallas_call(
        body, name="bias_tables", grid=(9,),
        out_shape=jax.ShapeDtypeStruct((9, 8, QB, KB), F32),
        in_specs=[pl.BlockSpec(memory_space=pltpu.SMEM), pl.BlockSpec((None, 8, KB), lambda i: (i, 0, 0))],
        out_specs=pl.BlockSpec((None, 8, QB, KB), lambda i: (i, 0, 0, 0)),
        compiler_params=_params(("parallel",)),
    )(rel_bias, bkt_all)
    return out.reshape(3, 3, N_PAIR, 2 * QB, KB)


def _diag_bucket_onehot(dilation):
    out = []
    c = jnp.arange(KB)
    for variant in range(3):
        off = (0, HALF, 2 * HALF)[variant]
        rel = ((c - off + 128) % 256) - 128
        band = jnp.abs(rel) <= HALF
        bkt = _t5_bucket(jnp.clip(rel, -HALF, HALF) * dilation)
        oh = (bkt[:, None] == jnp.arange(128)[None, :]) & band[:, None]
        out.append(oh.astype(F32))
    return jnp.stack(out, axis=0)


def _wgather(w_in, w_out, conv_w):
    rin, rout = w_in.shape[0] // 2, w_out.shape[0] // 2

    def body(win_ref, wout_ref, cw_ref, win_o, wout_o, cw_o, send_sems, recv_sems):
        x, y, c = lax.axis_index("x"), lax.axis_index("y"), lax.axis_index("c")
        b = 2 * x + y
        win_o[b] = win_ref[...].astype(BF16)
        wout_o[b] = wout_ref[...].astype(BF16)
        cw_o[b] = cw_ref[...]
        xnbr, ynbr, diag, sib = (1 - x, y, c), (x, 1 - y, c), (1 - x, 1 - y, c), (x, y, 1 - c)
        bx, by, bd = b ^ 2, b ^ 1, b ^ 3

        def copy(sem, ref, to):
            return pltpu.make_async_remote_copy(src_ref=ref, dst_ref=ref, send_sem=send_sems.at[sem],
                                                recv_sem=recv_sems.at[sem], device_id=to, device_id_type=MESH)

        def rows_of(ref, half):
            def rows(blk, quarter=None):
                if quarter is None:
                    return ref.at[blk, pl.ds(c * half, half), :]
                return ref.at[blk, pl.ds(c * half + quarter * (half // 2), half // 2), :]
            return rows

        def send_own(ref, half, base):
            rows = rows_of(ref, half)
            own_x, own_y = copy(base + 0, rows(b), xnbr), copy(base + 1, rows(b), ynbr)
            own_x.start()
            own_y.start()
            return [own_x, own_y]

        def relay(ref, half, base):
            rows = rows_of(ref, half)
            copy(base + 0, rows(bx), xnbr).wait_recv()
            pass_y = copy(base + 2, rows(bx, 0), ynbr)
            pass_y.start()
            to_sib = [copy(base + 4, rows(bx), sib)]
            to_sib[-1].start()
            copy(base + 1, rows(by), ynbr).wait_recv()
            pass_x = copy(base + 3, rows(by, 1), xnbr)
            pass_x.start()
            to_sib.append(copy(base + 5, rows(by), sib))
            to_sib[-1].start()
            copy(base + 2, rows(bd, 0), ynbr).wait_recv()
            copy(base + 3, rows(bd, 1), xnbr).wait_recv()
            to_sib.append(copy(base + 6, rows(bd), sib))
            to_sib[-1].start()
            return [pass_y, pass_x] + to_sib

        def from_sibling(ref, half, base):
            for k, blk in enumerate((bx, by, bd)):
                copy(base + 4 + k, ref.at[blk, pl.ds((1 - c) * half, half), :], sib).wait_recv()

        small = [copy(14 + k, cw_o.at[b], to) for k, to in enumerate((xnbr, ynbr, diag))]
        for cp in small:
            cp.start()
        started = send_own(win_o, rin, 0) + send_own(wout_o, rout, 7)
        started += relay(win_o, rin, 0) + relay(wout_o, rout, 7)
        for k, blk in enumerate((bx, by, bd)):
            copy(14 + k, cw_o.at[blk], sib).wait_recv()
        from_sibling(win_o, rin, 0)
        from_sibling(wout_o, rout, 7)
        for cp in small + started:
            cp.wait_send()

    vm = pl.BlockSpec(memory_space=pltpu.VMEM)
    return pl.pallas_call(
        body, name="wgather",
        out_shape=(jax.ShapeDtypeStruct((4,) + w_in.shape, BF16),
                   jax.ShapeDtypeStruct((4,) + w_out.shape, BF16),
                   jax.ShapeDtypeStruct((4,) + conv_w.shape, F32)),
        in_specs=[vm, vm, vm], out_specs=(vm, vm, vm),
        scratch_shapes=[pltpu.SemaphoreType.DMA((17,)), pltpu.SemaphoreType.DMA((17,))],
        compiler_params=_params(),
    )(w_in, w_out, conv_w)


TM_MATMUL = 512
TM_COMBINE = 256


def _resident(shape):
    return pl.BlockSpec(shape, lambda i: (0,) * len(shape), pipeline_mode=pl.Buffered(1))


def _to_slabs(slab, val, j0=0):
    for j in range(val.shape[1] // 128):
        slab[j0 + j] = val[:, 128 * j:128 * (j + 1)]


def _scatter_classes(slab, j0, nj, out_ref, d, part=0, mid=None):
    tm = slab.shape[1]
    n = tm // d
    if d == 4:
        for r in range(d):
            for j in range(nj):
                out_ref[r, part * n:(part + 1) * n, 128 * j:128 * (j + 1)] = (
                    slab[j0 + j, pl.ds(r, n, stride=d), :].astype(out_ref.dtype))
        return
    q = tm // 4
    for lo in range(4):
        for j in range(nj):
            mid[j0 + j, lo * q:(lo + 1) * q, :] = slab[j0 + j, pl.ds(lo, q, stride=4), :]
    for hi in range(4):
        for lo in range(4):
            for j in range(nj):
                out_ref[4 * hi + lo, part * n:(part + 1) * n, 128 * j:128 * (j + 1)] = (
                    mid[j0 + j, pl.ds(lo * q + hi, n, stride=4), :].astype(out_ref.dtype))


def _gather_classes(slab, piece, nj, d, mid=None):
    tm = slab.shape[1]
    n = tm // d
    if d == 4:
        for r in range(d):
            for j in range(nj):
                slab[j, pl.ds(r, n, stride=d), :] = piece(r, j).astype(F32)
    else:
        q = tm // 4
        for hi in range(4):
            for lo in range(4):
                for j in range(nj):
                    mid[j, pl.ds(lo * q + hi, n, stride=4), :] = piece(4 * hi + lo, j).astype(F32)
        for lo in range(4):
            for j in range(nj):
                slab[j, pl.ds(lo, q, stride=4), :] = mid[j, lo * q:(lo + 1) * q, :]
    return jnp.concatenate([slab[j] for j in range(nj)], axis=1)


def _class_spec(d, width, tm):
    return pl.BlockSpec((d, tm // d, width), lambda i: (0, i, 0))


def _proj(x, tgt, norm_w, wblk, qkw, b256):
    s = x.shape[0]
    tm = TM_MATMUL
    nparts = 2
    tp = tm // nparts

    def body(x_ref, t_ref, nw_ref, w_ref, qkw_ref, b_ref, h_o, cg_o, qkr_o, qkn_o, vz_o, qkn4_o, v4_o, qkn16_o,
             v16_o, xmt_o, slabs, mids):
        for part in range(nparts):
            rows = slice(part * tp, (part + 1) * tp)
            slab = slabs.at[part]
            xf = x_ref[rows, :]
            xmt_o[rows, :] = xf - t_ref[rows, :]
            r = lax.rsqrt(jnp.mean(xf * xf, axis=-1, keepdims=True) + EPS)
            h = (xf * r * nw_ref[...]).astype(BF16)
            h_o[rows, :] = h
            p2 = jnp.dot(h, w_ref[2], preferred_element_type=F32)
            qkr_o[rows, :] = p2.astype(BF16)
            ss = _group_sum(p2 * p2, b_ref, split=False)
            rr = lax.rsqrt(ss * (1.0 / HEAD_DIM) + EPS)
            qkn = p2 * rr * qkw_ref[...]
            qkn_o[rows, :] = qkn.astype(BF16)
            _to_slabs(slab, qkn)
            p3 = jnp.dot(h, w_ref[3], preferred_element_type=F32)
            vz_o[rows, :] = p3.astype(BF16)
            _to_slabs(slab, p3[:, 0:512], 8)
            cg_o[rows, 0:1024] = jnp.dot(h, w_ref[0], preferred_element_type=F32).astype(BF16)
            cg_o[rows, 1024:2048] = jnp.dot(h, w_ref[1], preferred_element_type=F32).astype(BF16)
            for d, q_o, v_o in ((4, qkn4_o, v4_o), (16, qkn16_o, v16_o)):
                _scatter_classes(slab, 0, 8, q_o, d, part, mids.at[part])
                _scatter_classes(slab, 8, 4, v_o, d, part, mids.at[part])

    row = lambda w: pl.BlockSpec((tm, w), lambda i: (i, 0))
    full = lambda shp: pl.BlockSpec(shp, lambda i: (0,) * len(shp))
    nat = lambda w: jax.ShapeDtypeStruct((s, w), BF16)
    cls = lambda d, w: jax.ShapeDtypeStruct((d, s // d, w), BF16)
    return pl.pallas_call(
        body, name="proj", grid=(s // tm,),
        out_shape=(nat(1024), nat(2048), nat(1024), nat(1024), nat(1024),
                   cls(4, 1024), cls(4, 512), cls(16, 1024), cls(16, 512), jax.ShapeDtypeStruct((s, 1024), F32)),
        in_specs=[row(1024), row(1024), full((1, 1024)), _resident((4, 1024, 1024)), full((1, 1024)),
                  full((256, 256))],
        out_specs=(row(1024), row(2048), row(1024), row(1024), row(1024),
                   _class_spec(4, 1024, tm), _class_spec(4, 512, tm),
                   _class_spec(16, 1024, tm), _class_spec(16, 512, tm), row(1024)),
        scratch_shapes=[pltpu.VMEM((nparts, 12, tp, 128), F32), pltpu.VMEM((nparts, 12, tp, 128), F32)],
        compiler_params=_params(("parallel",)),
    )(x, tgt, norm_w, wblk, qkw, b256)


def _block_coords(t, i, nsub, nb, length):
    n = t * nsub + i
    q0 = i * QB
    start = pl.multiple_of(jnp.clip(n * QB - HALF, 0, length - KB), HALF)
    variant = jnp.where(n == 0, 0, jnp.where(n == nb - 1, 2, 1))
    return q0, start, variant


def _split_heads(a, lo):
    zero = jnp.zeros_like(a)
    return jnp.concatenate([jnp.where(lo, a, zero), jnp.where(lo, zero, a)], axis=0)


def _col_pair(ref, q0, lane):
    return jnp.concatenate([ref[pl.ds(q0, QB), lane:lane + 1],
                            ref[pl.ds(q0, QB), HEAD_DIM + lane:HEAD_DIM + lane + 1]], axis=0)


def _attn_fwd(qkn_l, v_l, bias, gi, name):
    r_cls, length, _ = qkn_l.shape
    qt = min(length, 2048)
    nb, nsub = length // QB, qt // QB

    def body(q_ref, k_ref, v_ref, b_ref, o_ref, lse_ref):
        t = pl.program_id(2)
        lo = lax.broadcasted_iota(jnp.int32, (QB, 128), 1) < HEAD_DIM

        starts, logits = [], []
        for i in range(nsub):
            _, start, variant = _block_coords(t, i, nsub, nb, length)
            qq = _split_heads(q_ref[i * QB:(i + 1) * QB, :], lo)
            k = k_ref[pl.ds(start, KB), :]
            logits.append(lax.dot_general(qq, k, (((1,), (1,)), ((), ())), preferred_element_type=F32)
                          + b_ref[variant])
            starts.append(start)
        lg = jnp.concatenate(logits, axis=0)
        m = jnp.max(lg, axis=-1, keepdims=True)
        p = jnp.exp(lg - m)
        pb = p.astype(BF16)
        l = jnp.sum(p, axis=-1, keepdims=True)
        lse = jnp.broadcast_to(m + jnp.log(l), (nsub * 2 * QB, 128))
        inv = 1.0 / l
        for i in range(nsub):
            rows = slice(2 * QB * i, 2 * QB * (i + 1))
            v = v_ref[pl.ds(starts[i], KB), :]
            pv = jnp.dot(pb[rows], v, preferred_element_type=F32) * inv[rows]
            o_ref[i * QB:(i + 1) * QB, :] = jnp.where(lo, pv[0:QB], pv[QB:2 * QB]).astype(BF16)
            ls = lse[rows]
            lse_ref[i * QB:(i + 1) * QB, :] = jnp.where(lo, ls[0:QB], ls[QB:2 * QB])

    return pl.pallas_call(
        body, name=name, grid=(N_PAIR, r_cls, length // qt),
        out_shape=(jax.ShapeDtypeStruct((r_cls, length, 512), BF16),
                   jax.ShapeDtypeStruct((r_cls, length, 512), F32)),
        in_specs=[pl.BlockSpec((None, qt, 128), lambda p, r, t: (r, t, p)),
                  pl.BlockSpec((None, length, 128), lambda p, r, t: (r, 0, 4 + p)),
                  pl.BlockSpec((None, length, 128), lambda p, r, t: (r, 0, p)),
                  pl.BlockSpec((None, 3, None, 2 * QB, KB), lambda p, r, t: (gi, 0, p, 0, 0))],
        out_specs=(pl.BlockSpec((None, qt, 128), lambda p, r, t: (r, t, p)),
                   pl.BlockSpec((None, qt, 128), lambda p, r, t: (r, t, p))),
        compiler_params=_params(("parallel", "parallel", "arbitrary")),
    )(qkn_l, qkn_l, v_l, bias)


def _combine(o_g, lse_g, cg, vz, xmt, wout, cw, cb, b256):
    s = xmt.shape[0]
    tm = TM_COMBINE
    hb = 16
    nt = s // tm

    def body(o1, o4, o16, l1, l4, l16, cg_ref, cgp_ref, cgn_ref, za_ref, xmt_ref, w_ref, cw_ref, cb_ref,
             b_ref, y_o, dout_o, ld1_o, do1_o, dza_o, dgbz_o, dzc_o, loss_o, dcb_o, dcw_o,
             do4_o, ld4_o, do16_o, ld16_o, slab, mid):
        i = pl.program_id(0)

        @pl.when(i == 0)
        def _():
            loss_o[...] = jnp.zeros_like(loss_o)
            dcb_o[...] = jnp.zeros_like(dcb_o)
            dcw_o[...] = jnp.zeros_like(dcw_o)

        u = cg_ref[:, 0:512].astype(F32)
        gb = cg_ref[:, 512:1024].astype(F32)
        gc = cg_ref[:, 1024:1536].astype(F32)
        zc = cg_ref[:, 1536:2048].astype(F32)
        tt = gc * u
        t_prev = cgp_ref[hb - 1:hb, 0:512].astype(F32) * cgp_ref[hb - 1:hb, 1024:1536].astype(F32)
        t_next = cgn_ref[0:1, 0:512].astype(F32) * cgn_ref[0:1, 1024:1536].astype(F32)
        t_prev = jnp.where(i == 0, 0.0, t_prev)
        t_next = jnp.where(i == nt - 1, 0.0, t_next)
        rows = lax.broadcasted_iota(jnp.int32, (tm, 512), 0)
        t_up = jnp.where(rows == 0, t_prev, pltpu.roll(tt, 1, 0))
        t_dn = jnp.where(rows == tm - 1, t_next, pltpu.roll(tt, tm - 1, 0))
        w0, w1, w2 = cw_ref[0:1, :], cw_ref[1:2, :], cw_ref[2:3, :]
        zb = w0 * t_up + w1 * tt + w2 * t_dn + cb_ref[...]
        sg = _sigmoid(zc)
        sz = zc * sg
        y_conv = gb * zb * sz

        a1, p1 = l1[0], o1[0].astype(F32)
        a4 = _gather_classes(slab, lambda r, j: l4[r, :, 128 * j:128 * (j + 1)], 4, 4)
        p4 = _gather_classes(slab, lambda r, j: o4[r, :, 128 * j:128 * (j + 1)], 4, 4)
        a16 = _gather_classes(slab, lambda r, j: l16[r, :, 128 * j:128 * (j + 1)], 4, 16, mid)
        p16 = _gather_classes(slab, lambda r, j: o16[r, :, 128 * j:128 * (j + 1)], 4, 16, mid)
        m = jnp.maximum(jnp.maximum(a1, a4), a16)
        e1, e4, e16 = jnp.exp(a1 - m), jnp.exp(a4 - m), jnp.exp(a16 - m)
        den = e1 + e4 + e16
        lse = m + jnp.log(den)
        o = (e1 * p1 + e4 * p4 + e16 * p16) / den
        za = za_ref[...].astype(F32)
        sga = _sigmoid(za)
        sa = za * sga
        y = jnp.concatenate([y_conv, o * sa], axis=1).astype(BF16)
        y_o[...] = y

        diff = xmt_ref[...] + jnp.dot(y, w_ref[...], preferred_element_type=F32)
        loss_o[...] += (0.5 / D_MODEL) * jnp.sum(diff * diff)
        dout = diff * (1.0 / D_MODEL)
        dout_o[...] = dout
        dy = lax.dot_general(dout.astype(BF16), w_ref[...], (((1,), (1,)), ((), ())), preferred_element_type=F32)
        dyc, dya = dy[:, 0:512], dy[:, 512:1024]

        do = dya * sa
        dza_o[...] = (dya * o * (sga * (1.0 + za * (1.0 - sga)))).astype(BF16)
        lane = lax.broadcasted_iota(jnp.int32, (tm, 512), 1)
        ld = jnp.where((lane & (HEAD_DIM - 1)) < HEAD_DIM // 2, lse, _group_sum(do * o, b_ref))
        do1_o[0] = do.astype(BF16)
        ld1_o[0] = ld
        _to_slabs(slab, do)
        _scatter_classes(slab, 0, 4, do4_o, 4)
        _scatter_classes(slab, 0, 4, do16_o, 16, 0, mid)
        _to_slabs(slab, ld)
        _scatter_classes(slab, 0, 4, ld4_o, 4)
        _scatter_classes(slab, 0, 4, ld16_o, 16, 0, mid)

        dzc = dyc * sz * gb
        dzc_o[...] = dzc.astype(BF16)
        dgbz_o[:, 0:512] = (dyc * sz * zb).astype(BF16)
        dgbz_o[:, 512:1024] = (dyc * gb * zb * (sg * (1.0 + zc * (1.0 - sg)))).astype(BF16)
        dcb_o[...] += jnp.sum(dzc, axis=0, keepdims=True)
        dcw_o[0:1, :] += jnp.sum(dzc * t_up, axis=0, keepdims=True)
        dcw_o[1:2, :] += jnp.sum(dzc * tt, axis=0, keepdims=True)
        dcw_o[2:3, :] += jnp.sum(dzc * t_dn, axis=0, keepdims=True)

    row = lambda w, j=0: pl.BlockSpec((tm, w), lambda i: (i, j))
    full = lambda shp: pl.BlockSpec(shp, lambda i: (0,) * len(shp))
    prev = pl.BlockSpec((hb, 2048), lambda i: (jnp.maximum(i * (tm // hb) - 1, 0), 0))
    nxt = pl.BlockSpec((hb, 2048), lambda i: (jnp.minimum((i + 1) * (tm // hb), s // hb - 1), 0))
    cls = lambda d, dt: jax.ShapeDtypeStruct((d, s // d, 512), dt)
    cspecs = [_class_spec(d, 512, tm) for d in DILATIONS]
    return pl.pallas_call(
        body, name="combine", grid=(nt,),
        out_shape=(jax.ShapeDtypeStruct((s, 1024), BF16), jax.ShapeDtypeStruct((s, 1024), F32),
                   cls(1, F32), cls(1, BF16), jax.ShapeDtypeStruct((s, 512), BF16),
                   jax.ShapeDtypeStruct((s, 1024), BF16), jax.ShapeDtypeStruct((s, 512), BF16),
                   jax.ShapeDtypeStruct((1, 128), F32), jax.ShapeDtypeStruct((1, 512), F32),
                   jax.ShapeDtypeStruct((8, 512), F32),
                   cls(4, BF16), cls(4, F32), cls(16, BF16), cls(16, F32)),
        in_specs=cspecs + cspecs + [row(2048), prev, nxt, row(512, 1), row(1024),
                                    _resident((1024, 1024)), full((8, 512)), full((1, 512)), full((256, 256))],
        out_specs=(row(1024), row(1024), cspecs[0], cspecs[0], row(512), row(1024), row(512),
                   full((1, 128)), full((1, 512)), full((8, 512)),
                   cspecs[1], cspecs[1], cspecs[2], cspecs[2]),
        scratch_shapes=[pltpu.VMEM((4, tm, 128), F32), pltpu.VMEM((4, tm, 128), F32)],
        compiler_params=_params(("arbitrary",)),
    )(*o_g, *lse_g, cg, cg, cg, vz, xmt, wout, cw, cb, b256)


def _attn_bwd(qkn_l, v_l, do_l, ld_l, bias, gi, name):
    r_cls, length, _ = qkn_l.shape
    qt = min(length, 2048 if length <= 4096 else 1024)
    nb, nsub, nt = length // QB, qt // QB, length // qt
    chunk = min(length, 4096)
    nchunk = length // chunk

    def body(q_ref, k_ref, v_ref, do_ref, ld_ref, b_ref, dq_ref, dkv_hbm, dsum_ref, dk_acc, dv_acc, stage, sems):
        p_id, r, t = pl.program_id(0), pl.program_id(1), pl.program_id(2)
        lo = lax.broadcasted_iota(jnp.int32, (QB, 128), 1) < HEAD_DIM

        @pl.when(t == 0)
        def _():
            dk_acc[...] = jnp.zeros_like(dk_acc)
            dv_acc[...] = jnp.zeros_like(dv_acc)

        @pl.when((t == 0) & (r == 0))
        def _():
            dsum_ref[...] = jnp.zeros_like(dsum_ref)

        nt_dims = (((1,), (1,)), ((), ()))
        tn_dims = (((0,), (0,)), ((), ()))
        coords, qqs, dds, logits, dps, lcols, dcols = [], [], [], [], [], [], []
        for i in range(nsub):
            q0, start, variant = _block_coords(t, i, nsub, nb, length)
            qq = _split_heads(q_ref[q0:q0 + QB, :], lo)
            dd = _split_heads(do_ref[q0:q0 + QB, :], lo)
            k = k_ref[pl.ds(start, KB), :]
            v = v_ref[pl.ds(start, KB), :]
            logits.append(lax.dot_general(qq, k, nt_dims, preferred_element_type=F32) + b_ref[variant])
            dps.append(lax.dot_general(dd, v, nt_dims, preferred_element_type=F32))
            lcols.append(_col_pair(ld_ref, q0, 0))
            dcols.append(_col_pair(ld_ref, q0, HEAD_DIM // 2))
            coords.append((q0, start, variant))
            qqs.append(qq)
            dds.append(dd)
        p = jnp.exp(jnp.concatenate(logits, axis=0) - jnp.concatenate(lcols, axis=0))
        ds = p * (jnp.concatenate(dps, axis=0) - jnp.concatenate(dcols, axis=0))
        pb = p.astype(BF16)
        dsb = ds.astype(BF16)
        middle = None
        for i in range(nsub):
            q0, start, variant = coords[i]
            rows = slice(2 * QB * i, 2 * QB * (i + 1))
            if 0 < i < nsub - 1:
                middle = ds[rows] if middle is None else middle + ds[rows]
            else:
                dsum_ref[variant] += ds[rows]
            dqq = jnp.dot(dsb[rows], k_ref[pl.ds(start, KB), :], preferred_element_type=F32)
            dq_ref[q0:q0 + QB, :] = jnp.where(lo, dqq[0:QB], dqq[QB:2 * QB]).astype(BF16)
            dk_acc[pl.ds(start, KB), :] += lax.dot_general(dsb[rows], qqs[i], tn_dims, preferred_element_type=F32)
            dv_acc[pl.ds(start, KB), :] += lax.dot_general(pb[rows], dds[i], tn_dims, preferred_element_type=F32)
        if middle is not None:
            dsum_ref[1] += middle

        @pl.when(t == nt - 1)
        def _():
            def copy(k):
                which, c = k // nchunk, k % nchunk
                rows = pl.ds(c * chunk, chunk)
                return pltpu.make_async_copy(stage.at[k % 2], dkv_hbm.at[r, p_id, which, rows, :], sems.at[k % 2])

            for k in range(2 * nchunk):
                if k < 2:
                    @pl.when((p_id > 0) | (r > 0))
                    def _():
                        copy(k).wait()
                else:
                    copy(k).wait()
                acc = (dk_acc, dv_acc)[k // nchunk]
                stage[k % 2] = acc[pl.ds((k % nchunk) * chunk, chunk), :].astype(BF16)
                copy(k).start()

            @pl.when((p_id == N_PAIR - 1) & (r == r_cls - 1))
            def _():
                copy(0).wait()
                copy(1).wait()

    qspec = pl.BlockSpec((None, qt, 128), lambda p, r, t: (r, t, p))
    return pl.pallas_call(
        body, name=name, grid=(N_PAIR, r_cls, nt),
        out_shape=(jax.ShapeDtypeStruct((r_cls, length, 512), BF16),
                   jax.ShapeDtypeStruct((r_cls, N_PAIR, 2, length, 128), BF16),
                   jax.ShapeDtypeStruct((N_PAIR, 3, 2 * QB, KB), F32)),
        in_specs=[qspec,
                  pl.BlockSpec((None, length, 128), lambda p, r, t: (r, 0, 4 + p)),
                  pl.BlockSpec((None, length, 128), lambda p, r, t: (r, 0, p)),
                  qspec, qspec,
                  pl.BlockSpec((None, 3, None, 2 * QB, KB), lambda p, r, t: (gi, 0, p, 0, 0))],
        out_specs=(qspec, pl.BlockSpec(memory_space=pl.ANY),
                   pl.BlockSpec((None, 3, 2 * QB, KB), lambda p, r, t: (p, 0, 0, 0))),
        scratch_shapes=[pltpu.VMEM((length, 128), F32), pltpu.VMEM((length, 128), F32),
                        pltpu.VMEM((2, chunk, 128), BF16), pltpu.SemaphoreType.DMA((2,))],
        compiler_params=_params(("arbitrary", "arbitrary", "arbitrary")),
    )(qkn_l, qkn_l, v_l, do_l, ld_l, bias)


def _bwd_tail(dq_g, dkv_g, qkr, qkw, dza, dgbz, dzc, cg, cw, wblk, x, norm_w, dout, b256):
    s = x.shape[0]
    tm = TM_COMBINE
    hb = 16
    nt = s // tm

    def body(dq1, dq4, dq16, dkv1, dkv4, dkv16, qkr_ref, qkw_ref, dza_ref, dgbz_ref, dzc_ref,
             dzp_ref, dzn_ref, u_ref, gc_ref, cw_ref, w_ref, x_ref, nw_ref, dout_ref, b_ref,
             gx_o, dproj_o, dnw_o, dqkw_o, slab, mid):
        i = pl.program_id(0)

        def nat_q(ref, d):
            return _gather_classes(slab, lambda r, j: ref[r, :, 128 * j:128 * (j + 1)], 4, d, mid)

        def nat_kv(ref, d, which):
            return _gather_classes(slab, lambda r, j: ref[r, j, which], 4, d, mid)

        @pl.when(i == 0)
        def _():
            dnw_o[...] = jnp.zeros_like(dnw_o)
            dqkw_o[...] = jnp.zeros_like(dqkw_o)

        dzc = dzc_ref[...].astype(F32)
        d_prev = jnp.where(i == 0, 0.0, dzp_ref[hb - 1:hb, :].astype(F32))
        d_next = jnp.where(i == nt - 1, 0.0, dzn_ref[0:1, :].astype(F32))
        rows = lax.broadcasted_iota(jnp.int32, (tm, 512), 0)
        d_up = jnp.where(rows == 0, d_prev, pltpu.roll(dzc, 1, 0))
        d_dn = jnp.where(rows == tm - 1, d_next, pltpu.roll(dzc, tm - 1, 0))
        dt = cw_ref[0:1, :] * d_dn + cw_ref[1:2, :] * dzc + cw_ref[2:3, :] * d_up
        u = u_ref[...].astype(F32)
        gc = gc_ref[...].astype(F32)
        dproj_o[:, 0:512] = (dt * gc).astype(BF16)
        dproj_o[:, 512:1024] = dgbz_ref[:, 0:512]
        dproj_o[:, 1024:1536] = (dt * u).astype(BF16)
        dproj_o[:, 1536:2048] = dgbz_ref[:, 512:1024]

        dqn = (dq1[0].astype(F32) + nat_q(dq4, 4) + nat_q(dq16, 16)) * (1.0 / 8.0)
        dk1 = jnp.concatenate([dkv1[0, j, 0] for j in range(N_PAIR)], axis=1)
        dv1 = jnp.concatenate([dkv1[0, j, 1] for j in range(N_PAIR)], axis=1)
        dkn = dk1 + nat_kv(dkv4, 4, 0) + nat_kv(dkv16, 16, 0)
        dvn = dv1 + nat_kv(dkv4, 4, 1) + nat_kv(dkv16, 16, 1)
        g = jnp.concatenate([dqn, dkn], axis=1) * qkw_ref[...]
        raw = qkr_ref[...].astype(F32)
        rr = lax.rsqrt(_group_sum(raw * raw, b_ref, split=False) * (1.0 / HEAD_DIM) + EPS)
        proj_gq = _group_sum(g * raw, b_ref) * (1.0 / HEAD_DIM)
        draw = rr * g - raw * (rr * rr * rr) * proj_gq
        dqkw_o[...] += jnp.sum(jnp.concatenate([dqn, dkn], axis=1) * raw * rr, axis=0, keepdims=True)
        dproj_o[:, 2048:3072] = draw.astype(BF16)
        dproj_o[:, 3072:3584] = dvn.astype(BF16)
        dproj_o[:, 3584:4096] = dza_ref[...]

        nt_dims = (((1,), (1,)), ((), ()))
        dh = lax.dot_general(dproj_o[:, 0:1024], w_ref[0], nt_dims, preferred_element_type=F32)
        for b in range(1, 4):
            dh += lax.dot_general(dproj_o[:, 1024 * b:1024 * b + 1024], w_ref[b], nt_dims,
                                  preferred_element_type=F32)

        xf = x_ref[...]
        r = lax.rsqrt(jnp.mean(xf * xf, axis=-1, keepdims=True) + EPS)
        gh = dh * nw_ref[...]
        dnw_o[...] += jnp.sum(dh * xf * r, axis=0, keepdims=True)
        mean_gx = jnp.mean(gh * xf, axis=-1, keepdims=True)
        gx_o[...] = dout_ref[...] + r * gh - xf * (r * r * r) * mean_gx

    row = lambda w, j=0: pl.BlockSpec((tm, w), lambda i: (i, j))
    full = lambda shp: pl.BlockSpec(shp, lambda i: (0,) * len(shp))
    prev = pl.BlockSpec((hb, 512), lambda i: (jnp.maximum(i * (tm // hb) - 1, 0), 0))
    nxt = pl.BlockSpec((hb, 512), lambda i: (jnp.minimum((i + 1) * (tm // hb), s // hb - 1), 0))
    return pl.pallas_call(
        body, name="bwd_tail", grid=(nt,),
        out_shape=(jax.ShapeDtypeStruct((s, 1024), F32), jax.ShapeDtypeStruct((s, 4096), BF16),
                   jax.ShapeDtypeStruct((1, 1024), F32), jax.ShapeDtypeStruct((1, 1024), F32)),
        in_specs=[_class_spec(d, 512, tm) for d in DILATIONS]
        + [pl.BlockSpec((d, N_PAIR, 2, tm // d, 128), lambda i: (0, 0, 0, i, 0)) for d in DILATIONS]
        + [row(1024), full((1, 1024)), row(512), row(1024), row(512), prev, nxt,
           row(512, 0), row(512, 2), full((8, 512)), _resident((4, 1024, 1024)), row(1024),
           full((1, 1024)), row(1024), full((256, 256))],
        out_specs=(row(1024), row(4096), full((1, 1024)), full((1, 1024))),
        scratch_shapes=[pltpu.VMEM((4, tm, 128), F32), pltpu.VMEM((4, tm, 128), F32)],
        compiler_params=_params(("arbitrary",)),
    )(*dq_g, *dkv_g, qkr, qkw, dza, dgbz, dzc, dzc, dzc, cg, cg, cw, wblk, x, norm_w, dout, b256)


def _wgrad(a, b, row_blocked, name):
    s, m = a.shape
    n = b.shape[1]
    tk = 1024
    ncol = min(n, 2048)
    nj, nk = n // ncol, s // tk

    def body(a_ref, b_ref, o_ref, acc):
        kk = pl.program_id(1)

        @pl.when(kk == 0)
        def _():
            acc[...] = jnp.zeros_like(acc)

        acc[...] += lax.dot_general(a_ref[...], b_ref[...].astype(BF16), (((0,), (0,)), ((), ())),
                                    preferred_element_type=F32)

        @pl.when(kk == nk - 1)
        def _():
            blocks, _, rows, _ = o_ref.shape
            for blk in range(blocks):
                for half in range(2):
                    if row_blocked:
                        r0 = (2 * blk + half) * rows
                        o_ref[blk, half] = acc[r0:r0 + rows, :].astype(BF16)
                    else:
                        o_ref[blk, half] = acc[half * rows:(half + 1) * rows,
                                               1024 * blk:1024 * (blk + 1)].astype(BF16)

    if row_blocked:
        out_shape = jax.ShapeDtypeStruct((4, 2, m // 8, 1024), BF16)
        out_spec = pl.BlockSpec((4, 2, m // 8, 1024), lambda j, k: (0, 0, 0, 0))
    else:
        out_shape = jax.ShapeDtypeStruct((n // 1024, 2, m // 2, 1024), BF16)
        out_spec = pl.BlockSpec((ncol // 1024, 2, m // 2, 1024), lambda j, k: (j, 0, 0, 0))
    return pl.pallas_call(
        body, name=name, grid=(nj, nk),
        out_shape=out_shape,
        in_specs=[pl.BlockSpec((tk, m), lambda j, k: (k, 0)), pl.BlockSpec((tk, ncol), lambda j, k: (k, j))],
        out_specs=out_spec,
        scratch_shapes=[pltpu.VMEM((m, ncol), F32)],
        compiler_params=_params(("parallel", "arbitrary")),
    )(a, b)


def _dbias(dsums, onehot_all):
    def body(ds1_ref, ds4_ref, ds16_ref, oh_ref, o_ref):
        @pl.when(pl.program_id(0) == 0)
        def _():
            o_ref[...] = jnp.zeros_like(o_ref)

        hrow = lax.broadcasted_iota(jnp.int32, (8, KB), 0)
        flip = (lax.broadcasted_iota(jnp.int32, (QB, QB), 0)
                + lax.broadcasted_iota(jnp.int32, (QB, QB), 1) == QB - 1).astype(F32)

        def diagonal_sums(tile):
            rev = jnp.dot(flip, tile, preferred_element_type=F32, precision=lax.Precision.HIGHEST)
            sums = jnp.sum(pltpu.roll(rev, 0, 1, stride=1, stride_axis=0), axis=0, keepdims=True)
            return pltpu.roll(sums, KB - (QB - 1), 1)

        for g, ds_ref in enumerate((ds1_ref, ds4_ref, ds16_ref)):
            diag = jnp.zeros((8, KB), F32)
            for p in range(N_PAIR):
                diag = jnp.where(hrow == 2 * p, diagonal_sums(ds_ref[p, 0:QB, :]), diag)
                diag = jnp.where(hrow == 2 * p + 1, diagonal_sums(ds_ref[p, QB:2 * QB, :]), diag)
            o_ref[...] += jnp.dot(diag, oh_ref[g], preferred_element_type=F32, precision=lax.Precision.HIGHEST)

    ds_spec = pl.BlockSpec((N_PAIR, None, 2 * QB, KB), lambda v: (0, v, 0, 0))
    return pl.pallas_call(
        body, name="dbias", grid=(3,),
        out_shape=jax.ShapeDtypeStruct((8, 128), F32),
        in_specs=[ds_spec, ds_spec, ds_spec, pl.BlockSpec((3, None, KB, 128), lambda v: (0, v, 0, 0))],
        out_specs=pl.BlockSpec((8, 128), lambda v: (0, 0)),
        compiler_params=_params(("arbitrary",)),
    )(*dsums, onehot_all)


def _gsync(pw_in, pw_out, small):
    hin, hout = pw_in.shape[2], pw_out.shape[2]
    nsmall = small.shape[0]

    def body(pin_hbm, pout_hbm, small_ref, gin_o, gout_o, small_o,
             mine_in, recv_in, s1_in, r1_in, s2_in, r2_in, mine_out, recv_out, s1_out, r1_out, s2_out, r2_out, gather,
             lsem, asend, arecv, bsend, brecv, csend, crecv, ssend, srecv):
        x, y, c = lax.axis_index("x"), lax.axis_index("y"), lax.axis_index("c")
        b = 2 * x + y
        dev = 4 * x + 2 * y + c
        sib = (x, y, 1 - c)
        xnbr, ynbr = (1 - x, y, c), (x, 1 - y, c)
        bx, by, bd = b ^ 2, b ^ 1, b ^ 3

        def rcopy(src, dst, ssem, rsem, to):
            return pltpu.make_async_remote_copy(src_ref=src, dst_ref=dst, send_sem=ssem, recv_sem=rsem,
                                                device_id=to, device_id_type=MESH)

        gather[dev] = small_ref[...]
        s_sends = []
        for k in range(1, 8):
            to = (x ^ (k >> 2), y ^ ((k >> 1) & 1), c ^ (k & 1))
            cp = rcopy(gather.at[dev], gather.at[dev], ssend.at[k - 1], srecv.at[k - 1], to)
            cp.start()
            s_sends.append(cp)

        a_in = rcopy(pin_hbm.at[:, 1 - c], recv_in, asend.at[0], arecv.at[0], sib)
        a_out = rcopy(pout_hbm.at[:, 1 - c], recv_out, asend.at[1], arecv.at[1], sib)
        a_in.start()
        a_out.start()
        l_in = pltpu.make_async_copy(pin_hbm.at[:, c], mine_in, lsem.at[0])
        l_out = pltpu.make_async_copy(pout_hbm.at[:, c], mine_out, lsem.at[1])
        l_in.start()
        l_out.start()
        l_in.wait()
        l_out.wait()

        def phase_one(a_cp, mine, recv, s1, r1, half, base):
            a_cp.wait_recv()
            q = half // 2
            sends = []
            for part, (peer, blk_peer) in enumerate(((xnbr, bx), (ynbr, by))):
                rows = pl.ds(part * q, q)
                for slot, blk in enumerate((blk_peer, bd)):
                    s1[part, slot] = (mine[blk, rows, :].astype(F32) + recv[blk, rows, :].astype(F32)).astype(BF16)
                cp = rcopy(s1.at[part], r1.at[part], bsend.at[base + part], brecv.at[base + part], peer)
                cp.start()
                sends.append(cp)
            return sends

        def phase_two(p1, mine, recv, r1, s2, r2, half, base):
            q = half // 2
            own, sends = [], []
            for part, (peer, blk_next) in enumerate(((ynbr, by), (xnbr, bx))):
                rows = pl.ds(part * q, q)
                p1[part].wait_recv()
                own.append(mine[b, rows, :].astype(F32) + recv[b, rows, :].astype(F32) + r1[part, 0].astype(F32))
                s2[part] = (mine[blk_next, rows, :].astype(F32) + recv[blk_next, rows, :].astype(F32)
                            + r1[part, 1].astype(F32)).astype(BF16)
                cp = rcopy(s2.at[part], r2.at[part], bsend.at[base + 2 + part], brecv.at[base + 2 + part], peer)
                cp.start()
                sends.append(cp)
            return own, sends

        def stage_c(own, p2, r2, g_o, half, idx):
            q = half // 2
            for part in range(2):
                p2[part].wait_recv()
                g_o[pl.ds(pl.multiple_of(c * half + part * q, q), q), :] = own[part] + r2[part].astype(F32)
            rows = g_o.at[pl.ds(pl.multiple_of(c * half, half), half), :]
            cp = rcopy(rows, rows, csend.at[idx], crecv.at[idx], sib)
            cp.start()
            return cp

        p1_in = phase_one(a_in, mine_in, recv_in, s1_in, r1_in, hin, 0)
        p1_out = phase_one(a_out, mine_out, recv_out, s1_out, r1_out, hout, 4)
        own_in, p2_in = phase_two(p1_in, mine_in, recv_in, r1_in, s2_in, r2_in, hin, 0)
        own_out, p2_out = phase_two(p1_out, mine_out, recv_out, r1_out, s2_out, r2_out, hout, 4)
        c_in = stage_c(own_in, p2_in, r2_in, gin_o, hin, 0)
        c_out = stage_c(own_out, p2_out, r2_out, gout_o, hout, 1)
        b_in, b_out = p1_in + p2_in, p1_out + p2_out

        for cp in s_sends:
            cp.wait_recv()
        tot = gather[0]
        for d in range(1, 8):
            tot = tot + gather[d]
        small_o[...] = tot

        for g_o, half, idx in ((gin_o, hin, 0), (gout_o, hout, 1)):
            other = g_o.at[pl.ds(pl.multiple_of((1 - c) * half, half), half), :]
            rcopy(other, other, csend.at[idx], crecv.at[idx], sib).wait_recv()
        for cp in s_sends + [a_in, a_out] + b_in + b_out + [c_in, c_out]:
            cp.wait_send()

    vm = pl.BlockSpec(memory_space=pltpu.VMEM)
    hbm = pl.BlockSpec(memory_space=pl.ANY)
    return pl.pallas_call(
        body, name="gsync",
        out_shape=(jax.ShapeDtypeStruct((2 * hin, 1024), F32), jax.ShapeDtypeStruct((2 * hout, 1024), F32),
                   jax.ShapeDtypeStruct((nsmall, 128), F32)),
        in_specs=[hbm, hbm, vm], out_specs=(vm, vm, vm),
        scratch_shapes=[pltpu.VMEM((4, hin, 1024), BF16), pltpu.VMEM((4, hin, 1024), BF16),
                        pltpu.VMEM((2, 2, hin // 2, 1024), BF16), pltpu.VMEM((2, 2, hin // 2, 1024), BF16),
                        pltpu.VMEM((2, hin // 2, 1024), BF16), pltpu.VMEM((2, hin // 2, 1024), BF16),
                        pltpu.VMEM((4, hout, 1024), BF16), pltpu.VMEM((4, hout, 1024), BF16),
                        pltpu.VMEM((2, 2, hout // 2, 1024), BF16), pltpu.VMEM((2, 2, hout // 2, 1024), BF16),
                        pltpu.VMEM((2, hout // 2, 1024), BF16), pltpu.VMEM((2, hout // 2, 1024), BF16),
                        pltpu.VMEM((8, nsmall, 128), F32),
                        pltpu.SemaphoreType.DMA((2,)),
                        pltpu.SemaphoreType.DMA((2,)), pltpu.SemaphoreType.DMA((2,)),
                        pltpu.SemaphoreType.DMA((8,)), pltpu.SemaphoreType.DMA((8,)),
                        pltpu.SemaphoreType.DMA((2,)), pltpu.SemaphoreType.DMA((2,)),
                        pltpu.SemaphoreType.DMA((7,)), pltpu.SemaphoreType.DMA((7,))],
        compiler_params=_params(),
    )(pw_in, pw_out, small)


def _adamw_math(w, g, m, v):
    m = ADAM_B1 * m + (1.0 - ADAM_B1) * g
    v = ADAM_B2 * v + (1.0 - ADAM_B2) * (g * g)
    m_hat = m / (1.0 - ADAM_B1 ** ADAM_STEP)
    v_hat = v / (1.0 - ADAM_B2 ** ADAM_STEP)
    delta = -ADAM_LR * (m_hat / (jnp.sqrt(v_hat) + ADAM_EPS) + ADAM_WD * w)
    return delta, m, v


def _adamw(w, g, m, v, name):
    rows, cols = w.shape
    tr = 256 if rows % 256 == 0 else rows

    def body(w_ref, g_ref, m_ref, v_ref, g_o, d_o, m_o, v_o):
        g = g_ref[...]
        d, m2, v2 = _adamw_math(w_ref[...], g, m_ref[...], v_ref[...])
        g_o[...] = g
        d_o[...] = d
        m_o[...] = m2
        v_o[...] = v2

    spec = pl.BlockSpec((tr, cols), lambda i: (i, 0))
    shp = jax.ShapeDtypeStruct((rows, cols), F32)
    return pl.pallas_call(
        body, name=name, grid=(rows // tr,), out_shape=(shp, shp, shp, shp),
        in_specs=[spec] * 4, out_specs=(spec, spec, spec, spec),
        compiler_params=_params(("parallel",)),
    )(w, g, m, v)


def _fold_heads(dqkw):
    def body(x_ref, o_ref):
        xs = x_ref[...]
        sq = xs[0:1] + xs[1:2] + xs[2:3] + xs[3:4]
        sk = xs[4:5] + xs[5:6] + xs[6:7] + xs[7:8]
        both = jnp.concatenate([sq, sk], axis=0)
        o_ref[...] = both + pltpu.roll(both, HEAD_DIM, 1)

    vm = pl.BlockSpec(memory_space=pltpu.VMEM)
    return pl.pallas_call(body, name="fold_heads", out_shape=jax.ShapeDtypeStruct((2, 128), F32),
                          in_specs=[vm], out_specs=vm, compiler_params=_params())(dqkw)


def kernel(x, norm_w, w_in, conv_w, conv_b, q_norm_w, k_norm_w, rel_bias, w_out, loss_target, m_norm_w, m_w_in, m_conv_w, m_conv_b, m_q_norm_w, m_k_norm_w, m_rel_bias, m_w_out, v_norm_w, v_w_in, v_conv_w, v_conv_b, v_q_norm_w, v_k_norm_w, v_rel_bias, v_w_out):
    x2 = x[0]
    tgt = loss_target[0]
    blk = 2 * lax.axis_index("x") + lax.axis_index("y")

    conv_w8 = jnp.pad(conv_w, ((0, 5), (0, 0)))
    wblk, woutblk, cwblk = _wgather(w_in, w_out, conv_w8)
    wout_full = woutblk.reshape(1024, 1024)
    cw_full = cwblk.transpose(1, 0, 2).reshape(8, 512)

    qkw = jnp.concatenate([jnp.tile(q_norm_w, 8) * 0.125, jnp.tile(k_norm_w, 8)])[None, :]
    qkw_raw = jnp.concatenate([jnp.tile(q_norm_w, 8), jnp.tile(k_norm_w, 8)])[None, :]
    gidx = jnp.arange(256) // HEAD_DIM
    b256 = (gidx[:, None] == gidx[None, :]).astype(BF16)

    h, cg, qkr, qkn, vz, qkn4, v4, qkn16, v16, xmt = _proj(x2, tgt, norm_w[None, :], wblk, qkw, b256)

    biases = _bias_tables(rel_bias)
    qkn_l = [qkn[None], qkn4, qkn16]
    v_l = [vz[None], v4, v16]
    o_g, lse_g = [], []
    for gi, d in enumerate(DILATIONS):
        o_l, lse_l = _attn_fwd(qkn_l[gi], v_l[gi], biases, gi, f"attn_fwd_d{d}")
        o_g.append(o_l)
        lse_g.append(lse_l)

    (y, dout, ld1, do1, dza, dgbz, dzc, loss_p, dcb, dcw, do4, ld4, do16, ld16) = _combine(
        o_g, lse_g, cg, vz, xmt, wout_full, cw_full, conv_b[None, :], b256)

    dq_g, dkv_g, dsums = [], [], []
    for gi, (d, do_l, ld_l) in enumerate(zip(DILATIONS, (do1, do4, do16), (ld1, ld4, ld16))):
        dq_l, dkv_l, dsum = _attn_bwd(qkn_l[gi], v_l[gi], do_l, ld_l, biases, gi, f"attn_bwd_d{d}")
        dq_g.append(dq_l)
        dkv_g.append(dkv_l)
        dsums.append(dsum)

    grad_x, dproj, dnw, dqkw = _bwd_tail(dq_g, dkv_g, qkr, qkw_raw, dza, dgbz, dzc, cg, cw_full, wblk,
                                         x2, norm_w[None, :], dout, b256)

    pw_in = _wgrad(h, dproj, False, "wgrad_in")
    pw_out = _wgrad(y, dout, True, "wgrad_out")
    dbias8 = _dbias(dsums, jnp.stack([_diag_bucket_onehot(d) for d in DILATIONS], axis=0))

    small = jnp.concatenate([dnw.reshape(8, 128), dcb.reshape(4, 128), dqkw.reshape(8, 128),
                             dcw[0:3].reshape(12, 128), dbias8, jnp.pad(loss_p, ((0, 7), (0, 0)))], axis=0)
    g_win, g_wout, gsmall = _gsync(pw_in, pw_out, small)

    g_nw = gsmall[0:8].reshape(1024)
    g_cb = gsmall[8:12].reshape(512)
    folded = _fold_heads(gsmall[12:20])
    g_qw, g_kw = folded[0, 0:64], folded[1, 0:64]
    g_cw = lax.dynamic_slice(gsmall[20:32].reshape(3, 512), (0, blk * 128), (3, 128))
    g_rb = gsmall[32:40][:, 0:32].T
    loss = gsmall[40, 0]

    g_win, d_win, nm_win, nv_win = _adamw(w_in, g_win, m_w_in, v_w_in, "adamw_w_in")
    g_wout, d_wout, nm_wout, nv_wout = _adamw(w_out, g_wout, m_w_out, v_w_out, "adamw_w_out")

    def pack(parts):
        rows = [parts[0].reshape(8, 128), parts[1].reshape(4, 128),
                jnp.pad(parts[2], (0, 64))[None, :], jnp.pad(parts[3], (0, 64))[None, :],
                parts[4], jnp.pad(parts[5].T, ((0, 0), (0, 96)))]
        return jnp.concatenate(rows, axis=0)

    ws = pack([norm_w, conv_b, q_norm_w, k_norm_w, conv_w, rel_bias])
    gs = pack([g_nw, g_cb, g_qw, g_kw, g_cw, g_rb])
    ms = pack([m_norm_w, m_conv_b, m_q_norm_w, m_k_norm_w, m_conv_w, m_rel_bias])
    vs = pack([v_norm_w, v_conv_b, v_q_norm_w, v_k_norm_w, v_conv_w, v_rel_bias])
    rpad = lambda a: jnp.pad(a, ((0, 7), (0, 0)))
    _, d_s, nm_s, nv_s = _adamw(rpad(ws), rpad(gs), rpad(ms), rpad(vs), "adamw_small")

    def unpack(a):
        return (a[0:8].reshape(1024), a[12:13, 0:64].reshape(64), a[13:14, 0:64].reshape(64),
                a[14:17], a[8:12].reshape(512), a[17:25, 0:32].T)

    def ordered(nw, win, cw, cb, qw, kw, rb, wout):
        return (nw, win, cw, cb, qw, kw, rb, wout)

    g_un = (g_nw, g_qw, g_kw, g_cw, g_cb, g_rb)
    outs = [loss, grad_x[None]]
    for un, win_v, wout_v in ((g_un, g_win, g_wout), (unpack(d_s), d_win, d_wout),
                              (unpack(nm_s), nm_win, nm_wout), (unpack(nv_s), nv_win, nv_wout)):
        nw, qw, kw, cw, cb, rb = un
        outs.extend(ordered(nw, win_v, cw, cb, qw, kw, rb, wout_v))
    return tuple(outs)
```

```python
import math

import jax
import jax.numpy as jnp
from jax import lax
from jax.experimental import pallas as pl
from jax.experimental.pallas import tpu as pltpu

F32 = jnp.float32
BF16 = jnp.bfloat16
MESH = pl.DeviceIdType.MESH

D_MODEL = 1024
CONV_W = 512
ATTN_W = 512
HEAD_DIM = 64
N_PAIR = 4
DILATIONS = (1, 4, 16)
HALF = 64
QB = 128
KB = QB + 2 * HALF
NUM_BUCKETS = 32
MAX_DISTANCE = 1024
EPS = 1e-6
NEG = -1e30
ADAM_LR, ADAM_B1, ADAM_B2, ADAM_EPS, ADAM_WD, ADAM_STEP = 0.001, 0.9, 0.999, 1e-08, 0.01, 10
VMEM_LIMIT = 48 << 20


def _params(sem=None, vmem=VMEM_LIMIT, **kw):
    if sem is not None:
        kw["dimension_semantics"] = sem
    return pltpu.CompilerParams(vmem_limit_bytes=vmem, **kw)


def _sigmoid(z):
    return 1.0 / (1.0 + jnp.exp(-z))


def _group_sum(val, b_ref, split=True):
    hi = val.astype(BF16)
    lo = (val - hi.astype(F32)).astype(BF16) if split else None
    outs = []
    for j in range(val.shape[1] // 256):
        sl = slice(256 * j, 256 * j + 256)
        part = jnp.dot(hi[:, sl], b_ref[...], preferred_element_type=F32)
        if split:
            part = part + jnp.dot(lo[:, sl], b_ref[...], preferred_element_type=F32)
        outs.append(part)
    return outs[0] if len(outs) == 1 else jnp.concatenate(outs, axis=1)


def _t5_bucket(rel):
    half_b = NUM_BUCKETS // 2
    max_exact = half_b // 2
    ret = jnp.where(rel > 0, half_b, 0)
    n = jnp.abs(rel)
    nf = jnp.maximum(n, 1).astype(F32)
    large = max_exact + (jnp.log(nf / max_exact) / math.log(MAX_DISTANCE / max_exact)
                         * (half_b - max_exact)).astype(jnp.int32)
    large = jnp.minimum(large, half_b - 1)
    return ret + jnp.where(n < max_exact, n, large)


def _bias_tables(rel_bias):
    rows = []
    key = jnp.arange(KB)
    for dilation in DILATIONS:
        for variant in range(3):
            off = (0, HALF, 2 * HALF)[variant]
            rel = ((key - off + KB // 2) % KB) - KB // 2
            bkt = _t5_bucket(jnp.clip(rel, -HALF, HALF) * dilation)
            rows.append(jnp.where(jnp.abs(rel) <= HALF, bkt, -1))
    bkt_all = jnp.broadcast_to(jnp.stack(rows, axis=0).astype(jnp.int32)[:, None, :], (9, 8, KB))

    def body(rb_ref, bkt_ref, o_ref):
        bkt = bkt_ref[...]
        off = (pl.program_id(0) % 3) * HALF
        rel = (lax.broadcasted_iota(jnp.int32, (QB, KB), 1) - lax.broadcasted_iota(jnp.int32, (QB, KB), 0)) - off
        band = jnp.abs(rel) <= HALF
        for h in range(8):
            acc = jnp.full((8, KB), NEG, F32)
            for b in range(NUM_BUCKETS):
                acc = jnp.where(bkt == b, rb_ref[b, h], acc)
            rolled = pltpu.roll(jnp.broadcast_to(acc[0:1], (QB, KB)), 0, 1, stride=1, stride_axis=0)
            o_ref[h] = jnp.where(band, rolled, NEG)

    out = pl.pallas_call(
        body, name="bias_tables", grid=(9,),
        out_shape=jax.ShapeDtypeStruct((9, 8, QB, KB), F32),
        in_specs=[pl.BlockSpec(memory_space=pltpu.SMEM), pl.BlockSpec((None, 8, KB), lambda i: (i, 0, 0))],
        out_specs=pl.BlockSpec((None, 8, QB, KB), lambda i: (i, 0, 0, 0)),
        compiler_params=_params(("parallel",)),
    )(rel_bias, bkt_all)
    return out.reshape(3, 3, N_PAIR, 2 * QB, KB)


def _diag_bucket_onehot(dilation):
    out = []
    c = jnp.arange(KB)
    for variant in range(3):
        off = (0, HALF, 2 * HALF)[variant]
        rel = ((c - off + 128) % 256) - 128
        band = jnp.abs(rel) <= HALF
        bkt = _t5_bucket(jnp.clip(rel, -HALF, HALF) * dilation)
        oh = (bkt[:, None] == jnp.arange(128)[None, :]) & band[:, None]
        out.append(oh.astype(F32))
    return jnp.stack(out, axis=0)


def _wgather(w_in, w_out, conv_w):
    rin, rout = w_in.shape[0] // 2, w_out.shape[0] // 2

    def body(win_ref, wout_ref, cw_ref, win_o, wout_o, cw_o, send_sems, recv_sems):
        x, y, c = lax.axis_index("x"), lax.axis_index("y"), lax.axis_index("c")
        b = 2 * x + y
        win_o[b] = win_ref[...].astype(BF16)
        wout_o[b] = wout_ref[...].astype(BF16)
        cw_o[b] = cw_ref[...]
        xnbr, ynbr, diag, sib = (1 - x, y, c), (x, 1 - y, c), (1 - x, 1 - y, c), (x, y, 1 - c)
        bx, by, bd = b ^ 2, b ^ 1, b ^ 3

        def copy(sem, ref, to):
            return pltpu.make_async_remote_copy(src_ref=ref, dst_ref=ref, send_sem=send_sems.at[sem],
                                                recv_sem=recv_sems.at[sem], device_id=to, device_id_type=MESH)

        def rows_of(ref, half):
            def rows(blk, quarter=None):
                if quarter is None:
                    return ref.at[blk, pl.ds(c * half, half), :]
                return ref.at[blk, pl.ds(c * half + quarter * (half // 2), half // 2), :]
            return rows

        def send_own(ref, half, base):
            rows = rows_of(ref, half)
            sends = [copy(base + 0, rows(b, 0), xnbr), copy(base + 2, rows(b, 1), ynbr),
                     copy(base + 1, rows(b, 1), xnbr), copy(base + 3, rows(b, 0), ynbr)]
            for cp in sends:
                cp.start()
            return sends

        def relay(ref, half, base):
            rows = rows_of(ref, half)
            copy(base + 0, rows(bx, 0), xnbr).wait_recv()
            pass_y = copy(base + 4, rows(bx, 0), ynbr)
            pass_y.start()
            copy(base + 2, rows(by, 1), ynbr).wait_recv()
            pass_x = copy(base + 5, rows(by, 1), xnbr)
            pass_x.start()
            to_sib = []
            for k, (blk, landed) in enumerate(((bx, ((1, 1, xnbr),)), (by, ((3, 0, ynbr),)),
                                               (bd, ((4, 0, ynbr), (5, 1, xnbr))))):
                for sem, quarter, frm in landed:
                    copy(base + sem, rows(blk, quarter), frm).wait_recv()
                to_sib.append(copy(base + 6 + k, rows(blk), sib))
                to_sib[-1].start()
            return [pass_y, pass_x] + to_sib

        def from_sibling(ref, half, base):
            for k, blk in enumerate((bx, by, bd)):
                copy(base + 6 + k, ref.at[blk, pl.ds((1 - c) * half, half), :], sib).wait_recv()

        small = [copy(18 + k, cw_o.at[b], to) for k, to in enumerate((xnbr, ynbr, diag))]
        for cp in small:
            cp.start()
        started = send_own(win_o, rin, 0) + send_own(wout_o, rout, 9)
        started += relay(win_o, rin, 0) + relay(wout_o, rout, 9)
        for k, blk in enumerate((bx, by, bd)):
            copy(18 + k, cw_o.at[blk], sib).wait_recv()
        from_sibling(win_o, rin, 0)
        from_sibling(wout_o, rout, 9)
        for cp in small + started:
            cp.wait_send()

    vm = pl.BlockSpec(memory_space=pltpu.VMEM)
    return pl.pallas_call(
        body, name="wgather",
        out_shape=(jax.ShapeDtypeStruct((4,) + w_in.shape, BF16),
                   jax.ShapeDtypeStruct((4,) + w_out.shape, BF16),
                   jax.ShapeDtypeStruct((4,) + conv_w.shape, F32)),
        in_specs=[vm, vm, vm], out_specs=(vm, vm, vm),
        scratch_shapes=[pltpu.SemaphoreType.DMA((21,)), pltpu.SemaphoreType.DMA((21,))],
        compiler_params=_params(),
    )(w_in, w_out, conv_w)


TM_MATMUL = 512
TM_COMBINE = 256


def _resident(shape):
    return pl.BlockSpec(shape, lambda i: (0,) * len(shape), pipeline_mode=pl.Buffered(1))


def _to_slabs(slab, val, j0=0):
    for j in range(val.shape[1] // 128):
        slab[j0 + j] = val[:, 128 * j:128 * (j + 1)]


def _scatter_classes(slab, j0, nj, out_ref, d, part=0, mid=None):
    tm = slab.shape[1]
    n = tm // d
    if d == 4:
        for r in range(d):
            for j in range(nj):
                out_ref[r, part * n:(part + 1) * n, 128 * j:128 * (j + 1)] = (
                    slab[j0 + j, pl.ds(r, n, stride=d), :].astype(out_ref.dtype))
        return
    q = tm // 4
    for lo in range(4):
        for j in range(nj):
            mid[j0 + j, lo * q:(lo + 1) * q, :] = slab[j0 + j, pl.ds(lo, q, stride=4), :]
    for hi in range(4):
        for lo in range(4):
            for j in range(nj):
                out_ref[4 * hi + lo, part * n:(part + 1) * n, 128 * j:128 * (j + 1)] = (
                    mid[j0 + j, pl.ds(lo * q + hi, n, stride=4), :].astype(out_ref.dtype))


def _gather_classes(slab, piece, nj, d, mid=None):
    tm = slab.shape[1]
    n = tm // d
    if d == 4:
        for r in range(d):
            for j in range(nj):
                slab[j, pl.ds(r, n, stride=d), :] = piece(r, j).astype(F32)
    else:
        q = tm // 4
        for hi in range(4):
            for lo in range(4):
                for j in range(nj):
                    mid[j, pl.ds(lo * q + hi, n, stride=4), :] = piece(4 * hi + lo, j).astype(F32)
        for lo in range(4):
            for j in range(nj):
                slab[j, pl.ds(lo, q, stride=4), :] = mid[j, lo * q:(lo + 1) * q, :]
    return jnp.concatenate([slab[j] for j in range(nj)], axis=1)


def _class_spec(d, width, tm):
    return pl.BlockSpec((d, tm // d, width), lambda i: (0, i, 0))


def _proj(x, tgt, norm_w, wblk, qkw, b256):
    s = x.shape[0]
    tm = TM_MATMUL
    nparts = 2
    tp = tm // nparts

    def body(x_ref, t_ref, nw_ref, w_ref, qkw_ref, b_ref, h_o, cg_o, qkr_o, qkn_o, vz_o, qkn4_o, v4_o, qkn16_o,
             v16_o, xmt_o, slabs, mids):
        for part in range(nparts):
            rows = slice(part * tp, (part + 1) * tp)
            slab = slabs.at[part]
            xf = x_ref[rows, :]
            xmt_o[rows, :] = xf - t_ref[rows, :]
            r = lax.rsqrt(jnp.mean(xf * xf, axis=-1, keepdims=True) + EPS)
            h = (xf * r * nw_ref[...]).astype(BF16)
            h_o[rows, :] = h
            p2 = jnp.dot(h, w_ref[2], preferred_element_type=F32)
            qkr_o[rows, :] = p2.astype(BF16)
            ss = _group_sum(p2 * p2, b_ref, split=False)
            rr = lax.rsqrt(ss * (1.0 / HEAD_DIM) + EPS)
            qkn = p2 * rr * qkw_ref[...]
            qkn_o[rows, :] = qkn.astype(BF16)
            _to_slabs(slab, qkn)
            p3 = jnp.dot(h, w_ref[3], preferred_element_type=F32)
            vz_o[rows, :] = p3.astype(BF16)
            _to_slabs(slab, p3[:, 0:512], 8)
            cg_o[rows, 0:1024] = jnp.dot(h, w_ref[0], preferred_element_type=F32).astype(BF16)
            cg_o[rows, 1024:2048] = jnp.dot(h, w_ref[1], preferred_element_type=F32).astype(BF16)
            for d, q_o, v_o in ((4, qkn4_o, v4_o), (16, qkn16_o, v16_o)):
                _scatter_classes(slab, 0, 8, q_o, d, part, mids.at[part])
                _scatter_classes(slab, 8, 4, v_o, d, part, mids.at[part])

    row = lambda w: pl.BlockSpec((tm, w), lambda i: (i, 0))
    full = lambda shp: pl.BlockSpec(shp, lambda i: (0,) * len(shp))
    nat = lambda w: jax.ShapeDtypeStruct((s, w), BF16)
    cls = lambda d, w: jax.ShapeDtypeStruct((d, s // d, w), BF16)
    return pl.pallas_call(
        body, name="proj", grid=(s // tm,),
        out_shape=(nat(1024), nat(2048), nat(1024), nat(1024), nat(1024),
                   cls(4, 1024), cls(4, 512), cls(16, 1024), cls(16, 512), jax.ShapeDtypeStruct((s, 1024), F32)),
        in_specs=[row(1024), row(1024), full((1, 1024)), _resident((4, 1024, 1024)), full((1, 1024)),
                  full((256, 256))],
        out_specs=(row(1024), row(2048), row(1024), row(1024), row(1024),
                   _class_spec(4, 1024, tm), _class_spec(4, 512, tm),
                   _class_spec(16, 1024, tm), _class_spec(16, 512, tm), row(1024)),
        scratch_shapes=[pltpu.VMEM((nparts, 12, tp, 128), F32), pltpu.VMEM((nparts, 12, tp, 128), F32)],
        compiler_params=_params(("parallel",)),
    )(x, tgt, norm_w, wblk, qkw, b256)


def _block_coords(t, i, nsub, nb, length):
    n = t * nsub + i
    q0 = i * QB
    start = pl.multiple_of(jnp.clip(n * QB - HALF, 0, length - KB), HALF)
    variant = jnp.where(n == 0, 0, jnp.where(n == nb - 1, 2, 1))
    return q0, start, variant


def _split_heads(a, lo):
    zero = jnp.zeros_like(a)
    return jnp.concatenate([jnp.where(lo, a, zero), jnp.where(lo, zero, a)], axis=0)


def _col_pair(ref, q0, lane):
    return jnp.concatenate([ref[pl.ds(q0, QB), lane:lane + 1],
                            ref[pl.ds(q0, QB), HEAD_DIM + lane:HEAD_DIM + lane + 1]], axis=0)


def _attn_fwd(qkn_l, v_l, bias, gi, name):
    r_cls, length, _ = qkn_l.shape
    qt = min(length, 2048)
    nb, nsub = length // QB, qt // QB

    def body(q_ref, k_ref, v_ref, b_ref, o_ref, lse_ref):
        t = pl.program_id(2)
        lo = lax.broadcasted_iota(jnp.int32, (QB, 128), 1) < HEAD_DIM

        starts, logits = [], []
        for i in range(nsub):
            _, start, variant = _block_coords(t, i, nsub, nb, length)
            qq = _split_heads(q_ref[i * QB:(i + 1) * QB, :], lo)
            k = k_ref[pl.ds(start, KB), :]
            logits.append(lax.dot_general(qq, k, (((1,), (1,)), ((), ())), preferred_element_type=F32)
                          + b_ref[variant])
            starts.append(start)
        lg = jnp.concatenate(logits, axis=0)
        m = jnp.max(lg, axis=-1, keepdims=True)
        p = jnp.exp(lg - m)
        pb = p.astype(BF16)
        l = jnp.sum(p, axis=-1, keepdims=True)
        lse = jnp.broadcast_to(m + jnp.log(l), (nsub * 2 * QB, 128))
        inv = 1.0 / l
        for i in range(nsub):
            rows = slice(2 * QB * i, 2 * QB * (i + 1))
            v = v_ref[pl.ds(starts[i], KB), :]
            pv = jnp.dot(pb[rows], v, preferred_element_type=F32) * inv[rows]
            o_ref[i * QB:(i + 1) * QB, :] = jnp.where(lo, pv[0:QB], pv[QB:2 * QB]).astype(BF16)
            ls = lse[rows]
            lse_ref[i * QB:(i + 1) * QB, :] = jnp.where(lo, ls[0:QB], ls[QB:2 * QB])

    return pl.pallas_call(
        body, name=name, grid=(N_PAIR, r_cls, length // qt),
        out_shape=(jax.ShapeDtypeStruct((r_cls, length, 512), BF16),
                   jax.ShapeDtypeStruct((r_cls, length, 512), F32)),
        in_specs=[pl.BlockSpec((None, qt, 128), lambda p, r, t: (r, t, p)),
                  pl.BlockSpec((None, length, 128), lambda p, r, t: (r, 0, 4 + p)),
                  pl.BlockSpec((None, length, 128), lambda p, r, t: (r, 0, p)),
                  pl.BlockSpec((None, 3, None, 2 * QB, KB), lambda p, r, t: (gi, 0, p, 0, 0))],
        out_specs=(pl.BlockSpec((None, qt, 128), lambda p, r, t: (r, t, p)),
                   pl.BlockSpec((None, qt, 128), lambda p, r, t: (r, t, p))),
        compiler_params=_params(("parallel", "parallel", "arbitrary")),
    )(qkn_l, qkn_l, v_l, bias)


def _combine(o_g, lse_g, cg, vz, xmt, wout, cw, cb, b256):
    s = xmt.shape[0]
    tm = TM_COMBINE
    hb = 16
    nt = s // tm

    def body(o1, o4, o16, l1, l4, l16, cg_ref, cgp_ref, cgn_ref, za_ref, xmt_ref, w_ref, cw_ref, cb_ref,
             b_ref, y_o, dout_o, ld1_o, do1_o, dza_o, dgbz_o, dzc_o, loss_o, dcb_o, dcw_o,
             do4_o, ld4_o, do16_o, ld16_o, slab, mid):
        i = pl.program_id(0)

        @pl.when(i == 0)
        def _():
            loss_o[...] = jnp.zeros_like(loss_o)
            dcb_o[...] = jnp.zeros_like(dcb_o)
            dcw_o[...] = jnp.zeros_like(dcw_o)

        u = cg_ref[:, 0:512].astype(F32)
        gb = cg_ref[:, 512:1024].astype(F32)
        gc = cg_ref[:, 1024:1536].astype(F32)
        zc = cg_ref[:, 1536:2048].astype(F32)
        tt = gc * u
        t_prev = cgp_ref[hb - 1:hb, 0:512].astype(F32) * cgp_ref[hb - 1:hb, 1024:1536].astype(F32)
        t_next = cgn_ref[0:1, 0:512].astype(F32) * cgn_ref[0:1, 1024:1536].astype(F32)
        t_prev = jnp.where(i == 0, 0.0, t_prev)
        t_next = jnp.where(i == nt - 1, 0.0, t_next)
        rows = lax.broadcasted_iota(jnp.int32, (tm, 512), 0)
        t_up = jnp.where(rows == 0, t_prev, pltpu.roll(tt, 1, 0))
        t_dn = jnp.where(rows == tm - 1, t_next, pltpu.roll(tt, tm - 1, 0))
        w0, w1, w2 = cw_ref[0:1, :], cw_ref[1:2, :], cw_ref[2:3, :]
        zb = w0 * t_up + w1 * tt + w2 * t_dn + cb_ref[...]
        sg = _sigmoid(zc)
        sz = zc * sg
        y_conv = gb * zb * sz

        a1, p1 = l1[0], o1[0].astype(F32)
        a4 = _gather_classes(slab, lambda r, j: l4[r, :, 128 * j:128 * (j + 1)], 4, 4)
        p4 = _gather_classes(slab, lambda r, j: o4[r, :, 128 * j:128 * (j + 1)], 4, 4)
        a16 = _gather_classes(slab, lambda r, j: l16[r, :, 128 * j:128 * (j + 1)], 4, 16, mid)
        p16 = _gather_classes(slab, lambda r, j: o16[r, :, 128 * j:128 * (j + 1)], 4, 16, mid)
        m = jnp.maximum(jnp.maximum(a1, a4), a16)
        e1, e4, e16 = jnp.exp(a1 - m), jnp.exp(a4 - m), jnp.exp(a16 - m)
        den = e1 + e4 + e16
        lse = m + jnp.log(den)
        o = (e1 * p1 + e4 * p4 + e16 * p16) / den
        za = za_ref[...].astype(F32)
        sga = _sigmoid(za)
        sa = za * sga
        y = jnp.concatenate([y_conv, o * sa], axis=1).astype(BF16)
        y_o[...] = y

        diff = xmt_ref[...] + jnp.dot(y, w_ref[...], preferred_element_type=F32)
        loss_o[...] += (0.5 / D_MODEL) * jnp.sum(diff * diff)
        dout = diff * (1.0 / D_MODEL)
        dout_o[...] = dout
        dy = lax.dot_general(dout.astype(BF16), w_ref[...], (((1,), (1,)), ((), ())), preferred_element_type=F32)
        dyc, dya = dy[:, 0:512], dy[:, 512:1024]

        do = dya * sa
        dza_o[...] = (dya * o * (sga * (1.0 + za * (1.0 - sga)))).astype(BF16)
        lane = lax.broadcasted_iota(jnp.int32, (tm, 512), 1)
        ld = jnp.where((lane & (HEAD_DIM - 1)) < HEAD_DIM // 2, lse, _group_sum(do * o, b_ref))
        do1_o[0] = do.astype(BF16)
        ld1_o[0] = ld
        _to_slabs(slab, do)
        _scatter_classes(slab, 0, 4, do4_o, 4)
        _scatter_classes(slab, 0, 4, do16_o, 16, 0, mid)
        _to_slabs(slab, ld)
        _scatter_classes(slab, 0, 4, ld4_o, 4)
        _scatter_classes(slab, 0, 4, ld16_o, 16, 0, mid)

        dzc = dyc * sz * gb
        dzc_o[...] = dzc.astype(BF16)
        dgbz_o[:, 0:512] = (dyc * sz * zb).astype(BF16)
        dgbz_o[:, 512:1024] = (dyc * gb * zb * (sg * (1.0 + zc * (1.0 - sg)))).astype(BF16)
        dcb_o[...] += jnp.sum(dzc, axis=0, keepdims=True)
        dcw_o[0:1, :] += jnp.sum(dzc * t_up, axis=0, keepdims=True)
        dcw_o[1:2, :] += jnp.sum(dzc * tt, axis=0, keepdims=True)
        dcw_o[2:3, :] += jnp.sum(dzc * t_dn, axis=0, keepdims=True)

    row = lambda w, j=0: pl.BlockSpec((tm, w), lambda i: (i, j))
    full = lambda shp: pl.BlockSpec(shp, lambda i: (0,) * len(shp))
    prev = pl.BlockSpec((hb, 2048), lambda i: (jnp.maximum(i * (tm // hb) - 1, 0), 0))
    nxt = pl.BlockSpec((hb, 2048), lambda i: (jnp.minimum((i + 1) * (tm // hb), s // hb - 1), 0))
    cls = lambda d, dt: jax.ShapeDtypeStruct((d, s // d, 512), dt)
    cspecs = [_class_spec(d, 512, tm) for d in DILATIONS]
    return pl.pallas_call(
        body, name="combine", grid=(nt,),
        out_shape=(jax.ShapeDtypeStruct((s, 1024), BF16), jax.ShapeDtypeStruct((s, 1024), F32),
                   cls(1, F32), cls(1, BF16), jax.ShapeDtypeStruct((s, 512), BF16),
                   jax.ShapeDtypeStruct((s, 1024), BF16), jax.ShapeDtypeStruct((s, 512), BF16),
                   jax.ShapeDtypeStruct((1, 128), F32), jax.ShapeDtypeStruct((1, 512), F32),
                   jax.ShapeDtypeStruct((8, 512), F32),
                   cls(4, BF16), cls(4, F32), cls(16, BF16), cls(16, F32)),
        in_specs=cspecs + cspecs + [row(2048), prev, nxt, row(512, 1), row(1024),
                                    _resident((1024, 1024)), full((8, 512)), full((1, 512)), full((256, 256))],
        out_specs=(row(1024), row(1024), cspecs[0], cspecs[0], row(512), row(1024), row(512),
                   full((1, 128)), full((1, 512)), full((8, 512)),
                   cspecs[1], cspecs[1], cspecs[2], cspecs[2]),
        scratch_shapes=[pltpu.VMEM((4, tm, 128), F32), pltpu.VMEM((4, tm, 128), F32)],
        compiler_params=_params(("arbitrary",)),
    )(*o_g, *lse_g, cg, cg, cg, vz, xmt, wout, cw, cb, b256)


def _attn_bwd(qkn_l, v_l, do_l, ld_l, bias, gi, name):
    r_cls, length, _ = qkn_l.shape
    qt = min(length, 2048 if length <= 4096 else 1024)
    nb, nsub, nt = length // QB, qt // QB, length // qt
    chunk = min(length, 4096)
    nchunk = length // chunk

    def body(q_ref, k_ref, v_ref, do_ref, ld_ref, b_ref, dq_ref, dkv_hbm, dsum_ref, dk_acc, dv_acc, stage, sems):
        p_id, r, t = pl.program_id(0), pl.program_id(1), pl.program_id(2)
        lo = lax.broadcasted_iota(jnp.int32, (QB, 128), 1) < HEAD_DIM

        @pl.when(t == 0)
        def _():
            dk_acc[...] = jnp.zeros_like(dk_acc)
            dv_acc[...] = jnp.zeros_like(dv_acc)

        @pl.when((t == 0) & (r == 0))
        def _():
            dsum_ref[...] = jnp.zeros_like(dsum_ref)

        nt_dims = (((1,), (1,)), ((), ()))
        tn_dims = (((0,), (0,)), ((), ()))
        coords, qqs, dds, logits, dps, lcols, dcols = [], [], [], [], [], [], []
        for i in range(nsub):
            q0, start, variant = _block_coords(t, i, nsub, nb, length)
            qq = _split_heads(q_ref[q0:q0 + QB, :], lo)
            dd = _split_heads(do_ref[q0:q0 + QB, :], lo)
            k = k_ref[pl.ds(start, KB), :]
            v = v_ref[pl.ds(start, KB), :]
            logits.append(lax.dot_general(qq, k, nt_dims, preferred_element_type=F32) + b_ref[variant])
            dps.append(lax.dot_general(dd, v, nt_dims, preferred_element_type=F32))
            lcols.append(_col_pair(ld_ref, q0, 0))
            dcols.append(_col_pair(ld_ref, q0, HEAD_DIM // 2))
            coords.append((q0, start, variant))
            qqs.append(qq)
            dds.append(dd)
        p = jnp.exp(jnp.concatenate(logits, axis=0) - jnp.concatenate(lcols, axis=0))
        ds = p * (jnp.concatenate(dps, axis=0) - jnp.concatenate(dcols, axis=0))
        pb = p.astype(BF16)
        dsb = ds.astype(BF16)
        middle = None
        for i in range(nsub):
            q0, start, variant = coords[i]
            rows = slice(2 * QB * i, 2 * QB * (i + 1))
            if 0 < i < nsub - 1:
                middle = ds[rows] if middle is None else middle + ds[rows]
            else:
                dsum_ref[variant] += ds[rows]
            dqq = jnp.dot(dsb[rows], k_ref[pl.ds(start, KB), :], preferred_element_type=F32)
            dq_ref[q0:q0 + QB, :] = jnp.where(lo, dqq[0:QB], dqq[QB:2 * QB]).astype(BF16)
            dk_acc[pl.ds(start, KB), :] += lax.dot_general(dsb[rows], qqs[i], tn_dims, preferred_element_type=F32)
            dv_acc[pl.ds(start, KB), :] += lax.dot_general(pb[rows], dds[i], tn_dims, preferred_element_type=F32)
        if middle is not None:
            dsum_ref[1] += middle

        @pl.when(t == nt - 1)
        def _():
            def copy(k):
                which, c = k // nchunk, k % nchunk
                rows = pl.ds(c * chunk, chunk)
                return pltpu.make_async_copy(stage.at[k % 2], dkv_hbm.at[r, p_id, which, rows, :], sems.at[k % 2])

            for k in range(2 * nchunk):
                if k < 2:
                    @pl.when((p_id > 0) | (r > 0))
                    def _():
                        copy(k).wait()
                else:
                    copy(k).wait()
                acc = (dk_acc, dv_acc)[k // nchunk]
                stage[k % 2] = acc[pl.ds((k % nchunk) * chunk, chunk), :].astype(BF16)
                copy(k).start()

            @pl.when((p_id == N_PAIR - 1) & (r == r_cls - 1))
            def _():
                copy(0).wait()
                copy(1).wait()

    qspec = pl.BlockSpec((None, qt, 128), lambda p, r, t: (r, t, p))
    return pl.pallas_call(
        body, name=name, grid=(N_PAIR, r_cls, nt),
        out_shape=(jax.ShapeDtypeStruct((r_cls, length, 512), BF16),
                   jax.ShapeDtypeStruct((r_cls, N_PAIR, 2, length, 128), BF16),
                   jax.ShapeDtypeStruct((N_PAIR, 3, 2 * QB, KB), F32)),
        in_specs=[qspec,
                  pl.BlockSpec((None, length, 128), lambda p, r, t: (r, 0, 4 + p)),
                  pl.BlockSpec((None, length, 128), lambda p, r, t: (r, 0, p)),
                  qspec, qspec,
                  pl.BlockSpec((None, 3, None, 2 * QB, KB), lambda p, r, t: (gi, 0, p, 0, 0))],
        out_specs=(qspec, pl.BlockSpec(memory_space=pl.ANY),
                   pl.BlockSpec((None, 3, 2 * QB, KB), lambda p, r, t: (p, 0, 0, 0))),
        scratch_shapes=[pltpu.VMEM((length, 128), F32), pltpu.VMEM((length, 128), F32),
                        pltpu.VMEM((2, chunk, 128), BF16), pltpu.SemaphoreType.DMA((2,))],
        compiler_params=_params(("arbitrary", "arbitrary", "arbitrary")),
    )(qkn_l, qkn_l, v_l, do_l, ld_l, bias)


def _bwd_tail(dq_g, dkv_g, qkr, qkw, dza, dgbz, dzc, cg, cw, wblk, x, norm_w, dout, b256):
    s = x.shape[0]
    tm = TM_COMBINE
    hb = 16
    nt = s // tm

    def body(dq1, dq4, dq16, dkv1, dkv4, dkv16, qkr_ref, qkw_ref, dza_ref, dgbz_ref, dzc_ref,
             dzp_ref, dzn_ref, u_ref, gc_ref, cw_ref, w_ref, x_ref, nw_ref, dout_ref, b_ref,
             gx_o, dproj_o, dnw_o, dqkw_o, slab, mid):
        i = pl.program_id(0)

        def nat_q(ref, d):
            return _gather_classes(slab, lambda r, j: ref[r, :, 128 * j:128 * (j + 1)], 4, d, mid)

        def nat_kv(ref, d, which):
            return _gather_classes(slab, lambda r, j: ref[r, j, which], 4, d, mid)

        @pl.when(i == 0)
        def _():
            dnw_o[...] = jnp.zeros_like(dnw_o)
            dqkw_o[...] = jnp.zeros_like(dqkw_o)

        dzc = dzc_ref[...].astype(F32)
        d_prev = jnp.where(i == 0, 0.0, dzp_ref[hb - 1:hb, :].astype(F32))
        d_next = jnp.where(i == nt - 1, 0.0, dzn_ref[0:1, :].astype(F32))
        rows = lax.broadcasted_iota(jnp.int32, (tm, 512), 0)
        d_up = jnp.where(rows == 0, d_prev, pltpu.roll(dzc, 1, 0))
        d_dn = jnp.where(rows == tm - 1, d_next, pltpu.roll(dzc, tm - 1, 0))
        dt = cw_ref[0:1, :] * d_dn + cw_ref[1:2, :] * dzc + cw_ref[2:3, :] * d_up
        u = u_ref[...].astype(F32)
        gc = gc_ref[...].astype(F32)
        dproj_o[:, 0:512] = (dt * gc).astype(BF16)
        dproj_o[:, 512:1024] = dgbz_ref[:, 0:512]
        dproj_o[:, 1024:1536] = (dt * u).astype(BF16)
        dproj_o[:, 1536:2048] = dgbz_ref[:, 512:1024]

        dqn = (dq1[0].astype(F32) + nat_q(dq4, 4) + nat_q(dq16, 16)) * (1.0 / 8.0)
        dk1 = jnp.concatenate([dkv1[0, j, 0] for j in range(N_PAIR)], axis=1)
        dv1 = jnp.concatenate([dkv1[0, j, 1] for j in range(N_PAIR)], axis=1)
        dkn = dk1 + nat_kv(dkv4, 4, 0) + nat_kv(dkv16, 16, 0)
        dvn = dv1 + nat_kv(dkv4, 4, 1) + nat_kv(dkv16, 16, 1)
        g = jnp.concatenate([dqn, dkn], axis=1) * qkw_ref[...]
        raw = qkr_ref[...].astype(F32)
        rr = lax.rsqrt(_group_sum(raw * raw, b_ref, split=False) * (1.0 / HEAD_DIM) + EPS)
        proj_gq = _group_sum(g * raw, b_ref) * (1.0 / HEAD_DIM)
        draw = rr * g - raw * (rr * rr * rr) * proj_gq
        dqkw_o[...] += jnp.sum(jnp.concatenate([dqn, dkn], axis=1) * raw * rr, axis=0, keepdims=True)
        dproj_o[:, 2048:3072] = draw.astype(BF16)
        dproj_o[:, 3072:3584] = dvn.astype(BF16)
        dproj_o[:, 3584:4096] = dza_ref[...]

        nt_dims = (((1,), (1,)), ((), ()))
        dh = lax.dot_general(dproj_o[:, 0:1024], w_ref[0], nt_dims, preferred_element_type=F32)
        for b in range(1, 4):
            dh += lax.dot_general(dproj_o[:, 1024 * b:1024 * b + 1024], w_ref[b], nt_dims,
                                  preferred_element_type=F32)

        xf = x_ref[...]
        r = lax.rsqrt(jnp.mean(xf * xf, axis=-1, keepdims=True) + EPS)
        gh = dh * nw_ref[...]
        dnw_o[...] += jnp.sum(dh * xf * r, axis=0, keepdims=True)
        mean_gx = jnp.mean(gh * xf, axis=-1, keepdims=True)
        gx_o[...] = dout_ref[...] + r * gh - xf * (r * r * r) * mean_gx

    row = lambda w, j=0: pl.BlockSpec((tm, w), lambda i: (i, j))
    full = lambda shp: pl.BlockSpec(shp, lambda i: (0,) * len(shp))
    prev = pl.BlockSpec((hb, 512), lambda i: (jnp.maximum(i * (tm // hb) - 1, 0), 0))
    nxt = pl.BlockSpec((hb, 512), lambda i: (jnp.minimum((i + 1) * (tm // hb), s // hb - 1), 0))
    return pl.pallas_call(
        body, name="bwd_tail", grid=(nt,),
        out_shape=(jax.ShapeDtypeStruct((s, 1024), F32), jax.ShapeDtypeStruct((s, 4096), BF16),
                   jax.ShapeDtypeStruct((1, 1024), F32), jax.ShapeDtypeStruct((1, 1024), F32)),
        in_specs=[_class_spec(d, 512, tm) for d in DILATIONS]
        + [pl.BlockSpec((d, N_PAIR, 2, tm // d, 128), lambda i: (0, 0, 0, i, 0)) for d in DILATIONS]
        + [row(1024), full((1, 1024)), row(512), row(1024), row(512), prev, nxt,
           row(512, 0), row(512, 2), full((8, 512)), _resident((4, 1024, 1024)), row(1024),
           full((1, 1024)), row(1024), full((256, 256))],
        out_specs=(row(1024), row(4096), full((1, 1024)), full((1, 1024))),
        scratch_shapes=[pltpu.VMEM((4, tm, 128), F32), pltpu.VMEM((4, tm, 128), F32)],
        compiler_params=_params(("arbitrary",)),
    )(*dq_g, *dkv_g, qkr, qkw, dza, dgbz, dzc, dzc, dzc, cg, cg, cw, wblk, x, norm_w, dout, b256)


def _wgrad(a, b, row_blocked, name):
    s, m = a.shape
    n = b.shape[1]
    tk = 1024
    ncol = min(n, 2048)
    nj, nk = n // ncol, s // tk

    def body(a_ref, b_ref, o_ref, acc):
        kk = pl.program_id(1)

        @pl.when(kk == 0)
        def _():
            acc[...] = jnp.zeros_like(acc)

        acc[...] += lax.dot_general(a_ref[...], b_ref[...].astype(BF16), (((0,), (0,)), ((), ())),
                                    preferred_element_type=F32)

        @pl.when(kk == nk - 1)
        def _():
            blocks, _, rows, _ = o_ref.shape
            for blk in range(blocks):
                for half in range(2):
                    if row_blocked:
                        r0 = (2 * blk + half) * rows
                        o_ref[blk, half] = acc[r0:r0 + rows, :].astype(BF16)
                    else:
                        o_ref[blk, half] = acc[half * rows:(half + 1) * rows,
                                               1024 * blk:1024 * (blk + 1)].astype(BF16)

    if row_blocked:
        out_shape = jax.ShapeDtypeStruct((4, 2, m // 8, 1024), BF16)
        out_spec = pl.BlockSpec((4, 2, m // 8, 1024), lambda j, k: (0, 0, 0, 0))
    else:
        out_shape = jax.ShapeDtypeStruct((n // 1024, 2, m // 2, 1024), BF16)
        out_spec = pl.BlockSpec((ncol // 1024, 2, m // 2, 1024), lambda j, k: (j, 0, 0, 0))
    return pl.pallas_call(
        body, name=name, grid=(nj, nk),
        out_shape=out_shape,
        in_specs=[pl.BlockSpec((tk, m), lambda j, k: (k, 0)), pl.BlockSpec((tk, ncol), lambda j, k: (k, j))],
        out_specs=out_spec,
        scratch_shapes=[pltpu.VMEM((m, ncol), F32)],
        compiler_params=_params(("parallel", "arbitrary")),
    )(a, b)


def _dbias(dsums, onehot_all):
    def body(ds1_ref, ds4_ref, ds16_ref, oh_ref, o_ref):
        @pl.when(pl.program_id(0) == 0)
        def _():
            o_ref[...] = jnp.zeros_like(o_ref)

        hrow = lax.broadcasted_iota(jnp.int32, (8, KB), 0)
        flip = (lax.broadcasted_iota(jnp.int32, (QB, QB), 0)
                + lax.broadcasted_iota(jnp.int32, (QB, QB), 1) == QB - 1).astype(F32)

        def diagonal_sums(tile):
            rev = jnp.dot(flip, tile, preferred_element_type=F32, precision=lax.Precision.HIGHEST)
            sums = jnp.sum(pltpu.roll(rev, 0, 1, stride=1, stride_axis=0), axis=0, keepdims=True)
            return pltpu.roll(sums, KB - (QB - 1), 1)

        for g, ds_ref in enumerate((ds1_ref, ds4_ref, ds16_ref)):
            diag = jnp.zeros((8, KB), F32)
            for p in range(N_PAIR):
                diag = jnp.where(hrow == 2 * p, diagonal_sums(ds_ref[p, 0:QB, :]), diag)
                diag = jnp.where(hrow == 2 * p + 1, diagonal_sums(ds_ref[p, QB:2 * QB, :]), diag)
            o_ref[...] += jnp.dot(diag, oh_ref[g], preferred_element_type=F32, precision=lax.Precision.HIGHEST)

    ds_spec = pl.BlockSpec((N_PAIR, None, 2 * QB, KB), lambda v: (0, v, 0, 0))
    return pl.pallas_call(
        body, name="dbias", grid=(3,),
        out_shape=jax.ShapeDtypeStruct((8, 128), F32),
        in_specs=[ds_spec, ds_spec, ds_spec, pl.BlockSpec((3, None, KB, 128), lambda v: (0, v, 0, 0))],
        out_specs=pl.BlockSpec((8, 128), lambda v: (0, 0)),
        compiler_params=_params(("arbitrary",)),
    )(*dsums, onehot_all)


def _gsync(pw_in, pw_out, small):
    hin, hout = pw_in.shape[2], pw_out.shape[2]
    nsmall = small.shape[0]

    def body(pin_hbm, pout_hbm, small_ref, gin_o, gout_o, small_o,
             mine_in, recv_in, s1_in, r1_in, s2_in, r2_in, mine_out, recv_out, s1_out, r1_out, s2_out, r2_out, gather,
             lsem, asend, arecv, bsend, brecv, csend, crecv, ssend, srecv):
        x, y, c = lax.axis_index("x"), lax.axis_index("y"), lax.axis_index("c")
        b = 2 * x + y
        dev = 4 * x + 2 * y + c
        sib = (x, y, 1 - c)
        xnbr, ynbr = (1 - x, y, c), (x, 1 - y, c)
        bx, by, bd = b ^ 2, b ^ 1, b ^ 3

        def rcopy(src, dst, ssem, rsem, to):
            return pltpu.make_async_remote_copy(src_ref=src, dst_ref=dst, send_sem=ssem, recv_sem=rsem,
                                                device_id=to, device_id_type=MESH)

        gather[dev] = small_ref[...]
        s_sends = []
        for k in range(1, 8):
            to = (x ^ (k >> 2), y ^ ((k >> 1) & 1), c ^ (k & 1))
            cp = rcopy(gather.at[dev], gather.at[dev], ssend.at[k - 1], srecv.at[k - 1], to)
            cp.start()
            s_sends.append(cp)

        a_in = rcopy(pin_hbm.at[:, 1 - c], recv_in, asend.at[0], arecv.at[0], sib)
        a_out = rcopy(pout_hbm.at[:, 1 - c], recv_out, asend.at[1], arecv.at[1], sib)
        a_in.start()
        a_out.start()
        l_in = pltpu.make_async_copy(pin_hbm.at[:, c], mine_in, lsem.at[0])
        l_out = pltpu.make_async_copy(pout_hbm.at[:, c], mine_out, lsem.at[1])
        l_in.start()
        l_out.start()
        l_in.wait()
        l_out.wait()

        def phase_one(a_cp, mine, recv, s1, r1, half, base):
            a_cp.wait_recv()
            q = half // 2
            sends = []
            for slot in (1, 0):
                for part, (peer, blk_peer) in enumerate(((xnbr, bx), (ynbr, by))):
                    rows = pl.ds(part * q, q)
                    blk = (blk_peer, bd)[slot]
                    s1[part, slot] = (mine[blk, rows, :].astype(F32) + recv[blk, rows, :].astype(F32)).astype(BF16)
                    sem = base + 2 * slot + part
                    cp = rcopy(s1.at[part, slot], r1.at[part, slot], bsend.at[sem], brecv.at[sem], peer)
                    cp.start()
                    sends.append(cp)
            return sends[0:2], sends[2:4]

        def phase_two(p1, mine, recv, r1, s2, r2, half, base):
            q = half // 2
            passed, kept = p1
            own, sends = [], []
            for part, (peer, blk_next) in enumerate(((ynbr, by), (xnbr, bx))):
                rows = pl.ds(part * q, q)
                passed[part].wait_recv()
                s2[part] = (mine[blk_next, rows, :].astype(F32) + recv[blk_next, rows, :].astype(F32)
                            + r1[part, 1].astype(F32)).astype(BF16)
                cp = rcopy(s2.at[part], r2.at[part], bsend.at[base + 4 + part], brecv.at[base + 4 + part], peer)
                cp.start()
                sends.append(cp)
            for part in range(2):
                rows = pl.ds(part * q, q)
                kept[part].wait_recv()
                own.append(mine[b, rows, :].astype(F32) + recv[b, rows, :].astype(F32) + r1[part, 0].astype(F32))
            return own, sends

        def stage_c(own, p2, r2, g_o, half, idx):
            q = half // 2
            for part in range(2):
                p2[part].wait_recv()
                g_o[pl.ds(pl.multiple_of(c * half + part * q, q), q), :] = own[part] + r2[part].astype(F32)
            rows = g_o.at[pl.ds(pl.multiple_of(c * half, half), half), :]
            cp = rcopy(rows, rows, csend.at[idx], crecv.at[idx], sib)
            cp.start()
            return cp

        p1_in = phase_one(a_in, mine_in, recv_in, s1_in, r1_in, hin, 0)
        p1_out = phase_one(a_out, mine_out, recv_out, s1_out, r1_out, hout, 6)
        own_in, p2_in = phase_two(p1_in, mine_in, recv_in, r1_in, s2_in, r2_in, hin, 0)
        own_out, p2_out = phase_two(p1_out, mine_out, recv_out, r1_out, s2_out, r2_out, hout, 6)
        c_in = stage_c(own_in, p2_in, r2_in, gin_o, hin, 0)
        c_out = stage_c(own_out, p2_out, r2_out, gout_o, hout, 1)
        b_in, b_out = p1_in[0] + p1_in[1] + p2_in, p1_out[0] + p1_out[1] + p2_out

        for cp in s_sends:
            cp.wait_recv()
        tot = gather[0]
        for d in range(1, 8):
            tot = tot + gather[d]
        small_o[...] = tot

        for g_o, half, idx in ((gin_o, hin, 0), (gout_o, hout, 1)):
            other = g_o.at[pl.ds(pl.multiple_of((1 - c) * half, half), half), :]
            rcopy(other, other, csend.at[idx], crecv.at[idx], sib).wait_recv()
        for cp in s_sends + [a_in, a_out] + b_in + b_out + [c_in, c_out]:
            cp.wait_send()

    vm = pl.BlockSpec(memory_space=pltpu.VMEM)
    hbm = pl.BlockSpec(memory_space=pl.ANY)
    return pl.pallas_call(
        body, name="gsync",
        out_shape=(jax.ShapeDtypeStruct((2 * hin, 1024), F32), jax.ShapeDtypeStruct((2 * hout, 1024), F32),
                   jax.ShapeDtypeStruct((nsmall, 128), F32)),
        in_specs=[hbm, hbm, vm], out_specs=(vm, vm, vm),
        scratch_shapes=[pltpu.VMEM((4, hin, 1024), BF16), pltpu.VMEM((4, hin, 1024), BF16),
                        pltpu.VMEM((2, 2, hin // 2, 1024), BF16), pltpu.VMEM((2, 2, hin // 2, 1024), BF16),
                        pltpu.VMEM((2, hin // 2, 1024), BF16), pltpu.VMEM((2, hin // 2, 1024), BF16),
                        pltpu.VMEM((4, hout, 1024), BF16), pltpu.VMEM((4, hout, 1024), BF16),
                        pltpu.VMEM((2, 2, hout // 2, 1024), BF16), pltpu.VMEM((2, 2, hout // 2, 1024), BF16),
                        pltpu.VMEM((2, hout // 2, 1024), BF16), pltpu.VMEM((2, hout // 2, 1024), BF16),
                        pltpu.VMEM((8, nsmall, 128), F32),
                        pltpu.SemaphoreType.DMA((2,)),
                        pltpu.SemaphoreType.DMA((2,)), pltpu.SemaphoreType.DMA((2,)),
                        pltpu.SemaphoreType.DMA((12,)), pltpu.SemaphoreType.DMA((12,)),
                        pltpu.SemaphoreType.DMA((2,)), pltpu.SemaphoreType.DMA((2,)),
                        pltpu.SemaphoreType.DMA((7,)), pltpu.SemaphoreType.DMA((7,))],
        compiler_params=_params(),
    )(pw_in, pw_out, small)


def _adamw_math(w, g, m, v):
    m = ADAM_B1 * m + (1.0 - ADAM_B1) * g
    v = ADAM_B2 * v + (1.0 - ADAM_B2) * (g * g)
    m_hat = m / (1.0 - ADAM_B1 ** ADAM_STEP)
    v_hat = v / (1.0 - ADAM_B2 ** ADAM_STEP)
    delta = -ADAM_LR * (m_hat / (jnp.sqrt(v_hat) + ADAM_EPS) + ADAM_WD * w)
    return delta, m, v


def _adamw(w, g, m, v, name):
    rows, cols = w.shape
    tr = 256 if rows % 256 == 0 else rows

    def body(w_ref, g_ref, m_ref, v_ref, g_o, d_o, m_o, v_o):
        g = g_ref[...]
        d, m2, v2 = _adamw_math(w_ref[...], g, m_ref[...], v_ref[...])
        g_o[...] = g
        d_o[...] = d
        m_o[...] = m2
        v_o[...] = v2

    spec = pl.BlockSpec((tr, cols), lambda i: (i, 0))
    shp = jax.ShapeDtypeStruct((rows, cols), F32)
    return pl.pallas_call(
        body, name=name, grid=(rows // tr,), out_shape=(shp, shp, shp, shp),
        in_specs=[spec] * 4, out_specs=(spec, spec, spec, spec),
        compiler_params=_params(("parallel",)),
    )(w, g, m, v)


def _fold_heads(dqkw):
    def body(x_ref, o_ref):
        xs = x_ref[...]
        sq = xs[0:1] + xs[1:2] + xs[2:3] + xs[3:4]
        sk = xs[4:5] + xs[5:6] + xs[6:7] + xs[7:8]
        both = jnp.concatenate([sq, sk], axis=0)
        o_ref[...] = both + pltpu.roll(both, HEAD_DIM, 1)

    vm = pl.BlockSpec(memory_space=pltpu.VMEM)
    return pl.pallas_call(body, name="fold_heads", out_shape=jax.ShapeDtypeStruct((2, 128), F32),
                          in_specs=[vm], out_specs=vm, compiler_params=_params())(dqkw)


def kernel(x, norm_w, w_in, conv_w, conv_b, q_norm_w, k_norm_w, rel_bias, w_out, loss_target, m_norm_w, m_w_in, m_conv_w, m_conv_b, m_q_norm_w, m_k_norm_w, m_rel_bias, m_w_out, v_norm_w, v_w_in, v_conv_w, v_conv_b, v_q_norm_w, v_k_norm_w, v_rel_bias, v_w_out):
    x2 = x[0]
    tgt = loss_target[0]
    blk = 2 * lax.axis_index("x") + lax.axis_index("y")

    conv_w8 = jnp.pad(conv_w, ((0, 5), (0, 0)))
    wblk, woutblk, cwblk = _wgather(w_in, w_out, conv_w8)
    wout_full = woutblk.reshape(1024, 1024)
    cw_full = cwblk.transpose(1, 0, 2).reshape(8, 512)

    qkw = jnp.concatenate([jnp.tile(q_norm_w, 8) * 0.125, jnp.tile(k_norm_w, 8)])[None, :]
    qkw_raw = jnp.concatenate([jnp.tile(q_norm_w, 8), jnp.tile(k_norm_w, 8)])[None, :]
    gidx = jnp.arange(256) // HEAD_DIM
    b256 = (gidx[:, None] == gidx[None, :]).astype(BF16)

    h, cg, qkr, qkn, vz, qkn4, v4, qkn16, v16, xmt = _proj(x2, tgt, norm_w[None, :], wblk, qkw, b256)

    biases = _bias_tables(rel_bias)
    qkn_l = [qkn[None], qkn4, qkn16]
    v_l = [vz[None], v4, v16]
    o_g, lse_g = [], []
    for gi, d in enumerate(DILATIONS):
        o_l, lse_l = _attn_fwd(qkn_l[gi], v_l[gi], biases, gi, f"attn_fwd_d{d}")
        o_g.append(o_l)
        lse_g.append(lse_l)

    (y, dout, ld1, do1, dza, dgbz, dzc, loss_p, dcb, dcw, do4, ld4, do16, ld16) = _combine(
        o_g, lse_g, cg, vz, xmt, wout_full, cw_full, conv_b[None, :], b256)

    dq_g, dkv_g, dsums = [], [], []
    for gi, (d, do_l, ld_l) in enumerate(zip(DILATIONS, (do1, do4, do16), (ld1, ld4, ld16))):
        dq_l, dkv_l, dsum = _attn_bwd(qkn_l[gi], v_l[gi], do_l, ld_l, biases, gi, f"attn_bwd_d{d}")
        dq_g.append(dq_l)
        dkv_g.append(dkv_l)
        dsums.append(dsum)

    grad_x, dproj, dnw, dqkw = _bwd_tail(dq_g, dkv_g, qkr, qkw_raw, dza, dgbz, dzc, cg, cw_full, wblk,
                                         x2, norm_w[None, :], dout, b256)

    pw_in = _wgrad(h, dproj, False, "wgrad_in")
    pw_out = _wgrad(y, dout, True, "wgrad_out")
    dbias8 = _dbias(dsums, jnp.stack([_diag_bucket_onehot(d) for d in DILATIONS], axis=0))

    small = jnp.concatenate([dnw.reshape(8, 128), dcb.reshape(4, 128), dqkw.reshape(8, 128),
                             dcw[0:3].reshape(12, 128), dbias8, jnp.pad(loss_p, ((0, 7), (0, 0)))], axis=0)
    g_win, g_wout, gsmall = _gsync(pw_in, pw_out, small)

    g_nw = gsmall[0:8].reshape(1024)
    g_cb = gsmall[8:12].reshape(512)
    folded = _fold_heads(gsmall[12:20])
    g_qw, g_kw = folded[0, 0:64], folded[1, 0:64]
    g_cw = lax.dynamic_slice(gsmall[20:32].reshape(3, 512), (0, blk * 128), (3, 128))
    g_rb = gsmall[32:40][:, 0:32].T
    loss = gsmall[40, 0]

    g_win, d_win, nm_win, nv_win = _adamw(w_in, g_win, m_w_in, v_w_in, "adamw_w_in")
    g_wout, d_wout, nm_wout, nv_wout = _adamw(w_out, g_wout, m_w_out, v_w_out, "adamw_w_out")

    def pack(parts):
        rows = [parts[0].reshape(8, 128), parts[1].reshape(4, 128),
                jnp.pad(parts[2], (0, 64))[None, :], jnp.pad(parts[3], (0, 64))[None, :],
                parts[4], jnp.pad(parts[5].T, ((0, 0), (0, 96)))]
        return jnp.concatenate(rows, axis=0)

    ws = pack([norm_w, conv_b, q_norm_w, k_norm_w, conv_w, rel_bias])
    gs = pack([g_nw, g_cb, g_qw, g_kw, g_cw, g_rb])
    ms = pack([m_norm_w, m_conv_b, m_q_norm_w, m_k_norm_w, m_conv_w, m_rel_bias])
    vs = pack([v_norm_w, v_conv_b, v_q_norm_w, v_k_norm_w, v_conv_w, v_rel_bias])
    rpad = lambda a: jnp.pad(a, ((0, 7), (0, 0)))
    _, d_s, nm_s, nv_s = _adamw(rpad(ws), rpad(gs), rpad(ms), rpad(vs), "adamw_small")

    def unpack(a):
        return (a[0:8].reshape(1024), a[12:13, 0:64].reshape(64), a[13:14, 0:64].reshape(64),
                a[14:17], a[8:12].reshape(512), a[17:25, 0:32].T)

    def ordered(nw, win, cw, cb, qw, kw, rb, wout):
        return (nw, win, cw, cb, qw, kw, rb, wout)

    g_un = (g_nw, g_qw, g_kw, g_cw, g_cb, g_rb)
    outs = [loss, grad_x[None]]
    for un, win_v, wout_v in ((g_un, g_win, g_wout), (unpack(d_s), d_win, d_wout),
                              (unpack(nm_s), nm_win, nm_wout), (unpack(nv_s), nv_win, nv_wout)):
        nw, qw, kw, cw, cb, rb = un
        outs.extend(ordered(nw, win_v, cw, cb, qw, kw, rb, wout_v))
    return tuple(outs)
```

```python
import math

import jax
import jax.numpy as jnp
from jax import lax
from jax.experimental import pallas as pl
from jax.experimental.pallas import tpu as pltpu

F32 = jnp.float32
BF16 = jnp.bfloat16
MESH = pl.DeviceIdType.MESH

D_MODEL = 1024
CONV_W = 512
ATTN_W = 512
HEAD_DIM = 64
N_PAIR = 4
DILATIONS = (1, 4, 16)
HALF = 64
QB = 128
KB = QB + 2 * HALF
NUM_BUCKETS = 32
MAX_DISTANCE = 1024
EPS = 1e-6
NEG = -1e30
ADAM_LR, ADAM_B1, ADAM_B2, ADAM_EPS, ADAM_WD, ADAM_STEP = 0.001, 0.9, 0.999, 1e-08, 0.01, 10
VMEM_LIMIT = 48 << 20


def _params(sem=None, vmem=VMEM_LIMIT, **kw):
    if sem is not None:
        kw["dimension_semantics"] = sem
    return pltpu.CompilerParams(vmem_limit_bytes=vmem, **kw)


def _sigmoid(z):
    return 1.0 / (1.0 + jnp.exp(-z))


def _group_sum(val, b_ref, split=True):
    hi = val.astype(BF16)
    lo = (val - hi.astype(F32)).astype(BF16) if split else None
    outs = []
    for j in range(val.shape[1] // 256):
        sl = slice(256 * j, 256 * j + 256)
        part = jnp.dot(hi[:, sl], b_ref[...], preferred_element_type=F32)
        if split:
            part = part + jnp.dot(lo[:, sl], b_ref[...], preferred_element_type=F32)
        outs.append(part)
    return outs[0] if len(outs) == 1 else jnp.concatenate(outs, axis=1)


def _t5_bucket(rel):
    half_b = NUM_BUCKETS // 2
    max_exact = half_b // 2
    ret = jnp.where(rel > 0, half_b, 0)
    n = jnp.abs(rel)
    nf = jnp.maximum(n, 1).astype(F32)
    large = max_exact + (jnp.log(nf / max_exact) / math.log(MAX_DISTANCE / max_exact)
                         * (half_b - max_exact)).astype(jnp.int32)
    large = jnp.minimum(large, half_b - 1)
    return ret + jnp.where(n < max_exact, n, large)


def _bias_tables(rel_bias):
    rows = []
    key = jnp.arange(KB)
    for dilation in DILATIONS:
        for variant in range(3):
            off = (0, HALF, 2 * HALF)[variant]
            rel = ((key - off + KB // 2) % KB) - KB // 2
            bkt = _t5_bucket(jnp.clip(rel, -HALF, HALF) * dilation)
            rows.append(jnp.where(jnp.abs(rel) <= HALF, bkt, -1))
    bkt_all = jnp.broadcast_to(jnp.stack(rows, axis=0).astype(jnp.int32)[:, None, :], (9, 8, KB))

    def body(rb_ref, bkt_ref, o_ref):
        bkt = bkt_ref[...]
        off = (pl.program_id(0) % 3) * HALF
        rel = (lax.broadcasted_iota(jnp.int32, (QB, KB), 1) - lax.broadcasted_iota(jnp.int32, (QB, KB), 0)) - off
        band = jnp.abs(rel) <= HALF
        for h in range(8):
            acc = jnp.full((8, KB), NEG, F32)
            for b in range(NUM_BUCKETS):
                acc = jnp.where(bkt == b, rb_ref[b, h], acc)
            rolled = pltpu.roll(jnp.broadcast_to(acc[0:1], (QB, KB)), 0, 1, stride=1, stride_axis=0)
            o_ref[h] = jnp.where(band, rolled, NEG)

    out = pl.pallas_call(
        body, name="bias_tables", grid=(9,),
        out_shape=jax.ShapeDtypeStruct((9, 8, QB, KB), F32),
        in_specs=[pl.BlockSpec(memory_space=pltpu.SMEM), pl.BlockSpec((None, 8, KB), lambda i: (i, 0, 0))],
        out_specs=pl.BlockSpec((None, 8, QB, KB), lambda i: (i, 0, 0, 0)),
        compiler_params=_params(("parallel",)),
    )(rel_bias, bkt_all)
    return out.reshape(3, 3, N_PAIR, 2 * QB, KB)


def _diag_bucket_onehot(dilation):
    out = []
    c = jnp.arange(KB)
    for variant in range(3):
        off = (0, HALF, 2 * HALF)[variant]
        rel = ((c - off + 128) % 256) - 128
        band = jnp.abs(rel) <= HALF
        bkt = _t5_bucket(jnp.clip(rel, -HALF, HALF) * dilation)
        oh = (bkt[:, None] == jnp.arange(128)[None, :]) & band[:, None]
        out.append(oh.astype(F32))
    return jnp.stack(out, axis=0)


def _wgather(w_in, w_out, conv_w):
    rin, rout = w_in.shape[0] // 2, w_out.shape[0] // 2

    def body(win_ref, wout_ref, cw_ref, win_o, wout_o, cw_o, send_sems, recv_sems):
        x, y, c = lax.axis_index("x"), lax.axis_index("y"), lax.axis_index("c")
        b = 2 * x + y
        win_o[b] = win_ref[...].astype(BF16)
        wout_o[b] = wout_ref[...].astype(BF16)
        cw_o[b] = cw_ref[...]
        xnbr, ynbr, diag, sib = (1 - x, y, c), (x, 1 - y, c), (1 - x, 1 - y, c), (x, y, 1 - c)
        bx, by, bd = b ^ 2, b ^ 1, b ^ 3

        def copy(sem, ref, to):
            return pltpu.make_async_remote_copy(src_ref=ref, dst_ref=ref, send_sem=send_sems.at[sem],
                                                recv_sem=recv_sems.at[sem], device_id=to, device_id_type=MESH)

        def rows_of(ref, half):
            def rows(blk, quarter=None):
                if quarter is None:
                    return ref.at[blk, pl.ds(c * half, half), :]
                return ref.at[blk, pl.ds(c * half + quarter * (half // 2), half // 2), :]
            return rows

        def send_own(ref, half, base):
            rows = rows_of(ref, half)
            own_x, own_y = copy(base + 0, rows(b), xnbr), copy(base + 1, rows(b), ynbr)
            own_x.start()
            own_y.start()
            return [own_x, own_y]

        def relay(ref, half, base):
            rows = rows_of(ref, half)
            copy(base + 0, rows(bx), xnbr).wait_recv()
            pass_y = copy(base + 2, rows(bx, 0), ynbr)
            pass_y.start()
            to_sib = [copy(base + 4, rows(bx), sib)]
            to_sib[-1].start()
            copy(base + 1, rows(by), ynbr).wait_recv()
            pass_x = copy(base + 3, rows(by, 1), xnbr)
            pass_x.start()
            to_sib.append(copy(base + 5, rows(by), sib))
            to_sib[-1].start()
            copy(base + 2, rows(bd, 0), ynbr).wait_recv()
            copy(base + 3, rows(bd, 1), xnbr).wait_recv()
            to_sib.append(copy(base + 6, rows(bd), sib))
            to_sib[-1].start()
            return [pass_y, pass_x] + to_sib

        def from_sibling(ref, half, base):
            for k, blk in enumerate((bx, by, bd)):
                copy(base + 4 + k, ref.at[blk, pl.ds((1 - c) * half, half), :], sib).wait_recv()

        small = [copy(14 + k, cw_o.at[b], to) for k, to in enumerate((xnbr, ynbr, diag))]
        for cp in small:
            cp.start()
        started = send_own(win_o, rin, 0) + send_own(wout_o, rout, 7)
        started += relay(win_o, rin, 0) + relay(wout_o, rout, 7)
        for k, blk in enumerate((bx, by, bd)):
            copy(14 + k, cw_o.at[blk], sib).wait_recv()
        from_sibling(win_o, rin, 0)
        from_sibling(wout_o, rout, 7)
        for cp in small + started:
            cp.wait_send()

    vm = pl.BlockSpec(memory_space=pltpu.VMEM)
    return pl.pallas_call(
        body, name="wgather",
        out_shape=(jax.ShapeDtypeStruct((4,) + w_in.shape, BF16),
                   jax.ShapeDtypeStruct((4,) + w_out.shape, BF16),
                   jax.ShapeDtypeStruct((4,) + conv_w.shape, F32)),
        in_specs=[vm, vm, vm], out_specs=(vm, vm, vm),
        scratch_shapes=[pltpu.SemaphoreType.DMA((17,)), pltpu.SemaphoreType.DMA((17,))],
        compiler_params=_params(),
    )(w_in, w_out, conv_w)


TM_MATMUL = 512
TM_COMBINE = 256


def _resident(shape):
    return pl.BlockSpec(shape, lambda i: (0,) * len(shape), pipeline_mode=pl.Buffered(1))


def _to_slabs(slab, val, j0=0):
    for j in range(val.shape[1] // 128):
        slab[j0 + j] = val[:, 128 * j:128 * (j + 1)]


def _scatter_classes(slab, j0, nj, out_ref, d, part=0, mid=None):
    tm = slab.shape[1]
    n = tm // d
    if d == 4:
        for r in range(d):
            for j in range(nj):
                out_ref[r, part * n:(part + 1) * n, 128 * j:128 * (j + 1)] = (
                    slab[j0 + j, pl.ds(r, n, stride=d), :].astype(out_ref.dtype))
        return
    q = tm // 4
    for lo in range(4):
        for j in range(nj):
            mid[j0 + j, lo * q:(lo + 1) * q, :] = slab[j0 + j, pl.ds(lo, q, stride=4), :]
    for hi in range(4):
        for lo in range(4):
            for j in range(nj):
                out_ref[4 * hi + lo, part * n:(part + 1) * n, 128 * j:128 * (j + 1)] = (
                    mid[j0 + j, pl.ds(lo * q + hi, n, stride=4), :].astype(out_ref.dtype))


def _gather_classes(slab, piece, nj, d, mid=None):
    tm = slab.shape[1]
    n = tm // d
    if d == 4:
        for r in range(d):
            for j in range(nj):
                slab[j, pl.ds(r, n, stride=d), :] = piece(r, j).astype(F32)
    else:
        q = tm // 4
        for hi in range(4):
            for lo in range(4):
                for j in range(nj):
                    mid[j, pl.ds(lo * q + hi, n, stride=4), :] = piece(4 * hi + lo, j).astype(F32)
        for lo in range(4):
            for j in range(nj):
                slab[j, pl.ds(lo, q, stride=4), :] = mid[j, lo * q:(lo + 1) * q, :]
    return jnp.concatenate([slab[j] for j in range(nj)], axis=1)


def _class_spec(d, width, tm):
    return pl.BlockSpec((d, tm // d, width), lambda i: (0, i, 0))


def _proj(x, tgt, norm_w, wblk, qkw, b256):
    s = x.shape[0]
    tm = TM_MATMUL
    nparts = 2
    tp = tm // nparts

    def body(x_ref, t_ref, nw_ref, w_ref, qkw_ref, b_ref, h_o, cg_o, qkr_o, qkn_o, vz_o, qkn4_o, v4_o, qkn16_o,
             v16_o, xmt_o, slabs, mids):
        for part in range(nparts):
            rows = slice(part * tp, (part + 1) * tp)
            slab = slabs.at[part]
            xf = x_ref[rows, :]
            xmt_o[rows, :] = xf - t_ref[rows, :]
            r = lax.rsqrt(jnp.mean(xf * xf, axis=-1, keepdims=True) + EPS)
            h = (xf * r * nw_ref[...]).astype(BF16)
            h_o[rows, :] = h
            p2 = jnp.dot(h, w_ref[2], preferred_element_type=F32)
            qkr_o[rows, :] = p2.astype(BF16)
            ss = _group_sum(p2 * p2, b_ref, split=False)
            rr = lax.rsqrt(ss * (1.0 / HEAD_DIM) + EPS)
            qkn = p2 * rr * qkw_ref[...]
            qkn_o[rows, :] = qkn.astype(BF16)
            _to_slabs(slab, qkn)
            p3 = jnp.dot(h, w_ref[3], preferred_element_type=F32)
            vz_o[rows, :] = p3.astype(BF16)
            _to_slabs(slab, p3[:, 0:512], 8)
            cg_o[rows, 0:1024] = jnp.dot(h, w_ref[0], preferred_element_type=F32).astype(BF16)
            cg_o[rows, 1024:2048] = jnp.dot(h, w_ref[1], preferred_element_type=F32).astype(BF16)
            for d, q_o, v_o in ((4, qkn4_o, v4_o), (16, qkn16_o, v16_o)):
                _scatter_classes(slab, 0, 8, q_o, d, part, mids.at[part])
                _scatter_classes(slab, 8, 4, v_o, d, part, mids.at[part])

    row = lambda w: pl.BlockSpec((tm, w), lambda i: (i, 0))
    full = lambda shp: pl.BlockSpec(shp, lambda i: (0,) * len(shp))
    nat = lambda w: jax.ShapeDtypeStruct((s, w), BF16)
    cls = lambda d, w: jax.ShapeDtypeStruct((d, s // d, w), BF16)
    return pl.pallas_call(
        body, name="proj", grid=(s // tm,),
        out_shape=(nat(1024), nat(2048), nat(1024), nat(1024), nat(1024),
                   cls(4, 1024), cls(4, 512), cls(16, 1024), cls(16, 512), jax.ShapeDtypeStruct((s, 1024), F32)),
        in_specs=[row(1024), row(1024), full((1, 1024)), _resident((4, 1024, 1024)), full((1, 1024)),
                  full((256, 256))],
        out_specs=(row(1024), row(2048), row(1024), row(1024), row(1024),
                   _class_spec(4, 1024, tm), _class_spec(4, 512, tm),
                   _class_spec(16, 1024, tm), _class_spec(16, 512, tm), row(1024)),
        scratch_shapes=[pltpu.VMEM((nparts, 12, tp, 128), F32), pltpu.VMEM((nparts, 12, tp, 128), F32)],
        compiler_params=_params(("parallel",)),
    )(x, tgt, norm_w, wblk, qkw, b256)


def _block_coords(t, i, nsub, nb, length):
    n = t * nsub + i
    q0 = i * QB
    start = pl.multiple_of(jnp.clip(n * QB - HALF, 0, length - KB), HALF)
    variant = jnp.where(n == 0, 0, jnp.where(n == nb - 1, 2, 1))
    return q0, start, variant


def _split_heads(a, lo):
    zero = jnp.zeros_like(a)
    return jnp.concatenate([jnp.where(lo, a, zero), jnp.where(lo, zero, a)], axis=0)


def _col_pair(ref, q0, lane):
    return jnp.concatenate([ref[pl.ds(q0, QB), lane:lane + 1],
                            ref[pl.ds(q0, QB), HEAD_DIM + lane:HEAD_DIM + lane + 1]], axis=0)


def _attn_fwd(qkn_l, v_l, bias, gi, name):
    r_cls, length, _ = qkn_l.shape
    qt = min(length, 2048)
    nb, nsub = length // QB, qt // QB

    def body(q_ref, k_ref, v_ref, b_ref, o_ref, lse_ref):
        t = pl.program_id(2)
        lo = lax.broadcasted_iota(jnp.int32, (QB, 128), 1) < HEAD_DIM

        starts, logits = [], []
        for i in range(nsub):
            _, start, variant = _block_coords(t, i, nsub, nb, length)
            qq = _split_heads(q_ref[i * QB:(i + 1) * QB, :], lo)
            k = k_ref[pl.ds(start, KB), :]
            logits.append(lax.dot_general(qq, k, (((1,), (1,)), ((), ())), preferred_element_type=F32)
                          + b_ref[variant])
            starts.append(start)
        lg = jnp.concatenate(logits, axis=0)
        m = jnp.max(lg, axis=-1, keepdims=True)
        p = jnp.exp(lg - m)
        pb = p.astype(BF16)
        l = jnp.sum(p, axis=-1, keepdims=True)
        lse = jnp.broadcast_to(m + jnp.log(l), (nsub * 2 * QB, 128))
        inv = 1.0 / l
        for i in range(nsub):
            rows = slice(2 * QB * i, 2 * QB * (i + 1))
            v = v_ref[pl.ds(starts[i], KB), :]
            pv = jnp.dot(pb[rows], v, preferred_element_type=F32) * inv[rows]
            o_ref[i * QB:(i + 1) * QB, :] = jnp.where(lo, pv[0:QB], pv[QB:2 * QB]).astype(BF16)
            ls = lse[rows]
            lse_ref[i * QB:(i + 1) * QB, :] = jnp.where(lo, ls[0:QB], ls[QB:2 * QB])

    return pl.pallas_call(
        body, name=name, grid=(N_PAIR, r_cls, length // qt),
        out_shape=(jax.ShapeDtypeStruct((r_cls, length, 512), BF16),
                   jax.ShapeDtypeStruct((r_cls, length, 512), F32)),
        in_specs=[pl.BlockSpec((None, qt, 128), lambda p, r, t: (r, t, p)),
                  pl.BlockSpec((None, length, 128), lambda p, r, t: (r, 0, 4 + p)),
                  pl.BlockSpec((None, length, 128), lambda p, r, t: (r, 0, p)),
                  pl.BlockSpec((None, 3, None, 2 * QB, KB), lambda p, r, t: (gi, 0, p, 0, 0))],
        out_specs=(pl.BlockSpec((None, qt, 128), lambda p, r, t: (r, t, p)),
                   pl.BlockSpec((None, qt, 128), lambda p, r, t: (r, t, p))),
        compiler_params=_params(("parallel", "parallel", "arbitrary")),
    )(qkn_l, qkn_l, v_l, bias)


def _combine(o_g, lse_g, cg, vz, xmt, wout, cw, cb, b256):
    s = xmt.shape[0]
    tm = TM_COMBINE
    hb = 16
    nt = s // tm

    def body(o1, o4, o16, l1, l4, l16, cg_ref, cgp_ref, cgn_ref, za_ref, xmt_ref, w_ref, cw_ref, cb_ref,
             b_ref, y_o, dout_o, ld1_o, do1_o, dza_o, dgbz_o, dzc_o, loss_o, dcb_o, dcw_o,
             do4_o, ld4_o, do16_o, ld16_o, slab, mid):
        i = pl.program_id(0)

        @pl.when(i == 0)
        def _():
            loss_o[...] = jnp.zeros_like(loss_o)
            dcb_o[...] = jnp.zeros_like(dcb_o)
            dcw_o[...] = jnp.zeros_like(dcw_o)

        u = cg_ref[:, 0:512].astype(F32)
        gb = cg_ref[:, 512:1024].astype(F32)
        gc = cg_ref[:, 1024:1536].astype(F32)
        zc = cg_ref[:, 1536:2048].astype(F32)
        tt = gc * u
        t_prev = cgp_ref[hb - 1:hb, 0:512].astype(F32) * cgp_ref[hb - 1:hb, 1024:1536].astype(F32)
        t_next = cgn_ref[0:1, 0:512].astype(F32) * cgn_ref[0:1, 1024:1536].astype(F32)
        t_prev = jnp.where(i == 0, 0.0, t_prev)
        t_next = jnp.where(i == nt - 1, 0.0, t_next)
        rows = lax.broadcasted_iota(jnp.int32, (tm, 512), 0)
        t_up = jnp.where(rows == 0, t_prev, pltpu.roll(tt, 1, 0))
        t_dn = jnp.where(rows == tm - 1, t_next, pltpu.roll(tt, tm - 1, 0))
        w0, w1, w2 = cw_ref[0:1, :], cw_ref[1:2, :], cw_ref[2:3, :]
        zb = w0 * t_up + w1 * tt + w2 * t_dn + cb_ref[...]
        sg = _sigmoid(zc)
        sz = zc * sg
        y_conv = gb * zb * sz

        a1, p1 = l1[0], o1[0].astype(F32)
        a4 = _gather_classes(slab, lambda r, j: l4[r, :, 128 * j:128 * (j + 1)], 4, 4)
        p4 = _gather_classes(slab, lambda r, j: o4[r, :, 128 * j:128 * (j + 1)], 4, 4)
        a16 = _gather_classes(slab, lambda r, j: l16[r, :, 128 * j:128 * (j + 1)], 4, 16, mid)
        p16 = _gather_classes(slab, lambda r, j: o16[r, :, 128 * j:128 * (j + 1)], 4, 16, mid)
        m = jnp.maximum(jnp.maximum(a1, a4), a16)
        e1, e4, e16 = jnp.exp(a1 - m), jnp.exp(a4 - m), jnp.exp(a16 - m)
        den = e1 + e4 + e16
        lse = m + jnp.log(den)
        o = (e1 * p1 + e4 * p4 + e16 * p16) / den
        za = za_ref[...].astype(F32)
        sga = _sigmoid(za)
        sa = za * sga
        y = jnp.concatenate([y_conv, o * sa], axis=1).astype(BF16)
        y_o[...] = y

        diff = xmt_ref[...] + jnp.dot(y, w_ref[...], preferred_element_type=F32)
        loss_o[...] += (0.5 / D_MODEL) * jnp.sum(diff * diff)
        dout = diff * (1.0 / D_MODEL)
        dout_o[...] = dout
        dy = lax.dot_general(dout.astype(BF16), w_ref[...], (((1,), (1,)), ((), ())), preferred_element_type=F32)
        dyc, dya = dy[:, 0:512], dy[:, 512:1024]

        do = dya * sa
        dza_o[...] = (dya * o * (sga * (1.0 + za * (1.0 - sga)))).astype(BF16)
        lane = lax.broadcasted_iota(jnp.int32, (tm, 512), 1)
        ld = jnp.where((lane & (HEAD_DIM - 1)) < HEAD_DIM // 2, lse, _group_sum(do * o, b_ref))
        do1_o[0] = do.astype(BF16)
        ld1_o[0] = ld
        _to_slabs(slab, do)
        _scatter_classes(slab, 0, 4, do4_o, 4)
        _scatter_classes(slab, 0, 4, do16_o, 16, 0, mid)
        _to_slabs(slab, ld)
        _scatter_classes(slab, 0, 4, ld4_o, 4)
        _scatter_classes(slab, 0, 4, ld16_o, 16, 0, mid)

        dzc = dyc * sz * gb
        dzc_o[...] = dzc.astype(BF16)
        dgbz_o[:, 0:512] = (dyc * sz * zb).astype(BF16)
        dgbz_o[:, 512:1024] = (dyc * gb * zb * (sg * (1.0 + zc * (1.0 - sg)))).astype(BF16)
        dcb_o[...] += jnp.sum(dzc, axis=0, keepdims=True)
        dcw_o[0:1, :] += jnp.sum(dzc * t_up, axis=0, keepdims=True)
        dcw_o[1:2, :] += jnp.sum(dzc * tt, axis=0, keepdims=True)
        dcw_o[2:3, :] += jnp.sum(dzc * t_dn, axis=0, keepdims=True)

    row = lambda w, j=0: pl.BlockSpec((tm, w), lambda i: (i, j))
    full = lambda shp: pl.BlockSpec(shp, lambda i: (0,) * len(shp))
    prev = pl.BlockSpec((hb, 2048), lambda i: (jnp.maximum(i * (tm // hb) - 1, 0), 0))
    nxt = pl.BlockSpec((hb, 2048), lambda i: (jnp.minimum((i + 1) * (tm // hb), s // hb - 1), 0))
    cls = lambda d, dt: jax.ShapeDtypeStruct((d, s // d, 512), dt)
    cspecs = [_class_spec(d, 512, tm) for d in DILATIONS]
    return pl.pallas_call(
        body, name="combine", grid=(nt,),
        out_shape=(jax.ShapeDtypeStruct((s, 1024), BF16), jax.ShapeDtypeStruct((s, 1024), F32),
                   cls(1, F32), cls(1, BF16), jax.ShapeDtypeStruct((s, 512), BF16),
                   jax.ShapeDtypeStruct((s, 1024), BF16), jax.ShapeDtypeStruct((s, 512), BF16),
                   jax.ShapeDtypeStruct((1, 128), F32), jax.ShapeDtypeStruct((1, 512), F32),
                   jax.ShapeDtypeStruct((8, 512), F32),
                   cls(4, BF16), cls(4, F32), cls(16, BF16), cls(16, F32)),
        in_specs=cspecs + cspecs + [row(2048), prev, nxt, row(512, 1), row(1024),
                                    _resident((1024, 1024)), full((8, 512)), full((1, 512)), full((256, 256))],
        out_specs=(row(1024), row(1024), cspecs[0], cspecs[0], row(512), row(1024), row(512),
                   full((1, 128)), full((1, 512)), full((8, 512)),
                   cspecs[1], cspecs[1], cspecs[2], cspecs[2]),
        scratch_shapes=[pltpu.VMEM((4, tm, 128), F32), pltpu.VMEM((4, tm, 128), F32)],
        compiler_params=_params(("arbitrary",)),
    )(*o_g, *lse_g, cg, cg, cg, vz, xmt, wout, cw, cb, b256)


def _attn_bwd(qkn_l, v_l, do_l, ld_l, bias, gi, name):
    r_cls, length, _ = qkn_l.shape
    qt = min(length, 2048 if length <= 4096 else 1024)
    nb, nsub, nt = length // QB, qt // QB, length // qt
    chunk = min(length, 4096)
    nchunk = length // chunk

    def body(q_ref, k_ref, v_ref, do_ref, ld_ref, b_ref, dq_ref, dkv_hbm, dsum_ref, dk_acc, dv_acc, stage, sems):
        p_id, r, t = pl.program_id(0), pl.program_id(1), pl.program_id(2)
        lo = lax.broadcasted_iota(jnp.int32, (QB, 128), 1) < HEAD_DIM

        @pl.when(t == 0)
        def _():
            dk_acc[...] = jnp.zeros_like(dk_acc)
            dv_acc[...] = jnp.zeros_like(dv_acc)

        @pl.when((t == 0) & (r == 0))
        def _():
            dsum_ref[...] = jnp.zeros_like(dsum_ref)

        nt_dims = (((1,), (1,)), ((), ()))
        tn_dims = (((0,), (0,)), ((), ()))
        coords, qqs, dds, logits, dps, lcols, dcols = [], [], [], [], [], [], []
        for i in range(nsub):
            q0, start, variant = _block_coords(t, i, nsub, nb, length)
            qq = _split_heads(q_ref[q0:q0 + QB, :], lo)
            dd = _split_heads(do_ref[q0:q0 + QB, :], lo)
            k = k_ref[pl.ds(start, KB), :]
            v = v_ref[pl.ds(start, KB), :]
            logits.append(lax.dot_general(qq, k, nt_dims, preferred_element_type=F32) + b_ref[variant])
            dps.append(lax.dot_general(dd, v, nt_dims, preferred_element_type=F32))
            lcols.append(_col_pair(ld_ref, q0, 0))
            dcols.append(_col_pair(ld_ref, q0, HEAD_DIM // 2))
            coords.append((q0, start, variant))
            qqs.append(qq)
            dds.append(dd)
        p = jnp.exp(jnp.concatenate(logits, axis=0) - jnp.concatenate(lcols, axis=0))
        ds = p * (jnp.concatenate(dps, axis=0) - jnp.concatenate(dcols, axis=0))
        pb = p.astype(BF16)
        dsb = ds.astype(BF16)
        middle = None
        for i in range(nsub):
            q0, start, variant = coords[i]
            rows = slice(2 * QB * i, 2 * QB * (i + 1))
            if 0 < i < nsub - 1:
                middle = ds[rows] if middle is None else middle + ds[rows]
            else:
                dsum_ref[variant] += ds[rows]
            dqq = jnp.dot(dsb[rows], k_ref[pl.ds(start, KB), :], preferred_element_type=F32)
            dq_ref[q0:q0 + QB, :] = jnp.where(lo, dqq[0:QB], dqq[QB:2 * QB]).astype(BF16)
            dk_acc[pl.ds(start, KB), :] += lax.dot_general(dsb[rows], qqs[i], tn_dims, preferred_element_type=F32)
            dv_acc[pl.ds(start, KB), :] += lax.dot_general(pb[rows], dds[i], tn_dims, preferred_element_type=F32)
        if middle is not None:
            dsum_ref[1] += middle

        @pl.when(t == nt - 1)
        def _():
            def copy(k):
                which, c = k // nchunk, k % nchunk
                rows = pl.ds(c * chunk, chunk)
                return pltpu.make_async_copy(stage.at[k % 2], dkv_hbm.at[r, p_id, which, rows, :], sems.at[k % 2])

            for k in range(2 * nchunk):
                if k < 2:
                    @pl.when((p_id > 0) | (r > 0))
                    def _():
                        copy(k).wait()
                else:
                    copy(k).wait()
                acc = (dk_acc, dv_acc)[k // nchunk]
                stage[k % 2] = acc[pl.ds((k % nchunk) * chunk, chunk), :].astype(BF16)
                copy(k).start()

            @pl.when((p_id == N_PAIR - 1) & (r == r_cls - 1))
            def _():
                copy(0).wait()
                copy(1).wait()

    qspec = pl.BlockSpec((None, qt, 128), lambda p, r, t: (r, t, p))
    return pl.pallas_call(
        body, name=name, grid=(N_PAIR, r_cls, nt),
        out_shape=(jax.ShapeDtypeStruct((r_cls, length, 512), BF16),
                   jax.ShapeDtypeStruct((r_cls, N_PAIR, 2, length, 128), BF16),
                   jax.ShapeDtypeStruct((N_PAIR, 3, 2 * QB, KB), F32)),
        in_specs=[qspec,
                  pl.BlockSpec((None, length, 128), lambda p, r, t: (r, 0, 4 + p)),
                  pl.BlockSpec((None, length, 128), lambda p, r, t: (r, 0, p)),
                  qspec, qspec,
                  pl.BlockSpec((None, 3, None, 2 * QB, KB), lambda p, r, t: (gi, 0, p, 0, 0))],
        out_specs=(qspec, pl.BlockSpec(memory_space=pl.ANY),
                   pl.BlockSpec((None, 3, 2 * QB, KB), lambda p, r, t: (p, 0, 0, 0))),
        scratch_shapes=[pltpu.VMEM((length, 128), F32), pltpu.VMEM((length, 128), F32),
                        pltpu.VMEM((2, chunk, 128), BF16), pltpu.SemaphoreType.DMA((2,))],
        compiler_params=_params(("arbitrary", "arbitrary", "arbitrary")),
    )(qkn_l, qkn_l, v_l, do_l, ld_l, bias)


def _bwd_tail(dq_g, dkv_g, qkr, qkw, dza, dgbz, dzc, cg, cw, wblk, x, norm_w, dout, b256):
    s = x.shape[0]
    tm = TM_COMBINE
    hb = 16
    nt = s // tm

    def body(dq1, dq4, dq16, dkv1, dkv4, dkv16, qkr_ref, qkw_ref, dza_ref, dgbz_ref, dzc_ref,
             dzp_ref, dzn_ref, u_ref, gc_ref, cw_ref, w_ref, x_ref, nw_ref, dout_ref, b_ref,
             gx_o, dproj_o, dnw_o, dqkw_o, slab, mid):
        i = pl.program_id(0)

        def nat_q(ref, d):
            return _gather_classes(slab, lambda r, j: ref[r, :, 128 * j:128 * (j + 1)], 4, d, mid)

        def nat_kv(ref, d, which):
            return _gather_classes(slab, lambda r, j: ref[r, j, which], 4, d, mid)

        @pl.when(i == 0)
        def _():
            dnw_o[...] = jnp.zeros_like(dnw_o)
            dqkw_o[...] = jnp.zeros_like(dqkw_o)

        dzc = dzc_ref[...].astype(F32)
        d_prev = jnp.where(i == 0, 0.0, dzp_ref[hb - 1:hb, :].astype(F32))
        d_next = jnp.where(i == nt - 1, 0.0, dzn_ref[0:1, :].astype(F32))
        rows = lax.broadcasted_iota(jnp.int32, (tm, 512), 0)
        d_up = jnp.where(rows == 0, d_prev, pltpu.roll(dzc, 1, 0))
        d_dn = jnp.where(rows == tm - 1, d_next, pltpu.roll(dzc, tm - 1, 0))
        dt = cw_ref[0:1, :] * d_dn + cw_ref[1:2, :] * dzc + cw_ref[2:3, :] * d_up
        u = u_ref[...].astype(F32)
        gc = gc_ref[...].astype(F32)
        dproj_o[:, 0:512] = (dt * gc).astype(BF16)
        dproj_o[:, 512:1024] = dgbz_ref[:, 0:512]
        dproj_o[:, 1024:1536] = (dt * u).astype(BF16)
        dproj_o[:, 1536:2048] = dgbz_ref[:, 512:1024]

        dqn = (dq1[0].astype(F32) + nat_q(dq4, 4) + nat_q(dq16, 16)) * (1.0 / 8.0)
        dk1 = jnp.concatenate([dkv1[0, j, 0] for j in range(N_PAIR)], axis=1)
        dv1 = jnp.concatenate([dkv1[0, j, 1] for j in range(N_PAIR)], axis=1)
        dkn = dk1 + nat_kv(dkv4, 4, 0) + nat_kv(dkv16, 16, 0)
        dvn = dv1 + nat_kv(dkv4, 4, 1) + nat_kv(dkv16, 16, 1)
        g = jnp.concatenate([dqn, dkn], axis=1) * qkw_ref[...]
        raw = qkr_ref[...].astype(F32)
        rr = lax.rsqrt(_group_sum(raw * raw, b_ref, split=False) * (1.0 / HEAD_DIM) + EPS)
        proj_gq = _group_sum(g * raw, b_ref) * (1.0 / HEAD_DIM)
        draw = rr * g - raw * (rr * rr * rr) * proj_gq
        dqkw_o[...] += jnp.sum(jnp.concatenate([dqn, dkn], axis=1) * raw * rr, axis=0, keepdims=True)
        dproj_o[:, 2048:3072] = draw.astype(BF16)
        dproj_o[:, 3072:3584] = dvn.astype(BF16)
        dproj_o[:, 3584:4096] = dza_ref[...]

        nt_dims = (((1,), (1,)), ((), ()))
        dh = lax.dot_general(dproj_o[:, 0:1024], w_ref[0], nt_dims, preferred_element_type=F32)
        for b in range(1, 4):
            dh += lax.dot_general(dproj_o[:, 1024 * b:1024 * b + 1024], w_ref[b], nt_dims,
                                  preferred_element_type=F32)

        xf = x_ref[...]
        r = lax.rsqrt(jnp.mean(xf * xf, axis=-1, keepdims=True) + EPS)
        gh = dh * nw_ref[...]
        dnw_o[...] += jnp.sum(dh * xf * r, axis=0, keepdims=True)
        mean_gx = jnp.mean(gh * xf, axis=-1, keepdims=True)
        gx_o[...] = dout_ref[...] + r * gh - xf * (r * r * r) * mean_gx

    row = lambda w, j=0: pl.BlockSpec((tm, w), lambda i: (i, j))
    full = lambda shp: pl.BlockSpec(shp, lambda i: (0,) * len(shp))
    prev = pl.BlockSpec((hb, 512), lambda i: (jnp.maximum(i * (tm // hb) - 1, 0), 0))
    nxt = pl.BlockSpec((hb, 512), lambda i: (jnp.minimum((i + 1) * (tm // hb), s // hb - 1), 0))
    return pl.pallas_call(
        body, name="bwd_tail", grid=(nt,),
        out_shape=(jax.ShapeDtypeStruct((s, 1024), F32), jax.ShapeDtypeStruct((s, 4096), BF16),
                   jax.ShapeDtypeStruct((1, 1024), F32), jax.ShapeDtypeStruct((1, 1024), F32)),
        in_specs=[_class_spec(d, 512, tm) for d in DILATIONS]
        + [pl.BlockSpec((d, N_PAIR, 2, tm // d, 128), lambda i: (0, 0, 0, i, 0)) for d in DILATIONS]
        + [row(1024), full((1, 1024)), row(512), row(1024), row(512), prev, nxt,
           row(512, 0), row(512, 2), full((8, 512)), _resident((4, 1024, 1024)), row(1024),
           full((1, 1024)), row(1024), full((256, 256))],
        out_specs=(row(1024), row(4096), full((1, 1024)), full((1, 1024))),
        scratch_shapes=[pltpu.VMEM((4, tm, 128), F32), pltpu.VMEM((4, tm, 128), F32)],
        compiler_params=_params(("arbitrary",)),
    )(*dq_g, *dkv_g, qkr, qkw, dza, dgbz, dzc, dzc, dzc, cg, cg, cw, wblk, x, norm_w, dout, b256)


def _wgrad(a, b, row_blocked, name):
    s, m = a.shape
    n = b.shape[1]
    tk = 2048
    ncol = min(n, 2048)
    nj, nk = n // ncol, s // tk

    def body(a_ref, b_ref, o_ref, acc):
        kk = pl.program_id(1)

        @pl.when(kk == 0)
        def _():
            acc[...] = jnp.zeros_like(acc)

        acc[...] += lax.dot_general(a_ref[...], b_ref[...].astype(BF16), (((0,), (0,)), ((), ())),
                                    preferred_element_type=F32)

        @pl.when(kk == nk - 1)
        def _():
            blocks, _, rows, _ = o_ref.shape
            for blk in range(blocks):
                for half in range(2):
                    if row_blocked:
                        r0 = (2 * blk + half) * rows
                        o_ref[blk, half] = acc[r0:r0 + rows, :].astype(BF16)
                    else:
                        o_ref[blk, half] = acc[half * rows:(half + 1) * rows,
                                               1024 * blk:1024 * (blk + 1)].astype(BF16)

    if row_blocked:
        out_shape = jax.ShapeDtypeStruct((4, 2, m // 8, 1024), BF16)
        out_spec = pl.BlockSpec((4, 2, m // 8, 1024), lambda j, k: (0, 0, 0, 0))
    else:
        out_shape = jax.ShapeDtypeStruct((n // 1024, 2, m // 2, 1024), BF16)
        out_spec = pl.BlockSpec((ncol // 1024, 2, m // 2, 1024), lambda j, k: (j, 0, 0, 0))
    return pl.pallas_call(
        body, name=name, grid=(nj, nk),
        out_shape=out_shape,
        in_specs=[pl.BlockSpec((tk, m), lambda j, k: (k, 0)), pl.BlockSpec((tk, ncol), lambda j, k: (k, j))],
        out_specs=out_spec,
        scratch_shapes=[pltpu.VMEM((m, ncol), F32)],
        compiler_params=_params(("parallel", "arbitrary")),
    )(a, b)


def _dbias(dsums, onehot_all):
    def body(ds1_ref, ds4_ref, ds16_ref, oh_ref, o_ref):
        @pl.when(pl.program_id(0) == 0)
        def _():
            o_ref[...] = jnp.zeros_like(o_ref)

        hrow = lax.broadcasted_iota(jnp.int32, (8, KB), 0)
        flip = (lax.broadcasted_iota(jnp.int32, (QB, QB), 0)
                + lax.broadcasted_iota(jnp.int32, (QB, QB), 1) == QB - 1).astype(F32)

        def diagonal_sums(tile):
            rev = jnp.dot(flip, tile, preferred_element_type=F32, precision=lax.Precision.HIGHEST)
            sums = jnp.sum(pltpu.roll(rev, 0, 1, stride=1, stride_axis=0), axis=0, keepdims=True)
            return pltpu.roll(sums, KB - (QB - 1), 1)

        for g, ds_ref in enumerate((ds1_ref, ds4_ref, ds16_ref)):
            diag = jnp.zeros((8, KB), F32)
            for p in range(N_PAIR):
                diag = jnp.where(hrow == 2 * p, diagonal_sums(ds_ref[p, 0:QB, :]), diag)
                diag = jnp.where(hrow == 2 * p + 1, diagonal_sums(ds_ref[p, QB:2 * QB, :]), diag)
            o_ref[...] += jnp.dot(diag, oh_ref[g], preferred_element_type=F32, precision=lax.Precision.HIGHEST)

    ds_spec = pl.BlockSpec((N_PAIR, None, 2 * QB, KB), lambda v: (0, v, 0, 0))
    return pl.pallas_call(
        body, name="dbias", grid=(3,),
        out_shape=jax.ShapeDtypeStruct((8, 128), F32),
        in_specs=[ds_spec, ds_spec, ds_spec, pl.BlockSpec((3, None, KB, 128), lambda v: (0, v, 0, 0))],
        out_specs=pl.BlockSpec((8, 128), lambda v: (0, 0)),
        compiler_params=_params(("arbitrary",)),
    )(*dsums, onehot_all)


def _gsync(pw_in, pw_out, small):
    hin, hout = pw_in.shape[2], pw_out.shape[2]
    nsmall = small.shape[0]

    def body(pin_hbm, pout_hbm, small_ref, gin_o, gout_o, small_o,
             mine_in, recv_in, s1_in, r1_in, s2_in, r2_in, mine_out, recv_out, s1_out, r1_out, s2_out, r2_out, gather,
             lsem, asend, arecv, bsend, brecv, csend, crecv, ssend, srecv):
        x, y, c = lax.axis_index("x"), lax.axis_index("y"), lax.axis_index("c")
        b = 2 * x + y
        dev = 4 * x + 2 * y + c
        sib = (x, y, 1 - c)
        xnbr, ynbr = (1 - x, y, c), (x, 1 - y, c)
        bx, by, bd = b ^ 2, b ^ 1, b ^ 3

        def rcopy(src, dst, ssem, rsem, to):
            return pltpu.make_async_remote_copy(src_ref=src, dst_ref=dst, send_sem=ssem, recv_sem=rsem,
                                                device_id=to, device_id_type=MESH)

        gather[dev] = small_ref[...]
        s_sends = []
        for k in range(1, 8):
            to = (x ^ (k >> 2), y ^ ((k >> 1) & 1), c ^ (k & 1))
            cp = rcopy(gather.at[dev], gather.at[dev], ssend.at[k - 1], srecv.at[k - 1], to)
            cp.start()
            s_sends.append(cp)

        a_in = rcopy(pin_hbm.at[:, 1 - c], recv_in, asend.at[0], arecv.at[0], sib)
        a_out = rcopy(pout_hbm.at[:, 1 - c], recv_out, asend.at[1], arecv.at[1], sib)
        a_in.start()
        a_out.start()
        l_in = pltpu.make_async_copy(pin_hbm.at[:, c], mine_in, lsem.at[0])
        l_out = pltpu.make_async_copy(pout_hbm.at[:, c], mine_out, lsem.at[1])
        l_in.start()
        l_out.start()
        l_in.wait()
        l_out.wait()

        def phase_one(a_cp, mine, recv, s1, r1, half, base):
            a_cp.wait_recv()
            q = half // 2
            sends = []
            for part, (peer, blk_peer) in enumerate(((xnbr, bx), (ynbr, by))):
                rows = pl.ds(part * q, q)
                for slot, blk in enumerate((blk_peer, bd)):
                    s1[part, slot] = (mine[blk, rows, :].astype(F32) + recv[blk, rows, :].astype(F32)).astype(BF16)
                cp = rcopy(s1.at[part], r1.at[part], bsend.at[base + part], brecv.at[base + part], peer)
                cp.start()
                sends.append(cp)
            return sends

        def phase_two(p1, mine, recv, r1, s2, r2, half, base):
            q = half // 2
            own, sends = [], []
            for part, (peer, blk_next) in enumerate(((ynbr, by), (xnbr, bx))):
                rows = pl.ds(part * q, q)
                p1[part].wait_recv()
                own.append(mine[b, rows, :].astype(F32) + recv[b, rows, :].astype(F32) + r1[part, 0].astype(F32))
                s2[part] = (mine[blk_next, rows, :].astype(F32) + recv[blk_next, rows, :].astype(F32)
                            + r1[part, 1].astype(F32)).astype(BF16)
                cp = rcopy(s2.at[part], r2.at[part], bsend.at[base + 2 + part], brecv.at[base + 2 + part], peer)
                cp.start()
                sends.append(cp)
            return own, sends

        def stage_c(own, p2, r2, g_o, half, idx):
            q = half // 2
            for part in range(2):
                p2[part].wait_recv()
                g_o[pl.ds(pl.multiple_of(c * half + part * q, q), q), :] = own[part] + r2[part].astype(F32)
            rows = g_o.at[pl.ds(pl.multiple_of(c * half, half), half), :]
            cp = rcopy(rows, rows, csend.at[idx], crecv.at[idx], sib)
            cp.start()
            return cp

        p1_in = phase_one(a_in, mine_in, recv_in, s1_in, r1_in, hin, 0)
        p1_out = phase_one(a_out, mine_out, recv_out, s1_out, r1_out, hout, 4)
        own_in, p2_in = phase_two(p1_in, mine_in, recv_in, r1_in, s2_in, r2_in, hin, 0)
        own_out, p2_out = phase_two(p1_out, mine_out, recv_out, r1_out, s2_out, r2_out, hout, 4)
        c_in = stage_c(own_in, p2_in, r2_in, gin_o, hin, 0)
        c_out = stage_c(own_out, p2_out, r2_out, gout_o, hout, 1)
        b_in, b_out = p1_in + p2_in, p1_out + p2_out

        for cp in s_sends:
            cp.wait_recv()
        tot = gather[0]
        for d in range(1, 8):
            tot = tot + gather[d]
        small_o[...] = tot

        for g_o, half, idx in ((gin_o, hin, 0), (gout_o, hout, 1)):
            other = g_o.at[pl.ds(pl.multiple_of((1 - c) * half, half), half), :]
            rcopy(other, other, csend.at[idx], crecv.at[idx], sib).wait_recv()
        for cp in s_sends + [a_in, a_out] + b_in + b_out + [c_in, c_out]:
            cp.wait_send()

    vm = pl.BlockSpec(memory_space=pltpu.VMEM)
    hbm = pl.BlockSpec(memory_space=pl.ANY)
    return pl.pallas_call(
        body, name="gsync",
        out_shape=(jax.ShapeDtypeStruct((2 * hin, 1024), F32), jax.ShapeDtypeStruct((2 * hout, 1024), F32),
                   jax.ShapeDtypeStruct((nsmall, 128), F32)),
        in_specs=[hbm, hbm, vm], out_specs=(vm, vm, vm),
        scratch_shapes=[pltpu.VMEM((4, hin, 1024), BF16), pltpu.VMEM((4, hin, 1024), BF16),
                        pltpu.VMEM((2, 2, hin // 2, 1024), BF16), pltpu.VMEM((2, 2, hin // 2, 1024), BF16),
                        pltpu.VMEM((2, hin // 2, 1024), BF16), pltpu.VMEM((2, hin // 2, 1024), BF16),
                        pltpu.VMEM((4, hout, 1024), BF16), pltpu.VMEM((4, hout, 1024), BF16),
                        pltpu.VMEM((2, 2, hout // 2, 1024), BF16), pltpu.VMEM((2, 2, hout // 2, 1024), BF16),
                        pltpu.VMEM((2, hout // 2, 1024), BF16), pltpu.VMEM((2, hout // 2, 1024), BF16),
                        pltpu.VMEM((8, nsmall, 128), F32),
                        pltpu.SemaphoreType.DMA((2,)),
                        pltpu.SemaphoreType.DMA((2,)), pltpu.SemaphoreType.DMA((2,)),
                        pltpu.SemaphoreType.DMA((8,)), pltpu.SemaphoreType.DMA((8,)),
                        pltpu.SemaphoreType.DMA((2,)), pltpu.SemaphoreType.DMA((2,)),
                        pltpu.SemaphoreType.DMA((7,)), pltpu.SemaphoreType.DMA((7,))],
        compiler_params=_params(),
    )(pw_in, pw_out, small)


def _adamw_math(w, g, m, v):
    m = ADAM_B1 * m + (1.0 - ADAM_B1) * g
    v = ADAM_B2 * v + (1.0 - ADAM_B2) * (g * g)
    m_hat = m / (1.0 - ADAM_B1 ** ADAM_STEP)
    v_hat = v / (1.0 - ADAM_B2 ** ADAM_STEP)
    delta = -ADAM_LR * (m_hat / (jnp.sqrt(v_hat) + ADAM_EPS) + ADAM_WD * w)
    return delta, m, v


def _adamw(w, g, m, v, name):
    rows, cols = w.shape
    tr = 256 if rows % 256 == 0 else rows

    def body(w_ref, g_ref, m_ref, v_ref, g_o, d_o, m_o, v_o):
        g = g_ref[...]
        d, m2, v2 = _adamw_math(w_ref[...], g, m_ref[...], v_ref[...])
        g_o[...] = g
        d_o[...] = d
        m_o[...] = m2
        v_o[...] = v2

    spec = pl.BlockSpec((tr, cols), lambda i: (i, 0))
    shp = jax.ShapeDtypeStruct((rows, cols), F32)
    return pl.pallas_call(
        body, name=name, grid=(rows // tr,), out_shape=(shp, shp, shp, shp),
        in_specs=[spec] * 4, out_specs=(spec, spec, spec, spec),
        compiler_params=_params(("parallel",)),
    )(w, g, m, v)


def _fold_heads(dqkw):
    def body(x_ref, o_ref):
        xs = x_ref[...]
        sq = xs[0:1] + xs[1:2] + xs[2:3] + xs[3:4]
        sk = xs[4:5] + xs[5:6] + xs[6:7] + xs[7:8]
        both = jnp.concatenate([sq, sk], axis=0)
        o_ref[...] = both + pltpu.roll(both, HEAD_DIM, 1)

    vm = pl.BlockSpec(memory_space=pltpu.VMEM)
    return pl.pallas_call(body, name="fold_heads", out_shape=jax.ShapeDtypeStruct((2, 128), F32),
                          in_specs=[vm], out_specs=vm, compiler_params=_params())(dqkw)


def kernel(x, norm_w, w_in, conv_w, conv_b, q_norm_w, k_norm_w, rel_bias, w_out, loss_target, m_norm_w, m_w_in, m_conv_w, m_conv_b, m_q_norm_w, m_k_norm_w, m_rel_bias, m_w_out, v_norm_w, v_w_in, v_conv_w, v_conv_b, v_q_norm_w, v_k_norm_w, v_rel_bias, v_w_out):
    x2 = x[0]
    tgt = loss_target[0]
    blk = 2 * lax.axis_index("x") + lax.axis_index("y")

    conv_w8 = jnp.pad(conv_w, ((0, 5), (0, 0)))
    wblk, woutblk, cwblk = _wgather(w_in, w_out, conv_w8)
    wout_full = woutblk.reshape(1024, 1024)
    cw_full = cwblk.transpose(1, 0, 2).reshape(8, 512)

    qkw = jnp.concatenate([jnp.tile(q_norm_w, 8) * 0.125, jnp.tile(k_norm_w, 8)])[None, :]
    qkw_raw = jnp.concatenate([jnp.tile(q_norm_w, 8), jnp.tile(k_norm_w, 8)])[None, :]
    gidx = jnp.arange(256) // HEAD_DIM
    b256 = (gidx[:, None] == gidx[None, :]).astype(BF16)

    h, cg, qkr, qkn, vz, qkn4, v4, qkn16, v16, xmt = _proj(x2, tgt, norm_w[None, :], wblk, qkw, b256)

    biases = _bias_tables(rel_bias)
    qkn_l = [qkn[None], qkn4, qkn16]
    v_l = [vz[None], v4, v16]
    o_g, lse_g = [], []
    for gi, d in enumerate(DILATIONS):
        o_l, lse_l = _attn_fwd(qkn_l[gi], v_l[gi], biases, gi, f"attn_fwd_d{d}")
        o_g.append(o_l)
        lse_g.append(lse_l)

    (y, dout, ld1, do1, dza, dgbz, dzc, loss_p, dcb, dcw, do4, ld4, do16, ld16) = _combine(
        o_g, lse_g, cg, vz, xmt, wout_full, cw_full, conv_b[None, :], b256)

    dq_g, dkv_g, dsums = [], [], []
    for gi, (d, do_l, ld_l) in enumerate(zip(DILATIONS, (do1, do4, do16), (ld1, ld4, ld16))):
        dq_l, dkv_l, dsum = _attn_bwd(qkn_l[gi], v_l[gi], do_l, ld_l, biases, gi, f"attn_bwd_d{d}")
        dq_g.append(dq_l)
        dkv_g.append(dkv_l)
        dsums.append(dsum)

    grad_x, dproj, dnw, dqkw = _bwd_tail(dq_g, dkv_g, qkr, qkw_raw, dza, dgbz, dzc, cg, cw_full, wblk,
                                         x2, norm_w[None, :], dout, b256)

    pw_in = _wgrad(h, dproj, False, "wgrad_in")
    pw_out = _wgrad(y, dout, True, "wgrad_out")
    dbias8 = _dbias(dsums, jnp.stack([_diag_bucket_onehot(d) for d in DILATIONS], axis=0))

    small = jnp.concatenate([dnw.reshape(8, 128), dcb.reshape(4, 128), dqkw.reshape(8, 128),
                             dcw[0:3].reshape(12, 128), dbias8, jnp.pad(loss_p, ((0, 7), (0, 0)))], axis=0)
    g_win, g_wout, gsmall = _gsync(pw_in, pw_out, small)

    g_nw = gsmall[0:8].reshape(1024)
    g_cb = gsmall[8:12].reshape(512)
    folded = _fold_heads(gsmall[12:20])
    g_qw, g_kw = folded[0, 0:64], folded[1, 0:64]
    g_cw = lax.dynamic_slice(gsmall[20:32].reshape(3, 512), (0, blk * 128), (3, 128))
    g_rb = gsmall[32:40][:, 0:32].T
    loss = gsmall[40, 0]

    g_win, d_win, nm_win, nv_win = _adamw(w_in, g_win, m_w_in, v_w_in, "adamw_w_in")
    g_wout, d_wout, nm_wout, nv_wout = _adamw(w_out, g_wout, m_w_out, v_w_out, "adamw_w_out")

    def pack(parts):
        rows = [parts[0].reshape(8, 128), parts[1].reshape(4, 128),
                jnp.pad(parts[2], (0, 64))[None, :], jnp.pad(parts[3], (0, 64))[None, :],
                parts[4], jnp.pad(parts[5].T, ((0, 0), (0, 96)))]
        return jnp.concatenate(rows, axis=0)

    ws = pack([norm_w, conv_b, q_norm_w, k_norm_w, conv_w, rel_bias])
    gs = pack([g_nw, g_cb, g_qw, g_kw, g_cw, g_rb])
    ms = pack([m_norm_w, m_conv_b, m_q_norm_w, m_k_norm_w, m_conv_w, m_rel_bias])
    vs = pack([v_norm_w, v_conv_b, v_q_norm_w, v_k_norm_w, v_conv_w, v_rel_bias])
    rpad = lambda a: jnp.pad(a, ((0, 7), (0, 0)))
    _, d_s, nm_s, nv_s = _adamw(rpad(ws), rpad(gs), rpad(ms), rpad(vs), "adamw_small")

    def unpack(a):
        return (a[0:8].reshape(1024), a[12:13, 0:64].reshape(64), a[13:14, 0:64].reshape(64),
                a[14:17], a[8:12].reshape(512), a[17:25, 0:32].T)

    def ordered(nw, win, cw, cb, qw, kw, rb, wout):
        return (nw, win, cw, cb, qw, kw, rb, wout)

    g_un = (g_nw, g_qw, g_kw, g_cw, g_cb, g_rb)
    outs = [loss, grad_x[None]]
    for un, win_v, wout_v in ((g_un, g_win, g_wout), (unpack(d_s), d_win, d_wout),
                              (unpack(nm_s), nm_win, nm_wout), (unpack(nv_s), nv_win, nv_wout)):
        nw, qw, kw, cw, cb, rb = un
        outs.extend(ordered(nw, win_v, cw, cb, qw, kw, rb, wout_v))
    return tuple(outs)
```

```python
import math

import jax
import jax.numpy as jnp
from jax import lax
from jax.experimental import pallas as pl
from jax.experimental.pallas import tpu as pltpu

F32 = jnp.float32
BF16 = jnp.bfloat16
MESH = pl.DeviceIdType.MESH

D_MODEL = 1024
CONV_W = 512
ATTN_W = 512
HEAD_DIM = 64
N_PAIR = 4
DILATIONS = (1, 4, 16)
HALF = 64
QB = 128
KB = QB + 2 * HALF
NUM_BUCKETS = 32
MAX_DISTANCE = 1024
EPS = 1e-6
NEG = -1e30
ADAM_LR, ADAM_B1, ADAM_B2, ADAM_EPS, ADAM_WD, ADAM_STEP = 0.001, 0.9, 0.999, 1e-08, 0.01, 10
VMEM_LIMIT = 48 << 20


def _params(sem=None, vmem=VMEM_LIMIT, **kw):
    if sem is not None:
        kw["dimension_semantics"] = sem
    return pltpu.CompilerParams(vmem_limit_bytes=vmem, **kw)


def _sigmoid(z):
    return 1.0 / (1.0 + jnp.exp(-z))


def _group_sum(val, b_ref, split=True):
    hi = val.astype(BF16)
    lo = (val - hi.astype(F32)).astype(BF16) if split else None
    outs = []
    for j in range(val.shape[1] // 256):
        sl = slice(256 * j, 256 * j + 256)
        part = jnp.dot(hi[:, sl], b_ref[...], preferred_element_type=F32)
        if split:
            part = part + jnp.dot(lo[:, sl], b_ref[...], preferred_element_type=F32)
        outs.append(part)
    return outs[0] if len(outs) == 1 else jnp.concatenate(outs, axis=1)


def _t5_bucket(rel):
    half_b = NUM_BUCKETS // 2
    max_exact = half_b // 2
    ret = jnp.where(rel > 0, half_b, 0)
    n = jnp.abs(rel)
    nf = jnp.maximum(n, 1).astype(F32)
    large = max_exact + (jnp.log(nf / max_exact) / math.log(MAX_DISTANCE / max_exact)
                         * (half_b - max_exact)).astype(jnp.int32)
    large = jnp.minimum(large, half_b - 1)
    return ret + jnp.where(n < max_exact, n, large)


def _bias_tables(rel_bias):
    rows = []
    key = jnp.arange(KB)
    for dilation in DILATIONS:
        for variant in range(3):
            off = (0, HALF, 2 * HALF)[variant]
            rel = ((key - off + KB // 2) % KB) - KB // 2
            bkt = _t5_bucket(jnp.clip(rel, -HALF, HALF) * dilation)
            rows.append(jnp.where(jnp.abs(rel) <= HALF, bkt, -1))
    bkt_all = jnp.broadcast_to(jnp.stack(rows, axis=0).astype(jnp.int32)[:, None, :], (9, 8, KB))

    def body(rb_ref, bkt_ref, o_ref):
        bkt = bkt_ref[...]
        off = (pl.program_id(0) % 3) * HALF
        rel = (lax.broadcasted_iota(jnp.int32, (QB, KB), 1) - lax.broadcasted_iota(jnp.int32, (QB, KB), 0)) - off
        band = jnp.abs(rel) <= HALF
        for h in range(8):
            acc = jnp.full((8, KB), NEG, F32)
            for b in range(NUM_BUCKETS):
                acc = jnp.where(bkt == b, rb_ref[b, h], acc)
            rolled = pltpu.roll(jnp.broadcast_to(acc[0:1], (QB, KB)), 0, 1, stride=1, stride_axis=0)
            o_ref[h] = jnp.where(band, rolled, NEG)

    out = pl.pallas_call(
        body, name="bias_tables", grid=(9,),
        out_shape=jax.ShapeDtypeStruct((9, 8, QB, KB), F32),
        in_specs=[pl.BlockSpec(memory_space=pltpu.SMEM), pl.BlockSpec((None, 8, KB), lambda i: (i, 0, 0))],
        out_specs=pl.BlockSpec((None, 8, QB, KB), lambda i: (i, 0, 0, 0)),
        compiler_params=_params(("parallel",)),
    )(rel_bias, bkt_all)
    return out.reshape(3, 3, N_PAIR, 2 * QB, KB)


def _diag_bucket_onehot(dilation):
    out = []
    c = jnp.arange(KB)
    for variant in range(3):
        off = (0, HALF, 2 * HALF)[variant]
        rel = ((c - off + 128) % 256) - 128
        band = jnp.abs(rel) <= HALF
        bkt = _t5_bucket(jnp.clip(rel, -HALF, HALF) * dilation)
        oh = (bkt[:, None] == jnp.arange(128)[None, :]) & band[:, None]
        out.append(oh.astype(F32))
    return jnp.stack(out, axis=0)


def _wgather(w_in, w_out, conv_w):
    rin, rout = w_in.shape[0] // 2, w_out.shape[0] // 2

    def body(win_ref, wout_ref, cw_ref, win_o, wout_o, cw_o, send_sems, recv_sems):
        x, y, c = lax.axis_index("x"), lax.axis_index("y"), lax.axis_index("c")
        b = 2 * x + y
        win_o[b] = win_ref[...].astype(BF16)
        wout_o[b] = wout_ref[...].astype(BF16)
        cw_o[b] = cw_ref[...]
        xnbr, ynbr, diag, sib = (1 - x, y, c), (x, 1 - y, c), (1 - x, 1 - y, c), (x, y, 1 - c)
        bx, by, bd = b ^ 2, b ^ 1, b ^ 3

        def copy(sem, ref, to):
            return pltpu.make_async_remote_copy(src_ref=ref, dst_ref=ref, send_sem=send_sems.at[sem],
                                                recv_sem=recv_sems.at[sem], device_id=to, device_id_type=MESH)

        def rows_of(ref, half):
            def rows(blk, quarter=None):
                if quarter is None:
                    return ref.at[blk, pl.ds(c * half, half), :]
                return ref.at[blk, pl.ds(c * half + quarter * (half // 2), half // 2), :]
            return rows

        def send_own(ref, half, base):
            rows = rows_of(ref, half)
            own_x, own_y = copy(base + 0, rows(b), xnbr), copy(base + 1, rows(b), ynbr)
            own_x.start()
            own_y.start()
            return [own_x, own_y]

        def relay(ref, half, base):
            rows = rows_of(ref, half)
            copy(base + 0, rows(bx), xnbr).wait_recv()
            pass_y = copy(base + 2, rows(bx, 0), ynbr)
            pass_y.start()
            to_sib = [copy(base + 4, rows(bx), sib)]
            to_sib[-1].start()
            copy(base + 1, rows(by), ynbr).wait_recv()
            pass_x = copy(base + 3, rows(by, 1), xnbr)
            pass_x.start()
            to_sib.append(copy(base + 5, rows(by), sib))
            to_sib[-1].start()
            copy(base + 2, rows(bd, 0), ynbr).wait_recv()
            copy(base + 3, rows(bd, 1), xnbr).wait_recv()
            to_sib.append(copy(base + 6, rows(bd), sib))
            to_sib[-1].start()
            return [pass_y, pass_x] + to_sib

        def from_sibling(ref, half, base):
            for k, blk in enumerate((bx, by, bd)):
                copy(base + 4 + k, ref.at[blk, pl.ds((1 - c) * half, half), :], sib).wait_recv()

        small = [copy(14 + k, cw_o.at[b], to) for k, to in enumerate((xnbr, ynbr, diag))]
        for cp in small:
            cp.start()
        started = send_own(win_o, rin, 0) + send_own(wout_o, rout, 7)
        started += relay(win_o, rin, 0) + relay(wout_o, rout, 7)
        for k, blk in enumerate((bx, by, bd)):
            copy(14 + k, cw_o.at[blk], sib).wait_recv()
        from_sibling(win_o, rin, 0)
        from_sibling(wout_o, rout, 7)
        for cp in small + started:
            cp.wait_send()

    vm = pl.BlockSpec(memory_space=pltpu.VMEM)
    return pl.pallas_call(
        body, name="wgather",
        out_shape=(jax.ShapeDtypeStruct((4,) + w_in.shape, BF16),
                   jax.ShapeDtypeStruct((4,) + w_out.shape, BF16),
                   jax.ShapeDtypeStruct((4,) + conv_w.shape, F32)),
        in_specs=[vm, vm, vm], out_specs=(vm, vm, vm),
        scratch_shapes=[pltpu.SemaphoreType.DMA((17,)), pltpu.SemaphoreType.DMA((17,))],
        compiler_params=_params(),
    )(w_in, w_out, conv_w)


TM_MATMUL = 512
TM_COMBINE = 256


def _resident(shape):
    return pl.BlockSpec(shape, lambda i: (0,) * len(shape), pipeline_mode=pl.Buffered(1))


def _to_slabs(slab, val, j0=0):
    for j in range(val.shape[1] // 128):
        slab[j0 + j] = val[:, 128 * j:128 * (j + 1)]


def _scatter_classes(slab, j0, nj, out_ref, d, part=0, mid=None):
    tm = slab.shape[1]
    n = tm // d
    if d == 4:
        for r in range(d):
            for j in range(nj):
                out_ref[r, part * n:(part + 1) * n, 128 * j:128 * (j + 1)] = (
                    slab[j0 + j, pl.ds(r, n, stride=d), :].astype(out_ref.dtype))
        return
    q = tm // 4
    for lo in range(4):
        for j in range(nj):
            mid[j0 + j, lo * q:(lo + 1) * q, :] = slab[j0 + j, pl.ds(lo, q, stride=4), :]
    for hi in range(4):
        for lo in range(4):
            for j in range(nj):
                out_ref[4 * hi + lo, part * n:(part + 1) * n, 128 * j:128 * (j + 1)] = (
                    mid[j0 + j, pl.ds(lo * q + hi, n, stride=4), :].astype(out_ref.dtype))


def _gather_classes(slab, piece, nj, d, mid=None):
    tm = slab.shape[1]
    n = tm // d
    if d == 4:
        for r in range(d):
            for j in range(nj):
                slab[j, pl.ds(r, n, stride=d), :] = piece(r, j).astype(F32)
    else:
        q = tm // 4
        for hi in range(4):
            for lo in range(4):
                for j in range(nj):
                    mid[j, pl.ds(lo * q + hi, n, stride=4), :] = piece(4 * hi + lo, j).astype(F32)
        for lo in range(4):
            for j in range(nj):
                slab[j, pl.ds(lo, q, stride=4), :] = mid[j, lo * q:(lo + 1) * q, :]
    return jnp.concatenate([slab[j] for j in range(nj)], axis=1)


def _class_spec(d, width, tm):
    return pl.BlockSpec((d, tm // d, width), lambda i: (0, i, 0))


def _proj(x, tgt, norm_w, wblk, qkw, b256):
    s = x.shape[0]
    tm = TM_MATMUL
    nparts = 2
    tp = tm // nparts

    def body(x_ref, t_ref, nw_ref, w_ref, qkw_ref, b_ref, h_o, cg_o, qkr_o, qkn_o, vz_o, qkn4_o, v4_o, qkn16_o,
             v16_o, xmt_o, slabs, mids):
        for part in range(nparts):
            rows = slice(part * tp, (part + 1) * tp)
            slab = slabs.at[part]
            xf = x_ref[rows, :]
            xmt_o[rows, :] = xf - t_ref[rows, :]
            r = lax.rsqrt(jnp.mean(xf * xf, axis=-1, keepdims=True) + EPS)
            h = (xf * r * nw_ref[...]).astype(BF16)
            h_o[rows, :] = h
            p2 = jnp.dot(h, w_ref[2], preferred_element_type=F32)
            qkr_o[rows, :] = p2.astype(BF16)
            ss = _group_sum(p2 * p2, b_ref, split=False)
            rr = lax.rsqrt(ss * (1.0 / HEAD_DIM) + EPS)
            qkn = p2 * rr * qkw_ref[...]
            qkn_o[rows, :] = qkn.astype(BF16)
            _to_slabs(slab, qkn)
            p3 = jnp.dot(h, w_ref[3], preferred_element_type=F32)
            vz_o[rows, :] = p3.astype(BF16)
            _to_slabs(slab, p3[:, 0:512], 8)
            cg_o[rows, 0:1024] = jnp.dot(h, w_ref[0], preferred_element_type=F32).astype(BF16)
            cg_o[rows, 1024:2048] = jnp.dot(h, w_ref[1], preferred_element_type=F32).astype(BF16)
            for d, q_o, v_o in ((4, qkn4_o, v4_o), (16, qkn16_o, v16_o)):
                _scatter_classes(slab, 0, 8, q_o, d, part, mids.at[part])
                _scatter_classes(slab, 8, 4, v_o, d, part, mids.at[part])

    row = lambda w: pl.BlockSpec((tm, w), lambda i: (i, 0))
    full = lambda shp: pl.BlockSpec(shp, lambda i: (0,) * len(shp))
    nat = lambda w: jax.ShapeDtypeStruct((s, w), BF16)
    cls = lambda d, w: jax.ShapeDtypeStruct((d, s // d, w), BF16)
    return pl.pallas_call(
        body, name="proj", grid=(s // tm,),
        out_shape=(nat(1024), nat(2048), nat(1024), nat(1024), nat(1024),
                   cls(4, 1024), cls(4, 512), cls(16, 1024), cls(16, 512), jax.ShapeDtypeStruct((s, 1024), F32)),
        in_specs=[row(1024), row(1024), full((1, 1024)), _resident((4, 1024, 1024)), full((1, 1024)),
                  full((256, 256))],
        out_specs=(row(1024), row(2048), row(1024), row(1024), row(1024),
                   _class_spec(4, 1024, tm), _class_spec(4, 512, tm),
                   _class_spec(16, 1024, tm), _class_spec(16, 512, tm), row(1024)),
        scratch_shapes=[pltpu.VMEM((nparts, 12, tp, 128), F32), pltpu.VMEM((nparts, 12, tp, 128), F32)],
        compiler_params=_params(("parallel",)),
    )(x, tgt, norm_w, wblk, qkw, b256)


def _block_coords(t, i, nsub, nb, length):
    n = t * nsub + i
    q0 = i * QB
    start = pl.multiple_of(jnp.clip(n * QB - HALF, 0, length - KB), HALF)
    variant = jnp.where(n == 0, 0, jnp.where(n == nb - 1, 2, 1))
    return q0, start, variant


def _classes_per_step(r_cls, length, qt):
    return 2 if (length == qt and qt // QB <= 8 and r_cls % 2 == 0) else 1


def _split_heads(a, lo):
    zero = jnp.zeros_like(a)
    return jnp.concatenate([jnp.where(lo, a, zero), jnp.where(lo, zero, a)], axis=0)


def _col_pair(ref, q0, lane):
    return jnp.concatenate([ref[pl.ds(q0, QB), lane:lane + 1],
                            ref[pl.ds(q0, QB), HEAD_DIM + lane:HEAD_DIM + lane + 1]], axis=0)


def _attn_fwd(qkn_l, v_l, bias, gi, name):
    r_cls, length, _ = qkn_l.shape
    qt = min(length, 2048)
    nb, nsub = length // QB, qt // QB
    cb = _classes_per_step(r_cls, length, qt)

    def body(q_ref, k_ref, v_ref, b_ref, o_ref, lse_ref):
        t = pl.program_id(2)
        lo = lax.broadcasted_iota(jnp.int32, (QB, 128), 1) < HEAD_DIM

        starts, logits = [], []
        for ci in range(cb):
            for i in range(nsub):
                _, start, variant = _block_coords(t, i, nsub, nb, length)
                qq = _split_heads(q_ref[ci, i * QB:(i + 1) * QB, :], lo)
                k = k_ref[ci, pl.ds(start, KB), :]
                logits.append(lax.dot_general(qq, k, (((1,), (1,)), ((), ())), preferred_element_type=F32)
                              + b_ref[variant])
                starts.append(start)
        lg = jnp.concatenate(logits, axis=0)
        m = jnp.max(lg, axis=-1, keepdims=True)
        p = jnp.exp(lg - m)
        pb = p.astype(BF16)
        l = jnp.sum(p, axis=-1, keepdims=True)
        lse = jnp.broadcast_to(m + jnp.log(l), (cb * nsub * 2 * QB, 128))
        inv = 1.0 / l
        for ci in range(cb):
            for i in range(nsub):
                j = ci * nsub + i
                rows = slice(2 * QB * j, 2 * QB * (j + 1))
                v = v_ref[ci, pl.ds(starts[j], KB), :]
                pv = jnp.dot(pb[rows], v, preferred_element_type=F32) * inv[rows]
                o_ref[ci, i * QB:(i + 1) * QB, :] = jnp.where(lo, pv[0:QB], pv[QB:2 * QB]).astype(BF16)
                ls = lse[rows]
                lse_ref[ci, i * QB:(i + 1) * QB, :] = jnp.where(lo, ls[0:QB], ls[QB:2 * QB])

    return pl.pallas_call(
        body, name=name, grid=(N_PAIR, r_cls // cb, length // qt),
        out_shape=(jax.ShapeDtypeStruct((r_cls, length, 512), BF16),
                   jax.ShapeDtypeStruct((r_cls, length, 512), F32)),
        in_specs=[pl.BlockSpec((cb, qt, 128), lambda p, r, t: (r, t, p)),
                  pl.BlockSpec((cb, length, 128), lambda p, r, t: (r, 0, 4 + p)),
                  pl.BlockSpec((cb, length, 128), lambda p, r, t: (r, 0, p)),
                  pl.BlockSpec((None, 3, None, 2 * QB, KB), lambda p, r, t: (gi, 0, p, 0, 0))],
        out_specs=(pl.BlockSpec((cb, qt, 128), lambda p, r, t: (r, t, p)),
                   pl.BlockSpec((cb, qt, 128), lambda p, r, t: (r, t, p))),
        compiler_params=_params(("parallel", "parallel", "arbitrary")),
    )(qkn_l, qkn_l, v_l, bias)


def _combine(o_g, lse_g, cg, vz, xmt, wout, cw, cb, b256):
    s = xmt.shape[0]
    tm = TM_COMBINE
    hb = 16
    nt = s // tm

    def body(o1, o4, o16, l1, l4, l16, cg_ref, cgp_ref, cgn_ref, za_ref, xmt_ref, w_ref, cw_ref, cb_ref,
             b_ref, y_o, dout_o, ld1_o, do1_o, dza_o, dgbz_o, dzc_o, loss_o, dcb_o, dcw_o,
             do4_o, ld4_o, do16_o, ld16_o, slab, mid):
        i = pl.program_id(0)

        @pl.when(i == 0)
        def _():
            loss_o[...] = jnp.zeros_like(loss_o)
            dcb_o[...] = jnp.zeros_like(dcb_o)
            dcw_o[...] = jnp.zeros_like(dcw_o)

        u = cg_ref[:, 0:512].astype(F32)
        gb = cg_ref[:, 512:1024].astype(F32)
        gc = cg_ref[:, 1024:1536].astype(F32)
        zc = cg_ref[:, 1536:2048].astype(F32)
        tt = gc * u
        t_prev = cgp_ref[hb - 1:hb, 0:512].astype(F32) * cgp_ref[hb - 1:hb, 1024:1536].astype(F32)
        t_next = cgn_ref[0:1, 0:512].astype(F32) * cgn_ref[0:1, 1024:1536].astype(F32)
        t_prev = jnp.where(i == 0, 0.0, t_prev)
        t_next = jnp.where(i == nt - 1, 0.0, t_next)
        rows = lax.broadcasted_iota(jnp.int32, (tm, 512), 0)
        t_up = jnp.where(rows == 0, t_prev, pltpu.roll(tt, 1, 0))
        t_dn = jnp.where(rows == tm - 1, t_next, pltpu.roll(tt, tm - 1, 0))
        w0, w1, w2 = cw_ref[0:1, :], cw_ref[1:2, :], cw_ref[2:3, :]
        zb = w0 * t_up + w1 * tt + w2 * t_dn + cb_ref[...]
        sg = _sigmoid(zc)
        sz = zc * sg
        y_conv = gb * zb * sz

        a1, p1 = l1[0], o1[0].astype(F32)
        a4 = _gather_classes(slab, lambda r, j: l4[r, :, 128 * j:128 * (j + 1)], 4, 4)
        p4 = _gather_classes(slab, lambda r, j: o4[r, :, 128 * j:128 * (j + 1)], 4, 4)
        a16 = _gather_classes(slab, lambda r, j: l16[r, :, 128 * j:128 * (j + 1)], 4, 16, mid)
        p16 = _gather_classes(slab, lambda r, j: o16[r, :, 128 * j:128 * (j + 1)], 4, 16, mid)
        m = jnp.maximum(jnp.maximum(a1, a4), a16)
        e1, e4, e16 = jnp.exp(a1 - m), jnp.exp(a4 - m), jnp.exp(a16 - m)
        den = e1 + e4 + e16
        lse = m + jnp.log(den)
        o = (e1 * p1 + e4 * p4 + e16 * p16) / den
        za = za_ref[...].astype(F32)
        sga = _sigmoid(za)
        sa = za * sga
        y = jnp.concatenate([y_conv, o * sa], axis=1).astype(BF16)
        y_o[...] = y

        diff = xmt_ref[...] + jnp.dot(y, w_ref[...], preferred_element_type=F32)
        loss_o[...] += (0.5 / D_MODEL) * jnp.sum(diff * diff)
        dout = diff * (1.0 / D_MODEL)
        dout_o[...] = dout
        dy = lax.dot_general(dout.astype(BF16), w_ref[...], (((1,), (1,)), ((), ())), preferred_element_type=F32)
        dyc, dya = dy[:, 0:512], dy[:, 512:1024]

        do = dya * sa
        dza_o[...] = (dya * o * (sga * (1.0 + za * (1.0 - sga)))).astype(BF16)
        lane = lax.broadcasted_iota(jnp.int32, (tm, 512), 1)
        ld = jnp.where((lane & (HEAD_DIM - 1)) < HEAD_DIM // 2, lse, _group_sum(do * o, b_ref))
        do1_o[0] = do.astype(BF16)
        ld1_o[0] = ld
        _to_slabs(slab, do)
        _scatter_classes(slab, 0, 4, do4_o, 4)
        _scatter_classes(slab, 0, 4, do16_o, 16, 0, mid)
        _to_slabs(slab, ld)
        _scatter_classes(slab, 0, 4, ld4_o, 4)
        _scatter_classes(slab, 0, 4, ld16_o, 16, 0, mid)

        dzc = dyc * sz * gb
        dzc_o[...] = dzc.astype(BF16)
        dgbz_o[:, 0:512] = (dyc * sz * zb).astype(BF16)
        dgbz_o[:, 512:1024] = (dyc * gb * zb * (sg * (1.0 + zc * (1.0 - sg)))).astype(BF16)
        dcb_o[...] += jnp.sum(dzc, axis=0, keepdims=True)
        dcw_o[0:1, :] += jnp.sum(dzc * t_up, axis=0, keepdims=True)
        dcw_o[1:2, :] += jnp.sum(dzc * tt, axis=0, keepdims=True)
        dcw_o[2:3, :] += jnp.sum(dzc * t_dn, axis=0, keepdims=True)

    row = lambda w, j=0: pl.BlockSpec((tm, w), lambda i: (i, j))
    full = lambda shp: pl.BlockSpec(shp, lambda i: (0,) * len(shp))
    prev = pl.BlockSpec((hb, 2048), lambda i: (jnp.maximum(i * (tm // hb) - 1, 0), 0))
    nxt = pl.BlockSpec((hb, 2048), lambda i: (jnp.minimum((i + 1) * (tm // hb), s // hb - 1), 0))
    cls = lambda d, dt: jax.ShapeDtypeStruct((d, s // d, 512), dt)
    cspecs = [_class_spec(d, 512, tm) for d in DILATIONS]
    return pl.pallas_call(
        body, name="combine", grid=(nt,),
        out_shape=(jax.ShapeDtypeStruct((s, 1024), BF16), jax.ShapeDtypeStruct((s, 1024), F32),
                   cls(1, F32), cls(1, BF16), jax.ShapeDtypeStruct((s, 512), BF16),
                   jax.ShapeDtypeStruct((s, 1024), BF16), jax.ShapeDtypeStruct((s, 512), BF16),
                   jax.ShapeDtypeStruct((1, 128), F32), jax.ShapeDtypeStruct((1, 512), F32),
                   jax.ShapeDtypeStruct((8, 512), F32),
                   cls(4, BF16), cls(4, F32), cls(16, BF16), cls(16, F32)),
        in_specs=cspecs + cspecs + [row(2048), prev, nxt, row(512, 1), row(1024),
                                    _resident((1024, 1024)), full((8, 512)), full((1, 512)), full((256, 256))],
        out_specs=(row(1024), row(1024), cspecs[0], cspecs[0], row(512), row(1024), row(512),
                   full((1, 128)), full((1, 512)), full((8, 512)),
                   cspecs[1], cspecs[1], cspecs[2], cspecs[2]),
        scratch_shapes=[pltpu.VMEM((4, tm, 128), F32), pltpu.VMEM((4, tm, 128), F32)],
        compiler_params=_params(("arbitrary",)),
    )(*o_g, *lse_g, cg, cg, cg, vz, xmt, wout, cw, cb, b256)


def _attn_bwd(qkn_l, v_l, do_l, ld_l, bias, gi, name):
    r_cls, length, _ = qkn_l.shape
    qt = min(length, 2048 if length <= 4096 else 1024)
    nb, nsub, nt = length // QB, qt // QB, length // qt
    chunk = min(length, 4096)
    nchunk = length // chunk
    cb = _classes_per_step(r_cls, length, qt)
    nrb = r_cls // cb

    def body(q_ref, k_ref, v_ref, do_ref, ld_ref, b_ref, dq_ref, dkv_hbm, dsum_ref, dk_acc, dv_acc, stage, sems):
        p_id, r, t = pl.program_id(0), pl.program_id(1), pl.program_id(2)
        lo = lax.broadcasted_iota(jnp.int32, (QB, 128), 1) < HEAD_DIM

        @pl.when(t == 0)
        def _():
            dk_acc[...] = jnp.zeros_like(dk_acc)
            dv_acc[...] = jnp.zeros_like(dv_acc)

        @pl.when((t == 0) & (r == 0))
        def _():
            dsum_ref[...] = jnp.zeros_like(dsum_ref)

        nt_dims = (((1,), (1,)), ((), ()))
        tn_dims = (((0,), (0,)), ((), ()))
        coords, qqs, dds, logits, dps, lcols, dcols = [], [], [], [], [], [], []
        for ci in range(cb):
            for i in range(nsub):
                q0, start, variant = _block_coords(t, i, nsub, nb, length)
                qq = _split_heads(q_ref[ci, q0:q0 + QB, :], lo)
                dd = _split_heads(do_ref[ci, q0:q0 + QB, :], lo)
                k = k_ref[ci, pl.ds(start, KB), :]
                v = v_ref[ci, pl.ds(start, KB), :]
                logits.append(lax.dot_general(qq, k, nt_dims, preferred_element_type=F32) + b_ref[variant])
                dps.append(lax.dot_general(dd, v, nt_dims, preferred_element_type=F32))
                lcols.append(_col_pair(ld_ref.at[ci], q0, 0))
                dcols.append(_col_pair(ld_ref.at[ci], q0, HEAD_DIM // 2))
                coords.append((ci, i, q0, start, variant))
                qqs.append(qq)
                dds.append(dd)
        p = jnp.exp(jnp.concatenate(logits, axis=0) - jnp.concatenate(lcols, axis=0))
        ds = p * (jnp.concatenate(dps, axis=0) - jnp.concatenate(dcols, axis=0))
        pb = p.astype(BF16)
        dsb = ds.astype(BF16)
        middle = None
        for j, (ci, i, q0, start, variant) in enumerate(coords):
            rows = slice(2 * QB * j, 2 * QB * (j + 1))
            if 0 < i < nsub - 1:
                middle = ds[rows] if middle is None else middle + ds[rows]
            else:
                dsum_ref[variant] += ds[rows]
            dqq = jnp.dot(dsb[rows], k_ref[ci, pl.ds(start, KB), :], preferred_element_type=F32)
            dq_ref[ci, q0:q0 + QB, :] = jnp.where(lo, dqq[0:QB], dqq[QB:2 * QB]).astype(BF16)
            dk_acc[ci, pl.ds(start, KB), :] += lax.dot_general(dsb[rows], qqs[j], tn_dims,
                                                               preferred_element_type=F32)
            dv_acc[ci, pl.ds(start, KB), :] += lax.dot_general(pb[rows], dds[j], tn_dims,
                                                               preferred_element_type=F32)
        if middle is not None:
            dsum_ref[1] += middle

        @pl.when(t == nt - 1)
        def _():
            def copy(k):
                ci, which, c = k // (2 * nchunk), (k // nchunk) % 2, k % nchunk
                rows = pl.ds(c * chunk, chunk)
                return pltpu.make_async_copy(stage.at[k % 2], dkv_hbm.at[r * cb + ci, p_id, which, rows, :],
                                             sems.at[k % 2])

            for k in range(2 * nchunk * cb):
                if k < 2:
                    @pl.when((p_id > 0) | (r > 0))
                    def _():
                        copy(k).wait()
                else:
                    copy(k).wait()
                acc = (dk_acc, dv_acc)[(k // nchunk) % 2]
                stage[k % 2] = acc[k // (2 * nchunk), pl.ds((k % nchunk) * chunk, chunk), :].astype(BF16)
                copy(k).start()

            @pl.when((p_id == N_PAIR - 1) & (r == nrb - 1))
            def _():
                copy(0).wait()
                copy(1).wait()

    qspec = pl.BlockSpec((cb, qt, 128), lambda p, r, t: (r, t, p))
    return pl.pallas_call(
        body, name=name, grid=(N_PAIR, nrb, nt),
        out_shape=(jax.ShapeDtypeStruct((r_cls, length, 512), BF16),
                   jax.ShapeDtypeStruct((r_cls, N_PAIR, 2, length, 128), BF16),
                   jax.ShapeDtypeStruct((N_PAIR, 3, 2 * QB, KB), F32)),
        in_specs=[qspec,
                  pl.BlockSpec((cb, length, 128), lambda p, r, t: (r, 0, 4 + p)),
                  pl.BlockSpec((cb, length, 128), lambda p, r, t: (r, 0, p)),
                  qspec, qspec,
                  pl.BlockSpec((None, 3, None, 2 * QB, KB), lambda p, r, t: (gi, 0, p, 0, 0))],
        out_specs=(qspec, pl.BlockSpec(memory_space=pl.ANY),
                   pl.BlockSpec((None, 3, 2 * QB, KB), lambda p, r, t: (p, 0, 0, 0))),
        scratch_shapes=[pltpu.VMEM((cb, length, 128), F32), pltpu.VMEM((cb, length, 128), F32),
                        pltpu.VMEM((2, chunk, 128), BF16), pltpu.SemaphoreType.DMA((2,))],
        compiler_params=_params(("arbitrary", "arbitrary", "arbitrary")),
    )(qkn_l, qkn_l, v_l, do_l, ld_l, bias)


def _bwd_tail(dq_g, dkv_g, qkr, qkw, dza, dgbz, dzc, cg, cw, wblk, x, norm_w, dout, b256):
    s = x.shape[0]
    tm = TM_COMBINE
    hb = 16
    nt = s // tm

    def body(dq1, dq4, dq16, dkv1, dkv4, dkv16, qkr_ref, qkw_ref, dza_ref, dgbz_ref, dzc_ref,
             dzp_ref, dzn_ref, u_ref, gc_ref, cw_ref, w_ref, x_ref, nw_ref, dout_ref, b_ref,
             gx_o, dproj_o, dnw_o, dqkw_o, slab, mid):
        i = pl.program_id(0)

        def nat_q(ref, d):
            return _gather_classes(slab, lambda r, j: ref[r, :, 128 * j:128 * (j + 1)], 4, d, mid)

        def nat_kv(ref, d, which):
            return _gather_classes(slab, lambda r, j: ref[r, j, which], 4, d, mid)

        @pl.when(i == 0)
        def _():
            dnw_o[...] = jnp.zeros_like(dnw_o)
            dqkw_o[...] = jnp.zeros_like(dqkw_o)

        dzc = dzc_ref[...].astype(F32)
        d_prev = jnp.where(i == 0, 0.0, dzp_ref[hb - 1:hb, :].astype(F32))
        d_next = jnp.where(i == nt - 1, 0.0, dzn_ref[0:1, :].astype(F32))
        rows = lax.broadcasted_iota(jnp.int32, (tm, 512), 0)
        d_up = jnp.where(rows == 0, d_prev, pltpu.roll(dzc, 1, 0))
        d_dn = jnp.where(rows == tm - 1, d_next, pltpu.roll(dzc, tm - 1, 0))
        dt = cw_ref[0:1, :] * d_dn + cw_ref[1:2, :] * dzc + cw_ref[2:3, :] * d_up
        u = u_ref[...].astype(F32)
        gc = gc_ref[...].astype(F32)
        dproj_o[:, 0:512] = (dt * gc).astype(BF16)
        dproj_o[:, 512:1024] = dgbz_ref[:, 0:512]
        dproj_o[:, 1024:1536] = (dt * u).astype(BF16)
        dproj_o[:, 1536:2048] = dgbz_ref[:, 512:1024]

        dqn = (dq1[0].astype(F32) + nat_q(dq4, 4) + nat_q(dq16, 16)) * (1.0 / 8.0)
        dk1 = jnp.concatenate([dkv1[0, j, 0] for j in range(N_PAIR)], axis=1)
        dv1 = jnp.concatenate([dkv1[0, j, 1] for j in range(N_PAIR)], axis=1)
        dkn = dk1 + nat_kv(dkv4, 4, 0) + nat_kv(dkv16, 16, 0)
        dvn = dv1 + nat_kv(dkv4, 4, 1) + nat_kv(dkv16, 16, 1)
        g = jnp.concatenate([dqn, dkn], axis=1) * qkw_ref[...]
        raw = qkr_ref[...].astype(F32)
        rr = lax.rsqrt(_group_sum(raw * raw, b_ref, split=False) * (1.0 / HEAD_DIM) + EPS)
        proj_gq = _group_sum(g * raw, b_ref) * (1.0 / HEAD_DIM)
        draw = rr * g - raw * (rr * rr * rr) * proj_gq
        dqkw_o[...] += jnp.sum(jnp.concatenate([dqn, dkn], axis=1) * raw * rr, axis=0, keepdims=True)
        dproj_o[:, 2048:3072] = draw.astype(BF16)
        dproj_o[:, 3072:3584] = dvn.astype(BF16)
        dproj_o[:, 3584:4096] = dza_ref[...]

        nt_dims = (((1,), (1,)), ((), ()))
        dh = lax.dot_general(dproj_o[:, 0:1024], w_ref[0], nt_dims, preferred_element_type=F32)
        for b in range(1, 4):
            dh += lax.dot_general(dproj_o[:, 1024 * b:1024 * b + 1024], w_ref[b], nt_dims,
                                  preferred_element_type=F32)

        xf = x_ref[...]
        r = lax.rsqrt(jnp.mean(xf * xf, axis=-1, keepdims=True) + EPS)
        gh = dh * nw_ref[...]
        dnw_o[...] += jnp.sum(dh * xf * r, axis=0, keepdims=True)
        mean_gx = jnp.mean(gh * xf, axis=-1, keepdims=True)
        gx_o[...] = dout_ref[...] + r * gh - xf * (r * r * r) * mean_gx

    row = lambda w, j=0: pl.BlockSpec((tm, w), lambda i: (i, j))
    full = lambda shp: pl.BlockSpec(shp, lambda i: (0,) * len(shp))
    prev = pl.BlockSpec((hb, 512), lambda i: (jnp.maximum(i * (tm // hb) - 1, 0), 0))
    nxt = pl.BlockSpec((hb, 512), lambda i: (jnp.minimum((i + 1) * (tm // hb), s // hb - 1), 0))
    return pl.pallas_call(
        body, name="bwd_tail", grid=(nt,),
        out_shape=(jax.ShapeDtypeStruct((s, 1024), F32), jax.ShapeDtypeStruct((s, 4096), BF16),
                   jax.ShapeDtypeStruct((1, 1024), F32), jax.ShapeDtypeStruct((1, 1024), F32)),
        in_specs=[_class_spec(d, 512, tm) for d in DILATIONS]
        + [pl.BlockSpec((d, N_PAIR, 2, tm // d, 128), lambda i: (0, 0, 0, i, 0)) for d in DILATIONS]
        + [row(1024), full((1, 1024)), row(512), row(1024), row(512), prev, nxt,
           row(512, 0), row(512, 2), full((8, 512)), _resident((4, 1024, 1024)), row(1024),
           full((1, 1024)), row(1024), full((256, 256))],
        out_specs=(row(1024), row(4096), full((1, 1024)), full((1, 1024))),
        scratch_shapes=[pltpu.VMEM((4, tm, 128), F32), pltpu.VMEM((4, tm, 128), F32)],
        compiler_params=_params(("arbitrary",)),
    )(*dq_g, *dkv_g, qkr, qkw, dza, dgbz, dzc, dzc, dzc, cg, cg, cw, wblk, x, norm_w, dout, b256)


def _wgrad(a, b, row_blocked, name):
    s, m = a.shape
    n = b.shape[1]
    tk = 2048
    ncol = min(n, 2048)
    nj, nk = n // ncol, s // tk

    def body(a_ref, b_ref, o_ref, acc):
        kk = pl.program_id(1)

        @pl.when(kk == 0)
        def _():
            acc[...] = jnp.zeros_like(acc)

        acc[...] += lax.dot_general(a_ref[...], b_ref[...].astype(BF16), (((0,), (0,)), ((), ())),
                                    preferred_element_type=F32)

        @pl.when(kk == nk - 1)
        def _():
            blocks, _, rows, _ = o_ref.shape
            for blk in range(blocks):
                for half in range(2):
                    if row_blocked:
                        r0 = (2 * blk + half) * rows
                        o_ref[blk, half] = acc[r0:r0 + rows, :].astype(BF16)
                    else:
                        o_ref[blk, half] = acc[half * rows:(half + 1) * rows,
                                               1024 * blk:1024 * (blk + 1)].astype(BF16)

    if row_blocked:
        out_shape = jax.ShapeDtypeStruct((4, 2, m // 8, 1024), BF16)
        out_spec = pl.BlockSpec((4, 2, m // 8, 1024), lambda j, k: (0, 0, 0, 0))
    else:
        out_shape = jax.ShapeDtypeStruct((n // 1024, 2, m // 2, 1024), BF16)
        out_spec = pl.BlockSpec((ncol // 1024, 2, m // 2, 1024), lambda j, k: (j, 0, 0, 0))
    return pl.pallas_call(
        body, name=name, grid=(nj, nk),
        out_shape=out_shape,
        in_specs=[pl.BlockSpec((tk, m), lambda j, k: (k, 0)), pl.BlockSpec((tk, ncol), lambda j, k: (k, j))],
        out_specs=out_spec,
        scratch_shapes=[pltpu.VMEM((m, ncol), F32)],
        compiler_params=_params(("parallel", "arbitrary")),
    )(a, b)


def _dbias(dsums, onehot_all):
    def body(ds1_ref, ds4_ref, ds16_ref, oh_ref, o_ref):
        @pl.when(pl.program_id(0) == 0)
        def _():
            o_ref[...] = jnp.zeros_like(o_ref)

        hrow = lax.broadcasted_iota(jnp.int32, (8, KB), 0)
        flip = (lax.broadcasted_iota(jnp.int32, (QB, QB), 0)
                + lax.broadcasted_iota(jnp.int32, (QB, QB), 1) == QB - 1).astype(F32)

        def diagonal_sums(tile):
            rev = jnp.dot(flip, tile, preferred_element_type=F32, precision=lax.Precision.HIGHEST)
            sums = jnp.sum(pltpu.roll(rev, 0, 1, stride=1, stride_axis=0), axis=0, keepdims=True)
            return pltpu.roll(sums, KB - (QB - 1), 1)

        for g, ds_ref in enumerate((ds1_ref, ds4_ref, ds16_ref)):
            diag = jnp.zeros((8, KB), F32)
            for p in range(N_PAIR):
                diag = jnp.where(hrow == 2 * p, diagonal_sums(ds_ref[p, 0:QB, :]), diag)
                diag = jnp.where(hrow == 2 * p + 1, diagonal_sums(ds_ref[p, QB:2 * QB, :]), diag)
            o_ref[...] += jnp.dot(diag, oh_ref[g], preferred_element_type=F32, precision=lax.Precision.HIGHEST)

    ds_spec = pl.BlockSpec((N_PAIR, None, 2 * QB, KB), lambda v: (0, v, 0, 0))
    return pl.pallas_call(
        body, name="dbias", grid=(3,),
        out_shape=jax.ShapeDtypeStruct((8, 128), F32),
        in_specs=[ds_spec, ds_spec, ds_spec, pl.BlockSpec((3, None, KB, 128), lambda v: (0, v, 0, 0))],
        out_specs=pl.BlockSpec((8, 128), lambda v: (0, 0)),
        compiler_params=_params(("arbitrary",)),
    )(*dsums, onehot_all)


def _gsync(pw_in, pw_out, small):
    hin, hout = pw_in.shape[2], pw_out.shape[2]
    nsmall = small.shape[0]

    def body(pin_hbm, pout_hbm, small_ref, gin_o, gout_o, small_o,
             mine_in, recv_in, s1_in, r1_in, s2_in, r2_in, mine_out, recv_out, s1_out, r1_out, s2_out, r2_out, gather,
             lsem, asend, arecv, bsend, brecv, csend, crecv, ssend, srecv):
        x, y, c = lax.axis_index("x"), lax.axis_index("y"), lax.axis_index("c")
        b = 2 * x + y
        dev = 4 * x + 2 * y + c
        sib = (x, y, 1 - c)
        xnbr, ynbr = (1 - x, y, c), (x, 1 - y, c)
        bx, by, bd = b ^ 2, b ^ 1, b ^ 3

        def rcopy(src, dst, ssem, rsem, to):
            return pltpu.make_async_remote_copy(src_ref=src, dst_ref=dst, send_sem=ssem, recv_sem=rsem,
                                                device_id=to, device_id_type=MESH)

        gather[dev] = small_ref[...]
        s_sends = []
        for k in range(1, 8):
            to = (x ^ (k >> 2), y ^ ((k >> 1) & 1), c ^ (k & 1))
            cp = rcopy(gather.at[dev], gather.at[dev], ssend.at[k - 1], srecv.at[k - 1], to)
            cp.start()
            s_sends.append(cp)

        a_in = rcopy(pin_hbm.at[:, 1 - c], recv_in, asend.at[0], arecv.at[0], sib)
        a_out = rcopy(pout_hbm.at[:, 1 - c], recv_out, asend.at[1], arecv.at[1], sib)
        a_in.start()
        a_out.start()
        l_in = pltpu.make_async_copy(pin_hbm.at[:, c], mine_in, lsem.at[0])
        l_out = pltpu.make_async_copy(pout_hbm.at[:, c], mine_out, lsem.at[1])
        l_in.start()
        l_out.start()
        l_in.wait()
        l_out.wait()

        def phase_one(a_cp, mine, recv, s1, r1, half, base):
            a_cp.wait_recv()
            q = half // 2
            sends = []
            for part, (peer, blk_peer) in enumerate(((xnbr, bx), (ynbr, by))):
                rows = pl.ds(part * q, q)
                for slot, blk in enumerate((blk_peer, bd)):
                    s1[part, slot] = (mine[blk, rows, :].astype(F32) + recv[blk, rows, :].astype(F32)).astype(BF16)
                cp = rcopy(s1.at[part], r1.at[part], bsend.at[base + part], brecv.at[base + part], peer)
                cp.start()
                sends.append(cp)
            return sends

        def phase_two(p1, mine, recv, r1, s2, r2, half, base):
            q = half // 2
            own, sends = [], []
            for part, (peer, blk_next) in enumerate(((ynbr, by), (xnbr, bx))):
                rows = pl.ds(part * q, q)
                p1[part].wait_recv()
                own.append(mine[b, rows, :].astype(F32) + recv[b, rows, :].astype(F32) + r1[part, 0].astype(F32))
                s2[part] = (mine[blk_next, rows, :].astype(F32) + recv[blk_next, rows, :].astype(F32)
                            + r1[part, 1].astype(F32)).astype(BF16)
                cp = rcopy(s2.at[part], r2.at[part], bsend.at[base + 2 + part], brecv.at[base + 2 + part], peer)
                cp.start()
                sends.append(cp)
            return own, sends

        def stage_c(own, p2, r2, g_o, half, idx):
            q = half // 2
            for part in range(2):
                p2[part].wait_recv()
                g_o[pl.ds(pl.multiple_of(c * half + part * q, q), q), :] = own[part] + r2[part].astype(F32)
            rows = g_o.at[pl.ds(pl.multiple_of(c * half, half), half), :]
            cp = rcopy(rows, rows, csend.at[idx], crecv.at[idx], sib)
            cp.start()
            return cp

        p1_in = phase_one(a_in, mine_in, recv_in, s1_in, r1_in, hin, 0)
        p1_out = phase_one(a_out, mine_out, recv_out, s1_out, r1_out, hout, 4)
        own_in, p2_in = phase_two(p1_in, mine_in, recv_in, r1_in, s2_in, r2_in, hin, 0)
        own_out, p2_out = phase_two(p1_out, mine_out, recv_out, r1_out, s2_out, r2_out, hout, 4)
        c_in = stage_c(own_in, p2_in, r2_in, gin_o, hin, 0)
        c_out = stage_c(own_out, p2_out, r2_out, gout_o, hout, 1)
        b_in, b_out = p1_in + p2_in, p1_out + p2_out

        for cp in s_sends:
            cp.wait_recv()
        tot = gather[0]
        for d in range(1, 8):
            tot = tot + gather[d]
        small_o[...] = tot

        for g_o, half, idx in ((gin_o, hin, 0), (gout_o, hout, 1)):
            other = g_o.at[pl.ds(pl.multiple_of((1 - c) * half, half), half), :]
            rcopy(other, other, csend.at[idx], crecv.at[idx], sib).wait_recv()
        for cp in s_sends + [a_in, a_out] + b_in + b_out + [c_in, c_out]:
            cp.wait_send()

    vm = pl.BlockSpec(memory_space=pltpu.VMEM)
    hbm = pl.BlockSpec(memory_space=pl.ANY)
    return pl.pallas_call(
        body, name="gsync",
        out_shape=(jax.ShapeDtypeStruct((2 * hin, 1024), F32), jax.ShapeDtypeStruct((2 * hout, 1024), F32),
                   jax.ShapeDtypeStruct((nsmall, 128), F32)),
        in_specs=[hbm, hbm, vm], out_specs=(vm, vm, vm),
        scratch_shapes=[pltpu.VMEM((4, hin, 1024), BF16), pltpu.VMEM((4, hin, 1024), BF16),
                        pltpu.VMEM((2, 2, hin // 2, 1024), BF16), pltpu.VMEM((2, 2, hin // 2, 1024), BF16),
                        pltpu.VMEM((2, hin // 2, 1024), BF16), pltpu.VMEM((2, hin // 2, 1024), BF16),
                        pltpu.VMEM((4, hout, 1024), BF16), pltpu.VMEM((4, hout, 1024), BF16),
                        pltpu.VMEM((2, 2, hout // 2, 1024), BF16), pltpu.VMEM((2, 2, hout // 2, 1024), BF16),
                        pltpu.VMEM((2, hout // 2, 1024), BF16), pltpu.VMEM((2, hout // 2, 1024), BF16),
                        pltpu.VMEM((8, nsmall, 128), F32),
                        pltpu.SemaphoreType.DMA((2,)),
                        pltpu.SemaphoreType.DMA((2,)), pltpu.SemaphoreType.DMA((2,)),
                        pltpu.SemaphoreType.DMA((8,)), pltpu.SemaphoreType.DMA((8,)),
                        pltpu.SemaphoreType.DMA((2,)), pltpu.SemaphoreType.DMA((2,)),
                        pltpu.SemaphoreType.DMA((7,)), pltpu.SemaphoreType.DMA((7,))],
        compiler_params=_params(),
    )(pw_in, pw_out, small)


def _adamw_math(w, g, m, v):
    m = ADAM_B1 * m + (1.0 - ADAM_B1) * g
    v = ADAM_B2 * v + (1.0 - ADAM_B2) * (g * g)
    m_hat = m / (1.0 - ADAM_B1 ** ADAM_STEP)
    v_hat = v / (1.0 - ADAM_B2 ** ADAM_STEP)
    delta = -ADAM_LR * (m_hat / (jnp.sqrt(v_hat) + ADAM_EPS) + ADAM_WD * w)
    return delta, m, v


def _adamw(w, g, m, v, name):
    rows, cols = w.shape
    tr = 256 if rows % 256 == 0 else rows

    def body(w_ref, g_ref, m_ref, v_ref, g_o, d_o, m_o, v_o):
        g = g_ref[...]
        d, m2, v2 = _adamw_math(w_ref[...], g, m_ref[...], v_ref[...])
        g_o[...] = g
        d_o[...] = d
        m_o[...] = m2
        v_o[...] = v2

    spec = pl.BlockSpec((tr, cols), lambda i: (i, 0))
    shp = jax.ShapeDtypeStruct((rows, cols), F32)
    return pl.pallas_call(
        body, name=name, grid=(rows // tr,), out_shape=(shp, shp, shp, shp),
        in_specs=[spec] * 4, out_specs=(spec, spec, spec, spec),
        compiler_params=_params(("parallel",)),
    )(w, g, m, v)


def _fold_heads(dqkw):
    def body(x_ref, o_ref):
        xs = x_ref[...]
        sq = xs[0:1] + xs[1:2] + xs[2:3] + xs[3:4]
        sk = xs[4:5] + xs[5:6] + xs[6:7] + xs[7:8]
        both = jnp.concatenate([sq, sk], axis=0)
        o_ref[...] = both + pltpu.roll(both, HEAD_DIM, 1)

    vm = pl.BlockSpec(memory_space=pltpu.VMEM)
    return pl.pallas_call(body, name="fold_heads", out_shape=jax.ShapeDtypeStruct((2, 128), F32),
                          in_specs=[vm], out_specs=vm, compiler_params=_params())(dqkw)


def kernel(x, norm_w, w_in, conv_w, conv_b, q_norm_w, k_norm_w, rel_bias, w_out, loss_target, m_norm_w, m_w_in, m_conv_w, m_conv_b, m_q_norm_w, m_k_norm_w, m_rel_bias, m_w_out, v_norm_w, v_w_in, v_conv_w, v_conv_b, v_q_norm_w, v_k_norm_w, v_rel_bias, v_w_out):
    x2 = x[0]
    tgt = loss_target[0]
    blk = 2 * lax.axis_index("x") + lax.axis_index("y")

    conv_w8 = jnp.pad(conv_w, ((0, 5), (0, 0)))
    wblk, woutblk, cwblk = _wgather(w_in, w_out, conv_w8)
    wout_full = woutblk.reshape(1024, 1024)
    cw_full = cwblk.transpose(1, 0, 2).reshape(8, 512)

    qkw = jnp.concatenate([jnp.tile(q_norm_w, 8) * 0.125, jnp.tile(k_norm_w, 8)])[None, :]
    qkw_raw = jnp.concatenate([jnp.tile(q_norm_w, 8), jnp.tile(k_norm_w, 8)])[None, :]
    gidx = jnp.arange(256) // HEAD_DIM
    b256 = (gidx[:, None] == gidx[None, :]).astype(BF16)

    h, cg, qkr, qkn, vz, qkn4, v4, qkn16, v16, xmt = _proj(x2, tgt, norm_w[None, :], wblk, qkw, b256)

    biases = _bias_tables(rel_bias)
    qkn_l = [qkn[None], qkn4, qkn16]
    v_l = [vz[None], v4, v16]
    o_g, lse_g = [], []
    for gi, d in enumerate(DILATIONS):
        o_l, lse_l = _attn_fwd(qkn_l[gi], v_l[gi], biases, gi, f"attn_fwd_d{d}")
        o_g.append(o_l)
        lse_g.append(lse_l)

    (y, dout, ld1, do1, dza, dgbz, dzc, loss_p, dcb, dcw, do4, ld4, do16, ld16) = _combine(
        o_g, lse_g, cg, vz, xmt, wout_full, cw_full, conv_b[None, :], b256)

    dq_g, dkv_g, dsums = [], [], []
    for gi, (d, do_l, ld_l) in enumerate(zip(DILATIONS, (do1, do4, do16), (ld1, ld4, ld16))):
        dq_l, dkv_l, dsum = _attn_bwd(qkn_l[gi], v_l[gi], do_l, ld_l, biases, gi, f"attn_bwd_d{d}")
        dq_g.append(dq_l)
        dkv_g.append(dkv_l)
        dsums.append(dsum)

    grad_x, dproj, dnw, dqkw = _bwd_tail(dq_g, dkv_g, qkr, qkw_raw, dza, dgbz, dzc, cg, cw_full, wblk,
                                         x2, norm_w[None, :], dout, b256)

    pw_in = _wgrad(h, dproj, False, "wgrad_in")
    pw_out = _wgrad(y, dout, True, "wgrad_out")
    dbias8 = _dbias(dsums, jnp.stack([_diag_bucket_onehot(d) for d in DILATIONS], axis=0))

    small = jnp.concatenate([dnw.reshape(8, 128), dcb.reshape(4, 128), dqkw.reshape(8, 128),
                             dcw[0:3].reshape(12, 128), dbias8, jnp.pad(loss_p, ((0, 7), (0, 0)))], axis=0)
    g_win, g_wout, gsmall = _gsync(pw_in, pw_out, small)

    g_nw = gsmall[0:8].reshape(1024)
    g_cb = gsmall[8:12].reshape(512)
    folded = _fold_heads(gsmall[12:20])
    g_qw, g_kw = folded[0, 0:64], folded[1, 0:64]
    g_cw = lax.dynamic_slice(gsmall[20:32].reshape(3, 512), (0, blk * 128), (3, 128))
    g_rb = gsmall[32:40][:, 0:32].T
    loss = gsmall[40, 0]

    g_win, d_win, nm_win, nv_win = _adamw(w_in, g_win, m_w_in, v_w_in, "adamw_w_in")
    g_wout, d_wout, nm_wout, nv_wout = _adamw(w_out, g_wout, m_w_out, v_w_out, "adamw_w_out")

    def pack(parts):
        rows = [parts[0].reshape(8, 128), parts[1].reshape(4, 128),
                jnp.pad(parts[2], (0, 64))[None, :], jnp.pad(parts[3], (0, 64))[None, :],
                parts[4], jnp.pad(parts[5].T, ((0, 0), (0, 96)))]
        return jnp.concatenate(rows, axis=0)

    ws = pack([norm_w, conv_b, q_norm_w, k_norm_w, conv_w, rel_bias])
    gs = pack([g_nw, g_cb, g_qw, g_kw, g_cw, g_rb])
    ms = pack([m_norm_w, m_conv_b, m_q_norm_w, m_k_norm_w, m_conv_w, m_rel_bias])
    vs = pack([v_norm_w, v_conv_b, v_q_norm_w, v_k_norm_w, v_conv_w, v_rel_bias])
    rpad = lambda a: jnp.pad(a, ((0, 7), (0, 0)))
    _, d_s, nm_s, nv_s = _adamw(rpad(ws), rpad(gs), rpad(ms), rpad(vs), "adamw_small")

    def unpack(a):
        return (a[0:8].reshape(1024), a[12:13, 0:64].reshape(64), a[13:14, 0:64].reshape(64),
                a[14:17], a[8:12].reshape(512), a[17:25, 0:32].T)

    def ordered(nw, win, cw, cb, qw, kw, rb, wout):
        return (nw, win, cw, cb, qw, kw, rb, wout)

    g_un = (g_nw, g_qw, g_kw, g_cw, g_cb, g_rb)
    outs = [loss, grad_x[None]]
    for un, win_v, wout_v in ((g_un, g_win, g_wout), (unpack(d_s), d_win, d_wout),
                              (unpack(nm_s), nm_win, nm_wout), (unpack(nv_s), nv_win, nv_wout)):
        nw, qw, kw, cw, cb, rb = un
        outs.extend(ordered(nw, win_v, cw, cb, qw, kw, rb, wout_v))
    return tuple(outs)
```

```python
import math

import jax
import jax.numpy as jnp
from jax import lax
from jax.experimental import pallas as pl
from jax.experimental.pallas import tpu as pltpu

F32 = jnp.float32
BF16 = jnp.bfloat16
MESH = pl.DeviceIdType.MESH

D_MODEL = 1024
CONV_W = 512
ATTN_W = 512
HEAD_DIM = 64
N_PAIR = 4
DILATIONS = (1, 4, 16)
HALF = 64
QB = 128
KB = QB + 2 * HALF
NUM_BUCKETS = 32
MAX_DISTANCE = 1024
EPS = 1e-6
NEG = -1e30
ADAM_LR, ADAM_B1, ADAM_B2, ADAM_EPS, ADAM_WD, ADAM_STEP = 0.001, 0.9, 0.999, 1e-08, 0.01, 10
VMEM_LIMIT = 48 << 20


def _params(sem=None, vmem=VMEM_LIMIT, **kw):
    if sem is not None:
        kw["dimension_semantics"] = sem
    return pltpu.CompilerParams(vmem_limit_bytes=vmem, **kw)


def _sigmoid(z):
    return 1.0 / (1.0 + jnp.exp(-z))


def _group_sum(val, b_ref, split=True):
    hi = val.astype(BF16)
    lo = (val - hi.astype(F32)).astype(BF16) if split else None
    outs = []
    for j in range(val.shape[1] // 256):
        sl = slice(256 * j, 256 * j + 256)
        part = jnp.dot(hi[:, sl], b_ref[...], preferred_element_type=F32)
        if split:
            part = part + jnp.dot(lo[:, sl], b_ref[...], preferred_element_type=F32)
        outs.append(part)
    return outs[0] if len(outs) == 1 else jnp.concatenate(outs, axis=1)


def _t5_bucket(rel):
    half_b = NUM_BUCKETS // 2
    max_exact = half_b // 2
    ret = jnp.where(rel > 0, half_b, 0)
    n = jnp.abs(rel)
    nf = jnp.maximum(n, 1).astype(F32)
    large = max_exact + (jnp.log(nf / max_exact) / math.log(MAX_DISTANCE / max_exact)
                         * (half_b - max_exact)).astype(jnp.int32)
    large = jnp.minimum(large, half_b - 1)
    return ret + jnp.where(n < max_exact, n, large)


def _bias_tables(rel_bias):
    rows = []
    key = jnp.arange(KB)
    for dilation in DILATIONS:
        for variant in range(3):
            off = (0, HALF, 2 * HALF)[variant]
            rel = ((key - off + KB // 2) % KB) - KB // 2
            bkt = _t5_bucket(jnp.clip(rel, -HALF, HALF) * dilation)
            rows.append(jnp.where(jnp.abs(rel) <= HALF, bkt, -1))
    bkt_all = jnp.broadcast_to(jnp.stack(rows, axis=0).astype(jnp.int32)[:, None, :], (9, 8, KB))

    def body(rb_ref, bkt_ref, o_ref):
        bkt = bkt_ref[...]
        off = (pl.program_id(0) % 3) * HALF
        rel = (lax.broadcasted_iota(jnp.int32, (QB, KB), 1) - lax.broadcasted_iota(jnp.int32, (QB, KB), 0)) - off
        band = jnp.abs(rel) <= HALF
        for h in range(8):
            acc = jnp.full((8, KB), NEG, F32)
            for b in range(NUM_BUCKETS):
                acc = jnp.where(bkt == b, rb_ref[b, h], acc)
            rolled = pltpu.roll(jnp.broadcast_to(acc[0:1], (QB, KB)), 0, 1, stride=1, stride_axis=0)
            o_ref[h] = jnp.where(band, rolled, NEG)

    out = pl.pallas_call(
        body, name="bias_tables", grid=(9,),
        out_shape=jax.ShapeDtypeStruct((9, 8, QB, KB), F32),
        in_specs=[pl.BlockSpec(memory_space=pltpu.SMEM), pl.BlockSpec((None, 8, KB), lambda i: (i, 0, 0))],
        out_specs=pl.BlockSpec((None, 8, QB, KB), lambda i: (i, 0, 0, 0)),
        compiler_params=_params(("parallel",)),
    )(rel_bias, bkt_all)
    return out.reshape(3, 3, N_PAIR, 2 * QB, KB)


def _diag_bucket_onehot(dilation):
    out = []
    c = jnp.arange(KB)
    for variant in range(3):
        off = (0, HALF, 2 * HALF)[variant]
        rel = ((c - off + 128) % 256) - 128
        band = jnp.abs(rel) <= HALF
        bkt = _t5_bucket(jnp.clip(rel, -HALF, HALF) * dilation)
        oh = (bkt[:, None] == jnp.arange(128)[None, :]) & band[:, None]
        out.append(oh.astype(F32))
    return jnp.stack(out, axis=0)


def _wgather(w_in, w_out, conv_w):
    rin, rout = w_in.shape[0] // 2, w_out.shape[0] // 2

    def body(win_ref, wout_ref, cw_ref, win_o, wout_o, cw_o, send_sems, recv_sems):
        x, y, c = lax.axis_index("x"), lax.axis_index("y"), lax.axis_index("c")
        b = 2 * x + y
        win_o[b] = win_ref[...].astype(BF16)
        wout_o[b] = wout_ref[...].astype(BF16)
        cw_o[b] = cw_ref[...]
        xnbr, ynbr, diag, sib = (1 - x, y, c), (x, 1 - y, c), (1 - x, 1 - y, c), (x, y, 1 - c)
        bx, by, bd = b ^ 2, b ^ 1, b ^ 3

        def copy(sem, ref, to):
            return pltpu.make_async_remote_copy(src_ref=ref, dst_ref=ref, send_sem=send_sems.at[sem],
                                                recv_sem=recv_sems.at[sem], device_id=to, device_id_type=MESH)

        def rows_of(ref, half):
            def rows(blk, quarter=None):
                if quarter is None:
                    return ref.at[blk, pl.ds(c * half, half), :]
                return ref.at[blk, pl.ds(c * half + quarter * (half // 2), half // 2), :]
            return rows

        def send_own(ref, half, base):
            rows = rows_of(ref, half)
            own_x, own_y = copy(base + 0, rows(b), xnbr), copy(base + 1, rows(b), ynbr)
            own_x.start()
            own_y.start()
            return [own_x, own_y]

        def relay(ref, half, base):
            rows = rows_of(ref, half)
            copy(base + 0, rows(bx), xnbr).wait_recv()
            pass_y = copy(base + 2, rows(bx, 0), ynbr)
            pass_y.start()
            to_sib = [copy(base + 4, rows(bx), sib)]
            to_sib[-1].start()
            copy(base + 1, rows(by), ynbr).wait_recv()
            pass_x = copy(base + 3, rows(by, 1), xnbr)
            pass_x.start()
            to_sib.append(copy(base + 5, rows(by), sib))
            to_sib[-1].start()
            copy(base + 2, rows(bd, 0), ynbr).wait_recv()
            copy(base + 3, rows(bd, 1), xnbr).wait_recv()
            to_sib.append(copy(base + 6, rows(bd), sib))
            to_sib[-1].start()
            return [pass_y, pass_x] + to_sib

        def from_sibling(ref, half, base):
            for k, blk in enumerate((bx, by, bd)):
                copy(base + 4 + k, ref.at[blk, pl.ds((1 - c) * half, half), :], sib).wait_recv()

        small = [copy(14 + k, cw_o.at[b], to) for k, to in enumerate((xnbr, ynbr, diag))]
        for cp in small:
            cp.start()
        started = send_own(win_o, rin, 0) + send_own(wout_o, rout, 7)
        started += relay(win_o, rin, 0) + relay(wout_o, rout, 7)
        for k, blk in enumerate((bx, by, bd)):
            copy(14 + k, cw_o.at[blk], sib).wait_recv()
        from_sibling(win_o, rin, 0)
        from_sibling(wout_o, rout, 7)
        for cp in small + started:
            cp.wait_send()

    vm = pl.BlockSpec(memory_space=pltpu.VMEM)
    return pl.pallas_call(
        body, name="wgather",
        out_shape=(jax.ShapeDtypeStruct((4,) + w_in.shape, BF16),
                   jax.ShapeDtypeStruct((4,) + w_out.shape, BF16),
                   jax.ShapeDtypeStruct((4,) + conv_w.shape, F32)),
        in_specs=[vm, vm, vm], out_specs=(vm, vm, vm),
        scratch_shapes=[pltpu.SemaphoreType.DMA((17,)), pltpu.SemaphoreType.DMA((17,))],
        compiler_params=_params(),
    )(w_in, w_out, conv_w)


TM_MATMUL = 512
TM_COMBINE = 256


def _resident(shape):
    return pl.BlockSpec(shape, lambda i: (0,) * len(shape), pipeline_mode=pl.Buffered(1))


def _to_slabs(slab, val, j0=0):
    for j in range(val.shape[1] // 128):
        slab[j0 + j] = val[:, 128 * j:128 * (j + 1)]


def _scatter_classes(slab, j0, nj, out_ref, d, part=0, mid=None):
    tm = slab.shape[1]
    n = tm // d
    if d == 4:
        for r in range(d):
            for j in range(nj):
                out_ref[r, part * n:(part + 1) * n, 128 * j:128 * (j + 1)] = (
                    slab[j0 + j, pl.ds(r, n, stride=d), :].astype(out_ref.dtype))
        return
    q = tm // 4
    for lo in range(4):
        for j in range(nj):
            mid[j0 + j, lo * q:(lo + 1) * q, :] = slab[j0 + j, pl.ds(lo, q, stride=4), :]
    for hi in range(4):
        for lo in range(4):
            for j in range(nj):
                out_ref[4 * hi + lo, part * n:(part + 1) * n, 128 * j:128 * (j + 1)] = (
                    mid[j0 + j, pl.ds(lo * q + hi, n, stride=4), :].astype(out_ref.dtype))


def _gather_classes(slab, piece, nj, d, mid=None):
    tm = slab.shape[1]
    n = tm // d
    if d == 4:
        for r in range(d):
            for j in range(nj):
                slab[j, pl.ds(r, n, stride=d), :] = piece(r, j).astype(F32)
    else:
        q = tm // 4
        for hi in range(4):
            for lo in range(4):
                for j in range(nj):
                    mid[j, pl.ds(lo * q + hi, n, stride=4), :] = piece(4 * hi + lo, j).astype(F32)
        for lo in range(4):
            for j in range(nj):
                slab[j, pl.ds(lo, q, stride=4), :] = mid[j, lo * q:(lo + 1) * q, :]
    return jnp.concatenate([slab[j] for j in range(nj)], axis=1)


def _class_spec(d, width, tm):
    return pl.BlockSpec((d, tm // d, width), lambda i: (0, i, 0))


def _proj(x, tgt, norm_w, wblk, qkw, b256):
    s = x.shape[0]
    tm = TM_MATMUL
    nparts = 2
    tp = tm // nparts

    def body(x_ref, t_ref, nw_ref, w_ref, qkw_ref, b_ref, h_o, cg_o, qkr_o, qkn_o, vz_o, qkn4_o, v4_o, qkn16_o,
             v16_o, xmt_o, slabs, mids):
        for part in range(nparts):
            rows = slice(part * tp, (part + 1) * tp)
            slab = slabs.at[part]
            xf = x_ref[rows, :]
            xmt_o[rows, :] = xf - t_ref[rows, :]
            r = lax.rsqrt(jnp.mean(xf * xf, axis=-1, keepdims=True) + EPS)
            h = (xf * r * nw_ref[...]).astype(BF16)
            h_o[rows, :] = h
            p2 = jnp.dot(h, w_ref[2], preferred_element_type=F32)
            qkr_o[rows, :] = p2.astype(BF16)
            ss = _group_sum(p2 * p2, b_ref, split=False)
            rr = lax.rsqrt(ss * (1.0 / HEAD_DIM) + EPS)
            qkn = p2 * rr * qkw_ref[...]
            qkn_o[rows, :] = qkn.astype(BF16)
            _to_slabs(slab, qkn)
            p3 = jnp.dot(h, w_ref[3], preferred_element_type=F32)
            vz_o[rows, :] = p3.astype(BF16)
            _to_slabs(slab, p3[:, 0:512], 8)
            cg_o[rows, 0:1024] = jnp.dot(h, w_ref[0], preferred_element_type=F32).astype(BF16)
            cg_o[rows, 1024:2048] = jnp.dot(h, w_ref[1], preferred_element_type=F32).astype(BF16)
            for d, q_o, v_o in ((4, qkn4_o, v4_o), (16, qkn16_o, v16_o)):
                _scatter_classes(slab, 0, 8, q_o, d, part, mids.at[part])
                _scatter_classes(slab, 8, 4, v_o, d, part, mids.at[part])

    row = lambda w: pl.BlockSpec((tm, w), lambda i: (i, 0))
    full = lambda shp: pl.BlockSpec(shp, lambda i: (0,) * len(shp))
    nat = lambda w: jax.ShapeDtypeStruct((s, w), BF16)
    cls = lambda d, w: jax.ShapeDtypeStruct((d, s // d, w), BF16)
    return pl.pallas_call(
        body, name="proj", grid=(s // tm,),
        out_shape=(nat(1024), nat(2048), nat(1024), nat(1024), nat(1024),
                   cls(4, 1024), cls(4, 512), cls(16, 1024), cls(16, 512), jax.ShapeDtypeStruct((s, 1024), F32)),
        in_specs=[row(1024), row(1024), full((1, 1024)), _resident((4, 1024, 1024)), full((1, 1024)),
                  full((256, 256))],
        out_specs=(row(1024), row(2048), row(1024), row(1024), row(1024),
                   _class_spec(4, 1024, tm), _class_spec(4, 512, tm),
                   _class_spec(16, 1024, tm), _class_spec(16, 512, tm), row(1024)),
        scratch_shapes=[pltpu.VMEM((nparts, 12, tp, 128), F32), pltpu.VMEM((nparts, 12, tp, 128), F32)],
        compiler_params=_params(("parallel",)),
    )(x, tgt, norm_w, wblk, qkw, b256)


def _block_coords(t, i, nsub, nb, length):
    n = t * nsub + i
    q0 = i * QB
    start = pl.multiple_of(jnp.clip(n * QB - HALF, 0, length - KB), HALF)
    variant = jnp.where(n == 0, 0, jnp.where(n == nb - 1, 2, 1))
    return q0, start, variant


def _classes_per_step(r_cls, length, qt):
    return 2 if (length == qt and qt // QB <= 8 and r_cls % 2 == 0) else 1


def _split_heads(a, lo):
    zero = jnp.zeros_like(a)
    return jnp.concatenate([jnp.where(lo, a, zero), jnp.where(lo, zero, a)], axis=0)


def _col_pair(ref, q0, lane):
    return jnp.concatenate([ref[pl.ds(q0, QB), lane:lane + 1],
                            ref[pl.ds(q0, QB), HEAD_DIM + lane:HEAD_DIM + lane + 1]], axis=0)


def _attn_fwd(qkn_l, v_l, bias, gi, name):
    r_cls, length, _ = qkn_l.shape
    qt = min(length, 2048)
    nb, nsub = length // QB, qt // QB
    cb = _classes_per_step(r_cls, length, qt)

    def body(q_ref, k_ref, v_ref, b_ref, o_ref, lse_ref):
        t = pl.program_id(2)
        lo = lax.broadcasted_iota(jnp.int32, (QB, 128), 1) < HEAD_DIM

        starts, logits = [], []
        for ci in range(cb):
            for i in range(nsub):
                _, start, variant = _block_coords(t, i, nsub, nb, length)
                qq = _split_heads(q_ref[ci, i * QB:(i + 1) * QB, :], lo)
                k = k_ref[ci, pl.ds(start, KB), :]
                logits.append(lax.dot_general(qq, k, (((1,), (1,)), ((), ())), preferred_element_type=F32)
                              + b_ref[variant])
                starts.append(start)
        lg = jnp.concatenate(logits, axis=0)
        m = jnp.max(lg, axis=-1, keepdims=True)
        p = jnp.exp(lg - m)
        pb = p.astype(BF16)
        l = jnp.sum(p, axis=-1, keepdims=True)
        lse = jnp.broadcast_to(m + jnp.log(l), (cb * nsub * 2 * QB, 128))
        inv = 1.0 / l
        for ci in range(cb):
            for i in range(nsub):
                j = ci * nsub + i
                rows = slice(2 * QB * j, 2 * QB * (j + 1))
                v = v_ref[ci, pl.ds(starts[j], KB), :]
                pv = jnp.dot(pb[rows], v, preferred_element_type=F32) * inv[rows]
                o_ref[ci, i * QB:(i + 1) * QB, :] = jnp.where(lo, pv[0:QB], pv[QB:2 * QB]).astype(BF16)
                ls = lse[rows]
                lse_ref[ci, i * QB:(i + 1) * QB, :] = jnp.where(lo, ls[0:QB], ls[QB:2 * QB])

    return pl.pallas_call(
        body, name=name, grid=(N_PAIR, r_cls // cb, length // qt),
        out_shape=(jax.ShapeDtypeStruct((r_cls, length, 512), BF16),
                   jax.ShapeDtypeStruct((r_cls, length, 512), F32)),
        in_specs=[pl.BlockSpec((cb, qt, 128), lambda p, r, t: (r, t, p)),
                  pl.BlockSpec((cb, length, 128), lambda p, r, t: (r, 0, 4 + p)),
                  pl.BlockSpec((cb, length, 128), lambda p, r, t: (r, 0, p)),
                  pl.BlockSpec((None, 3, None, 2 * QB, KB), lambda p, r, t: (gi, 0, p, 0, 0))],
        out_specs=(pl.BlockSpec((cb, qt, 128), lambda p, r, t: (r, t, p)),
                   pl.BlockSpec((cb, qt, 128), lambda p, r, t: (r, t, p))),
        compiler_params=_params(("parallel", "parallel", "arbitrary")),
    )(qkn_l, qkn_l, v_l, bias)


def _combine(o_g, lse_g, cg, vz, xmt, wout, cw, cb, b256):
    s = xmt.shape[0]
    tm = TM_COMBINE
    hb = 16
    nt = s // tm

    def body(o1, o4, o16, l1, l4, l16, cg_ref, cgp_ref, cgn_ref, za_ref, xmt_ref, w_ref, cw_ref, cb_ref,
             b_ref, y_o, dout_o, ld1_o, do1_o, dza_o, dgbz_o, dzc_o, loss_o, dcb_o, dcw_o,
             do4_o, ld4_o, do16_o, ld16_o, slab, mid):
        i = pl.program_id(0)

        @pl.when(i == 0)
        def _():
            loss_o[...] = jnp.zeros_like(loss_o)
            dcb_o[...] = jnp.zeros_like(dcb_o)
            dcw_o[...] = jnp.zeros_like(dcw_o)

        u = cg_ref[:, 0:512].astype(F32)
        gb = cg_ref[:, 512:1024].astype(F32)
        gc = cg_ref[:, 1024:1536].astype(F32)
        zc = cg_ref[:, 1536:2048].astype(F32)
        tt = gc * u
        t_prev = cgp_ref[hb - 1:hb, 0:512].astype(F32) * cgp_ref[hb - 1:hb, 1024:1536].astype(F32)
        t_next = cgn_ref[0:1, 0:512].astype(F32) * cgn_ref[0:1, 1024:1536].astype(F32)
        t_prev = jnp.where(i == 0, 0.0, t_prev)
        t_next = jnp.where(i == nt - 1, 0.0, t_next)
        rows = lax.broadcasted_iota(jnp.int32, (tm, 512), 0)
        t_up = jnp.where(rows == 0, t_prev, pltpu.roll(tt, 1, 0))
        t_dn = jnp.where(rows == tm - 1, t_next, pltpu.roll(tt, tm - 1, 0))
        w0, w1, w2 = cw_ref[0:1, :], cw_ref[1:2, :], cw_ref[2:3, :]
        zb = w0 * t_up + w1 * tt + w2 * t_dn + cb_ref[...]
        sg = _sigmoid(zc)
        sz = zc * sg
        y_conv = gb * zb * sz

        a1, p1 = l1[0], o1[0].astype(F32)
        a4 = _gather_classes(slab, lambda r, j: l4[r, :, 128 * j:128 * (j + 1)], 4, 4)
        p4 = _gather_classes(slab, lambda r, j: o4[r, :, 128 * j:128 * (j + 1)], 4, 4)
        a16 = _gather_classes(slab, lambda r, j: l16[r, :, 128 * j:128 * (j + 1)], 4, 16, mid)
        p16 = _gather_classes(slab, lambda r, j: o16[r, :, 128 * j:128 * (j + 1)], 4, 16, mid)
        m = jnp.maximum(jnp.maximum(a1, a4), a16)
        e1, e4, e16 = jnp.exp(a1 - m), jnp.exp(a4 - m), jnp.exp(a16 - m)
        den = e1 + e4 + e16
        lse = m + jnp.log(den)
        o = (e1 * p1 + e4 * p4 + e16 * p16) / den
        za = za_ref[...].astype(F32)
        sga = _sigmoid(za)
        sa = za * sga
        y = jnp.concatenate([y_conv, o * sa], axis=1).astype(BF16)
        y_o[...] = y

        diff = xmt_ref[...] + jnp.dot(y, w_ref[...], preferred_element_type=F32)
        loss_o[...] += (0.5 / D_MODEL) * jnp.sum(diff * diff)
        dout = diff * (1.0 / D_MODEL)
        dout_o[...] = dout
        dy = lax.dot_general(dout.astype(BF16), w_ref[...], (((1,), (1,)), ((), ())), preferred_element_type=F32)
        dyc, dya = dy[:, 0:512], dy[:, 512:1024]

        do = dya * sa
        dza_o[...] = (dya * o * (sga * (1.0 + za * (1.0 - sga)))).astype(BF16)
        lane = lax.broadcasted_iota(jnp.int32, (tm, 512), 1)
        ld = jnp.where((lane & (HEAD_DIM - 1)) < HEAD_DIM // 2, lse, _group_sum(do * o, b_ref))
        do1_o[0] = do.astype(BF16)
        ld1_o[0] = ld
        _to_slabs(slab, do)
        _scatter_classes(slab, 0, 4, do4_o, 4)
        _scatter_classes(slab, 0, 4, do16_o, 16, 0, mid)
        _to_slabs(slab, ld)
        _scatter_classes(slab, 0, 4, ld4_o, 4)
        _scatter_classes(slab, 0, 4, ld16_o, 16, 0, mid)

        dzc = dyc * sz * gb
        dzc_o[...] = dzc.astype(BF16)
        dgbz_o[:, 0:512] = (dyc * sz * zb).astype(BF16)
        dgbz_o[:, 512:1024] = (dyc * gb * zb * (sg * (1.0 + zc * (1.0 - sg)))).astype(BF16)
        dcb_o[...] += jnp.sum(dzc, axis=0, keepdims=True)
        dcw_o[0:1, :] += jnp.sum(dzc * t_up, axis=0, keepdims=True)
        dcw_o[1:2, :] += jnp.sum(dzc * tt, axis=0, keepdims=True)
        dcw_o[2:3, :] += jnp.sum(dzc * t_dn, axis=0, keepdims=True)

    row = lambda w, j=0: pl.BlockSpec((tm, w), lambda i: (i, j))
    full = lambda shp: pl.BlockSpec(shp, lambda i: (0,) * len(shp))
    prev = pl.BlockSpec((hb, 2048), lambda i: (jnp.maximum(i * (tm // hb) - 1, 0), 0))
    nxt = pl.BlockSpec((hb, 2048), lambda i: (jnp.minimum((i + 1) * (tm // hb), s // hb - 1), 0))
    cls = lambda d, dt: jax.ShapeDtypeStruct((d, s // d, 512), dt)
    cspecs = [_class_spec(d, 512, tm) for d in DILATIONS]
    return pl.pallas_call(
        body, name="combine", grid=(nt,),
        out_shape=(jax.ShapeDtypeStruct((s, 1024), BF16), jax.ShapeDtypeStruct((s, 1024), F32),
                   cls(1, F32), cls(1, BF16), jax.ShapeDtypeStruct((s, 512), BF16),
                   jax.ShapeDtypeStruct((s, 1024), BF16), jax.ShapeDtypeStruct((s, 512), BF16),
                   jax.ShapeDtypeStruct((1, 128), F32), jax.ShapeDtypeStruct((1, 512), F32),
                   jax.ShapeDtypeStruct((8, 512), F32),
                   cls(4, BF16), cls(4, F32), cls(16, BF16), cls(16, F32)),
        in_specs=cspecs + cspecs + [row(2048), prev, nxt, row(512, 1), row(1024),
                                    _resident((1024, 1024)), full((8, 512)), full((1, 512)), full((256, 256))],
        out_specs=(row(1024), row(1024), cspecs[0], cspecs[0], row(512), row(1024), row(512),
                   full((1, 128)), full((1, 512)), full((8, 512)),
                   cspecs[1], cspecs[1], cspecs[2], cspecs[2]),
        scratch_shapes=[pltpu.VMEM((4, tm, 128), F32), pltpu.VMEM((4, tm, 128), F32)],
        compiler_params=_params(("arbitrary",)),
    )(*o_g, *lse_g, cg, cg, cg, vz, xmt, wout, cw, cb, b256)


def _attn_bwd(qkn_l, v_l, do_l, ld_l, bias, gi, name):
    r_cls, length, _ = qkn_l.shape
    qt = min(length, 2048 if length <= 4096 else 1024)
    nb, nsub, nt = length // QB, qt // QB, length // qt
    chunk = min(length, 4096)
    nchunk = length // chunk
    cb = _classes_per_step(r_cls, length, qt)
    nrb = r_cls // cb
    nbuf = 2 * cb

    def body(q_ref, k_ref, v_ref, do_ref, ld_ref, b_ref, dq_ref, dkv_hbm, dsum_ref, dk_acc, dv_acc, stage, sems):
        p_id, r, t = pl.program_id(0), pl.program_id(1), pl.program_id(2)
        lo = lax.broadcasted_iota(jnp.int32, (QB, 128), 1) < HEAD_DIM

        @pl.when(t == 0)
        def _():
            dk_acc[...] = jnp.zeros_like(dk_acc)
            dv_acc[...] = jnp.zeros_like(dv_acc)

        @pl.when((t == 0) & (r == 0))
        def _():
            dsum_ref[...] = jnp.zeros_like(dsum_ref)

        nt_dims = (((1,), (1,)), ((), ()))
        tn_dims = (((0,), (0,)), ((), ()))
        coords, qqs, dds, logits, dps, lcols, dcols = [], [], [], [], [], [], []
        for ci in range(cb):
            for i in range(nsub):
                q0, start, variant = _block_coords(t, i, nsub, nb, length)
                qq = _split_heads(q_ref[ci, q0:q0 + QB, :], lo)
                dd = _split_heads(do_ref[ci, q0:q0 + QB, :], lo)
                k = k_ref[ci, pl.ds(start, KB), :]
                v = v_ref[ci, pl.ds(start, KB), :]
                logits.append(lax.dot_general(qq, k, nt_dims, preferred_element_type=F32) + b_ref[variant])
                dps.append(lax.dot_general(dd, v, nt_dims, preferred_element_type=F32))
                lcols.append(_col_pair(ld_ref.at[ci], q0, 0))
                dcols.append(_col_pair(ld_ref.at[ci], q0, HEAD_DIM // 2))
                coords.append((ci, i, q0, start, variant))
                qqs.append(qq)
                dds.append(dd)
        p = jnp.exp(jnp.concatenate(logits, axis=0) - jnp.concatenate(lcols, axis=0))
        ds = p * (jnp.concatenate(dps, axis=0) - jnp.concatenate(dcols, axis=0))
        pb = p.astype(BF16)
        dsb = ds.astype(BF16)
        middle = None
        for j, (ci, i, q0, start, variant) in enumerate(coords):
            rows = slice(2 * QB * j, 2 * QB * (j + 1))
            if 0 < i < nsub - 1:
                middle = ds[rows] if middle is None else middle + ds[rows]
            else:
                dsum_ref[variant] += ds[rows]
            dqq = jnp.dot(dsb[rows], k_ref[ci, pl.ds(start, KB), :], preferred_element_type=F32)
            dq_ref[ci, q0:q0 + QB, :] = jnp.where(lo, dqq[0:QB], dqq[QB:2 * QB]).astype(BF16)
            dk_acc[ci, pl.ds(start, KB), :] += lax.dot_general(dsb[rows], qqs[j], tn_dims,
                                                               preferred_element_type=F32)
            dv_acc[ci, pl.ds(start, KB), :] += lax.dot_general(pb[rows], dds[j], tn_dims,
                                                               preferred_element_type=F32)
        if middle is not None:
            dsum_ref[1] += middle

        @pl.when(t == nt - 1)
        def _():
            def copy(k):
                ci, which, c = k // (2 * nchunk), (k // nchunk) % 2, k % nchunk
                rows = pl.ds(c * chunk, chunk)
                return pltpu.make_async_copy(stage.at[k % nbuf], dkv_hbm.at[r * cb + ci, p_id, which, rows, :],
                                             sems.at[k % nbuf])

            for k in range(2 * nchunk * cb):
                if k < nbuf:
                    @pl.when((p_id > 0) | (r > 0))
                    def _():
                        copy(k).wait()
                else:
                    copy(k).wait()
                acc = (dk_acc, dv_acc)[(k // nchunk) % 2]
                stage[k % nbuf] = acc[k // (2 * nchunk), pl.ds((k % nchunk) * chunk, chunk), :].astype(BF16)
                copy(k).start()

            @pl.when((p_id == N_PAIR - 1) & (r == nrb - 1))
            def _():
                for k in range(nbuf):
                    copy(k).wait()

    qspec = pl.BlockSpec((cb, qt, 128), lambda p, r, t: (r, t, p))
    return pl.pallas_call(
        body, name=name, grid=(N_PAIR, nrb, nt),
        out_shape=(jax.ShapeDtypeStruct((r_cls, length, 512), BF16),
                   jax.ShapeDtypeStruct((r_cls, N_PAIR, 2, length, 128), BF16),
                   jax.ShapeDtypeStruct((N_PAIR, 3, 2 * QB, KB), F32)),
        in_specs=[qspec,
                  pl.BlockSpec((cb, length, 128), lambda p, r, t: (r, 0, 4 + p)),
                  pl.BlockSpec((cb, length, 128), lambda p, r, t: (r, 0, p)),
                  qspec, qspec,
                  pl.BlockSpec((None, 3, None, 2 * QB, KB), lambda p, r, t: (gi, 0, p, 0, 0))],
        out_specs=(qspec, pl.BlockSpec(memory_space=pl.ANY),
                   pl.BlockSpec((None, 3, 2 * QB, KB), lambda p, r, t: (p, 0, 0, 0))),
        scratch_shapes=[pltpu.VMEM((cb, length, 128), F32), pltpu.VMEM((cb, length, 128), F32),
                        pltpu.VMEM((nbuf, chunk, 128), BF16), pltpu.SemaphoreType.DMA((nbuf,))],
        compiler_params=_params(("arbitrary", "arbitrary", "arbitrary")),
    )(qkn_l, qkn_l, v_l, do_l, ld_l, bias)


def _bwd_tail(dq_g, dkv_g, qkr, qkw, dza, dgbz, dzc, cg, cw, wblk, x, norm_w, dout, b256):
    s = x.shape[0]
    tm = TM_COMBINE
    hb = 16
    nt = s // tm

    def body(dq1, dq4, dq16, dkv1, dkv4, dkv16, qkr_ref, qkw_ref, dza_ref, dgbz_ref, dzc_ref,
             dzp_ref, dzn_ref, u_ref, gc_ref, cw_ref, w_ref, x_ref, nw_ref, dout_ref, b_ref,
             gx_o, dproj_o, dnw_o, dqkw_o, slab, mid):
        i = pl.program_id(0)

        def nat_q(ref, d):
            return _gather_classes(slab, lambda r, j: ref[r, :, 128 * j:128 * (j + 1)], 4, d, mid)

        def nat_kv(ref, d, which):
            return _gather_classes(slab, lambda r, j: ref[r, j, which], 4, d, mid)

        @pl.when(i == 0)
        def _():
            dnw_o[...] = jnp.zeros_like(dnw_o)
            dqkw_o[...] = jnp.zeros_like(dqkw_o)

        dzc = dzc_ref[...].astype(F32)
        d_prev = jnp.where(i == 0, 0.0, dzp_ref[hb - 1:hb, :].astype(F32))
        d_next = jnp.where(i == nt - 1, 0.0, dzn_ref[0:1, :].astype(F32))
        rows = lax.broadcasted_iota(jnp.int32, (tm, 512), 0)
        d_up = jnp.where(rows == 0, d_prev, pltpu.roll(dzc, 1, 0))
        d_dn = jnp.where(rows == tm - 1, d_next, pltpu.roll(dzc, tm - 1, 0))
        dt = cw_ref[0:1, :] * d_dn + cw_ref[1:2, :] * dzc + cw_ref[2:3, :] * d_up
        u = u_ref[...].astype(F32)
        gc = gc_ref[...].astype(F32)
        dproj_o[:, 0:512] = (dt * gc).astype(BF16)
        dproj_o[:, 512:1024] = dgbz_ref[:, 0:512]
        dproj_o[:, 1024:1536] = (dt * u).astype(BF16)
        dproj_o[:, 1536:2048] = dgbz_ref[:, 512:1024]

        dqn = (dq1[0].astype(F32) + nat_q(dq4, 4) + nat_q(dq16, 16)) * (1.0 / 8.0)
        dk1 = jnp.concatenate([dkv1[0, j, 0] for j in range(N_PAIR)], axis=1)
        dv1 = jnp.concatenate([dkv1[0, j, 1] for j in range(N_PAIR)], axis=1)
        dkn = dk1 + nat_kv(dkv4, 4, 0) + nat_kv(dkv16, 16, 0)
        dvn = dv1 + nat_kv(dkv4, 4, 1) + nat_kv(dkv16, 16, 1)
        g = jnp.concatenate([dqn, dkn], axis=1) * qkw_ref[...]
        raw = qkr_ref[...].astype(F32)
        rr = lax.rsqrt(_group_sum(raw * raw, b_ref, split=False) * (1.0 / HEAD_DIM) + EPS)
        proj_gq = _group_sum(g * raw, b_ref) * (1.0 / HEAD_DIM)
        draw = rr * g - raw * (rr * rr * rr) * proj_gq
        dqkw_o[...] += jnp.sum(jnp.concatenate([dqn, dkn], axis=1) * raw * rr, axis=0, keepdims=True)
        dproj_o[:, 2048:3072] = draw.astype(BF16)
        dproj_o[:, 3072:3584] = dvn.astype(BF16)
        dproj_o[:, 3584:4096] = dza_ref[...]

        nt_dims = (((1,), (1,)), ((), ()))
        dh = lax.dot_general(dproj_o[:, 0:1024], w_ref[0], nt_dims, preferred_element_type=F32)
        for b in range(1, 4):
            dh += lax.dot_general(dproj_o[:, 1024 * b:1024 * b + 1024], w_ref[b], nt_dims,
                                  preferred_element_type=F32)

        xf = x_ref[...]
        r = lax.rsqrt(jnp.mean(xf * xf, axis=-1, keepdims=True) + EPS)
        gh = dh * nw_ref[...]
        dnw_o[...] += jnp.sum(dh * xf * r, axis=0, keepdims=True)
        mean_gx = jnp.mean(gh * xf, axis=-1, keepdims=True)
        gx_o[...] = dout_ref[...] + r * gh - xf * (r * r * r) * mean_gx

    row = lambda w, j=0: pl.BlockSpec((tm, w), lambda i: (i, j))
    full = lambda shp: pl.BlockSpec(shp, lambda i: (0,) * len(shp))
    prev = pl.BlockSpec((hb, 512), lambda i: (jnp.maximum(i * (tm // hb) - 1, 0), 0))
    nxt = pl.BlockSpec((hb, 512), lambda i: (jnp.minimum((i + 1) * (tm // hb), s // hb - 1), 0))
    return pl.pallas_call(
        body, name="bwd_tail", grid=(nt,),
        out_shape=(jax.ShapeDtypeStruct((s, 1024), F32), jax.ShapeDtypeStruct((s, 4096), BF16),
                   jax.ShapeDtypeStruct((1, 1024), F32), jax.ShapeDtypeStruct((1, 1024), F32)),
        in_specs=[_class_spec(d, 512, tm) for d in DILATIONS]
        + [pl.BlockSpec((d, N_PAIR, 2, tm // d, 128), lambda i: (0, 0, 0, i, 0)) for d in DILATIONS]
        + [row(1024), full((1, 1024)), row(512), row(1024), row(512), prev, nxt,
           row(512, 0), row(512, 2), full((8, 512)), _resident((4, 1024, 1024)), row(1024),
           full((1, 1024)), row(1024), full((256, 256))],
        out_specs=(row(1024), row(4096), full((1, 1024)), full((1, 1024))),
        scratch_shapes=[pltpu.VMEM((4, tm, 128), F32), pltpu.VMEM((4, tm, 128), F32)],
        compiler_params=_params(("arbitrary",)),
    )(*dq_g, *dkv_g, qkr, qkw, dza, dgbz, dzc, dzc, dzc, cg, cg, cw, wblk, x, norm_w, dout, b256)


def _wgrad(a, b, row_blocked, name):
    s, m = a.shape
    n = b.shape[1]
    tk = 2048
    ncol = min(n, 2048)
    nj, nk = n // ncol, s // tk

    def body(a_ref, b_ref, o_ref, acc):
        kk = pl.program_id(1)

        @pl.when(kk == 0)
        def _():
            acc[...] = jnp.zeros_like(acc)

        acc[...] += lax.dot_general(a_ref[...], b_ref[...].astype(BF16), (((0,), (0,)), ((), ())),
                                    preferred_element_type=F32)

        @pl.when(kk == nk - 1)
        def _():
            blocks, _, rows, _ = o_ref.shape
            for blk in range(blocks):
                for half in range(2):
                    if row_blocked:
                        r0 = (2 * blk + half) * rows
                        o_ref[blk, half] = acc[r0:r0 + rows, :].astype(BF16)
                    else:
                        o_ref[blk, half] = acc[half * rows:(half + 1) * rows,
                                               1024 * blk:1024 * (blk + 1)].astype(BF16)

    if row_blocked:
        out_shape = jax.ShapeDtypeStruct((4, 2, m // 8, 1024), BF16)
        out_spec = pl.BlockSpec((4, 2, m // 8, 1024), lambda j, k: (0, 0, 0, 0))
    else:
        out_shape = jax.ShapeDtypeStruct((n // 1024, 2, m // 2, 1024), BF16)
        out_spec = pl.BlockSpec((ncol // 1024, 2, m // 2, 1024), lambda j, k: (j, 0, 0, 0))
    return pl.pallas_call(
        body, name=name, grid=(nj, nk),
        out_shape=out_shape,
        in_specs=[pl.BlockSpec((tk, m), lambda j, k: (k, 0)), pl.BlockSpec((tk, ncol), lambda j, k: (k, j))],
        out_specs=out_spec,
        scratch_shapes=[pltpu.VMEM((m, ncol), F32)],
        compiler_params=_params(("parallel", "arbitrary")),
    )(a, b)


def _dbias(dsums, onehot_all):
    def body(ds1_ref, ds4_ref, ds16_ref, oh_ref, o_ref):
        @pl.when(pl.program_id(0) == 0)
        def _():
            o_ref[...] = jnp.zeros_like(o_ref)

        hrow = lax.broadcasted_iota(jnp.int32, (8, KB), 0)
        flip = (lax.broadcasted_iota(jnp.int32, (QB, QB), 0)
                + lax.broadcasted_iota(jnp.int32, (QB, QB), 1) == QB - 1).astype(F32)

        def diagonal_sums(tile):
            rev = jnp.dot(flip, tile, preferred_element_type=F32, precision=lax.Precision.HIGHEST)
            sums = jnp.sum(pltpu.roll(rev, 0, 1, stride=1, stride_axis=0), axis=0, keepdims=True)
            return pltpu.roll(sums, KB - (QB - 1), 1)

        for g, ds_ref in enumerate((ds1_ref, ds4_ref, ds16_ref)):
            diag = jnp.zeros((8, KB), F32)
            for p in range(N_PAIR):
                diag = jnp.where(hrow == 2 * p, diagonal_sums(ds_ref[p, 0:QB, :]), diag)
                diag = jnp.where(hrow == 2 * p + 1, diagonal_sums(ds_ref[p, QB:2 * QB, :]), diag)
            o_ref[...] += jnp.dot(diag, oh_ref[g], preferred_element_type=F32, precision=lax.Precision.HIGHEST)

    ds_spec = pl.BlockSpec((N_PAIR, None, 2 * QB, KB), lambda v: (0, v, 0, 0))
    return pl.pallas_call(
        body, name="dbias", grid=(3,),
        out_shape=jax.ShapeDtypeStruct((8, 128), F32),
        in_specs=[ds_spec, ds_spec, ds_spec, pl.BlockSpec((3, None, KB, 128), lambda v: (0, v, 0, 0))],
        out_specs=pl.BlockSpec((8, 128), lambda v: (0, 0)),
        compiler_params=_params(("arbitrary",)),
    )(*dsums, onehot_all)


def _gsync(pw_in, pw_out, small):
    hin, hout = pw_in.shape[2], pw_out.shape[2]
    nsmall = small.shape[0]

    def body(pin_hbm, pout_hbm, small_ref, gin_o, gout_o, small_o,
             mine_in, recv_in, s1_in, r1_in, s2_in, r2_in, mine_out, recv_out, s1_out, r1_out, s2_out, r2_out, gather,
             lsem, asend, arecv, bsend, brecv, csend, crecv, ssend, srecv):
        x, y, c = lax.axis_index("x"), lax.axis_index("y"), lax.axis_index("c")
        b = 2 * x + y
        dev = 4 * x + 2 * y + c
        sib = (x, y, 1 - c)
        xnbr, ynbr = (1 - x, y, c), (x, 1 - y, c)
        bx, by, bd = b ^ 2, b ^ 1, b ^ 3

        def rcopy(src, dst, ssem, rsem, to):
            return pltpu.make_async_remote_copy(src_ref=src, dst_ref=dst, send_sem=ssem, recv_sem=rsem,
                                                device_id=to, device_id_type=MESH)

        gather[dev] = small_ref[...]
        s_sends = []
        for k in range(1, 8):
            to = (x ^ (k >> 2), y ^ ((k >> 1) & 1), c ^ (k & 1))
            cp = rcopy(gather.at[dev], gather.at[dev], ssend.at[k - 1], srecv.at[k - 1], to)
            cp.start()
            s_sends.append(cp)

        a_in = rcopy(pin_hbm.at[:, 1 - c], recv_in, asend.at[0], arecv.at[0], sib)
        a_out = rcopy(pout_hbm.at[:, 1 - c], recv_out, asend.at[1], arecv.at[1], sib)
        a_in.start()
        a_out.start()
        l_in = pltpu.make_async_copy(pin_hbm.at[:, c], mine_in, lsem.at[0])
        l_out = pltpu.make_async_copy(pout_hbm.at[:, c], mine_out, lsem.at[1])
        l_in.start()
        l_out.start()
        l_in.wait()
        l_out.wait()

        def phase_one(a_cp, mine, recv, s1, r1, half, base):
            a_cp.wait_recv()
            q = half // 2
            sends = []
            for part, (peer, blk_peer) in enumerate(((xnbr, bx), (ynbr, by))):
                rows = pl.ds(part * q, q)
                for slot, blk in enumerate((blk_peer, bd)):
                    s1[part, slot] = (mine[blk, rows, :].astype(F32) + recv[blk, rows, :].astype(F32)).astype(BF16)
                cp = rcopy(s1.at[part], r1.at[part], bsend.at[base + part], brecv.at[base + part], peer)
                cp.start()
                sends.append(cp)
            return sends

        def phase_two(p1, mine, recv, r1, s2, r2, half, base):
            q = half // 2
            own, sends = [], []
            for part, (peer, blk_next) in enumerate(((ynbr, by), (xnbr, bx))):
                rows = pl.ds(part * q, q)
                p1[part].wait_recv()
                own.append(mine[b, rows, :].astype(F32) + recv[b, rows, :].astype(F32) + r1[part, 0].astype(F32))
                s2[part] = (mine[blk_next, rows, :].astype(F32) + recv[blk_next, rows, :].astype(F32)
                            + r1[part, 1].astype(F32)).astype(BF16)
                cp = rcopy(s2.at[part], r2.at[part], bsend.at[base + 2 + part], brecv.at[base + 2 + part], peer)
                cp.start()
                sends.append(cp)
            return own, sends

        def stage_c(own, p2, r2, g_o, half, idx):
            q = half // 2
            for part in range(2):
                p2[part].wait_recv()
                g_o[pl.ds(pl.multiple_of(c * half + part * q, q), q), :] = own[part] + r2[part].astype(F32)
            rows = g_o.at[pl.ds(pl.multiple_of(c * half, half), half), :]
            cp = rcopy(rows, rows, csend.at[idx], crecv.at[idx], sib)
            cp.start()
            return cp

        p1_in = phase_one(a_in, mine_in, recv_in, s1_in, r1_in, hin, 0)
        p1_out = phase_one(a_out, mine_out, recv_out, s1_out, r1_out, hout, 4)
        own_in, p2_in = phase_two(p1_in, mine_in, recv_in, r1_in, s2_in, r2_in, hin, 0)
        own_out, p2_out = phase_two(p1_out, mine_out, recv_out, r1_out, s2_out, r2_out, hout, 4)
        c_in = stage_c(own_in, p2_in, r2_in, gin_o, hin, 0)
        c_out = stage_c(own_out, p2_out, r2_out, gout_o, hout, 1)
        b_in, b_out = p1_in + p2_in, p1_out + p2_out

        for cp in s_sends:
            cp.wait_recv()
        tot = gather[0]
        for d in range(1, 8):
            tot = tot + gather[d]
        small_o[...] = tot

        for g_o, half, idx in ((gin_o, hin, 0), (gout_o, hout, 1)):
            other = g_o.at[pl.ds(pl.multiple_of((1 - c) * half, half), half), :]
            rcopy(other, other, csend.at[idx], crecv.at[idx], sib).wait_recv()
        for cp in s_sends + [a_in, a_out] + b_in + b_out + [c_in, c_out]:
            cp.wait_send()

    vm = pl.BlockSpec(memory_space=pltpu.VMEM)
    hbm = pl.BlockSpec(memory_space=pl.ANY)
    return pl.pallas_call(
        body, name="gsync",
        out_shape=(jax.ShapeDtypeStruct((2 * hin, 1024), F32), jax.ShapeDtypeStruct((2 * hout, 1024), F32),
                   jax.ShapeDtypeStruct((nsmall, 128), F32)),
        in_specs=[hbm, hbm, vm], out_specs=(vm, vm, vm),
        scratch_shapes=[pltpu.VMEM((4, hin, 1024), BF16), pltpu.VMEM((4, hin, 1024), BF16),
                        pltpu.VMEM((2, 2, hin // 2, 1024), BF16), pltpu.VMEM((2, 2, hin // 2, 1024), BF16),
                        pltpu.VMEM((2, hin // 2, 1024), BF16), pltpu.VMEM((2, hin // 2, 1024), BF16),
                        pltpu.VMEM((4, hout, 1024), BF16), pltpu.VMEM((4, hout, 1024), BF16),
                        pltpu.VMEM((2, 2, hout // 2, 1024), BF16), pltpu.VMEM((2, 2, hout // 2, 1024), BF16),
                        pltpu.VMEM((2, hout // 2, 1024), BF16), pltpu.VMEM((2, hout // 2, 1024), BF16),
                        pltpu.VMEM((8, nsmall, 128), F32),
                        pltpu.SemaphoreType.DMA((2,)),
                        pltpu.SemaphoreType.DMA((2,)), pltpu.SemaphoreType.DMA((2,)),
                        pltpu.SemaphoreType.DMA((8,)), pltpu.SemaphoreType.DMA((8,)),
                        pltpu.SemaphoreType.DMA((2,)), pltpu.SemaphoreType.DMA((2,)),
                        pltpu.SemaphoreType.DMA((7,)), pltpu.SemaphoreType.DMA((7,))],
        compiler_params=_params(),
    )(pw_in, pw_out, small)


def _adamw_math(w, g, m, v):
    m = ADAM_B1 * m + (1.0 - ADAM_B1) * g
    v = ADAM_B2 * v + (1.0 - ADAM_B2) * (g * g)
    m_hat = m / (1.0 - ADAM_B1 ** ADAM_STEP)
    v_hat = v / (1.0 - ADAM_B2 ** ADAM_STEP)
    delta = -ADAM_LR * (m_hat / (jnp.sqrt(v_hat) + ADAM_EPS) + ADAM_WD * w)
    return delta, m, v


def _adamw(w, g, m, v, name):
    rows, cols = w.shape
    tr = 256 if rows % 256 == 0 else rows

    def body(w_ref, g_ref, m_ref, v_ref, g_o, d_o, m_o, v_o):
        g = g_ref[...]
        d, m2, v2 = _adamw_math(w_ref[...], g, m_ref[...], v_ref[...])
        g_o[...] = g
        d_o[...] = d
        m_o[...] = m2
        v_o[...] = v2

    spec = pl.BlockSpec((tr, cols), lambda i: (i, 0))
    shp = jax.ShapeDtypeStruct((rows, cols), F32)
    return pl.pallas_call(
        body, name=name, grid=(rows // tr,), out_shape=(shp, shp, shp, shp),
        in_specs=[spec] * 4, out_specs=(spec, spec, spec, spec),
        compiler_params=_params(("parallel",)),
    )(w, g, m, v)


def _fold_heads(dqkw):
    def body(x_ref, o_ref):
        xs = x_ref[...]
        sq = xs[0:1] + xs[1:2] + xs[2:3] + xs[3:4]
        sk = xs[4:5] + xs[5:6] + xs[6:7] + xs[7:8]
        both = jnp.concatenate([sq, sk], axis=0)
        o_ref[...] = both + pltpu.roll(both, HEAD_DIM, 1)

    vm = pl.BlockSpec(memory_space=pltpu.VMEM)
    return pl.pallas_call(body, name="fold_heads", out_shape=jax.ShapeDtypeStruct((2, 128), F32),
                          in_specs=[vm], out_specs=vm, compiler_params=_params())(dqkw)


def kernel(x, norm_w, w_in, conv_w, conv_b, q_norm_w, k_norm_w, rel_bias, w_out, loss_target, m_norm_w, m_w_in, m_conv_w, m_conv_b, m_q_norm_w, m_k_norm_w, m_rel_bias, m_w_out, v_norm_w, v_w_in, v_conv_w, v_conv_b, v_q_norm_w, v_k_norm_w, v_rel_bias, v_w_out):
    x2 = x[0]
    tgt = loss_target[0]
    blk = 2 * lax.axis_index("x") + lax.axis_index("y")

    conv_w8 = jnp.pad(conv_w, ((0, 5), (0, 0)))
    wblk, woutblk, cwblk = _wgather(w_in, w_out, conv_w8)
    wout_full = woutblk.reshape(1024, 1024)
    cw_full = cwblk.transpose(1, 0, 2).reshape(8, 512)

    qkw = jnp.concatenate([jnp.tile(q_norm_w, 8) * 0.125, jnp.tile(k_norm_w, 8)])[None, :]
    qkw_raw = jnp.concatenate([jnp.tile(q_norm_w, 8), jnp.tile(k_norm_w, 8)])[None, :]
    gidx = jnp.arange(256) // HEAD_DIM
    b256 = (gidx[:, None] == gidx[None, :]).astype(BF16)

    h, cg, qkr, qkn, vz, qkn4, v4, qkn16, v16, xmt = _proj(x2, tgt, norm_w[None, :], wblk, qkw, b256)

    biases = _bias_tables(rel_bias)
    qkn_l = [qkn[None], qkn4, qkn16]
    v_l = [vz[None], v4, v16]
    o_g, lse_g = [], []
    for gi, d in enumerate(DILATIONS):
        o_l, lse_l = _attn_fwd(qkn_l[gi], v_l[gi], biases, gi, f"attn_fwd_d{d}")
        o_g.append(o_l)
        lse_g.append(lse_l)

    (y, dout, ld1, do1, dza, dgbz, dzc, loss_p, dcb, dcw, do4, ld4, do16, ld16) = _combine(
        o_g, lse_g, cg, vz, xmt, wout_full, cw_full, conv_b[None, :], b256)

    dq_g, dkv_g, dsums = [], [], []
    for gi, (d, do_l, ld_l) in enumerate(zip(DILATIONS, (do1, do4, do16), (ld1, ld4, ld16))):
        dq_l, dkv_l, dsum = _attn_bwd(qkn_l[gi], v_l[gi], do_l, ld_l, biases, gi, f"attn_bwd_d{d}")
        dq_g.append(dq_l)
        dkv_g.append(dkv_l)
        dsums.append(dsum)

    grad_x, dproj, dnw, dqkw = _bwd_tail(dq_g, dkv_g, qkr, qkw_raw, dza, dgbz, dzc, cg, cw_full, wblk,
                                         x2, norm_w[None, :], dout, b256)

    pw_in = _wgrad(h, dproj, False, "wgrad_in")
    pw_out = _wgrad(y, dout, True, "wgrad_out")
    dbias8 = _dbias(dsums, jnp.stack([_diag_bucket_onehot(d) for d in DILATIONS], axis=0))

    small = jnp.concatenate([dnw.reshape(8, 128), dcb.reshape(4, 128), dqkw.reshape(8, 128),
                             dcw[0:3].reshape(12, 128), dbias8, jnp.pad(loss_p, ((0, 7), (0, 0)))], axis=0)
    g_win, g_wout, gsmall = _gsync(pw_in, pw_out, small)

    g_nw = gsmall[0:8].reshape(1024)
    g_cb = gsmall[8:12].reshape(512)
    folded = _fold_heads(gsmall[12:20])
    g_qw, g_kw = folded[0, 0:64], folded[1, 0:64]
    g_cw = lax.dynamic_slice(gsmall[20:32].reshape(3, 512), (0, blk * 128), (3, 128))
    g_rb = gsmall[32:40][:, 0:32].T
    loss = gsmall[40, 0]

    g_win, d_win, nm_win, nv_win = _adamw(w_in, g_win, m_w_in, v_w_in, "adamw_w_in")
    g_wout, d_wout, nm_wout, nv_wout = _adamw(w_out, g_wout, m_w_out, v_w_out, "adamw_w_out")

    def pack(parts):
        rows = [parts[0].reshape(8, 128), parts[1].reshape(4, 128),
                jnp.pad(parts[2], (0, 64))[None, :], jnp.pad(parts[3], (0, 64))[None, :],
                parts[4], jnp.pad(parts[5].T, ((0, 0), (0, 96)))]
        return jnp.concatenate(rows, axis=0)

    ws = pack([norm_w, conv_b, q_norm_w, k_norm_w, conv_w, rel_bias])
    gs = pack([g_nw, g_cb, g_qw, g_kw, g_cw, g_rb])
    ms = pack([m_norm_w, m_conv_b, m_q_norm_w, m_k_norm_w, m_conv_w, m_rel_bias])
    vs = pack([v_norm_w, v_conv_b, v_q_norm_w, v_k_norm_w, v_conv_w, v_rel_bias])
    rpad = lambda a: jnp.pad(a, ((0, 7), (0, 0)))
    _, d_s, nm_s, nv_s = _adamw(rpad(ws), rpad(gs), rpad(ms), rpad(vs), "adamw_small")

    def unpack(a):
        return (a[0:8].reshape(1024), a[12:13, 0:64].reshape(64), a[13:14, 0:64].reshape(64),
                a[14:17], a[8:12].reshape(512), a[17:25, 0:32].T)

    def ordered(nw, win, cw, cb, qw, kw, rb, wout):
        return (nw, win, cw, cb, qw, kw, rb, wout)

    g_un = (g_nw, g_qw, g_kw, g_cw, g_cb, g_rb)
    outs = [loss, grad_x[None]]
    for un, win_v, wout_v in ((g_un, g_win, g_wout), (unpack(d_s), d_win, d_wout),
                              (unpack(nm_s), nm_win, nm_wout), (unpack(nv_s), nv_win, nv_wout)):
        nw, qw, kw, cw, cb, rb = un
        outs.extend(ordered(nw, win_v, cw, cb, qw, kw, rb, wout_v))
    return tuple(outs)
```

```python
import math

import jax
import jax.numpy as jnp
from jax import lax
from jax.experimental import pallas as pl
from jax.experimental.pallas import tpu as pltpu

F32 = jnp.float32
BF16 = jnp.bfloat16
MESH = pl.DeviceIdType.MESH

D_MODEL = 1024
CONV_W = 512
ATTN_W = 512
HEAD_DIM = 64
N_PAIR = 4
DILATIONS = (1, 4, 16)
HALF = 64
QB = 128
KB = QB + 2 * HALF
NUM_BUCKETS = 32
MAX_DISTANCE = 1024
EPS = 1e-6
NEG = -1e30
ADAM_LR, ADAM_B1, ADAM_B2, ADAM_EPS, ADAM_WD, ADAM_STEP = 0.001, 0.9, 0.999, 1e-08, 0.01, 10
VMEM_LIMIT = 48 << 20


def _params(sem=None, vmem=VMEM_LIMIT, **kw):
    if sem is not None:
        kw["dimension_semantics"] = sem
    return pltpu.CompilerParams(vmem_limit_bytes=vmem, **kw)


def _sigmoid(z):
    return 1.0 / (1.0 + jnp.exp(-z))


def _group_sum(val, b_ref, split=True):
    hi = val.astype(BF16)
    lo = (val - hi.astype(F32)).astype(BF16) if split else None
    outs = []
    for j in range(val.shape[1] // 256):
        sl = slice(256 * j, 256 * j + 256)
        part = jnp.dot(hi[:, sl], b_ref[...], preferred_element_type=F32)
        if split:
            part = part + jnp.dot(lo[:, sl], b_ref[...], preferred_element_type=F32)
        outs.append(part)
    return outs[0] if len(outs) == 1 else jnp.concatenate(outs, axis=1)


def _t5_bucket(rel):
    half_b = NUM_BUCKETS // 2
    max_exact = half_b // 2
    ret = jnp.where(rel > 0, half_b, 0)
    n = jnp.abs(rel)
    nf = jnp.maximum(n, 1).astype(F32)
    large = max_exact + (jnp.log(nf / max_exact) / math.log(MAX_DISTANCE / max_exact)
                         * (half_b - max_exact)).astype(jnp.int32)
    large = jnp.minimum(large, half_b - 1)
    return ret + jnp.where(n < max_exact, n, large)


def _bias_tables(rel_bias):
    rows = []
    key = jnp.arange(KB)
    for dilation in DILATIONS:
        for variant in range(3):
            off = (0, HALF, 2 * HALF)[variant]
            rel = ((key - off + KB // 2) % KB) - KB // 2
            bkt = _t5_bucket(jnp.clip(rel, -HALF, HALF) * dilation)
            rows.append(jnp.where(jnp.abs(rel) <= HALF, bkt, -1))
    bkt_all = jnp.broadcast_to(jnp.stack(rows, axis=0).astype(jnp.int32)[:, None, :], (9, 8, KB))

    def body(rb_ref, bkt_ref, o_ref):
        bkt = bkt_ref[...]
        off = (pl.program_id(0) % 3) * HALF
        rel = (lax.broadcasted_iota(jnp.int32, (QB, KB), 1) - lax.broadcasted_iota(jnp.int32, (QB, KB), 0)) - off
        band = jnp.abs(rel) <= HALF
        for h in range(8):
            acc = jnp.full((8, KB), NEG, F32)
            for b in range(NUM_BUCKETS):
                acc = jnp.where(bkt == b, rb_ref[b, h], acc)
            rolled = pltpu.roll(jnp.broadcast_to(acc[0:1], (QB, KB)), 0, 1, stride=1, stride_axis=0)
            o_ref[h] = jnp.where(band, rolled, NEG)

    out = pl.pallas_call(
        body, name="bias_tables", grid=(9,),
        out_shape=jax.ShapeDtypeStruct((9, 8, QB, KB), F32),
        in_specs=[pl.BlockSpec(memory_space=pltpu.SMEM), pl.BlockSpec((None, 8, KB), lambda i: (i, 0, 0))],
        out_specs=pl.BlockSpec((None, 8, QB, KB), lambda i: (i, 0, 0, 0)),
        compiler_params=_params(("parallel",)),
    )(rel_bias, bkt_all)
    return out.reshape(3, 3, N_PAIR, 2 * QB, KB)


def _diag_bucket_onehot(dilation):
    out = []
    c = jnp.arange(KB)
    for variant in range(3):
        off = (0, HALF, 2 * HALF)[variant]
        rel = ((c - off + 128) % 256) - 128
        band = jnp.abs(rel) <= HALF
        bkt = _t5_bucket(jnp.clip(rel, -HALF, HALF) * dilation)
        oh = (bkt[:, None] == jnp.arange(128)[None, :]) & band[:, None]
        out.append(oh.astype(F32))
    return jnp.stack(out, axis=0)


def _wgather(w_in, w_out, conv_w):
    rin, rout = w_in.shape[0] // 2, w_out.shape[0] // 2

    def body(win_ref, wout_ref, cw_ref, win_o, wout_o, cw_o, send_sems, recv_sems):
        x, y, c = lax.axis_index("x"), lax.axis_index("y"), lax.axis_index("c")
        b = 2 * x + y
        win_o[b] = win_ref[...].astype(BF16)
        wout_o[b] = wout_ref[...].astype(BF16)
        cw_o[b] = cw_ref[...]
        xnbr, ynbr, diag, sib = (1 - x, y, c), (x, 1 - y, c), (1 - x, 1 - y, c), (x, y, 1 - c)
        bx, by, bd = b ^ 2, b ^ 1, b ^ 3

        def copy(sem, ref, to):
            return pltpu.make_async_remote_copy(src_ref=ref, dst_ref=ref, send_sem=send_sems.at[sem],
                                                recv_sem=recv_sems.at[sem], device_id=to, device_id_type=MESH)

        def rows_of(ref, half):
            def rows(blk, quarter=None):
                if quarter is None:
                    return ref.at[blk, pl.ds(c * half, half), :]
                return ref.at[blk, pl.ds(c * half + quarter * (half // 2), half // 2), :]
            return rows

        def send_own(ref, half, base):
            rows = rows_of(ref, half)
            own_x, own_y = copy(base + 0, rows(b), xnbr), copy(base + 1, rows(b), ynbr)
            own_x.start()
            own_y.start()
            return [own_x, own_y]

        def relay(ref, half, base):
            rows = rows_of(ref, half)
            copy(base + 0, rows(bx), xnbr).wait_recv()
            pass_y = copy(base + 2, rows(bx, 0), ynbr)
            pass_y.start()
            to_sib = [copy(base + 4, rows(bx), sib)]
            to_sib[-1].start()
            copy(base + 1, rows(by), ynbr).wait_recv()
            pass_x = copy(base + 3, rows(by, 1), xnbr)
            pass_x.start()
            to_sib.append(copy(base + 5, rows(by), sib))
            to_sib[-1].start()
            copy(base + 2, rows(bd, 0), ynbr).wait_recv()
            copy(base + 3, rows(bd, 1), xnbr).wait_recv()
            to_sib.append(copy(base + 6, rows(bd), sib))
            to_sib[-1].start()
            return [pass_y, pass_x] + to_sib

        def from_sibling(ref, half, base):
            for k, blk in enumerate((bx, by, bd)):
                copy(base + 4 + k, ref.at[blk, pl.ds((1 - c) * half, half), :], sib).wait_recv()

        small = [copy(14 + k, cw_o.at[b], to) for k, to in enumerate((xnbr, ynbr, diag))]
        for cp in small:
            cp.start()
        started = send_own(win_o, rin, 0) + send_own(wout_o, rout, 7)
        started += relay(win_o, rin, 0) + relay(wout_o, rout, 7)
        for k, blk in enumerate((bx, by, bd)):
            copy(14 + k, cw_o.at[blk], sib).wait_recv()
        from_sibling(win_o, rin, 0)
        from_sibling(wout_o, rout, 7)
        for cp in small + started:
            cp.wait_send()

    vm = pl.BlockSpec(memory_space=pltpu.VMEM)
    return pl.pallas_call(
        body, name="wgather",
        out_shape=(jax.ShapeDtypeStruct((4,) + w_in.shape, BF16),
                   jax.ShapeDtypeStruct((4,) + w_out.shape, BF16),
                   jax.ShapeDtypeStruct((4,) + conv_w.shape, F32)),
        in_specs=[vm, vm, vm], out_specs=(vm, vm, vm),
        scratch_shapes=[pltpu.SemaphoreType.DMA((17,)), pltpu.SemaphoreType.DMA((17,))],
        compiler_params=_params(),
    )(w_in, w_out, conv_w)


TM_MATMUL = 512
TM_COMBINE = 256


def _resident(shape):
    return pl.BlockSpec(shape, lambda i: (0,) * len(shape), pipeline_mode=pl.Buffered(1))


def _to_slabs(slab, val, j0=0):
    for j in range(val.shape[1] // 128):
        slab[j0 + j] = val[:, 128 * j:128 * (j + 1)]


def _scatter_classes(slab, j0, nj, out_ref, d, part=0, mid=None):
    tm = slab.shape[1]
    n = tm // d
    if d == 4:
        for r in range(d):
            for j in range(nj):
                out_ref[r, part * n:(part + 1) * n, 128 * j:128 * (j + 1)] = (
                    slab[j0 + j, pl.ds(r, n, stride=d), :].astype(out_ref.dtype))
        return
    q = tm // 4
    for lo in range(4):
        for j in range(nj):
            mid[j0 + j, lo * q:(lo + 1) * q, :] = slab[j0 + j, pl.ds(lo, q, stride=4), :]
    for hi in range(4):
        for lo in range(4):
            for j in range(nj):
                out_ref[4 * hi + lo, part * n:(part + 1) * n, 128 * j:128 * (j + 1)] = (
                    mid[j0 + j, pl.ds(lo * q + hi, n, stride=4), :].astype(out_ref.dtype))


def _gather_classes(slab, piece, nj, d, mid=None):
    tm = slab.shape[1]
    n = tm // d
    if d == 4:
        for r in range(d):
            for j in range(nj):
                slab[j, pl.ds(r, n, stride=d), :] = piece(r, j).astype(F32)
    else:
        q = tm // 4
        for hi in range(4):
            for lo in range(4):
                for j in range(nj):
                    mid[j, pl.ds(lo * q + hi, n, stride=4), :] = piece(4 * hi + lo, j).astype(F32)
        for lo in range(4):
            for j in range(nj):
                slab[j, pl.ds(lo, q, stride=4), :] = mid[j, lo * q:(lo + 1) * q, :]
    return jnp.concatenate([slab[j] for j in range(nj)], axis=1)


def _class_spec(d, width, tm):
    return pl.BlockSpec((d, tm // d, width), lambda i: (0, i, 0))


def _proj(x, tgt, norm_w, wblk, qkw, b256):
    s = x.shape[0]
    tm = TM_MATMUL
    nparts = 2
    tp = tm // nparts

    def body(x_ref, t_ref, nw_ref, w_ref, qkw_ref, b_ref, h_o, cg_o, qkr_o, qkn_o, vz_o, qkn4_o, v4_o, qkn16_o,
             v16_o, xmt_o, slabs, mids):
        for part in range(nparts):
            rows = slice(part * tp, (part + 1) * tp)
            slab = slabs.at[part]
            xf = x_ref[rows, :]
            xmt_o[rows, :] = xf - t_ref[rows, :]
            r = lax.rsqrt(jnp.mean(xf * xf, axis=-1, keepdims=True) + EPS)
            h = (xf * r * nw_ref[...]).astype(BF16)
            h_o[rows, :] = h
            p2 = jnp.dot(h, w_ref[2], preferred_element_type=F32)
            qkr_o[rows, :] = p2.astype(BF16)
            ss = _group_sum(p2 * p2, b_ref, split=False)
            rr = lax.rsqrt(ss * (1.0 / HEAD_DIM) + EPS)
            qkn = p2 * rr * qkw_ref[...]
            qkn_o[rows, :] = qkn.astype(BF16)
            _to_slabs(slab, qkn)
            p3 = jnp.dot(h, w_ref[3], preferred_element_type=F32)
            vz_o[rows, :] = p3.astype(BF16)
            _to_slabs(slab, p3[:, 0:512], 8)
            cg_o[rows, 0:1024] = jnp.dot(h, w_ref[0], preferred_element_type=F32).astype(BF16)
            cg_o[rows, 1024:2048] = jnp.dot(h, w_ref[1], preferred_element_type=F32).astype(BF16)
            for d, q_o, v_o in ((4, qkn4_o, v4_o), (16, qkn16_o, v16_o)):
                _scatter_classes(slab, 0, 8, q_o, d, part, mids.at[part])
                _scatter_classes(slab, 8, 4, v_o, d, part, mids.at[part])

    row = lambda w: pl.BlockSpec((tm, w), lambda i: (i, 0))
    full = lambda shp: pl.BlockSpec(shp, lambda i: (0,) * len(shp))
    nat = lambda w: jax.ShapeDtypeStruct((s, w), BF16)
    cls = lambda d, w: jax.ShapeDtypeStruct((d, s // d, w), BF16)
    return pl.pallas_call(
        body, name="proj", grid=(s // tm,),
        out_shape=(nat(1024), nat(2048), nat(1024), nat(1024), nat(1024),
                   cls(4, 1024), cls(4, 512), cls(16, 1024), cls(16, 512), jax.ShapeDtypeStruct((s, 1024), F32)),
        in_specs=[row(1024), row(1024), full((1, 1024)), _resident((4, 1024, 1024)), full((1, 1024)),
                  full((256, 256))],
        out_specs=(row(1024), row(2048), row(1024), row(1024), row(1024),
                   _class_spec(4, 1024, tm), _class_spec(4, 512, tm),
                   _class_spec(16, 1024, tm), _class_spec(16, 512, tm), row(1024)),
        scratch_shapes=[pltpu.VMEM((nparts, 12, tp, 128), F32), pltpu.VMEM((nparts, 12, tp, 128), F32)],
        compiler_params=_params(("parallel",)),
    )(x, tgt, norm_w, wblk, qkw, b256)


def _block_coords(t, i, nsub, nb, length):
    n = t * nsub + i
    q0 = i * QB
    start = pl.multiple_of(jnp.clip(n * QB - HALF, 0, length - KB), HALF)
    variant = jnp.where(n == 0, 0, jnp.where(n == nb - 1, 2, 1))
    return q0, start, variant


def _classes_per_step(r_cls, length, qt):
    return 2 if (length == qt and qt // QB <= 8 and r_cls % 2 == 0) else 1


def _split_heads(a, lo):
    zero = jnp.zeros_like(a)
    return jnp.concatenate([jnp.where(lo, a, zero), jnp.where(lo, zero, a)], axis=0)


def _col_pair(ref, q0, lane):
    return jnp.concatenate([ref[pl.ds(q0, QB), lane:lane + 1],
                            ref[pl.ds(q0, QB), HEAD_DIM + lane:HEAD_DIM + lane + 1]], axis=0)


def _attn_fwd(qkn_l, v_l, bias, gi, name):
    r_cls, length, _ = qkn_l.shape
    qt = min(length, 2048)
    nb, nsub = length // QB, qt // QB
    cb = _classes_per_step(r_cls, length, qt)

    def body(q_ref, k_ref, v_ref, b_ref, o_ref, lse_ref):
        t = pl.program_id(2)
        lo = lax.broadcasted_iota(jnp.int32, (QB, 128), 1) < HEAD_DIM

        starts, logits = [], []
        for ci in range(cb):
            for i in range(nsub):
                _, start, variant = _block_coords(t, i, nsub, nb, length)
                qq = _split_heads(q_ref[ci, i * QB:(i + 1) * QB, :], lo)
                k = k_ref[ci, pl.ds(start, KB), :]
                logits.append(lax.dot_general(qq, k, (((1,), (1,)), ((), ())), preferred_element_type=F32)
                              + b_ref[variant])
                starts.append(start)
        lg = jnp.concatenate(logits, axis=0)
        m = jnp.max(lg, axis=-1, keepdims=True)
        p = jnp.exp(lg - m)
        pb = p.astype(BF16)
        l = jnp.sum(p, axis=-1, keepdims=True)
        lse = jnp.broadcast_to(m + jnp.log(l), (cb * nsub * 2 * QB, 128))
        inv = 1.0 / l
        for ci in range(cb):
            for i in range(nsub):
                j = ci * nsub + i
                rows = slice(2 * QB * j, 2 * QB * (j + 1))
                v = v_ref[ci, pl.ds(starts[j], KB), :]
                pv = jnp.dot(pb[rows], v, preferred_element_type=F32) * inv[rows]
                o_ref[ci, i * QB:(i + 1) * QB, :] = jnp.where(lo, pv[0:QB], pv[QB:2 * QB]).astype(BF16)
                ls = lse[rows]
                lse_ref[ci, i * QB:(i + 1) * QB, :] = jnp.where(lo, ls[0:QB], ls[QB:2 * QB])

    return pl.pallas_call(
        body, name=name, grid=(N_PAIR, r_cls // cb, length // qt),
        out_shape=(jax.ShapeDtypeStruct((r_cls, length, 512), BF16),
                   jax.ShapeDtypeStruct((r_cls, length, 512), F32)),
        in_specs=[pl.BlockSpec((cb, qt, 128), lambda p, r, t: (r, t, p)),
                  pl.BlockSpec((cb, length, 128), lambda p, r, t: (r, 0, 4 + p)),
                  pl.BlockSpec((cb, length, 128), lambda p, r, t: (r, 0, p)),
                  pl.BlockSpec((None, 3, None, 2 * QB, KB), lambda p, r, t: (gi, 0, p, 0, 0))],
        out_specs=(pl.BlockSpec((cb, qt, 128), lambda p, r, t: (r, t, p)),
                   pl.BlockSpec((cb, qt, 128), lambda p, r, t: (r, t, p))),
        compiler_params=_params(("parallel", "parallel", "arbitrary")),
    )(qkn_l, qkn_l, v_l, bias)


def _combine(o_g, lse_g, cg, vz, xmt, wout, cw, cb, b256):
    s = xmt.shape[0]
    tm = TM_COMBINE
    hb = 16
    nt = s // tm

    def body(o1, o4, o16, l1, l4, l16, cg_ref, cgp_ref, cgn_ref, za_ref, xmt_ref, w_ref, cw_ref, cb_ref,
             b_ref, y_o, dout_o, ld1_o, do1_o, dza_o, dgbz_o, dzc_o, loss_o, dcb_o, dcw_o,
             do4_o, ld4_o, do16_o, ld16_o, slab, mid):
        i = pl.program_id(0)

        @pl.when(i == 0)
        def _():
            loss_o[...] = jnp.zeros_like(loss_o)
            dcb_o[...] = jnp.zeros_like(dcb_o)
            dcw_o[...] = jnp.zeros_like(dcw_o)

        u = cg_ref[:, 0:512].astype(F32)
        gb = cg_ref[:, 512:1024].astype(F32)
        gc = cg_ref[:, 1024:1536].astype(F32)
        zc = cg_ref[:, 1536:2048].astype(F32)
        tt = gc * u
        t_prev = cgp_ref[hb - 1:hb, 0:512].astype(F32) * cgp_ref[hb - 1:hb, 1024:1536].astype(F32)
        t_next = cgn_ref[0:1, 0:512].astype(F32) * cgn_ref[0:1, 1024:1536].astype(F32)
        t_prev = jnp.where(i == 0, 0.0, t_prev)
        t_next = jnp.where(i == nt - 1, 0.0, t_next)
        rows = lax.broadcasted_iota(jnp.int32, (tm, 512), 0)
        t_up = jnp.where(rows == 0, t_prev, pltpu.roll(tt, 1, 0))
        t_dn = jnp.where(rows == tm - 1, t_next, pltpu.roll(tt, tm - 1, 0))
        w0, w1, w2 = cw_ref[0:1, :], cw_ref[1:2, :], cw_ref[2:3, :]
        zb = w0 * t_up + w1 * tt + w2 * t_dn + cb_ref[...]
        sg = _sigmoid(zc)
        sz = zc * sg
        y_conv = gb * zb * sz

        a1, p1 = l1[0], o1[0].astype(F32)
        a4 = _gather_classes(slab, lambda r, j: l4[r, :, 128 * j:128 * (j + 1)], 4, 4)
        p4 = _gather_classes(slab, lambda r, j: o4[r, :, 128 * j:128 * (j + 1)], 4, 4)
        a16 = _gather_classes(slab, lambda r, j: l16[r, :, 128 * j:128 * (j + 1)], 4, 16, mid)
        p16 = _gather_classes(slab, lambda r, j: o16[r, :, 128 * j:128 * (j + 1)], 4, 16, mid)
        m = jnp.maximum(jnp.maximum(a1, a4), a16)
        e1, e4, e16 = jnp.exp(a1 - m), jnp.exp(a4 - m), jnp.exp(a16 - m)
        den = e1 + e4 + e16
        lse = m + jnp.log(den)
        o = (e1 * p1 + e4 * p4 + e16 * p16) / den
        za = za_ref[...].astype(F32)
        sga = _sigmoid(za)
        sa = za * sga
        y = jnp.concatenate([y_conv, o * sa], axis=1).astype(BF16)
        y_o[...] = y

        diff = xmt_ref[...] + jnp.dot(y, w_ref[...], preferred_element_type=F32)
        loss_o[...] += (0.5 / D_MODEL) * jnp.sum(diff * diff)
        dout = diff * (1.0 / D_MODEL)
        dout_o[...] = dout
        dy = lax.dot_general(dout.astype(BF16), w_ref[...], (((1,), (1,)), ((), ())), preferred_element_type=F32)
        dyc, dya = dy[:, 0:512], dy[:, 512:1024]

        do = dya * sa
        dza_o[...] = (dya * o * (sga * (1.0 + za * (1.0 - sga)))).astype(BF16)
        lane = lax.broadcasted_iota(jnp.int32, (tm, 512), 1)
        ld = jnp.where((lane & (HEAD_DIM - 1)) < HEAD_DIM // 2, lse, _group_sum(do * o, b_ref))
        do1_o[0] = do.astype(BF16)
        ld1_o[0] = ld
        _to_slabs(slab, do)
        _scatter_classes(slab, 0, 4, do4_o, 4)
        _scatter_classes(slab, 0, 4, do16_o, 16, 0, mid)
        _to_slabs(slab, ld)
        _scatter_classes(slab, 0, 4, ld4_o, 4)
        _scatter_classes(slab, 0, 4, ld16_o, 16, 0, mid)

        dzc = dyc * sz * gb
        dzc_o[...] = dzc.astype(BF16)
        dgbz_o[:, 0:512] = (dyc * sz * zb).astype(BF16)
        dgbz_o[:, 512:1024] = (dyc * gb * zb * (sg * (1.0 + zc * (1.0 - sg)))).astype(BF16)
        dcb_o[...] += jnp.sum(dzc, axis=0, keepdims=True)
        dcw_o[0:1, :] += jnp.sum(dzc * t_up, axis=0, keepdims=True)
        dcw_o[1:2, :] += jnp.sum(dzc * tt, axis=0, keepdims=True)
        dcw_o[2:3, :] += jnp.sum(dzc * t_dn, axis=0, keepdims=True)

    row = lambda w, j=0: pl.BlockSpec((tm, w), lambda i: (i, j))
    full = lambda shp: pl.BlockSpec(shp, lambda i: (0,) * len(shp))
    prev = pl.BlockSpec((hb, 2048), lambda i: (jnp.maximum(i * (tm // hb) - 1, 0), 0))
    nxt = pl.BlockSpec((hb, 2048), lambda i: (jnp.minimum((i + 1) * (tm // hb), s // hb - 1), 0))
    cls = lambda d, dt: jax.ShapeDtypeStruct((d, s // d, 512), dt)
    cspecs = [_class_spec(d, 512, tm) for d in DILATIONS]
    return pl.pallas_call(
        body, name="combine", grid=(nt,),
        out_shape=(jax.ShapeDtypeStruct((s, 1024), BF16), jax.ShapeDtypeStruct((s, 1024), F32),
                   cls(1, F32), cls(1, BF16), jax.ShapeDtypeStruct((s, 512), BF16),
                   jax.ShapeDtypeStruct((s, 1024), BF16), jax.ShapeDtypeStruct((s, 512), BF16),
                   jax.ShapeDtypeStruct((1, 128), F32), jax.ShapeDtypeStruct((1, 512), F32),
                   jax.ShapeDtypeStruct((8, 512), F32),
                   cls(4, BF16), cls(4, F32), cls(16, BF16), cls(16, F32)),
        in_specs=cspecs + cspecs + [row(2048), prev, nxt, row(512, 1), row(1024),
                                    _resident((1024, 1024)), full((8, 512)), full((1, 512)), full((256, 256))],
        out_specs=(row(1024), row(1024), cspecs[0], cspecs[0], row(512), row(1024), row(512),
                   full((1, 128)), full((1, 512)), full((8, 512)),
                   cspecs[1], cspecs[1], cspecs[2], cspecs[2]),
        scratch_shapes=[pltpu.VMEM((4, tm, 128), F32), pltpu.VMEM((4, tm, 128), F32)],
        compiler_params=_params(("arbitrary",)),
    )(*o_g, *lse_g, cg, cg, cg, vz, xmt, wout, cw, cb, b256)


def _attn_bwd(qkn_l, v_l, do_l, ld_l, bias, gi, name):
    r_cls, length, _ = qkn_l.shape
    qt = min(length, 2048 if length <= 4096 else 1024)
    nb, nsub, nt = length // QB, qt // QB, length // qt
    chunk = min(length, 4096)
    nchunk = length // chunk
    cb = _classes_per_step(r_cls, length, qt)
    nrb = r_cls // cb
    nbuf = min(4, 2 * nchunk * cb)

    def body(q_ref, k_ref, v_ref, do_ref, ld_ref, b_ref, dq_ref, dkv_hbm, dsum_ref, dk_acc, dv_acc, stage, sems):
        p_id, r, t = pl.program_id(0), pl.program_id(1), pl.program_id(2)
        lo = lax.broadcasted_iota(jnp.int32, (QB, 128), 1) < HEAD_DIM

        @pl.when(t == 0)
        def _():
            dk_acc[...] = jnp.zeros_like(dk_acc)
            dv_acc[...] = jnp.zeros_like(dv_acc)

        @pl.when((t == 0) & (r == 0))
        def _():
            dsum_ref[...] = jnp.zeros_like(dsum_ref)

        nt_dims = (((1,), (1,)), ((), ()))
        tn_dims = (((0,), (0,)), ((), ()))
        coords, qqs, dds, logits, dps, lcols, dcols = [], [], [], [], [], [], []
        for ci in range(cb):
            for i in range(nsub):
                q0, start, variant = _block_coords(t, i, nsub, nb, length)
                qq = _split_heads(q_ref[ci, q0:q0 + QB, :], lo)
                dd = _split_heads(do_ref[ci, q0:q0 + QB, :], lo)
                k = k_ref[ci, pl.ds(start, KB), :]
                v = v_ref[ci, pl.ds(start, KB), :]
                logits.append(lax.dot_general(qq, k, nt_dims, preferred_element_type=F32) + b_ref[variant])
                dps.append(lax.dot_general(dd, v, nt_dims, preferred_element_type=F32))
                lcols.append(_col_pair(ld_ref.at[ci], q0, 0))
                dcols.append(_col_pair(ld_ref.at[ci], q0, HEAD_DIM // 2))
                coords.append((ci, i, q0, start, variant))
                qqs.append(qq)
                dds.append(dd)
        p = jnp.exp(jnp.concatenate(logits, axis=0) - jnp.concatenate(lcols, axis=0))
        ds = p * (jnp.concatenate(dps, axis=0) - jnp.concatenate(dcols, axis=0))
        pb = p.astype(BF16)
        dsb = ds.astype(BF16)
        middle = None
        for j, (ci, i, q0, start, variant) in enumerate(coords):
            rows = slice(2 * QB * j, 2 * QB * (j + 1))
            if 0 < i < nsub - 1:
                middle = ds[rows] if middle is None else middle + ds[rows]
            else:
                dsum_ref[variant] += ds[rows]
            dqq = jnp.dot(dsb[rows], k_ref[ci, pl.ds(start, KB), :], preferred_element_type=F32)
            dq_ref[ci, q0:q0 + QB, :] = jnp.where(lo, dqq[0:QB], dqq[QB:2 * QB]).astype(BF16)
            dk_acc[ci, pl.ds(start, KB), :] += lax.dot_general(dsb[rows], qqs[j], tn_dims,
                                                               preferred_element_type=F32)
            dv_acc[ci, pl.ds(start, KB), :] += lax.dot_general(pb[rows], dds[j], tn_dims,
                                                               preferred_element_type=F32)
        if middle is not None:
            dsum_ref[1] += middle

        @pl.when(t == nt - 1)
        def _():
            def copy(k):
                ci, which, c = k // (2 * nchunk), (k // nchunk) % 2, k % nchunk
                rows = pl.ds(c * chunk, chunk)
                return pltpu.make_async_copy(stage.at[k % nbuf], dkv_hbm.at[r * cb + ci, p_id, which, rows, :],
                                             sems.at[k % nbuf])

            for k in range(2 * nchunk * cb):
                if k < nbuf:
                    @pl.when((p_id > 0) | (r > 0))
                    def _():
                        copy(k).wait()
                else:
                    copy(k).wait()
                acc = (dk_acc, dv_acc)[(k // nchunk) % 2]
                stage[k % nbuf] = acc[k // (2 * nchunk), pl.ds((k % nchunk) * chunk, chunk), :].astype(BF16)
                copy(k).start()

            @pl.when((p_id == N_PAIR - 1) & (r == nrb - 1))
            def _():
                for k in range(nbuf):
                    copy(k).wait()

    qspec = pl.BlockSpec((cb, qt, 128), lambda p, r, t: (r, t, p))
    return pl.pallas_call(
        body, name=name, grid=(N_PAIR, nrb, nt),
        out_shape=(jax.ShapeDtypeStruct((r_cls, length, 512), BF16),
                   jax.ShapeDtypeStruct((r_cls, N_PAIR, 2, length, 128), BF16),
                   jax.ShapeDtypeStruct((N_PAIR, 3, 2 * QB, KB), F32)),
        in_specs=[qspec,
                  pl.BlockSpec((cb, length, 128), lambda p, r, t: (r, 0, 4 + p)),
                  pl.BlockSpec((cb, length, 128), lambda p, r, t: (r, 0, p)),
                  qspec, qspec,
                  pl.BlockSpec((None, 3, None, 2 * QB, KB), lambda p, r, t: (gi, 0, p, 0, 0))],
        out_specs=(qspec, pl.BlockSpec(memory_space=pl.ANY),
                   pl.BlockSpec((None, 3, 2 * QB, KB), lambda p, r, t: (p, 0, 0, 0))),
        scratch_shapes=[pltpu.VMEM((cb, length, 128), F32), pltpu.VMEM((cb, length, 128), F32),
                        pltpu.VMEM((nbuf, chunk, 128), BF16), pltpu.SemaphoreType.DMA((nbuf,))],
        compiler_params=_params(("arbitrary", "arbitrary", "arbitrary")),
    )(qkn_l, qkn_l, v_l, do_l, ld_l, bias)


def _bwd_tail(dq_g, dkv_g, qkr, qkw, dza, dgbz, dzc, cg, cw, wblk, x, norm_w, dout, b256):
    s = x.shape[0]
    tm = TM_COMBINE
    hb = 16
    nt = s // tm

    def body(dq1, dq4, dq16, dkv1, dkv4, dkv16, qkr_ref, qkw_ref, dza_ref, dgbz_ref, dzc_ref,
             dzp_ref, dzn_ref, u_ref, gc_ref, cw_ref, w_ref, x_ref, nw_ref, dout_ref, b_ref,
             gx_o, dproj_o, dnw_o, dqkw_o, slab, mid):
        i = pl.program_id(0)

        def nat_q(ref, d):
            return _gather_classes(slab, lambda r, j: ref[r, :, 128 * j:128 * (j + 1)], 4, d, mid)

        def nat_kv(ref, d, which):
            return _gather_classes(slab, lambda r, j: ref[r, j, which], 4, d, mid)

        @pl.when(i == 0)
        def _():
            dnw_o[...] = jnp.zeros_like(dnw_o)
            dqkw_o[...] = jnp.zeros_like(dqkw_o)

        dzc = dzc_ref[...].astype(F32)
        d_prev = jnp.where(i == 0, 0.0, dzp_ref[hb - 1:hb, :].astype(F32))
        d_next = jnp.where(i == nt - 1, 0.0, dzn_ref[0:1, :].astype(F32))
        rows = lax.broadcasted_iota(jnp.int32, (tm, 512), 0)
        d_up = jnp.where(rows == 0, d_prev, pltpu.roll(dzc, 1, 0))
        d_dn = jnp.where(rows == tm - 1, d_next, pltpu.roll(dzc, tm - 1, 0))
        dt = cw_ref[0:1, :] * d_dn + cw_ref[1:2, :] * dzc + cw_ref[2:3, :] * d_up
        u = u_ref[...].astype(F32)
        gc = gc_ref[...].astype(F32)
        dproj_o[:, 0:512] = (dt * gc).astype(BF16)
        dproj_o[:, 512:1024] = dgbz_ref[:, 0:512]
        dproj_o[:, 1024:1536] = (dt * u).astype(BF16)
        dproj_o[:, 1536:2048] = dgbz_ref[:, 512:1024]

        dqn = (dq1[0].astype(F32) + nat_q(dq4, 4) + nat_q(dq16, 16)) * (1.0 / 8.0)
        dk1 = jnp.concatenate([dkv1[0, j, 0] for j in range(N_PAIR)], axis=1)
        dv1 = jnp.concatenate([dkv1[0, j, 1] for j in range(N_PAIR)], axis=1)
        dkn = dk1 + nat_kv(dkv4, 4, 0) + nat_kv(dkv16, 16, 0)
        dvn = dv1 + nat_kv(dkv4, 4, 1) + nat_kv(dkv16, 16, 1)
        g = jnp.concatenate([dqn, dkn], axis=1) * qkw_ref[...]
        raw = qkr_ref[...].astype(F32)
        rr = lax.rsqrt(_group_sum(raw * raw, b_ref, split=False) * (1.0 / HEAD_DIM) + EPS)
        proj_gq = _group_sum(g * raw, b_ref) * (1.0 / HEAD_DIM)
        draw = rr * g - raw * (rr * rr * rr) * proj_gq
        dqkw_o[...] += jnp.sum(jnp.concatenate([dqn, dkn], axis=1) * raw * rr, axis=0, keepdims=True)
        dproj_o[:, 2048:3072] = draw.astype(BF16)
        dproj_o[:, 3072:3584] = dvn.astype(BF16)
        dproj_o[:, 3584:4096] = dza_ref[...]

        nt_dims = (((1,), (1,)), ((), ()))
        dh = lax.dot_general(dproj_o[:, 0:1024], w_ref[0], nt_dims, preferred_element_type=F32)
        for b in range(1, 4):
            dh += lax.dot_general(dproj_o[:, 1024 * b:1024 * b + 1024], w_ref[b], nt_dims,
                                  preferred_element_type=F32)

        xf = x_ref[...]
        r = lax.rsqrt(jnp.mean(xf * xf, axis=-1, keepdims=True) + EPS)
        gh = dh * nw_ref[...]
        dnw_o[...] += jnp.sum(dh * xf * r, axis=0, keepdims=True)
        mean_gx = jnp.mean(gh * xf, axis=-1, keepdims=True)
        gx_o[...] = dout_ref[...] + r * gh - xf * (r * r * r) * mean_gx

    row = lambda w, j=0: pl.BlockSpec((tm, w), lambda i: (i, j))
    full = lambda shp: pl.BlockSpec(shp, lambda i: (0,) * len(shp))
    prev = pl.BlockSpec((hb, 512), lambda i: (jnp.maximum(i * (tm // hb) - 1, 0), 0))
    nxt = pl.BlockSpec((hb, 512), lambda i: (jnp.minimum((i + 1) * (tm // hb), s // hb - 1), 0))
    return pl.pallas_call(
        body, name="bwd_tail", grid=(nt,),
        out_shape=(jax.ShapeDtypeStruct((s, 1024), F32), jax.ShapeDtypeStruct((s, 4096), BF16),
                   jax.ShapeDtypeStruct((1, 1024), F32), jax.ShapeDtypeStruct((1, 1024), F32)),
        in_specs=[_class_spec(d, 512, tm) for d in DILATIONS]
        + [pl.BlockSpec((d, N_PAIR, 2, tm // d, 128), lambda i: (0, 0, 0, i, 0)) for d in DILATIONS]
        + [row(1024), full((1, 1024)), row(512), row(1024), row(512), prev, nxt,
           row(512, 0), row(512, 2), full((8, 512)), _resident((4, 1024, 1024)), row(1024),
           full((1, 1024)), row(1024), full((256, 256))],
        out_specs=(row(1024), row(4096), full((1, 1024)), full((1, 1024))),
        scratch_shapes=[pltpu.VMEM((4, tm, 128), F32), pltpu.VMEM((4, tm, 128), F32)],
        compiler_params=_params(("arbitrary",)),
    )(*dq_g, *dkv_g, qkr, qkw, dza, dgbz, dzc, dzc, dzc, cg, cg, cw, wblk, x, norm_w, dout, b256)


def _wgrad(a, b, row_blocked, name):
    s, m = a.shape
    n = b.shape[1]
    tk = 2048
    ncol = min(n, 2048)
    nj, nk = n // ncol, s // tk

    def body(a_ref, b_ref, o_ref, acc):
        kk = pl.program_id(1)

        @pl.when(kk == 0)
        def _():
            acc[...] = jnp.zeros_like(acc)

        acc[...] += lax.dot_general(a_ref[...], b_ref[...].astype(BF16), (((0,), (0,)), ((), ())),
                                    preferred_element_type=F32)

        @pl.when(kk == nk - 1)
        def _():
            blocks, _, rows, _ = o_ref.shape
            for blk in range(blocks):
                for half in range(2):
                    if row_blocked:
                        r0 = (2 * blk + half) * rows
                        o_ref[blk, half] = acc[r0:r0 + rows, :].astype(BF16)
                    else:
                        o_ref[blk, half] = acc[half * rows:(half + 1) * rows,
                                               1024 * blk:1024 * (blk + 1)].astype(BF16)

    if row_blocked:
        out_shape = jax.ShapeDtypeStruct((4, 2, m // 8, 1024), BF16)
        out_spec = pl.BlockSpec((4, 2, m // 8, 1024), lambda j, k: (0, 0, 0, 0))
    else:
        out_shape = jax.ShapeDtypeStruct((n // 1024, 2, m // 2, 1024), BF16)
        out_spec = pl.BlockSpec((ncol // 1024, 2, m // 2, 1024), lambda j, k: (j, 0, 0, 0))
    return pl.pallas_call(
        body, name=name, grid=(nj, nk),
        out_shape=out_shape,
        in_specs=[pl.BlockSpec((tk, m), lambda j, k: (k, 0)), pl.BlockSpec((tk, ncol), lambda j, k: (k, j))],
        out_specs=out_spec,
        scratch_shapes=[pltpu.VMEM((m, ncol), F32)],
        compiler_params=_params(("parallel", "arbitrary")),
    )(a, b)


def _dbias(dsums, onehot_all):
    def body(ds1_ref, ds4_ref, ds16_ref, oh_ref, o_ref):
        @pl.when(pl.program_id(0) == 0)
        def _():
            o_ref[...] = jnp.zeros_like(o_ref)

        hrow = lax.broadcasted_iota(jnp.int32, (8, KB), 0)
        flip = (lax.broadcasted_iota(jnp.int32, (QB, QB), 0)
                + lax.broadcasted_iota(jnp.int32, (QB, QB), 1) == QB - 1).astype(F32)

        def diagonal_sums(tile):
            rev = jnp.dot(flip, tile, preferred_element_type=F32, precision=lax.Precision.HIGHEST)
            sums = jnp.sum(pltpu.roll(rev, 0, 1, stride=1, stride_axis=0), axis=0, keepdims=True)
            return pltpu.roll(sums, KB - (QB - 1), 1)

        for g, ds_ref in enumerate((ds1_ref, ds4_ref, ds16_ref)):
            diag = jnp.zeros((8, KB), F32)
            for p in range(N_PAIR):
                diag = jnp.where(hrow == 2 * p, diagonal_sums(ds_ref[p, 0:QB, :]), diag)
                diag = jnp.where(hrow == 2 * p + 1, diagonal_sums(ds_ref[p, QB:2 * QB, :]), diag)
            o_ref[...] += jnp.dot(diag, oh_ref[g], preferred_element_type=F32, precision=lax.Precision.HIGHEST)

    ds_spec = pl.BlockSpec((N_PAIR, None, 2 * QB, KB), lambda v: (0, v, 0, 0))
    return pl.pallas_call(
        body, name="dbias", grid=(3,),
        out_shape=jax.ShapeDtypeStruct((8, 128), F32),
        in_specs=[ds_spec, ds_spec, ds_spec, pl.BlockSpec((3, None, KB, 128), lambda v: (0, v, 0, 0))],
        out_specs=pl.BlockSpec((8, 128), lambda v: (0, 0)),
        compiler_params=_params(("arbitrary",)),
    )(*dsums, onehot_all)


def _gsync(pw_in, pw_out, small):
    hin, hout = pw_in.shape[2], pw_out.shape[2]
    nsmall = small.shape[0]

    def body(pin_hbm, pout_hbm, small_ref, gin_o, gout_o, small_o,
             mine_in, recv_in, s1_in, r1_in, s2_in, r2_in, mine_out, recv_out, s1_out, r1_out, s2_out, r2_out, gather,
             lsem, asend, arecv, bsend, brecv, csend, crecv, ssend, srecv):
        x, y, c = lax.axis_index("x"), lax.axis_index("y"), lax.axis_index("c")
        b = 2 * x + y
        dev = 4 * x + 2 * y + c
        sib = (x, y, 1 - c)
        xnbr, ynbr = (1 - x, y, c), (x, 1 - y, c)
        bx, by, bd = b ^ 2, b ^ 1, b ^ 3

        def rcopy(src, dst, ssem, rsem, to):
            return pltpu.make_async_remote_copy(src_ref=src, dst_ref=dst, send_sem=ssem, recv_sem=rsem,
                                                device_id=to, device_id_type=MESH)

        gather[dev] = small_ref[...]
        s_sends = []
        for k in range(1, 8):
            to = (x ^ (k >> 2), y ^ ((k >> 1) & 1), c ^ (k & 1))
            cp = rcopy(gather.at[dev], gather.at[dev], ssend.at[k - 1], srecv.at[k - 1], to)
            cp.start()
            s_sends.append(cp)

        a_in = rcopy(pin_hbm.at[:, 1 - c], recv_in, asend.at[0], arecv.at[0], sib)
        a_out = rcopy(pout_hbm.at[:, 1 - c], recv_out, asend.at[1], arecv.at[1], sib)
        a_in.start()
        a_out.start()
        l_in = pltpu.make_async_copy(pin_hbm.at[:, c], mine_in, lsem.at[0])
        l_out = pltpu.make_async_copy(pout_hbm.at[:, c], mine_out, lsem.at[1])
        l_in.start()
        l_out.start()
        l_in.wait()
        l_out.wait()

        def phase_one(a_cp, mine, recv, s1, r1, half, base):
            a_cp.wait_recv()
            q = half // 2
            sends = []
            for part, (peer, blk_peer) in enumerate(((xnbr, bx), (ynbr, by))):
                rows = pl.ds(part * q, q)
                for slot, blk in enumerate((blk_peer, bd)):
                    s1[part, slot] = (mine[blk, rows, :].astype(F32) + recv[blk, rows, :].astype(F32)).astype(BF16)
                cp = rcopy(s1.at[part], r1.at[part], bsend.at[base + part], brecv.at[base + part], peer)
                cp.start()
                sends.append(cp)
            return sends

        def phase_two(p1, mine, recv, r1, s2, r2, half, base):
            q = half // 2
            own, sends = [], []
            for part, (peer, blk_next) in enumerate(((ynbr, by), (xnbr, bx))):
                rows = pl.ds(part * q, q)
                p1[part].wait_recv()
                own.append(mine[b, rows, :].astype(F32) + recv[b, rows, :].astype(F32) + r1[part, 0].astype(F32))
                s2[part] = (mine[blk_next, rows, :].astype(F32) + recv[blk_next, rows, :].astype(F32)
                            + r1[part, 1].astype(F32)).astype(BF16)
                cp = rcopy(s2.at[part], r2.at[part], bsend.at[base + 2 + part], brecv.at[base + 2 + part], peer)
                cp.start()
                sends.append(cp)
            return own, sends

        def stage_c(own, p2, r2, g_o, half, idx):
            q = half // 2
            for part in range(2):
                p2[part].wait_recv()
                g_o[pl.ds(pl.multiple_of(c * half + part * q, q), q), :] = own[part] + r2[part].astype(F32)
            rows = g_o.at[pl.ds(pl.multiple_of(c * half, half), half), :]
            cp = rcopy(rows, rows, csend.at[idx], crecv.at[idx], sib)
            cp.start()
            return cp

        p1_in = phase_one(a_in, mine_in, recv_in, s1_in, r1_in, hin, 0)
        p1_out = phase_one(a_out, mine_out, recv_out, s1_out, r1_out, hout, 4)
        own_in, p2_in = phase_two(p1_in, mine_in, recv_in, r1_in, s2_in, r2_in, hin, 0)
        own_out, p2_out = phase_two(p1_out, mine_out, recv_out, r1_out, s2_out, r2_out, hout, 4)
        c_in = stage_c(own_in, p2_in, r2_in, gin_o, hin, 0)
        c_out = stage_c(own_out, p2_out, r2_out, gout_o, hout, 1)
        b_in, b_out = p1_in + p2_in, p1_out + p2_out

        for cp in s_sends:
            cp.wait_recv()
        tot = gather[0]
        for d in range(1, 8):
            tot = tot + gather[d]
        small_o[...] = tot

        for g_o, half, idx in ((gin_o, hin, 0), (gout_o, hout, 1)):
            other = g_o.at[pl.ds(pl.multiple_of((1 - c) * half, half), half), :]
            rcopy(other, other, csend.at[idx], crecv.at[idx], sib).wait_recv()
        for cp in s_sends + [a_in, a_out] + b_in + b_out + [c_in, c_out]:
            cp.wait_send()

    vm = pl.BlockSpec(memory_space=pltpu.VMEM)
    hbm = pl.BlockSpec(memory_space=pl.ANY)
    return pl.pallas_call(
        body, name="gsync",
        out_shape=(jax.ShapeDtypeStruct((2 * hin, 1024), F32), jax.ShapeDtypeStruct((2 * hout, 1024), F32),
                   jax.ShapeDtypeStruct((nsmall, 128), F32)),
        in_specs=[hbm, hbm, vm], out_specs=(vm, vm, vm),
        scratch_shapes=[pltpu.VMEM((4, hin, 1024), BF16), pltpu.VMEM((4, hin, 1024), BF16),
                        pltpu.VMEM((2, 2, hin // 2, 1024), BF16), pltpu.VMEM((2, 2, hin // 2, 1024), BF16),
                        pltpu.VMEM((2, hin // 2, 1024), BF16), pltpu.VMEM((2, hin // 2, 1024), BF16),
                        pltpu.VMEM((4, hout, 1024), BF16), pltpu.VMEM((4, hout, 1024), BF16),
                        pltpu.VMEM((2, 2, hout // 2, 1024), BF16), pltpu.VMEM((2, 2, hout // 2, 1024), BF16),
                        pltpu.VMEM((2, hout // 2, 1024), BF16), pltpu.VMEM((2, hout // 2, 1024), BF16),
                        pltpu.VMEM((8, nsmall, 128), F32),
                        pltpu.SemaphoreType.DMA((2,)),
                        pltpu.SemaphoreType.DMA((2,)), pltpu.SemaphoreType.DMA((2,)),
                        pltpu.SemaphoreType.DMA((8,)), pltpu.SemaphoreType.DMA((8,)),
                        pltpu.SemaphoreType.DMA((2,)), pltpu.SemaphoreType.DMA((2,)),
                        pltpu.SemaphoreType.DMA((7,)), pltpu.SemaphoreType.DMA((7,))],
        compiler_params=_params(),
    )(pw_in, pw_out, small)


def _adamw_math(w, g, m, v):
    m = ADAM_B1 * m + (1.0 - ADAM_B1) * g
    v = ADAM_B2 * v + (1.0 - ADAM_B2) * (g * g)
    m_hat = m / (1.0 - ADAM_B1 ** ADAM_STEP)
    v_hat = v / (1.0 - ADAM_B2 ** ADAM_STEP)
    delta = -ADAM_LR * (m_hat / (jnp.sqrt(v_hat) + ADAM_EPS) + ADAM_WD * w)
    return delta, m, v


def _adamw(w, g, m, v, name):
    rows, cols = w.shape
    tr = 256 if rows % 256 == 0 else rows

    def body(w_ref, g_ref, m_ref, v_ref, g_o, d_o, m_o, v_o):
        g = g_ref[...]
        d, m2, v2 = _adamw_math(w_ref[...], g, m_ref[...], v_ref[...])
        g_o[...] = g
        d_o[...] = d
        m_o[...] = m2
        v_o[...] = v2

    spec = pl.BlockSpec((tr, cols), lambda i: (i, 0))
    shp = jax.ShapeDtypeStruct((rows, cols), F32)
    return pl.pallas_call(
        body, name=name, grid=(rows // tr,), out_shape=(shp, shp, shp, shp),
        in_specs=[spec] * 4, out_specs=(spec, spec, spec, spec),
        compiler_params=_params(("parallel",)),
    )(w, g, m, v)


def _fold_heads(dqkw):
    def body(x_ref, o_ref):
        xs = x_ref[...]
        sq = xs[0:1] + xs[1:2] + xs[2:3] + xs[3:4]
        sk = xs[4:5] + xs[5:6] + xs[6:7] + xs[7:8]
        both = jnp.concatenate([sq, sk], axis=0)
        o_ref[...] = both + pltpu.roll(both, HEAD_DIM, 1)

    vm = pl.BlockSpec(memory_space=pltpu.VMEM)
    return pl.pallas_call(body, name="fold_heads", out_shape=jax.ShapeDtypeStruct((2, 128), F32),
                          in_specs=[vm], out_specs=vm, compiler_params=_params())(dqkw)


def kernel(x, norm_w, w_in, conv_w, conv_b, q_norm_w, k_norm_w, rel_bias, w_out, loss_target, m_norm_w, m_w_in, m_conv_w, m_conv_b, m_q_norm_w, m_k_norm_w, m_rel_bias, m_w_out, v_norm_w, v_w_in, v_conv_w, v_conv_b, v_q_norm_w, v_k_norm_w, v_rel_bias, v_w_out):
    x2 = x[0]
    tgt = loss_target[0]
    blk = 2 * lax.axis_index("x") + lax.axis_index("y")

    conv_w8 = jnp.pad(conv_w, ((0, 5), (0, 0)))
    wblk, woutblk, cwblk = _wgather(w_in, w_out, conv_w8)
    wout_full = woutblk.reshape(1024, 1024)
    cw_full = cwblk.transpose(1, 0, 2).reshape(8, 512)

    qkw = jnp.concatenate([jnp.tile(q_norm_w, 8) * 0.125, jnp.tile(k_norm_w, 8)])[None, :]
    qkw_raw = jnp.concatenate([jnp.tile(q_norm_w, 8), jnp.tile(k_norm_w, 8)])[None, :]
    gidx = jnp.arange(256) // HEAD_DIM
    b256 = (gidx[:, None] == gidx[None, :]).astype(BF16)

    h, cg, qkr, qkn, vz, qkn4, v4, qkn16, v16, xmt = _proj(x2, tgt, norm_w[None, :], wblk, qkw, b256)

    biases = _bias_tables(rel_bias)
    qkn_l = [qkn[None], qkn4, qkn16]
    v_l = [vz[None], v4, v16]
    o_g, lse_g = [], []
    for gi, d in enumerate(DILATIONS):
        o_l, lse_l = _attn_fwd(qkn_l[gi], v_l[gi], biases, gi, f"attn_fwd_d{d}")
        o_g.append(o_l)
        lse_g.append(lse_l)

    (y, dout, ld1, do1, dza, dgbz, dzc, loss_p, dcb, dcw, do4, ld4, do16, ld16) = _combine(
        o_g, lse_g, cg, vz, xmt, wout_full, cw_full, conv_b[None, :], b256)

    dq_g, dkv_g, dsums = [], [], []
    for gi, (d, do_l, ld_l) in enumerate(zip(DILATIONS, (do1, do4, do16), (ld1, ld4, ld16))):
        dq_l, dkv_l, dsum = _attn_bwd(qkn_l[gi], v_l[gi], do_l, ld_l, biases, gi, f"attn_bwd_d{d}")
        dq_g.append(dq_l)
        dkv_g.append(dkv_l)
        dsums.append(dsum)

    grad_x, dproj, dnw, dqkw = _bwd_tail(dq_g, dkv_g, qkr, qkw_raw, dza, dgbz, dzc, cg, cw_full, wblk,
                                         x2, norm_w[None, :], dout, b256)

    pw_in = _wgrad(h, dproj, False, "wgrad_in")
    pw_out = _wgrad(y, dout, True, "wgrad_out")
    dbias8 = _dbias(dsums, jnp.stack([_diag_bucket_onehot(d) for d in DILATIONS], axis=0))

    small = jnp.concatenate([dnw.reshape(8, 128), dcb.reshape(4, 128), dqkw.reshape(8, 128),
                             dcw[0:3].reshape(12, 128), dbias8, jnp.pad(loss_p, ((0, 7), (0, 0)))], axis=0)
    g_win, g_wout, gsmall = _gsync(pw_in, pw_out, small)

    g_nw = gsmall[0:8].reshape(1024)
    g_cb = gsmall[8:12].reshape(512)
    folded = _fold_heads(gsmall[12:20])
    g_qw, g_kw = folded[0, 0:64], folded[1, 0:64]
    g_cw = lax.dynamic_slice(gsmall[20:32].reshape(3, 512), (0, blk * 128), (3, 128))
    g_rb = gsmall[32:40][:, 0:32].T
    loss = gsmall[40, 0]

    g_win, d_win, nm_win, nv_win = _adamw(w_in, g_win, m_w_in, v_w_in, "adamw_w_in")
    g_wout, d_wout, nm_wout, nv_wout = _adamw(w_out, g_wout, m_w_out, v_w_out, "adamw_w_out")

    def pack(parts):
        rows = [parts[0].reshape(8, 128), parts[1].reshape(4, 128),
                jnp.pad(parts[2], (0, 64))[None, :], jnp.pad(parts[3], (0, 64))[None, :],
                parts[4], jnp.pad(parts[5].T, ((0, 0), (0, 96)))]
        return jnp.concatenate(rows, axis=0)

    ws = pack([norm_w, conv_b, q_norm_w, k_norm_w, conv_w, rel_bias])
    gs = pack([g_nw, g_cb, g_qw, g_kw, g_cw, g_rb])
    ms = pack([m_norm_w, m_conv_b, m_q_norm_w, m_k_norm_w, m_conv_w, m_rel_bias])
    vs = pack([v_norm_w, v_conv_b, v_q_norm_w, v_k_norm_w, v_conv_w, v_rel_bias])
    rpad = lambda a: jnp.pad(a, ((0, 7), (0, 0)))
    _, d_s, nm_s, nv_s = _adamw(rpad(ws), rpad(gs), rpad(ms), rpad(vs), "adamw_small")

    def unpack(a):
        return (a[0:8].reshape(1024), a[12:13, 0:64].reshape(64), a[13:14, 0:64].reshape(64),
                a[14:17], a[8:12].reshape(512), a[17:25, 0:32].T)

    def ordered(nw, win, cw, cb, qw, kw, rb, wout):
        return (nw, win, cw, cb, qw, kw, rb, wout)

    g_un = (g_nw, g_qw, g_kw, g_cw, g_cb, g_rb)
    outs = [loss, grad_x[None]]
    for un, win_v, wout_v in ((g_un, g_win, g_wout), (unpack(d_s), d_win, d_wout),
                              (unpack(nm_s), nm_win, nm_wout), (unpack(nv_s), nv_win, nv_wout)):
        nw, qw, kw, cw, cb, rb = un
        outs.extend(ordered(nw, win_v, cw, cb, qw, kw, rb, wout_v))
    return tuple(outs)
```

```python
import math

import jax
import jax.numpy as jnp
from jax import lax
from jax.experimental import pallas as pl
from jax.experimental.pallas import tpu as pltpu

F32 = jnp.float32
BF16 = jnp.bfloat16
MESH = pl.DeviceIdType.MESH

D_MODEL = 1024
CONV_W = 512
ATTN_W = 512
HEAD_DIM = 64
N_PAIR = 4
DILATIONS = (1, 4, 16)
HALF = 64
QB = 128
KB = QB + 2 * HALF
NUM_BUCKETS = 32
MAX_DISTANCE = 1024
EPS = 1e-6
NEG = -1e30
ADAM_LR, ADAM_B1, ADAM_B2, ADAM_EPS, ADAM_WD, ADAM_STEP = 0.001, 0.9, 0.999, 1e-08, 0.01, 10
VMEM_LIMIT = 48 << 20


def _params(sem=None, vmem=VMEM_LIMIT, **kw):
    if sem is not None:
        kw["dimension_semantics"] = sem
    return pltpu.CompilerParams(vmem_limit_bytes=vmem, **kw)


def _sigmoid(z):
    return 1.0 / (1.0 + jnp.exp(-z))


def _group_sum(val, b_ref, split=True):
    hi = val.astype(BF16)
    lo = (val - hi.astype(F32)).astype(BF16) if split else None
    outs = []
    for j in range(val.shape[1] // 256):
        sl = slice(256 * j, 256 * j + 256)
        part = jnp.dot(hi[:, sl], b_ref[...], preferred_element_type=F32)
        if split:
            part = part + jnp.dot(lo[:, sl], b_ref[...], preferred_element_type=F32)
        outs.append(part)
    return outs[0] if len(outs) == 1 else jnp.concatenate(outs, axis=1)


def _t5_bucket(rel):
    half_b = NUM_BUCKETS // 2
    max_exact = half_b // 2
    ret = jnp.where(rel > 0, half_b, 0)
    n = jnp.abs(rel)
    nf = jnp.maximum(n, 1).astype(F32)
    large = max_exact + (jnp.log(nf / max_exact) / math.log(MAX_DISTANCE / max_exact)
                         * (half_b - max_exact)).astype(jnp.int32)
    large = jnp.minimum(large, half_b - 1)
    return ret + jnp.where(n < max_exact, n, large)


def _bias_tables(rel_bias):
    rows = []
    key = jnp.arange(KB)
    for dilation in DILATIONS:
        for variant in range(3):
            off = (0, HALF, 2 * HALF)[variant]
            rel = ((key - off + KB // 2) % KB) - KB // 2
            bkt = _t5_bucket(jnp.clip(rel, -HALF, HALF) * dilation)
            rows.append(jnp.where(jnp.abs(rel) <= HALF, bkt, -1))
    bkt_all = jnp.broadcast_to(jnp.stack(rows, axis=0).astype(jnp.int32)[:, None, :], (9, 8, KB))

    def body(rb_ref, bkt_ref, o_ref):
        bkt = bkt_ref[...]
        off = (pl.program_id(0) % 3) * HALF
        rel = (lax.broadcasted_iota(jnp.int32, (QB, KB), 1) - lax.broadcasted_iota(jnp.int32, (QB, KB), 0)) - off
        band = jnp.abs(rel) <= HALF
        for h in range(8):
            acc = jnp.full((8, KB), NEG, F32)
            for b in range(NUM_BUCKETS):
                acc = jnp.where(bkt == b, rb_ref[b, h], acc)
            rolled = pltpu.roll(jnp.broadcast_to(acc[0:1], (QB, KB)), 0, 1, stride=1, stride_axis=0)
            o_ref[h] = jnp.where(band, rolled, NEG)

    out = pl.pallas_call(
        body, name="bias_tables", grid=(9,),
        out_shape=jax.ShapeDtypeStruct((9, 8, QB, KB), F32),
        in_specs=[pl.BlockSpec(memory_space=pltpu.SMEM), pl.BlockSpec((None, 8, KB), lambda i: (i, 0, 0))],
        out_specs=pl.BlockSpec((None, 8, QB, KB), lambda i: (i, 0, 0, 0)),
        compiler_params=_params(("parallel",)),
    )(rel_bias, bkt_all)
    return out.reshape(3, 3, N_PAIR, 2 * QB, KB)


def _diag_bucket_onehot(dilation):
    out = []
    c = jnp.arange(KB)
    for variant in range(3):
        off = (0, HALF, 2 * HALF)[variant]
        rel = ((c - off + 128) % 256) - 128
        band = jnp.abs(rel) <= HALF
        bkt = _t5_bucket(jnp.clip(rel, -HALF, HALF) * dilation)
        oh = (bkt[:, None] == jnp.arange(128)[None, :]) & band[:, None]
        out.append(oh.astype(F32))
    return jnp.stack(out, axis=0)


def _wgather(w_in, w_out, conv_w):
    rin, rout = w_in.shape[0] // 2, w_out.shape[0] // 2

    def body(win_ref, wout_ref, cw_ref, win_o, wout_o, cw_o, send_sems, recv_sems):
        x, y, c = lax.axis_index("x"), lax.axis_index("y"), lax.axis_index("c")
        b = 2 * x + y
        win_o[b] = win_ref[...].astype(BF16)
        wout_o[b] = wout_ref[...].astype(BF16)
        cw_o[b] = cw_ref[...]
        xnbr, ynbr, diag, sib = (1 - x, y, c), (x, 1 - y, c), (1 - x, 1 - y, c), (x, y, 1 - c)
        bx, by, bd = b ^ 2, b ^ 1, b ^ 3

        def copy(sem, ref, to):
            return pltpu.make_async_remote_copy(src_ref=ref, dst_ref=ref, send_sem=send_sems.at[sem],
                                                recv_sem=recv_sems.at[sem], device_id=to, device_id_type=MESH)

        def rows_of(ref, half):
            def rows(blk, quarter=None):
                if quarter is None:
                    return ref.at[blk, pl.ds(c * half, half), :]
                return ref.at[blk, pl.ds(c * half + quarter * (half // 2), half // 2), :]
            return rows

        def send_own(ref, half, base):
            rows = rows_of(ref, half)
            own_x, own_y = copy(base + 0, rows(b), xnbr), copy(base + 1, rows(b), ynbr)
            own_x.start()
            own_y.start()
            return [own_x, own_y]

        def relay(ref, half, base):
            rows = rows_of(ref, half)
            copy(base + 0, rows(bx), xnbr).wait_recv()
            pass_y = copy(base + 2, rows(bx, 0), ynbr)
            pass_y.start()
            to_sib = [copy(base + 4, rows(bx), sib)]
            to_sib[-1].start()
            copy(base + 1, rows(by), ynbr).wait_recv()
            pass_x = copy(base + 3, rows(by, 1), xnbr)
            pass_x.start()
            to_sib.append(copy(base + 5, rows(by), sib))
            to_sib[-1].start()
            copy(base + 2, rows(bd, 0), ynbr).wait_recv()
            copy(base + 3, rows(bd, 1), xnbr).wait_recv()
            to_sib.append(copy(base + 6, rows(bd), sib))
            to_sib[-1].start()
            return [pass_y, pass_x] + to_sib

        def from_sibling(ref, half, base):
            for k, blk in enumerate((bx, by, bd)):
                copy(base + 4 + k, ref.at[blk, pl.ds((1 - c) * half, half), :], sib).wait_recv()

        small = [copy(14 + k, cw_o.at[b], to) for k, to in enumerate((xnbr, ynbr, diag))]
        for cp in small:
            cp.start()
        started = send_own(win_o, rin, 0) + send_own(wout_o, rout, 7)
        started += relay(win_o, rin, 0) + relay(wout_o, rout, 7)
        for k, blk in enumerate((bx, by, bd)):
            copy(14 + k, cw_o.at[blk], sib).wait_recv()
        from_sibling(win_o, rin, 0)
        from_sibling(wout_o, rout, 7)
        for cp in small + started:
            cp.wait_send()

    vm = pl.BlockSpec(memory_space=pltpu.VMEM)
    return pl.pallas_call(
        body, name="wgather",
        out_shape=(jax.ShapeDtypeStruct((4,) + w_in.shape, BF16),
                   jax.ShapeDtypeStruct((4,) + w_out.shape, BF16),
                   jax.ShapeDtypeStruct((4,) + conv_w.shape, F32)),
        in_specs=[vm, vm, vm], out_specs=(vm, vm, vm),
        scratch_shapes=[pltpu.SemaphoreType.DMA((17,)), pltpu.SemaphoreType.DMA((17,))],
        compiler_params=_params(),
    )(w_in, w_out, conv_w)


TM_MATMUL = 512
TM_COMBINE = 256


def _resident(shape):
    return pl.BlockSpec(shape, lambda i: (0,) * len(shape), pipeline_mode=pl.Buffered(1))


def _to_slabs(slab, val, j0=0):
    for j in range(val.shape[1] // 128):
        slab[j0 + j] = val[:, 128 * j:128 * (j + 1)]


def _scatter_classes(slab, j0, nj, out_ref, d, part=0, mid=None):
    tm = slab.shape[1]
    n = tm // d
    if d == 4:
        for r in range(d):
            for j in range(nj):
                out_ref[r, part * n:(part + 1) * n, 128 * j:128 * (j + 1)] = (
                    slab[j0 + j, pl.ds(r, n, stride=d), :].astype(out_ref.dtype))
        return
    q = tm // 4
    for lo in range(4):
        for j in range(nj):
            mid[j0 + j, lo * q:(lo + 1) * q, :] = slab[j0 + j, pl.ds(lo, q, stride=4), :]
    for hi in range(4):
        for lo in range(4):
            for j in range(nj):
                out_ref[4 * hi + lo, part * n:(part + 1) * n, 128 * j:128 * (j + 1)] = (
                    mid[j0 + j, pl.ds(lo * q + hi, n, stride=4), :].astype(out_ref.dtype))


def _gather_classes(slab, piece, nj, d, mid=None):
    tm = slab.shape[1]
    n = tm // d
    if d == 4:
        for r in range(d):
            for j in range(nj):
                slab[j, pl.ds(r, n, stride=d), :] = piece(r, j).astype(F32)
    else:
        q = tm // 4
        for hi in range(4):
            for lo in range(4):
                for j in range(nj):
                    mid[j, pl.ds(lo * q + hi, n, stride=4), :] = piece(4 * hi + lo, j).astype(F32)
        for lo in range(4):
            for j in range(nj):
                slab[j, pl.ds(lo, q, stride=4), :] = mid[j, lo * q:(lo + 1) * q, :]
    return jnp.concatenate([slab[j] for j in range(nj)], axis=1)


def _class_spec(d, width, tm):
    return pl.BlockSpec((d, tm // d, width), lambda i: (0, i, 0))


def _proj(x, tgt, norm_w, wblk, qkw, b256):
    s = x.shape[0]
    tm = TM_MATMUL
    nparts = 2
    tp = tm // nparts

    def body(x_ref, t_ref, nw_ref, w_ref, qkw_ref, b_ref, h_o, cg_o, qkr_o, qkn_o, vz_o, qkn4_o, v4_o, qkn16_o,
             v16_o, xmt_o, slabs, mids):
        for part in range(nparts):
            rows = slice(part * tp, (part + 1) * tp)
            slab = slabs.at[part]
            xf = x_ref[rows, :]
            xmt_o[rows, :] = xf - t_ref[rows, :]
            r = lax.rsqrt(jnp.mean(xf * xf, axis=-1, keepdims=True) + EPS)
            h = (xf * r * nw_ref[...]).astype(BF16)
            h_o[rows, :] = h
            p2 = jnp.dot(h, w_ref[2], preferred_element_type=F32)
            qkr_o[rows, :] = p2.astype(BF16)
            ss = _group_sum(p2 * p2, b_ref, split=False)
            rr = lax.rsqrt(ss * (1.0 / HEAD_DIM) + EPS)
            qkn = p2 * rr * qkw_ref[...]
            qkn_o[rows, :] = qkn.astype(BF16)
            _to_slabs(slab, qkn)
            p3 = jnp.dot(h, w_ref[3], preferred_element_type=F32)
            vz_o[rows, :] = p3.astype(BF16)
            _to_slabs(slab, p3[:, 0:512], 8)
            cg_o[rows, 0:1024] = jnp.dot(h, w_ref[0], preferred_element_type=F32).astype(BF16)
            cg_o[rows, 1024:2048] = jnp.dot(h, w_ref[1], preferred_element_type=F32).astype(BF16)
            for d, q_o, v_o in ((4, qkn4_o, v4_o), (16, qkn16_o, v16_o)):
                _scatter_classes(slab, 0, 8, q_o, d, part, mids.at[part])
                _scatter_classes(slab, 8, 4, v_o, d, part, mids.at[part])

    row = lambda w: pl.BlockSpec((tm, w), lambda i: (i, 0))
    full = lambda shp: pl.BlockSpec(shp, lambda i: (0,) * len(shp))
    nat = lambda w: jax.ShapeDtypeStruct((s, w), BF16)
    cls = lambda d, w: jax.ShapeDtypeStruct((d, s // d, w), BF16)
    return pl.pallas_call(
        body, name="proj", grid=(s // tm,),
        out_shape=(nat(1024), nat(2048), nat(1024), nat(1024), nat(1024),
                   cls(4, 1024), cls(4, 512), cls(16, 1024), cls(16, 512), jax.ShapeDtypeStruct((s, 1024), F32)),
        in_specs=[row(1024), row(1024), full((1, 1024)), _resident((4, 1024, 1024)), full((1, 1024)),
                  full((256, 256))],
        out_specs=(row(1024), row(2048), row(1024), row(1024), row(1024),
                   _class_spec(4, 1024, tm), _class_spec(4, 512, tm),
                   _class_spec(16, 1024, tm), _class_spec(16, 512, tm), row(1024)),
        scratch_shapes=[pltpu.VMEM((nparts, 12, tp, 128), F32), pltpu.VMEM((nparts, 12, tp, 128), F32)],
        compiler_params=_params(("parallel",)),
    )(x, tgt, norm_w, wblk, qkw, b256)


def _block_coords(t, i, nsub, nb, length):
    n = t * nsub + i
    q0 = i * QB
    start = pl.multiple_of(jnp.clip(n * QB - HALF, 0, length - KB), HALF)
    variant = jnp.where(n == 0, 0, jnp.where(n == nb - 1, 2, 1))
    return q0, start, variant


def _classes_per_step(r_cls, length, qt):
    return 2 if (length == qt and qt // QB <= 8 and r_cls % 2 == 0) else 1


def _split_heads(a, lo):
    zero = jnp.zeros_like(a)
    return jnp.concatenate([jnp.where(lo, a, zero), jnp.where(lo, zero, a)], axis=0)


def _col_pair(ref, q0, lane):
    return jnp.concatenate([ref[pl.ds(q0, QB), lane:lane + 1],
                            ref[pl.ds(q0, QB), HEAD_DIM + lane:HEAD_DIM + lane + 1]], axis=0)


def _attn_fwd(qkn_l, v_l, bias, gi, name):
    r_cls, length, _ = qkn_l.shape
    qt = min(length, 2048)
    nb, nsub = length // QB, qt // QB
    cb = _classes_per_step(r_cls, length, qt)

    def body(q_ref, k_ref, v_ref, b_ref, o_ref, lse_ref):
        t = pl.program_id(2)
        lo = lax.broadcasted_iota(jnp.int32, (QB, 128), 1) < HEAD_DIM

        starts, logits = [], []
        for ci in range(cb):
            for i in range(nsub):
                _, start, variant = _block_coords(t, i, nsub, nb, length)
                qq = _split_heads(q_ref[ci, i * QB:(i + 1) * QB, :], lo)
                k = k_ref[ci, pl.ds(start, KB), :]
                logits.append(lax.dot_general(qq, k, (((1,), (1,)), ((), ())), preferred_element_type=F32)
                              + b_ref[variant])
                starts.append(start)
        lg = jnp.concatenate(logits, axis=0)
        m = jnp.max(lg, axis=-1, keepdims=True)
        p = jnp.exp(lg - m)
        pb = p.astype(BF16)
        l = jnp.sum(p, axis=-1, keepdims=True)
        lse = jnp.broadcast_to(m + jnp.log(l), (cb * nsub * 2 * QB, 128))
        inv = 1.0 / l
        for ci in range(cb):
            for i in range(nsub):
                j = ci * nsub + i
                rows = slice(2 * QB * j, 2 * QB * (j + 1))
                v = v_ref[ci, pl.ds(starts[j], KB), :]
                pv = jnp.dot(pb[rows], v, preferred_element_type=F32) * inv[rows]
                o_ref[ci, i * QB:(i + 1) * QB, :] = jnp.where(lo, pv[0:QB], pv[QB:2 * QB]).astype(BF16)
                ls = lse[rows]
                lse_ref[ci, i * QB:(i + 1) * QB, :] = jnp.where(lo, ls[0:QB], ls[QB:2 * QB])

    return pl.pallas_call(
        body, name=name, grid=(N_PAIR, r_cls // cb, length // qt),
        out_shape=(jax.ShapeDtypeStruct((r_cls, length, 512), BF16),
                   jax.ShapeDtypeStruct((r_cls, length, 512), F32)),
        in_specs=[pl.BlockSpec((cb, qt, 128), lambda p, r, t: (r, t, p)),
                  pl.BlockSpec((cb, length, 128), lambda p, r, t: (r, 0, 4 + p)),
                  pl.BlockSpec((cb, length, 128), lambda p, r, t: (r, 0, p)),
                  pl.BlockSpec((None, 3, None, 2 * QB, KB), lambda p, r, t: (gi, 0, p, 0, 0))],
        out_specs=(pl.BlockSpec((cb, qt, 128), lambda p, r, t: (r, t, p)),
                   pl.BlockSpec((cb, qt, 128), lambda p, r, t: (r, t, p))),
        compiler_params=_params(("parallel", "parallel", "arbitrary")),
    )(qkn_l, qkn_l, v_l, bias)


def _combine(o_g, lse_g, cg, vz, xmt, wout, cw, cb, b256):
    s = xmt.shape[0]
    tm = TM_COMBINE
    hb = 16
    nt = s // tm

    def body(o1, o4, o16, l1, l4, l16, cg_hbm, cgp_ref, cgn_ref, za_ref, xmt_hbm, w_ref, cw_ref, cb_ref,
             b_ref, y_o, dout_o, ld1_o, do1_o, dza_o, dgbz_o, dzc_o, loss_o, dcb_o, dcw_o,
             do4_o, ld4_o, do16_o, ld16_o, slab, mid, cg_ring, xmt_ring, ring_sems):
        i = pl.program_id(0)

        def fetch(step, slot):
            rows = pl.ds(pl.multiple_of(step * tm, tm), tm)
            return (pltpu.make_async_copy(cg_hbm.at[rows, :], cg_ring.at[slot], ring_sems.at[0, slot]),
                    pltpu.make_async_copy(xmt_hbm.at[rows, :], xmt_ring.at[slot], ring_sems.at[1, slot]))

        @pl.when(i == 0)
        def _():
            loss_o[...] = jnp.zeros_like(loss_o)
            dcb_o[...] = jnp.zeros_like(dcb_o)
            dcw_o[...] = jnp.zeros_like(dcw_o)
            for first in range(2):
                for cp in fetch(first, first):
                    cp.start()

        @pl.when(i + 2 < nt)
        def _():
            for cp in fetch(i + 2, (i + 2) % 3):
                cp.start()

        slot = i % 3
        for cp in fetch(i, slot):
            cp.wait()
        cg_ref = cg_ring.at[slot]
        xmt_ref = xmt_ring.at[slot]

        u = cg_ref[:, 0:512].astype(F32)
        gb = cg_ref[:, 512:1024].astype(F32)
        gc = cg_ref[:, 1024:1536].astype(F32)
        zc = cg_ref[:, 1536:2048].astype(F32)
        tt = gc * u
        t_prev = cgp_ref[hb - 1:hb, 0:512].astype(F32) * cgp_ref[hb - 1:hb, 1024:1536].astype(F32)
        t_next = cgn_ref[0:1, 0:512].astype(F32) * cgn_ref[0:1, 1024:1536].astype(F32)
        t_prev = jnp.where(i == 0, 0.0, t_prev)
        t_next = jnp.where(i == nt - 1, 0.0, t_next)
        rows = lax.broadcasted_iota(jnp.int32, (tm, 512), 0)
        t_up = jnp.where(rows == 0, t_prev, pltpu.roll(tt, 1, 0))
        t_dn = jnp.where(rows == tm - 1, t_next, pltpu.roll(tt, tm - 1, 0))
        w0, w1, w2 = cw_ref[0:1, :], cw_ref[1:2, :], cw_ref[2:3, :]
        zb = w0 * t_up + w1 * tt + w2 * t_dn + cb_ref[...]
        sg = _sigmoid(zc)
        sz = zc * sg
        y_conv = gb * zb * sz

        a1, p1 = l1[0], o1[0].astype(F32)
        a4 = _gather_classes(slab, lambda r, j: l4[r, :, 128 * j:128 * (j + 1)], 4, 4)
        p4 = _gather_classes(slab, lambda r, j: o4[r, :, 128 * j:128 * (j + 1)], 4, 4)
        a16 = _gather_classes(slab, lambda r, j: l16[r, :, 128 * j:128 * (j + 1)], 4, 16, mid)
        p16 = _gather_classes(slab, lambda r, j: o16[r, :, 128 * j:128 * (j + 1)], 4, 16, mid)
        m = jnp.maximum(jnp.maximum(a1, a4), a16)
        e1, e4, e16 = jnp.exp(a1 - m), jnp.exp(a4 - m), jnp.exp(a16 - m)
        den = e1 + e4 + e16
        lse = m + jnp.log(den)
        o = (e1 * p1 + e4 * p4 + e16 * p16) / den
        za = za_ref[...].astype(F32)
        sga = _sigmoid(za)
        sa = za * sga
        y = jnp.concatenate([y_conv, o * sa], axis=1).astype(BF16)
        y_o[...] = y

        diff = xmt_ref[...] + jnp.dot(y, w_ref[...], preferred_element_type=F32)
        loss_o[...] += (0.5 / D_MODEL) * jnp.sum(diff * diff)
        dout = diff * (1.0 / D_MODEL)
        dout_o[...] = dout
        dy = lax.dot_general(dout.astype(BF16), w_ref[...], (((1,), (1,)), ((), ())), preferred_element_type=F32)
        dyc, dya = dy[:, 0:512], dy[:, 512:1024]

        do = dya * sa
        dza_o[...] = (dya * o * (sga * (1.0 + za * (1.0 - sga)))).astype(BF16)
        lane = lax.broadcasted_iota(jnp.int32, (tm, 512), 1)
        ld = jnp.where((lane & (HEAD_DIM - 1)) < HEAD_DIM // 2, lse, _group_sum(do * o, b_ref))
        do1_o[0] = do.astype(BF16)
        ld1_o[0] = ld
        _to_slabs(slab, do)
        _scatter_classes(slab, 0, 4, do4_o, 4)
        _scatter_classes(slab, 0, 4, do16_o, 16, 0, mid)
        _to_slabs(slab, ld)
        _scatter_classes(slab, 0, 4, ld4_o, 4)
        _scatter_classes(slab, 0, 4, ld16_o, 16, 0, mid)

        dzc = dyc * sz * gb
        dzc_o[...] = dzc.astype(BF16)
        dgbz_o[:, 0:512] = (dyc * sz * zb).astype(BF16)
        dgbz_o[:, 512:1024] = (dyc * gb * zb * (sg * (1.0 + zc * (1.0 - sg)))).astype(BF16)
        dcb_o[...] += jnp.sum(dzc, axis=0, keepdims=True)
        dcw_o[0:1, :] += jnp.sum(dzc * t_up, axis=0, keepdims=True)
        dcw_o[1:2, :] += jnp.sum(dzc * tt, axis=0, keepdims=True)
        dcw_o[2:3, :] += jnp.sum(dzc * t_dn, axis=0, keepdims=True)

    row = lambda w, j=0: pl.BlockSpec((tm, w), lambda i: (i, j))
    full = lambda shp: pl.BlockSpec(shp, lambda i: (0,) * len(shp))
    prev = pl.BlockSpec((hb, 2048), lambda i: (jnp.maximum(i * (tm // hb) - 1, 0), 0))
    nxt = pl.BlockSpec((hb, 2048), lambda i: (jnp.minimum((i + 1) * (tm // hb), s // hb - 1), 0))
    cls = lambda d, dt: jax.ShapeDtypeStruct((d, s // d, 512), dt)
    cspecs = [_class_spec(d, 512, tm) for d in DILATIONS]
    return pl.pallas_call(
        body, name="combine", grid=(nt,),
        out_shape=(jax.ShapeDtypeStruct((s, 1024), BF16), jax.ShapeDtypeStruct((s, 1024), F32),
                   cls(1, F32), cls(1, BF16), jax.ShapeDtypeStruct((s, 512), BF16),
                   jax.ShapeDtypeStruct((s, 1024), BF16), jax.ShapeDtypeStruct((s, 512), BF16),
                   jax.ShapeDtypeStruct((1, 128), F32), jax.ShapeDtypeStruct((1, 512), F32),
                   jax.ShapeDtypeStruct((8, 512), F32),
                   cls(4, BF16), cls(4, F32), cls(16, BF16), cls(16, F32)),
        in_specs=cspecs + cspecs + [pl.BlockSpec(memory_space=pl.ANY), prev, nxt, row(512, 1),
                                    pl.BlockSpec(memory_space=pl.ANY),
                                    _resident((1024, 1024)), full((8, 512)), full((1, 512)), full((256, 256))],
        out_specs=(row(1024), row(1024), cspecs[0], cspecs[0], row(512), row(1024), row(512),
                   full((1, 128)), full((1, 512)), full((8, 512)),
                   cspecs[1], cspecs[1], cspecs[2], cspecs[2]),
        scratch_shapes=[pltpu.VMEM((4, tm, 128), F32), pltpu.VMEM((4, tm, 128), F32),
                        pltpu.VMEM((3, tm, 2048), BF16), pltpu.VMEM((3, tm, 1024), F32),
                        pltpu.SemaphoreType.DMA((2, 3))],
        compiler_params=_params(("arbitrary",)),
    )(*o_g, *lse_g, cg, cg, cg, vz, xmt, wout, cw, cb, b256)


def _attn_bwd(qkn_l, v_l, do_l, ld_l, bias, gi, name):
    r_cls, length, _ = qkn_l.shape
    qt = min(length, 2048 if length <= 4096 else 1024)
    nb, nsub, nt = length // QB, qt // QB, length // qt
    chunk = min(length, 4096)
    nchunk = length // chunk
    cb = _classes_per_step(r_cls, length, qt)
    nrb = r_cls // cb
    nbuf = min(4, 2 * nchunk * cb)

    def body(q_ref, k_ref, v_ref, do_ref, ld_ref, b_ref, dq_ref, dkv_hbm, dsum_ref, dk_acc, dv_acc, stage, sems):
        p_id, r, t = pl.program_id(0), pl.program_id(1), pl.program_id(2)
        lo = lax.broadcasted_iota(jnp.int32, (QB, 128), 1) < HEAD_DIM

        @pl.when(t == 0)
        def _():
            dk_acc[...] = jnp.zeros_like(dk_acc)
            dv_acc[...] = jnp.zeros_like(dv_acc)

        @pl.when((t == 0) & (r == 0))
        def _():
            dsum_ref[...] = jnp.zeros_like(dsum_ref)

        nt_dims = (((1,), (1,)), ((), ()))
        tn_dims = (((0,), (0,)), ((), ()))
        coords, qqs, dds, logits, dps, lcols, dcols = [], [], [], [], [], [], []
        for ci in range(cb):
            for i in range(nsub):
                q0, start, variant = _block_coords(t, i, nsub, nb, length)
                qq = _split_heads(q_ref[ci, q0:q0 + QB, :], lo)
                dd = _split_heads(do_ref[ci, q0:q0 + QB, :], lo)
                k = k_ref[ci, pl.ds(start, KB), :]
                v = v_ref[ci, pl.ds(start, KB), :]
                logits.append(lax.dot_general(qq, k, nt_dims, preferred_element_type=F32) + b_ref[variant])
                dps.append(lax.dot_general(dd, v, nt_dims, preferred_element_type=F32))
                lcols.append(_col_pair(ld_ref.at[ci], q0, 0))
                dcols.append(_col_pair(ld_ref.at[ci], q0, HEAD_DIM // 2))
                coords.append((ci, i, q0, start, variant))
                qqs.append(qq)
                dds.append(dd)
        p = jnp.exp(jnp.concatenate(logits, axis=0) - jnp.concatenate(lcols, axis=0))
        ds = p * (jnp.concatenate(dps, axis=0) - jnp.concatenate(dcols, axis=0))
        pb = p.astype(BF16)
        dsb = ds.astype(BF16)
        middle = None
        for j, (ci, i, q0, start, variant) in enumerate(coords):
            rows = slice(2 * QB * j, 2 * QB * (j + 1))
            if 0 < i < nsub - 1:
                middle = ds[rows] if middle is None else middle + ds[rows]
            else:
                dsum_ref[variant] += ds[rows]
            dqq = jnp.dot(dsb[rows], k_ref[ci, pl.ds(start, KB), :], preferred_element_type=F32)
            dq_ref[ci, q0:q0 + QB, :] = jnp.where(lo, dqq[0:QB], dqq[QB:2 * QB]).astype(BF16)
            dk_acc[ci, pl.ds(start, KB), :] += lax.dot_general(dsb[rows], qqs[j], tn_dims,
                                                               preferred_element_type=F32)
            dv_acc[ci, pl.ds(start, KB), :] += lax.dot_general(pb[rows], dds[j], tn_dims,
                                                               preferred_element_type=F32)
        if middle is not None:
            dsum_ref[1] += middle

        @pl.when(t == nt - 1)
        def _():
            def copy(k):
                ci, which, c = k // (2 * nchunk), (k // nchunk) % 2, k % nchunk
                rows = pl.ds(c * chunk, chunk)
                return pltpu.make_async_copy(stage.at[k % nbuf], dkv_hbm.at[r * cb + ci, p_id, which, rows, :],
                                             sems.at[k % nbuf])

            for k in range(2 * nchunk * cb):
                if k < nbuf:
                    @pl.when((p_id > 0) | (r > 0))
                    def _():
                        copy(k).wait()
                else:
                    copy(k).wait()
                acc = (dk_acc, dv_acc)[(k // nchunk) % 2]
                stage[k % nbuf] = acc[k // (2 * nchunk), pl.ds((k % nchunk) * chunk, chunk), :].astype(BF16)
                copy(k).start()

            @pl.when((p_id == N_PAIR - 1) & (r == nrb - 1))
            def _():
                for k in range(nbuf):
                    copy(k).wait()

    qspec = pl.BlockSpec((cb, qt, 128), lambda p, r, t: (r, t, p))
    return pl.pallas_call(
        body, name=name, grid=(N_PAIR, nrb, nt),
        out_shape=(jax.ShapeDtypeStruct((r_cls, length, 512), BF16),
                   jax.ShapeDtypeStruct((r_cls, N_PAIR, 2, length, 128), BF16),
                   jax.ShapeDtypeStruct((N_PAIR, 3, 2 * QB, KB), F32)),
        in_specs=[qspec,
                  pl.BlockSpec((cb, length, 128), lambda p, r, t: (r, 0, 4 + p)),
                  pl.BlockSpec((cb, length, 128), lambda p, r, t: (r, 0, p)),
                  qspec, qspec,
                  pl.BlockSpec((None, 3, None, 2 * QB, KB), lambda p, r, t: (gi, 0, p, 0, 0))],
        out_specs=(qspec, pl.BlockSpec(memory_space=pl.ANY),
                   pl.BlockSpec((None, 3, 2 * QB, KB), lambda p, r, t: (p, 0, 0, 0))),
        scratch_shapes=[pltpu.VMEM((cb, length, 128), F32), pltpu.VMEM((cb, length, 128), F32),
                        pltpu.VMEM((nbuf, chunk, 128), BF16), pltpu.SemaphoreType.DMA((nbuf,))],
        compiler_params=_params(("arbitrary", "arbitrary", "arbitrary")),
    )(qkn_l, qkn_l, v_l, do_l, ld_l, bias)


def _bwd_tail(dq_g, dkv_g, qkr, qkw, dza, dgbz, dzc, cg, cw, wblk, x, norm_w, dout, b256):
    s = x.shape[0]
    tm = TM_COMBINE
    hb = 16
    nt = s // tm

    def body(dq1, dq4, dq16, dkv1, dkv4, dkv16, qkr_ref, qkw_ref, dza_ref, dgbz_ref, dzc_ref,
             dzp_ref, dzn_ref, u_ref, gc_ref, cw_ref, w_ref, x_ref, nw_ref, dout_ref, b_ref,
             gx_o, dproj_o, dnw_o, dqkw_o, slab, mid):
        i = pl.program_id(0)

        def nat_q(ref, d):
            return _gather_classes(slab, lambda r, j: ref[r, :, 128 * j:128 * (j + 1)], 4, d, mid)

        def nat_kv(ref, d, which):
            return _gather_classes(slab, lambda r, j: ref[r, j, which], 4, d, mid)

        @pl.when(i == 0)
        def _():
            dnw_o[...] = jnp.zeros_like(dnw_o)
            dqkw_o[...] = jnp.zeros_like(dqkw_o)

        dzc = dzc_ref[...].astype(F32)
        d_prev = jnp.where(i == 0, 0.0, dzp_ref[hb - 1:hb, :].astype(F32))
        d_next = jnp.where(i == nt - 1, 0.0, dzn_ref[0:1, :].astype(F32))
        rows = lax.broadcasted_iota(jnp.int32, (tm, 512), 0)
        d_up = jnp.where(rows == 0, d_prev, pltpu.roll(dzc, 1, 0))
        d_dn = jnp.where(rows == tm - 1, d_next, pltpu.roll(dzc, tm - 1, 0))
        dt = cw_ref[0:1, :] * d_dn + cw_ref[1:2, :] * dzc + cw_ref[2:3, :] * d_up
        u = u_ref[...].astype(F32)
        gc = gc_ref[...].astype(F32)
        dproj_o[:, 0:512] = (dt * gc).astype(BF16)
        dproj_o[:, 512:1024] = dgbz_ref[:, 0:512]
        dproj_o[:, 1024:1536] = (dt * u).astype(BF16)
        dproj_o[:, 1536:2048] = dgbz_ref[:, 512:1024]

        dqn = (dq1[0].astype(F32) + nat_q(dq4, 4) + nat_q(dq16, 16)) * (1.0 / 8.0)
        dk1 = jnp.concatenate([dkv1[0, j, 0] for j in range(N_PAIR)], axis=1)
        dv1 = jnp.concatenate([dkv1[0, j, 1] for j in range(N_PAIR)], axis=1)
        dkn = dk1 + nat_kv(dkv4, 4, 0) + nat_kv(dkv16, 16, 0)
        dvn = dv1 + nat_kv(dkv4, 4, 1) + nat_kv(dkv16, 16, 1)
        g = jnp.concatenate([dqn, dkn], axis=1) * qkw_ref[...]
        raw = qkr_ref[...].astype(F32)
        rr = lax.rsqrt(_group_sum(raw * raw, b_ref, split=False) * (1.0 / HEAD_DIM) + EPS)
        proj_gq = _group_sum(g * raw, b_ref) * (1.0 / HEAD_DIM)
        draw = rr * g - raw * (rr * rr * rr) * proj_gq
        dqkw_o[...] += jnp.sum(jnp.concatenate([dqn, dkn], axis=1) * raw * rr, axis=0, keepdims=True)
        dproj_o[:, 2048:3072] = draw.astype(BF16)
        dproj_o[:, 3072:3584] = dvn.astype(BF16)
        dproj_o[:, 3584:4096] = dza_ref[...]

        nt_dims = (((1,), (1,)), ((), ()))
        dh = lax.dot_general(dproj_o[:, 0:1024], w_ref[0], nt_dims, preferred_element_type=F32)
        for b in range(1, 4):
            dh += lax.dot_general(dproj_o[:, 1024 * b:1024 * b + 1024], w_ref[b], nt_dims,
                                  preferred_element_type=F32)

        xf = x_ref[...]
        r = lax.rsqrt(jnp.mean(xf * xf, axis=-1, keepdims=True) + EPS)
        gh = dh * nw_ref[...]
        dnw_o[...] += jnp.sum(dh * xf * r, axis=0, keepdims=True)
        mean_gx = jnp.mean(gh * xf, axis=-1, keepdims=True)
        gx_o[...] = dout_ref[...] + r * gh - xf * (r * r * r) * mean_gx

    row = lambda w, j=0: pl.BlockSpec((tm, w), lambda i: (i, j))
    full = lambda shp: pl.BlockSpec(shp, lambda i: (0,) * len(shp))
    prev = pl.BlockSpec((hb, 512), lambda i: (jnp.maximum(i * (tm // hb) - 1, 0), 0))
    nxt = pl.BlockSpec((hb, 512), lambda i: (jnp.minimum((i + 1) * (tm // hb), s // hb - 1), 0))
    return pl.pallas_call(
        body, name="bwd_tail", grid=(nt,),
        out_shape=(jax.ShapeDtypeStruct((s, 1024), F32), jax.ShapeDtypeStruct((s, 4096), BF16),
                   jax.ShapeDtypeStruct((1, 1024), F32), jax.ShapeDtypeStruct((1, 1024), F32)),
        in_specs=[_class_spec(d, 512, tm) for d in DILATIONS]
        + [pl.BlockSpec((d, N_PAIR, 2, tm // d, 128), lambda i: (0, 0, 0, i, 0)) for d in DILATIONS]
        + [row(1024), full((1, 1024)), row(512), row(1024), row(512), prev, nxt,
           row(512, 0), row(512, 2), full((8, 512)), _resident((4, 1024, 1024)), row(1024),
           full((1, 1024)), row(1024), full((256, 256))],
        out_specs=(row(1024), row(4096), full((1, 1024)), full((1, 1024))),
        scratch_shapes=[pltpu.VMEM((4, tm, 128), F32), pltpu.VMEM((4, tm, 128), F32)],
        compiler_params=_params(("arbitrary",)),
    )(*dq_g, *dkv_g, qkr, qkw, dza, dgbz, dzc, dzc, dzc, cg, cg, cw, wblk, x, norm_w, dout, b256)


def _wgrad(a, b, row_blocked, name):
    s, m = a.shape
    n = b.shape[1]
    tk = 2048
    ncol = min(n, 2048)
    nj, nk = n // ncol, s // tk

    def body(a_ref, b_ref, o_ref, acc):
        kk = pl.program_id(1)

        @pl.when(kk == 0)
        def _():
            acc[...] = jnp.zeros_like(acc)

        acc[...] += lax.dot_general(a_ref[...], b_ref[...].astype(BF16), (((0,), (0,)), ((), ())),
                                    preferred_element_type=F32)

        @pl.when(kk == nk - 1)
        def _():
            blocks, _, rows, _ = o_ref.shape
            for blk in range(blocks):
                for half in range(2):
                    if row_blocked:
                        r0 = (2 * blk + half) * rows
                        o_ref[blk, half] = acc[r0:r0 + rows, :].astype(BF16)
                    else:
                        o_ref[blk, half] = acc[half * rows:(half + 1) * rows,
                                               1024 * blk:1024 * (blk + 1)].astype(BF16)

    if row_blocked:
        out_shape = jax.ShapeDtypeStruct((4, 2, m // 8, 1024), BF16)
        out_spec = pl.BlockSpec((4, 2, m // 8, 1024), lambda j, k: (0, 0, 0, 0))
    else:
        out_shape = jax.ShapeDtypeStruct((n // 1024, 2, m // 2, 1024), BF16)
        out_spec = pl.BlockSpec((ncol // 1024, 2, m // 2, 1024), lambda j, k: (j, 0, 0, 0))
    return pl.pallas_call(
        body, name=name, grid=(nj, nk),
        out_shape=out_shape,
        in_specs=[pl.BlockSpec((tk, m), lambda j, k: (k, 0)), pl.BlockSpec((tk, ncol), lambda j, k: (k, j))],
        out_specs=out_spec,
        scratch_shapes=[pltpu.VMEM((m, ncol), F32)],
        compiler_params=_params(("parallel", "arbitrary")),
    )(a, b)


def _dbias(dsums, onehot_all):
    def body(ds1_ref, ds4_ref, ds16_ref, oh_ref, o_ref):
        @pl.when(pl.program_id(0) == 0)
        def _():
            o_ref[...] = jnp.zeros_like(o_ref)

        hrow = lax.broadcasted_iota(jnp.int32, (8, KB), 0)
        flip = (lax.broadcasted_iota(jnp.int32, (QB, QB), 0)
                + lax.broadcasted_iota(jnp.int32, (QB, QB), 1) == QB - 1).astype(F32)

        def diagonal_sums(tile):
            rev = jnp.dot(flip, tile, preferred_element_type=F32, precision=lax.Precision.HIGHEST)
            sums = jnp.sum(pltpu.roll(rev, 0, 1, stride=1, stride_axis=0), axis=0, keepdims=True)
            return pltpu.roll(sums, KB - (QB - 1), 1)

        for g, ds_ref in enumerate((ds1_ref, ds4_ref, ds16_ref)):
            diag = jnp.zeros((8, KB), F32)
            for p in range(N_PAIR):
                diag = jnp.where(hrow == 2 * p, diagonal_sums(ds_ref[p, 0:QB, :]), diag)
                diag = jnp.where(hrow == 2 * p + 1, diagonal_sums(ds_ref[p, QB:2 * QB, :]), diag)
            o_ref[...] += jnp.dot(diag, oh_ref[g], preferred_element_type=F32, precision=lax.Precision.HIGHEST)

    ds_spec = pl.BlockSpec((N_PAIR, None, 2 * QB, KB), lambda v: (0, v, 0, 0))
    return pl.pallas_call(
        body, name="dbias", grid=(3,),
        out_shape=jax.ShapeDtypeStruct((8, 128), F32),
        in_specs=[ds_spec, ds_spec, ds_spec, pl.BlockSpec((3, None, KB, 128), lambda v: (0, v, 0, 0))],
        out_specs=pl.BlockSpec((8, 128), lambda v: (0, 0)),
        compiler_params=_params(("arbitrary",)),
    )(*dsums, onehot_all)


def _gsync(pw_in, pw_out, small):
    hin, hout = pw_in.shape[2], pw_out.shape[2]
    nsmall = small.shape[0]

    def body(pin_hbm, pout_hbm, small_ref, gin_o, gout_o, small_o,
             mine_in, recv_in, s1_in, r1_in, s2_in, r2_in, mine_out, recv_out, s1_out, r1_out, s2_out, r2_out, gather,
             lsem, asend, arecv, bsend, brecv, csend, crecv, ssend, srecv):
        x, y, c = lax.axis_index("x"), lax.axis_index("y"), lax.axis_index("c")
        b = 2 * x + y
        dev = 4 * x + 2 * y + c
        sib = (x, y, 1 - c)
        xnbr, ynbr = (1 - x, y, c), (x, 1 - y, c)
        bx, by, bd = b ^ 2, b ^ 1, b ^ 3

        def rcopy(src, dst, ssem, rsem, to):
            return pltpu.make_async_remote_copy(src_ref=src, dst_ref=dst, send_sem=ssem, recv_sem=rsem,
                                                device_id=to, device_id_type=MESH)

        gather[dev] = small_ref[...]
        s_sends = []
        for k in range(1, 8):
            to = (x ^ (k >> 2), y ^ ((k >> 1) & 1), c ^ (k & 1))
            cp = rcopy(gather.at[dev], gather.at[dev], ssend.at[k - 1], srecv.at[k - 1], to)
            cp.start()
            s_sends.append(cp)

        a_in = rcopy(pin_hbm.at[:, 1 - c], recv_in, asend.at[0], arecv.at[0], sib)
        a_out = rcopy(pout_hbm.at[:, 1 - c], recv_out, asend.at[1], arecv.at[1], sib)
        a_in.start()
        a_out.start()
        l_in = pltpu.make_async_copy(pin_hbm.at[:, c], mine_in, lsem.at[0])
        l_out = pltpu.make_async_copy(pout_hbm.at[:, c], mine_out, lsem.at[1])
        l_in.start()
        l_out.start()
        l_in.wait()
        l_out.wait()

        def phase_one(a_cp, mine, recv, s1, r1, half, base):
            a_cp.wait_recv()
            q = half // 2
            sends = []
            for part, (peer, blk_peer) in enumerate(((xnbr, bx), (ynbr, by))):
                rows = pl.ds(part * q, q)
                for slot, blk in enumerate((blk_peer, bd)):
                    s1[part, slot] = (mine[blk, rows, :].astype(F32) + recv[blk, rows, :].astype(F32)).astype(BF16)
                cp = rcopy(s1.at[part], r1.at[part], bsend.at[base + part], brecv.at[base + part], peer)
                cp.start()
                sends.append(cp)
            return sends

        def phase_two(p1, mine, recv, r1, s2, r2, half, base):
            q = half // 2
            own, sends = [], []
            for part, (peer, blk_next) in enumerate(((ynbr, by), (xnbr, bx))):
                rows = pl.ds(part * q, q)
                p1[part].wait_recv()
                own.append(mine[b, rows, :].astype(F32) + recv[b, rows, :].astype(F32) + r1[part, 0].astype(F32))
                s2[part] = (mine[blk_next, rows, :].astype(F32) + recv[blk_next, rows, :].astype(F32)
                            + r1[part, 1].astype(F32)).astype(BF16)
                cp = rcopy(s2.at[part], r2.at[part], bsend.at[base + 2 + part], brecv.at[base + 2 + part], peer)
                cp.start()
                sends.append(cp)
            return own, sends

        def stage_c(own, p2, r2, g_o, half, idx):
            q = half // 2
            for part in range(2):
                p2[part].wait_recv()
                g_o[pl.ds(pl.multiple_of(c * half + part * q, q), q), :] = own[part] + r2[part].astype(F32)
            rows = g_o.at[pl.ds(pl.multiple_of(c * half, half), half), :]
            cp = rcopy(rows, rows, csend.at[idx], crecv.at[idx], sib)
            cp.start()
            return cp

        p1_in = phase_one(a_in, mine_in, recv_in, s1_in, r1_in, hin, 0)
        p1_out = phase_one(a_out, mine_out, recv_out, s1_out, r1_out, hout, 4)
        own_in, p2_in = phase_two(p1_in, mine_in, recv_in, r1_in, s2_in, r2_in, hin, 0)
        own_out, p2_out = phase_two(p1_out, mine_out, recv_out, r1_out, s2_out, r2_out, hout, 4)
        c_in = stage_c(own_in, p2_in, r2_in, gin_o, hin, 0)
        c_out = stage_c(own_out, p2_out, r2_out, gout_o, hout, 1)
        b_in, b_out = p1_in + p2_in, p1_out + p2_out

        for cp in s_sends:
            cp.wait_recv()
        tot = gather[0]
        for d in range(1, 8):
            tot = tot + gather[d]
        small_o[...] = tot

        for g_o, half, idx in ((gin_o, hin, 0), (gout_o, hout, 1)):
            other = g_o.at[pl.ds(pl.multiple_of((1 - c) * half, half), half), :]
            rcopy(other, other, csend.at[idx], crecv.at[idx], sib).wait_recv()
        for cp in s_sends + [a_in, a_out] + b_in + b_out + [c_in, c_out]:
            cp.wait_send()

    vm = pl.BlockSpec(memory_space=pltpu.VMEM)
    hbm = pl.BlockSpec(memory_space=pl.ANY)
    return pl.pallas_call(
        body, name="gsync",
        out_shape=(jax.ShapeDtypeStruct((2 * hin, 1024), F32), jax.ShapeDtypeStruct((2 * hout, 1024), F32),
                   jax.ShapeDtypeStruct((nsmall, 128), F32)),
        in_specs=[hbm, hbm, vm], out_specs=(vm, vm, vm),
        scratch_shapes=[pltpu.VMEM((4, hin, 1024), BF16), pltpu.VMEM((4, hin, 1024), BF16),
                        pltpu.VMEM((2, 2, hin // 2, 1024), BF16), pltpu.VMEM((2, 2, hin // 2, 1024), BF16),
                        pltpu.VMEM((2, hin // 2, 1024), BF16), pltpu.VMEM((2, hin // 2, 1024), BF16),
                        pltpu.VMEM((4, hout, 1024), BF16), pltpu.VMEM((4, hout, 1024), BF16),
                        pltpu.VMEM((2, 2, hout // 2, 1024), BF16), pltpu.VMEM((2, 2, hout // 2, 1024), BF16),
                        pltpu.VMEM((2, hout // 2, 1024), BF16), pltpu.VMEM((2, hout // 2, 1024), BF16),
                        pltpu.VMEM((8, nsmall, 128), F32),
                        pltpu.SemaphoreType.DMA((2,)),
                        pltpu.SemaphoreType.DMA((2,)), pltpu.SemaphoreType.DMA((2,)),
                        pltpu.SemaphoreType.DMA((8,)), pltpu.SemaphoreType.DMA((8,)),
                        pltpu.SemaphoreType.DMA((2,)), pltpu.SemaphoreType.DMA((2,)),
                        pltpu.SemaphoreType.DMA((7,)), pltpu.SemaphoreType.DMA((7,))],
        compiler_params=_params(),
    )(pw_in, pw_out, small)


def _adamw_math(w, g, m, v):
    m = ADAM_B1 * m + (1.0 - ADAM_B1) * g
    v = ADAM_B2 * v + (1.0 - ADAM_B2) * (g * g)
    m_hat = m / (1.0 - ADAM_B1 ** ADAM_STEP)
    v_hat = v / (1.0 - ADAM_B2 ** ADAM_STEP)
    delta = -ADAM_LR * (m_hat / (jnp.sqrt(v_hat) + ADAM_EPS) + ADAM_WD * w)
    return delta, m, v


def _adamw(w, g, m, v, name):
    rows, cols = w.shape
    tr = 256 if rows % 256 == 0 else rows

    def body(w_ref, g_ref, m_ref, v_ref, g_o, d_o, m_o, v_o):
        g = g_ref[...]
        d, m2, v2 = _adamw_math(w_ref[...], g, m_ref[...], v_ref[...])
        g_o[...] = g
        d_o[...] = d
        m_o[...] = m2
        v_o[...] = v2

    spec = pl.BlockSpec((tr, cols), lambda i: (i, 0))
    shp = jax.ShapeDtypeStruct((rows, cols), F32)
    return pl.pallas_call(
        body, name=name, grid=(rows // tr,), out_shape=(shp, shp, shp, shp),
        in_specs=[spec] * 4, out_specs=(spec, spec, spec, spec),
        compiler_params=_params(("parallel",)),
    )(w, g, m, v)


def _fold_heads(dqkw):
    def body(x_ref, o_ref):
        xs = x_ref[...]
        sq = xs[0:1] + xs[1:2] + xs[2:3] + xs[3:4]
        sk = xs[4:5] + xs[5:6] + xs[6:7] + xs[7:8]
        both = jnp.concatenate([sq, sk], axis=0)
        o_ref[...] = both + pltpu.roll(both, HEAD_DIM, 1)

    vm = pl.BlockSpec(memory_space=pltpu.VMEM)
    return pl.pallas_call(body, name="fold_heads", out_shape=jax.ShapeDtypeStruct((2, 128), F32),
                          in_specs=[vm], out_specs=vm, compiler_params=_params())(dqkw)


def kernel(x, norm_w, w_in, conv_w, conv_b, q_norm_w, k_norm_w, rel_bias, w_out, loss_target, m_norm_w, m_w_in, m_conv_w, m_conv_b, m_q_norm_w, m_k_norm_w, m_rel_bias, m_w_out, v_norm_w, v_w_in, v_conv_w, v_conv_b, v_q_norm_w, v_k_norm_w, v_rel_bias, v_w_out):
    x2 = x[0]
    tgt = loss_target[0]
    blk = 2 * lax.axis_index("x") + lax.axis_index("y")

    conv_w8 = jnp.pad(conv_w, ((0, 5), (0, 0)))
    wblk, woutblk, cwblk = _wgather(w_in, w_out, conv_w8)
    wout_full = woutblk.reshape(1024, 1024)
    cw_full = cwblk.transpose(1, 0, 2).reshape(8, 512)

    qkw = jnp.concatenate([jnp.tile(q_norm_w, 8) * 0.125, jnp.tile(k_norm_w, 8)])[None, :]
    qkw_raw = jnp.concatenate([jnp.tile(q_norm_w, 8), jnp.tile(k_norm_w, 8)])[None, :]
    gidx = jnp.arange(256) // HEAD_DIM
    b256 = (gidx[:, None] == gidx[None, :]).astype(BF16)

    h, cg, qkr, qkn, vz, qkn4, v4, qkn16, v16, xmt = _proj(x2, tgt, norm_w[None, :], wblk, qkw, b256)

    biases = _bias_tables(rel_bias)
    qkn_l = [qkn[None], qkn4, qkn16]
    v_l = [vz[None], v4, v16]
    o_g, lse_g = [], []
    for gi, d in enumerate(DILATIONS):
        o_l, lse_l = _attn_fwd(qkn_l[gi], v_l[gi], biases, gi, f"attn_fwd_d{d}")
        o_g.append(o_l)
        lse_g.append(lse_l)

    (y, dout, ld1, do1, dza, dgbz, dzc, loss_p, dcb, dcw, do4, ld4, do16, ld16) = _combine(
        o_g, lse_g, cg, vz, xmt, wout_full, cw_full, conv_b[None, :], b256)

    dq_g, dkv_g, dsums = [], [], []
    for gi, (d, do_l, ld_l) in enumerate(zip(DILATIONS, (do1, do4, do16), (ld1, ld4, ld16))):
        dq_l, dkv_l, dsum = _attn_bwd(qkn_l[gi], v_l[gi], do_l, ld_l, biases, gi, f"attn_bwd_d{d}")
        dq_g.append(dq_l)
        dkv_g.append(dkv_l)
        dsums.append(dsum)

    grad_x, dproj, dnw, dqkw = _bwd_tail(dq_g, dkv_g, qkr, qkw_raw, dza, dgbz, dzc, cg, cw_full, wblk,
                                         x2, norm_w[None, :], dout, b256)

    pw_in = _wgrad(h, dproj, False, "wgrad_in")
    pw_out = _wgrad(y, dout, True, "wgrad_out")
    dbias8 = _dbias(dsums, jnp.stack([_diag_bucket_onehot(d) for d in DILATIONS], axis=0))

    small = jnp.concatenate([dnw.reshape(8, 128), dcb.reshape(4, 128), dqkw.reshape(8, 128),
                             dcw[0:3].reshape(12, 128), dbias8, jnp.pad(loss_p, ((0, 7), (0, 0)))], axis=0)
    g_win, g_wout, gsmall = _gsync(pw_in, pw_out, small)

    g_nw = gsmall[0:8].reshape(1024)
    g_cb = gsmall[8:12].reshape(512)
    folded = _fold_heads(gsmall[12:20])
    g_qw, g_kw = folded[0, 0:64], folded[1, 0:64]
    g_cw = lax.dynamic_slice(gsmall[20:32].reshape(3, 512), (0, blk * 128), (3, 128))
    g_rb = gsmall[32:40][:, 0:32].T
    loss = gsmall[40, 0]

    g_win, d_win, nm_win, nv_win = _adamw(w_in, g_win, m_w_in, v_w_in, "adamw_w_in")
    g_wout, d_wout, nm_wout, nv_wout = _adamw(w_out, g_wout, m_w_out, v_w_out, "adamw_w_out")

    def pack(parts):
        rows = [parts[0].reshape(8, 128), parts[1].reshape(4, 128),
                jnp.pad(parts[2], (0, 64))[None, :], jnp.pad(parts[3], (0, 64))[None, :],
                parts[4], jnp.pad(parts[5].T, ((0, 0), (0, 96)))]
        return jnp.concatenate(rows, axis=0)

    ws = pack([norm_w, conv_b, q_norm_w, k_norm_w, conv_w, rel_bias])
    gs = pack([g_nw, g_cb, g_qw, g_kw, g_cw, g_rb])
    ms = pack([m_norm_w, m_conv_b, m_q_norm_w, m_k_norm_w, m_conv_w, m_rel_bias])
    vs = pack([v_norm_w, v_conv_b, v_q_norm_w, v_k_norm_w, v_conv_w, v_rel_bias])
    rpad = lambda a: jnp.pad(a, ((0, 7), (0, 0)))
    _, d_s, nm_s, nv_s = _adamw(rpad(ws), rpad(gs), rpad(ms), rpad(vs), "adamw_small")

    def unpack(a):
        return (a[0:8].reshape(1024), a[12:13, 0:64].reshape(64), a[13:14, 0:64].reshape(64),
                a[14:17], a[8:12].reshape(512), a[17:25, 0:32].T)

    def ordered(nw, win, cw, cb, qw, kw, rb, wout):
        return (nw, win, cw, cb, qw, kw, rb, wout)

    g_un = (g_nw, g_qw, g_kw, g_cw, g_cb, g_rb)
    outs = [loss, grad_x[None]]
    for un, win_v, wout_v in ((g_un, g_win, g_wout), (unpack(d_s), d_win, d_wout),
                              (unpack(nm_s), nm_win, nm_wout), (unpack(nv_s), nv_win, nv_wout)):
        nw, qw, kw, cw, cb, rb = un
        outs.extend(ordered(nw, win_v, cw, cb, qw, kw, rb, wout_v))
    return tuple(outs)
```

```python
import math

import jax
import jax.numpy as jnp
from jax import lax
from jax.experimental import pallas as pl
from jax.experimental.pallas import tpu as pltpu

F32 = jnp.float32
BF16 = jnp.bfloat16
MESH = pl.DeviceIdType.MESH

D_MODEL = 1024
CONV_W = 512
ATTN_W = 512
HEAD_DIM = 64
N_PAIR = 4
DILATIONS = (1, 4, 16)
HALF = 64
QB = 128
KB = QB + 2 * HALF
NUM_BUCKETS = 32
MAX_DISTANCE = 1024
EPS = 1e-6
NEG = -1e30
ADAM_LR, ADAM_B1, ADAM_B2, ADAM_EPS, ADAM_WD, ADAM_STEP = 0.001, 0.9, 0.999, 1e-08, 0.01, 10
VMEM_LIMIT = 48 << 20
VMEM_LIMIT_TAIL = 62 << 20


def _params(sem=None, vmem=VMEM_LIMIT, **kw):
    if sem is not None:
        kw["dimension_semantics"] = sem
    return pltpu.CompilerParams(vmem_limit_bytes=vmem, **kw)


def _sigmoid(z):
    return 1.0 / (1.0 + jnp.exp(-z))


def _group_sum(val, b_ref, split=True):
    hi = val.astype(BF16)
    lo = (val - hi.astype(F32)).astype(BF16) if split else None
    outs = []
    for j in range(val.shape[1] // 256):
        sl = slice(256 * j, 256 * j + 256)
        part = jnp.dot(hi[:, sl], b_ref[...], preferred_element_type=F32)
        if split:
            part = part + jnp.dot(lo[:, sl], b_ref[...], preferred_element_type=F32)
        outs.append(part)
    return outs[0] if len(outs) == 1 else jnp.concatenate(outs, axis=1)


def _t5_bucket(rel):
    half_b = NUM_BUCKETS // 2
    max_exact = half_b // 2
    ret = jnp.where(rel > 0, half_b, 0)
    n = jnp.abs(rel)
    nf = jnp.maximum(n, 1).astype(F32)
    large = max_exact + (jnp.log(nf / max_exact) / math.log(MAX_DISTANCE / max_exact)
                         * (half_b - max_exact)).astype(jnp.int32)
    large = jnp.minimum(large, half_b - 1)
    return ret + jnp.where(n < max_exact, n, large)


def _bias_tables(rel_bias):
    rows = []
    key = jnp.arange(KB)
    for dilation in DILATIONS:
        for variant in range(3):
            off = (0, HALF, 2 * HALF)[variant]
            rel = ((key - off + KB // 2) % KB) - KB // 2
            bkt = _t5_bucket(jnp.clip(rel, -HALF, HALF) * dilation)
            rows.append(jnp.where(jnp.abs(rel) <= HALF, bkt, -1))
    bkt_all = jnp.broadcast_to(jnp.stack(rows, axis=0).astype(jnp.int32)[:, None, :], (9, 8, KB))

    def body(rb_ref, bkt_ref, o_ref):
        bkt = bkt_ref[...]
        off = (pl.program_id(0) % 3) * HALF
        rel = (lax.broadcasted_iota(jnp.int32, (QB, KB), 1) - lax.broadcasted_iota(jnp.int32, (QB, KB), 0)) - off
        band = jnp.abs(rel) <= HALF
        for h in range(8):
            acc = jnp.full((8, KB), NEG, F32)
            for b in range(NUM_BUCKETS):
                acc = jnp.where(bkt == b, rb_ref[b, h], acc)
            rolled = pltpu.roll(jnp.broadcast_to(acc[0:1], (QB, KB)), 0, 1, stride=1, stride_axis=0)
            o_ref[h] = jnp.where(band, rolled, NEG)

    out = pl.pallas_call(
        body, name="bias_tables", grid=(9,),
        out_shape=jax.ShapeDtypeStruct((9, 8, QB, KB), F32),
        in_specs=[pl.BlockSpec(memory_space=pltpu.SMEM), pl.BlockSpec((None, 8, KB), lambda i: (i, 0, 0))],
        out_specs=pl.BlockSpec((None, 8, QB, KB), lambda i: (i, 0, 0, 0)),
        compiler_params=_params(("parallel",)),
    )(rel_bias, bkt_all)
    return out.reshape(3, 3, N_PAIR, 2 * QB, KB)


def _diag_bucket_onehot(dilation):
    out = []
    c = jnp.arange(KB)
    for variant in range(3):
        off = (0, HALF, 2 * HALF)[variant]
        rel = ((c - off + 128) % 256) - 128
        band = jnp.abs(rel) <= HALF
        bkt = _t5_bucket(jnp.clip(rel, -HALF, HALF) * dilation)
        oh = (bkt[:, None] == jnp.arange(128)[None, :]) & band[:, None]
        out.append(oh.astype(F32))
    return jnp.stack(out, axis=0)


def _wgather(w_in, w_out, conv_w):
    rin, rout = w_in.shape[0] // 2, w_out.shape[0] // 2

    def body(win_ref, wout_ref, cw_ref, win_o, wout_o, cw_o, send_sems, recv_sems):
        x, y, c = lax.axis_index("x"), lax.axis_index("y"), lax.axis_index("c")
        b = 2 * x + y
        win_o[b] = win_ref[...].astype(BF16)
        wout_o[b] = wout_ref[...].astype(BF16)
        cw_o[b] = cw_ref[...]
        xnbr, ynbr, diag, sib = (1 - x, y, c), (x, 1 - y, c), (1 - x, 1 - y, c), (x, y, 1 - c)
        bx, by, bd = b ^ 2, b ^ 1, b ^ 3

        def copy(sem, ref, to):
            return pltpu.make_async_remote_copy(src_ref=ref, dst_ref=ref, send_sem=send_sems.at[sem],
                                                recv_sem=recv_sems.at[sem], device_id=to, device_id_type=MESH)

        def rows_of(ref, half):
            def rows(blk, quarter=None):
                if quarter is None:
                    return ref.at[blk, pl.ds(c * half, half), :]
                return ref.at[blk, pl.ds(c * half + quarter * (half // 2), half // 2), :]
            return rows

        def send_own(ref, half, base):
            rows = rows_of(ref, half)
            own_x, own_y = copy(base + 0, rows(b), xnbr), copy(base + 1, rows(b), ynbr)
            own_x.start()
            own_y.start()
            return [own_x, own_y]

        def relay(ref, half, base):
            rows = rows_of(ref, half)
            copy(base + 0, rows(bx), xnbr).wait_recv()
            pass_y = copy(base + 2, rows(bx, 0), ynbr)
            pass_y.start()
            to_sib = [copy(base + 4, rows(bx), sib)]
            to_sib[-1].start()
            copy(base + 1, rows(by), ynbr).wait_recv()
            pass_x = copy(base + 3, rows(by, 1), xnbr)
            pass_x.start()
            to_sib.append(copy(base + 5, rows(by), sib))
            to_sib[-1].start()
            copy(base + 2, rows(bd, 0), ynbr).wait_recv()
            copy(base + 3, rows(bd, 1), xnbr).wait_recv()
            to_sib.append(copy(base + 6, rows(bd), sib))
            to_sib[-1].start()
            return [pass_y, pass_x] + to_sib

        def from_sibling(ref, half, base):
            for k, blk in enumerate((bx, by, bd)):
                copy(base + 4 + k, ref.at[blk, pl.ds((1 - c) * half, half), :], sib).wait_recv()

        small = [copy(14 + k, cw_o.at[b], to) for k, to in enumerate((xnbr, ynbr, diag))]
        for cp in small:
            cp.start()
        started = send_own(win_o, rin, 0) + send_own(wout_o, rout, 7)
        started += relay(win_o, rin, 0) + relay(wout_o, rout, 7)
        for k, blk in enumerate((bx, by, bd)):
            copy(14 + k, cw_o.at[blk], sib).wait_recv()
        from_sibling(win_o, rin, 0)
        from_sibling(wout_o, rout, 7)
        for cp in small + started:
            cp.wait_send()

    vm = pl.BlockSpec(memory_space=pltpu.VMEM)
    return pl.pallas_call(
        body, name="wgather",
        out_shape=(jax.ShapeDtypeStruct((4,) + w_in.shape, BF16),
                   jax.ShapeDtypeStruct((4,) + w_out.shape, BF16),
                   jax.ShapeDtypeStruct((4,) + conv_w.shape, F32)),
        in_specs=[vm, vm, vm], out_specs=(vm, vm, vm),
        scratch_shapes=[pltpu.SemaphoreType.DMA((17,)), pltpu.SemaphoreType.DMA((17,))],
        compiler_params=_params(),
    )(w_in, w_out, conv_w)


TM_MATMUL = 512
TM_COMBINE = 256


def _resident(shape):
    return pl.BlockSpec(shape, lambda i: (0,) * len(shape), pipeline_mode=pl.Buffered(1))


def _to_slabs(slab, val, j0=0):
    for j in range(val.shape[1] // 128):
        slab[j0 + j] = val[:, 128 * j:128 * (j + 1)]


def _scatter_classes(slab, j0, nj, out_ref, d, part=0, mid=None):
    tm = slab.shape[1]
    n = tm // d
    if d == 4:
        for r in range(d):
            for j in range(nj):
                out_ref[r, part * n:(part + 1) * n, 128 * j:128 * (j + 1)] = (
                    slab[j0 + j, pl.ds(r, n, stride=d), :].astype(out_ref.dtype))
        return
    q = tm // 4
    for lo in range(4):
        for j in range(nj):
            mid[j0 + j, lo * q:(lo + 1) * q, :] = slab[j0 + j, pl.ds(lo, q, stride=4), :]
    for hi in range(4):
        for lo in range(4):
            for j in range(nj):
                out_ref[4 * hi + lo, part * n:(part + 1) * n, 128 * j:128 * (j + 1)] = (
                    mid[j0 + j, pl.ds(lo * q + hi, n, stride=4), :].astype(out_ref.dtype))


def _gather_classes(slab, piece, nj, d, mid=None):
    tm = slab.shape[1]
    n = tm // d
    if d == 4:
        for r in range(d):
            for j in range(nj):
                slab[j, pl.ds(r, n, stride=d), :] = piece(r, j).astype(F32)
    else:
        q = tm // 4
        for hi in range(4):
            for lo in range(4):
                for j in range(nj):
                    mid[j, pl.ds(lo * q + hi, n, stride=4), :] = piece(4 * hi + lo, j).astype(F32)
        for lo in range(4):
            for j in range(nj):
                slab[j, pl.ds(lo, q, stride=4), :] = mid[j, lo * q:(lo + 1) * q, :]
    return jnp.concatenate([slab[j] for j in range(nj)], axis=1)


def _class_spec(d, width, tm):
    return pl.BlockSpec((d, tm // d, width), lambda i: (0, i, 0))


def _proj(x, tgt, norm_w, wblk, qkw, b256):
    s = x.shape[0]
    tm = TM_MATMUL
    nparts = 2
    tp = tm // nparts

    def body(x_ref, t_ref, nw_ref, w_ref, qkw_ref, b_ref, h_o, cg_o, qkr_o, qkn_o, vz_o, qkn4_o, v4_o, qkn16_o,
             v16_o, xmt_o, slabs, mids):
        for part in range(nparts):
            rows = slice(part * tp, (part + 1) * tp)
            slab = slabs.at[part]
            xf = x_ref[rows, :]
            xmt_o[rows, :] = xf - t_ref[rows, :]
            r = lax.rsqrt(jnp.mean(xf * xf, axis=-1, keepdims=True) + EPS)
            h = (xf * r * nw_ref[...]).astype(BF16)
            h_o[rows, :] = h
            p2 = jnp.dot(h, w_ref[2], preferred_element_type=F32)
            qkr_o[rows, :] = p2.astype(BF16)
            ss = _group_sum(p2 * p2, b_ref, split=False)
            rr = lax.rsqrt(ss * (1.0 / HEAD_DIM) + EPS)
            qkn = p2 * rr * qkw_ref[...]
            qkn_o[rows, :] = qkn.astype(BF16)
            _to_slabs(slab, qkn)
            p3 = jnp.dot(h, w_ref[3], preferred_element_type=F32)
            vz_o[rows, :] = p3.astype(BF16)
            _to_slabs(slab, p3[:, 0:512], 8)
            cg_o[rows, 0:1024] = jnp.dot(h, w_ref[0], preferred_element_type=F32).astype(BF16)
            cg_o[rows, 1024:2048] = jnp.dot(h, w_ref[1], preferred_element_type=F32).astype(BF16)
            for d, q_o, v_o in ((4, qkn4_o, v4_o), (16, qkn16_o, v16_o)):
                _scatter_classes(slab, 0, 8, q_o, d, part, mids.at[part])
                _scatter_classes(slab, 8, 4, v_o, d, part, mids.at[part])

    row = lambda w: pl.BlockSpec((tm, w), lambda i: (i, 0))
    full = lambda shp: pl.BlockSpec(shp, lambda i: (0,) * len(shp))
    nat = lambda w: jax.ShapeDtypeStruct((s, w), BF16)
    cls = lambda d, w: jax.ShapeDtypeStruct((d, s // d, w), BF16)
    return pl.pallas_call(
        body, name="proj", grid=(s // tm,),
        out_shape=(nat(1024), nat(2048), nat(1024), nat(1024), nat(1024),
                   cls(4, 1024), cls(4, 512), cls(16, 1024), cls(16, 512), jax.ShapeDtypeStruct((s, 1024), F32)),
        in_specs=[row(1024), row(1024), full((1, 1024)), _resident((4, 1024, 1024)), full((1, 1024)),
                  full((256, 256))],
        out_specs=(row(1024), row(2048), row(1024), row(1024), row(1024),
                   _class_spec(4, 1024, tm), _class_spec(4, 512, tm),
                   _class_spec(16, 1024, tm), _class_spec(16, 512, tm), row(1024)),
        scratch_shapes=[pltpu.VMEM((nparts, 12, tp, 128), F32), pltpu.VMEM((nparts, 12, tp, 128), F32)],
        compiler_params=_params(("parallel",)),
    )(x, tgt, norm_w, wblk, qkw, b256)


def _block_coords(t, i, nsub, nb, length):
    n = t * nsub + i
    q0 = i * QB
    start = pl.multiple_of(jnp.clip(n * QB - HALF, 0, length - KB), HALF)
    variant = jnp.where(n == 0, 0, jnp.where(n == nb - 1, 2, 1))
    return q0, start, variant


def _classes_per_step(r_cls, length, qt):
    return 2 if (length == qt and qt // QB <= 8 and r_cls % 2 == 0) else 1


def _split_heads(a, lo):
    zero = jnp.zeros_like(a)
    return jnp.concatenate([jnp.where(lo, a, zero), jnp.where(lo, zero, a)], axis=0)


def _col_pair(ref, q0, lane):
    return jnp.concatenate([ref[pl.ds(q0, QB), lane:lane + 1],
                            ref[pl.ds(q0, QB), HEAD_DIM + lane:HEAD_DIM + lane + 1]], axis=0)


def _attn_fwd(qkn_l, v_l, bias, gi, name):
    r_cls, length, _ = qkn_l.shape
    qt = min(length, 2048)
    nb, nsub = length // QB, qt // QB
    cb = _classes_per_step(r_cls, length, qt)

    def body(q_ref, k_ref, v_ref, b_ref, o_ref, lse_ref):
        t = pl.program_id(2)
        lo = lax.broadcasted_iota(jnp.int32, (QB, 128), 1) < HEAD_DIM

        starts, logits = [], []
        for ci in range(cb):
            for i in range(nsub):
                _, start, variant = _block_coords(t, i, nsub, nb, length)
                qq = _split_heads(q_ref[ci, i * QB:(i + 1) * QB, :], lo)
                k = k_ref[ci, pl.ds(start, KB), :]
                logits.append(lax.dot_general(qq, k, (((1,), (1,)), ((), ())), preferred_element_type=F32)
                              + b_ref[variant])
                starts.append(start)
        lg = jnp.concatenate(logits, axis=0)
        m = jnp.max(lg, axis=-1, keepdims=True)
        p = jnp.exp(lg - m)
        pb = p.astype(BF16)
        l = jnp.sum(p, axis=-1, keepdims=True)
        lse = jnp.broadcast_to(m + jnp.log(l), (cb * nsub * 2 * QB, 128))
        inv = 1.0 / l
        for ci in range(cb):
            for i in range(nsub):
                j = ci * nsub + i
                rows = slice(2 * QB * j, 2 * QB * (j + 1))
                v = v_ref[ci, pl.ds(starts[j], KB), :]
                pv = jnp.dot(pb[rows], v, preferred_element_type=F32) * inv[rows]
                o_ref[ci, i * QB:(i + 1) * QB, :] = jnp.where(lo, pv[0:QB], pv[QB:2 * QB]).astype(BF16)
                ls = lse[rows]
                lse_ref[ci, i * QB:(i + 1) * QB, :] = jnp.where(lo, ls[0:QB], ls[QB:2 * QB])

    return pl.pallas_call(
        body, name=name, grid=(N_PAIR, r_cls // cb, length // qt),
        out_shape=(jax.ShapeDtypeStruct((r_cls, length, 512), BF16),
                   jax.ShapeDtypeStruct((r_cls, length, 512), F32)),
        in_specs=[pl.BlockSpec((cb, qt, 128), lambda p, r, t: (r, t, p)),
                  pl.BlockSpec((cb, length, 128), lambda p, r, t: (r, 0, 4 + p)),
                  pl.BlockSpec((cb, length, 128), lambda p, r, t: (r, 0, p)),
                  pl.BlockSpec((None, 3, None, 2 * QB, KB), lambda p, r, t: (gi, 0, p, 0, 0))],
        out_specs=(pl.BlockSpec((cb, qt, 128), lambda p, r, t: (r, t, p)),
                   pl.BlockSpec((cb, qt, 128), lambda p, r, t: (r, t, p))),
        compiler_params=_params(("parallel", "parallel", "arbitrary")),
    )(qkn_l, qkn_l, v_l, bias)


def _combine(o_g, lse_g, cg, vz, xmt, wout, cw, cb, b256):
    s = xmt.shape[0]
    tm = TM_COMBINE
    hb = 16
    nt = s // tm

    def body(o1, o4, o16, l1, l4, l16, cg_hbm, cgp_ref, cgn_ref, za_ref, xmt_hbm, w_ref, cw_ref, cb_ref,
             b_ref, y_o, dout_o, ld1_o, do1_o, dza_o, dgbz_o, dzc_o, loss_o, dcb_o, dcw_o,
             do4_o, ld4_o, do16_o, ld16_o, slab, mid, cg_ring, xmt_ring, ring_sems):
        i = pl.program_id(0)

        def fetch(step, slot):
            rows = pl.ds(pl.multiple_of(step * tm, tm), tm)
            return (pltpu.make_async_copy(cg_hbm.at[rows, :], cg_ring.at[slot], ring_sems.at[0, slot]),
                    pltpu.make_async_copy(xmt_hbm.at[rows, :], xmt_ring.at[slot], ring_sems.at[1, slot]))

        @pl.when(i == 0)
        def _():
            loss_o[...] = jnp.zeros_like(loss_o)
            dcb_o[...] = jnp.zeros_like(dcb_o)
            dcw_o[...] = jnp.zeros_like(dcw_o)
            for first in range(2):
                for cp in fetch(first, first):
                    cp.start()

        @pl.when(i + 2 < nt)
        def _():
            for cp in fetch(i + 2, (i + 2) % 3):
                cp.start()

        slot = i % 3
        for cp in fetch(i, slot):
            cp.wait()
        cg_ref = cg_ring.at[slot]
        xmt_ref = xmt_ring.at[slot]

        u = cg_ref[:, 0:512].astype(F32)
        gb = cg_ref[:, 512:1024].astype(F32)
        gc = cg_ref[:, 1024:1536].astype(F32)
        zc = cg_ref[:, 1536:2048].astype(F32)
        tt = gc * u
        t_prev = cgp_ref[hb - 1:hb, 0:512].astype(F32) * cgp_ref[hb - 1:hb, 1024:1536].astype(F32)
        t_next = cgn_ref[0:1, 0:512].astype(F32) * cgn_ref[0:1, 1024:1536].astype(F32)
        t_prev = jnp.where(i == 0, 0.0, t_prev)
        t_next = jnp.where(i == nt - 1, 0.0, t_next)
        rows = lax.broadcasted_iota(jnp.int32, (tm, 512), 0)
        t_up = jnp.where(rows == 0, t_prev, pltpu.roll(tt, 1, 0))
        t_dn = jnp.where(rows == tm - 1, t_next, pltpu.roll(tt, tm - 1, 0))
        w0, w1, w2 = cw_ref[0:1, :], cw_ref[1:2, :], cw_ref[2:3, :]
        zb = w0 * t_up + w1 * tt + w2 * t_dn + cb_ref[...]
        sg = _sigmoid(zc)
        sz = zc * sg
        y_conv = gb * zb * sz

        a1, p1 = l1[0], o1[0].astype(F32)
        a4 = _gather_classes(slab, lambda r, j: l4[r, :, 128 * j:128 * (j + 1)], 4, 4)
        p4 = _gather_classes(slab, lambda r, j: o4[r, :, 128 * j:128 * (j + 1)], 4, 4)
        a16 = _gather_classes(slab, lambda r, j: l16[r, :, 128 * j:128 * (j + 1)], 4, 16, mid)
        p16 = _gather_classes(slab, lambda r, j: o16[r, :, 128 * j:128 * (j + 1)], 4, 16, mid)
        m = jnp.maximum(jnp.maximum(a1, a4), a16)
        e1, e4, e16 = jnp.exp(a1 - m), jnp.exp(a4 - m), jnp.exp(a16 - m)
        den = e1 + e4 + e16
        lse = m + jnp.log(den)
        o = (e1 * p1 + e4 * p4 + e16 * p16) / den
        za = za_ref[...].astype(F32)
        sga = _sigmoid(za)
        sa = za * sga
        y = jnp.concatenate([y_conv, o * sa], axis=1).astype(BF16)
        y_o[...] = y

        diff = xmt_ref[...] + jnp.dot(y, w_ref[...], preferred_element_type=F32)
        loss_o[...] += (0.5 / D_MODEL) * jnp.sum(diff * diff)
        dout = diff * (1.0 / D_MODEL)
        dout_o[...] = dout
        dy = lax.dot_general(dout.astype(BF16), w_ref[...], (((1,), (1,)), ((), ())), preferred_element_type=F32)
        dyc, dya = dy[:, 0:512], dy[:, 512:1024]

        do = dya * sa
        dza_o[...] = (dya * o * (sga * (1.0 + za * (1.0 - sga)))).astype(BF16)
        lane = lax.broadcasted_iota(jnp.int32, (tm, 512), 1)
        ld = jnp.where((lane & (HEAD_DIM - 1)) < HEAD_DIM // 2, lse, _group_sum(do * o, b_ref))
        do1_o[0] = do.astype(BF16)
        ld1_o[0] = ld
        _to_slabs(slab, do)
        _scatter_classes(slab, 0, 4, do4_o, 4)
        _scatter_classes(slab, 0, 4, do16_o, 16, 0, mid)
        _to_slabs(slab, ld)
        _scatter_classes(slab, 0, 4, ld4_o, 4)
        _scatter_classes(slab, 0, 4, ld16_o, 16, 0, mid)

        dzc = dyc * sz * gb
        dzc_o[...] = dzc.astype(BF16)
        dgbz_o[:, 0:512] = (dyc * sz * zb).astype(BF16)
        dgbz_o[:, 512:1024] = (dyc * gb * zb * (sg * (1.0 + zc * (1.0 - sg)))).astype(BF16)
        dcb_o[...] += jnp.sum(dzc, axis=0, keepdims=True)
        dcw_o[0:1, :] += jnp.sum(dzc * t_up, axis=0, keepdims=True)
        dcw_o[1:2, :] += jnp.sum(dzc * tt, axis=0, keepdims=True)
        dcw_o[2:3, :] += jnp.sum(dzc * t_dn, axis=0, keepdims=True)

    row = lambda w, j=0: pl.BlockSpec((tm, w), lambda i: (i, j))
    full = lambda shp: pl.BlockSpec(shp, lambda i: (0,) * len(shp))
    prev = pl.BlockSpec((hb, 2048), lambda i: (jnp.maximum(i * (tm // hb) - 1, 0), 0))
    nxt = pl.BlockSpec((hb, 2048), lambda i: (jnp.minimum((i + 1) * (tm // hb), s // hb - 1), 0))
    cls = lambda d, dt: jax.ShapeDtypeStruct((d, s // d, 512), dt)
    cspecs = [_class_spec(d, 512, tm) for d in DILATIONS]
    return pl.pallas_call(
        body, name="combine", grid=(nt,),
        out_shape=(jax.ShapeDtypeStruct((s, 1024), BF16), jax.ShapeDtypeStruct((s, 1024), F32),
                   cls(1, F32), cls(1, BF16), jax.ShapeDtypeStruct((s, 512), BF16),
                   jax.ShapeDtypeStruct((s, 1024), BF16), jax.ShapeDtypeStruct((s, 512), BF16),
                   jax.ShapeDtypeStruct((1, 128), F32), jax.ShapeDtypeStruct((1, 512), F32),
                   jax.ShapeDtypeStruct((8, 512), F32),
                   cls(4, BF16), cls(4, F32), cls(16, BF16), cls(16, F32)),
        in_specs=cspecs + cspecs + [pl.BlockSpec(memory_space=pl.ANY), prev, nxt, row(512, 1),
                                    pl.BlockSpec(memory_space=pl.ANY),
                                    _resident((1024, 1024)), full((8, 512)), full((1, 512)), full((256, 256))],
        out_specs=(row(1024), row(1024), cspecs[0], cspecs[0], row(512), row(1024), row(512),
                   full((1, 128)), full((1, 512)), full((8, 512)),
                   cspecs[1], cspecs[1], cspecs[2], cspecs[2]),
        scratch_shapes=[pltpu.VMEM((4, tm, 128), F32), pltpu.VMEM((4, tm, 128), F32),
                        pltpu.VMEM((3, tm, 2048), BF16), pltpu.VMEM((3, tm, 1024), F32),
                        pltpu.SemaphoreType.DMA((2, 3))],
        compiler_params=_params(("arbitrary",)),
    )(*o_g, *lse_g, cg, cg, cg, vz, xmt, wout, cw, cb, b256)


def _attn_bwd(qkn_l, v_l, do_l, ld_l, bias, gi, name):
    r_cls, length, _ = qkn_l.shape
    qt = min(length, 2048 if length <= 4096 else 1024)
    nb, nsub, nt = length // QB, qt // QB, length // qt
    chunk = min(length, 4096)
    nchunk = length // chunk
    cb = _classes_per_step(r_cls, length, qt)
    nrb = r_cls // cb
    nbuf = min(4, 2 * nchunk * cb)

    def body(q_ref, k_ref, v_ref, do_ref, ld_ref, b_ref, dq_ref, dkv_hbm, dsum_ref, dk_acc, dv_acc, stage, sems):
        p_id, r, t = pl.program_id(0), pl.program_id(1), pl.program_id(2)
        lo = lax.broadcasted_iota(jnp.int32, (QB, 128), 1) < HEAD_DIM

        @pl.when(t == 0)
        def _():
            dk_acc[...] = jnp.zeros_like(dk_acc)
            dv_acc[...] = jnp.zeros_like(dv_acc)

        @pl.when((t == 0) & (r == 0))
        def _():
            dsum_ref[...] = jnp.zeros_like(dsum_ref)

        nt_dims = (((1,), (1,)), ((), ()))
        tn_dims = (((0,), (0,)), ((), ()))
        coords, qqs, dds, logits, dps, lcols, dcols = [], [], [], [], [], [], []
        for ci in range(cb):
            for i in range(nsub):
                q0, start, variant = _block_coords(t, i, nsub, nb, length)
                qq = _split_heads(q_ref[ci, q0:q0 + QB, :], lo)
                dd = _split_heads(do_ref[ci, q0:q0 + QB, :], lo)
                k = k_ref[ci, pl.ds(start, KB), :]
                v = v_ref[ci, pl.ds(start, KB), :]
                logits.append(lax.dot_general(qq, k, nt_dims, preferred_element_type=F32) + b_ref[variant])
                dps.append(lax.dot_general(dd, v, nt_dims, preferred_element_type=F32))
                lcols.append(_col_pair(ld_ref.at[ci], q0, 0))
                dcols.append(_col_pair(ld_ref.at[ci], q0, HEAD_DIM // 2))
                coords.append((ci, i, q0, start, variant))
                qqs.append(qq)
                dds.append(dd)
        p = jnp.exp(jnp.concatenate(logits, axis=0) - jnp.concatenate(lcols, axis=0))
        ds = p * (jnp.concatenate(dps, axis=0) - jnp.concatenate(dcols, axis=0))
        pb = p.astype(BF16)
        dsb = ds.astype(BF16)
        middle = None
        for j, (ci, i, q0, start, variant) in enumerate(coords):
            rows = slice(2 * QB * j, 2 * QB * (j + 1))
            if 0 < i < nsub - 1:
                middle = ds[rows] if middle is None else middle + ds[rows]
            else:
                dsum_ref[variant] += ds[rows]
            dqq = jnp.dot(dsb[rows], k_ref[ci, pl.ds(start, KB), :], preferred_element_type=F32)
            dq_ref[ci, q0:q0 + QB, :] = jnp.where(lo, dqq[0:QB], dqq[QB:2 * QB]).astype(BF16)
            dk_acc[ci, pl.ds(start, KB), :] += lax.dot_general(dsb[rows], qqs[j], tn_dims,
                                                               preferred_element_type=F32)
            dv_acc[ci, pl.ds(start, KB), :] += lax.dot_general(pb[rows], dds[j], tn_dims,
                                                               preferred_element_type=F32)
        if middle is not None:
            dsum_ref[1] += middle

        @pl.when(t == nt - 1)
        def _():
            def copy(k):
                ci, which, c = k // (2 * nchunk), (k // nchunk) % 2, k % nchunk
                rows = pl.ds(c * chunk, chunk)
                return pltpu.make_async_copy(stage.at[k % nbuf], dkv_hbm.at[r * cb + ci, p_id, which, rows, :],
                                             sems.at[k % nbuf])

            for k in range(2 * nchunk * cb):
                if k < nbuf:
                    @pl.when((p_id > 0) | (r > 0))
                    def _():
                        copy(k).wait()
                else:
                    copy(k).wait()
                acc = (dk_acc, dv_acc)[(k // nchunk) % 2]
                stage[k % nbuf] = acc[k // (2 * nchunk), pl.ds((k % nchunk) * chunk, chunk), :].astype(BF16)
                copy(k).start()

            @pl.when((p_id == N_PAIR - 1) & (r == nrb - 1))
            def _():
                for k in range(nbuf):
                    copy(k).wait()

    qspec = pl.BlockSpec((cb, qt, 128), lambda p, r, t: (r, t, p))
    return pl.pallas_call(
        body, name=name, grid=(N_PAIR, nrb, nt),
        out_shape=(jax.ShapeDtypeStruct((r_cls, length, 512), BF16),
                   jax.ShapeDtypeStruct((r_cls, N_PAIR, 2, length, 128), BF16),
                   jax.ShapeDtypeStruct((N_PAIR, 3, 2 * QB, KB), F32)),
        in_specs=[qspec,
                  pl.BlockSpec((cb, length, 128), lambda p, r, t: (r, 0, 4 + p)),
                  pl.BlockSpec((cb, length, 128), lambda p, r, t: (r, 0, p)),
                  qspec, qspec,
                  pl.BlockSpec((None, 3, None, 2 * QB, KB), lambda p, r, t: (gi, 0, p, 0, 0))],
        out_specs=(qspec, pl.BlockSpec(memory_space=pl.ANY),
                   pl.BlockSpec((None, 3, 2 * QB, KB), lambda p, r, t: (p, 0, 0, 0))),
        scratch_shapes=[pltpu.VMEM((cb, length, 128), F32), pltpu.VMEM((cb, length, 128), F32),
                        pltpu.VMEM((nbuf, chunk, 128), BF16), pltpu.SemaphoreType.DMA((nbuf,))],
        compiler_params=_params(("arbitrary", "arbitrary", "arbitrary")),
    )(qkn_l, qkn_l, v_l, do_l, ld_l, bias)


def _bwd_tail(dq_g, dkv_g, qkr, qkw, dza, dgbz, dzc, cg, cw, wblk, x, norm_w, dout, b256):
    s = x.shape[0]
    tm = TM_MATMUL
    hb = 16
    nt = s // tm

    def body(dq1, dq4, dq16, dkv1, dkv4, dkv16, qkr_ref, qkw_ref, dza_ref, dgbz_ref, dzc_ref,
             dzp_ref, dzn_ref, u_ref, gc_ref, cw_ref, w_ref, x_ref, nw_ref, dout_ref, b_ref,
             gx_o, dproj_o, dnw_o, dqkw_o, slab, mid):
        i = pl.program_id(0)

        def nat_q(ref, d):
            return _gather_classes(slab, lambda r, j: ref[r, :, 128 * j:128 * (j + 1)], 4, d, mid)

        def nat_kv(ref, d, which):
            return _gather_classes(slab, lambda r, j: ref[r, j, which], 4, d, mid)

        @pl.when(i == 0)
        def _():
            dnw_o[...] = jnp.zeros_like(dnw_o)
            dqkw_o[...] = jnp.zeros_like(dqkw_o)

        dzc = dzc_ref[...].astype(F32)
        d_prev = jnp.where(i == 0, 0.0, dzp_ref[hb - 1:hb, :].astype(F32))
        d_next = jnp.where(i == nt - 1, 0.0, dzn_ref[0:1, :].astype(F32))
        rows = lax.broadcasted_iota(jnp.int32, (tm, 512), 0)
        d_up = jnp.where(rows == 0, d_prev, pltpu.roll(dzc, 1, 0))
        d_dn = jnp.where(rows == tm - 1, d_next, pltpu.roll(dzc, tm - 1, 0))
        dt = cw_ref[0:1, :] * d_dn + cw_ref[1:2, :] * dzc + cw_ref[2:3, :] * d_up
        u = u_ref[...].astype(F32)
        gc = gc_ref[...].astype(F32)
        dproj_o[:, 0:512] = (dt * gc).astype(BF16)
        dproj_o[:, 512:1024] = dgbz_ref[:, 0:512]
        dproj_o[:, 1024:1536] = (dt * u).astype(BF16)
        dproj_o[:, 1536:2048] = dgbz_ref[:, 512:1024]

        dqn = (dq1[0].astype(F32) + nat_q(dq4, 4) + nat_q(dq16, 16)) * (1.0 / 8.0)
        dk1 = jnp.concatenate([dkv1[0, j, 0] for j in range(N_PAIR)], axis=1)
        dv1 = jnp.concatenate([dkv1[0, j, 1] for j in range(N_PAIR)], axis=1)
        dkn = dk1 + nat_kv(dkv4, 4, 0) + nat_kv(dkv16, 16, 0)
        dvn = dv1 + nat_kv(dkv4, 4, 1) + nat_kv(dkv16, 16, 1)
        g = jnp.concatenate([dqn, dkn], axis=1) * qkw_ref[...]
        raw = qkr_ref[...].astype(F32)
        rr = lax.rsqrt(_group_sum(raw * raw, b_ref, split=False) * (1.0 / HEAD_DIM) + EPS)
        proj_gq = _group_sum(g * raw, b_ref) * (1.0 / HEAD_DIM)
        draw = rr * g - raw * (rr * rr * rr) * proj_gq
        dqkw_o[...] += jnp.sum(jnp.concatenate([dqn, dkn], axis=1) * raw * rr, axis=0, keepdims=True)
        dproj_o[:, 2048:3072] = draw.astype(BF16)
        dproj_o[:, 3072:3584] = dvn.astype(BF16)
        dproj_o[:, 3584:4096] = dza_ref[...]

        nt_dims = (((1,), (1,)), ((), ()))
        dh = lax.dot_general(dproj_o[:, 0:1024], w_ref[0], nt_dims, preferred_element_type=F32)
        for b in range(1, 4):
            dh += lax.dot_general(dproj_o[:, 1024 * b:1024 * b + 1024], w_ref[b], nt_dims,
                                  preferred_element_type=F32)

        xf = x_ref[...]
        r = lax.rsqrt(jnp.mean(xf * xf, axis=-1, keepdims=True) + EPS)
        gh = dh * nw_ref[...]
        dnw_o[...] += jnp.sum(dh * xf * r, axis=0, keepdims=True)
        mean_gx = jnp.mean(gh * xf, axis=-1, keepdims=True)
        gx_o[...] = dout_ref[...] + r * gh - xf * (r * r * r) * mean_gx

    row = lambda w, j=0: pl.BlockSpec((tm, w), lambda i: (i, j))
    full = lambda shp: pl.BlockSpec(shp, lambda i: (0,) * len(shp))
    prev = pl.BlockSpec((hb, 512), lambda i: (jnp.maximum(i * (tm // hb) - 1, 0), 0))
    nxt = pl.BlockSpec((hb, 512), lambda i: (jnp.minimum((i + 1) * (tm // hb), s // hb - 1), 0))
    return pl.pallas_call(
        body, name="bwd_tail", grid=(nt,),
        out_shape=(jax.ShapeDtypeStruct((s, 1024), F32), jax.ShapeDtypeStruct((s, 4096), BF16),
                   jax.ShapeDtypeStruct((1, 1024), F32), jax.ShapeDtypeStruct((1, 1024), F32)),
        in_specs=[_class_spec(d, 512, tm) for d in DILATIONS]
        + [pl.BlockSpec((d, N_PAIR, 2, tm // d, 128), lambda i: (0, 0, 0, i, 0)) for d in DILATIONS]
        + [row(1024), full((1, 1024)), row(512), row(1024), row(512), prev, nxt,
           row(512, 0), row(512, 2), full((8, 512)), _resident((4, 1024, 1024)), row(1024),
           full((1, 1024)), row(1024), full((256, 256))],
        out_specs=(row(1024), row(4096), full((1, 1024)), full((1, 1024))),
        scratch_shapes=[pltpu.VMEM((4, tm, 128), F32), pltpu.VMEM((4, tm, 128), F32)],
        compiler_params=_params(("arbitrary",), vmem=VMEM_LIMIT_TAIL),
    )(*dq_g, *dkv_g, qkr, qkw, dza, dgbz, dzc, dzc, dzc, cg, cg, cw, wblk, x, norm_w, dout, b256)


def _wgrad(a, b, row_blocked, name):
    s, m = a.shape
    n = b.shape[1]
    tk = 2048
    ncol = min(n, 2048)
    nj, nk = n // ncol, s // tk

    def body(a_ref, b_ref, o_ref, acc):
        kk = pl.program_id(1)

        @pl.when(kk == 0)
        def _():
            acc[...] = jnp.zeros_like(acc)

        acc[...] += lax.dot_general(a_ref[...], b_ref[...].astype(BF16), (((0,), (0,)), ((), ())),
                                    preferred_element_type=F32)

        @pl.when(kk == nk - 1)
        def _():
            blocks, _, rows, _ = o_ref.shape
            for blk in range(blocks):
                for half in range(2):
                    if row_blocked:
                        r0 = (2 * blk + half) * rows
                        o_ref[blk, half] = acc[r0:r0 + rows, :].astype(BF16)
                    else:
                        o_ref[blk, half] = acc[half * rows:(half + 1) * rows,
                                               1024 * blk:1024 * (blk + 1)].astype(BF16)

    if row_blocked:
        out_shape = jax.ShapeDtypeStruct((4, 2, m // 8, 1024), BF16)
        out_spec = pl.BlockSpec((4, 2, m // 8, 1024), lambda j, k: (0, 0, 0, 0))
    else:
        out_shape = jax.ShapeDtypeStruct((n // 1024, 2, m // 2, 1024), BF16)
        out_spec = pl.BlockSpec((ncol // 1024, 2, m // 2, 1024), lambda j, k: (j, 0, 0, 0))
    return pl.pallas_call(
        body, name=name, grid=(nj, nk),
        out_shape=out_shape,
        in_specs=[pl.BlockSpec((tk, m), lambda j, k: (k, 0)), pl.BlockSpec((tk, ncol), lambda j, k: (k, j))],
        out_specs=out_spec,
        scratch_shapes=[pltpu.VMEM((m, ncol), F32)],
        compiler_params=_params(("parallel", "arbitrary")),
    )(a, b)


def _dbias(dsums, onehot_all):
    def body(ds1_ref, ds4_ref, ds16_ref, oh_ref, o_ref):
        @pl.when(pl.program_id(0) == 0)
        def _():
            o_ref[...] = jnp.zeros_like(o_ref)

        hrow = lax.broadcasted_iota(jnp.int32, (8, KB), 0)
        flip = (lax.broadcasted_iota(jnp.int32, (QB, QB), 0)
                + lax.broadcasted_iota(jnp.int32, (QB, QB), 1) == QB - 1).astype(F32)

        def diagonal_sums(tile):
            rev = jnp.dot(flip, tile, preferred_element_type=F32, precision=lax.Precision.HIGHEST)
            sums = jnp.sum(pltpu.roll(rev, 0, 1, stride=1, stride_axis=0), axis=0, keepdims=True)
            return pltpu.roll(sums, KB - (QB - 1), 1)

        for g, ds_ref in enumerate((ds1_ref, ds4_ref, ds16_ref)):
            diag = jnp.zeros((8, KB), F32)
            for p in range(N_PAIR):
                diag = jnp.where(hrow == 2 * p, diagonal_sums(ds_ref[p, 0:QB, :]), diag)
                diag = jnp.where(hrow == 2 * p + 1, diagonal_sums(ds_ref[p, QB:2 * QB, :]), diag)
            o_ref[...] += jnp.dot(diag, oh_ref[g], preferred_element_type=F32, precision=lax.Precision.HIGHEST)

    ds_spec = pl.BlockSpec((N_PAIR, None, 2 * QB, KB), lambda v: (0, v, 0, 0))
    return pl.pallas_call(
        body, name="dbias", grid=(3,),
        out_shape=jax.ShapeDtypeStruct((8, 128), F32),
        in_specs=[ds_spec, ds_spec, ds_spec, pl.BlockSpec((3, None, KB, 128), lambda v: (0, v, 0, 0))],
        out_specs=pl.BlockSpec((8, 128), lambda v: (0, 0)),
        compiler_params=_params(("arbitrary",)),
    )(*dsums, onehot_all)


def _gsync(pw_in, pw_out, small):
    hin, hout = pw_in.shape[2], pw_out.shape[2]
    nsmall = small.shape[0]

    def body(pin_hbm, pout_hbm, small_ref, gin_o, gout_o, small_o,
             mine_in, recv_in, s1_in, r1_in, s2_in, r2_in, mine_out, recv_out, s1_out, r1_out, s2_out, r2_out, gather,
             lsem, asend, arecv, bsend, brecv, csend, crecv, ssend, srecv):
        x, y, c = lax.axis_index("x"), lax.axis_index("y"), lax.axis_index("c")
        b = 2 * x + y
        dev = 4 * x + 2 * y + c
        sib = (x, y, 1 - c)
        xnbr, ynbr = (1 - x, y, c), (x, 1 - y, c)
        bx, by, bd = b ^ 2, b ^ 1, b ^ 3

        def rcopy(src, dst, ssem, rsem, to):
            return pltpu.make_async_remote_copy(src_ref=src, dst_ref=dst, send_sem=ssem, recv_sem=rsem,
                                                device_id=to, device_id_type=MESH)

        gather[dev] = small_ref[...]
        s_sends = []
        for k in range(1, 8):
            to = (x ^ (k >> 2), y ^ ((k >> 1) & 1), c ^ (k & 1))
            cp = rcopy(gather.at[dev], gather.at[dev], ssend.at[k - 1], srecv.at[k - 1], to)
            cp.start()
            s_sends.append(cp)

        a_in = rcopy(pin_hbm.at[:, 1 - c], recv_in, asend.at[0], arecv.at[0], sib)
        a_out = rcopy(pout_hbm.at[:, 1 - c], recv_out, asend.at[1], arecv.at[1], sib)
        a_in.start()
        a_out.start()
        l_in = pltpu.make_async_copy(pin_hbm.at[:, c], mine_in, lsem.at[0])
        l_out = pltpu.make_async_copy(pout_hbm.at[:, c], mine_out, lsem.at[1])
        l_in.start()
        l_out.start()
        l_in.wait()
        l_out.wait()

        def phase_one(a_cp, mine, recv, s1, r1, half, base):
            a_cp.wait_recv()
            q = half // 2
            sends = []
            for part, (peer, blk_peer) in enumerate(((xnbr, bx), (ynbr, by))):
                rows = pl.ds(part * q, q)
                for slot, blk in enumerate((blk_peer, bd)):
                    s1[part, slot] = (mine[blk, rows, :].astype(F32) + recv[blk, rows, :].astype(F32)).astype(BF16)
                cp = rcopy(s1.at[part], r1.at[part], bsend.at[base + part], brecv.at[base + part], peer)
                cp.start()
                sends.append(cp)
            return sends

        def phase_two(p1, mine, recv, r1, s2, r2, half, base):
            q = half // 2
            own, sends = [], []
            for part, (peer, blk_next) in enumerate(((ynbr, by), (xnbr, bx))):
                rows = pl.ds(part * q, q)
                p1[part].wait_recv()
                own.append(mine[b, rows, :].astype(F32) + recv[b, rows, :].astype(F32) + r1[part, 0].astype(F32))
                s2[part] = (mine[blk_next, rows, :].astype(F32) + recv[blk_next, rows, :].astype(F32)
                            + r1[part, 1].astype(F32)).astype(BF16)
                cp = rcopy(s2.at[part], r2.at[part], bsend.at[base + 2 + part], brecv.at[base + 2 + part], peer)
                cp.start()
                sends.append(cp)
            return own, sends

        def stage_c(own, p2, r2, g_o, half, idx):
            q = half // 2
            for part in range(2):
                p2[part].wait_recv()
                g_o[pl.ds(pl.multiple_of(c * half + part * q, q), q), :] = own[part] + r2[part].astype(F32)
            rows = g_o.at[pl.ds(pl.multiple_of(c * half, half), half), :]
            cp = rcopy(rows, rows, csend.at[idx], crecv.at[idx], sib)
            cp.start()
            return cp

        p1_in = phase_one(a_in, mine_in, recv_in, s1_in, r1_in, hin, 0)
        p1_out = phase_one(a_out, mine_out, recv_out, s1_out, r1_out, hout, 4)
        own_in, p2_in = phase_two(p1_in, mine_in, recv_in, r1_in, s2_in, r2_in, hin, 0)
        own_out, p2_out = phase_two(p1_out, mine_out, recv_out, r1_out, s2_out, r2_out, hout, 4)
        c_in = stage_c(own_in, p2_in, r2_in, gin_o, hin, 0)
        c_out = stage_c(own_out, p2_out, r2_out, gout_o, hout, 1)
        b_in, b_out = p1_in + p2_in, p1_out + p2_out

        for cp in s_sends:
            cp.wait_recv()
        tot = gather[0]
        for d in range(1, 8):
            tot = tot + gather[d]
        small_o[...] = tot

        for g_o, half, idx in ((gin_o, hin, 0), (gout_o, hout, 1)):
            other = g_o.at[pl.ds(pl.multiple_of((1 - c) * half, half), half), :]
            rcopy(other, other, csend.at[idx], crecv.at[idx], sib).wait_recv()
        for cp in s_sends + [a_in, a_out] + b_in + b_out + [c_in, c_out]:
            cp.wait_send()

    vm = pl.BlockSpec(memory_space=pltpu.VMEM)
    hbm = pl.BlockSpec(memory_space=pl.ANY)
    return pl.pallas_call(
        body, name="gsync",
        out_shape=(jax.ShapeDtypeStruct((2 * hin, 1024), F32), jax.ShapeDtypeStruct((2 * hout, 1024), F32),
                   jax.ShapeDtypeStruct((nsmall, 128), F32)),
        in_specs=[hbm, hbm, vm], out_specs=(vm, vm, vm),
        scratch_shapes=[pltpu.VMEM((4, hin, 1024), BF16), pltpu.VMEM((4, hin, 1024), BF16),
                        pltpu.VMEM((2, 2, hin // 2, 1024), BF16), pltpu.VMEM((2, 2, hin // 2, 1024), BF16),
                        pltpu.VMEM((2, hin // 2, 1024), BF16), pltpu.VMEM((2, hin // 2, 1024), BF16),
                        pltpu.VMEM((4, hout, 1024), BF16), pltpu.VMEM((4, hout, 1024), BF16),
                        pltpu.VMEM((2, 2, hout // 2, 1024), BF16), pltpu.VMEM((2, 2, hout // 2, 1024), BF16),
                        pltpu.VMEM((2, hout // 2, 1024), BF16), pltpu.VMEM((2, hout // 2, 1024), BF16),
                        pltpu.VMEM((8, nsmall, 128), F32),
                        pltpu.SemaphoreType.DMA((2,)),
                        pltpu.SemaphoreType.DMA((2,)), pltpu.SemaphoreType.DMA((2,)),
                        pltpu.SemaphoreType.DMA((8,)), pltpu.SemaphoreType.DMA((8,)),
                        pltpu.SemaphoreType.DMA((2,)), pltpu.SemaphoreType.DMA((2,)),
                        pltpu.SemaphoreType.DMA((7,)), pltpu.SemaphoreType.DMA((7,))],
        compiler_params=_params(),
    )(pw_in, pw_out, small)


def _adamw_math(w, g, m, v):
    m = ADAM_B1 * m + (1.0 - ADAM_B1) * g
    v = ADAM_B2 * v + (1.0 - ADAM_B2) * (g * g)
    m_hat = m / (1.0 - ADAM_B1 ** ADAM_STEP)
    v_hat = v / (1.0 - ADAM_B2 ** ADAM_STEP)
    delta = -ADAM_LR * (m_hat / (jnp.sqrt(v_hat) + ADAM_EPS) + ADAM_WD * w)
    return delta, m, v


def _adamw(w, g, m, v, name):
    rows, cols = w.shape
    tr = 256 if rows % 256 == 0 else rows

    def body(w_ref, g_ref, m_ref, v_ref, g_o, d_o, m_o, v_o):
        g = g_ref[...]
        d, m2, v2 = _adamw_math(w_ref[...], g, m_ref[...], v_ref[...])
        g_o[...] = g
        d_o[...] = d
        m_o[...] = m2
        v_o[...] = v2

    spec = pl.BlockSpec((tr, cols), lambda i: (i, 0))
    shp = jax.ShapeDtypeStruct((rows, cols), F32)
    return pl.pallas_call(
        body, name=name, grid=(rows // tr,), out_shape=(shp, shp, shp, shp),
        in_specs=[spec] * 4, out_specs=(spec, spec, spec, spec),
        compiler_params=_params(("parallel",)),
    )(w, g, m, v)


def _fold_heads(dqkw):
    def body(x_ref, o_ref):
        xs = x_ref[...]
        sq = xs[0:1] + xs[1:2] + xs[2:3] + xs[3:4]
        sk = xs[4:5] + xs[5:6] + xs[6:7] + xs[7:8]
        both = jnp.concatenate([sq, sk], axis=0)
        o_ref[...] = both + pltpu.roll(both, HEAD_DIM, 1)

    vm = pl.BlockSpec(memory_space=pltpu.VMEM)
    return pl.pallas_call(body, name="fold_heads", out_shape=jax.ShapeDtypeStruct((2, 128), F32),
                          in_specs=[vm], out_specs=vm, compiler_params=_params())(dqkw)


def kernel(x, norm_w, w_in, conv_w, conv_b, q_norm_w, k_norm_w, rel_bias, w_out, loss_target, m_norm_w, m_w_in, m_conv_w, m_conv_b, m_q_norm_w, m_k_norm_w, m_rel_bias, m_w_out, v_norm_w, v_w_in, v_conv_w, v_conv_b, v_q_norm_w, v_k_norm_w, v_rel_bias, v_w_out):
    x2 = x[0]
    tgt = loss_target[0]
    blk = 2 * lax.axis_index("x") + lax.axis_index("y")

    conv_w8 = jnp.pad(conv_w, ((0, 5), (0, 0)))
    wblk, woutblk, cwblk = _wgather(w_in, w_out, conv_w8)
    wout_full = woutblk.reshape(1024, 1024)
    cw_full = cwblk.transpose(1, 0, 2).reshape(8, 512)

    qkw = jnp.concatenate([jnp.tile(q_norm_w, 8) * 0.125, jnp.tile(k_norm_w, 8)])[None, :]
    qkw_raw = jnp.concatenate([jnp.tile(q_norm_w, 8), jnp.tile(k_norm_w, 8)])[None, :]
    gidx = jnp.arange(256) // HEAD_DIM
    b256 = (gidx[:, None] == gidx[None, :]).astype(BF16)

    h, cg, qkr, qkn, vz, qkn4, v4, qkn16, v16, xmt = _proj(x2, tgt, norm_w[None, :], wblk, qkw, b256)

    biases = _bias_tables(rel_bias)
    qkn_l = [qkn[None], qkn4, qkn16]
    v_l = [vz[None], v4, v16]
    o_g, lse_g = [], []
    for gi, d in enumerate(DILATIONS):
        o_l, lse_l = _attn_fwd(qkn_l[gi], v_l[gi], biases, gi, f"attn_fwd_d{d}")
        o_g.append(o_l)
        lse_g.append(lse_l)

    (y, dout, ld1, do1, dza, dgbz, dzc, loss_p, dcb, dcw, do4, ld4, do16, ld16) = _combine(
        o_g, lse_g, cg, vz, xmt, wout_full, cw_full, conv_b[None, :], b256)

    dq_g, dkv_g, dsums = [], [], []
    for gi, (d, do_l, ld_l) in enumerate(zip(DILATIONS, (do1, do4, do16), (ld1, ld4, ld16))):
        dq_l, dkv_l, dsum = _attn_bwd(qkn_l[gi], v_l[gi], do_l, ld_l, biases, gi, f"attn_bwd_d{d}")
        dq_g.append(dq_l)
        dkv_g.append(dkv_l)
        dsums.append(dsum)

    grad_x, dproj, dnw, dqkw = _bwd_tail(dq_g, dkv_g, qkr, qkw_raw, dza, dgbz, dzc, cg, cw_full, wblk,
                                         x2, norm_w[None, :], dout, b256)

    pw_in = _wgrad(h, dproj, False, "wgrad_in")
    pw_out = _wgrad(y, dout, True, "wgrad_out")
    dbias8 = _dbias(dsums, jnp.stack([_diag_bucket_onehot(d) for d in DILATIONS], axis=0))

    small = jnp.concatenate([dnw.reshape(8, 128), dcb.reshape(4, 128), dqkw.reshape(8, 128),
                             dcw[0:3].reshape(12, 128), dbias8, jnp.pad(loss_p, ((0, 7), (0, 0)))], axis=0)
    g_win, g_wout, gsmall = _gsync(pw_in, pw_out, small)

    g_nw = gsmall[0:8].reshape(1024)
    g_cb = gsmall[8:12].reshape(512)
    folded = _fold_heads(gsmall[12:20])
    g_qw, g_kw = folded[0, 0:64], folded[1, 0:64]
    g_cw = lax.dynamic_slice(gsmall[20:32].reshape(3, 512), (0, blk * 128), (3, 128))
    g_rb = gsmall[32:40][:, 0:32].T
    loss = gsmall[40, 0]

    g_win, d_win, nm_win, nv_win = _adamw(w_in, g_win, m_w_in, v_w_in, "adamw_w_in")
    g_wout, d_wout, nm_wout, nv_wout = _adamw(w_out, g_wout, m_w_out, v_w_out, "adamw_w_out")

    def pack(parts):
        rows = [parts[0].reshape(8, 128), parts[1].reshape(4, 128),
                jnp.pad(parts[2], (0, 64))[None, :], jnp.pad(parts[3], (0, 64))[None, :],
                parts[4], jnp.pad(parts[5].T, ((0, 0), (0, 96)))]
        return jnp.concatenate(rows, axis=0)

    ws = pack([norm_w, conv_b, q_norm_w, k_norm_w, conv_w, rel_bias])
    gs = pack([g_nw, g_cb, g_qw, g_kw, g_cw, g_rb])
    ms = pack([m_norm_w, m_conv_b, m_q_norm_w, m_k_norm_w, m_conv_w, m_rel_bias])
    vs = pack([v_norm_w, v_conv_b, v_q_norm_w, v_k_norm_w, v_conv_w, v_rel_bias])
    rpad = lambda a: jnp.pad(a, ((0, 7), (0, 0)))
    _, d_s, nm_s, nv_s = _adamw(rpad(ws), rpad(gs), rpad(ms), rpad(vs), "adamw_small")

    def unpack(a):
        return (a[0:8].reshape(1024), a[12:13, 0:64].reshape(64), a[13:14, 0:64].reshape(64),
                a[14:17], a[8:12].reshape(512), a[17:25, 0:32].T)

    def ordered(nw, win, cw, cb, qw, kw, rb, wout):
        return (nw, win, cw, cb, qw, kw, rb, wout)

    g_un = (g_nw, g_qw, g_kw, g_cw, g_cb, g_rb)
    outs = [loss, grad_x[None]]
    for un, win_v, wout_v in ((g_un, g_win, g_wout), (unpack(d_s), d_win, d_wout),
                              (unpack(nm_s), nm_win, nm_wout), (unpack(nv_s), nv_win, nv_wout)):
        nw, qw, kw, cw, cb, rb = un
        outs.extend(ordered(nw, win_v, cw, cb, qw, kw, rb, wout_v))
    return tuple(outs)
```

```python
import math

import jax
import jax.numpy as jnp
from jax import lax
from jax.experimental import pallas as pl
from jax.experimental.pallas import tpu as pltpu

F32 = jnp.float32
BF16 = jnp.bfloat16
MESH = pl.DeviceIdType.MESH

D_MODEL = 1024
CONV_W = 512
ATTN_W = 512
HEAD_DIM = 64
N_PAIR = 4
DILATIONS = (1, 4, 16)
HALF = 64
QB = 128
KB = QB + 2 * HALF
NUM_BUCKETS = 32
MAX_DISTANCE = 1024
EPS = 1e-6
NEG = -1e30
ADAM_LR, ADAM_B1, ADAM_B2, ADAM_EPS, ADAM_WD, ADAM_STEP = 0.001, 0.9, 0.999, 1e-08, 0.01, 10
VMEM_LIMIT = 48 << 20
VMEM_LIMIT_TAIL = 62 << 20


def _params(sem=None, vmem=VMEM_LIMIT, **kw):
    if sem is not None:
        kw["dimension_semantics"] = sem
    return pltpu.CompilerParams(vmem_limit_bytes=vmem, **kw)


def _sigmoid(z):
    return 1.0 / (1.0 + jnp.exp(-z))


def _group_sum(val, b_ref, split=True):
    hi = val.astype(BF16)
    lo = (val - hi.astype(F32)).astype(BF16) if split else None
    outs = []
    for j in range(val.shape[1] // 256):
        sl = slice(256 * j, 256 * j + 256)
        part = jnp.dot(hi[:, sl], b_ref[...], preferred_element_type=F32)
        if split:
            part = part + jnp.dot(lo[:, sl], b_ref[...], preferred_element_type=F32)
        outs.append(part)
    return outs[0] if len(outs) == 1 else jnp.concatenate(outs, axis=1)


def _t5_bucket(rel):
    half_b = NUM_BUCKETS // 2
    max_exact = half_b // 2
    ret = jnp.where(rel > 0, half_b, 0)
    n = jnp.abs(rel)
    nf = jnp.maximum(n, 1).astype(F32)
    large = max_exact + (jnp.log(nf / max_exact) / math.log(MAX_DISTANCE / max_exact)
                         * (half_b - max_exact)).astype(jnp.int32)
    large = jnp.minimum(large, half_b - 1)
    return ret + jnp.where(n < max_exact, n, large)


def _bias_tables(rel_bias):
    rows = []
    key = jnp.arange(KB)
    for dilation in DILATIONS:
        for variant in range(3):
            off = (0, HALF, 2 * HALF)[variant]
            rel = ((key - off + KB // 2) % KB) - KB // 2
            bkt = _t5_bucket(jnp.clip(rel, -HALF, HALF) * dilation)
            rows.append(jnp.where(jnp.abs(rel) <= HALF, bkt, -1))
    bkt_all = jnp.broadcast_to(jnp.stack(rows, axis=0).astype(jnp.int32)[:, None, :], (9, 8, KB))

    def body(rb_ref, bkt_ref, o_ref):
        bkt = bkt_ref[...]
        off = (pl.program_id(0) % 3) * HALF
        rel = (lax.broadcasted_iota(jnp.int32, (QB, KB), 1) - lax.broadcasted_iota(jnp.int32, (QB, KB), 0)) - off
        band = jnp.abs(rel) <= HALF
        for h in range(8):
            acc = jnp.full((8, KB), NEG, F32)
            for b in range(NUM_BUCKETS):
                acc = jnp.where(bkt == b, rb_ref[b, h], acc)
            rolled = pltpu.roll(jnp.broadcast_to(acc[0:1], (QB, KB)), 0, 1, stride=1, stride_axis=0)
            o_ref[h] = jnp.where(band, rolled, NEG)

    out = pl.pallas_call(
        body, name="bias_tables", grid=(9,),
        out_shape=jax.ShapeDtypeStruct((9, 8, QB, KB), F32),
        in_specs=[pl.BlockSpec(memory_space=pltpu.SMEM), pl.BlockSpec((None, 8, KB), lambda i: (i, 0, 0))],
        out_specs=pl.BlockSpec((None, 8, QB, KB), lambda i: (i, 0, 0, 0)),
        compiler_params=_params(("parallel",)),
    )(rel_bias, bkt_all)
    return out.reshape(3, 3, N_PAIR, 2 * QB, KB)


def _diag_bucket_onehot(dilation):
    out = []
    c = jnp.arange(KB)
    for variant in range(3):
        off = (0, HALF, 2 * HALF)[variant]
        rel = ((c - off + 128) % 256) - 128
        band = jnp.abs(rel) <= HALF
        bkt = _t5_bucket(jnp.clip(rel, -HALF, HALF) * dilation)
        oh = (bkt[:, None] == jnp.arange(128)[None, :]) & band[:, None]
        out.append(oh.astype(F32))
    return jnp.stack(out, axis=0)


def _wgather(w_in, w_out, conv_w):
    rin, rout = w_in.shape[0] // 2, w_out.shape[0] // 2

    def body(win_ref, wout_ref, cw_ref, win_o, wout_o, cw_o, send_sems, recv_sems):
        x, y, c = lax.axis_index("x"), lax.axis_index("y"), lax.axis_index("c")
        b = 2 * x + y
        win_o[b] = win_ref[...].astype(BF16)
        wout_o[b] = wout_ref[...].astype(BF16)
        cw_o[b] = cw_ref[...]
        xnbr, ynbr, diag, sib = (1 - x, y, c), (x, 1 - y, c), (1 - x, 1 - y, c), (x, y, 1 - c)
        bx, by, bd = b ^ 2, b ^ 1, b ^ 3

        def copy(sem, ref, to):
            return pltpu.make_async_remote_copy(src_ref=ref, dst_ref=ref, send_sem=send_sems.at[sem],
                                                recv_sem=recv_sems.at[sem], device_id=to, device_id_type=MESH)

        def rows_of(ref, half):
            def rows(blk, quarter=None):
                if quarter is None:
                    return ref.at[blk, pl.ds(c * half, half), :]
                return ref.at[blk, pl.ds(c * half + quarter * (half // 2), half // 2), :]
            return rows

        def send_own(ref, half, base):
            rows = rows_of(ref, half)
            own_x, own_y = copy(base + 0, rows(b), xnbr), copy(base + 1, rows(b), ynbr)
            own_x.start()
            own_y.start()
            return [own_x, own_y]

        def relay(ref, half, base):
            rows = rows_of(ref, half)
            copy(base + 0, rows(bx), xnbr).wait_recv()
            pass_y = copy(base + 2, rows(bx, 0), ynbr)
            pass_y.start()
            to_sib = [copy(base + 4, rows(bx), sib)]
            to_sib[-1].start()
            copy(base + 1, rows(by), ynbr).wait_recv()
            pass_x = copy(base + 3, rows(by, 1), xnbr)
            pass_x.start()
            to_sib.append(copy(base + 5, rows(by), sib))
            to_sib[-1].start()
            copy(base + 2, rows(bd, 0), ynbr).wait_recv()
            copy(base + 3, rows(bd, 1), xnbr).wait_recv()
            to_sib.append(copy(base + 6, rows(bd), sib))
            to_sib[-1].start()
            return [pass_y, pass_x] + to_sib

        def from_sibling(ref, half, base):
            for k, blk in enumerate((bx, by, bd)):
                copy(base + 4 + k, ref.at[blk, pl.ds((1 - c) * half, half), :], sib).wait_recv()

        small = [copy(14 + k, cw_o.at[b], to) for k, to in enumerate((xnbr, ynbr, diag))]
        for cp in small:
            cp.start()
        started = send_own(win_o, rin, 0) + send_own(wout_o, rout, 7)
        started += relay(win_o, rin, 0) + relay(wout_o, rout, 7)
        for k, blk in enumerate((bx, by, bd)):
            copy(14 + k, cw_o.at[blk], sib).wait_recv()
        from_sibling(win_o, rin, 0)
        from_sibling(wout_o, rout, 7)
        for cp in small + started:
            cp.wait_send()

    vm = pl.BlockSpec(memory_space=pltpu.VMEM)
    return pl.pallas_call(
        body, name="wgather",
        out_shape=(jax.ShapeDtypeStruct((4,) + w_in.shape, BF16),
                   jax.ShapeDtypeStruct((4,) + w_out.shape, BF16),
                   jax.ShapeDtypeStruct((4,) + conv_w.shape, F32)),
        in_specs=[vm, vm, vm], out_specs=(vm, vm, vm),
        scratch_shapes=[pltpu.SemaphoreType.DMA((17,)), pltpu.SemaphoreType.DMA((17,))],
        compiler_params=_params(),
    )(w_in, w_out, conv_w)


TM_MATMUL = 512
TM_COMBINE = 256


def _resident(shape):
    return pl.BlockSpec(shape, lambda i: (0,) * len(shape), pipeline_mode=pl.Buffered(1))


def _to_slabs(slab, val, j0=0):
    for j in range(val.shape[1] // 128):
        slab[j0 + j] = val[:, 128 * j:128 * (j + 1)]


def _scatter_classes(slab, j0, nj, out_ref, d, part=0, mid=None):
    tm = slab.shape[1]
    n = tm // d
    if d == 4:
        for r in range(d):
            for j in range(nj):
                out_ref[r, part * n:(part + 1) * n, 128 * j:128 * (j + 1)] = (
                    slab[j0 + j, pl.ds(r, n, stride=d), :].astype(out_ref.dtype))
        return
    q = tm // 4
    for lo in range(4):
        for j in range(nj):
            mid[j0 + j, lo * q:(lo + 1) * q, :] = slab[j0 + j, pl.ds(lo, q, stride=4), :]
    for hi in range(4):
        for lo in range(4):
            for j in range(nj):
                out_ref[4 * hi + lo, part * n:(part + 1) * n, 128 * j:128 * (j + 1)] = (
                    mid[j0 + j, pl.ds(lo * q + hi, n, stride=4), :].astype(out_ref.dtype))


def _gather_classes(slab, piece, nj, d, mid=None):
    tm = slab.shape[1]
    n = tm // d
    if d == 4:
        for r in range(d):
            for j in range(nj):
                slab[j, pl.ds(r, n, stride=d), :] = piece(r, j).astype(F32)
    else:
        q = tm // 4
        for hi in range(4):
            for lo in range(4):
                for j in range(nj):
                    mid[j, pl.ds(lo * q + hi, n, stride=4), :] = piece(4 * hi + lo, j).astype(F32)
        for lo in range(4):
            for j in range(nj):
                slab[j, pl.ds(lo, q, stride=4), :] = mid[j, lo * q:(lo + 1) * q, :]
    return jnp.concatenate([slab[j] for j in range(nj)], axis=1)


def _class_spec(d, width, tm):
    return pl.BlockSpec((d, tm // d, width), lambda i: (0, i, 0))


def _proj(x, tgt, norm_w, wblk, qkw, b256):
    s = x.shape[0]
    tm = TM_MATMUL
    nparts = 2
    tp = tm // nparts

    def body(x_ref, t_ref, nw_ref, w_ref, qkw_ref, b_ref, h_o, cg_o, qkr_o, qkn_o, vz_o, qkn4_o, v4_o, qkn16_o,
             v16_o, xmt_o, slabs, mids):
        for part in range(nparts):
            rows = slice(part * tp, (part + 1) * tp)
            slab = slabs.at[part]
            xf = x_ref[rows, :]
            xmt_o[rows, :] = xf - t_ref[rows, :]
            r = lax.rsqrt(jnp.mean(xf * xf, axis=-1, keepdims=True) + EPS)
            h = (xf * r * nw_ref[...]).astype(BF16)
            h_o[rows, :] = h
            p2 = jnp.dot(h, w_ref[2], preferred_element_type=F32)
            qkr_o[rows, :] = p2.astype(BF16)
            ss = _group_sum(p2 * p2, b_ref, split=False)
            rr = lax.rsqrt(ss * (1.0 / HEAD_DIM) + EPS)
            qkn = p2 * rr * qkw_ref[...]
            qkn_o[rows, :] = qkn.astype(BF16)
            _to_slabs(slab, qkn)
            p3 = jnp.dot(h, w_ref[3], preferred_element_type=F32)
            vz_o[rows, :] = p3.astype(BF16)
            _to_slabs(slab, p3[:, 0:512], 8)
            cg_o[rows, 0:1024] = jnp.dot(h, w_ref[0], preferred_element_type=F32).astype(BF16)
            cg_o[rows, 1024:2048] = jnp.dot(h, w_ref[1], preferred_element_type=F32).astype(BF16)
            for d, q_o, v_o in ((4, qkn4_o, v4_o), (16, qkn16_o, v16_o)):
                _scatter_classes(slab, 0, 8, q_o, d, part, mids.at[part])
                _scatter_classes(slab, 8, 4, v_o, d, part, mids.at[part])

    row = lambda w: pl.BlockSpec((tm, w), lambda i: (i, 0))
    full = lambda shp: pl.BlockSpec(shp, lambda i: (0,) * len(shp))
    nat = lambda w: jax.ShapeDtypeStruct((s, w), BF16)
    cls = lambda d, w: jax.ShapeDtypeStruct((d, s // d, w), BF16)
    return pl.pallas_call(
        body, name="proj", grid=(s // tm,),
        out_shape=(nat(1024), nat(2048), nat(1024), nat(1024), nat(1024),
                   cls(4, 1024), cls(4, 512), cls(16, 1024), cls(16, 512), jax.ShapeDtypeStruct((s, 1024), F32)),
        in_specs=[row(1024), row(1024), full((1, 1024)), _resident((4, 1024, 1024)), full((1, 1024)),
                  full((256, 256))],
        out_specs=(row(1024), row(2048), row(1024), row(1024), row(1024),
                   _class_spec(4, 1024, tm), _class_spec(4, 512, tm),
                   _class_spec(16, 1024, tm), _class_spec(16, 512, tm), row(1024)),
        scratch_shapes=[pltpu.VMEM((nparts, 12, tp, 128), F32), pltpu.VMEM((nparts, 12, tp, 128), F32)],
        compiler_params=_params(("parallel",)),
    )(x, tgt, norm_w, wblk, qkw, b256)


def _block_coords(t, i, nsub, nb, length):
    n = t * nsub + i
    q0 = i * QB
    start = pl.multiple_of(jnp.clip(n * QB - HALF, 0, length - KB), HALF)
    variant = jnp.where(n == 0, 0, jnp.where(n == nb - 1, 2, 1))
    return q0, start, variant


def _classes_per_step(r_cls, length, qt):
    return 2 if (length == qt and qt // QB <= 8 and r_cls % 2 == 0) else 1


def _split_heads(a, lo):
    zero = jnp.zeros_like(a)
    return jnp.concatenate([jnp.where(lo, a, zero), jnp.where(lo, zero, a)], axis=0)


def _col_pair(ref, q0, lane):
    return jnp.concatenate([ref[pl.ds(q0, QB), lane:lane + 1],
                            ref[pl.ds(q0, QB), HEAD_DIM + lane:HEAD_DIM + lane + 1]], axis=0)


def _attn_fwd(qkn_l, v_l, bias, gi, name):
    r_cls, length, _ = qkn_l.shape
    qt = min(length, 2048)
    nb, nsub = length // QB, qt // QB
    cb = _classes_per_step(r_cls, length, qt)

    def body(q_ref, k_ref, v_ref, b_ref, o_ref, lse_ref):
        t = pl.program_id(2)
        lo = lax.broadcasted_iota(jnp.int32, (QB, 128), 1) < HEAD_DIM

        starts, logits = [], []
        for ci in range(cb):
            for i in range(nsub):
                _, start, variant = _block_coords(t, i, nsub, nb, length)
                qq = _split_heads(q_ref[ci, i * QB:(i + 1) * QB, :], lo)
                k = k_ref[ci, pl.ds(start, KB), :]
                logits.append(lax.dot_general(qq, k, (((1,), (1,)), ((), ())), preferred_element_type=F32)
                              + b_ref[variant])
                starts.append(start)
        lg = jnp.concatenate(logits, axis=0)
        m = jnp.max(lg, axis=-1, keepdims=True)
        p = jnp.exp(lg - m)
        pb = p.astype(BF16)
        l = jnp.sum(p, axis=-1, keepdims=True)
        lse = jnp.broadcast_to(m + jnp.log(l), (cb * nsub * 2 * QB, 128))
        inv = 1.0 / l
        for ci in range(cb):
            for i in range(nsub):
                j = ci * nsub + i
                rows = slice(2 * QB * j, 2 * QB * (j + 1))
                v = v_ref[ci, pl.ds(starts[j], KB), :]
                pv = jnp.dot(pb[rows], v, preferred_element_type=F32) * inv[rows]
                o_ref[ci, i * QB:(i + 1) * QB, :] = jnp.where(lo, pv[0:QB], pv[QB:2 * QB]).astype(BF16)
                ls = lse[rows]
                lse_ref[ci, i * QB:(i + 1) * QB, :] = jnp.where(lo, ls[0:QB], ls[QB:2 * QB])

    return pl.pallas_call(
        body, name=name, grid=(N_PAIR, r_cls // cb, length // qt),
        out_shape=(jax.ShapeDtypeStruct((r_cls, length, 512), BF16),
                   jax.ShapeDtypeStruct((r_cls, length, 512), F32)),
        in_specs=[pl.BlockSpec((cb, qt, 128), lambda p, r, t: (r, t, p)),
                  pl.BlockSpec((cb, length, 128), lambda p, r, t: (r, 0, 4 + p)),
                  pl.BlockSpec((cb, length, 128), lambda p, r, t: (r, 0, p)),
                  pl.BlockSpec((None, 3, None, 2 * QB, KB), lambda p, r, t: (gi, 0, p, 0, 0))],
        out_specs=(pl.BlockSpec((cb, qt, 128), lambda p, r, t: (r, t, p)),
                   pl.BlockSpec((cb, qt, 128), lambda p, r, t: (r, t, p))),
        compiler_params=_params(("parallel", "parallel", "arbitrary")),
    )(qkn_l, qkn_l, v_l, bias)


def _combine(o_g, lse_g, cg, vz, xmt, wout, cw, cb, b256):
    s = xmt.shape[0]
    tm = TM_MATMUL
    hb = 16
    nt = s // tm

    def body(o1, o4, o16, l1, l4, l16, cg_hbm, cgp_ref, cgn_ref, za_ref, xmt_hbm, w_ref, cw_ref, cb_ref,
             b_ref, y_o, dout_o, ld1_o, do1_o, dza_o, dgbz_o, dzc_o, loss_o, dcb_o, dcw_o,
             do4_o, ld4_o, do16_o, ld16_o, slab, mid, cg_ring, xmt_ring, ring_sems):
        i = pl.program_id(0)

        def fetch(step, slot):
            rows = pl.ds(pl.multiple_of(step * tm, tm), tm)
            return (pltpu.make_async_copy(cg_hbm.at[rows, :], cg_ring.at[slot], ring_sems.at[0, slot]),
                    pltpu.make_async_copy(xmt_hbm.at[rows, :], xmt_ring.at[slot], ring_sems.at[1, slot]))

        @pl.when(i == 0)
        def _():
            loss_o[...] = jnp.zeros_like(loss_o)
            dcb_o[...] = jnp.zeros_like(dcb_o)
            dcw_o[...] = jnp.zeros_like(dcw_o)
            for first in range(2):
                for cp in fetch(first, first):
                    cp.start()

        @pl.when(i + 2 < nt)
        def _():
            for cp in fetch(i + 2, (i + 2) % 3):
                cp.start()

        slot = i % 3
        for cp in fetch(i, slot):
            cp.wait()
        cg_ref = cg_ring.at[slot]
        xmt_ref = xmt_ring.at[slot]

        u = cg_ref[:, 0:512].astype(F32)
        gb = cg_ref[:, 512:1024].astype(F32)
        gc = cg_ref[:, 1024:1536].astype(F32)
        zc = cg_ref[:, 1536:2048].astype(F32)
        tt = gc * u
        t_prev = cgp_ref[hb - 1:hb, 0:512].astype(F32) * cgp_ref[hb - 1:hb, 1024:1536].astype(F32)
        t_next = cgn_ref[0:1, 0:512].astype(F32) * cgn_ref[0:1, 1024:1536].astype(F32)
        t_prev = jnp.where(i == 0, 0.0, t_prev)
        t_next = jnp.where(i == nt - 1, 0.0, t_next)
        rows = lax.broadcasted_iota(jnp.int32, (tm, 512), 0)
        t_up = jnp.where(rows == 0, t_prev, pltpu.roll(tt, 1, 0))
        t_dn = jnp.where(rows == tm - 1, t_next, pltpu.roll(tt, tm - 1, 0))
        w0, w1, w2 = cw_ref[0:1, :], cw_ref[1:2, :], cw_ref[2:3, :]
        zb = w0 * t_up + w1 * tt + w2 * t_dn + cb_ref[...]
        sg = _sigmoid(zc)
        sz = zc * sg
        y_conv = gb * zb * sz

        a1, p1 = l1[0], o1[0].astype(F32)
        a4 = _gather_classes(slab, lambda r, j: l4[r, :, 128 * j:128 * (j + 1)], 4, 4)
        p4 = _gather_classes(slab, lambda r, j: o4[r, :, 128 * j:128 * (j + 1)], 4, 4)
        a16 = _gather_classes(slab, lambda r, j: l16[r, :, 128 * j:128 * (j + 1)], 4, 16, mid)
        p16 = _gather_classes(slab, lambda r, j: o16[r, :, 128 * j:128 * (j + 1)], 4, 16, mid)
        m = jnp.maximum(jnp.maximum(a1, a4), a16)
        e1, e4, e16 = jnp.exp(a1 - m), jnp.exp(a4 - m), jnp.exp(a16 - m)
        den = e1 + e4 + e16
        lse = m + jnp.log(den)
        o = (e1 * p1 + e4 * p4 + e16 * p16) / den
        za = za_ref[...].astype(F32)
        sga = _sigmoid(za)
        sa = za * sga
        y = jnp.concatenate([y_conv, o * sa], axis=1).astype(BF16)
        y_o[...] = y

        diff = xmt_ref[...] + jnp.dot(y, w_ref[...], preferred_element_type=F32)
        loss_o[...] += (0.5 / D_MODEL) * jnp.sum(diff * diff)
        dout = diff * (1.0 / D_MODEL)
        dout_o[...] = dout
        dy = lax.dot_general(dout.astype(BF16), w_ref[...], (((1,), (1,)), ((), ())), preferred_element_type=F32)
        dyc, dya = dy[:, 0:512], dy[:, 512:1024]

        do = dya * sa
        dza_o[...] = (dya * o * (sga * (1.0 + za * (1.0 - sga)))).astype(BF16)
        lane = lax.broadcasted_iota(jnp.int32, (tm, 512), 1)
        ld = jnp.where((lane & (HEAD_DIM - 1)) < HEAD_DIM // 2, lse, _group_sum(do * o, b_ref))
        do1_o[0] = do.astype(BF16)
        ld1_o[0] = ld
        _to_slabs(slab, do)
        _scatter_classes(slab, 0, 4, do4_o, 4)
        _scatter_classes(slab, 0, 4, do16_o, 16, 0, mid)
        _to_slabs(slab, ld)
        _scatter_classes(slab, 0, 4, ld4_o, 4)
        _scatter_classes(slab, 0, 4, ld16_o, 16, 0, mid)

        dzc = dyc * sz * gb
        dzc_o[...] = dzc.astype(BF16)
        dgbz_o[:, 0:512] = (dyc * sz * zb).astype(BF16)
        dgbz_o[:, 512:1024] = (dyc * gb * zb * (sg * (1.0 + zc * (1.0 - sg)))).astype(BF16)
        dcb_o[...] += jnp.sum(dzc, axis=0, keepdims=True)
        dcw_o[0:1, :] += jnp.sum(dzc * t_up, axis=0, keepdims=True)
        dcw_o[1:2, :] += jnp.sum(dzc * tt, axis=0, keepdims=True)
        dcw_o[2:3, :] += jnp.sum(dzc * t_dn, axis=0, keepdims=True)

    row = lambda w, j=0: pl.BlockSpec((tm, w), lambda i: (i, j))
    full = lambda shp: pl.BlockSpec(shp, lambda i: (0,) * len(shp))
    prev = pl.BlockSpec((hb, 2048), lambda i: (jnp.maximum(i * (tm // hb) - 1, 0), 0))
    nxt = pl.BlockSpec((hb, 2048), lambda i: (jnp.minimum((i + 1) * (tm // hb), s // hb - 1), 0))
    cls = lambda d, dt: jax.ShapeDtypeStruct((d, s // d, 512), dt)
    cspecs = [_class_spec(d, 512, tm) for d in DILATIONS]
    return pl.pallas_call(
        body, name="combine", grid=(nt,),
        out_shape=(jax.ShapeDtypeStruct((s, 1024), BF16), jax.ShapeDtypeStruct((s, 1024), F32),
                   cls(1, F32), cls(1, BF16), jax.ShapeDtypeStruct((s, 512), BF16),
                   jax.ShapeDtypeStruct((s, 1024), BF16), jax.ShapeDtypeStruct((s, 512), BF16),
                   jax.ShapeDtypeStruct((1, 128), F32), jax.ShapeDtypeStruct((1, 512), F32),
                   jax.ShapeDtypeStruct((8, 512), F32),
                   cls(4, BF16), cls(4, F32), cls(16, BF16), cls(16, F32)),
        in_specs=cspecs + cspecs + [pl.BlockSpec(memory_space=pl.ANY), prev, nxt, row(512, 1),
                                    pl.BlockSpec(memory_space=pl.ANY),
                                    _resident((1024, 1024)), full((8, 512)), full((1, 512)), full((256, 256))],
        out_specs=(row(1024), row(1024), cspecs[0], cspecs[0], row(512), row(1024), row(512),
                   full((1, 128)), full((1, 512)), full((8, 512)),
                   cspecs[1], cspecs[1], cspecs[2], cspecs[2]),
        scratch_shapes=[pltpu.VMEM((4, tm, 128), F32), pltpu.VMEM((4, tm, 128), F32),
                        pltpu.VMEM((3, tm, 2048), BF16), pltpu.VMEM((3, tm, 1024), F32),
                        pltpu.SemaphoreType.DMA((2, 3))],
        compiler_params=_params(("arbitrary",), vmem=VMEM_LIMIT_TAIL),
    )(*o_g, *lse_g, cg, cg, cg, vz, xmt, wout, cw, cb, b256)


def _attn_bwd(qkn_l, v_l, do_l, ld_l, bias, gi, name):
    r_cls, length, _ = qkn_l.shape
    qt = min(length, 2048 if length <= 4096 else 1024)
    nb, nsub, nt = length // QB, qt // QB, length // qt
    chunk = min(length, 4096)
    nchunk = length // chunk
    cb = _classes_per_step(r_cls, length, qt)
    nrb = r_cls // cb
    nbuf = min(4, 2 * nchunk * cb)

    def body(q_ref, k_ref, v_ref, do_ref, ld_ref, b_ref, dq_ref, dkv_hbm, dsum_ref, dk_acc, dv_acc, stage, sems):
        p_id, r, t = pl.program_id(0), pl.program_id(1), pl.program_id(2)
        lo = lax.broadcasted_iota(jnp.int32, (QB, 128), 1) < HEAD_DIM

        @pl.when(t == 0)
        def _():
            dk_acc[...] = jnp.zeros_like(dk_acc)
            dv_acc[...] = jnp.zeros_like(dv_acc)

        @pl.when((t == 0) & (r == 0))
        def _():
            dsum_ref[...] = jnp.zeros_like(dsum_ref)

        nt_dims = (((1,), (1,)), ((), ()))
        tn_dims = (((0,), (0,)), ((), ()))
        coords, qqs, dds, logits, dps, lcols, dcols = [], [], [], [], [], [], []
        for ci in range(cb):
            for i in range(nsub):
                q0, start, variant = _block_coords(t, i, nsub, nb, length)
                qq = _split_heads(q_ref[ci, q0:q0 + QB, :], lo)
                dd = _split_heads(do_ref[ci, q0:q0 + QB, :], lo)
                k = k_ref[ci, pl.ds(start, KB), :]
                v = v_ref[ci, pl.ds(start, KB), :]
                logits.append(lax.dot_general(qq, k, nt_dims, preferred_element_type=F32) + b_ref[variant])
                dps.append(lax.dot_general(dd, v, nt_dims, preferred_element_type=F32))
                lcols.append(_col_pair(ld_ref.at[ci], q0, 0))
                dcols.append(_col_pair(ld_ref.at[ci], q0, HEAD_DIM // 2))
                coords.append((ci, i, q0, start, variant))
                qqs.append(qq)
                dds.append(dd)
        p = jnp.exp(jnp.concatenate(logits, axis=0) - jnp.concatenate(lcols, axis=0))
        ds = p * (jnp.concatenate(dps, axis=0) - jnp.concatenate(dcols, axis=0))
        pb = p.astype(BF16)
        dsb = ds.astype(BF16)
        middle = None
        for j, (ci, i, q0, start, variant) in enumerate(coords):
            rows = slice(2 * QB * j, 2 * QB * (j + 1))
            if 0 < i < nsub - 1:
                middle = ds[rows] if middle is None else middle + ds[rows]
            else:
                dsum_ref[variant] += ds[rows]
            dqq = jnp.dot(dsb[rows], k_ref[ci, pl.ds(start, KB), :], preferred_element_type=F32)
            dq_ref[ci, q0:q0 + QB, :] = jnp.where(lo, dqq[0:QB], dqq[QB:2 * QB]).astype(BF16)
            dk_acc[ci, pl.ds(start, KB), :] += lax.dot_general(dsb[rows], qqs[j], tn_dims,
                                                               preferred_element_type=F32)
            dv_acc[ci, pl.ds(start, KB), :] += lax.dot_general(pb[rows], dds[j], tn_dims,
                                                               preferred_element_type=F32)
        if middle is not None:
            dsum_ref[1] += middle

        @pl.when(t == nt - 1)
        def _():
            def copy(k):
                ci, which, c = k // (2 * nchunk), (k // nchunk) % 2, k % nchunk
                rows = pl.ds(c * chunk, chunk)
                return pltpu.make_async_copy(stage.at[k % nbuf], dkv_hbm.at[r * cb + ci, p_id, which, rows, :],
                                             sems.at[k % nbuf])

            for k in range(2 * nchunk * cb):
                if k < nbuf:
                    @pl.when((p_id > 0) | (r > 0))
                    def _():
                        copy(k).wait()
                else:
                    copy(k).wait()
                acc = (dk_acc, dv_acc)[(k // nchunk) % 2]
                stage[k % nbuf] = acc[k // (2 * nchunk), pl.ds((k % nchunk) * chunk, chunk), :].astype(BF16)
                copy(k).start()

            @pl.when((p_id == N_PAIR - 1) & (r == nrb - 1))
            def _():
                for k in range(nbuf):
                    copy(k).wait()

    qspec = pl.BlockSpec((cb, qt, 128), lambda p, r, t: (r, t, p))
    return pl.pallas_call(
        body, name=name, grid=(N_PAIR, nrb, nt),
        out_shape=(jax.ShapeDtypeStruct((r_cls, length, 512), BF16),
                   jax.ShapeDtypeStruct((r_cls, N_PAIR, 2, length, 128), BF16),
                   jax.ShapeDtypeStruct((N_PAIR, 3, 2 * QB, KB), F32)),
        in_specs=[qspec,
                  pl.BlockSpec((cb, length, 128), lambda p, r, t: (r, 0, 4 + p)),
                  pl.BlockSpec((cb, length, 128), lambda p, r, t: (r, 0, p)),
                  qspec, qspec,
                  pl.BlockSpec((None, 3, None, 2 * QB, KB), lambda p, r, t: (gi, 0, p, 0, 0))],
        out_specs=(qspec, pl.BlockSpec(memory_space=pl.ANY),
                   pl.BlockSpec((None, 3, 2 * QB, KB), lambda p, r, t: (p, 0, 0, 0))),
        scratch_shapes=[pltpu.VMEM((cb, length, 128), F32), pltpu.VMEM((cb, length, 128), F32),
                        pltpu.VMEM((nbuf, chunk, 128), BF16), pltpu.SemaphoreType.DMA((nbuf,))],
        compiler_params=_params(("arbitrary", "arbitrary", "arbitrary")),
    )(qkn_l, qkn_l, v_l, do_l, ld_l, bias)


def _bwd_tail(dq_g, dkv_g, qkr, qkw, dza, dgbz, dzc, cg, cw, wblk, x, norm_w, dout, b256):
    s = x.shape[0]
    tm = TM_MATMUL
    hb = 16
    nt = s // tm

    def body(dq1, dq4, dq16, dkv1, dkv4, dkv16, qkr_ref, qkw_ref, dza_ref, dgbz_ref, dzc_ref,
             dzp_ref, dzn_ref, u_ref, gc_ref, cw_ref, w_ref, x_ref, nw_ref, dout_ref, b_ref,
             gx_o, dproj_o, dnw_o, dqkw_o, slab, mid):
        i = pl.program_id(0)

        def nat_q(ref, d):
            return _gather_classes(slab, lambda r, j: ref[r, :, 128 * j:128 * (j + 1)], 4, d, mid)

        def nat_kv(ref, d, which):
            return _gather_classes(slab, lambda r, j: ref[r, j, which], 4, d, mid)

        @pl.when(i == 0)
        def _():
            dnw_o[...] = jnp.zeros_like(dnw_o)
            dqkw_o[...] = jnp.zeros_like(dqkw_o)

        dzc = dzc_ref[...].astype(F32)
        d_prev = jnp.where(i == 0, 0.0, dzp_ref[hb - 1:hb, :].astype(F32))
        d_next = jnp.where(i == nt - 1, 0.0, dzn_ref[0:1, :].astype(F32))
        rows = lax.broadcasted_iota(jnp.int32, (tm, 512), 0)
        d_up = jnp.where(rows == 0, d_prev, pltpu.roll(dzc, 1, 0))
        d_dn = jnp.where(rows == tm - 1, d_next, pltpu.roll(dzc, tm - 1, 0))
        dt = cw_ref[0:1, :] * d_dn + cw_ref[1:2, :] * dzc + cw_ref[2:3, :] * d_up
        u = u_ref[...].astype(F32)
        gc = gc_ref[...].astype(F32)
        dproj_o[:, 0:512] = (dt * gc).astype(BF16)
        dproj_o[:, 512:1024] = dgbz_ref[:, 0:512]
        dproj_o[:, 1024:1536] = (dt * u).astype(BF16)
        dproj_o[:, 1536:2048] = dgbz_ref[:, 512:1024]

        dqn = (dq1[0].astype(F32) + nat_q(dq4, 4) + nat_q(dq16, 16)) * (1.0 / 8.0)
        dk1 = jnp.concatenate([dkv1[0, j, 0] for j in range(N_PAIR)], axis=1)
        dv1 = jnp.concatenate([dkv1[0, j, 1] for j in range(N_PAIR)], axis=1)
        dkn = dk1 + nat_kv(dkv4, 4, 0) + nat_kv(dkv16, 16, 0)
        dvn = dv1 + nat_kv(dkv4, 4, 1) + nat_kv(dkv16, 16, 1)
        g = jnp.concatenate([dqn, dkn], axis=1) * qkw_ref[...]
        raw = qkr_ref[...].astype(F32)
        rr = lax.rsqrt(_group_sum(raw * raw, b_ref, split=False) * (1.0 / HEAD_DIM) + EPS)
        proj_gq = _group_sum(g * raw, b_ref) * (1.0 / HEAD_DIM)
        draw = rr * g - raw * (rr * rr * rr) * proj_gq
        dqkw_o[...] += jnp.sum(jnp.concatenate([dqn, dkn], axis=1) * raw * rr, axis=0, keepdims=True)
        dproj_o[:, 2048:3072] = draw.astype(BF16)
        dproj_o[:, 3072:3584] = dvn.astype(BF16)
        dproj_o[:, 3584:4096] = dza_ref[...]

        nt_dims = (((1,), (1,)), ((), ()))
        dh = lax.dot_general(dproj_o[:, 0:1024], w_ref[0], nt_dims, preferred_element_type=F32)
        for b in range(1, 4):
            dh += lax.dot_general(dproj_o[:, 1024 * b:1024 * b + 1024], w_ref[b], nt_dims,
                                  preferred_element_type=F32)

        xf = x_ref[...]
        r = lax.rsqrt(jnp.mean(xf * xf, axis=-1, keepdims=True) + EPS)
        gh = dh * nw_ref[...]
        dnw_o[...] += jnp.sum(dh * xf * r, axis=0, keepdims=True)
        mean_gx = jnp.mean(gh * xf, axis=-1, keepdims=True)
        gx_o[...] = dout_ref[...] + r * gh - xf * (r * r * r) * mean_gx

    row = lambda w, j=0: pl.BlockSpec((tm, w), lambda i: (i, j))
    full = lambda shp: pl.BlockSpec(shp, lambda i: (0,) * len(shp))
    prev = pl.BlockSpec((hb, 512), lambda i: (jnp.maximum(i * (tm // hb) - 1, 0), 0))
    nxt = pl.BlockSpec((hb, 512), lambda i: (jnp.minimum((i + 1) * (tm // hb), s // hb - 1), 0))
    return pl.pallas_call(
        body, name="bwd_tail", grid=(nt,),
        out_shape=(jax.ShapeDtypeStruct((s, 1024), F32), jax.ShapeDtypeStruct((s, 4096), BF16),
                   jax.ShapeDtypeStruct((1, 1024), F32), jax.ShapeDtypeStruct((1, 1024), F32)),
        in_specs=[_class_spec(d, 512, tm) for d in DILATIONS]
        + [pl.BlockSpec((d, N_PAIR, 2, tm // d, 128), lambda i: (0, 0, 0, i, 0)) for d in DILATIONS]
        + [row(1024), full((1, 1024)), row(512), row(1024), row(512), prev, nxt,
           row(512, 0), row(512, 2), full((8, 512)), _resident((4, 1024, 1024)), row(1024),
           full((1, 1024)), row(1024), full((256, 256))],
        out_specs=(row(1024), row(4096), full((1, 1024)), full((1, 1024))),
        scratch_shapes=[pltpu.VMEM((4, tm, 128), F32), pltpu.VMEM((4, tm, 128), F32)],
        compiler_params=_params(("arbitrary",), vmem=VMEM_LIMIT_TAIL),
    )(*dq_g, *dkv_g, qkr, qkw, dza, dgbz, dzc, dzc, dzc, cg, cg, cw, wblk, x, norm_w, dout, b256)


def _wgrad(a, b, row_blocked, name):
    s, m = a.shape
    n = b.shape[1]
    tk = 2048
    ncol = min(n, 2048)
    nj, nk = n // ncol, s // tk

    def body(a_ref, b_ref, o_ref, acc):
        kk = pl.program_id(1)

        @pl.when(kk == 0)
        def _():
            acc[...] = jnp.zeros_like(acc)

        acc[...] += lax.dot_general(a_ref[...], b_ref[...].astype(BF16), (((0,), (0,)), ((), ())),
                                    preferred_element_type=F32)

        @pl.when(kk == nk - 1)
        def _():
            blocks, _, rows, _ = o_ref.shape
            for blk in range(blocks):
                for half in range(2):
                    if row_blocked:
                        r0 = (2 * blk + half) * rows
                        o_ref[blk, half] = acc[r0:r0 + rows, :].astype(BF16)
                    else:
                        o_ref[blk, half] = acc[half * rows:(half + 1) * rows,
                                               1024 * blk:1024 * (blk + 1)].astype(BF16)

    if row_blocked:
        out_shape = jax.ShapeDtypeStruct((4, 2, m // 8, 1024), BF16)
        out_spec = pl.BlockSpec((4, 2, m // 8, 1024), lambda j, k: (0, 0, 0, 0))
    else:
        out_shape = jax.ShapeDtypeStruct((n // 1024, 2, m // 2, 1024), BF16)
        out_spec = pl.BlockSpec((ncol // 1024, 2, m // 2, 1024), lambda j, k: (j, 0, 0, 0))
    return pl.pallas_call(
        body, name=name, grid=(nj, nk),
        out_shape=out_shape,
        in_specs=[pl.BlockSpec((tk, m), lambda j, k: (k, 0)), pl.BlockSpec((tk, ncol), lambda j, k: (k, j))],
        out_specs=out_spec,
        scratch_shapes=[pltpu.VMEM((m, ncol), F32)],
        compiler_params=_params(("parallel", "arbitrary")),
    )(a, b)


def _dbias(dsums, onehot_all):
    def body(ds1_ref, ds4_ref, ds16_ref, oh_ref, o_ref):
        @pl.when(pl.program_id(0) == 0)
        def _():
            o_ref[...] = jnp.zeros_like(o_ref)

        hrow = lax.broadcasted_iota(jnp.int32, (8, KB), 0)
        flip = (lax.broadcasted_iota(jnp.int32, (QB, QB), 0)
                + lax.broadcasted_iota(jnp.int32, (QB, QB), 1) == QB - 1).astype(F32)

        def diagonal_sums(tile):
            rev = jnp.dot(flip, tile, preferred_element_type=F32, precision=lax.Precision.HIGHEST)
            sums = jnp.sum(pltpu.roll(rev, 0, 1, stride=1, stride_axis=0), axis=0, keepdims=True)
            return pltpu.roll(sums, KB - (QB - 1), 1)

        for g, ds_ref in enumerate((ds1_ref, ds4_ref, ds16_ref)):
            diag = jnp.zeros((8, KB), F32)
            for p in range(N_PAIR):
                diag = jnp.where(hrow == 2 * p, diagonal_sums(ds_ref[p, 0:QB, :]), diag)
                diag = jnp.where(hrow == 2 * p + 1, diagonal_sums(ds_ref[p, QB:2 * QB, :]), diag)
            o_ref[...] += jnp.dot(diag, oh_ref[g], preferred_element_type=F32, precision=lax.Precision.HIGHEST)

    ds_spec = pl.BlockSpec((N_PAIR, None, 2 * QB, KB), lambda v: (0, v, 0, 0))
    return pl.pallas_call(
        body, name="dbias", grid=(3,),
        out_shape=jax.ShapeDtypeStruct((8, 128), F32),
        in_specs=[ds_spec, ds_spec, ds_spec, pl.BlockSpec((3, None, KB, 128), lambda v: (0, v, 0, 0))],
        out_specs=pl.BlockSpec((8, 128), lambda v: (0, 0)),
        compiler_params=_params(("arbitrary",)),
    )(*dsums, onehot_all)


def _gsync(pw_in, pw_out, small):
    hin, hout = pw_in.shape[2], pw_out.shape[2]
    nsmall = small.shape[0]

    def body(pin_hbm, pout_hbm, small_ref, gin_o, gout_o, small_o,
             mine_in, recv_in, s1_in, r1_in, s2_in, r2_in, mine_out, recv_out, s1_out, r1_out, s2_out, r2_out, gather,
             lsem, asend, arecv, bsend, brecv, csend, crecv, ssend, srecv):
        x, y, c = lax.axis_index("x"), lax.axis_index("y"), lax.axis_index("c")
        b = 2 * x + y
        dev = 4 * x + 2 * y + c
        sib = (x, y, 1 - c)
        xnbr, ynbr = (1 - x, y, c), (x, 1 - y, c)
        bx, by, bd = b ^ 2, b ^ 1, b ^ 3

        def rcopy(src, dst, ssem, rsem, to):
            return pltpu.make_async_remote_copy(src_ref=src, dst_ref=dst, send_sem=ssem, recv_sem=rsem,
                                                device_id=to, device_id_type=MESH)

        gather[dev] = small_ref[...]
        s_sends = []
        for k in range(1, 8):
            to = (x ^ (k >> 2), y ^ ((k >> 1) & 1), c ^ (k & 1))
            cp = rcopy(gather.at[dev], gather.at[dev], ssend.at[k - 1], srecv.at[k - 1], to)
            cp.start()
            s_sends.append(cp)

        a_in = rcopy(pin_hbm.at[:, 1 - c], recv_in, asend.at[0], arecv.at[0], sib)
        a_out = rcopy(pout_hbm.at[:, 1 - c], recv_out, asend.at[1], arecv.at[1], sib)
        a_in.start()
        a_out.start()
        l_in = pltpu.make_async_copy(pin_hbm.at[:, c], mine_in, lsem.at[0])
        l_out = pltpu.make_async_copy(pout_hbm.at[:, c], mine_out, lsem.at[1])
        l_in.start()
        l_out.start()
        l_in.wait()
        l_out.wait()

        def phase_one(a_cp, mine, recv, s1, r1, half, base):
            a_cp.wait_recv()
            q = half // 2
            sends = []
            for part, (peer, blk_peer) in enumerate(((xnbr, bx), (ynbr, by))):
                rows = pl.ds(part * q, q)
                for slot, blk in enumerate((blk_peer, bd)):
                    s1[part, slot] = (mine[blk, rows, :].astype(F32) + recv[blk, rows, :].astype(F32)).astype(BF16)
                cp = rcopy(s1.at[part], r1.at[part], bsend.at[base + part], brecv.at[base + part], peer)
                cp.start()
                sends.append(cp)
            return sends

        def phase_two(p1, mine, recv, r1, s2, r2, half, base):
            q = half // 2
            own, sends = [], []
            for part, (peer, blk_next) in enumerate(((ynbr, by), (xnbr, bx))):
                rows = pl.ds(part * q, q)
                p1[part].wait_recv()
                own.append(mine[b, rows, :].astype(F32) + recv[b, rows, :].astype(F32) + r1[part, 0].astype(F32))
                s2[part] = (mine[blk_next, rows, :].astype(F32) + recv[blk_next, rows, :].astype(F32)
                            + r1[part, 1].astype(F32)).astype(BF16)
                cp = rcopy(s2.at[part], r2.at[part], bsend.at[base + 2 + part], brecv.at[base + 2 + part], peer)
                cp.start()
                sends.append(cp)
            return own, sends

        def stage_c(own, p2, r2, g_o, half, idx):
            q = half // 2
            for part in range(2):
                p2[part].wait_recv()
                g_o[pl.ds(pl.multiple_of(c * half + part * q, q), q), :] = own[part] + r2[part].astype(F32)
            rows = g_o.at[pl.ds(pl.multiple_of(c * half, half), half), :]
            cp = rcopy(rows, rows, csend.at[idx], crecv.at[idx], sib)
            cp.start()
            return cp

        p1_in = phase_one(a_in, mine_in, recv_in, s1_in, r1_in, hin, 0)
        p1_out = phase_one(a_out, mine_out, recv_out, s1_out, r1_out, hout, 4)
        own_in, p2_in = phase_two(p1_in, mine_in, recv_in, r1_in, s2_in, r2_in, hin, 0)
        own_out, p2_out = phase_two(p1_out, mine_out, recv_out, r1_out, s2_out, r2_out, hout, 4)
        c_in = stage_c(own_in, p2_in, r2_in, gin_o, hin, 0)
        c_out = stage_c(own_out, p2_out, r2_out, gout_o, hout, 1)
        b_in, b_out = p1_in + p2_in, p1_out + p2_out

        for cp in s_sends:
            cp.wait_recv()
        tot = gather[0]
        for d in range(1, 8):
            tot = tot + gather[d]
        small_o[...] = tot

        for g_o, half, idx in ((gin_o, hin, 0), (gout_o, hout, 1)):
            other = g_o.at[pl.ds(pl.multiple_of((1 - c) * half, half), half), :]
            rcopy(other, other, csend.at[idx], crecv.at[idx], sib).wait_recv()
        for cp in s_sends + [a_in, a_out] + b_in + b_out + [c_in, c_out]:
            cp.wait_send()

    vm = pl.BlockSpec(memory_space=pltpu.VMEM)
    hbm = pl.BlockSpec(memory_space=pl.ANY)
    return pl.pallas_call(
        body, name="gsync",
        out_shape=(jax.ShapeDtypeStruct((2 * hin, 1024), F32), jax.ShapeDtypeStruct((2 * hout, 1024), F32),
                   jax.ShapeDtypeStruct((nsmall, 128), F32)),
        in_specs=[hbm, hbm, vm], out_specs=(vm, vm, vm),
        scratch_shapes=[pltpu.VMEM((4, hin, 1024), BF16), pltpu.VMEM((4, hin, 1024), BF16),
                        pltpu.VMEM((2, 2, hin // 2, 1024), BF16), pltpu.VMEM((2, 2, hin // 2, 1024), BF16),
                        pltpu.VMEM((2, hin // 2, 1024), BF16), pltpu.VMEM((2, hin // 2, 1024), BF16),
                        pltpu.VMEM((4, hout, 1024), BF16), pltpu.VMEM((4, hout, 1024), BF16),
                        pltpu.VMEM((2, 2, hout // 2, 1024), BF16), pltpu.VMEM((2, 2, hout // 2, 1024), BF16),
                        pltpu.VMEM((2, hout // 2, 1024), BF16), pltpu.VMEM((2, hout // 2, 1024), BF16),
                        pltpu.VMEM((8, nsmall, 128), F32),
                        pltpu.SemaphoreType.DMA((2,)),
                        pltpu.SemaphoreType.DMA((2,)), pltpu.SemaphoreType.DMA((2,)),
                        pltpu.SemaphoreType.DMA((8,)), pltpu.SemaphoreType.DMA((8,)),
                        pltpu.SemaphoreType.DMA((2,)), pltpu.SemaphoreType.DMA((2,)),
                        pltpu.SemaphoreType.DMA((7,)), pltpu.SemaphoreType.DMA((7,))],
        compiler_params=_params(),
    )(pw_in, pw_out, small)


def _adamw_math(w, g, m, v):
    m = ADAM_B1 * m + (1.0 - ADAM_B1) * g
    v = ADAM_B2 * v + (1.0 - ADAM_B2) * (g * g)
    m_hat = m / (1.0 - ADAM_B1 ** ADAM_STEP)
    v_hat = v / (1.0 - ADAM_B2 ** ADAM_STEP)
    delta = -ADAM_LR * (m_hat / (jnp.sqrt(v_hat) + ADAM_EPS) + ADAM_WD * w)
    return delta, m, v


def _adamw(w, g, m, v, name):
    rows, cols = w.shape
    tr = 256 if rows % 256 == 0 else rows

    def body(w_ref, g_ref, m_ref, v_ref, g_o, d_o, m_o, v_o):
        g = g_ref[...]
        d, m2, v2 = _adamw_math(w_ref[...], g, m_ref[...], v_ref[...])
        g_o[...] = g
        d_o[...] = d
        m_o[...] = m2
        v_o[...] = v2

    spec = pl.BlockSpec((tr, cols), lambda i: (i, 0))
    shp = jax.ShapeDtypeStruct((rows, cols), F32)
    return pl.pallas_call(
        body, name=name, grid=(rows // tr,), out_shape=(shp, shp, shp, shp),
        in_specs=[spec] * 4, out_specs=(spec, spec, spec, spec),
        compiler_params=_params(("parallel",)),
    )(w, g, m, v)


def _fold_heads(dqkw):
    def body(x_ref, o_ref):
        xs = x_ref[...]
        sq = xs[0:1] + xs[1:2] + xs[2:3] + xs[3:4]
        sk = xs[4:5] + xs[5:6] + xs[6:7] + xs[7:8]
        both = jnp.concatenate([sq, sk], axis=0)
        o_ref[...] = both + pltpu.roll(both, HEAD_DIM, 1)

    vm = pl.BlockSpec(memory_space=pltpu.VMEM)
    return pl.pallas_call(body, name="fold_heads", out_shape=jax.ShapeDtypeStruct((2, 128), F32),
                          in_specs=[vm], out_specs=vm, compiler_params=_params())(dqkw)


def kernel(x, norm_w, w_in, conv_w, conv_b, q_norm_w, k_norm_w, rel_bias, w_out, loss_target, m_norm_w, m_w_in, m_conv_w, m_conv_b, m_q_norm_w, m_k_norm_w, m_rel_bias, m_w_out, v_norm_w, v_w_in, v_conv_w, v_conv_b, v_q_norm_w, v_k_norm_w, v_rel_bias, v_w_out):
    x2 = x[0]
    tgt = loss_target[0]
    blk = 2 * lax.axis_index("x") + lax.axis_index("y")

    conv_w8 = jnp.pad(conv_w, ((0, 5), (0, 0)))
    wblk, woutblk, cwblk = _wgather(w_in, w_out, conv_w8)
    wout_full = woutblk.reshape(1024, 1024)
    cw_full = cwblk.transpose(1, 0, 2).reshape(8, 512)

    qkw = jnp.concatenate([jnp.tile(q_norm_w, 8) * 0.125, jnp.tile(k_norm_w, 8)])[None, :]
    qkw_raw = jnp.concatenate([jnp.tile(q_norm_w, 8), jnp.tile(k_norm_w, 8)])[None, :]
    gidx = jnp.arange(256) // HEAD_DIM
    b256 = (gidx[:, None] == gidx[None, :]).astype(BF16)

    h, cg, qkr, qkn, vz, qkn4, v4, qkn16, v16, xmt = _proj(x2, tgt, norm_w[None, :], wblk, qkw, b256)

    biases = _bias_tables(rel_bias)
    qkn_l = [qkn[None], qkn4, qkn16]
    v_l = [vz[None], v4, v16]
    o_g, lse_g = [], []
    for gi, d in enumerate(DILATIONS):
        o_l, lse_l = _attn_fwd(qkn_l[gi], v_l[gi], biases, gi, f"attn_fwd_d{d}")
        o_g.append(o_l)
        lse_g.append(lse_l)

    (y, dout, ld1, do1, dza, dgbz, dzc, loss_p, dcb, dcw, do4, ld4, do16, ld16) = _combine(
        o_g, lse_g, cg, vz, xmt, wout_full, cw_full, conv_b[None, :], b256)

    dq_g, dkv_g, dsums = [], [], []
    for gi, (d, do_l, ld_l) in enumerate(zip(DILATIONS, (do1, do4, do16), (ld1, ld4, ld16))):
        dq_l, dkv_l, dsum = _attn_bwd(qkn_l[gi], v_l[gi], do_l, ld_l, biases, gi, f"attn_bwd_d{d}")
        dq_g.append(dq_l)
        dkv_g.append(dkv_l)
        dsums.append(dsum)

    grad_x, dproj, dnw, dqkw = _bwd_tail(dq_g, dkv_g, qkr, qkw_raw, dza, dgbz, dzc, cg, cw_full, wblk,
                                         x2, norm_w[None, :], dout, b256)

    pw_in = _wgrad(h, dproj, False, "wgrad_in")
    pw_out = _wgrad(y, dout, True, "wgrad_out")
    dbias8 = _dbias(dsums, jnp.stack([_diag_bucket_onehot(d) for d in DILATIONS], axis=0))

    small = jnp.concatenate([dnw.reshape(8, 128), dcb.reshape(4, 128), dqkw.reshape(8, 128),
                             dcw[0:3].reshape(12, 128), dbias8, jnp.pad(loss_p, ((0, 7), (0, 0)))], axis=0)
    g_win, g_wout, gsmall = _gsync(pw_in, pw_out, small)

    g_nw = gsmall[0:8].reshape(1024)
    g_cb = gsmall[8:12].reshape(512)
    folded = _fold_heads(gsmall[12:20])
    g_qw, g_kw = folded[0, 0:64], folded[1, 0:64]
    g_cw = lax.dynamic_slice(gsmall[20:32].reshape(3, 512), (0, blk * 128), (3, 128))
    g_rb = gsmall[32:40][:, 0:32].T
    loss = gsmall[40, 0]

    g_win, d_win, nm_win, nv_win = _adamw(w_in, g_win, m_w_in, v_w_in, "adamw_w_in")
    g_wout, d_wout, nm_wout, nv_wout = _adamw(w_out, g_wout, m_w_out, v_w_out, "adamw_w_out")

    def pack(parts):
        rows = [parts[0].reshape(8, 128), parts[1].reshape(4, 128),
                jnp.pad(parts[2], (0, 64))[None, :], jnp.pad(parts[3], (0, 64))[None, :],
                parts[4], jnp.pad(parts[5].T, ((0, 0), (0, 96)))]
        return jnp.concatenate(rows, axis=0)

    ws = pack([norm_w, conv_b, q_norm_w, k_norm_w, conv_w, rel_bias])
    gs = pack([g_nw, g_cb, g_qw, g_kw, g_cw, g_rb])
    ms = pack([m_norm_w, m_conv_b, m_q_norm_w, m_k_norm_w, m_conv_w, m_rel_bias])
    vs = pack([v_norm_w, v_conv_b, v_q_norm_w, v_k_norm_w, v_conv_w, v_rel_bias])
    rpad = lambda a: jnp.pad(a, ((0, 7), (0, 0)))
    _, d_s, nm_s, nv_s = _adamw(rpad(ws), rpad(gs), rpad(ms), rpad(vs), "adamw_small")

    def unpack(a):
        return (a[0:8].reshape(1024), a[12:13, 0:64].reshape(64), a[13:14, 0:64].reshape(64),
                a[14:17], a[8:12].reshape(512), a[17:25, 0:32].T)

    def ordered(nw, win, cw, cb, qw, kw, rb, wout):
        return (nw, win, cw, cb, qw, kw, rb, wout)

    g_un = (g_nw, g_qw, g_kw, g_cw, g_cb, g_rb)
    outs = [loss, grad_x[None]]
    for un, win_v, wout_v in ((g_un, g_win, g_wout), (unpack(d_s), d_win, d_wout),
                              (unpack(nm_s), nm_win, nm_wout), (unpack(nv_s), nv_win, nv_wout)):
        nw, qw, kw, cw, cb, rb = un
        outs.extend(ordered(nw, win_v, cw, cb, qw, kw, rb, wout_v))
    return tuple(outs)
```
